```python
import math
import jax, jax.numpy as jnp
from jax import lax
import numpy as np

D_MODEL = 1024
BATCH = 8
SEQ = 8192
DEPTH = 1

HEAD_DIM = 64
DIL_GROUPS = ((128, 1), (512, 4), (2048, 16))
N_GROUPS = 3
HEADS_PER_GROUP = 4
N_ATTN_HEADS = N_GROUPS * HEADS_PER_GROUP
ATTN_W = N_ATTN_HEADS * HEAD_DIM
ATTN_OUT_W = HEADS_PER_GROUP * HEAD_DIM
BLK = 128
SGU_CHUNK = 128
SGU_GROUPS = 4
SGU_W = 512
SGU_GROUP_W = SGU_W // SGU_GROUPS
N_BRANCH = 2
IN_W = 3 * ATTN_W + 2 * SGU_W + N_BRANCH * D_MODEL
MEM_LEN = 256
MEM_HEADS = 4
MEM_HEAD_DIM = 128
MEM_W = MEM_HEADS * MEM_HEAD_DIM
D_FF = -(-8 * D_MODEL // (3 * 256)) * 256
EPS = 1e-6

kernel_name = "hybrid_dilated_attn_sgu_gated_block"


def rmsnorm(x, g):
    xf = x.astype(jnp.float32)
    r = lax.rsqrt(jnp.mean(xf * xf, axis=-1, keepdims=True) + EPS)
    return (xf * r * g.astype(jnp.float32)).astype(x.dtype)


def alibi_slopes_grouped():
    def pow2(n):
        start = 2.0 ** (-8.0 / n)
        return [start ** (i + 1) for i in range(n)]
    n = N_ATTN_HEADS
    if math.log2(n).is_integer():
        s = pow2(n)
    else:
        c = 2 ** int(math.floor(math.log2(n)))
        s = pow2(c) + pow2(2 * c)[0::2][: n - c]
    s = np.array(sorted(s, reverse=True), dtype=np.float32)
    return s.reshape(N_GROUPS, HEADS_PER_GROUP)


def dilated_causal_window_attention(q, k, v, slopes, window, dilation):
    B, S, H, Dh = q.shape
    n_back = window // dilation
    assert n_back <= BLK
    sub_len = -(-S // dilation)
    L = -(-sub_len // BLK) * BLK
    S_pad = L * dilation
    nb = L // BLK

    def to_blocks(t):
        t = jnp.pad(t, ((0, 0), (0, S_pad - S), (0, 0), (0, 0)))
        t = t.reshape(B, nb, BLK, dilation, H, Dh)
        return t.transpose(0, 3, 4, 1, 2, 5)

    def with_prev(t):
        prev = jnp.pad(t, ((0, 0), (0, 0), (0, 0), (1, 0), (0, 0), (0, 0)))[:, :, :, :-1]
        return jnp.concatenate([prev, t], axis=4)

    qb = to_blocks(q)
    kk = with_prev(to_blocks(k))
    vv = with_prev(to_blocks(v))

    s = jnp.einsum('brhnqd,brhnkd->brhnqk', qb, kk).astype(jnp.float32) * (Dh ** -0.5)
    steps = (np.arange(BLK)[:, None] + BLK) - np.arange(2 * BLK)[None, :]
    band = (steps >= 0) & (steps <= n_back)
    no_prev = (np.arange(nb)[:, None, None] == 0) & (np.arange(2 * BLK)[None, None, :] < BLK)
    valid = band[None] & ~no_prev
    dist = jnp.asarray((np.clip(steps, 0, None) * dilation).astype(np.float32))
    bias = -jnp.asarray(slopes)[:, None, None, None] * dist[None, None]
    s = jnp.where(jnp.asarray(valid), s + bias, -jnp.inf)
    mx = jnp.max(s, axis=-1, keepdims=True)
    e = jnp.exp(s - mx)
    den = jnp.sum(e, axis=-1, keepdims=True)
    o = jnp.einsum('brhnqk,brhnkd->brhnqd', e, vv.astype(jnp.float32)) / den
    lse = (mx + jnp.log(den))[..., 0]
    o = o.transpose(0, 3, 4, 1, 2, 5).reshape(B, S_pad, H, Dh)[:, :S]
    lse = lse.transpose(0, 3, 4, 1, 2).reshape(B, S_pad, H)[:, :S]
    return o, lse


def spatial_gating(uv, w_s, b_s, g):
    B, S, _ = uv.shape
    z = jax.nn.gelu(uv)
    u, v = jnp.split(z, 2, axis=-1)
    v = rmsnorm(v, g)
    v = v.reshape(B, S // SGU_CHUNK, SGU_CHUNK, SGU_GROUPS, SGU_GROUP_W)
    mixed = jnp.einsum('gts,bnsgc->bntgc', jnp.tril(w_s), v) + b_s.T[:, :, None]
    return u * mixed.reshape(B, S, SGU_W)


def memory_cross_attention(c, m, w_q, w_kv, w_o):
    B, S, _ = c.shape
    M = m.shape[1]
    q = (c @ w_q).reshape(B, S, MEM_HEADS, MEM_HEAD_DIM)
    kv = (m @ w_kv).reshape(B, M, 2, MEM_HEADS, MEM_HEAD_DIM)
    k, v = kv[:, :, 0], kv[:, :, 1]
    s = jnp.einsum('bshd,bmhd->bhsm', q, k).astype(jnp.float32) * (MEM_HEAD_DIM ** -0.5)
    p = jax.nn.softmax(s, axis=-1)
    o = jnp.einsum('bhsm,bmhd->bshd', p, v.astype(jnp.float32)).astype(c.dtype)
    return o.reshape(B, S, MEM_W) @ w_o


def _fwd_setup_inputs(seed: int = 0) -> dict:
    key = jax.random.key(seed)
    ks = jax.random.split(key, 24)
    f32 = jnp.float32

    def nrm(k, shape, scale):
        return jax.random.normal(k, shape, f32) * scale

    def gain(k, shape):
        return 1.0 + 0.02 * jax.random.normal(k, shape, f32)

    L = DEPTH
    return {
        "x": nrm(ks[0], (BATCH, SEQ, D_MODEL), 1.0),
        "mem": nrm(ks[1], (BATCH, MEM_LEN, D_MODEL), 1.0),
        "g_mix": gain(ks[2], (L, D_MODEL)),
        "w_in": nrm(ks[3], (L, D_MODEL, IN_W), D_MODEL ** -0.5),
        "b_gate": nrm(ks[4], (L, N_BRANCH * D_MODEL), 0.1),
        "w_sgu_spatial": nrm(ks[5], (L, SGU_GROUPS, SGU_CHUNK, SGU_CHUNK), 0.5 * SGU_CHUNK ** -0.5),
        "b_sgu_spatial": 1.0 + nrm(ks[6], (L, SGU_GROUPS, SGU_CHUNK), 0.1),
        "g_sgu": gain(ks[7], (L, SGU_W)),
        "w_branch_attn": nrm(ks[8], (L, ATTN_OUT_W, D_MODEL), ATTN_OUT_W ** -0.5),
        "w_branch_sgu": nrm(ks[9], (L, SGU_W, D_MODEL), SGU_W ** -0.5),
        "w_out": nrm(ks[10], (L, D_MODEL, D_MODEL), D_MODEL ** -0.5),
        "g_cross": gain(ks[11], (L, D_MODEL)),
        "g_mem": gain(ks[12], (L, D_MODEL)),
        "w_q_cross": nrm(ks[13], (L, D_MODEL, MEM_W), D_MODEL ** -0.5),
        "w_kv_cross": nrm(ks[14], (L, D_MODEL, 2 * MEM_W), D_MODEL ** -0.5),
        "w_o_cross": nrm(ks[15], (L, MEM_W, D_MODEL), MEM_W ** -0.5),
        "g_ffn": gain(ks[16], (L, D_MODEL)),
        "w_gate_up": nrm(ks[17], (L, D_MODEL, 2 * D_FF), D_MODEL ** -0.5),
        "w_down": nrm(ks[18], (L, D_FF, D_MODEL), D_FF ** -0.5),
        "g_final": gain(ks[19], (D_MODEL,)),
    }


def _fwd_reference(x, mem, g_mix, w_in, b_gate, w_sgu_spatial, b_sgu_spatial, g_sgu,
              w_branch_attn, w_branch_sgu, w_out, g_cross, g_mem, w_q_cross,
              w_kv_cross, w_o_cross, g_ffn, w_gate_up, w_down, g_final):
    B, S, D = x.shape
    slopes = alibi_slopes_grouped()
    h = x
    for l in range(DEPTH):
        a = rmsnorm(h, g_mix[l])
        proj = a @ w_in[l]
        q, k, v, uv, gl = jnp.split(
            proj, [ATTN_W, 2 * ATTN_W, 3 * ATTN_W, 3 * ATTN_W + 2 * SGU_W], axis=-1)
        q = q.reshape(B, S, N_GROUPS, HEADS_PER_GROUP, HEAD_DIM)
        k = k.reshape(B, S, N_GROUPS, HEADS_PER_GROUP, HEAD_DIM)
        v = v.reshape(B, S, N_GROUPS, HEADS_PER_GROUP, HEAD_DIM)
        outs, lses = [], []
        for gi, (win, dil) in enumerate(DIL_GROUPS):
            o, lse = dilated_causal_window_attention(
                q[:, :, gi], k[:, :, gi], v[:, :, gi], slopes[gi], win, dil)
            outs.append(o)
            lses.append(lse)
        outs = jnp.stack(outs)
        alpha = jax.nn.softmax(jnp.stack(lses), axis=0)
        y_attn = jnp.sum(alpha[..., None] * outs, axis=0).reshape(B, S, ATTN_OUT_W).astype(x.dtype)

        y_sgu = spatial_gating(uv, w_sgu_spatial[l], b_sgu_spatial[l], g_sgu[l])

        gates = jax.nn.sigmoid((gl + b_gate[l]).astype(jnp.float32)).astype(x.dtype)
        gates = gates.reshape(B, S, N_BRANCH, D)
        merged = (gates[:, :, 0] * (y_attn @ w_branch_attn[l])
                  + gates[:, :, 1] * (y_sgu @ w_branch_sgu[l]))
        h = h + merged @ w_out[l]

        c = rmsnorm(h, g_cross[l])
        m = rmsnorm(mem, g_mem[l])
        h = h + memory_cross_attention(c, m, w_q_cross[l], w_kv_cross[l], w_o_cross[l])

        f = rmsnorm(h, g_ffn[l])
        gt, up = jnp.split(f @ w_gate_up[l], 2, axis=-1)
        h = h + (jax.nn.silu(gt) * up) @ w_down[l]
    return rmsnorm(h, g_final)


import jax as _jax
import jax.numpy as _jnp

TWIN_FORMAT = 'train_step'
FWD_PARAMS = ['x', 'mem', 'g_mix', 'w_in', 'b_gate', 'w_sgu_spatial', 'b_sgu_spatial', 'g_sgu', 'w_branch_attn', 'w_branch_sgu', 'w_out', 'g_cross', 'g_mem', 'w_q_cross', 'w_kv_cross', 'w_o_cross', 'g_ffn', 'w_gate_up', 'w_down', 'g_final']
TWIN_WEIGHTS = ['g_mix', 'w_in', 'b_gate', 'w_sgu_spatial', 'b_sgu_spatial', 'g_sgu', 'w_branch_attn', 'w_branch_sgu', 'w_out', 'g_cross', 'g_mem', 'w_q_cross', 'w_kv_cross', 'w_o_cross', 'g_ffn', 'w_gate_up', 'w_down', 'g_final']
TWIN_DIFF_INPUT = 'x'
TWIN_INPUTS = ['x', 'mem', 'g_mix', 'w_in', 'b_gate', 'w_sgu_spatial', 'b_sgu_spatial', 'g_sgu', 'w_branch_attn', 'w_branch_sgu', 'w_out', 'g_cross', 'g_mem', 'w_q_cross', 'w_kv_cross', 'w_o_cross', 'g_ffn', 'w_gate_up', 'w_down', 'g_final', 'loss_target', 'm_g_mix', 'm_w_in', 'm_b_gate', 'm_w_sgu_spatial', 'm_b_sgu_spatial', 'm_g_sgu', 'm_w_branch_attn', 'm_w_branch_sgu', 'm_w_out', 'm_g_cross', 'm_g_mem', 'm_w_q_cross', 'm_w_kv_cross', 'm_w_o_cross', 'm_g_ffn', 'm_w_gate_up', 'm_w_down', 'm_g_final', 'v_g_mix', 'v_w_in', 'v_b_gate', 'v_w_sgu_spatial', 'v_b_sgu_spatial', 'v_g_sgu', 'v_w_branch_attn', 'v_w_branch_sgu', 'v_w_out', 'v_g_cross', 'v_g_mem', 'v_w_q_cross', 'v_w_kv_cross', 'v_w_o_cross', 'v_g_ffn', 'v_w_gate_up', 'v_w_down', 'v_g_final']
TWIN_OUTPUTS = ['loss', 'grad_x', 'grad_g_mix', 'grad_w_in', 'grad_b_gate', 'grad_w_sgu_spatial', 'grad_b_sgu_spatial', 'grad_g_sgu', 'grad_w_branch_attn', 'grad_w_branch_sgu', 'grad_w_out', 'grad_g_cross', 'grad_g_mem', 'grad_w_q_cross', 'grad_w_kv_cross', 'grad_w_o_cross', 'grad_g_ffn', 'grad_w_gate_up', 'grad_w_down', 'grad_g_final', 'delta_g_mix', 'delta_w_in', 'delta_b_gate', 'delta_w_sgu_spatial', 'delta_b_sgu_spatial', 'delta_g_sgu', 'delta_w_branch_attn', 'delta_w_branch_sgu', 'delta_w_out', 'delta_g_cross', 'delta_g_mem', 'delta_w_q_cross', 'delta_w_kv_cross', 'delta_w_o_cross', 'delta_g_ffn', 'delta_w_gate_up', 'delta_w_down', 'delta_g_final', 'new_m_g_mix', 'new_m_w_in', 'new_m_b_gate', 'new_m_w_sgu_spatial', 'new_m_b_sgu_spatial', 'new_m_g_sgu', 'new_m_w_branch_attn', 'new_m_w_branch_sgu', 'new_m_w_out', 'new_m_g_cross', 'new_m_g_mem', 'new_m_w_q_cross', 'new_m_w_kv_cross', 'new_m_w_o_cross', 'new_m_g_ffn', 'new_m_w_gate_up', 'new_m_w_down', 'new_m_g_final', 'new_v_g_mix', 'new_v_w_in', 'new_v_b_gate', 'new_v_w_sgu_spatial', 'new_v_b_sgu_spatial', 'new_v_g_sgu', 'new_v_w_branch_attn', 'new_v_w_branch_sgu', 'new_v_w_out', 'new_v_g_cross', 'new_v_g_mem', 'new_v_w_q_cross', 'new_v_w_kv_cross', 'new_v_w_o_cross', 'new_v_g_ffn', 'new_v_w_gate_up', 'new_v_w_down', 'new_v_g_final']
TWIN_LEAF_KINDS = {'loss': 'loss', 'grad_x': 'grad_x', 'grad_g_mix': 'grad_w', 'grad_w_in': 'grad_w', 'grad_b_gate': 'grad_w', 'grad_w_sgu_spatial': 'grad_w', 'grad_b_sgu_spatial': 'grad_w', 'grad_g_sgu': 'grad_w', 'grad_w_branch_attn': 'grad_w', 'grad_w_branch_sgu': 'grad_w', 'grad_w_out': 'grad_w', 'grad_g_cross': 'grad_w', 'grad_g_mem': 'grad_w', 'grad_w_q_cross': 'grad_w', 'grad_w_kv_cross': 'grad_w', 'grad_w_o_cross': 'grad_w', 'grad_g_ffn': 'grad_w', 'grad_w_gate_up': 'grad_w', 'grad_w_down': 'grad_w', 'grad_g_final': 'grad_w', 'delta_g_mix': 'delta_w', 'delta_w_in': 'delta_w', 'delta_b_gate': 'delta_w', 'delta_w_sgu_spatial': 'delta_w', 'delta_b_sgu_spatial': 'delta_w', 'delta_g_sgu': 'delta_w', 'delta_w_branch_attn': 'delta_w', 'delta_w_branch_sgu': 'delta_w', 'delta_w_out': 'delta_w', 'delta_g_cross': 'delta_w', 'delta_g_mem': 'delta_w', 'delta_w_q_cross': 'delta_w', 'delta_w_kv_cross': 'delta_w', 'delta_w_o_cross': 'delta_w', 'delta_g_ffn': 'delta_w', 'delta_w_gate_up': 'delta_w', 'delta_w_down': 'delta_w', 'delta_g_final': 'delta_w', 'new_m_g_mix': 'new_m', 'new_m_w_in': 'new_m', 'new_m_b_gate': 'new_m', 'new_m_w_sgu_spatial': 'new_m', 'new_m_b_sgu_spatial': 'new_m', 'new_m_g_sgu': 'new_m', 'new_m_w_branch_attn': 'new_m', 'new_m_w_branch_sgu': 'new_m', 'new_m_w_out': 'new_m', 'new_m_g_cross': 'new_m', 'new_m_g_mem': 'new_m', 'new_m_w_q_cross': 'new_m', 'new_m_w_kv_cross': 'new_m', 'new_m_w_o_cross': 'new_m', 'new_m_g_ffn': 'new_m', 'new_m_w_gate_up': 'new_m', 'new_m_w_down': 'new_m', 'new_m_g_final': 'new_m', 'new_v_g_mix': 'new_v', 'new_v_w_in': 'new_v', 'new_v_b_gate': 'new_v', 'new_v_w_sgu_spatial': 'new_v', 'new_v_b_sgu_spatial': 'new_v', 'new_v_g_sgu': 'new_v', 'new_v_w_branch_attn': 'new_v', 'new_v_w_branch_sgu': 'new_v', 'new_v_w_out': 'new_v', 'new_v_g_cross': 'new_v', 'new_v_g_mem': 'new_v', 'new_v_w_q_cross': 'new_v', 'new_v_w_kv_cross': 'new_v', 'new_v_w_o_cross': 'new_v', 'new_v_g_ffn': 'new_v', 'new_v_w_gate_up': 'new_v', 'new_v_w_down': 'new_v', 'new_v_g_final': 'new_v'}


def _forward(args):
    return _fwd_reference(*[args[k] for k in FWD_PARAMS])


def _output_shape():
    def fwd():
        inp = _fwd_setup_inputs(0)
        return _fwd_reference(*[inp[k] for k in FWD_PARAMS])
    out = _jax.eval_shape(fwd)
    return out.shape, out.dtype

N_MICROBATCH = 1
ADAM_LR = 0.001
ADAM_B1 = 0.9
ADAM_B2 = 0.999
ADAM_EPS = 1e-08
ADAM_WD = 0.01
ADAM_STEP = 10
PER_EXAMPLE_BATCH_AXIS = {'x': 0, 'mem': 0, 'loss_target': 0}
SHARED_INPUTS = []
_WEIGHT_DTYPES = {'g_mix': _jnp.float32, 'w_in': _jnp.float32, 'b_gate': _jnp.float32, 'w_sgu_spatial': _jnp.float32, 'b_sgu_spatial': _jnp.float32, 'g_sgu': _jnp.float32, 'w_branch_attn': _jnp.float32, 'w_branch_sgu': _jnp.float32, 'w_out': _jnp.float32, 'g_cross': _jnp.float32, 'g_mem': _jnp.float32, 'w_q_cross': _jnp.float32, 'w_kv_cross': _jnp.float32, 'w_o_cross': _jnp.float32, 'g_ffn': _jnp.float32, 'w_gate_up': _jnp.float32, 'w_down': _jnp.float32, 'g_final': _jnp.float32}
MOMENT_SCALE = {'g_mix': 1.418917e-01, 'w_in': 6.019157e-02, 'b_gate': 2.972508e-02, 'w_sgu_spatial': 9.110712e-02, 'b_sgu_spatial': 1.277873e-01, 'g_sgu': 4.596154e-02, 'w_branch_attn': 5.229790e-02, 'w_branch_sgu': 9.976563e-02, 'w_out': 1.093383e-01, 'g_cross': 2.765343e-02, 'g_mem': 4.125459e-02, 'w_q_cross': 3.864680e-02, 'w_kv_cross': 3.870021e-02, 'w_o_cross': 2.739097e-02, 'g_ffn': 2.006041e-01, 'w_gate_up': 7.403814e-02, 'w_down': 1.211899e-01, 'g_final': 6.397787e+01}


def _to_microbatches(a, axis):
    t = _jnp.moveaxis(a, axis, 0)
    t = t.reshape((N_MICROBATCH, t.shape[0] // N_MICROBATCH) + t.shape[1:])
    return _jnp.moveaxis(t, 1, axis + 1)


def setup_inputs(seed: int = 0) -> dict:
    inp = _fwd_setup_inputs(seed)
    key = _jax.random.fold_in(_jax.random.key(seed), 7919)
    shape, _ = _output_shape()
    out = dict(inp)
    out["loss_target"] = _jax.random.normal(_jax.random.fold_in(key, 0), shape, _jnp.float32)
    for i, name in enumerate(TWIN_WEIGHTS):
        w = inp[name].astype(_jnp.float32)
        if MOMENT_SCALE is None:
            s = _jnp.sqrt(_jnp.mean(_jnp.square(w)) + 1e-30)
        else:
            s = MOMENT_SCALE[name]
        km, kv = _jax.random.split(_jax.random.fold_in(key, i + 1))
        out[name] = w
        out["m_" + name] = s * _jax.random.normal(km, w.shape, _jnp.float32)
        out["v_" + name] = (s * s) * _jax.random.uniform(kv, w.shape, _jnp.float32, 0.5, 1.5)
    if N_MICROBATCH > 1:
        for name, axis in PER_EXAMPLE_BATCH_AXIS.items():
            out[name] = _to_microbatches(out[name], axis)
    return {'x': out['x'], 'mem': out['mem'], 'g_mix': out['g_mix'], 'w_in': out['w_in'], 'b_gate': out['b_gate'], 'w_sgu_spatial': out['w_sgu_spatial'], 'b_sgu_spatial': out['b_sgu_spatial'], 'g_sgu': out['g_sgu'], 'w_branch_attn': out['w_branch_attn'], 'w_branch_sgu': out['w_branch_sgu'], 'w_out': out['w_out'], 'g_cross': out['g_cross'], 'g_mem': out['g_mem'], 'w_q_cross': out['w_q_cross'], 'w_kv_cross': out['w_kv_cross'], 'w_o_cross': out['w_o_cross'], 'g_ffn': out['g_ffn'], 'w_gate_up': out['w_gate_up'], 'w_down': out['w_down'], 'g_final': out['g_final'], 'loss_target': out['loss_target'], 'm_g_mix': out['m_g_mix'], 'm_w_in': out['m_w_in'], 'm_b_gate': out['m_b_gate'], 'm_w_sgu_spatial': out['m_w_sgu_spatial'], 'm_b_sgu_spatial': out['m_b_sgu_spatial'], 'm_g_sgu': out['m_g_sgu'], 'm_w_branch_attn': out['m_w_branch_attn'], 'm_w_branch_sgu': out['m_w_branch_sgu'], 'm_w_out': out['m_w_out'], 'm_g_cross': out['m_g_cross'], 'm_g_mem': out['m_g_mem'], 'm_w_q_cross': out['m_w_q_cross'], 'm_w_kv_cross': out['m_w_kv_cross'], 'm_w_o_cross': out['m_w_o_cross'], 'm_g_ffn': out['m_g_ffn'], 'm_w_gate_up': out['m_w_gate_up'], 'm_w_down': out['m_w_down'], 'm_g_final': out['m_g_final'], 'v_g_mix': out['v_g_mix'], 'v_w_in': out['v_w_in'], 'v_b_gate': out['v_b_gate'], 'v_w_sgu_spatial': out['v_w_sgu_spatial'], 'v_b_sgu_spatial': out['v_b_sgu_spatial'], 'v_g_sgu': out['v_g_sgu'], 'v_w_branch_attn': out['v_w_branch_attn'], 'v_w_branch_sgu': out['v_w_branch_sgu'], 'v_w_out': out['v_w_out'], 'v_g_cross': out['v_g_cross'], 'v_g_mem': out['v_g_mem'], 'v_w_q_cross': out['v_w_q_cross'], 'v_w_kv_cross': out['v_w_kv_cross'], 'v_w_o_cross': out['v_w_o_cross'], 'v_g_ffn': out['v_g_ffn'], 'v_w_gate_up': out['v_w_gate_up'], 'v_w_down': out['v_w_down'], 'v_g_final': out['v_g_final']}


def _loss(weights, diff, rest, loss_target):
    with _jax.named_scope("forward"):
        args = {**rest, TWIN_DIFF_INPUT: diff, **{k: w.astype(_WEIGHT_DTYPES[k]) for k, w in weights.items()}}
        y = _forward(args)
    with _jax.named_scope("loss_head"):
        err = _jnp.square(y.astype(_jnp.float32) - loss_target)
        return 0.5 * _jnp.sum(_jnp.mean(err, axis=-1)) if err.ndim else 0.5 * err


def _adamw(w, g, m, v):
    m = ADAM_B1 * m + (1.0 - ADAM_B1) * g
    v = ADAM_B2 * v + (1.0 - ADAM_B2) * _jnp.square(g)
    m_hat = m / (1.0 - ADAM_B1 ** ADAM_STEP)
    v_hat = v / (1.0 - ADAM_B2 ** ADAM_STEP)
    delta = -ADAM_LR * (m_hat / (_jnp.sqrt(v_hat) + ADAM_EPS) + ADAM_WD * w)
    return delta, m, v


def reference(x, mem, g_mix, w_in, b_gate, w_sgu_spatial, b_sgu_spatial, g_sgu, w_branch_attn, w_branch_sgu, w_out, g_cross, g_mem, w_q_cross, w_kv_cross, w_o_cross, g_ffn, w_gate_up, w_down, g_final, loss_target, m_g_mix, m_w_in, m_b_gate, m_w_sgu_spatial, m_b_sgu_spatial, m_g_sgu, m_w_branch_attn, m_w_branch_sgu, m_w_out, m_g_cross, m_g_mem, m_w_q_cross, m_w_kv_cross, m_w_o_cross, m_g_ffn, m_w_gate_up, m_w_down, m_g_final, v_g_mix, v_w_in, v_b_gate, v_w_sgu_spatial, v_b_sgu_spatial, v_g_sgu, v_w_branch_attn, v_w_branch_sgu, v_w_out, v_g_cross, v_g_mem, v_w_q_cross, v_w_kv_cross, v_w_o_cross, v_g_ffn, v_w_gate_up, v_w_down, v_g_final):
    given = dict(x=x, mem=mem, g_mix=g_mix, w_in=w_in, b_gate=b_gate, w_sgu_spatial=w_sgu_spatial, b_sgu_spatial=b_sgu_spatial, g_sgu=g_sgu, w_branch_attn=w_branch_attn, w_branch_sgu=w_branch_sgu, w_out=w_out, g_cross=g_cross, g_mem=g_mem, w_q_cross=w_q_cross, w_kv_cross=w_kv_cross, w_o_cross=w_o_cross, g_ffn=g_ffn, w_gate_up=w_gate_up, w_down=w_down, g_final=g_final, loss_target=loss_target, m_g_mix=m_g_mix, m_w_in=m_w_in, m_b_gate=m_b_gate, m_w_sgu_spatial=m_w_sgu_spatial, m_b_sgu_spatial=m_b_sgu_spatial, m_g_sgu=m_g_sgu, m_w_branch_attn=m_w_branch_attn, m_w_branch_sgu=m_w_branch_sgu, m_w_out=m_w_out, m_g_cross=m_g_cross, m_g_mem=m_g_mem, m_w_q_cross=m_w_q_cross, m_w_kv_cross=m_w_kv_cross, m_w_o_cross=m_w_o_cross, m_g_ffn=m_g_ffn, m_w_gate_up=m_w_gate_up, m_w_down=m_w_down, m_g_final=m_g_final, v_g_mix=v_g_mix, v_w_in=v_w_in, v_b_gate=v_b_gate, v_w_sgu_spatial=v_w_sgu_spatial, v_b_sgu_spatial=v_b_sgu_spatial, v_g_sgu=v_g_sgu, v_w_branch_attn=v_w_branch_attn, v_w_branch_sgu=v_w_branch_sgu, v_w_out=v_w_out, v_g_cross=v_g_cross, v_g_mem=v_g_mem, v_w_q_cross=v_w_q_cross, v_w_kv_cross=v_w_kv_cross, v_w_o_cross=v_w_o_cross, v_g_ffn=v_g_ffn, v_w_gate_up=v_w_gate_up, v_w_down=v_w_down, v_g_final=v_g_final)
    weights = {n: given[n] for n in TWIN_WEIGHTS}
    shared = {n: given[n] for n in SHARED_INPUTS}
    per_example = {n: given[n] for n in ['x', 'mem']}
    grad_fn = _jax.value_and_grad(_loss, argnums=(0, 1))

    def one_microbatch(ex, loss_target):
        ex = dict(ex)
        diff = ex.pop(TWIN_DIFF_INPUT)
        return grad_fn(weights, diff, {**shared, **ex}, loss_target)

    if N_MICROBATCH == 1:
        loss, (grad_w, grad_x) = one_microbatch(per_example, given["loss_target"])
    else:
        def body(carry, xs):
            loss_sum, grad_sum = carry
            l_k, (gw_k, gx_k) = one_microbatch(xs[0], xs[1])
            with _jax.named_scope("update"):
                return (loss_sum + l_k, _jax.tree.map(_jnp.add, grad_sum, gw_k)), gx_k

        init = (_jnp.zeros((), _jnp.float32), _jax.tree.map(_jnp.zeros_like, weights))
        (loss, grad_w), grad_x = _jax.lax.scan(body, init, (per_example, given["loss_target"]))
    with _jax.named_scope("update"):
        delta_w, new_m, new_v = {}, {}, {}
        for n in TWIN_WEIGHTS:
            delta_w[n], new_m[n], new_v[n] = _adamw(weights[n], grad_w[n], given["m_" + n], given["v_" + n])
    return (loss, grad_x, *[grad_w[n] for n in TWIN_WEIGHTS], *[delta_w[n] for n in TWIN_WEIGHTS],
            *[new_m[n] for n in TWIN_WEIGHTS], *[new_v[n] for n in TWIN_WEIGHTS])
```

```python
import math

import numpy as np
import jax
import jax.numpy as jnp
from jax import lax
from jax.experimental import pallas as pl
from jax.experimental.pallas import tpu as pltpu

F32, BF16 = jnp.float32, jnp.bfloat16
MESH = pl.DeviceIdType.MESH
ANY = pl.BlockSpec(memory_space=pl.ANY)
RES = pl.BlockSpec(memory_space=pltpu.VMEM)

D = 1024
HEAD = 64
GROUP_W = 256
DIL_GROUPS = ((128, 1), (512, 4), (2048, 16))
BLK = 128
SGU_W = 512
MEM_HEADS, MEM_HD, MEM_W = 4, 128, 512
D_FF = 2816
FF_CHUNK = 256
EPS = 1e-6
NEG = -1e30
LR, B1, B2, AEPS, WD, STEP = 0.001, 0.9, 0.999, 1e-08, 0.01, 10
GELU_K, GELU_C = 0.7978845608028654, 0.044715


def _dot(a, b):
    return jnp.dot(a, b, preferred_element_type=F32)


def _dot_nt(a, b):
    return lax.dot_general(a, b, (((1,), (1,)), ((), ())), preferred_element_type=F32)


def _dot_tn(a, b):
    return lax.dot_general(a, b, (((0,), (0,)), ((), ())), preferred_element_type=F32)


def _row(tm, w):
    return pl.BlockSpec((tm, w), lambda i: (i, 0))


def _acc(shape):
    return pl.BlockSpec(shape, lambda i: (0,) * len(shape))


def _params(sem, mb):
    return pltpu.CompilerParams(dimension_semantics=sem, vmem_limit_bytes=mb << 20)


def _sds(shape, dt):
    return jax.ShapeDtypeStruct(shape, dt)


def _rms(h):
    return lax.rsqrt(jnp.mean(h * h, axis=-1, keepdims=True) + EPS)


def _rms_bwd(dy, h, r, g):
    t = dy * g
    dh = r * t - h * (r * r * r) * jnp.mean(t * h, axis=-1, keepdims=True)
    return dh, dy * h * r


def _gelu(x):
    t = jnp.tanh(GELU_K * (x + GELU_C * x * x * x))
    return 0.5 * x * (1.0 + t), t


def _gelu_grad(x, t):
    return 0.5 * (1.0 + t) + 0.5 * x * (1.0 - t * t) * GELU_K * (1.0 + 3.0 * GELU_C * x * x)


def _alibi_slopes():
    def pow2(n):
        start = 2.0 ** (-8.0 / n)
        return [start ** (i + 1) for i in range(n)]
    n = 12
    c = 2 ** int(math.floor(math.log2(n)))
    s = pow2(c) + pow2(2 * c)[0::2][: n - c]
    return np.array(sorted(s, reverse=True), dtype=np.float32).reshape(3, 4)


def _attn_bias(g):
    win, dil = DIL_GROUPS[g]
    steps = (np.arange(BLK)[:, None] + BLK) - np.arange(2 * BLK)[None, :]
    valid = (steps >= 0) & (steps <= win // dil)
    dist = (np.clip(steps, 0, None) * dil).astype(np.float32)
    b = -_alibi_slopes()[g][:, None, None] * dist[None]
    return np.where(valid[None], b, NEG).astype(np.float32)


def _head_masks():
    lane = lax.broadcasted_iota(jnp.int32, (1, GROUP_W), 1)
    return lane, [(lane >= HEAD * h) & (lane < HEAD * (h + 1)) for h in range(4)]


def _fwd_in(x, g_mix, wt_in, tm=512):
    S = x.shape[0]

    def body(x_ref, g_ref, w_ref, a_ref, qkv_ref, uv_ref, gl_ref):
        xv = x_ref[...]
        a = (xv * _rms(xv) * g_ref[...]).astype(BF16)
        a_ref[...] = a
        for j in range(9):
            qkv_ref[:, j * 256:(j + 1) * 256] = _dot_nt(a, w_ref[j * 256:(j + 1) * 256, :]).astype(BF16)
        uv_ref[...] = _dot_nt(a, w_ref[2304:3328, :]).astype(BF16)
        gl_ref[...] = _dot_nt(a, w_ref[3328:5376, :]).astype(BF16)

    return pl.pallas_call(
        body, grid=(S // tm,), name="fwd_in",
        in_specs=[_row(tm, D), RES, RES],
        out_specs=[_row(tm, D), _row(tm, 2304), _row(tm, 1024), _row(tm, 2048)],
        out_shape=[_sds((S, D), BF16), _sds((S, 2304), BF16), _sds((S, 1024), BF16), _sds((S, 2048), BF16)],
        compiler_params=_params(("parallel",), 56),
    )(x, g_mix, wt_in)


def _attn_fwd(qkv, g):
    S = qkv.shape[0]
    d = DIL_GROUPS[g][1]
    L = S // d
    nb = L // BLK
    qv = qkv.reshape(L, d * 2304)
    bias = jnp.asarray(_attn_bias(g))

    def body(q_ref, kc_ref, kp_ref, vc_ref, vp_ref, b_ref, o_ref, l_ref):
        n = pl.program_id(1)
        q = q_ref[...]
        kk = jnp.concatenate([kp_ref[...], kc_ref[...]], axis=0)
        vv = jnp.concatenate([vp_ref[...], vc_ref[...]], axis=0)
        lane, masks = _head_masks()
        extra = jnp.where((n == 0) & (lane < BLK), NEG, 0.0).astype(F32)
        o_acc = jnp.zeros((BLK, GROUP_W), F32)
        l_acc = jnp.zeros((BLK, GROUP_W), F32)
        for h in range(4):
            qh = jnp.where(masks[h], q, jnp.zeros_like(q))
            s = _dot_nt(qh, kk) * 0.125 + b_ref[h] + extra
            mx = jnp.max(s, axis=-1, keepdims=True)
            e = jnp.exp(s - mx)
            den = jnp.sum(e, axis=-1, keepdims=True)
            oh = _dot(e.astype(BF16), vv) / den
            o_acc = jnp.where(masks[h], oh, o_acc)
            l_acc = jnp.where(masks[h], mx + jnp.log(den), l_acc)
        o_ref[...] = o_acc
        l_ref[...] = l_acc

    def blk(f):
        return pl.BlockSpec((BLK, GROUP_W), f)

    o, lse = pl.pallas_call(
        body, grid=(d, nb), name=f"attn_fwd_g{g}",
        in_specs=[blk(lambda r, n: (n, 9 * r + g)),
                  blk(lambda r, n: (n, 9 * r + 3 + g)), blk(lambda r, n: (jnp.maximum(n - 1, 0), 9 * r + 3 + g)),
                  blk(lambda r, n: (n, 9 * r + 6 + g)), blk(lambda r, n: (jnp.maximum(n - 1, 0), 9 * r + 6 + g)),
                  pl.BlockSpec((4, BLK, 2 * BLK), lambda r, n: (0, 0, 0))],
        out_specs=[blk(lambda r, n: (n, r)), blk(lambda r, n: (n, r))],
        out_shape=[_sds((L, d * GROUP_W), F32), _sds((L, d * GROUP_W), F32)],
        compiler_params=_params(("parallel", "parallel"), 32),
    )(qv, qv, qv, qv, qv, bias)
    return o.reshape(S, GROUP_W), lse.reshape(S, GROUP_W)


def _group_weights(l0, l1, l2):
    m = jnp.maximum(jnp.maximum(l0, l1), l2)
    e0, e1, e2 = jnp.exp(l0 - m), jnp.exp(l1 - m), jnp.exp(l2 - m)
    inv = 1.0 / (e0 + e1 + e2)
    return e0 * inv, e1 * inv, e2 * inv


def _sgu_forward(uvf, gs, wt_ref, bst_ref, mixed_s, tm):
    z, t = _gelu(uvf)
    u, v = z[:, :SGU_W], z[:, SGU_W:]
    rv = _rms(v)
    vnb = (v * rv * gs).astype(BF16)
    for ci in range(tm // 128):
        for g in range(4):
            rs, cs = slice(ci * 128, (ci + 1) * 128), slice(g * 128, (g + 1) * 128)
            mixed_s[rs, cs] = _dot(wt_ref[g], vnb[rs, cs]) + bst_ref[:, g:g + 1]
    return u, v, rv, vnb, t


def _fwd_mid(x, os_, ls_, uv, gl, wt, bst, g_sgu, b_gate, wt_ba, wt_bs, w_out, tm=256):
    S = x.shape[0]

    def body(x_ref, o0, o1, o2, l0, l1, l2, uv_ref, gl_ref, wt_ref, bst_ref, gs_ref, bg_ref, wba_ref, wbs_ref, wo_ref,
             ya_ref, ys_ref, ba_ref, bs_ref, mg_ref, h1_ref, mixed_s):
        a0, a1, a2 = _group_weights(l0[...], l1[...], l2[...])
        yab = (a0 * o0[...] + a1 * o1[...] + a2 * o2[...]).astype(BF16)
        ya_ref[...] = yab
        u, _, _, _, _ = _sgu_forward(uv_ref[...].astype(F32), gs_ref[...], wt_ref, bst_ref, mixed_s, tm)
        ysb = (u * mixed_s[...]).astype(BF16)
        ys_ref[...] = ysb
        gates = jax.nn.sigmoid(gl_ref[...].astype(F32) + bg_ref[...])
        ba = _dot_nt(yab, wba_ref[...])
        bs = _dot_nt(ysb, wbs_ref[...])
        ba_ref[...] = ba.astype(BF16)
        bs_ref[...] = bs.astype(BF16)
        mgb = (gates[:, :D] * ba + gates[:, D:] * bs).astype(BF16)
        mg_ref[...] = mgb
        h1_ref[...] = x_ref[...] + _dot(mgb, wo_ref[...])

    gw = _row(tm, GROUP_W)
    return pl.pallas_call(
        body, grid=(S // tm,), name="fwd_mid",
        in_specs=[_row(tm, D), gw, gw, gw, gw, gw, gw, _row(tm, 1024), _row(tm, 2048)] + [RES] * 7,
        out_specs=[gw, _row(tm, SGU_W), _row(tm, D), _row(tm, D), _row(tm, D), _row(tm, D)],
        out_shape=[_sds((S, GROUP_W), BF16), _sds((S, SGU_W), BF16), _sds((S, D), BF16), _sds((S, D), BF16),
                   _sds((S, D), BF16), _sds((S, D), F32)],
        scratch_shapes=[pltpu.VMEM((tm, SGU_W), F32)],
        compiler_params=_params(("parallel",), 48),
    )(x, *os_, *ls_, uv, gl, wt, bst, g_sgu, b_gate, wt_ba, wt_bs, w_out)


def _mem_fwd(mem, g_mem, w_kv):
    def body(m_ref, g_ref, w_ref, mb_ref, kv_ref):
        mv = m_ref[...]
        mb = (mv * _rms(mv) * g_ref[...]).astype(BF16)
        mb_ref[...] = mb
        kv_ref[...] = _dot(mb, w_ref[...]).astype(BF16)

    return pl.pallas_call(
        body, name="mem_fwd", in_specs=[RES, RES, RES], out_specs=[RES, RES],
        out_shape=[_sds(mem.shape, BF16), _sds((mem.shape[0], 2 * MEM_W), BF16)],
        compiler_params=pltpu.CompilerParams(vmem_limit_bytes=32 << 20),
    )(mem, g_mem, w_kv)


def _cross_probs(qh, kh):
    s = _dot_nt(qh, kh) * (MEM_HD ** -0.5)
    e = jnp.exp(s - jnp.max(s, axis=-1, keepdims=True))
    return e / jnp.sum(e, axis=-1, keepdims=True)


def _fwd_cross(h1, g_cross, w_q, kv, wt_o, tm=512):
    S = h1.shape[0]

    def body(h_ref, g_ref, wq_ref, kv_ref, wo_ref, c_ref, qc_ref, oc_ref, h2_ref):
        hv = h_ref[...]
        cb = (hv * _rms(hv) * g_ref[...]).astype(BF16)
        c_ref[...] = cb
        qcb = _dot(cb, wq_ref[...]).astype(BF16)
        qc_ref[...] = qcb
        for h in range(MEM_HEADS):
            cs = slice(h * MEM_HD, (h + 1) * MEM_HD)
            p = _cross_probs(qcb[:, cs], kv_ref[:, cs])
            oc_ref[:, cs] = _dot(p.astype(BF16), kv_ref[:, MEM_W + h * MEM_HD:MEM_W + (h + 1) * MEM_HD]).astype(BF16)
        h2_ref[...] = hv + _dot_nt(oc_ref[...], wo_ref[...])

    return pl.pallas_call(
        body, grid=(S // tm,), name="fwd_cross",
        in_specs=[_row(tm, D), RES, RES, RES, RES],
        out_specs=[_row(tm, D), _row(tm, MEM_W), _row(tm, MEM_W), _row(tm, D)],
        out_shape=[_sds((S, D), BF16), _sds((S, MEM_W), BF16), _sds((S, MEM_W), BF16), _sds((S, D), F32)],
        compiler_params=_params(("parallel",), 40),
    )(h1, g_cross, w_q, kv, wt_o)


def _ffn_fwd_bwd(h2, target, g_ffn, g_final, wt_gu, w_down, tm=256):
    S = h2.shape[0]
    nch = D_FF // FF_CHUNK

    def body(h_ref, t_ref, gf_ref, gz_ref, wgu_ref, wd_ref,
             f_ref, act_ref, dgu_ref, dh3b_ref, dh2_ref, dh2b_ref, dgf_ref, dgz_ref, loss_ref, gu_s):
        i = pl.program_id(0)
        hv = h_ref[...]
        r2 = _rms(hv)
        gf = gf_ref[...]
        fb = (hv * r2 * gf).astype(BF16)
        f_ref[...] = fb
        h3 = hv
        for c in range(nch):
            cs = slice(c * FF_CHUNK, (c + 1) * FF_CHUNK)
            us = slice(D_FF + c * FF_CHUNK, D_FF + (c + 1) * FF_CHUNK)
            gt = _dot_nt(fb, wgu_ref[cs, :])
            up = _dot_nt(fb, wgu_ref[us, :])
            gu_s[:, cs] = gt
            gu_s[:, us] = up
            actb = (gt * jax.nn.sigmoid(gt) * up).astype(BF16)
            act_ref[:, cs] = actb
            h3 = h3 + _dot(actb, wd_ref[cs, :])
        r3 = _rms(h3)
        gz = gz_ref[...]
        diff = h3 * r3 * gz - t_ref[...]
        dy = diff * (1.0 / D)
        dh3, dgz_rows = _rms_bwd(dy, h3, r3, gz)
        dh3b = dh3.astype(BF16)
        dh3b_ref[...] = dh3b
        df = jnp.zeros((tm, D), F32)
        for c in range(nch):
            cs = slice(c * FF_CHUNK, (c + 1) * FF_CHUNK)
            us = slice(D_FF + c * FF_CHUNK, D_FF + (c + 1) * FF_CHUNK)
            dact = _dot_nt(dh3b, wd_ref[cs, :])
            gt, up = gu_s[:, cs], gu_s[:, us]
            sg = jax.nn.sigmoid(gt)
            dgt = (dact * up * (sg * (1.0 + gt * (1.0 - sg)))).astype(BF16)
            dup = (dact * (gt * sg)).astype(BF16)
            dgu_ref[:, cs] = dgt
            dgu_ref[:, us] = dup
            df = df + _dot(dgt, wgu_ref[cs, :]) + _dot(dup, wgu_ref[us, :])
        dhn, dgf_rows = _rms_bwd(df, hv, r2, gf)
        dh2 = dh3 + dhn
        dh2_ref[...] = dh2
        dh2b_ref[...] = dh2.astype(BF16)

        @pl.when(i == 0)
        def _():
            dgf_ref[...] = jnp.zeros_like(dgf_ref)
            dgz_ref[...] = jnp.zeros_like(dgz_ref)
            loss_ref[...] = jnp.zeros_like(loss_ref)

        dgf_ref[...] += jnp.sum(dgf_rows, axis=0, keepdims=True)
        dgz_ref[...] += jnp.sum(dgz_rows, axis=0, keepdims=True)
        loss_ref[...] += jnp.sum(jnp.sum(diff * diff, axis=0, keepdims=True), axis=1, keepdims=True) * (0.5 / D)

    return pl.pallas_call(
        body, grid=(S // tm,), name="ffn_fwd_bwd",
        in_specs=[_row(tm, D), _row(tm, D), RES, RES, RES, RES],
        out_specs=[_row(tm, D), _row(tm, D_FF), _row(tm, 2 * D_FF), _row(tm, D), _row(tm, D), _row(tm, D),
                   _acc((1, D)), _acc((1, D)), _acc((1, 128))],
        out_shape=[_sds((S, D), BF16), _sds((S, D_FF), BF16), _sds((S, 2 * D_FF), BF16), _sds((S, D), BF16),
                   _sds((S, D), F32), _sds((S, D), BF16), _sds((1, D), F32), _sds((1, D), F32), _sds((1, 128), F32)],
        scratch_shapes=[pltpu.VMEM((tm, 2 * D_FF), F32)],
        compiler_params=_params(("arbitrary",), 56),
    )(h2, target, g_ffn, g_final, wt_gu, w_down)


def _bwd_cross(dh2, h1, qc, g_cross, w_q, kv, wt_o, tm=256):
    S = h1.shape[0]

    def body(d_ref, h_ref, qc_ref, g_ref, wq_ref, kv_ref, wo_ref, dqc_ref, dh1_ref, dh1b_ref, dkv_ref, dg_ref):
        i = pl.program_id(0)

        @pl.when(i == 0)
        def _():
            dkv_ref[...] = jnp.zeros_like(dkv_ref)
            dg_ref[...] = jnp.zeros_like(dg_ref)

        dh2 = d_ref[...]
        doc = _dot(dh2.astype(BF16), wo_ref[...])
        qcb = qc_ref[...]
        for h in range(MEM_HEADS):
            cs = slice(h * MEM_HD, (h + 1) * MEM_HD)
            vs = slice(MEM_W + h * MEM_HD, MEM_W + (h + 1) * MEM_HD)
            qh, kh, vh = qcb[:, cs], kv_ref[:, cs], kv_ref[:, vs]
            p = _cross_probs(qh, kh)
            dohb = doc[:, cs].astype(BF16)
            dp = _dot_nt(dohb, vh)
            dsb = (p * (dp - jnp.sum(dp * p, axis=-1, keepdims=True)) * (MEM_HD ** -0.5)).astype(BF16)
            dqc_ref[:, cs] = _dot(dsb, kh).astype(BF16)
            dkv_ref[:, cs] += _dot_tn(dsb, qh)
            dkv_ref[:, vs] += _dot_tn(p.astype(BF16), dohb)
        dc = _dot_nt(dqc_ref[...], wq_ref[...])
        hv = h_ref[...]
        dhn, dg_rows = _rms_bwd(dc, hv, _rms(hv), g_ref[...])
        dh1 = dh2 + dhn
        dh1_ref[...] = dh1
        dh1b_ref[...] = dh1.astype(BF16)
        dg_ref[...] += jnp.sum(dg_rows, axis=0, keepdims=True)

    return pl.pallas_call(
        body, grid=(S // tm,), name="bwd_cross",
        in_specs=[_row(tm, D), _row(tm, D), _row(tm, MEM_W), RES, RES, RES, RES],
        out_specs=[_row(tm, MEM_W), _row(tm, D), _row(tm, D), _acc((256, 2 * MEM_W)), _acc((1, D))],
        out_shape=[_sds((S, MEM_W), BF16), _sds((S, D), F32), _sds((S, D), BF16), _sds((256, 2 * MEM_W), F32),
                   _sds((1, D), F32)],
        compiler_params=_params(("arbitrary",), 40),
    )(dh2, h1, qc, g_cross, w_q, kv, wt_o)


def _mem_bwd(dkv, mem, mb, g_mem, w_kv):
    def body(dkv_ref, m_ref, mb_ref, g_ref, w_ref, dw_ref, dg_ref):
        dkvb = dkv_ref[...].astype(BF16)
        dw_ref[...] = _dot_tn(mb_ref[...], dkvb)
        dm = _dot_nt(dkvb, w_ref[...])
        mv = m_ref[...]
        dg_ref[...] = jnp.sum(dm * mv * _rms(mv), axis=0, keepdims=True)

    return pl.pallas_call(
        body, name="mem_bwd", in_specs=[RES] * 5, out_specs=[RES, RES],
        out_shape=[_sds((D, 2 * MEM_W), F32), _sds((1, D), F32)],
        compiler_params=pltpu.CompilerParams(vmem_limit_bytes=32 << 20),
    )(dkv, mem, mb, g_mem, w_kv)


def _bwd_mid(dh1, gl, ba, bs, uv, ls_, ya, wt, bst, g_sgu, b_gate, wt_ba, wt_bs, w_out, tm=256):
    S = dh1.shape[0]

    def body(d_ref, gl_ref, ba_ref, bs_ref, uv_ref, l0, l1, l2, ya_ref,
             wt_ref, bst_ref, gs_ref, bg_ref, wba_ref, wbs_ref, wo_ref,
             dba_ref, dbs_ref, dgl_ref, duv_ref, do0, do1, do2, c0, c1, c2,
             dbg_ref, dgs_ref, dws_ref, dbsa_ref, mixed_s, dvn_s):
        i = pl.program_id(0)

        @pl.when(i == 0)
        def _():
            for r in (dbg_ref, dgs_ref, dws_ref, dbsa_ref):
                r[...] = jnp.zeros_like(r)

        dm = _dot_nt(d_ref[...].astype(BF16), wo_ref[...])
        gates = jax.nn.sigmoid(gl_ref[...].astype(F32) + bg_ref[...])
        g0, g1 = gates[:, :D], gates[:, D:]
        dbab = (dm * g0).astype(BF16)
        dbsb = (dm * g1).astype(BF16)
        dba_ref[...] = dbab
        dbs_ref[...] = dbsb
        dg0 = dm * ba_ref[...].astype(F32) * g0 * (1.0 - g0)
        dg1 = dm * bs_ref[...].astype(F32) * g1 * (1.0 - g1)
        dgl_ref[:, :D] = dg0.astype(BF16)
        dgl_ref[:, D:] = dg1.astype(BF16)
        dbg_ref[:, :D] += jnp.sum(dg0, axis=0, keepdims=True)
        dbg_ref[:, D:] += jnp.sum(dg1, axis=0, keepdims=True)
        dya = _dot(dbab, wba_ref[...])
        dys = _dot(dbsb, wbs_ref[...])

        uvf = uv_ref[...].astype(F32)
        gs = gs_ref[...]
        u, v, rv, vnb, t = _sgu_forward(uvf, gs, wt_ref, bst_ref, mixed_s, tm)
        du = dys * mixed_s[...]
        dmixed = dys * u
        for ci in range(tm // 128):
            for g in range(4):
                rs, cs = slice(ci * 128, (ci + 1) * 128), slice(g * 128, (g + 1) * 128)
                dmx = dmixed[rs, cs]
                dmxb = dmx.astype(BF16)
                dvn_s[rs, cs] = _dot_tn(wt_ref[g], dmxb)
                dws_ref[g] += _dot_nt(dmxb, vnb[rs, cs])
                dbsa_ref[g] += dmx
        dv, dgs_rows = _rms_bwd(dvn_s[...], v, rv, gs)
        dgs_ref[...] += jnp.sum(dgs_rows, axis=0, keepdims=True)
        gg = _gelu_grad(uvf, t)
        duv_ref[:, :SGU_W] = (du * gg[:, :SGU_W]).astype(BF16)
        duv_ref[:, SGU_W:] = (dv * gg[:, SGU_W:]).astype(BF16)

        a0, a1, a2 = _group_weights(l0[...], l1[...], l2[...])
        prod = dya * ya_ref[...].astype(F32)
        _, masks = _head_masks()
        hs = jnp.zeros_like(prod)
        for h in range(4):
            sh = jnp.sum(jnp.where(masks[h], prod, 0.0), axis=-1, keepdims=True)
            hs = jnp.where(masks[h], sh, hs)
        for a, do_ref, c_ref in ((a0, do0, c0), (a1, do1, c1), (a2, do2, c2)):
            do_ref[...] = (a * dya).astype(BF16)
            c_ref[...] = a * hs

    gw = _row(tm, GROUP_W)
    return pl.pallas_call(
        body, grid=(S // tm,), name="bwd_mid",
        in_specs=[_row(tm, D), _row(tm, 2048), _row(tm, D), _row(tm, D), _row(tm, 1024), gw, gw, gw, gw] + [RES] * 7,
        out_specs=[_row(tm, D), _row(tm, D), _row(tm, 2048), _row(tm, 1024), gw, gw, gw, gw, gw, gw,
                   _acc((1, 2048)), _acc((1, SGU_W)), _acc((4, 128, 128)), _acc((4, 128, 128))],
        out_shape=[_sds((S, D), BF16), _sds((S, D), BF16), _sds((S, 2048), BF16), _sds((S, 1024), BF16)]
        + [_sds((S, GROUP_W), BF16)] * 3 + [_sds((S, GROUP_W), F32)] * 3
        + [_sds((1, 2048), F32), _sds((1, SGU_W), F32), _sds((4, 128, 128), F32), _sds((4, 128, 128), F32)],
        scratch_shapes=[pltpu.VMEM((tm, SGU_W), F32), pltpu.VMEM((tm, SGU_W), F32)],
        compiler_params=_params(("arbitrary",), 48),
    )(dh1, gl, ba, bs, uv, *ls_, ya, wt, bst, g_sgu, b_gate, wt_ba, wt_bs, w_out)


def _attn_bwd(qkv, do, lse, corr, g, prev):
    S = qkv.shape[0]
    d = DIL_GROUPS[g][1]
    L = S // d
    nb = L // BLK
    qv = qkv.reshape(L, d * 2304)
    view = lambda t: t.reshape(L, d * GROUP_W)
    bias = jnp.asarray(_attn_bias(g))
    n_prev = 0 if prev is None else 3

    def body(*refs):
        q_ref, kc_ref, kp_ref, vc_ref, vp_ref, do_ref, l_ref, c_ref, b_ref = refs[:9]
        dq_ref, dk_ref, dv_ref, ck_s, cv_s = refs[9 + n_prev:]
        j = pl.program_id(1)

        @pl.when(j == 0)
        def _():
            ck_s[...] = jnp.zeros_like(ck_s)
            cv_s[...] = jnp.zeros_like(cv_s)

        @pl.when(j < nb)
        def _():
            q = q_ref[...]
            dob = do_ref[...]
            kk = jnp.concatenate([kp_ref[...], kc_ref[...]], axis=0)
            vv = jnp.concatenate([vp_ref[...], vc_ref[...]], axis=0)
            lane, masks = _head_masks()
            extra = jnp.where((j == 0) & (lane < BLK), NEG, 0.0).astype(F32)
            dq = jnp.zeros((BLK, GROUP_W), F32)
            dkk = jnp.zeros((2 * BLK, GROUP_W), F32)
            dvv = jnp.zeros((2 * BLK, GROUP_W), F32)
            for h in range(4):
                mh = masks[h]
                qh = jnp.where(mh, q, jnp.zeros_like(q))
                doh = jnp.where(mh, dob, jnp.zeros_like(dob))
                s = _dot_nt(qh, kk) * 0.125 + b_ref[h] + extra
                p = jnp.exp(s - l_ref[:, HEAD * h:HEAD * h + 1])
                dp = _dot_nt(doh, vv)
                dsb = (p * (dp - c_ref[:, HEAD * h:HEAD * h + 1]) * 0.125).astype(BF16)
                dq = jnp.where(mh, _dot(dsb, kk), dq)
                dkk = jnp.where(mh, _dot_tn(dsb, q), dkk)
                dvv = jnp.where(mh, _dot_tn(p.astype(BF16), dob), dvv)
            dq_ref[...] = dq.astype(BF16)
            dk_ref[...] = (ck_s[...] + dkk[:BLK]).astype(BF16)
            dv_ref[...] = (cv_s[...] + dvv[:BLK]).astype(BF16)
            ck_s[...] = dkk[BLK:]
            cv_s[...] = dvv[BLK:]

        @pl.when(j == nb)
        def _():
            dk_ref[...] = ck_s[...].astype(BF16)
            dv_ref[...] = cv_s[...].astype(BF16)

    def blk(f):
        return pl.BlockSpec((BLK, GROUP_W), f)

    qn = lambda j: jnp.minimum(j, nb - 1)
    pn = lambda j: jnp.maximum(jnp.minimum(j, nb - 1) - 1, 0)
    kn = lambda j: jnp.maximum(j - 1, 0)
    ins = [qv, qv, qv, qv, qv, view(do), view(lse), view(corr), bias]
    in_specs = [blk(lambda r, j: (qn(j), 9 * r + g)),
                blk(lambda r, j: (qn(j), 9 * r + 3 + g)), blk(lambda r, j: (pn(j), 9 * r + 3 + g)),
                blk(lambda r, j: (qn(j), 9 * r + 6 + g)), blk(lambda r, j: (pn(j), 9 * r + 6 + g)),
                blk(lambda r, j: (qn(j), r)), blk(lambda r, j: (qn(j), r)), blk(lambda r, j: (qn(j), r)),
                pl.BlockSpec((4, BLK, 2 * BLK), lambda r, j: (0, 0, 0))]
    aliases = {}
    if prev is not None:
        ins += [t.reshape(L, d * 768) for t in prev]
        in_specs += [ANY] * 3
        aliases = {9: 0, 10: 1, 11: 2}
    outs = pl.pallas_call(
        body, grid=(d, nb + 1), name=f"attn_bwd_g{g}",
        in_specs=in_specs,
        out_specs=[blk(lambda r, j: (qn(j), 3 * r + g)), blk(lambda r, j: (kn(j), 3 * r + g)),
                   blk(lambda r, j: (kn(j), 3 * r + g))],
        out_shape=[_sds((L, d * 768), BF16)] * 3,
        scratch_shapes=[pltpu.VMEM((BLK, GROUP_W), F32), pltpu.VMEM((BLK, GROUP_W), F32)],
        input_output_aliases=aliases,
        compiler_params=_params(("arbitrary", "arbitrary"), 32),
    )(*ins)
    return [t.reshape(S, 768) for t in outs]


def _bwd_in(dq, dk, dv, duv, dgl, dh1, x, g_mix, wt_in, tm=512):
    S = x.shape[0]

    def body(dq_ref, dk_ref, dv_ref, duv_ref, dgl_ref, d_ref, x_ref, g_ref, w_ref, dx_ref, dg_ref):
        i = pl.program_id(0)

        @pl.when(i == 0)
        def _():
            dg_ref[...] = jnp.zeros_like(dg_ref)

        da = (_dot(dq_ref[...], w_ref[0:768, :]) + _dot(dk_ref[...], w_ref[768:1536, :])
              + _dot(dv_ref[...], w_ref[1536:2304, :]) + _dot(duv_ref[...], w_ref[2304:3328, :])
              + _dot(dgl_ref[...], w_ref[3328:5376, :]))
        xv = x_ref[...]
        dxn, dg_rows = _rms_bwd(da, xv, _rms(xv), g_ref[...])
        dx_ref[...] = d_ref[...] + dxn
        dg_ref[...] += jnp.sum(dg_rows, axis=0, keepdims=True)

    return pl.pallas_call(
        body, grid=(S // tm,), name="bwd_in",
        in_specs=[_row(tm, 768), _row(tm, 768), _row(tm, 768), _row(tm, 1024), _row(tm, 2048), _row(tm, D), _row(tm, D),
                  RES, RES],
        out_specs=[_row(tm, D), _acc((1, D))],
        out_shape=[_sds((S, D), F32), _sds((1, D), F32)],
        compiler_params=_params(("arbitrary",), 56),
    )(dq, dk, dv, duv, dgl, dh1, x, g_mix, wt_in)


def _tn_matmul(a, b, name, tk, ts=512, into=None):
    S, K = a.shape
    N = b.shape[1]
    off = 0 if into is None else into[1] // tk

    def body(*refs):
        a_ref, b_ref, o_ref = refs[0], refs[1], refs[-1]

        @pl.when(pl.program_id(1) == 0)
        def _():
            o_ref[...] = jnp.zeros_like(o_ref)

        o_ref[...] += _dot_tn(a_ref[...], b_ref[...])

    ins = [a, b]
    in_specs = [pl.BlockSpec((ts, tk), lambda k, s: (s, k)), pl.BlockSpec((ts, N), lambda k, s: (s, 0))]
    out_shape = _sds((K, N), F32)
    aliases = {}
    if into is not None:
        ins.append(into[0])
        in_specs.append(ANY)
        out_shape = _sds(into[0].shape, F32)
        aliases = {2: 0}
    return pl.pallas_call(
        body, grid=(K // tk, S // ts), name=name, in_specs=in_specs,
        out_specs=pl.BlockSpec((tk, N), lambda k, s: (k + off, 0)), out_shape=out_shape,
        input_output_aliases=aliases,
        compiler_params=_params(("parallel", "arbitrary"), 48),
    )(*ins)


def _chip_peers(x, y):
    return [(1 - x, y), (x, 1 - y), (1 - x, 1 - y)]


def _remote(src, dst, ssem, rsem, dev):
    return pltpu.make_async_remote_copy(src_ref=src, dst_ref=dst, send_sem=ssem, recv_sem=rsem, device_id=dev,
                                        device_id_type=MESH)


def _gather_weights(shards):
    n = len(shards)
    halves = [s.reshape(2, s.shape[0] // 2, s.shape[1]) for s in shards]

    def body(*refs):
        ins, outs = refs[:n], refs[n:2 * n]
        ssem, rsem, lsem = refs[2 * n:]
        x, y, c = lax.axis_index("x"), lax.axis_index("y"), lax.axis_index("c")
        me = 2 * x + y
        chips = _chip_peers(x, y)
        sib = (x, y, 1 - c)
        local = [pltpu.make_async_copy(ins[i], outs[i].at[me], lsem.at[i]) for i in range(n)]
        for cp in local:
            cp.start()
        sends = []
        for i in range(n):
            for k, (px, py) in enumerate(chips):
                cp = _remote(ins[i].at[c], outs[i].at[me, c], ssem.at[6 * i + k], rsem.at[6 * i + k], (px, py, c))
                cp.start()
                sends.append(cp)
        for i in range(n):
            for k, (px, py) in enumerate(chips):
                landed = outs[i].at[2 * px + py, c]
                _remote(landed, landed, ssem.at[6 * i + k], rsem.at[6 * i + k], (px, py, c)).wait_recv()
                cp = _remote(landed, landed, ssem.at[6 * i + 3 + k], rsem.at[6 * i + 3 + k], sib)
                cp.start()
                sends.append(cp)
        for i in range(n):
            for k, (px, py) in enumerate(chips):
                passed = outs[i].at[2 * px + py, 1 - c]
                _remote(passed, passed, ssem.at[6 * i + 3 + k], rsem.at[6 * i + 3 + k], sib).wait_recv()
        for cp in sends:
            cp.wait_send()
        for cp in local:
            cp.wait()

    outs = pl.pallas_call(
        body, name="gather_weights", in_specs=[ANY] * n, out_specs=[ANY] * n,
        out_shape=[_sds((4,) + h.shape, BF16) for h in halves],
        scratch_shapes=[pltpu.SemaphoreType.DMA((6 * n,)), pltpu.SemaphoreType.DMA((6 * n,)),
                        pltpu.SemaphoreType.DMA((n,))],
    )(*halves)
    return [o.reshape(4 * s.shape[0], s.shape[1]) for o, s in zip(outs, shards)]


def _swap_halves(grads):
    n = len(grads)
    g4 = [g.reshape(4, 2, g.shape[0] // 8, g.shape[1]) for g in grads]

    def body(*refs):
        ins, own, got = refs[:n], refs[n:2 * n], refs[2 * n:3 * n]
        ssem, rsem, lsem = refs[3 * n:]
        x, y, c = lax.axis_index("x"), lax.axis_index("y"), lax.axis_index("c")
        sib = (x, y, 1 - c)
        cps = []
        for i in range(n):
            lc = pltpu.make_async_copy(ins[i].at[:, c], own[i], lsem.at[i])
            rc = _remote(ins[i].at[:, 1 - c], got[i], ssem.at[i], rsem.at[i], sib)
            lc.start()
            rc.start()
            cps += [lc, rc]
        for cp in cps:
            cp.wait()

    half = [_sds((4, g.shape[2], g.shape[3]), F32) for g in g4]
    outs = pl.pallas_call(
        body, name="swap_halves", in_specs=[ANY] * n, out_specs=[ANY] * (2 * n), out_shape=half + half,
        scratch_shapes=[pltpu.SemaphoreType.DMA((n,)), pltpu.SemaphoreType.DMA((n,)), pltpu.SemaphoreType.DMA((n,))],
    )(*g4)
    return outs[:n], outs[n:]


def _scatter_partials(sums, sums_b):
    n = len(sums)

    def body(*refs):
        hs, hbs, mine, parts = refs[:n], refs[n:2 * n], refs[2 * n:3 * n], refs[3 * n:4 * n]
        ssem, rsem, lsem = refs[4 * n:]
        x, y, c = lax.axis_index("x"), lax.axis_index("y"), lax.axis_index("c")
        me = 2 * x + y
        cps = []
        for i in range(n):
            lc = pltpu.make_async_copy(hs[i].at[me], mine[i], lsem.at[i])
            lc.start()
            cps.append(lc)
            for k, (px, py) in enumerate(_chip_peers(x, y)):
                rc = _remote(hbs[i].at[2 * px + py], parts[i].at[k], ssem.at[3 * i + k], rsem.at[3 * i + k], (px, py, c))
                rc.start()
                cps.append(rc)
        for cp in cps:
            cp.wait()

    outs = pl.pallas_call(
        body, name="scatter_partials", in_specs=[ANY] * (2 * n), out_specs=[ANY] * (2 * n),
        out_shape=[_sds(s.shape[1:], F32) for s in sums] + [_sds((3,) + s.shape[1:], BF16) for s in sums],
        scratch_shapes=[pltpu.SemaphoreType.DMA((3 * n,)), pltpu.SemaphoreType.DMA((3 * n,)),
                        pltpu.SemaphoreType.DMA((n,))],
    )(*sums, *sums_b)
    return outs[:n], outs[n:]


def _share_halves(reduced):
    n = len(reduced)

    def body(*refs):
        ins, outs = refs[:n], refs[n:2 * n]
        ssem, rsem, lsem = refs[2 * n:]
        x, y, c = lax.axis_index("x"), lax.axis_index("y"), lax.axis_index("c")
        sib = (x, y, 1 - c)
        cps = []
        for i in range(n):
            lc = pltpu.make_async_copy(ins[i], outs[i].at[c], lsem.at[i])
            lc.start()
            rc = _remote(ins[i], outs[i].at[c], ssem.at[i], rsem.at[i], sib)
            rc.start()
            cps.append(lc)
        for i in range(n):
            theirs = outs[i].at[1 - c]
            rc = _remote(ins[i], theirs, ssem.at[i], rsem.at[i], sib)
            rc.wait_send()
            rc.wait_recv()
        for cp in cps:
            cp.wait()

    outs = pl.pallas_call(
        body, name="share_halves", in_specs=[ANY] * n, out_specs=[ANY] * n,
        out_shape=[_sds((2,) + r.shape, F32) for r in reduced],
        scratch_shapes=[pltpu.SemaphoreType.DMA((n,)), pltpu.SemaphoreType.DMA((n,)), pltpu.SemaphoreType.DMA((n,))],
    )(*reduced)
    return [o.reshape(2 * r.shape[0], r.shape[1]) for o, r in zip(outs, reduced)]


def _gather_small(pack):
    P = pack.shape[0]

    def body(in_ref, out_ref, ssem, rsem, lsem):
        x, y, c = lax.axis_index("x"), lax.axis_index("y"), lax.axis_index("c")
        me = 4 * x + 2 * y + c
        lc = pltpu.make_async_copy(in_ref, out_ref.at[me], lsem)
        lc.start()
        cps = []
        for rel in range(1, 8):
            px = 1 - x if rel & 4 else x
            py = 1 - y if rel & 2 else y
            pc = 1 - c if rel & 1 else c
            rc = _remote(in_ref, out_ref.at[me], ssem.at[rel - 1], rsem.at[rel - 1], (px, py, pc))
            rc.start()
            cps.append((rc, 4 * px + 2 * py + pc))
        for rel, (rc, peer) in enumerate(cps):
            rc.wait_send()
            _remote(in_ref, out_ref.at[peer], ssem.at[rel], rsem.at[rel], (x, y, c)).wait_recv()
        lc.wait()

    return pl.pallas_call(
        body, name="gather_small", in_specs=[ANY], out_specs=ANY, out_shape=_sds((8, P, 128), F32),
        scratch_shapes=[pltpu.SemaphoreType.DMA((7,)), pltpu.SemaphoreType.DMA((7,)), pltpu.SemaphoreType.DMA],
    )(pack)


def _tile(rows, cap=256):
    t = min(rows, cap) // 16 * 16
    while rows % t:
        t -= 16
    return t


def _elementwise(fn, ins, out_dtypes, name):
    R, W = ins[0].shape
    tr = _tile(R, max(8, min(512, (1 << 18) // W // 8 * 8)))

    def body(*refs):
        outs = fn(*[r[...] for r in refs[:len(ins)]])
        for o_ref, o in zip(refs[len(ins):], outs):
            o_ref[...] = o.astype(o_ref.dtype)

    return pl.pallas_call(
        body, grid=(R // tr,), name=name, in_specs=[_row(tr, W)] * len(ins), out_specs=[_row(tr, W)] * len(out_dtypes),
        out_shape=[_sds((R, W), dt) for dt in out_dtypes],
        compiler_params=_params(("parallel",), 48),
    )(*ins)


def _adamw(w, g, m, v):
    m = B1 * m + (1.0 - B1) * g
    v = B2 * v + (1.0 - B2) * (g * g)
    m_hat = m / (1.0 - B1 ** STEP)
    v_hat = v / (1.0 - B2 ** STEP)
    return -LR * (m_hat / (jnp.sqrt(v_hat) + AEPS) + WD * w), m, v


def _adam_small(w, m, v, packs):
    def body(w_ref, m_ref, v_ref, p_ref, g_ref, d_ref, nm_ref, nv_ref):
        g = p_ref[0]
        for k in range(1, 8):
            g = g + p_ref[k]
        g_ref[...] = g
        d_ref[...], nm_ref[...], nv_ref[...] = _adamw(w_ref[...], g, m_ref[...], v_ref[...])

    return pl.pallas_call(
        body, name="adam_small", in_specs=[RES] * 4, out_specs=[RES] * 4, out_shape=[_sds(w.shape, F32)] * 4,
        compiler_params=pltpu.CompilerParams(vmem_limit_bytes=32 << 20),
    )(w, m, v, packs)


def _local_step(xs, tgt, mems, weights, small):
    wt_in, wt_ba, wt_bs, wo, wq, wkv, wt_o, wt_gu, wd = weights
    g_mix, b_gate, w_sgu, b_sgu, g_sgu, g_cross, g_mem, g_ffn, g_final = small
    wt = jnp.tril(w_sgu).astype(BF16)
    bst = b_sgu.T

    a, qkv, uv, gl = _fwd_in(xs, g_mix, wt_in)
    os_, ls_ = zip(*[_attn_fwd(qkv, g) for g in range(3)])
    ya, ys, ba, bs, mg, h1 = _fwd_mid(xs, os_, ls_, uv, gl, wt, bst, g_sgu, b_gate, wt_ba, wt_bs, wo)
    mb, kv = _mem_fwd(mems, g_mem, wkv)
    cb, qc, oc, h2 = _fwd_cross(h1, g_cross, wq, kv, wt_o)
    f, act, dgu, dh3b, dh2, dh2b, dg_ffn, dg_final, loss = _ffn_fwd_bwd(h2, tgt, g_ffn, g_final, wt_gu, wd)

    dqc, dh1, dh1b, dkv, dg_cross = _bwd_cross(dh2, h1, qc, g_cross, wq, kv, wt_o)
    dw_kv, dg_mem = _mem_bwd(dkv, mems, mb, g_mem, wkv)
    (dba, dbs, dgl, duv, do0, do1, do2, c0, c1, c2, db_gate, dg_sgu, dws, dbs_acc) = _bwd_mid(
        dh1, gl, ba, bs, uv, ls_, ya, wt, bst, g_sgu, b_gate, wt_ba, wt_bs, wo)
    dqkv = None
    for g, (do, corr) in enumerate(((do0, c0), (do1, c1), (do2, c2))):
        dqkv = _attn_bwd(qkv, do, ls_[g], corr, g, dqkv)
    dq, dk, dv = dqkv
    grad_x, dg_mix = _bwd_in(dq, dk, dv, duv, dgl, dh1, xs, g_mix, wt_in)

    dwt_in = _tn_matmul(dq, a, "dw_in_q", 256)
    dwt_in = jnp.concatenate([dwt_in, _tn_matmul(dk, a, "dw_in_k", 256), _tn_matmul(dv, a, "dw_in_v", 256),
                              _tn_matmul(duv, a, "dw_in_uv", 512), _tn_matmul(dgl, a, "dw_in_gl", 512)], axis=0)
    grads = [dwt_in,
             _tn_matmul(dba, ya, "dw_branch_attn", 512),
             _tn_matmul(dbs, ys, "dw_branch_sgu", 512),
             _tn_matmul(mg, dh1b, "dw_out", 512),
             _tn_matmul(cb, dqc, "dw_q_cross", 512),
             dw_kv,
             _tn_matmul(dh2b, oc, "dw_o_cross", 512),
             _tn_matmul(dgu, f, "dw_gate_up", 512),
             _tn_matmul(act, dh3b, "dw_down", 256)]

    return loss, grad_x, grads, (dg_mix, db_gate, dws, dbs_acc, dg_sgu, dg_cross, dg_mem, dg_ffn, dg_final)


def kernel(x, mem, g_mix, w_in, b_gate, w_sgu_spatial, b_sgu_spatial, g_sgu, w_branch_attn, w_branch_sgu, w_out, g_cross, g_mem, w_q_cross, w_kv_cross, w_o_cross, g_ffn, w_gate_up, w_down, g_final, loss_target, m_g_mix, m_w_in, m_b_gate, m_w_sgu_spatial, m_b_sgu_spatial, m_g_sgu, m_w_branch_attn, m_w_branch_sgu, m_w_out, m_g_cross, m_g_mem, m_w_q_cross, m_w_kv_cross, m_w_o_cross, m_g_ffn, m_w_gate_up, m_w_down, m_g_final, v_g_mix, v_w_in, v_b_gate, v_w_sgu_spatial, v_b_sgu_spatial, v_g_sgu, v_w_branch_attn, v_w_branch_sgu, v_w_out, v_g_cross, v_g_mem, v_w_q_cross, v_w_kv_cross, v_w_o_cross, v_g_ffn, v_w_gate_up, v_w_down, v_g_final):
    S = x.shape[1]
    xs, tgt, mems = x.reshape(S, D), loss_target.reshape(S, D), mem.reshape(mem.shape[1], D)
    g_final2 = g_final.reshape(1, D)

    big = [("w_in", w_in[0], m_w_in[0], v_w_in[0], True),
           ("w_branch_attn", w_branch_attn[0], m_w_branch_attn[0], v_w_branch_attn[0], True),
           ("w_branch_sgu", w_branch_sgu[0], m_w_branch_sgu[0], v_w_branch_sgu[0], True),
           ("w_out", w_out[0], m_w_out[0], v_w_out[0], False),
           ("w_q_cross", w_q_cross[0], m_w_q_cross[0], v_w_q_cross[0], False),
           ("w_kv_cross", w_kv_cross[0], m_w_kv_cross[0], v_w_kv_cross[0], False),
           ("w_o_cross", w_o_cross[0], m_w_o_cross[0], v_w_o_cross[0], True),
           ("w_gate_up", w_gate_up[0], m_w_gate_up[0], v_w_gate_up[0], True),
           ("w_down", w_down[0], m_w_down[0], v_w_down[0], False)]
    shards = [(w.T if tr else w).astype(BF16) for _, w, _, _, tr in big]
    wt_in, wt_ba, wt_bs, wo, wq, wkv, wt_o, wt_gu, wd = _gather_weights(shards)

    (loss, grad_x, grads, (dg_mix, db_gate, dws, dbs_acc, dg_sgu, dg_cross, dg_mem, dg_ffn, dg_final)) = _local_step(
        xs, tgt, mems, (wt_in, wt_ba, wt_bs, wo, wq, wkv, wt_o, wt_gu, wd),
        (g_mix, b_gate, w_sgu_spatial[0], b_sgu_spatial[0], g_sgu, g_cross, g_mem, g_ffn, g_final2))

    own, got = _swap_halves(grads)
    sums, sums_b = [], []
    for i, (o, t) in enumerate(zip(own, got)):
        shp = o.shape
        s_, sb_ = _elementwise(lambda p, q: (p + q, p + q), [o.reshape(-1, shp[2]), t.reshape(-1, shp[2])],
                               [F32, BF16], f"chip_sum_{big[i][0]}")
        sums.append(s_.reshape(shp))
        sums_b.append(sb_.reshape(shp))
    mine, parts = _scatter_partials(sums, sums_b)
    reduced = []
    for i, (o, p) in enumerate(zip(mine, parts)):
        reduced.append(_elementwise(lambda p0, p1, p2, p3: (p0 + p1.astype(F32) + p2.astype(F32) + p3.astype(F32),),
                                    [o.astype(F32)] + [p[k] for k in range(3)], [F32], f"mesh_sum_{big[i][0]}")[0])
    full = _share_halves(reduced)

    big_out = {}
    for (name, w, m, v, tr), gsh in zip(big, full):
        gsh = gsh.T if tr else gsh
        delta, nm, nv = _elementwise(_adamw, [w, gsh, m, v], [F32, F32, F32], f"adam_{name}")
        big_out[name] = tuple(t[None] for t in (gsh, delta, nm, nv))

    small = [("g_mix", g_mix, m_g_mix, v_g_mix, dg_mix), ("b_gate", b_gate, m_b_gate, v_b_gate, db_gate),
             ("w_sgu_spatial", w_sgu_spatial, m_w_sgu_spatial, v_w_sgu_spatial, jnp.tril(dws)),
             ("b_sgu_spatial", b_sgu_spatial, m_b_sgu_spatial, v_b_sgu_spatial, jnp.sum(dbs_acc, axis=-1)),
             ("g_sgu", g_sgu, m_g_sgu, v_g_sgu, dg_sgu), ("g_cross", g_cross, m_g_cross, v_g_cross, dg_cross),
             ("g_mem", g_mem, m_g_mem, v_g_mem, dg_mem), ("g_ffn", g_ffn, m_g_ffn, v_g_ffn, dg_ffn),
             ("g_final", g_final, m_g_final, v_g_final, dg_final)]

    def pack(parts_, tail):
        return jnp.concatenate([p.reshape(-1) for p in parts_] + [tail]).reshape(-1, 128)

    zeros = jnp.zeros((1024,), F32)
    gp = pack([s[4] for s in small], jnp.pad(loss.reshape(-1)[:1], (0, 1023)))
    wp, mp, vp = (pack([s[k] for s in small], zeros) for k in (1, 2, 3))
    gsum, dsm, nms, nvs = _adam_small(wp, mp, vp, _gather_small(gp))
    small_out, off = {}, 0
    for name, w, _, _, _ in small:
        n = w.size
        small_out[name] = tuple(t.reshape(-1)[off:off + n].reshape(w.shape) for t in (gsum, dsm, nms, nvs))
        off += n
    total_loss = gsum.reshape(-1)[off]

    order = ["g_mix", "w_in", "b_gate", "w_sgu_spatial", "b_sgu_spatial", "g_sgu", "w_branch_attn", "w_branch_sgu",
             "w_out", "g_cross", "g_mem", "w_q_cross", "w_kv_cross", "w_o_cross", "g_ffn", "w_gate_up", "w_down",
             "g_final"]
    res = {**big_out, **small_out}
    outs = [total_loss, grad_x.reshape(x.shape)]
    for k in range(4):
        outs += [res[nm][k] for nm in order]
    return tuple(outs)
```

```python
import math

import numpy as np
import jax
import jax.numpy as jnp
from jax import lax
from jax.experimental import pallas as pl
from jax.experimental.pallas import tpu as pltpu

F32, BF16 = jnp.float32, jnp.bfloat16
MESH = pl.DeviceIdType.MESH
ANY = pl.BlockSpec(memory_space=pl.ANY)
RES = pl.BlockSpec(memory_space=pltpu.VMEM)

D = 1024
HEAD = 64
GROUP_W = 256
DIL_GROUPS = ((128, 1), (512, 4), (2048, 16))
BLK = 128
SGU_W = 512
MEM_HEADS, MEM_HD, MEM_W = 4, 128, 512
D_FF = 2816
FF_CHUNK = 256
EPS = 1e-6
NEG = -1e30
LR, B1, B2, AEPS, WD, STEP = 0.001, 0.9, 0.999, 1e-08, 0.01, 10
GELU_K, GELU_C = 0.7978845608028654, 0.044715


def _dot(a, b):
    return jnp.dot(a, b, preferred_element_type=F32)


def _dot_nt(a, b):
    return lax.dot_general(a, b, (((1,), (1,)), ((), ())), preferred_element_type=F32)


def _dot_tn(a, b):
    return lax.dot_general(a, b, (((0,), (0,)), ((), ())), preferred_element_type=F32)


def _row(tm, w):
    return pl.BlockSpec((tm, w), lambda i: (i, 0))


def _acc(shape):
    return pl.BlockSpec(shape, lambda i: (0,) * len(shape))


def _params(sem, mb):
    return pltpu.CompilerParams(dimension_semantics=sem, vmem_limit_bytes=mb << 20)


def _sds(shape, dt):
    return jax.ShapeDtypeStruct(shape, dt)


def _rms(h):
    return lax.rsqrt(jnp.mean(h * h, axis=-1, keepdims=True) + EPS)


def _rms_bwd(dy, h, r, g):
    t = dy * g
    dh = r * t - h * (r * r * r) * jnp.mean(t * h, axis=-1, keepdims=True)
    return dh, dy * h * r


def _gelu(x):
    t = jnp.tanh(GELU_K * (x + GELU_C * x * x * x))
    return 0.5 * x * (1.0 + t), t


def _gelu_grad(x, t):
    return 0.5 * (1.0 + t) + 0.5 * x * (1.0 - t * t) * GELU_K * (1.0 + 3.0 * GELU_C * x * x)


def _alibi_slopes():
    def pow2(n):
        start = 2.0 ** (-8.0 / n)
        return [start ** (i + 1) for i in range(n)]
    n = 12
    c = 2 ** int(math.floor(math.log2(n)))
    s = pow2(c) + pow2(2 * c)[0::2][: n - c]
    return np.array(sorted(s, reverse=True), dtype=np.float32).reshape(3, 4)


def _attn_bias(g):
    win, dil = DIL_GROUPS[g]
    steps = (np.arange(BLK)[:, None] + BLK) - np.arange(2 * BLK)[None, :]
    valid = (steps >= 0) & (steps <= win // dil)
    dist = (np.clip(steps, 0, None) * dil).astype(np.float32)
    b = -_alibi_slopes()[g][:, None, None] * dist[None]
    return np.where(valid[None], b, NEG).astype(np.float32)


def _head_masks():
    lane = lax.broadcasted_iota(jnp.int32, (1, GROUP_W), 1)
    return lane, [(lane >= HEAD * h) & (lane < HEAD * (h + 1)) for h in range(4)]


def _fwd_in(x, g_mix, wt_in, tm=512):
    S = x.shape[0]

    def body(x_ref, g_ref, w_ref, a_ref, qkv_ref, uv_ref, gl_ref):
        xv = x_ref[...]
        a = (xv * _rms(xv) * g_ref[...]).astype(BF16)
        a_ref[...] = a
        for j in range(9):
            qkv_ref[:, j * 256:(j + 1) * 256] = _dot_nt(a, w_ref[j * 256:(j + 1) * 256, :]).astype(BF16)
        uv_ref[...] = _dot_nt(a, w_ref[2304:3328, :]).astype(BF16)
        gl_ref[...] = _dot_nt(a, w_ref[3328:5376, :]).astype(BF16)

    return pl.pallas_call(
        body, grid=(S // tm,), name="fwd_in",
        in_specs=[_row(tm, D), RES, RES],
        out_specs=[_row(tm, D), _row(tm, 2304), _row(tm, 1024), _row(tm, 2048)],
        out_shape=[_sds((S, D), BF16), _sds((S, 2304), BF16), _sds((S, 1024), BF16), _sds((S, 2048), BF16)],
        compiler_params=_params(("parallel",), 56),
    )(x, g_mix, wt_in)


def _attn_fwd(qkv, g):
    S = qkv.shape[0]
    d = DIL_GROUPS[g][1]
    L = S // d
    nb = L // BLK
    qv = qkv.reshape(L, d * 2304)
    bias = jnp.asarray(_attn_bias(g))

    def body(q_ref, kc_ref, kp_ref, vc_ref, vp_ref, b_ref, o_ref, l_ref):
        n = pl.program_id(1)
        q = q_ref[...]
        kk = jnp.concatenate([kp_ref[...], kc_ref[...]], axis=0)
        vv = jnp.concatenate([vp_ref[...], vc_ref[...]], axis=0)
        lane, masks = _head_masks()
        extra = jnp.where((n == 0) & (lane < BLK), NEG, 0.0).astype(F32)
        o_acc = jnp.zeros((BLK, GROUP_W), F32)
        l_acc = jnp.zeros((BLK, GROUP_W), F32)
        for h in range(4):
            qh = jnp.where(masks[h], q, jnp.zeros_like(q))
            s = _dot_nt(qh, kk) * 0.125 + b_ref[h] + extra
            mx = jnp.max(s, axis=-1, keepdims=True)
            e = jnp.exp(s - mx)
            den = jnp.sum(e, axis=-1, keepdims=True)
            oh = _dot(e.astype(BF16), vv) / den
            o_acc = jnp.where(masks[h], oh, o_acc)
            l_acc = jnp.where(masks[h], mx + jnp.log(den), l_acc)
        o_ref[...] = o_acc
        l_ref[...] = l_acc

    def blk(f):
        return pl.BlockSpec((BLK, GROUP_W), f)

    o, lse = pl.pallas_call(
        body, grid=(d, nb), name=f"attn_fwd_g{g}",
        in_specs=[blk(lambda r, n: (n, 9 * r + g)),
                  blk(lambda r, n: (n, 9 * r + 3 + g)), blk(lambda r, n: (jnp.maximum(n - 1, 0), 9 * r + 3 + g)),
                  blk(lambda r, n: (n, 9 * r + 6 + g)), blk(lambda r, n: (jnp.maximum(n - 1, 0), 9 * r + 6 + g)),
                  pl.BlockSpec((4, BLK, 2 * BLK), lambda r, n: (0, 0, 0))],
        out_specs=[blk(lambda r, n: (n, r)), blk(lambda r, n: (n, r))],
        out_shape=[_sds((L, d * GROUP_W), F32), _sds((L, d * GROUP_W), F32)],
        compiler_params=_params(("parallel", "parallel"), 32),
    )(qv, qv, qv, qv, qv, bias)
    return o.reshape(S, GROUP_W), lse.reshape(S, GROUP_W)


def _group_weights(l0, l1, l2):
    m = jnp.maximum(jnp.maximum(l0, l1), l2)
    e0, e1, e2 = jnp.exp(l0 - m), jnp.exp(l1 - m), jnp.exp(l2 - m)
    inv = 1.0 / (e0 + e1 + e2)
    return e0 * inv, e1 * inv, e2 * inv


def _sgu_forward(uvf, gs, wt_ref, bst_ref, mixed_s, tm):
    z, t = _gelu(uvf)
    u, v = z[:, :SGU_W], z[:, SGU_W:]
    rv = _rms(v)
    vnb = (v * rv * gs).astype(BF16)
    for ci in range(tm // 128):
        for g in range(4):
            rs, cs = slice(ci * 128, (ci + 1) * 128), slice(g * 128, (g + 1) * 128)
            mixed_s[rs, cs] = _dot(wt_ref[g], vnb[rs, cs]) + bst_ref[:, g:g + 1]
    return u, v, rv, vnb, t


def _fwd_mid(x, os_, ls_, uv, gl, wt, bst, g_sgu, b_gate, wt_ba, wt_bs, w_out, tm=256):
    S = x.shape[0]

    def body(x_ref, o0, o1, o2, l0, l1, l2, uv_ref, gl_ref, wt_ref, bst_ref, gs_ref, bg_ref, wba_ref, wbs_ref, wo_ref,
             ya_ref, ys_ref, ba_ref, bs_ref, mg_ref, h1_ref, mixed_s):
        a0, a1, a2 = _group_weights(l0[...], l1[...], l2[...])
        yab = (a0 * o0[...] + a1 * o1[...] + a2 * o2[...]).astype(BF16)
        ya_ref[...] = yab
        u, _, _, _, _ = _sgu_forward(uv_ref[...].astype(F32), gs_ref[...], wt_ref, bst_ref, mixed_s, tm)
        ysb = (u * mixed_s[...]).astype(BF16)
        ys_ref[...] = ysb
        gates = jax.nn.sigmoid(gl_ref[...].astype(F32) + bg_ref[...])
        ba = _dot_nt(yab, wba_ref[...])
        bs = _dot_nt(ysb, wbs_ref[...])
        ba_ref[...] = ba.astype(BF16)
        bs_ref[...] = bs.astype(BF16)
        mgb = (gates[:, :D] * ba + gates[:, D:] * bs).astype(BF16)
        mg_ref[...] = mgb
        h1_ref[...] = x_ref[...] + _dot(mgb, wo_ref[...])

    gw = _row(tm, GROUP_W)
    return pl.pallas_call(
        body, grid=(S // tm,), name="fwd_mid",
        in_specs=[_row(tm, D), gw, gw, gw, gw, gw, gw, _row(tm, 1024), _row(tm, 2048)] + [RES] * 7,
        out_specs=[gw, _row(tm, SGU_W), _row(tm, D), _row(tm, D), _row(tm, D), _row(tm, D)],
        out_shape=[_sds((S, GROUP_W), BF16), _sds((S, SGU_W), BF16), _sds((S, D), BF16), _sds((S, D), BF16),
                   _sds((S, D), BF16), _sds((S, D), F32)],
        scratch_shapes=[pltpu.VMEM((tm, SGU_W), F32)],
        compiler_params=_params(("parallel",), 48),
    )(x, *os_, *ls_, uv, gl, wt, bst, g_sgu, b_gate, wt_ba, wt_bs, w_out)


def _mem_fwd(mem, g_mem, w_kv):
    def body(m_ref, g_ref, w_ref, mb_ref, kv_ref):
        mv = m_ref[...]
        mb = (mv * _rms(mv) * g_ref[...]).astype(BF16)
        mb_ref[...] = mb
        kv_ref[...] = _dot(mb, w_ref[...]).astype(BF16)

    return pl.pallas_call(
        body, name="mem_fwd", in_specs=[RES, RES, RES], out_specs=[RES, RES],
        out_shape=[_sds(mem.shape, BF16), _sds((mem.shape[0], 2 * MEM_W), BF16)],
        compiler_params=pltpu.CompilerParams(vmem_limit_bytes=32 << 20),
    )(mem, g_mem, w_kv)


def _cross_probs(qh, kh):
    s = _dot_nt(qh, kh) * (MEM_HD ** -0.5)
    e = jnp.exp(s - jnp.max(s, axis=-1, keepdims=True))
    return e / jnp.sum(e, axis=-1, keepdims=True)


def _fwd_cross(h1, g_cross, w_q, kv, wt_o, tm=512):
    S = h1.shape[0]

    def body(h_ref, g_ref, wq_ref, kv_ref, wo_ref, c_ref, qc_ref, oc_ref, h2_ref):
        hv = h_ref[...]
        cb = (hv * _rms(hv) * g_ref[...]).astype(BF16)
        c_ref[...] = cb
        qcb = _dot(cb, wq_ref[...]).astype(BF16)
        qc_ref[...] = qcb
        for h in range(MEM_HEADS):
            cs = slice(h * MEM_HD, (h + 1) * MEM_HD)
            p = _cross_probs(qcb[:, cs], kv_ref[:, cs])
            oc_ref[:, cs] = _dot(p.astype(BF16), kv_ref[:, MEM_W + h * MEM_HD:MEM_W + (h + 1) * MEM_HD]).astype(BF16)
        h2_ref[...] = hv + _dot_nt(oc_ref[...], wo_ref[...])

    return pl.pallas_call(
        body, grid=(S // tm,), name="fwd_cross",
        in_specs=[_row(tm, D), RES, RES, RES, RES],
        out_specs=[_row(tm, D), _row(tm, MEM_W), _row(tm, MEM_W), _row(tm, D)],
        out_shape=[_sds((S, D), BF16), _sds((S, MEM_W), BF16), _sds((S, MEM_W), BF16), _sds((S, D), F32)],
        compiler_params=_params(("parallel",), 40),
    )(h1, g_cross, w_q, kv, wt_o)


def _ffn_fwd_bwd(h2, target, g_ffn, g_final, wt_gu, w_down, tm=256):
    S = h2.shape[0]
    nch = D_FF // FF_CHUNK

    def body(h_ref, t_ref, gf_ref, gz_ref, wgu_ref, wd_ref,
             f_ref, act_ref, dgu_ref, dh3b_ref, dh2_ref, dh2b_ref, dgf_ref, dgz_ref, loss_ref, gu_s):
        i = pl.program_id(0)
        hv = h_ref[...]
        r2 = _rms(hv)
        gf = gf_ref[...]
        fb = (hv * r2 * gf).astype(BF16)
        f_ref[...] = fb
        h3 = hv
        for c in range(nch):
            cs = slice(c * FF_CHUNK, (c + 1) * FF_CHUNK)
            us = slice(D_FF + c * FF_CHUNK, D_FF + (c + 1) * FF_CHUNK)
            gt = _dot_nt(fb, wgu_ref[cs, :])
            up = _dot_nt(fb, wgu_ref[us, :])
            gu_s[:, cs] = gt
            gu_s[:, us] = up
            actb = (gt * jax.nn.sigmoid(gt) * up).astype(BF16)
            act_ref[:, cs] = actb
            h3 = h3 + _dot(actb, wd_ref[cs, :])
        r3 = _rms(h3)
        gz = gz_ref[...]
        diff = h3 * r3 * gz - t_ref[...]
        dy = diff * (1.0 / D)
        dh3, dgz_rows = _rms_bwd(dy, h3, r3, gz)
        dh3b = dh3.astype(BF16)
        dh3b_ref[...] = dh3b
        df = jnp.zeros((tm, D), F32)
        for c in range(nch):
            cs = slice(c * FF_CHUNK, (c + 1) * FF_CHUNK)
            us = slice(D_FF + c * FF_CHUNK, D_FF + (c + 1) * FF_CHUNK)
            dact = _dot_nt(dh3b, wd_ref[cs, :])
            gt, up = gu_s[:, cs], gu_s[:, us]
            sg = jax.nn.sigmoid(gt)
            dgt = (dact * up * (sg * (1.0 + gt * (1.0 - sg)))).astype(BF16)
            dup = (dact * (gt * sg)).astype(BF16)
            dgu_ref[:, cs] = dgt
            dgu_ref[:, us] = dup
            df = df + _dot(dgt, wgu_ref[cs, :]) + _dot(dup, wgu_ref[us, :])
        dhn, dgf_rows = _rms_bwd(df, hv, r2, gf)
        dh2 = dh3 + dhn
        dh2_ref[...] = dh2
        dh2b_ref[...] = dh2.astype(BF16)

        @pl.when(i == 0)
        def _():
            dgf_ref[...] = jnp.zeros_like(dgf_ref)
            dgz_ref[...] = jnp.zeros_like(dgz_ref)
            loss_ref[...] = jnp.zeros_like(loss_ref)

        dgf_ref[...] += jnp.sum(dgf_rows, axis=0, keepdims=True)
        dgz_ref[...] += jnp.sum(dgz_rows, axis=0, keepdims=True)
        loss_ref[...] += jnp.sum(jnp.sum(diff * diff, axis=0, keepdims=True), axis=1, keepdims=True) * (0.5 / D)

    return pl.pallas_call(
        body, grid=(S // tm,), name="ffn_fwd_bwd",
        in_specs=[_row(tm, D), _row(tm, D), RES, RES, RES, RES],
        out_specs=[_row(tm, D), _row(tm, D_FF), _row(tm, 2 * D_FF), _row(tm, D), _row(tm, D), _row(tm, D),
                   _acc((1, D)), _acc((1, D)), _acc((1, 128))],
        out_shape=[_sds((S, D), BF16), _sds((S, D_FF), BF16), _sds((S, 2 * D_FF), BF16), _sds((S, D), BF16),
                   _sds((S, D), F32), _sds((S, D), BF16), _sds((1, D), F32), _sds((1, D), F32), _sds((1, 128), F32)],
        scratch_shapes=[pltpu.VMEM((tm, 2 * D_FF), F32)],
        compiler_params=_params(("arbitrary",), 56),
    )(h2, target, g_ffn, g_final, wt_gu, w_down)


def _bwd_cross(dh2, h1, qc, g_cross, w_q, kv, wt_o, tm=256):
    S = h1.shape[0]

    def body(d_ref, h_ref, qc_ref, g_ref, wq_ref, kv_ref, wo_ref, dqc_ref, dh1_ref, dh1b_ref, dkv_ref, dg_ref):
        i = pl.program_id(0)

        @pl.when(i == 0)
        def _():
            dkv_ref[...] = jnp.zeros_like(dkv_ref)
            dg_ref[...] = jnp.zeros_like(dg_ref)

        dh2 = d_ref[...]
        doc = _dot(dh2.astype(BF16), wo_ref[...])
        qcb = qc_ref[...]
        for h in range(MEM_HEADS):
            cs = slice(h * MEM_HD, (h + 1) * MEM_HD)
            vs = slice(MEM_W + h * MEM_HD, MEM_W + (h + 1) * MEM_HD)
            qh, kh, vh = qcb[:, cs], kv_ref[:, cs], kv_ref[:, vs]
            p = _cross_probs(qh, kh)
            dohb = doc[:, cs].astype(BF16)
            dp = _dot_nt(dohb, vh)
            dsb = (p * (dp - jnp.sum(dp * p, axis=-1, keepdims=True)) * (MEM_HD ** -0.5)).astype(BF16)
            dqc_ref[:, cs] = _dot(dsb, kh).astype(BF16)
            dkv_ref[:, cs] += _dot_tn(dsb, qh)
            dkv_ref[:, vs] += _dot_tn(p.astype(BF16), dohb)
        dc = _dot_nt(dqc_ref[...], wq_ref[...])
        hv = h_ref[...]
        dhn, dg_rows = _rms_bwd(dc, hv, _rms(hv), g_ref[...])
        dh1 = dh2 + dhn
        dh1_ref[...] = dh1
        dh1b_ref[...] = dh1.astype(BF16)
        dg_ref[...] += jnp.sum(dg_rows, axis=0, keepdims=True)

    return pl.pallas_call(
        body, grid=(S // tm,), name="bwd_cross",
        in_specs=[_row(tm, D), _row(tm, D), _row(tm, MEM_W), RES, RES, RES, RES],
        out_specs=[_row(tm, MEM_W), _row(tm, D), _row(tm, D), _acc((256, 2 * MEM_W)), _acc((1, D))],
        out_shape=[_sds((S, MEM_W), BF16), _sds((S, D), F32), _sds((S, D), BF16), _sds((256, 2 * MEM_W), F32),
                   _sds((1, D), F32)],
        compiler_params=_params(("arbitrary",), 40),
    )(dh2, h1, qc, g_cross, w_q, kv, wt_o)


def _mem_bwd(dkv, mem, mb, g_mem, w_kv):
    def body(dkv_ref, m_ref, mb_ref, g_ref, w_ref, dw_ref, dg_ref):
        dkvb = dkv_ref[...].astype(BF16)
        dw_ref[...] = _dot_tn(mb_ref[...], dkvb)
        dm = _dot_nt(dkvb, w_ref[...])
        mv = m_ref[...]
        dg_ref[...] = jnp.sum(dm * mv * _rms(mv), axis=0, keepdims=True)

    return pl.pallas_call(
        body, name="mem_bwd", in_specs=[RES] * 5, out_specs=[RES, RES],
        out_shape=[_sds((D, 2 * MEM_W), F32), _sds((1, D), F32)],
        compiler_params=pltpu.CompilerParams(vmem_limit_bytes=32 << 20),
    )(dkv, mem, mb, g_mem, w_kv)


def _bwd_mid(dh1, gl, ba, bs, uv, ls_, ya, wt, bst, g_sgu, b_gate, wt_ba, wt_bs, w_out, tm=256):
    S = dh1.shape[0]

    def body(d_ref, gl_ref, ba_ref, bs_ref, uv_ref, l0, l1, l2, ya_ref,
             wt_ref, bst_ref, gs_ref, bg_ref, wba_ref, wbs_ref, wo_ref,
             dba_ref, dbs_ref, dgl_ref, duv_ref, do0, do1, do2, c0, c1, c2,
             dbg_ref, dgs_ref, dws_ref, dbsa_ref, mixed_s, dvn_s):
        i = pl.program_id(0)

        @pl.when(i == 0)
        def _():
            for r in (dbg_ref, dgs_ref, dws_ref, dbsa_ref):
                r[...] = jnp.zeros_like(r)

        dm = _dot_nt(d_ref[...].astype(BF16), wo_ref[...])
        gates = jax.nn.sigmoid(gl_ref[...].astype(F32) + bg_ref[...])
        g0, g1 = gates[:, :D], gates[:, D:]
        dbab = (dm * g0).astype(BF16)
        dbsb = (dm * g1).astype(BF16)
        dba_ref[...] = dbab
        dbs_ref[...] = dbsb
        dg0 = dm * ba_ref[...].astype(F32) * g0 * (1.0 - g0)
        dg1 = dm * bs_ref[...].astype(F32) * g1 * (1.0 - g1)
        dgl_ref[:, :D] = dg0.astype(BF16)
        dgl_ref[:, D:] = dg1.astype(BF16)
        dbg_ref[:, :D] += jnp.sum(dg0, axis=0, keepdims=True)
        dbg_ref[:, D:] += jnp.sum(dg1, axis=0, keepdims=True)
        dya = _dot(dbab, wba_ref[...])
        dys = _dot(dbsb, wbs_ref[...])

        uvf = uv_ref[...].astype(F32)
        gs = gs_ref[...]
        u, v, rv, vnb, t = _sgu_forward(uvf, gs, wt_ref, bst_ref, mixed_s, tm)
        du = dys * mixed_s[...]
        dmixed = dys * u
        for ci in range(tm // 128):
            for g in range(4):
                rs, cs = slice(ci * 128, (ci + 1) * 128), slice(g * 128, (g + 1) * 128)
                dmx = dmixed[rs, cs]
                dmxb = dmx.astype(BF16)
                dvn_s[rs, cs] = _dot_tn(wt_ref[g], dmxb)
                dws_ref[g] += _dot_nt(dmxb, vnb[rs, cs])
                dbsa_ref[g] += dmx
        dv, dgs_rows = _rms_bwd(dvn_s[...], v, rv, gs)
        dgs_ref[...] += jnp.sum(dgs_rows, axis=0, keepdims=True)
        gg = _gelu_grad(uvf, t)
        duv_ref[:, :SGU_W] = (du * gg[:, :SGU_W]).astype(BF16)
        duv_ref[:, SGU_W:] = (dv * gg[:, SGU_W:]).astype(BF16)

        a0, a1, a2 = _group_weights(l0[...], l1[...], l2[...])
        prod = dya * ya_ref[...].astype(F32)
        _, masks = _head_masks()
        hs = jnp.zeros_like(prod)
        for h in range(4):
            sh = jnp.sum(jnp.where(masks[h], prod, 0.0), axis=-1, keepdims=True)
            hs = jnp.where(masks[h], sh, hs)
        for a, do_ref, c_ref in ((a0, do0, c0), (a1, do1, c1), (a2, do2, c2)):
            do_ref[...] = (a * dya).astype(BF16)
            c_ref[...] = a * hs

    gw = _row(tm, GROUP_W)
    return pl.pallas_call(
        body, grid=(S // tm,), name="bwd_mid",
        in_specs=[_row(tm, D), _row(tm, 2048), _row(tm, D), _row(tm, D), _row(tm, 1024), gw, gw, gw, gw] + [RES] * 7,
        out_specs=[_row(tm, D), _row(tm, D), _row(tm, 2048), _row(tm, 1024), gw, gw, gw, gw, gw, gw,
                   _acc((1, 2048)), _acc((1, SGU_W)), _acc((4, 128, 128)), _acc((4, 128, 128))],
        out_shape=[_sds((S, D), BF16), _sds((S, D), BF16), _sds((S, 2048), BF16), _sds((S, 1024), BF16)]
        + [_sds((S, GROUP_W), BF16)] * 3 + [_sds((S, GROUP_W), F32)] * 3
        + [_sds((1, 2048), F32), _sds((1, SGU_W), F32), _sds((4, 128, 128), F32), _sds((4, 128, 128), F32)],
        scratch_shapes=[pltpu.VMEM((tm, SGU_W), F32), pltpu.VMEM((tm, SGU_W), F32)],
        compiler_params=_params(("arbitrary",), 48),
    )(dh1, gl, ba, bs, uv, *ls_, ya, wt, bst, g_sgu, b_gate, wt_ba, wt_bs, w_out)


def _attn_bwd(qkv, do, lse, corr, g, prev):
    S = qkv.shape[0]
    d = DIL_GROUPS[g][1]
    L = S // d
    nb = L // BLK
    qv = qkv.reshape(L, d * 2304)
    view = lambda t: t.reshape(L, d * GROUP_W)
    bias = jnp.asarray(_attn_bias(g))
    n_prev = 0 if prev is None else 3

    def body(*refs):
        q_ref, kc_ref, kp_ref, vc_ref, vp_ref, do_ref, l_ref, c_ref, b_ref = refs[:9]
        dq_ref, dk_ref, dv_ref, ck_s, cv_s = refs[9 + n_prev:]
        j = pl.program_id(1)

        @pl.when(j == 0)
        def _():
            ck_s[...] = jnp.zeros_like(ck_s)
            cv_s[...] = jnp.zeros_like(cv_s)

        @pl.when(j < nb)
        def _():
            q = q_ref[...]
            dob = do_ref[...]
            kk = jnp.concatenate([kp_ref[...], kc_ref[...]], axis=0)
            vv = jnp.concatenate([vp_ref[...], vc_ref[...]], axis=0)
            lane, masks = _head_masks()
            extra = jnp.where((j == 0) & (lane < BLK), NEG, 0.0).astype(F32)
            dq = jnp.zeros((BLK, GROUP_W), F32)
            dkk = jnp.zeros((2 * BLK, GROUP_W), F32)
            dvv = jnp.zeros((2 * BLK, GROUP_W), F32)
            for h in range(4):
                mh = masks[h]
                qh = jnp.where(mh, q, jnp.zeros_like(q))
                doh = jnp.where(mh, dob, jnp.zeros_like(dob))
                s = _dot_nt(qh, kk) * 0.125 + b_ref[h] + extra
                p = jnp.exp(s - l_ref[:, HEAD * h:HEAD * h + 1])
                dp = _dot_nt(doh, vv)
                dsb = (p * (dp - c_ref[:, HEAD * h:HEAD * h + 1]) * 0.125).astype(BF16)
                dq = jnp.where(mh, _dot(dsb, kk), dq)
                dkk = jnp.where(mh, _dot_tn(dsb, q), dkk)
                dvv = jnp.where(mh, _dot_tn(p.astype(BF16), dob), dvv)
            dq_ref[...] = dq.astype(BF16)
            dk_ref[...] = (ck_s[...] + dkk[:BLK]).astype(BF16)
            dv_ref[...] = (cv_s[...] + dvv[:BLK]).astype(BF16)
            ck_s[...] = dkk[BLK:]
            cv_s[...] = dvv[BLK:]

        @pl.when(j == nb)
        def _():
            dk_ref[...] = ck_s[...].astype(BF16)
            dv_ref[...] = cv_s[...].astype(BF16)

    def blk(f):
        return pl.BlockSpec((BLK, GROUP_W), f)

    qn = lambda j: jnp.minimum(j, nb - 1)
    pn = lambda j: jnp.maximum(jnp.minimum(j, nb - 1) - 1, 0)
    kn = lambda j: jnp.maximum(j - 1, 0)
    ins = [qv, qv, qv, qv, qv, view(do), view(lse), view(corr), bias]
    in_specs = [blk(lambda r, j: (qn(j), 9 * r + g)),
                blk(lambda r, j: (qn(j), 9 * r + 3 + g)), blk(lambda r, j: (pn(j), 9 * r + 3 + g)),
                blk(lambda r, j: (qn(j), 9 * r + 6 + g)), blk(lambda r, j: (pn(j), 9 * r + 6 + g)),
                blk(lambda r, j: (qn(j), r)), blk(lambda r, j: (qn(j), r)), blk(lambda r, j: (qn(j), r)),
                pl.BlockSpec((4, BLK, 2 * BLK), lambda r, j: (0, 0, 0))]
    aliases = {}
    if prev is not None:
        ins += [t.reshape(L, d * 768) for t in prev]
        in_specs += [ANY] * 3
        aliases = {9: 0, 10: 1, 11: 2}
    outs = pl.pallas_call(
        body, grid=(d, nb + 1), name=f"attn_bwd_g{g}",
        in_specs=in_specs,
        out_specs=[blk(lambda r, j: (qn(j), 3 * r + g)), blk(lambda r, j: (kn(j), 3 * r + g)),
                   blk(lambda r, j: (kn(j), 3 * r + g))],
        out_shape=[_sds((L, d * 768), BF16)] * 3,
        scratch_shapes=[pltpu.VMEM((BLK, GROUP_W), F32), pltpu.VMEM((BLK, GROUP_W), F32)],
        input_output_aliases=aliases,
        compiler_params=_params(("arbitrary", "arbitrary"), 32),
    )(*ins)
    return [t.reshape(S, 768) for t in outs]


def _bwd_in(dq, dk, dv, duv, dgl, dh1, x, g_mix, wt_in, tm=512):
    S = x.shape[0]

    def body(dq_ref, dk_ref, dv_ref, duv_ref, dgl_ref, d_ref, x_ref, g_ref, w_ref, dx_ref, dg_ref):
        i = pl.program_id(0)

        @pl.when(i == 0)
        def _():
            dg_ref[...] = jnp.zeros_like(dg_ref)

        da = (_dot(dq_ref[...], w_ref[0:768, :]) + _dot(dk_ref[...], w_ref[768:1536, :])
              + _dot(dv_ref[...], w_ref[1536:2304, :]) + _dot(duv_ref[...], w_ref[2304:3328, :])
              + _dot(dgl_ref[...], w_ref[3328:5376, :]))
        xv = x_ref[...]
        dxn, dg_rows = _rms_bwd(da, xv, _rms(xv), g_ref[...])
        dx_ref[...] = d_ref[...] + dxn
        dg_ref[...] += jnp.sum(dg_rows, axis=0, keepdims=True)

    return pl.pallas_call(
        body, grid=(S // tm,), name="bwd_in",
        in_specs=[_row(tm, 768), _row(tm, 768), _row(tm, 768), _row(tm, 1024), _row(tm, 2048), _row(tm, D), _row(tm, D),
                  RES, RES],
        out_specs=[_row(tm, D), _acc((1, D))],
        out_shape=[_sds((S, D), F32), _sds((1, D), F32)],
        compiler_params=_params(("arbitrary",), 56),
    )(dq, dk, dv, duv, dgl, dh1, x, g_mix, wt_in)


def _tn_matmul(a, b, name, tk, ts=512, into=None):
    S, K = a.shape
    N = b.shape[1]
    off = 0 if into is None else into[1] // tk

    def body(*refs):
        a_ref, b_ref, o_ref = refs[0], refs[1], refs[-1]

        @pl.when(pl.program_id(1) == 0)
        def _():
            o_ref[...] = jnp.zeros_like(o_ref)

        o_ref[...] += _dot_tn(a_ref[...], b_ref[...])

    ins = [a, b]
    in_specs = [pl.BlockSpec((ts, tk), lambda k, s: (s, k)), pl.BlockSpec((ts, N), lambda k, s: (s, 0))]
    out_shape = _sds((K, N), F32)
    aliases = {}
    if into is not None:
        ins.append(into[0])
        in_specs.append(ANY)
        out_shape = _sds(into[0].shape, F32)
        aliases = {2: 0}
    return pl.pallas_call(
        body, grid=(K // tk, S // ts), name=name, in_specs=in_specs,
        out_specs=pl.BlockSpec((tk, N), lambda k, s: (k + off, 0)), out_shape=out_shape,
        input_output_aliases=aliases,
        compiler_params=_params(("parallel", "arbitrary"), 48),
    )(*ins)


def _chip_peers(x, y):
    return [(1 - x, y), (x, 1 - y), (1 - x, 1 - y)]


STAGE_BYTES = 2 << 20


def _chunk_plan(shapes, itemsize):
    plan = []
    for i, (rows, w) in enumerate(shapes):
        ch = max(16, min(rows, (STAGE_BYTES // (w * itemsize)) // 16 * 16))
        while rows % ch:
            ch -= 16
        plan += [(i, r0, ch) for r0 in range(0, rows, ch)]
    return plan


def _remote(src, dst, ssem, rsem, dev):
    return pltpu.make_async_remote_copy(src_ref=src, dst_ref=dst, send_sem=ssem, recv_sem=rsem, device_id=dev,
                                        device_id_type=MESH)


def _gather_weights(shards):
    n = len(shards)
    halves = [s.reshape(2, s.shape[0] // 2, s.shape[1]) for s in shards]
    plan = _chunk_plan([h.shape[1:] for h in halves], 2)

    def body(*refs):
        ins, outs = refs[:n], refs[n:2 * n]
        ssem, rsem, lsem, osem, buf = refs[2 * n:]
        x, y, c = lax.axis_index("x"), lax.axis_index("y"), lax.axis_index("c")
        me = 2 * x + y
        chips = _chip_peers(x, y)
        sib = (x, y, 1 - c)
        sends = []
        for i in range(n):
            for k, (px, py) in enumerate(chips):
                cp = _remote(ins[i].at[c], outs[i].at[me, c], ssem.at[6 * i + k], rsem.at[6 * i + k], (px, py, c))
                cp.start()
                sends.append(cp)
        pending = {}
        for k, (i, r0, ch) in enumerate(plan):
            for h in range(2):
                slot = (2 * k + h) % 2
                if slot in pending:
                    pending[slot].wait()
                stage = buf.at[slot, pl.ds(0, ch), pl.ds(0, halves[i].shape[2])]
                ld = pltpu.make_async_copy(ins[i].at[h, pl.ds(r0, ch)], stage, lsem.at[slot])
                ld.start()
                ld.wait()
                st = pltpu.make_async_copy(stage, outs[i].at[me, h, pl.ds(r0, ch)], osem.at[slot])
                st.start()
                pending[slot] = st
        for st in pending.values():
            st.wait()
        for i in range(n):
            for k, (px, py) in enumerate(chips):
                landed = outs[i].at[2 * px + py, c]
                _remote(landed, landed, ssem.at[6 * i + k], rsem.at[6 * i + k], (px, py, c)).wait_recv()
                cp = _remote(landed, landed, ssem.at[6 * i + 3 + k], rsem.at[6 * i + 3 + k], sib)
                cp.start()
                sends.append(cp)
        for i in range(n):
            for k, (px, py) in enumerate(chips):
                passed = outs[i].at[2 * px + py, 1 - c]
                _remote(passed, passed, ssem.at[6 * i + 3 + k], rsem.at[6 * i + 3 + k], sib).wait_recv()
        for cp in sends:
            cp.wait_send()

    outs = pl.pallas_call(
        body, name="gather_weights", in_specs=[ANY] * n, out_specs=[ANY] * n,
        out_shape=[_sds((4,) + h.shape, BF16) for h in halves],
        scratch_shapes=[pltpu.SemaphoreType.DMA((6 * n,)), pltpu.SemaphoreType.DMA((6 * n,)),
                        pltpu.SemaphoreType.DMA((2,)), pltpu.SemaphoreType.DMA((2,)),
                        pltpu.VMEM((2, max(p[2] for p in plan), max(h.shape[2] for h in halves)), BF16)],
        compiler_params=pltpu.CompilerParams(vmem_limit_bytes=32 << 20),
    )(*halves)
    return [o.reshape(4 * s.shape[0], s.shape[1]) for o, s in zip(outs, shards)]


def _swap_halves(grads):
    n = len(grads)
    g4 = [g.reshape(4, 2, g.shape[0] // 8, g.shape[1]) for g in grads]

    def body(*refs):
        ins, got = refs[:n], refs[n:2 * n]
        ssem, rsem = refs[2 * n:]
        x, y, c = lax.axis_index("x"), lax.axis_index("y"), lax.axis_index("c")
        sib = (x, y, 1 - c)
        cps = []
        for i in range(n):
            rc = _remote(ins[i].at[:, 1 - c], got[i], ssem.at[i], rsem.at[i], sib)
            rc.start()
            cps.append(rc)
        for cp in cps:
            cp.wait()

    half = [_sds((4, g.shape[2], g.shape[3]), F32) for g in g4]
    got = pl.pallas_call(
        body, name="swap_halves", in_specs=[ANY] * n, out_specs=[ANY] * n, out_shape=half,
        scratch_shapes=[pltpu.SemaphoreType.DMA((n,)), pltpu.SemaphoreType.DMA((n,))],
    )(*g4)
    c = lax.axis_index("c")
    own = [lax.dynamic_index_in_dim(g, c, axis=1, keepdims=False) for g in g4]
    return own, got


def _scatter_partials(sums, sums_b):
    n = len(sums)

    def body(*refs):
        hbs, parts = refs[:n], refs[n:2 * n]
        ssem, rsem = refs[2 * n:]
        x, y, c = lax.axis_index("x"), lax.axis_index("y"), lax.axis_index("c")
        cps = []
        for i in range(n):
            for k, (px, py) in enumerate(_chip_peers(x, y)):
                rc = _remote(hbs[i].at[2 * px + py], parts[i].at[k], ssem.at[3 * i + k], rsem.at[3 * i + k], (px, py, c))
                rc.start()
                cps.append(rc)
        for cp in cps:
            cp.wait()

    parts = pl.pallas_call(
        body, name="scatter_partials", in_specs=[ANY] * n, out_specs=[ANY] * n,
        out_shape=[_sds((3,) + s.shape[1:], BF16) for s in sums],
        scratch_shapes=[pltpu.SemaphoreType.DMA((3 * n,)), pltpu.SemaphoreType.DMA((3 * n,))],
    )(*sums_b)
    me = 2 * lax.axis_index("x") + lax.axis_index("y")
    mine = [lax.dynamic_index_in_dim(s, me, axis=0, keepdims=False) for s in sums]
    return mine, parts


def _share_halves(reduced):
    n = len(reduced)
    plan = _chunk_plan([r.shape for r in reduced], 4)
    max_rows = max(p[2] for p in plan)
    max_w = max(r.shape[1] for r in reduced)

    def body(*refs):
        ins, outs = refs[:n], refs[n:2 * n]
        ssem, rsem, lsem, osem, buf = refs[2 * n:]
        x, y, c = lax.axis_index("x"), lax.axis_index("y"), lax.axis_index("c")
        sib = (x, y, 1 - c)
        pending = {}
        for k, (i, r0, ch) in enumerate(plan):
            slot = k % 2
            if slot in pending:
                rc, lc = pending[slot]
                rc.wait_send()
                lc.wait()
            stage = buf.at[slot, pl.ds(0, ch), pl.ds(0, reduced[i].shape[1])]
            ld = pltpu.make_async_copy(ins[i].at[pl.ds(r0, ch)], stage, lsem.at[slot])
            ld.start()
            ld.wait()
            place = outs[i].at[c, pl.ds(r0, ch)]
            rc = _remote(stage, place, ssem.at[slot], rsem.at[i], sib)
            lc = pltpu.make_async_copy(stage, place, osem.at[slot])
            rc.start()
            lc.start()
            pending[slot] = (rc, lc)
        for rc, lc in pending.values():
            rc.wait_send()
            lc.wait()
        for i in range(n):
            theirs = outs[i].at[1 - c]
            _remote(theirs, theirs, ssem.at[0], rsem.at[i], sib).wait_recv()

    outs = pl.pallas_call(
        body, name="share_halves", in_specs=[ANY] * n, out_specs=[ANY] * n,
        out_shape=[_sds((2,) + r.shape, F32) for r in reduced],
        scratch_shapes=[pltpu.SemaphoreType.DMA((2,)), pltpu.SemaphoreType.DMA((n,)), pltpu.SemaphoreType.DMA((2,)),
                        pltpu.SemaphoreType.DMA((2,)), pltpu.VMEM((2, max_rows, max_w), F32)],
        compiler_params=pltpu.CompilerParams(vmem_limit_bytes=32 << 20),
    )(*reduced)
    return [o.reshape(2 * r.shape[0], r.shape[1]) for o, r in zip(outs, reduced)]


def _gather_small(pack):
    P = pack.shape[0]

    def body(in_ref, out_ref, ssem, rsem, lsem):
        x, y, c = lax.axis_index("x"), lax.axis_index("y"), lax.axis_index("c")
        me = 4 * x + 2 * y + c
        lc = pltpu.make_async_copy(in_ref, out_ref.at[me], lsem)
        lc.start()
        cps = []
        for rel in range(1, 8):
            px = 1 - x if rel & 4 else x
            py = 1 - y if rel & 2 else y
            pc = 1 - c if rel & 1 else c
            rc = _remote(in_ref, out_ref.at[me], ssem.at[rel - 1], rsem.at[rel - 1], (px, py, pc))
            rc.start()
            cps.append((rc, 4 * px + 2 * py + pc))
        for rel, (rc, peer) in enumerate(cps):
            rc.wait_send()
            _remote(in_ref, out_ref.at[peer], ssem.at[rel], rsem.at[rel], (x, y, c)).wait_recv()
        lc.wait()

    return pl.pallas_call(
        body, name="gather_small", in_specs=[ANY], out_specs=ANY, out_shape=_sds((8, P, 128), F32),
        scratch_shapes=[pltpu.SemaphoreType.DMA((7,)), pltpu.SemaphoreType.DMA((7,)), pltpu.SemaphoreType.DMA],
    )(pack)


def _tile(rows, cap=256):
    t = min(rows, cap) // 16 * 16
    while rows % t:
        t -= 16
    return t


def _elementwise(fn, ins, out_dtypes, name):
    R, W = ins[0].shape
    tr = _tile(R, max(8, min(512, (1 << 18) // W // 8 * 8)))

    def body(*refs):
        outs = fn(*[r[...] for r in refs[:len(ins)]])
        for o_ref, o in zip(refs[len(ins):], outs):
            o_ref[...] = o.astype(o_ref.dtype)

    return pl.pallas_call(
        body, grid=(R // tr,), name=name, in_specs=[_row(tr, W)] * len(ins), out_specs=[_row(tr, W)] * len(out_dtypes),
        out_shape=[_sds((R, W), dt) for dt in out_dtypes],
        compiler_params=_params(("parallel",), 48),
    )(*ins)


def _adamw(w, g, m, v):
    m = B1 * m + (1.0 - B1) * g
    v = B2 * v + (1.0 - B2) * (g * g)
    m_hat = m / (1.0 - B1 ** STEP)
    v_hat = v / (1.0 - B2 ** STEP)
    return -LR * (m_hat / (jnp.sqrt(v_hat) + AEPS) + WD * w), m, v


def _adam_small(w, m, v, packs):
    def body(w_ref, m_ref, v_ref, p_ref, g_ref, d_ref, nm_ref, nv_ref):
        g = p_ref[0]
        for k in range(1, 8):
            g = g + p_ref[k]
        g_ref[...] = g
        d_ref[...], nm_ref[...], nv_ref[...] = _adamw(w_ref[...], g, m_ref[...], v_ref[...])

    return pl.pallas_call(
        body, name="adam_small", in_specs=[RES] * 4, out_specs=[RES] * 4, out_shape=[_sds(w.shape, F32)] * 4,
        compiler_params=pltpu.CompilerParams(vmem_limit_bytes=32 << 20),
    )(w, m, v, packs)


def _local_step(xs, tgt, mems, weights, small):
    wt_in, wt_ba, wt_bs, wo, wq, wkv, wt_o, wt_gu, wd = weights
    g_mix, b_gate, w_sgu, b_sgu, g_sgu, g_cross, g_mem, g_ffn, g_final = small
    wt = jnp.tril(w_sgu).astype(BF16)
    bst = b_sgu.T

    a, qkv, uv, gl = _fwd_in(xs, g_mix, wt_in)
    os_, ls_ = zip(*[_attn_fwd(qkv, g) for g in range(3)])
    ya, ys, ba, bs, mg, h1 = _fwd_mid(xs, os_, ls_, uv, gl, wt, bst, g_sgu, b_gate, wt_ba, wt_bs, wo)
    mb, kv = _mem_fwd(mems, g_mem, wkv)
    cb, qc, oc, h2 = _fwd_cross(h1, g_cross, wq, kv, wt_o)
    f, act, dgu, dh3b, dh2, dh2b, dg_ffn, dg_final, loss = _ffn_fwd_bwd(h2, tgt, g_ffn, g_final, wt_gu, wd)

    dqc, dh1, dh1b, dkv, dg_cross = _bwd_cross(dh2, h1, qc, g_cross, wq, kv, wt_o)
    dw_kv, dg_mem = _mem_bwd(dkv, mems, mb, g_mem, wkv)
    (dba, dbs, dgl, duv, do0, do1, do2, c0, c1, c2, db_gate, dg_sgu, dws, dbs_acc) = _bwd_mid(
        dh1, gl, ba, bs, uv, ls_, ya, wt, bst, g_sgu, b_gate, wt_ba, wt_bs, wo)
    dqkv = None
    for g, (do, corr) in enumerate(((do0, c0), (do1, c1), (do2, c2))):
        dqkv = _attn_bwd(qkv, do, ls_[g], corr, g, dqkv)
    dq, dk, dv = dqkv
    grad_x, dg_mix = _bwd_in(dq, dk, dv, duv, dgl, dh1, xs, g_mix, wt_in)

    dwt_in = _tn_matmul(dq, a, "dw_in_q", 256)
    dwt_in = jnp.concatenate([dwt_in, _tn_matmul(dk, a, "dw_in_k", 256), _tn_matmul(dv, a, "dw_in_v", 256),
                              _tn_matmul(duv, a, "dw_in_uv", 512), _tn_matmul(dgl, a, "dw_in_gl", 512)], axis=0)
    grads = [dwt_in,
             _tn_matmul(dba, ya, "dw_branch_attn", 512),
             _tn_matmul(dbs, ys, "dw_branch_sgu", 512),
             _tn_matmul(mg, dh1b, "dw_out", 512),
             _tn_matmul(cb, dqc, "dw_q_cross", 512),
             dw_kv,
             _tn_matmul(dh2b, oc, "dw_o_cross", 512),
             _tn_matmul(dgu, f, "dw_gate_up", 512),
             _tn_matmul(act, dh3b, "dw_down", 256)]

    return loss, grad_x, grads, (dg_mix, db_gate, dws, dbs_acc, dg_sgu, dg_cross, dg_mem, dg_ffn, dg_final)


def kernel(x, mem, g_mix, w_in, b_gate, w_sgu_spatial, b_sgu_spatial, g_sgu, w_branch_attn, w_branch_sgu, w_out, g_cross, g_mem, w_q_cross, w_kv_cross, w_o_cross, g_ffn, w_gate_up, w_down, g_final, loss_target, m_g_mix, m_w_in, m_b_gate, m_w_sgu_spatial, m_b_sgu_spatial, m_g_sgu, m_w_branch_attn, m_w_branch_sgu, m_w_out, m_g_cross, m_g_mem, m_w_q_cross, m_w_kv_cross, m_w_o_cross, m_g_ffn, m_w_gate_up, m_w_down, m_g_final, v_g_mix, v_w_in, v_b_gate, v_w_sgu_spatial, v_b_sgu_spatial, v_g_sgu, v_w_branch_attn, v_w_branch_sgu, v_w_out, v_g_cross, v_g_mem, v_w_q_cross, v_w_kv_cross, v_w_o_cross, v_g_ffn, v_w_gate_up, v_w_down, v_g_final):
    S = x.shape[1]
    xs, tgt, mems = x.reshape(S, D), loss_target.reshape(S, D), mem.reshape(mem.shape[1], D)
    g_final2 = g_final.reshape(1, D)

    big = [("w_in", w_in[0], m_w_in[0], v_w_in[0], True),
           ("w_branch_attn", w_branch_attn[0], m_w_branch_attn[0], v_w_branch_attn[0], True),
           ("w_branch_sgu", w_branch_sgu[0], m_w_branch_sgu[0], v_w_branch_sgu[0], True),
           ("w_out", w_out[0], m_w_out[0], v_w_out[0], False),
           ("w_q_cross", w_q_cross[0], m_w_q_cross[0], v_w_q_cross[0], False),
           ("w_kv_cross", w_kv_cross[0], m_w_kv_cross[0], v_w_kv_cross[0], False),
           ("w_o_cross", w_o_cross[0], m_w_o_cross[0], v_w_o_cross[0], True),
           ("w_gate_up", w_gate_up[0], m_w_gate_up[0], v_w_gate_up[0], True),
           ("w_down", w_down[0], m_w_down[0], v_w_down[0], False)]
    shards = [(w.T if tr else w).astype(BF16) for _, w, _, _, tr in big]
    wt_in, wt_ba, wt_bs, wo, wq, wkv, wt_o, wt_gu, wd = _gather_weights(shards)

    (loss, grad_x, grads, (dg_mix, db_gate, dws, dbs_acc, dg_sgu, dg_cross, dg_mem, dg_ffn, dg_final)) = _local_step(
        xs, tgt, mems, (wt_in, wt_ba, wt_bs, wo, wq, wkv, wt_o, wt_gu, wd),
        (g_mix, b_gate, w_sgu_spatial[0], b_sgu_spatial[0], g_sgu, g_cross, g_mem, g_ffn, g_final2))

    own, got = _swap_halves(grads)
    sums, sums_b = [], []
    for i, (o, t) in enumerate(zip(own, got)):
        shp = o.shape
        s_, sb_ = _elementwise(lambda p, q: (p + q, p + q), [o.reshape(-1, shp[2]), t.reshape(-1, shp[2])],
                               [F32, BF16], f"chip_sum_{big[i][0]}")
        sums.append(s_.reshape(shp))
        sums_b.append(sb_.reshape(shp))
    mine, parts = _scatter_partials(sums, sums_b)
    reduced = []
    for i, (o, p) in enumerate(zip(mine, parts)):
        reduced.append(_elementwise(lambda p0, p1, p2, p3: (p0 + p1.astype(F32) + p2.astype(F32) + p3.astype(F32),),
                                    [o.astype(F32)] + [p[k] for k in range(3)], [F32], f"mesh_sum_{big[i][0]}")[0])
    full = _share_halves(reduced)

    big_out = {}
    for (name, w, m, v, tr), gsh in zip(big, full):
        gsh = gsh.T if tr else gsh
        delta, nm, nv = _elementwise(_adamw, [w, gsh, m, v], [F32, F32, F32], f"adam_{name}")
        big_out[name] = tuple(t[None] for t in (gsh, delta, nm, nv))

    small = [("g_mix", g_mix, m_g_mix, v_g_mix, dg_mix), ("b_gate", b_gate, m_b_gate, v_b_gate, db_gate),
             ("w_sgu_spatial", w_sgu_spatial, m_w_sgu_spatial, v_w_sgu_spatial, jnp.tril(dws)),
             ("b_sgu_spatial", b_sgu_spatial, m_b_sgu_spatial, v_b_sgu_spatial, jnp.sum(dbs_acc, axis=-1)),
             ("g_sgu", g_sgu, m_g_sgu, v_g_sgu, dg_sgu), ("g_cross", g_cross, m_g_cross, v_g_cross, dg_cross),
             ("g_mem", g_mem, m_g_mem, v_g_mem, dg_mem), ("g_ffn", g_ffn, m_g_ffn, v_g_ffn, dg_ffn),
             ("g_final", g_final, m_g_final, v_g_final, dg_final)]

    def pack(parts_, tail):
        return jnp.concatenate([p.reshape(-1) for p in parts_] + [tail]).reshape(-1, 128)

    zeros = jnp.zeros((1024,), F32)
    gp = pack([s[4] for s in small], jnp.pad(loss.reshape(-1)[:1], (0, 1023)))
    wp, mp, vp = (pack([s[k] for s in small], zeros) for k in (1, 2, 3))
    gsum, dsm, nms, nvs = _adam_small(wp, mp, vp, _gather_small(gp))
    small_out, off = {}, 0
    for name, w, _, _, _ in small:
        n = w.size
        small_out[name] = tuple(t.reshape(-1)[off:off + n].reshape(w.shape) for t in (gsum, dsm, nms, nvs))
        off += n
    total_loss = gsum.reshape(-1)[off]

    order = ["g_mix", "w_in", "b_gate", "w_sgu_spatial", "b_sgu_spatial", "g_sgu", "w_branch_attn", "w_branch_sgu",
             "w_out", "g_cross", "g_mem", "w_q_cross", "w_kv_cross", "w_o_cross", "g_ffn", "w_gate_up", "w_down",
             "g_final"]
    res = {**big_out, **small_out}
    outs = [total_loss, grad_x.reshape(x.shape)]
    for k in range(4):
        outs += [res[nm][k] for nm in order]
    return tuple(outs)
```

```python
import math

import numpy as np
import jax
import jax.numpy as jnp
from jax import lax
from jax.experimental import pallas as pl
from jax.experimental.pallas import tpu as pltpu

F32, BF16 = jnp.float32, jnp.bfloat16
MESH = pl.DeviceIdType.MESH
ANY = pl.BlockSpec(memory_space=pl.ANY)
RES = pl.BlockSpec(memory_space=pltpu.VMEM)

D = 1024
HEAD = 64
GROUP_W = 256
DIL_GROUPS = ((128, 1), (512, 4), (2048, 16))
BLK = 128
SGU_W = 512
MEM_HEADS, MEM_HD, MEM_W = 4, 128, 512
D_FF = 2816
FF_CHUNK = 256
EPS = 1e-6
NEG = -1e30
LR, B1, B2, AEPS, WD, STEP = 0.001, 0.9, 0.999, 1e-08, 0.01, 10
GELU_K, GELU_C = 0.7978845608028654, 0.044715


def _dot(a, b):
    return jnp.dot(a, b, preferred_element_type=F32)


def _dot_nt(a, b):
    return lax.dot_general(a, b, (((1,), (1,)), ((), ())), preferred_element_type=F32)


def _dot_tn(a, b):
    return lax.dot_general(a, b, (((0,), (0,)), ((), ())), preferred_element_type=F32)


def _row(tm, w):
    return pl.BlockSpec((tm, w), lambda i: (i, 0))


def _acc(shape):
    return pl.BlockSpec(shape, lambda i: (0,) * len(shape))


def _params(sem, mb):
    return pltpu.CompilerParams(dimension_semantics=sem, vmem_limit_bytes=mb << 20)


def _sds(shape, dt):
    return jax.ShapeDtypeStruct(shape, dt)


def _rms(h):
    return lax.rsqrt(jnp.mean(h * h, axis=-1, keepdims=True) + EPS)


def _rms_bwd(dy, h, r, g):
    t = dy * g
    dh = r * t - h * (r * r * r) * jnp.mean(t * h, axis=-1, keepdims=True)
    return dh, dy * h * r


def _gelu(x):
    t = jnp.tanh(GELU_K * (x + GELU_C * x * x * x))
    return 0.5 * x * (1.0 + t), t


def _gelu_grad(x, t):
    return 0.5 * (1.0 + t) + 0.5 * x * (1.0 - t * t) * GELU_K * (1.0 + 3.0 * GELU_C * x * x)


def _alibi_slopes():
    def pow2(n):
        start = 2.0 ** (-8.0 / n)
        return [start ** (i + 1) for i in range(n)]
    n = 12
    c = 2 ** int(math.floor(math.log2(n)))
    s = pow2(c) + pow2(2 * c)[0::2][: n - c]
    return np.array(sorted(s, reverse=True), dtype=np.float32).reshape(3, 4)


def _attn_bias(g):
    win, dil = DIL_GROUPS[g]
    steps = (np.arange(BLK)[:, None] + BLK) - np.arange(2 * BLK)[None, :]
    valid = (steps >= 0) & (steps <= win // dil)
    dist = (np.clip(steps, 0, None) * dil).astype(np.float32)
    b = -_alibi_slopes()[g][:, None, None] * dist[None]
    return np.where(valid[None], b, NEG).astype(np.float32)


def _head_masks():
    lane = lax.broadcasted_iota(jnp.int32, (1, GROUP_W), 1)
    return lane, [(lane >= HEAD * h) & (lane < HEAD * (h + 1)) for h in range(4)]


ATT_NB = 4


def _stack_heads(t, masks):
    z = jnp.zeros_like(t)
    return jnp.concatenate([jnp.where(m, t, z) for m in masks], axis=0)


def _unstack_heads(t, masks):
    out = jnp.zeros((BLK, GROUP_W), t.dtype)
    for h, m in enumerate(masks):
        out = jnp.where(m, t[h * BLK:(h + 1) * BLK], out)
    return out


def _stack_cols(ref, rows):
    return jnp.concatenate([ref[rows, HEAD * h:HEAD * h + 1] for h in range(4)], axis=0)


def _fwd_in(x, g_mix, wt_in, tm=512):
    S = x.shape[0]

    def body(x_ref, g_ref, w_ref, a_ref, qkv_ref, uv_ref, gl_ref):
        xv = x_ref[...]
        a = (xv * _rms(xv) * g_ref[...]).astype(BF16)
        a_ref[...] = a
        for j in range(9):
            qkv_ref[:, j * 256:(j + 1) * 256] = _dot_nt(a, w_ref[j * 256:(j + 1) * 256, :]).astype(BF16)
        uv_ref[...] = _dot_nt(a, w_ref[2304:3328, :]).astype(BF16)
        gl_ref[...] = _dot_nt(a, w_ref[3328:5376, :]).astype(BF16)

    return pl.pallas_call(
        body, grid=(S // tm,), name="fwd_in",
        in_specs=[_row(tm, D), RES, RES],
        out_specs=[_row(tm, D), _row(tm, 2304), _row(tm, 1024), _row(tm, 2048)],
        out_shape=[_sds((S, D), BF16), _sds((S, 2304), BF16), _sds((S, 1024), BF16), _sds((S, 2048), BF16)],
        compiler_params=_params(("parallel",), 56),
    )(x, g_mix, wt_in)


def _attn_fwd(qkv, g):
    S = qkv.shape[0]
    d = DIL_GROUPS[g][1]
    L = S // d
    nb = L // BLK
    qv = qkv.reshape(L, d * 2304)
    bias = jnp.asarray(_attn_bias(g).reshape(4 * BLK, 2 * BLK))
    NB = min(ATT_NB, nb)
    W = NB * BLK

    def body(q_ref, kc_ref, kp_ref, vc_ref, vp_ref, b_ref, o_ref, l_ref):
        st = pl.program_id(1)
        k_all = jnp.concatenate([kp_ref[...], kc_ref[...]], axis=0)
        v_all = jnp.concatenate([vp_ref[...], vc_ref[...]], axis=0)
        lane, masks = _head_masks()
        for b in range(NB):
            rows = slice(b * BLK, (b + 1) * BLK)
            kk, vv = k_all[b * BLK:(b + 2) * BLK], v_all[b * BLK:(b + 2) * BLK]
            s = _dot_nt(_stack_heads(q_ref[rows, :], masks), kk) * 0.125 + b_ref[...]
            if b == 0:
                s = s + jnp.where((st == 0) & (lane < BLK), NEG, 0.0).astype(F32)
            mx = jnp.max(s, axis=-1, keepdims=True)
            e = jnp.exp(s - mx)
            den = jnp.sum(e, axis=-1, keepdims=True)
            o_ref[rows, :] = _unstack_heads(_dot(e.astype(BF16), vv) / den, masks)
            l_ref[rows, :] = _unstack_heads(mx + jnp.log(den), masks)

    def wide(f):
        return pl.BlockSpec((W, GROUP_W), f)

    def one(f):
        return pl.BlockSpec((BLK, GROUP_W), f)

    prev = lambda s: jnp.maximum(s * NB - 1, 0)
    o, lse = pl.pallas_call(
        body, grid=(d, nb // NB), name=f"attn_fwd_g{g}",
        in_specs=[wide(lambda r, s: (s, 9 * r + g)),
                  wide(lambda r, s: (s, 9 * r + 3 + g)), one(lambda r, s: (prev(s), 9 * r + 3 + g)),
                  wide(lambda r, s: (s, 9 * r + 6 + g)), one(lambda r, s: (prev(s), 9 * r + 6 + g)),
                  pl.BlockSpec((4 * BLK, 2 * BLK), lambda r, s: (0, 0))],
        out_specs=[wide(lambda r, s: (s, r)), wide(lambda r, s: (s, r))],
        out_shape=[_sds((L, d * GROUP_W), F32), _sds((L, d * GROUP_W), F32)],
        compiler_params=_params(("parallel", "parallel"), 32),
    )(qv, qv, qv, qv, qv, bias)
    return o.reshape(S, GROUP_W), lse.reshape(S, GROUP_W)


def _group_weights(l0, l1, l2):
    m = jnp.maximum(jnp.maximum(l0, l1), l2)
    e0, e1, e2 = jnp.exp(l0 - m), jnp.exp(l1 - m), jnp.exp(l2 - m)
    inv = 1.0 / (e0 + e1 + e2)
    return e0 * inv, e1 * inv, e2 * inv


def _sgu_forward(uvf, gs, wt_ref, bst_ref, mixed_s, tm):
    z, t = _gelu(uvf)
    u, v = z[:, :SGU_W], z[:, SGU_W:]
    rv = _rms(v)
    vnb = (v * rv * gs).astype(BF16)
    for ci in range(tm // 128):
        for g in range(4):
            rs, cs = slice(ci * 128, (ci + 1) * 128), slice(g * 128, (g + 1) * 128)
            mixed_s[rs, cs] = _dot(wt_ref[g], vnb[rs, cs]) + bst_ref[:, g:g + 1]
    return u, v, rv, vnb, t


def _fwd_mid(x, os_, ls_, uv, gl, wt, bst, g_sgu, b_gate, wt_ba, wt_bs, w_out, tm=256):
    S = x.shape[0]

    def body(x_ref, o0, o1, o2, l0, l1, l2, uv_ref, gl_ref, wt_ref, bst_ref, gs_ref, bg_ref, wba_ref, wbs_ref, wo_ref,
             ya_ref, ys_ref, ba_ref, bs_ref, mg_ref, h1_ref, mixed_s):
        a0, a1, a2 = _group_weights(l0[...], l1[...], l2[...])
        yab = (a0 * o0[...] + a1 * o1[...] + a2 * o2[...]).astype(BF16)
        ya_ref[...] = yab
        u, _, _, _, _ = _sgu_forward(uv_ref[...].astype(F32), gs_ref[...], wt_ref, bst_ref, mixed_s, tm)
        ysb = (u * mixed_s[...]).astype(BF16)
        ys_ref[...] = ysb
        gates = jax.nn.sigmoid(gl_ref[...].astype(F32) + bg_ref[...])
        ba = _dot_nt(yab, wba_ref[...])
        bs = _dot_nt(ysb, wbs_ref[...])
        ba_ref[...] = ba.astype(BF16)
        bs_ref[...] = bs.astype(BF16)
        mgb = (gates[:, :D] * ba + gates[:, D:] * bs).astype(BF16)
        mg_ref[...] = mgb
        h1_ref[...] = x_ref[...] + _dot(mgb, wo_ref[...])

    gw = _row(tm, GROUP_W)
    return pl.pallas_call(
        body, grid=(S // tm,), name="fwd_mid",
        in_specs=[_row(tm, D), gw, gw, gw, gw, gw, gw, _row(tm, 1024), _row(tm, 2048)] + [RES] * 7,
        out_specs=[gw, _row(tm, SGU_W), _row(tm, D), _row(tm, D), _row(tm, D), _row(tm, D)],
        out_shape=[_sds((S, GROUP_W), BF16), _sds((S, SGU_W), BF16), _sds((S, D), BF16), _sds((S, D), BF16),
                   _sds((S, D), BF16), _sds((S, D), F32)],
        scratch_shapes=[pltpu.VMEM((tm, SGU_W), F32)],
        compiler_params=_params(("parallel",), 48),
    )(x, *os_, *ls_, uv, gl, wt, bst, g_sgu, b_gate, wt_ba, wt_bs, w_out)


def _mem_fwd(mem, g_mem, w_kv):
    def body(m_ref, g_ref, w_ref, mb_ref, kv_ref):
        mv = m_ref[...]
        mb = (mv * _rms(mv) * g_ref[...]).astype(BF16)
        mb_ref[...] = mb
        kv_ref[...] = _dot(mb, w_ref[...]).astype(BF16)

    return pl.pallas_call(
        body, name="mem_fwd", in_specs=[RES, RES, RES], out_specs=[RES, RES],
        out_shape=[_sds(mem.shape, BF16), _sds((mem.shape[0], 2 * MEM_W), BF16)],
        compiler_params=pltpu.CompilerParams(vmem_limit_bytes=32 << 20),
    )(mem, g_mem, w_kv)


def _cross_probs(qh, kh):
    s = _dot_nt(qh, kh) * (MEM_HD ** -0.5)
    e = jnp.exp(s - jnp.max(s, axis=-1, keepdims=True))
    return e / jnp.sum(e, axis=-1, keepdims=True)


def _fwd_cross(h1, g_cross, w_q, kv, wt_o, tm=512):
    S = h1.shape[0]

    def body(h_ref, g_ref, wq_ref, kv_ref, wo_ref, c_ref, qc_ref, oc_ref, h2_ref):
        hv = h_ref[...]
        cb = (hv * _rms(hv) * g_ref[...]).astype(BF16)
        c_ref[...] = cb
        qcb = _dot(cb, wq_ref[...]).astype(BF16)
        qc_ref[...] = qcb
        for h in range(MEM_HEADS):
            cs = slice(h * MEM_HD, (h + 1) * MEM_HD)
            p = _cross_probs(qcb[:, cs], kv_ref[:, cs])
            oc_ref[:, cs] = _dot(p.astype(BF16), kv_ref[:, MEM_W + h * MEM_HD:MEM_W + (h + 1) * MEM_HD]).astype(BF16)
        h2_ref[...] = hv + _dot_nt(oc_ref[...], wo_ref[...])

    return pl.pallas_call(
        body, grid=(S // tm,), name="fwd_cross",
        in_specs=[_row(tm, D), RES, RES, RES, RES],
        out_specs=[_row(tm, D), _row(tm, MEM_W), _row(tm, MEM_W), _row(tm, D)],
        out_shape=[_sds((S, D), BF16), _sds((S, MEM_W), BF16), _sds((S, MEM_W), BF16), _sds((S, D), F32)],
        compiler_params=_params(("parallel",), 40),
    )(h1, g_cross, w_q, kv, wt_o)


def _ffn_fwd_bwd(h2, target, g_ffn, g_final, wt_gu, w_down, tm=256):
    S = h2.shape[0]
    nch = D_FF // FF_CHUNK

    def body(h_ref, t_ref, gf_ref, gz_ref, wgu_ref, wd_ref,
             f_ref, act_ref, dgu_ref, dh3b_ref, dh2_ref, dh2b_ref, dgf_ref, dgz_ref, loss_ref, gu_s):
        i = pl.program_id(0)
        hv = h_ref[...]
        r2 = _rms(hv)
        gf = gf_ref[...]
        fb = (hv * r2 * gf).astype(BF16)
        f_ref[...] = fb
        h3 = hv
        for c in range(nch):
            cs = slice(c * FF_CHUNK, (c + 1) * FF_CHUNK)
            us = slice(D_FF + c * FF_CHUNK, D_FF + (c + 1) * FF_CHUNK)
            gt = _dot_nt(fb, wgu_ref[cs, :])
            up = _dot_nt(fb, wgu_ref[us, :])
            gu_s[:, cs] = gt
            gu_s[:, us] = up
            actb = (gt * jax.nn.sigmoid(gt) * up).astype(BF16)
            act_ref[:, cs] = actb
            h3 = h3 + _dot(actb, wd_ref[cs, :])
        r3 = _rms(h3)
        gz = gz_ref[...]
        diff = h3 * r3 * gz - t_ref[...]
        dy = diff * (1.0 / D)
        dh3, dgz_rows = _rms_bwd(dy, h3, r3, gz)
        dh3b = dh3.astype(BF16)
        dh3b_ref[...] = dh3b
        df = jnp.zeros((tm, D), F32)
        for c in range(nch):
            cs = slice(c * FF_CHUNK, (c + 1) * FF_CHUNK)
            us = slice(D_FF + c * FF_CHUNK, D_FF + (c + 1) * FF_CHUNK)
            dact = _dot_nt(dh3b, wd_ref[cs, :])
            gt, up = gu_s[:, cs], gu_s[:, us]
            sg = jax.nn.sigmoid(gt)
            dgt = (dact * up * (sg * (1.0 + gt * (1.0 - sg)))).astype(BF16)
            dup = (dact * (gt * sg)).astype(BF16)
            dgu_ref[:, cs] = dgt
            dgu_ref[:, us] = dup
            df = df + _dot(dgt, wgu_ref[cs, :]) + _dot(dup, wgu_ref[us, :])
        dhn, dgf_rows = _rms_bwd(df, hv, r2, gf)
        dh2 = dh3 + dhn
        dh2_ref[...] = dh2
        dh2b_ref[...] = dh2.astype(BF16)

        @pl.when(i == 0)
        def _():
            dgf_ref[...] = jnp.zeros_like(dgf_ref)
            dgz_ref[...] = jnp.zeros_like(dgz_ref)
            loss_ref[...] = jnp.zeros_like(loss_ref)

        dgf_ref[...] += jnp.sum(dgf_rows, axis=0, keepdims=True)
        dgz_ref[...] += jnp.sum(dgz_rows, axis=0, keepdims=True)
        loss_ref[...] += jnp.sum(jnp.sum(diff * diff, axis=0, keepdims=True), axis=1, keepdims=True) * (0.5 / D)

    return pl.pallas_call(
        body, grid=(S // tm,), name="ffn_fwd_bwd",
        in_specs=[_row(tm, D), _row(tm, D), RES, RES, RES, RES],
        out_specs=[_row(tm, D), _row(tm, D_FF), _row(tm, 2 * D_FF), _row(tm, D), _row(tm, D), _row(tm, D),
                   _acc((1, D)), _acc((1, D)), _acc((1, 128))],
        out_shape=[_sds((S, D), BF16), _sds((S, D_FF), BF16), _sds((S, 2 * D_FF), BF16), _sds((S, D), BF16),
                   _sds((S, D), F32), _sds((S, D), BF16), _sds((1, D), F32), _sds((1, D), F32), _sds((1, 128), F32)],
        scratch_shapes=[pltpu.VMEM((tm, 2 * D_FF), F32)],
        compiler_params=_params(("arbitrary",), 56),
    )(h2, target, g_ffn, g_final, wt_gu, w_down)


def _bwd_cross(dh2, h1, qc, g_cross, w_q, kv, wt_o, tm=256):
    S = h1.shape[0]

    def body(d_ref, h_ref, qc_ref, g_ref, wq_ref, kv_ref, wo_ref, dqc_ref, dh1_ref, dh1b_ref, dkv_ref, dg_ref):
        i = pl.program_id(0)

        @pl.when(i == 0)
        def _():
            dkv_ref[...] = jnp.zeros_like(dkv_ref)
            dg_ref[...] = jnp.zeros_like(dg_ref)

        dh2 = d_ref[...]
        doc = _dot(dh2.astype(BF16), wo_ref[...])
        qcb = qc_ref[...]
        for h in range(MEM_HEADS):
            cs = slice(h * MEM_HD, (h + 1) * MEM_HD)
            vs = slice(MEM_W + h * MEM_HD, MEM_W + (h + 1) * MEM_HD)
            qh, kh, vh = qcb[:, cs], kv_ref[:, cs], kv_ref[:, vs]
            p = _cross_probs(qh, kh)
            dohb = doc[:, cs].astype(BF16)
            dp = _dot_nt(dohb, vh)
            dsb = (p * (dp - jnp.sum(dp * p, axis=-1, keepdims=True)) * (MEM_HD ** -0.5)).astype(BF16)
            dqc_ref[:, cs] = _dot(dsb, kh).astype(BF16)
            dkv_ref[:, cs] += _dot_tn(dsb, qh)
            dkv_ref[:, vs] += _dot_tn(p.astype(BF16), dohb)
        dc = _dot_nt(dqc_ref[...], wq_ref[...])
        hv = h_ref[...]
        dhn, dg_rows = _rms_bwd(dc, hv, _rms(hv), g_ref[...])
        dh1 = dh2 + dhn
        dh1_ref[...] = dh1
        dh1b_ref[...] = dh1.astype(BF16)
        dg_ref[...] += jnp.sum(dg_rows, axis=0, keepdims=True)

    return pl.pallas_call(
        body, grid=(S // tm,), name="bwd_cross",
        in_specs=[_row(tm, D), _row(tm, D), _row(tm, MEM_W), RES, RES, RES, RES],
        out_specs=[_row(tm, MEM_W), _row(tm, D), _row(tm, D), _acc((256, 2 * MEM_W)), _acc((1, D))],
        out_shape=[_sds((S, MEM_W), BF16), _sds((S, D), F32), _sds((S, D), BF16), _sds((256, 2 * MEM_W), F32),
                   _sds((1, D), F32)],
        compiler_params=_params(("arbitrary",), 40),
    )(dh2, h1, qc, g_cross, w_q, kv, wt_o)


def _mem_bwd(dkv, mem, mb, g_mem, w_kv):
    def body(dkv_ref, m_ref, mb_ref, g_ref, w_ref, dw_ref, dg_ref):
        dkvb = dkv_ref[...].astype(BF16)
        dw_ref[...] = _dot_tn(mb_ref[...], dkvb)
        dm = _dot_nt(dkvb, w_ref[...])
        mv = m_ref[...]
        dg_ref[...] = jnp.sum(dm * mv * _rms(mv), axis=0, keepdims=True)

    return pl.pallas_call(
        body, name="mem_bwd", in_specs=[RES] * 5, out_specs=[RES, RES],
        out_shape=[_sds((D, 2 * MEM_W), F32), _sds((1, D), F32)],
        compiler_params=pltpu.CompilerParams(vmem_limit_bytes=32 << 20),
    )(dkv, mem, mb, g_mem, w_kv)


def _bwd_mid(dh1, gl, ba, bs, uv, ls_, ya, wt, bst, g_sgu, b_gate, wt_ba, wt_bs, w_out, tm=256):
    S = dh1.shape[0]

    def body(d_ref, gl_ref, ba_ref, bs_ref, uv_ref, l0, l1, l2, ya_ref,
             wt_ref, bst_ref, gs_ref, bg_ref, wba_ref, wbs_ref, wo_ref,
             dba_ref, dbs_ref, dgl_ref, duv_ref, do0, do1, do2, c0, c1, c2,
             dbg_ref, dgs_ref, dws_ref, dbsa_ref, mixed_s, dvn_s):
        i = pl.program_id(0)

        @pl.when(i == 0)
        def _():
            for r in (dbg_ref, dgs_ref, dws_ref, dbsa_ref):
                r[...] = jnp.zeros_like(r)

        dm = _dot_nt(d_ref[...].astype(BF16), wo_ref[...])
        gates = jax.nn.sigmoid(gl_ref[...].astype(F32) + bg_ref[...])
        g0, g1 = gates[:, :D], gates[:, D:]
        dbab = (dm * g0).astype(BF16)
        dbsb = (dm * g1).astype(BF16)
        dba_ref[...] = dbab
        dbs_ref[...] = dbsb
        dg0 = dm * ba_ref[...].astype(F32) * g0 * (1.0 - g0)
        dg1 = dm * bs_ref[...].astype(F32) * g1 * (1.0 - g1)
        dgl_ref[:, :D] = dg0.astype(BF16)
        dgl_ref[:, D:] = dg1.astype(BF16)
        dbg_ref[:, :D] += jnp.sum(dg0, axis=0, keepdims=True)
        dbg_ref[:, D:] += jnp.sum(dg1, axis=0, keepdims=True)
        dya = _dot(dbab, wba_ref[...])
        dys = _dot(dbsb, wbs_ref[...])

        uvf = uv_ref[...].astype(F32)
        gs = gs_ref[...]
        u, v, rv, vnb, t = _sgu_forward(uvf, gs, wt_ref, bst_ref, mixed_s, tm)
        du = dys * mixed_s[...]
        dmixed = dys * u
        for ci in range(tm // 128):
            for g in range(4):
                rs, cs = slice(ci * 128, (ci + 1) * 128), slice(g * 128, (g + 1) * 128)
                dmx = dmixed[rs, cs]
                dmxb = dmx.astype(BF16)
                dvn_s[rs, cs] = _dot_tn(wt_ref[g], dmxb)
                dws_ref[g] += _dot_nt(dmxb, vnb[rs, cs])
                dbsa_ref[g] += dmx
        dv, dgs_rows = _rms_bwd(dvn_s[...], v, rv, gs)
        dgs_ref[...] += jnp.sum(dgs_rows, axis=0, keepdims=True)
        gg = _gelu_grad(uvf, t)
        duv_ref[:, :SGU_W] = (du * gg[:, :SGU_W]).astype(BF16)
        duv_ref[:, SGU_W:] = (dv * gg[:, SGU_W:]).astype(BF16)

        a0, a1, a2 = _group_weights(l0[...], l1[...], l2[...])
        prod = dya * ya_ref[...].astype(F32)
        _, masks = _head_masks()
        hs = jnp.zeros_like(prod)
        for h in range(4):
            sh = jnp.sum(jnp.where(masks[h], prod, 0.0), axis=-1, keepdims=True)
            hs = jnp.where(masks[h], sh, hs)
        for a, do_ref, c_ref in ((a0, do0, c0), (a1, do1, c1), (a2, do2, c2)):
            do_ref[...] = (a * dya).astype(BF16)
            c_ref[...] = a * hs

    gw = _row(tm, GROUP_W)
    return pl.pallas_call(
        body, grid=(S // tm,), name="bwd_mid",
        in_specs=[_row(tm, D), _row(tm, 2048), _row(tm, D), _row(tm, D), _row(tm, 1024), gw, gw, gw, gw] + [RES] * 7,
        out_specs=[_row(tm, D), _row(tm, D), _row(tm, 2048), _row(tm, 1024), gw, gw, gw, gw, gw, gw,
                   _acc((1, 2048)), _acc((1, SGU_W)), _acc((4, 128, 128)), _acc((4, 128, 128))],
        out_shape=[_sds((S, D), BF16), _sds((S, D), BF16), _sds((S, 2048), BF16), _sds((S, 1024), BF16)]
        + [_sds((S, GROUP_W), BF16)] * 3 + [_sds((S, GROUP_W), F32)] * 3
        + [_sds((1, 2048), F32), _sds((1, SGU_W), F32), _sds((4, 128, 128), F32), _sds((4, 128, 128), F32)],
        scratch_shapes=[pltpu.VMEM((tm, SGU_W), F32), pltpu.VMEM((tm, SGU_W), F32)],
        compiler_params=_params(("arbitrary",), 48),
    )(dh1, gl, ba, bs, uv, *ls_, ya, wt, bst, g_sgu, b_gate, wt_ba, wt_bs, w_out)


def _attn_bwd(qkv, do, lse, corr, g, prev):
    S = qkv.shape[0]
    d = DIL_GROUPS[g][1]
    L = S // d
    nb = L // BLK
    NB = min(ATT_NB, nb)
    W = NB * BLK
    nsteps = nb // NB
    qv = qkv.reshape(L, d * 2304)
    view = lambda t: t.reshape(L, d * GROUP_W)
    bias = jnp.asarray(_attn_bias(g).reshape(4 * BLK, 2 * BLK))
    n_in = 13 + (0 if prev is None else 3)

    def body(*refs):
        (q_ref, kc_ref, kp_ref, vc_ref, vp_ref, do_ref, l_ref, c_ref,
         qn_ref, don_ref, ln_ref, cn_ref, b_ref) = refs[:13]
        dq_ref, dk_ref, dv_ref, dk_s, dv_s = refs[n_in:]
        st = pl.program_id(1)
        k_all = jnp.concatenate([kp_ref[...], kc_ref[...]], axis=0)
        v_all = jnp.concatenate([vp_ref[...], vc_ref[...]], axis=0)
        lane, masks = _head_masks()
        dk_s[...] = jnp.zeros_like(dk_s)
        dv_s[...] = jnp.zeros_like(dv_s)

        def block_terms(qs, dos, kk, vv, bias_v, lse_c, corr_c):
            s = _dot_nt(qs, kk) * 0.125 + bias_v
            p = jnp.exp(s - lse_c)
            dsb = (p * (_dot_nt(dos, vv) - corr_c) * 0.125).astype(BF16)
            return dsb, p.astype(BF16)

        for b in range(NB):
            rows = slice(b * BLK, (b + 1) * BLK)
            keys = slice(b * BLK, (b + 2) * BLK)
            kk, vv = k_all[keys], v_all[keys]
            qs, dos = _stack_heads(q_ref[rows, :], masks), _stack_heads(do_ref[rows, :], masks)
            bias_v = b_ref[...]
            if b == 0:
                bias_v = bias_v + jnp.where((st == 0) & (lane < BLK), NEG, 0.0).astype(F32)
            dsb, pb = block_terms(qs, dos, kk, vv, bias_v, _stack_cols(l_ref, rows), _stack_cols(c_ref, rows))
            dq_ref[rows, :] = _unstack_heads(_dot(dsb, kk), masks).astype(BF16)
            dk_s[keys, :] += _dot_tn(dsb, qs)
            dv_s[keys, :] += _dot_tn(pb, dos)

        @pl.when(st < nsteps - 1)
        def _():
            last = slice(NB * BLK, (NB + 1) * BLK)
            qs, dos = _stack_heads(qn_ref[...], masks), _stack_heads(don_ref[...], masks)
            every = slice(None)
            dsb, pb = block_terms(qs, dos, k_all[last], v_all[last], b_ref[:, :BLK],
                                  _stack_cols(ln_ref, every), _stack_cols(cn_ref, every))
            dk_s[last, :] += _dot_tn(dsb, qs)
            dv_s[last, :] += _dot_tn(pb, dos)

        dk_ref[...] = dk_s[BLK:, :].astype(BF16)
        dv_ref[...] = dv_s[BLK:, :].astype(BF16)

    def wide(f):
        return pl.BlockSpec((W, GROUP_W), f)

    def one(f):
        return pl.BlockSpec((BLK, GROUP_W), f)

    pv = lambda s: jnp.maximum(s * NB - 1, 0)
    nx = lambda s: jnp.minimum((s + 1) * NB, nb - 1)
    ins = [qv, qv, qv, qv, qv, view(do), view(lse), view(corr), qv, view(do), view(lse), view(corr), bias]
    in_specs = [wide(lambda r, s: (s, 9 * r + g)),
                wide(lambda r, s: (s, 9 * r + 3 + g)), one(lambda r, s: (pv(s), 9 * r + 3 + g)),
                wide(lambda r, s: (s, 9 * r + 6 + g)), one(lambda r, s: (pv(s), 9 * r + 6 + g)),
                wide(lambda r, s: (s, r)), wide(lambda r, s: (s, r)), wide(lambda r, s: (s, r)),
                one(lambda r, s: (nx(s), 9 * r + g)),
                one(lambda r, s: (nx(s), r)), one(lambda r, s: (nx(s), r)), one(lambda r, s: (nx(s), r)),
                pl.BlockSpec((4 * BLK, 2 * BLK), lambda r, s: (0, 0))]
    aliases = {}
    if prev is not None:
        ins += [t.reshape(L, d * 768) for t in prev]
        in_specs += [ANY] * 3
        aliases = {13: 0, 14: 1, 15: 2}
    outs = pl.pallas_call(
        body, grid=(d, nsteps), name=f"attn_bwd_g{g}",
        in_specs=in_specs,
        out_specs=[wide(lambda r, s: (s, 3 * r + g))] * 3,
        out_shape=[_sds((L, d * 768), BF16)] * 3,
        scratch_shapes=[pltpu.VMEM(((NB + 1) * BLK, GROUP_W), F32), pltpu.VMEM(((NB + 1) * BLK, GROUP_W), F32)],
        input_output_aliases=aliases,
        compiler_params=_params(("parallel", "parallel"), 32),
    )(*ins)
    return [t.reshape(S, 768) for t in outs]


def _bwd_in(dq, dk, dv, duv, dgl, dh1, x, g_mix, wt_in, tm=512):
    S = x.shape[0]

    def body(dq_ref, dk_ref, dv_ref, duv_ref, dgl_ref, d_ref, x_ref, g_ref, w_ref, dx_ref, dg_ref):
        i = pl.program_id(0)

        @pl.when(i == 0)
        def _():
            dg_ref[...] = jnp.zeros_like(dg_ref)

        da = (_dot(dq_ref[...], w_ref[0:768, :]) + _dot(dk_ref[...], w_ref[768:1536, :])
              + _dot(dv_ref[...], w_ref[1536:2304, :]) + _dot(duv_ref[...], w_ref[2304:3328, :])
              + _dot(dgl_ref[...], w_ref[3328:5376, :]))
        xv = x_ref[...]
        dxn, dg_rows = _rms_bwd(da, xv, _rms(xv), g_ref[...])
        dx_ref[...] = d_ref[...] + dxn
        dg_ref[...] += jnp.sum(dg_rows, axis=0, keepdims=True)

    return pl.pallas_call(
        body, grid=(S // tm,), name="bwd_in",
        in_specs=[_row(tm, 768), _row(tm, 768), _row(tm, 768), _row(tm, 1024), _row(tm, 2048), _row(tm, D), _row(tm, D),
                  RES, RES],
        out_specs=[_row(tm, D), _acc((1, D))],
        out_shape=[_sds((S, D), F32), _sds((1, D), F32)],
        compiler_params=_params(("arbitrary",), 56),
    )(dq, dk, dv, duv, dgl, dh1, x, g_mix, wt_in)


def _tn_matmul(a, b, name, tk, ts=2048, into=None):
    S, K = a.shape
    N = b.shape[1]
    off = 0 if into is None else into[1] // tk

    def body(*refs):
        a_ref, b_ref, o_ref = refs[0], refs[1], refs[-1]

        @pl.when(pl.program_id(1) == 0)
        def _():
            o_ref[...] = jnp.zeros_like(o_ref)

        o_ref[...] += _dot_tn(a_ref[...], b_ref[...])

    ins = [a, b]
    in_specs = [pl.BlockSpec((ts, tk), lambda k, s: (s, k)), pl.BlockSpec((ts, N), lambda k, s: (s, 0))]
    out_shape = _sds((K, N), F32)
    aliases = {}
    if into is not None:
        ins.append(into[0])
        in_specs.append(ANY)
        out_shape = _sds(into[0].shape, F32)
        aliases = {2: 0}
    return pl.pallas_call(
        body, grid=(K // tk, S // ts), name=name, in_specs=in_specs,
        out_specs=pl.BlockSpec((tk, N), lambda k, s: (k + off, 0)), out_shape=out_shape,
        input_output_aliases=aliases,
        compiler_params=_params(("parallel", "arbitrary"), 48),
    )(*ins)


def _chip_peers(x, y):
    return [(1 - x, y), (x, 1 - y), (1 - x, 1 - y)]


STAGE_BYTES = 2 << 20


def _chunk_plan(shapes, itemsize):
    plan = []
    for i, (rows, w) in enumerate(shapes):
        ch = max(16, min(rows, (STAGE_BYTES // (w * itemsize)) // 16 * 16))
        while rows % ch:
            ch -= 16
        plan += [(i, r0, ch) for r0 in range(0, rows, ch)]
    return plan


def _remote(src, dst, ssem, rsem, dev):
    return pltpu.make_async_remote_copy(src_ref=src, dst_ref=dst, send_sem=ssem, recv_sem=rsem, device_id=dev,
                                        device_id_type=MESH)


def _gather_weights(shards):
    n = len(shards)
    halves = [s.reshape(2, s.shape[0] // 2, s.shape[1]) for s in shards]
    plan = _chunk_plan([h.shape[1:] for h in halves], 2)

    def body(*refs):
        ins, outs = refs[:n], refs[n:2 * n]
        ssem, rsem, lsem, osem, buf = refs[2 * n:]
        x, y, c = lax.axis_index("x"), lax.axis_index("y"), lax.axis_index("c")
        me = 2 * x + y
        chips = _chip_peers(x, y)
        sib = (x, y, 1 - c)
        sends = []
        for i in range(n):
            for k, (px, py) in enumerate(chips):
                cp = _remote(ins[i].at[c], outs[i].at[me, c], ssem.at[6 * i + k], rsem.at[6 * i + k], (px, py, c))
                cp.start()
                sends.append(cp)
        pending = {}
        for k, (i, r0, ch) in enumerate(plan):
            for h in range(2):
                slot = (2 * k + h) % 2
                if slot in pending:
                    pending[slot].wait()
                stage = buf.at[slot, pl.ds(0, ch), pl.ds(0, halves[i].shape[2])]
                ld = pltpu.make_async_copy(ins[i].at[h, pl.ds(r0, ch)], stage, lsem.at[slot])
                ld.start()
                ld.wait()
                st = pltpu.make_async_copy(stage, outs[i].at[me, h, pl.ds(r0, ch)], osem.at[slot])
                st.start()
                pending[slot] = st
        for st in pending.values():
            st.wait()
        for i in range(n):
            for k, (px, py) in enumerate(chips):
                landed = outs[i].at[2 * px + py, c]
                _remote(landed, landed, ssem.at[6 * i + k], rsem.at[6 * i + k], (px, py, c)).wait_recv()
                cp = _remote(landed, landed, ssem.at[6 * i + 3 + k], rsem.at[6 * i + 3 + k], sib)
                cp.start()
                sends.append(cp)
        for i in range(n):
            for k, (px, py) in enumerate(chips):
                passed = outs[i].at[2 * px + py, 1 - c]
                _remote(passed, passed, ssem.at[6 * i + 3 + k], rsem.at[6 * i + 3 + k], sib).wait_recv()
        for cp in sends:
            cp.wait_send()

    outs = pl.pallas_call(
        body, name="gather_weights", in_specs=[ANY] * n, out_specs=[ANY] * n,
        out_shape=[_sds((4,) + h.shape, BF16) for h in halves],
        scratch_shapes=[pltpu.SemaphoreType.DMA((6 * n,)), pltpu.SemaphoreType.DMA((6 * n,)),
                        pltpu.SemaphoreType.DMA((2,)), pltpu.SemaphoreType.DMA((2,)),
                        pltpu.VMEM((2, max(p[2] for p in plan), max(h.shape[2] for h in halves)), BF16)],
        compiler_params=pltpu.CompilerParams(vmem_limit_bytes=32 << 20),
    )(*halves)
    return [o.reshape(4 * s.shape[0], s.shape[1]) for o, s in zip(outs, shards)]


def _swap_halves(grads):
    n = len(grads)
    g4 = [g.reshape(4, 2, g.shape[0] // 8, g.shape[1]) for g in grads]

    def body(*refs):
        ins, got = refs[:n], refs[n:2 * n]
        ssem, rsem = refs[2 * n:]
        x, y, c = lax.axis_index("x"), lax.axis_index("y"), lax.axis_index("c")
        sib = (x, y, 1 - c)
        cps = []
        for i in range(n):
            rc = _remote(ins[i].at[:, 1 - c], got[i], ssem.at[i], rsem.at[i], sib)
            rc.start()
            cps.append(rc)
        for cp in cps:
            cp.wait()

    half = [_sds((4, g.shape[2], g.shape[3]), F32) for g in g4]
    got = pl.pallas_call(
        body, name="swap_halves", in_specs=[ANY] * n, out_specs=[ANY] * n, out_shape=half,
        scratch_shapes=[pltpu.SemaphoreType.DMA((n,)), pltpu.SemaphoreType.DMA((n,))],
    )(*g4)
    c = lax.axis_index("c")
    own = [lax.dynamic_index_in_dim(g, c, axis=1, keepdims=False) for g in g4]
    return own, got


def _scatter_partials(sums, sums_b):
    n = len(sums)

    def body(*refs):
        hbs, parts = refs[:n], refs[n:2 * n]
        ssem, rsem = refs[2 * n:]
        x, y, c = lax.axis_index("x"), lax.axis_index("y"), lax.axis_index("c")
        cps = []
        for i in range(n):
            for k, (px, py) in enumerate(_chip_peers(x, y)):
                rc = _remote(hbs[i].at[2 * px + py], parts[i].at[k], ssem.at[3 * i + k], rsem.at[3 * i + k], (px, py, c))
                rc.start()
                cps.append(rc)
        for cp in cps:
            cp.wait()

    parts = pl.pallas_call(
        body, name="scatter_partials", in_specs=[ANY] * n, out_specs=[ANY] * n,
        out_shape=[_sds((3,) + s.shape[1:], BF16) for s in sums],
        scratch_shapes=[pltpu.SemaphoreType.DMA((3 * n,)), pltpu.SemaphoreType.DMA((3 * n,))],
    )(*sums_b)
    me = 2 * lax.axis_index("x") + lax.axis_index("y")
    mine = [lax.dynamic_index_in_dim(s, me, axis=0, keepdims=False) for s in sums]
    return mine, parts


def _share_halves(reduced):
    n = len(reduced)
    plan = _chunk_plan([r.shape for r in reduced], 4)
    max_rows = max(p[2] for p in plan)
    max_w = max(r.shape[1] for r in reduced)

    def body(*refs):
        ins, outs = refs[:n], refs[n:2 * n]
        ssem, rsem, lsem, osem, buf = refs[2 * n:]
        x, y, c = lax.axis_index("x"), lax.axis_index("y"), lax.axis_index("c")
        sib = (x, y, 1 - c)
        pending = {}
        for k, (i, r0, ch) in enumerate(plan):
            slot = k % 2
            if slot in pending:
                rc, lc = pending[slot]
                rc.wait_send()
                lc.wait()
            stage = buf.at[slot, pl.ds(0, ch), pl.ds(0, reduced[i].shape[1])]
            ld = pltpu.make_async_copy(ins[i].at[pl.ds(r0, ch)], stage, lsem.at[slot])
            ld.start()
            ld.wait()
            place = outs[i].at[c, pl.ds(r0, ch)]
            rc = _remote(stage, place, ssem.at[slot], rsem.at[i], sib)
            lc = pltpu.make_async_copy(stage, place, osem.at[slot])
            rc.start()
            lc.start()
            pending[slot] = (rc, lc)
        for rc, lc in pending.values():
            rc.wait_send()
            lc.wait()
        for i in range(n):
            theirs = outs[i].at[1 - c]
            _remote(theirs, theirs, ssem.at[0], rsem.at[i], sib).wait_recv()

    outs = pl.pallas_call(
        body, name="share_halves", in_specs=[ANY] * n, out_specs=[ANY] * n,
        out_shape=[_sds((2,) + r.shape, F32) for r in reduced],
        scratch_shapes=[pltpu.SemaphoreType.DMA((2,)), pltpu.SemaphoreType.DMA((n,)), pltpu.SemaphoreType.DMA((2,)),
                        pltpu.SemaphoreType.DMA((2,)), pltpu.VMEM((2, max_rows, max_w), F32)],
        compiler_params=pltpu.CompilerParams(vmem_limit_bytes=32 << 20),
    )(*reduced)
    return [o.reshape(2 * r.shape[0], r.shape[1]) for o, r in zip(outs, reduced)]


def _gather_small(pack):
    P = pack.shape[0]

    def body(in_ref, out_ref, ssem, rsem, lsem):
        x, y, c = lax.axis_index("x"), lax.axis_index("y"), lax.axis_index("c")
        me = 4 * x + 2 * y + c
        lc = pltpu.make_async_copy(in_ref, out_ref.at[me], lsem)
        lc.start()
        cps = []
        for rel in range(1, 8):
            px = 1 - x if rel & 4 else x
            py = 1 - y if rel & 2 else y
            pc = 1 - c if rel & 1 else c
            rc = _remote(in_ref, out_ref.at[me], ssem.at[rel - 1], rsem.at[rel - 1], (px, py, pc))
            rc.start()
            cps.append((rc, 4 * px + 2 * py + pc))
        for rel, (rc, peer) in enumerate(cps):
            rc.wait_send()
            _remote(in_ref, out_ref.at[peer], ssem.at[rel], rsem.at[rel], (x, y, c)).wait_recv()
        lc.wait()

    return pl.pallas_call(
        body, name="gather_small", in_specs=[ANY], out_specs=ANY, out_shape=_sds((8, P, 128), F32),
        scratch_shapes=[pltpu.SemaphoreType.DMA((7,)), pltpu.SemaphoreType.DMA((7,)), pltpu.SemaphoreType.DMA],
    )(pack)


def _tile(rows, cap=256):
    t = min(rows, cap) // 16 * 16
    while rows % t:
        t -= 16
    return t


def _elementwise(fn, ins, out_dtypes, name):
    R, W = ins[0].shape
    tr = _tile(R, max(8, min(512, (1 << 18) // W // 8 * 8)))

    def body(*refs):
        outs = fn(*[r[...] for r in refs[:len(ins)]])
        for o_ref, o in zip(refs[len(ins):], outs):
            o_ref[...] = o.astype(o_ref.dtype)

    return pl.pallas_call(
        body, grid=(R // tr,), name=name, in_specs=[_row(tr, W)] * len(ins), out_specs=[_row(tr, W)] * len(out_dtypes),
        out_shape=[_sds((R, W), dt) for dt in out_dtypes],
        compiler_params=_params(("parallel",), 48),
    )(*ins)


def _adamw(w, g, m, v):
    m = B1 * m + (1.0 - B1) * g
    v = B2 * v + (1.0 - B2) * (g * g)
    m_hat = m / (1.0 - B1 ** STEP)
    v_hat = v / (1.0 - B2 ** STEP)
    return -LR * (m_hat / (jnp.sqrt(v_hat) + AEPS) + WD * w), m, v


def _adam_small(w, m, v, packs):
    def body(w_ref, m_ref, v_ref, p_ref, g_ref, d_ref, nm_ref, nv_ref):
        g = p_ref[0]
        for k in range(1, 8):
            g = g + p_ref[k]
        g_ref[...] = g
        d_ref[...], nm_ref[...], nv_ref[...] = _adamw(w_ref[...], g, m_ref[...], v_ref[...])

    return pl.pallas_call(
        body, name="adam_small", in_specs=[RES] * 4, out_specs=[RES] * 4, out_shape=[_sds(w.shape, F32)] * 4,
        compiler_params=pltpu.CompilerParams(vmem_limit_bytes=32 << 20),
    )(w, m, v, packs)


def _local_step(xs, tgt, mems, weights, small):
    wt_in, wt_ba, wt_bs, wo, wq, wkv, wt_o, wt_gu, wd = weights
    g_mix, b_gate, w_sgu, b_sgu, g_sgu, g_cross, g_mem, g_ffn, g_final = small
    wt = jnp.tril(w_sgu).astype(BF16)
    bst = b_sgu.T

    a, qkv, uv, gl = _fwd_in(xs, g_mix, wt_in)
    os_, ls_ = zip(*[_attn_fwd(qkv, g) for g in range(3)])
    ya, ys, ba, bs, mg, h1 = _fwd_mid(xs, os_, ls_, uv, gl, wt, bst, g_sgu, b_gate, wt_ba, wt_bs, wo)
    mb, kv = _mem_fwd(mems, g_mem, wkv)
    cb, qc, oc, h2 = _fwd_cross(h1, g_cross, wq, kv, wt_o)
    f, act, dgu, dh3b, dh2, dh2b, dg_ffn, dg_final, loss = _ffn_fwd_bwd(h2, tgt, g_ffn, g_final, wt_gu, wd)

    dqc, dh1, dh1b, dkv, dg_cross = _bwd_cross(dh2, h1, qc, g_cross, wq, kv, wt_o)
    dw_kv, dg_mem = _mem_bwd(dkv, mems, mb, g_mem, wkv)
    (dba, dbs, dgl, duv, do0, do1, do2, c0, c1, c2, db_gate, dg_sgu, dws, dbs_acc) = _bwd_mid(
        dh1, gl, ba, bs, uv, ls_, ya, wt, bst, g_sgu, b_gate, wt_ba, wt_bs, wo)
    dqkv = None
    for g, (do, corr) in enumerate(((do0, c0), (do1, c1), (do2, c2))):
        dqkv = _attn_bwd(qkv, do, ls_[g], corr, g, dqkv)
    dq, dk, dv = dqkv
    grad_x, dg_mix = _bwd_in(dq, dk, dv, duv, dgl, dh1, xs, g_mix, wt_in)

    dwt_in = _tn_matmul(dq, a, "dw_in_q", 256)
    dwt_in = jnp.concatenate([dwt_in, _tn_matmul(dk, a, "dw_in_k", 256), _tn_matmul(dv, a, "dw_in_v", 256),
                              _tn_matmul(duv, a, "dw_in_uv", 512), _tn_matmul(dgl, a, "dw_in_gl", 512)], axis=0)
    grads = [dwt_in,
             _tn_matmul(dba, ya, "dw_branch_attn", 512),
             _tn_matmul(dbs, ys, "dw_branch_sgu", 512),
             _tn_matmul(mg, dh1b, "dw_out", 512),
             _tn_matmul(cb, dqc, "dw_q_cross", 512),
             dw_kv,
             _tn_matmul(dh2b, oc, "dw_o_cross", 512),
             _tn_matmul(dgu, f, "dw_gate_up", 512),
             _tn_matmul(act, dh3b, "dw_down", 256)]

    return loss, grad_x, grads, (dg_mix, db_gate, dws, dbs_acc, dg_sgu, dg_cross, dg_mem, dg_ffn, dg_final)


def kernel(x, mem, g_mix, w_in, b_gate, w_sgu_spatial, b_sgu_spatial, g_sgu, w_branch_attn, w_branch_sgu, w_out, g_cross, g_mem, w_q_cross, w_kv_cross, w_o_cross, g_ffn, w_gate_up, w_down, g_final, loss_target, m_g_mix, m_w_in, m_b_gate, m_w_sgu_spatial, m_b_sgu_spatial, m_g_sgu, m_w_branch_attn, m_w_branch_sgu, m_w_out, m_g_cross, m_g_mem, m_w_q_cross, m_w_kv_cross, m_w_o_cross, m_g_ffn, m_w_gate_up, m_w_down, m_g_final, v_g_mix, v_w_in, v_b_gate, v_w_sgu_spatial, v_b_sgu_spatial, v_g_sgu, v_w_branch_attn, v_w_branch_sgu, v_w_out, v_g_cross, v_g_mem, v_w_q_cross, v_w_kv_cross, v_w_o_cross, v_g_ffn, v_w_gate_up, v_w_down, v_g_final):
    S = x.shape[1]
    xs, tgt, mems = x.reshape(S, D), loss_target.reshape(S, D), mem.reshape(mem.shape[1], D)
    g_final2 = g_final.reshape(1, D)

    big = [("w_in", w_in[0], m_w_in[0], v_w_in[0], True),
           ("w_branch_attn", w_branch_attn[0], m_w_branch_attn[0], v_w_branch_attn[0], True),
           ("w_branch_sgu", w_branch_sgu[0], m_w_branch_sgu[0], v_w_branch_sgu[0], True),
           ("w_out", w_out[0], m_w_out[0], v_w_out[0], False),
           ("w_q_cross", w_q_cross[0], m_w_q_cross[0], v_w_q_cross[0], False),
           ("w_kv_cross", w_kv_cross[0], m_w_kv_cross[0], v_w_kv_cross[0], False),
           ("w_o_cross", w_o_cross[0], m_w_o_cross[0], v_w_o_cross[0], True),
           ("w_gate_up", w_gate_up[0], m_w_gate_up[0], v_w_gate_up[0], True),
           ("w_down", w_down[0], m_w_down[0], v_w_down[0], False)]
    shards = [(w.T if tr else w).astype(BF16) for _, w, _, _, tr in big]
    wt_in, wt_ba, wt_bs, wo, wq, wkv, wt_o, wt_gu, wd = _gather_weights(shards)

    (loss, grad_x, grads, (dg_mix, db_gate, dws, dbs_acc, dg_sgu, dg_cross, dg_mem, dg_ffn, dg_final)) = _local_step(
        xs, tgt, mems, (wt_in, wt_ba, wt_bs, wo, wq, wkv, wt_o, wt_gu, wd),
        (g_mix, b_gate, w_sgu_spatial[0], b_sgu_spatial[0], g_sgu, g_cross, g_mem, g_ffn, g_final2))

    own, got = _swap_halves(grads)
    sums, sums_b = [], []
    for i, (o, t) in enumerate(zip(own, got)):
        shp = o.shape
        s_, sb_ = _elementwise(lambda p, q: (p + q, p + q), [o.reshape(-1, shp[2]), t.reshape(-1, shp[2])],
                               [F32, BF16], f"chip_sum_{big[i][0]}")
        sums.append(s_.reshape(shp))
        sums_b.append(sb_.reshape(shp))
    mine, parts = _scatter_partials(sums, sums_b)
    reduced = []
    for i, (o, p) in enumerate(zip(mine, parts)):
        reduced.append(_elementwise(lambda p0, p1, p2, p3: (p0 + p1.astype(F32) + p2.astype(F32) + p3.astype(F32),),
                                    [o.astype(F32)] + [p[k] for k in range(3)], [F32], f"mesh_sum_{big[i][0]}")[0])
    full = _share_halves(reduced)

    big_out = {}
    for (name, w, m, v, tr), gsh in zip(big, full):
        gsh = gsh.T if tr else gsh
        delta, nm, nv = _elementwise(_adamw, [w, gsh, m, v], [F32, F32, F32], f"adam_{name}")
        big_out[name] = tuple(t[None] for t in (gsh, delta, nm, nv))

    small = [("g_mix", g_mix, m_g_mix, v_g_mix, dg_mix), ("b_gate", b_gate, m_b_gate, v_b_gate, db_gate),
             ("w_sgu_spatial", w_sgu_spatial, m_w_sgu_spatial, v_w_sgu_spatial, jnp.tril(dws)),
             ("b_sgu_spatial", b_sgu_spatial, m_b_sgu_spatial, v_b_sgu_spatial, jnp.sum(dbs_acc, axis=-1)),
             ("g_sgu", g_sgu, m_g_sgu, v_g_sgu, dg_sgu), ("g_cross", g_cross, m_g_cross, v_g_cross, dg_cross),
             ("g_mem", g_mem, m_g_mem, v_g_mem, dg_mem), ("g_ffn", g_ffn, m_g_ffn, v_g_ffn, dg_ffn),
             ("g_final", g_final, m_g_final, v_g_final, dg_final)]

    def pack(parts_, tail):
        return jnp.concatenate([p.reshape(-1) for p in parts_] + [tail]).reshape(-1, 128)

    zeros = jnp.zeros((1024,), F32)
    gp = pack([s[4] for s in small], jnp.pad(loss.reshape(-1)[:1], (0, 1023)))
    wp, mp, vp = (pack([s[k] for s in small], zeros) for k in (1, 2, 3))
    gsum, dsm, nms, nvs = _adam_small(wp, mp, vp, _gather_small(gp))
    small_out, off = {}, 0
    for name, w, _, _, _ in small:
        n = w.size
        small_out[name] = tuple(t.reshape(-1)[off:off + n].reshape(w.shape) for t in (gsum, dsm, nms, nvs))
        off += n
    total_loss = gsum.reshape(-1)[off]

    order = ["g_mix", "w_in", "b_gate", "w_sgu_spatial", "b_sgu_spatial", "g_sgu", "w_branch_attn", "w_branch_sgu",
             "w_out", "g_cross", "g_mem", "w_q_cross", "w_kv_cross", "w_o_cross", "g_ffn", "w_gate_up", "w_down",
             "g_final"]
    res = {**big_out, **small_out}
    outs = [total_loss, grad_x.reshape(x.shape)]
    for k in range(4):
        outs += [res[nm][k] for nm in order]
    return tuple(outs)
```

```python
import math

import numpy as np
import jax
import jax.numpy as jnp
from jax import lax
from jax.experimental import pallas as pl
from jax.experimental.pallas import tpu as pltpu

F32, BF16 = jnp.float32, jnp.bfloat16
MESH = pl.DeviceIdType.MESH
ANY = pl.BlockSpec(memory_space=pl.ANY)
RES = pl.BlockSpec(memory_space=pltpu.VMEM)

D = 1024
HEAD = 64
GROUP_W = 256
DIL_GROUPS = ((128, 1), (512, 4), (2048, 16))
BLK = 128
SGU_W = 512
MEM_HEADS, MEM_HD, MEM_W = 4, 128, 512
D_FF = 2816
FF_CHUNK = 256
EPS = 1e-6
NEG = -1e30
LR, B1, B2, AEPS, WD, STEP = 0.001, 0.9, 0.999, 1e-08, 0.01, 10
GELU_K, GELU_C = 0.7978845608028654, 0.044715


def _dot(a, b):
    return jnp.dot(a, b, preferred_element_type=F32)


def _dot_nt(a, b):
    return lax.dot_general(a, b, (((1,), (1,)), ((), ())), preferred_element_type=F32)


def _dot_tn(a, b):
    return lax.dot_general(a, b, (((0,), (0,)), ((), ())), preferred_element_type=F32)


def _row(tm, w):
    return pl.BlockSpec((tm, w), lambda i: (i, 0))


def _acc(shape):
    return pl.BlockSpec(shape, lambda i: (0,) * len(shape))


def _params(sem, mb):
    return pltpu.CompilerParams(dimension_semantics=sem, vmem_limit_bytes=mb << 20)


def _sds(shape, dt):
    return jax.ShapeDtypeStruct(shape, dt)


def _rms(h):
    return lax.rsqrt(jnp.mean(h * h, axis=-1, keepdims=True) + EPS)


def _rms_bwd(dy, h, r, g):
    t = dy * g
    dh = r * t - h * (r * r * r) * jnp.mean(t * h, axis=-1, keepdims=True)
    return dh, dy * h * r


def _gelu(x):
    t = jnp.tanh(GELU_K * (x + GELU_C * x * x * x))
    return 0.5 * x * (1.0 + t), t


def _gelu_grad(x, t):
    return 0.5 * (1.0 + t) + 0.5 * x * (1.0 - t * t) * GELU_K * (1.0 + 3.0 * GELU_C * x * x)


def _alibi_slopes():
    def pow2(n):
        start = 2.0 ** (-8.0 / n)
        return [start ** (i + 1) for i in range(n)]
    n = 12
    c = 2 ** int(math.floor(math.log2(n)))
    s = pow2(c) + pow2(2 * c)[0::2][: n - c]
    return np.array(sorted(s, reverse=True), dtype=np.float32).reshape(3, 4)


def _attn_bias(g):
    win, dil = DIL_GROUPS[g]
    steps = (np.arange(BLK)[:, None] + BLK) - np.arange(2 * BLK)[None, :]
    valid = (steps >= 0) & (steps <= win // dil)
    dist = (np.clip(steps, 0, None) * dil).astype(np.float32)
    b = -_alibi_slopes()[g][:, None, None] * dist[None]
    return np.where(valid[None], b, NEG).astype(np.float32)


def _head_masks():
    lane = lax.broadcasted_iota(jnp.int32, (1, GROUP_W), 1)
    return lane, [(lane >= HEAD * h) & (lane < HEAD * (h + 1)) for h in range(4)]


ATT_NB = 4


def _stack_heads(t, masks):
    z = jnp.zeros_like(t)
    return jnp.concatenate([jnp.where(m, t, z) for m in masks], axis=0)


def _unstack_heads(t, masks):
    out = jnp.zeros((BLK, GROUP_W), t.dtype)
    for h, m in enumerate(masks):
        out = jnp.where(m, t[h * BLK:(h + 1) * BLK], out)
    return out


def _stack_cols(ref, rows):
    return jnp.concatenate([ref[rows, HEAD * h:HEAD * h + 1] for h in range(4)], axis=0)


def _dil_spec(d, tm, w):
    return pl.BlockSpec((d, tm // d, w), lambda i: (0, i, 0))


def _to_dilated(val, s_ref, d, write):
    tm, w = val.shape
    for j in range(w // 128):
        s_ref[j, pl.ds(0, tm), :] = val[:, j * 128:(j + 1) * 128]
    for r in range(d):
        for j in range(w // 128):
            write(r, j, s_ref[j, pl.ds(r, tm // d, stride=d), :])


def _from_dilated(ref, s_ref, d, tm, w):
    if d == 1:
        return ref[0].astype(F32)
    for r in range(d):
        for j in range(w // 128):
            s_ref[j, pl.ds(r, tm // d, stride=d), :] = ref[r, :, j * 128:(j + 1) * 128].astype(F32)
    return jnp.concatenate([s_ref[j, pl.ds(0, tm), :] for j in range(w // 128)], axis=1)


def _fwd_in(x, g_mix, wt_in, tm=512):
    S = x.shape[0]
    dils = [d for _, d in DIL_GROUPS]

    def body(x_ref, g_ref, w_ref, a_ref, q0_ref, q1_ref, q2_ref, uv_ref, gl_ref, s_ref):
        xv = x_ref[...]
        a = (xv * _rms(xv) * g_ref[...]).astype(BF16)
        a_ref[...] = a
        for g, (d, out) in enumerate(zip(dils, (q0_ref, q1_ref, q2_ref))):
            for part in range(3):
                rows = part * 768 + g * 256
                val = _dot_nt(a, w_ref[rows:rows + 256, :])
                if d == 1:
                    out[0, :, part * 256:(part + 1) * 256] = val.astype(BF16)
                else:
                    def write(r, j, piece, out=out, part=part):
                        out[r, :, part * 256 + j * 128:part * 256 + (j + 1) * 128] = piece.astype(BF16)
                    _to_dilated(val, s_ref, d, write)
        uv_ref[...] = _dot_nt(a, w_ref[2304:3328, :]).astype(BF16)
        gl_ref[...] = _dot_nt(a, w_ref[3328:5376, :]).astype(BF16)

    return pl.pallas_call(
        body, grid=(S // tm,), name="fwd_in",
        in_specs=[_row(tm, D), RES, RES],
        out_specs=[_row(tm, D)] + [_dil_spec(d, tm, 768) for d in dils] + [_row(tm, 1024), _row(tm, 2048)],
        out_shape=[_sds((S, D), BF16)] + [_sds((d, S // d, 768), BF16) for d in dils]
        + [_sds((S, 1024), BF16), _sds((S, 2048), BF16)],
        scratch_shapes=[pltpu.VMEM((2, tm, 128), F32)],
        compiler_params=_params(("parallel",), 56),
    )(x, g_mix, wt_in)


def _attn_fwd(qkv, g):
    d, L, _ = qkv.shape
    nb = L // BLK
    bias = jnp.asarray(_attn_bias(g).reshape(4 * BLK, 2 * BLK))
    NB = min(ATT_NB, nb)
    W = NB * BLK

    def body(q_ref, kc_ref, kp_ref, vc_ref, vp_ref, b_ref, o_ref, l_ref):
        st = pl.program_id(1)
        k_all = jnp.concatenate([kp_ref[...], kc_ref[...]], axis=0)
        v_all = jnp.concatenate([vp_ref[...], vc_ref[...]], axis=0)
        lane, masks = _head_masks()
        for b in range(NB):
            rows = slice(b * BLK, (b + 1) * BLK)
            kk, vv = k_all[b * BLK:(b + 2) * BLK], v_all[b * BLK:(b + 2) * BLK]
            s = _dot_nt(_stack_heads(q_ref[rows, :], masks), kk) * 0.125 + b_ref[...]
            if b == 0:
                s = s + jnp.where((st == 0) & (lane < BLK), NEG, 0.0).astype(F32)
            mx = jnp.max(s, axis=-1, keepdims=True)
            e = jnp.exp(s - mx)
            den = jnp.sum(e, axis=-1, keepdims=True)
            o_ref[rows, :] = _unstack_heads(_dot(e.astype(BF16), vv) / den, masks)
            l_ref[rows, :] = _unstack_heads(mx + jnp.log(den), masks)

    def wide(col):
        return pl.BlockSpec((None, W, GROUP_W), lambda r, s: (r, s, col))

    def before(col):
        return pl.BlockSpec((None, BLK, GROUP_W), lambda r, s: (r, jnp.maximum(s * NB - 1, 0), col))

    return pl.pallas_call(
        body, grid=(d, nb // NB), name=f"attn_fwd_g{g}",
        in_specs=[wide(0), wide(1), before(1), wide(2), before(2),
                  pl.BlockSpec((4 * BLK, 2 * BLK), lambda r, s: (0, 0))],
        out_specs=[wide(0), wide(0)],
        out_shape=[_sds((d, L, GROUP_W), F32), _sds((d, L, GROUP_W), F32)],
        compiler_params=_params(("parallel", "parallel"), 32),
    )(qkv, qkv, qkv, qkv, qkv, bias)


def _group_weights(l0, l1, l2):
    m = jnp.maximum(jnp.maximum(l0, l1), l2)
    e0, e1, e2 = jnp.exp(l0 - m), jnp.exp(l1 - m), jnp.exp(l2 - m)
    inv = 1.0 / (e0 + e1 + e2)
    return e0 * inv, e1 * inv, e2 * inv


def _sgu_forward(uvf, gs, wt_ref, bst_ref, mixed_s, tm):
    z, t = _gelu(uvf)
    u, v = z[:, :SGU_W], z[:, SGU_W:]
    rv = _rms(v)
    vnb = (v * rv * gs).astype(BF16)
    for ci in range(tm // 128):
        for g in range(4):
            rs, cs = slice(ci * 128, (ci + 1) * 128), slice(g * 128, (g + 1) * 128)
            mixed_s[rs, cs] = _dot(wt_ref[g], vnb[rs, cs]) + bst_ref[:, g:g + 1]
    return u, v, rv, vnb, t


def _fwd_mid(x, os_, ls_, uv, gl, wt, bst, g_sgu, b_gate, wt_ba, wt_bs, w_out, tm=256):
    S = x.shape[0]
    dils = [d for _, d in DIL_GROUPS]

    def body(x_ref, o0, o1, o2, l0, l1, l2, uv_ref, gl_ref, wt_ref, bst_ref, gs_ref, bg_ref, wba_ref, wbs_ref, wo_ref,
             ya_ref, ys_ref, ba_ref, bs_ref, mg_ref, h1_ref, mixed_s, il_s):
        ls = [_from_dilated(r, il_s, d, tm, GROUP_W) for r, d in zip((l0, l1, l2), dils)]
        alphas = _group_weights(*ls)
        ya = jnp.zeros((tm, GROUP_W), F32)
        for a, r, d in zip(alphas, (o0, o1, o2), dils):
            ya = ya + a * _from_dilated(r, il_s, d, tm, GROUP_W)
        yab = ya.astype(BF16)
        ya_ref[...] = yab
        u, _, _, _, _ = _sgu_forward(uv_ref[...].astype(F32), gs_ref[...], wt_ref, bst_ref, mixed_s, tm)
        ysb = (u * mixed_s[...]).astype(BF16)
        ys_ref[...] = ysb
        gates = jax.nn.sigmoid(gl_ref[...].astype(F32) + bg_ref[...])
        ba = _dot_nt(yab, wba_ref[...])
        bs = _dot_nt(ysb, wbs_ref[...])
        ba_ref[...] = ba.astype(BF16)
        bs_ref[...] = bs.astype(BF16)
        mgb = (gates[:, :D] * ba + gates[:, D:] * bs).astype(BF16)
        mg_ref[...] = mgb
        h1_ref[...] = x_ref[...] + _dot(mgb, wo_ref[...])

    gw = _row(tm, GROUP_W)
    dil = [_dil_spec(d, tm, GROUP_W) for d in dils]
    return pl.pallas_call(
        body, grid=(S // tm,), name="fwd_mid",
        in_specs=[_row(tm, D)] + dil + dil + [_row(tm, 1024), _row(tm, 2048)] + [RES] * 7,
        out_specs=[gw, _row(tm, SGU_W), _row(tm, D), _row(tm, D), _row(tm, D), _row(tm, D)],
        out_shape=[_sds((S, GROUP_W), BF16), _sds((S, SGU_W), BF16), _sds((S, D), BF16), _sds((S, D), BF16),
                   _sds((S, D), BF16), _sds((S, D), F32)],
        scratch_shapes=[pltpu.VMEM((tm, SGU_W), F32), pltpu.VMEM((2, tm, 128), F32)],
        compiler_params=_params(("parallel",), 48),
    )(x, *os_, *ls_, uv, gl, wt, bst, g_sgu, b_gate, wt_ba, wt_bs, w_out)


def _mem_fwd(mem, g_mem, w_kv):
    def body(m_ref, g_ref, w_ref, mb_ref, kv_ref):
        mv = m_ref[...]
        mb = (mv * _rms(mv) * g_ref[...]).astype(BF16)
        mb_ref[...] = mb
        kv_ref[...] = _dot(mb, w_ref[...]).astype(BF16)

    return pl.pallas_call(
        body, name="mem_fwd", in_specs=[RES, RES, RES], out_specs=[RES, RES],
        out_shape=[_sds(mem.shape, BF16), _sds((mem.shape[0], 2 * MEM_W), BF16)],
        compiler_params=pltpu.CompilerParams(vmem_limit_bytes=32 << 20),
    )(mem, g_mem, w_kv)


def _cross_probs(qh, kh):
    s = _dot_nt(qh, kh) * (MEM_HD ** -0.5)
    e = jnp.exp(s - jnp.max(s, axis=-1, keepdims=True))
    return e / jnp.sum(e, axis=-1, keepdims=True)


def _fwd_cross(h1, g_cross, w_q, kv, wt_o, tm=512):
    S = h1.shape[0]

    def body(h_ref, g_ref, wq_ref, kv_ref, wo_ref, c_ref, qc_ref, oc_ref, h2_ref):
        hv = h_ref[...]
        cb = (hv * _rms(hv) * g_ref[...]).astype(BF16)
        c_ref[...] = cb
        qcb = _dot(cb, wq_ref[...]).astype(BF16)
        qc_ref[...] = qcb
        for h in range(MEM_HEADS):
            cs = slice(h * MEM_HD, (h + 1) * MEM_HD)
            p = _cross_probs(qcb[:, cs], kv_ref[:, cs])
            oc_ref[:, cs] = _dot(p.astype(BF16), kv_ref[:, MEM_W + h * MEM_HD:MEM_W + (h + 1) * MEM_HD]).astype(BF16)
        h2_ref[...] = hv + _dot_nt(oc_ref[...], wo_ref[...])

    return pl.pallas_call(
        body, grid=(S // tm,), name="fwd_cross",
        in_specs=[_row(tm, D), RES, RES, RES, RES],
        out_specs=[_row(tm, D), _row(tm, MEM_W), _row(tm, MEM_W), _row(tm, D)],
        out_shape=[_sds((S, D), BF16), _sds((S, MEM_W), BF16), _sds((S, MEM_W), BF16), _sds((S, D), F32)],
        compiler_params=_params(("parallel",), 40),
    )(h1, g_cross, w_q, kv, wt_o)


def _ffn_fwd_bwd(h2, target, g_ffn, g_final, wt_gu, w_down, tm=256):
    S = h2.shape[0]
    nch = D_FF // FF_CHUNK

    def body(h_ref, t_ref, gf_ref, gz_ref, wgu_ref, wd_ref,
             f_ref, act_ref, dgu_ref, dh3b_ref, dh2_ref, dh2b_ref, dgf_ref, dgz_ref, loss_ref, gu_s):
        i = pl.program_id(0)
        hv = h_ref[...]
        r2 = _rms(hv)
        gf = gf_ref[...]
        fb = (hv * r2 * gf).astype(BF16)
        f_ref[...] = fb
        h3 = hv
        for c in range(nch):
            cs = slice(c * FF_CHUNK, (c + 1) * FF_CHUNK)
            us = slice(D_FF + c * FF_CHUNK, D_FF + (c + 1) * FF_CHUNK)
            gt = _dot_nt(fb, wgu_ref[cs, :])
            up = _dot_nt(fb, wgu_ref[us, :])
            gu_s[:, cs] = gt
            gu_s[:, us] = up
            actb = (gt * jax.nn.sigmoid(gt) * up).astype(BF16)
            act_ref[:, cs] = actb
            h3 = h3 + _dot(actb, wd_ref[cs, :])
        r3 = _rms(h3)
        gz = gz_ref[...]
        diff = h3 * r3 * gz - t_ref[...]
        dy = diff * (1.0 / D)
        dh3, dgz_rows = _rms_bwd(dy, h3, r3, gz)
        dh3b = dh3.astype(BF16)
        dh3b_ref[...] = dh3b
        df = jnp.zeros((tm, D), F32)
        for c in range(nch):
            cs = slice(c * FF_CHUNK, (c + 1) * FF_CHUNK)
            us = slice(D_FF + c * FF_CHUNK, D_FF + (c + 1) * FF_CHUNK)
            dact = _dot_nt(dh3b, wd_ref[cs, :])
            gt, up = gu_s[:, cs], gu_s[:, us]
            sg = jax.nn.sigmoid(gt)
            dgt = (dact * up * (sg * (1.0 + gt * (1.0 - sg)))).astype(BF16)
            dup = (dact * (gt * sg)).astype(BF16)
            dgu_ref[:, cs] = dgt
            dgu_ref[:, us] = dup
            df = df + _dot(dgt, wgu_ref[cs, :]) + _dot(dup, wgu_ref[us, :])
        dhn, dgf_rows = _rms_bwd(df, hv, r2, gf)
        dh2 = dh3 + dhn
        dh2_ref[...] = dh2
        dh2b_ref[...] = dh2.astype(BF16)

        @pl.when(i == 0)
        def _():
            dgf_ref[...] = jnp.zeros_like(dgf_ref)
            dgz_ref[...] = jnp.zeros_like(dgz_ref)
            loss_ref[...] = jnp.zeros_like(loss_ref)

        dgf_ref[...] += jnp.sum(dgf_rows, axis=0, keepdims=True)
        dgz_ref[...] += jnp.sum(dgz_rows, axis=0, keepdims=True)
        loss_ref[...] += jnp.sum(jnp.sum(diff * diff, axis=0, keepdims=True), axis=1, keepdims=True) * (0.5 / D)

    return pl.pallas_call(
        body, grid=(S // tm,), name="ffn_fwd_bwd",
        in_specs=[_row(tm, D), _row(tm, D), RES, RES, RES, RES],
        out_specs=[_row(tm, D), _row(tm, D_FF), _row(tm, 2 * D_FF), _row(tm, D), _row(tm, D), _row(tm, D),
                   _acc((1, D)), _acc((1, D)), _acc((1, 128))],
        out_shape=[_sds((S, D), BF16), _sds((S, D_FF), BF16), _sds((S, 2 * D_FF), BF16), _sds((S, D), BF16),
                   _sds((S, D), F32), _sds((S, D), BF16), _sds((1, D), F32), _sds((1, D), F32), _sds((1, 128), F32)],
        scratch_shapes=[pltpu.VMEM((tm, 2 * D_FF), F32)],
        compiler_params=_params(("arbitrary",), 56),
    )(h2, target, g_ffn, g_final, wt_gu, w_down)


def _bwd_cross(dh2, h1, qc, g_cross, w_q, kv, wt_o, tm=256):
    S = h1.shape[0]

    def body(d_ref, h_ref, qc_ref, g_ref, wq_ref, kv_ref, wo_ref, dqc_ref, dh1_ref, dh1b_ref, dkv_ref, dg_ref):
        i = pl.program_id(0)

        @pl.when(i == 0)
        def _():
            dkv_ref[...] = jnp.zeros_like(dkv_ref)
            dg_ref[...] = jnp.zeros_like(dg_ref)

        dh2 = d_ref[...]
        doc = _dot(dh2.astype(BF16), wo_ref[...])
        qcb = qc_ref[...]
        for h in range(MEM_HEADS):
            cs = slice(h * MEM_HD, (h + 1) * MEM_HD)
            vs = slice(MEM_W + h * MEM_HD, MEM_W + (h + 1) * MEM_HD)
            qh, kh, vh = qcb[:, cs], kv_ref[:, cs], kv_ref[:, vs]
            p = _cross_probs(qh, kh)
            dohb = doc[:, cs].astype(BF16)
            dp = _dot_nt(dohb, vh)
            dsb = (p * (dp - jnp.sum(dp * p, axis=-1, keepdims=True)) * (MEM_HD ** -0.5)).astype(BF16)
            dqc_ref[:, cs] = _dot(dsb, kh).astype(BF16)
            dkv_ref[:, cs] += _dot_tn(dsb, qh)
            dkv_ref[:, vs] += _dot_tn(p.astype(BF16), dohb)
        dc = _dot_nt(dqc_ref[...], wq_ref[...])
        hv = h_ref[...]
        dhn, dg_rows = _rms_bwd(dc, hv, _rms(hv), g_ref[...])
        dh1 = dh2 + dhn
        dh1_ref[...] = dh1
        dh1b_ref[...] = dh1.astype(BF16)
        dg_ref[...] += jnp.sum(dg_rows, axis=0, keepdims=True)

    return pl.pallas_call(
        body, grid=(S // tm,), name="bwd_cross",
        in_specs=[_row(tm, D), _row(tm, D), _row(tm, MEM_W), RES, RES, RES, RES],
        out_specs=[_row(tm, MEM_W), _row(tm, D), _row(tm, D), _acc((256, 2 * MEM_W)), _acc((1, D))],
        out_shape=[_sds((S, MEM_W), BF16), _sds((S, D), F32), _sds((S, D), BF16), _sds((256, 2 * MEM_W), F32),
                   _sds((1, D), F32)],
        compiler_params=_params(("arbitrary",), 40),
    )(dh2, h1, qc, g_cross, w_q, kv, wt_o)


def _mem_bwd(dkv, mem, mb, g_mem, w_kv):
    def body(dkv_ref, m_ref, mb_ref, g_ref, w_ref, dw_ref, dg_ref):
        dkvb = dkv_ref[...].astype(BF16)
        dw_ref[...] = _dot_tn(mb_ref[...], dkvb)
        dm = _dot_nt(dkvb, w_ref[...])
        mv = m_ref[...]
        dg_ref[...] = jnp.sum(dm * mv * _rms(mv), axis=0, keepdims=True)

    return pl.pallas_call(
        body, name="mem_bwd", in_specs=[RES] * 5, out_specs=[RES, RES],
        out_shape=[_sds((D, 2 * MEM_W), F32), _sds((1, D), F32)],
        compiler_params=pltpu.CompilerParams(vmem_limit_bytes=32 << 20),
    )(dkv, mem, mb, g_mem, w_kv)


def _bwd_mid(dh1, gl, ba, bs, uv, ls_, ya, wt, bst, g_sgu, b_gate, wt_ba, wt_bs, w_out, tm=256):
    S = dh1.shape[0]
    dils = [d for _, d in DIL_GROUPS]

    def body(d_ref, gl_ref, ba_ref, bs_ref, uv_ref, l0, l1, l2, ya_ref,
             wt_ref, bst_ref, gs_ref, bg_ref, wba_ref, wbs_ref, wo_ref,
             dba_ref, dbs_ref, dgl_ref, duv_ref, do0, do1, do2, c0, c1, c2,
             dbg_ref, dgs_ref, dws_ref, dbsa_ref, mixed_s, dvn_s, il_s):
        i = pl.program_id(0)

        @pl.when(i == 0)
        def _():
            for r in (dbg_ref, dgs_ref, dws_ref, dbsa_ref):
                r[...] = jnp.zeros_like(r)

        dm = _dot_nt(d_ref[...].astype(BF16), wo_ref[...])
        gates = jax.nn.sigmoid(gl_ref[...].astype(F32) + bg_ref[...])
        g0, g1 = gates[:, :D], gates[:, D:]
        dbab = (dm * g0).astype(BF16)
        dbsb = (dm * g1).astype(BF16)
        dba_ref[...] = dbab
        dbs_ref[...] = dbsb
        dg0 = dm * ba_ref[...].astype(F32) * g0 * (1.0 - g0)
        dg1 = dm * bs_ref[...].astype(F32) * g1 * (1.0 - g1)
        dgl_ref[:, :D] = dg0.astype(BF16)
        dgl_ref[:, D:] = dg1.astype(BF16)
        dbg_ref[:, :D] += jnp.sum(dg0, axis=0, keepdims=True)
        dbg_ref[:, D:] += jnp.sum(dg1, axis=0, keepdims=True)
        dya = _dot(dbab, wba_ref[...])
        dys = _dot(dbsb, wbs_ref[...])

        uvf = uv_ref[...].astype(F32)
        gs = gs_ref[...]
        u, v, rv, vnb, t = _sgu_forward(uvf, gs, wt_ref, bst_ref, mixed_s, tm)
        du = dys * mixed_s[...]
        dmixed = dys * u
        for ci in range(tm // 128):
            for g in range(4):
                rs, cs = slice(ci * 128, (ci + 1) * 128), slice(g * 128, (g + 1) * 128)
                dmx = dmixed[rs, cs]
                dmxb = dmx.astype(BF16)
                dvn_s[rs, cs] = _dot_tn(wt_ref[g], dmxb)
                dws_ref[g] += _dot_nt(dmxb, vnb[rs, cs])
                dbsa_ref[g] += dmx
        dv, dgs_rows = _rms_bwd(dvn_s[...], v, rv, gs)
        dgs_ref[...] += jnp.sum(dgs_rows, axis=0, keepdims=True)
        gg = _gelu_grad(uvf, t)
        duv_ref[:, :SGU_W] = (du * gg[:, :SGU_W]).astype(BF16)
        duv_ref[:, SGU_W:] = (dv * gg[:, SGU_W:]).astype(BF16)

        alphas = _group_weights(*[_from_dilated(r, il_s, d, tm, GROUP_W) for r, d in zip((l0, l1, l2), dils)])
        prod = dya * ya_ref[...].astype(F32)
        _, masks = _head_masks()
        hs = jnp.zeros_like(prod)
        for h in range(4):
            sh = jnp.sum(jnp.where(masks[h], prod, 0.0), axis=-1, keepdims=True)
            hs = jnp.where(masks[h], sh, hs)
        for a, d, do_ref, c_ref in zip(alphas, dils, (do0, do1, do2), (c0, c1, c2)):
            for val, out in ((a * dya, do_ref), (a * hs, c_ref)):
                if d == 1:
                    out[0] = val.astype(out.dtype)
                else:
                    def write(r, j, piece, out=out):
                        out[r, :, j * 128:(j + 1) * 128] = piece.astype(out.dtype)
                    _to_dilated(val, il_s, d, write)

    gw = _row(tm, GROUP_W)
    dil = [_dil_spec(d, tm, GROUP_W) for d in dils]
    return pl.pallas_call(
        body, grid=(S // tm,), name="bwd_mid",
        in_specs=[_row(tm, D), _row(tm, 2048), _row(tm, D), _row(tm, D), _row(tm, 1024)] + dil + [gw] + [RES] * 7,
        out_specs=[_row(tm, D), _row(tm, D), _row(tm, 2048), _row(tm, 1024)] + dil + dil
        + [_acc((1, 2048)), _acc((1, SGU_W)), _acc((4, 128, 128)), _acc((4, 128, 128))],
        out_shape=[_sds((S, D), BF16), _sds((S, D), BF16), _sds((S, 2048), BF16), _sds((S, 1024), BF16)]
        + [_sds((d, S // d, GROUP_W), BF16) for d in dils] + [_sds((d, S // d, GROUP_W), F32) for d in dils]
        + [_sds((1, 2048), F32), _sds((1, SGU_W), F32), _sds((4, 128, 128), F32), _sds((4, 128, 128), F32)],
        scratch_shapes=[pltpu.VMEM((tm, SGU_W), F32), pltpu.VMEM((tm, SGU_W), F32), pltpu.VMEM((2, tm, 128), F32)],
        compiler_params=_params(("arbitrary",), 48),
    )(dh1, gl, ba, bs, uv, *ls_, ya, wt, bst, g_sgu, b_gate, wt_ba, wt_bs, w_out)


def _attn_bwd(qkv, do, lse, corr, g):
    d, L, _ = qkv.shape
    nb = L // BLK
    NB = min(ATT_NB, nb)
    W = NB * BLK
    nsteps = nb // NB
    bias = jnp.asarray(_attn_bias(g).reshape(4 * BLK, 2 * BLK))

    def body(q_ref, kc_ref, kp_ref, vc_ref, vp_ref, do_ref, l_ref, c_ref, qn_ref, don_ref, ln_ref, cn_ref, b_ref,
             out_ref, dk_s, dv_s):
        st = pl.program_id(1)
        k_all = jnp.concatenate([kp_ref[...], kc_ref[...]], axis=0)
        v_all = jnp.concatenate([vp_ref[...], vc_ref[...]], axis=0)
        lane, masks = _head_masks()
        dk_s[...] = jnp.zeros_like(dk_s)
        dv_s[...] = jnp.zeros_like(dv_s)

        def block_terms(qs, dos, kk, vv, bias_v, lse_c, corr_c):
            s = _dot_nt(qs, kk) * 0.125 + bias_v
            p = jnp.exp(s - lse_c)
            dsb = (p * (_dot_nt(dos, vv) - corr_c) * 0.125).astype(BF16)
            return dsb, p.astype(BF16)

        for b in range(NB):
            rows = slice(b * BLK, (b + 1) * BLK)
            keys = slice(b * BLK, (b + 2) * BLK)
            kk, vv = k_all[keys], v_all[keys]
            qs, dos = _stack_heads(q_ref[rows, :], masks), _stack_heads(do_ref[rows, :], masks)
            bias_v = b_ref[...]
            if b == 0:
                bias_v = bias_v + jnp.where((st == 0) & (lane < BLK), NEG, 0.0).astype(F32)
            dsb, pb = block_terms(qs, dos, kk, vv, bias_v, _stack_cols(l_ref, rows), _stack_cols(c_ref, rows))
            out_ref[rows, 0:GROUP_W] = _unstack_heads(_dot(dsb, kk), masks).astype(BF16)
            dk_s[keys, :] += _dot_tn(dsb, qs)
            dv_s[keys, :] += _dot_tn(pb, dos)

        @pl.when(st < nsteps - 1)
        def _():
            last = slice(NB * BLK, (NB + 1) * BLK)
            qs, dos = _stack_heads(qn_ref[...], masks), _stack_heads(don_ref[...], masks)
            every = slice(None)
            dsb, pb = block_terms(qs, dos, k_all[last], v_all[last], b_ref[:, :BLK],
                                  _stack_cols(ln_ref, every), _stack_cols(cn_ref, every))
            dk_s[last, :] += _dot_tn(dsb, qs)
            dv_s[last, :] += _dot_tn(pb, dos)

        out_ref[:, GROUP_W:2 * GROUP_W] = dk_s[BLK:, :].astype(BF16)
        out_ref[:, 2 * GROUP_W:] = dv_s[BLK:, :].astype(BF16)

    def wide(col, w=GROUP_W):
        return pl.BlockSpec((None, W, w), lambda r, s: (r, s, col))

    def before(col):
        return pl.BlockSpec((None, BLK, GROUP_W), lambda r, s: (r, jnp.maximum(s * NB - 1, 0), col))

    def after(col):
        return pl.BlockSpec((None, BLK, GROUP_W), lambda r, s: (r, jnp.minimum((s + 1) * NB, nb - 1), col))

    return pl.pallas_call(
        body, grid=(d, nsteps), name=f"attn_bwd_g{g}",
        in_specs=[wide(0), wide(1), before(1), wide(2), before(2), wide(0), wide(0), wide(0),
                  after(0), after(0), after(0), after(0), pl.BlockSpec((4 * BLK, 2 * BLK), lambda r, s: (0, 0))],
        out_specs=wide(0, 768),
        out_shape=_sds((d, L, 768), BF16),
        scratch_shapes=[pltpu.VMEM(((NB + 1) * BLK, GROUP_W), F32), pltpu.VMEM(((NB + 1) * BLK, GROUP_W), F32)],
        compiler_params=_params(("parallel", "parallel"), 32),
    )(qkv, qkv, qkv, qkv, qkv, do, lse, corr, qkv, do, lse, corr, bias)


def _bwd_in(dqkvs, duv, dgl, dh1, x, g_mix, wt_in, tm=512):
    S = x.shape[0]
    dils = [d for _, d in DIL_GROUPS]

    def body(q0_ref, q1_ref, q2_ref, duv_ref, dgl_ref, d_ref, x_ref, g_ref, w_ref, dx_ref, dp_ref, dg_ref, il_s):
        i = pl.program_id(0)

        @pl.when(i == 0)
        def _():
            dg_ref[...] = jnp.zeros_like(dg_ref)

        for g, (d, ref) in enumerate(zip(dils, (q0_ref, q1_ref, q2_ref))):
            nat = _from_dilated(ref, il_s, d, tm, 768).astype(BF16)
            for part in range(3):
                col = part * 768 + g * 256
                dp_ref[:, col:col + 256] = nat[:, part * 256:(part + 1) * 256]
        dp_ref[:, 2304:3328] = duv_ref[...]
        dp_ref[:, 3328:5376] = dgl_ref[...]
        da = _dot(dp_ref[...], w_ref[...])
        xv = x_ref[...]
        dxn, dg_rows = _rms_bwd(da, xv, _rms(xv), g_ref[...])
        dx_ref[...] = d_ref[...] + dxn
        dg_ref[...] += jnp.sum(dg_rows, axis=0, keepdims=True)

    return pl.pallas_call(
        body, grid=(S // tm,), name="bwd_in",
        in_specs=[_dil_spec(d, tm, 768) for d in dils] + [_row(tm, 1024), _row(tm, 2048), _row(tm, D), _row(tm, D),
                                                          RES, RES],
        out_specs=[_row(tm, D), _row(tm, 5376), _acc((1, D))],
        out_shape=[_sds((S, D), F32), _sds((S, 5376), BF16), _sds((1, D), F32)],
        scratch_shapes=[pltpu.VMEM((6, tm, 128), F32)],
        compiler_params=_params(("arbitrary",), 60),
    )(*dqkvs, duv, dgl, dh1, x, g_mix, wt_in)


def _tn_matmul(a, b, name, tk, ts=2048, into=None):
    S, K = a.shape
    N = b.shape[1]
    off = 0 if into is None else into[1] // tk

    def body(*refs):
        a_ref, b_ref, o_ref = refs[0], refs[1], refs[-1]

        @pl.when(pl.program_id(1) == 0)
        def _():
            o_ref[...] = jnp.zeros_like(o_ref)

        o_ref[...] += _dot_tn(a_ref[...], b_ref[...])

    ins = [a, b]
    in_specs = [pl.BlockSpec((ts, tk), lambda k, s: (s, k)), pl.BlockSpec((ts, N), lambda k, s: (s, 0))]
    out_shape = _sds((K, N), F32)
    aliases = {}
    if into is not None:
        ins.append(into[0])
        in_specs.append(ANY)
        out_shape = _sds(into[0].shape, F32)
        aliases = {2: 0}
    return pl.pallas_call(
        body, grid=(K // tk, S // ts), name=name, in_specs=in_specs,
        out_specs=pl.BlockSpec((tk, N), lambda k, s: (k + off, 0)), out_shape=out_shape,
        input_output_aliases=aliases,
        compiler_params=_params(("parallel", "arbitrary"), 48),
    )(*ins)


def _chip_peers(x, y):
    return [(1 - x, y), (x, 1 - y), (1 - x, 1 - y)]


STAGE_BYTES = 2 << 20


def _chunk_plan(shapes, itemsize):
    plan = []
    for i, (rows, w) in enumerate(shapes):
        ch = max(16, min(rows, (STAGE_BYTES // (w * itemsize)) // 16 * 16))
        while rows % ch:
            ch -= 16
        plan += [(i, r0, ch) for r0 in range(0, rows, ch)]
    return plan


def _remote(src, dst, ssem, rsem, dev):
    return pltpu.make_async_remote_copy(src_ref=src, dst_ref=dst, send_sem=ssem, recv_sem=rsem, device_id=dev,
                                        device_id_type=MESH)


def _gather_weights(shards):
    n = len(shards)
    halves = [s.reshape(2, s.shape[0] // 2, s.shape[1]) for s in shards]
    plan = _chunk_plan([h.shape[1:] for h in halves], 2)

    def body(*refs):
        ins, outs = refs[:n], refs[n:2 * n]
        ssem, rsem, lsem, osem, buf = refs[2 * n:]
        x, y, c = lax.axis_index("x"), lax.axis_index("y"), lax.axis_index("c")
        me = 2 * x + y
        chips = _chip_peers(x, y)
        sib = (x, y, 1 - c)
        sends = []
        for i in range(n):
            for k, (px, py) in enumerate(chips):
                cp = _remote(ins[i].at[c], outs[i].at[me, c], ssem.at[6 * i + k], rsem.at[6 * i + k], (px, py, c))
                cp.start()
                sends.append(cp)
        pending = {}
        for k, (i, r0, ch) in enumerate(plan):
            for h in range(2):
                slot = (2 * k + h) % 2
                if slot in pending:
                    pending[slot].wait()
                stage = buf.at[slot, pl.ds(0, ch), pl.ds(0, halves[i].shape[2])]
                ld = pltpu.make_async_copy(ins[i].at[h, pl.ds(r0, ch)], stage, lsem.at[slot])
                ld.start()
                ld.wait()
                st = pltpu.make_async_copy(stage, outs[i].at[me, h, pl.ds(r0, ch)], osem.at[slot])
                st.start()
                pending[slot] = st
        for st in pending.values():
            st.wait()
        for i in range(n):
            for k, (px, py) in enumerate(chips):
                landed = outs[i].at[2 * px + py, c]
                _remote(landed, landed, ssem.at[6 * i + k], rsem.at[6 * i + k], (px, py, c)).wait_recv()
                cp = _remote(landed, landed, ssem.at[6 * i + 3 + k], rsem.at[6 * i + 3 + k], sib)
                cp.start()
                sends.append(cp)
        for i in range(n):
            for k, (px, py) in enumerate(chips):
                passed = outs[i].at[2 * px + py, 1 - c]
                _remote(passed, passed, ssem.at[6 * i + 3 + k], rsem.at[6 * i + 3 + k], sib).wait_recv()
        for cp in sends:
            cp.wait_send()

    outs = pl.pallas_call(
        body, name="gather_weights", in_specs=[ANY] * n, out_specs=[ANY] * n,
        out_shape=[_sds((4,) + h.shape, BF16) for h in halves],
        scratch_shapes=[pltpu.SemaphoreType.DMA((6 * n,)), pltpu.SemaphoreType.DMA((6 * n,)),
                        pltpu.SemaphoreType.DMA((2,)), pltpu.SemaphoreType.DMA((2,)),
                        pltpu.VMEM((2, max(p[2] for p in plan), max(h.shape[2] for h in halves)), BF16)],
        compiler_params=pltpu.CompilerParams(vmem_limit_bytes=32 << 20),
    )(*halves)
    return [o.reshape(4 * s.shape[0], s.shape[1]) for o, s in zip(outs, shards)]


def _swap_halves(grads):
    n = len(grads)
    g4 = [g.reshape(4, 2, g.shape[0] // 8, g.shape[1]) for g in grads]

    def body(*refs):
        ins, got = refs[:n], refs[n:2 * n]
        ssem, rsem = refs[2 * n:]
        x, y, c = lax.axis_index("x"), lax.axis_index("y"), lax.axis_index("c")
        sib = (x, y, 1 - c)
        cps = []
        for i in range(n):
            rc = _remote(ins[i].at[:, 1 - c], got[i], ssem.at[i], rsem.at[i], sib)
            rc.start()
            cps.append(rc)
        for cp in cps:
            cp.wait()

    half = [_sds((4, g.shape[2], g.shape[3]), F32) for g in g4]
    got = pl.pallas_call(
        body, name="swap_halves", in_specs=[ANY] * n, out_specs=[ANY] * n, out_shape=half,
        scratch_shapes=[pltpu.SemaphoreType.DMA((n,)), pltpu.SemaphoreType.DMA((n,))],
    )(*g4)
    c = lax.axis_index("c")
    own = [lax.dynamic_index_in_dim(g, c, axis=1, keepdims=False) for g in g4]
    return own, got


def _scatter_partials(sums, sums_b):
    n = len(sums)

    def body(*refs):
        hbs, parts = refs[:n], refs[n:2 * n]
        ssem, rsem = refs[2 * n:]
        x, y, c = lax.axis_index("x"), lax.axis_index("y"), lax.axis_index("c")
        cps = []
        for i in range(n):
            for k, (px, py) in enumerate(_chip_peers(x, y)):
                rc = _remote(hbs[i].at[2 * px + py], parts[i].at[k], ssem.at[3 * i + k], rsem.at[3 * i + k], (px, py, c))
                rc.start()
                cps.append(rc)
        for cp in cps:
            cp.wait()

    parts = pl.pallas_call(
        body, name="scatter_partials", in_specs=[ANY] * n, out_specs=[ANY] * n,
        out_shape=[_sds((3,) + s.shape[1:], BF16) for s in sums],
        scratch_shapes=[pltpu.SemaphoreType.DMA((3 * n,)), pltpu.SemaphoreType.DMA((3 * n,))],
    )(*sums_b)
    me = 2 * lax.axis_index("x") + lax.axis_index("y")
    mine = [lax.dynamic_index_in_dim(s, me, axis=0, keepdims=False) for s in sums]
    return mine, parts


def _share_halves(reduced):
    n = len(reduced)
    plan = _chunk_plan([r.shape for r in reduced], 4)
    max_rows = max(p[2] for p in plan)
    max_w = max(r.shape[1] for r in reduced)

    def body(*refs):
        ins, outs = refs[:n], refs[n:2 * n]
        ssem, rsem, lsem, osem, buf = refs[2 * n:]
        x, y, c = lax.axis_index("x"), lax.axis_index("y"), lax.axis_index("c")
        sib = (x, y, 1 - c)
        pending = {}
        for k, (i, r0, ch) in enumerate(plan):
            slot = k % 2
            if slot in pending:
                rc, lc = pending[slot]
                rc.wait_send()
                lc.wait()
            stage = buf.at[slot, pl.ds(0, ch), pl.ds(0, reduced[i].shape[1])]
            ld = pltpu.make_async_copy(ins[i].at[pl.ds(r0, ch)], stage, lsem.at[slot])
            ld.start()
            ld.wait()
            place = outs[i].at[c, pl.ds(r0, ch)]
            rc = _remote(stage, place, ssem.at[slot], rsem.at[i], sib)
            lc = pltpu.make_async_copy(stage, place, osem.at[slot])
            rc.start()
            lc.start()
            pending[slot] = (rc, lc)
        for rc, lc in pending.values():
            rc.wait_send()
            lc.wait()
        for i in range(n):
            theirs = outs[i].at[1 - c]
            _remote(theirs, theirs, ssem.at[0], rsem.at[i], sib).wait_recv()

    outs = pl.pallas_call(
        body, name="share_halves", in_specs=[ANY] * n, out_specs=[ANY] * n,
        out_shape=[_sds((2,) + r.shape, F32) for r in reduced],
        scratch_shapes=[pltpu.SemaphoreType.DMA((2,)), pltpu.SemaphoreType.DMA((n,)), pltpu.SemaphoreType.DMA((2,)),
                        pltpu.SemaphoreType.DMA((2,)), pltpu.VMEM((2, max_rows, max_w), F32)],
        compiler_params=pltpu.CompilerParams(vmem_limit_bytes=32 << 20),
    )(*reduced)
    return [o.reshape(2 * r.shape[0], r.shape[1]) for o, r in zip(outs, reduced)]


def _gather_small(pack):
    P = pack.shape[0]

    def body(in_ref, out_ref, ssem, rsem, lsem):
        x, y, c = lax.axis_index("x"), lax.axis_index("y"), lax.axis_index("c")
        me = 4 * x + 2 * y + c
        lc = pltpu.make_async_copy(in_ref, out_ref.at[me], lsem)
        lc.start()
        cps = []
        for rel in range(1, 8):
            px = 1 - x if rel & 4 else x
            py = 1 - y if rel & 2 else y
            pc = 1 - c if rel & 1 else c
            rc = _remote(in_ref, out_ref.at[me], ssem.at[rel - 1], rsem.at[rel - 1], (px, py, pc))
            rc.start()
            cps.append((rc, 4 * px + 2 * py + pc))
        for rel, (rc, peer) in enumerate(cps):
            rc.wait_send()
            _remote(in_ref, out_ref.at[peer], ssem.at[rel], rsem.at[rel], (x, y, c)).wait_recv()
        lc.wait()

    return pl.pallas_call(
        body, name="gather_small", in_specs=[ANY], out_specs=ANY, out_shape=_sds((8, P, 128), F32),
        scratch_shapes=[pltpu.SemaphoreType.DMA((7,)), pltpu.SemaphoreType.DMA((7,)), pltpu.SemaphoreType.DMA],
    )(pack)


def _tile(rows, cap=256):
    t = min(rows, cap) // 16 * 16
    while rows % t:
        t -= 16
    return t


def _elementwise(fn, ins, out_dtypes, name):
    R, W = ins[0].shape
    tr = _tile(R, max(8, min(512, (1 << 18) // W // 8 * 8)))

    def body(*refs):
        outs = fn(*[r[...] for r in refs[:len(ins)]])
        for o_ref, o in zip(refs[len(ins):], outs):
            o_ref[...] = o.astype(o_ref.dtype)

    return pl.pallas_call(
        body, grid=(R // tr,), name=name, in_specs=[_row(tr, W)] * len(ins), out_specs=[_row(tr, W)] * len(out_dtypes),
        out_shape=[_sds((R, W), dt) for dt in out_dtypes],
        compiler_params=_params(("parallel",), 48),
    )(*ins)


def _adamw(w, g, m, v):
    m = B1 * m + (1.0 - B1) * g
    v = B2 * v + (1.0 - B2) * (g * g)
    m_hat = m / (1.0 - B1 ** STEP)
    v_hat = v / (1.0 - B2 ** STEP)
    return -LR * (m_hat / (jnp.sqrt(v_hat) + AEPS) + WD * w), m, v


def _adam_small(w, m, v, packs):
    def body(w_ref, m_ref, v_ref, p_ref, g_ref, d_ref, nm_ref, nv_ref):
        g = p_ref[0]
        for k in range(1, 8):
            g = g + p_ref[k]
        g_ref[...] = g
        d_ref[...], nm_ref[...], nv_ref[...] = _adamw(w_ref[...], g, m_ref[...], v_ref[...])

    return pl.pallas_call(
        body, name="adam_small", in_specs=[RES] * 4, out_specs=[RES] * 4, out_shape=[_sds(w.shape, F32)] * 4,
        compiler_params=pltpu.CompilerParams(vmem_limit_bytes=32 << 20),
    )(w, m, v, packs)


def _local_step(xs, tgt, mems, weights, small):
    wt_in, wt_ba, wt_bs, wo, wq, wkv, wt_o, wt_gu, wd = weights
    g_mix, b_gate, w_sgu, b_sgu, g_sgu, g_cross, g_mem, g_ffn, g_final = small
    wt = jnp.tril(w_sgu).astype(BF16)
    bst = b_sgu.T

    a, qkv0, qkv1, qkv2, uv, gl = _fwd_in(xs, g_mix, wt_in)
    qkvs = (qkv0, qkv1, qkv2)
    os_, ls_ = zip(*[_attn_fwd(qkvs[g], g) for g in range(3)])
    ya, ys, ba, bs, mg, h1 = _fwd_mid(xs, os_, ls_, uv, gl, wt, bst, g_sgu, b_gate, wt_ba, wt_bs, wo)
    mb, kv = _mem_fwd(mems, g_mem, wkv)
    cb, qc, oc, h2 = _fwd_cross(h1, g_cross, wq, kv, wt_o)
    f, act, dgu, dh3b, dh2, dh2b, dg_ffn, dg_final, loss = _ffn_fwd_bwd(h2, tgt, g_ffn, g_final, wt_gu, wd)

    dqc, dh1, dh1b, dkv, dg_cross = _bwd_cross(dh2, h1, qc, g_cross, wq, kv, wt_o)
    dw_kv, dg_mem = _mem_bwd(dkv, mems, mb, g_mem, wkv)
    (dba, dbs, dgl, duv, do0, do1, do2, c0, c1, c2, db_gate, dg_sgu, dws, dbs_acc) = _bwd_mid(
        dh1, gl, ba, bs, uv, ls_, ya, wt, bst, g_sgu, b_gate, wt_ba, wt_bs, wo)
    dqkvs = [_attn_bwd(qkvs[g], do, ls_[g], corr, g) for g, (do, corr) in enumerate(((do0, c0), (do1, c1), (do2, c2)))]
    grad_x, dproj, dg_mix = _bwd_in(dqkvs, duv, dgl, dh1, xs, g_mix, wt_in)

    grads = [_tn_matmul(dproj, a, "dw_in", 768),
             _tn_matmul(dba, ya, "dw_branch_attn", 512),
             _tn_matmul(dbs, ys, "dw_branch_sgu", 512),
             _tn_matmul(mg, dh1b, "dw_out", 512),
             _tn_matmul(cb, dqc, "dw_q_cross", 512),
             dw_kv,
             _tn_matmul(dh2b, oc, "dw_o_cross", 512),
             _tn_matmul(dgu, f, "dw_gate_up", 512),
             _tn_matmul(act, dh3b, "dw_down", 256)]

    return loss, grad_x, grads, (dg_mix, db_gate, dws, dbs_acc, dg_sgu, dg_cross, dg_mem, dg_ffn, dg_final)


def kernel(x, mem, g_mix, w_in, b_gate, w_sgu_spatial, b_sgu_spatial, g_sgu, w_branch_attn, w_branch_sgu, w_out, g_cross, g_mem, w_q_cross, w_kv_cross, w_o_cross, g_ffn, w_gate_up, w_down, g_final, loss_target, m_g_mix, m_w_in, m_b_gate, m_w_sgu_spatial, m_b_sgu_spatial, m_g_sgu, m_w_branch_attn, m_w_branch_sgu, m_w_out, m_g_cross, m_g_mem, m_w_q_cross, m_w_kv_cross, m_w_o_cross, m_g_ffn, m_w_gate_up, m_w_down, m_g_final, v_g_mix, v_w_in, v_b_gate, v_w_sgu_spatial, v_b_sgu_spatial, v_g_sgu, v_w_branch_attn, v_w_branch_sgu, v_w_out, v_g_cross, v_g_mem, v_w_q_cross, v_w_kv_cross, v_w_o_cross, v_g_ffn, v_w_gate_up, v_w_down, v_g_final):
    S = x.shape[1]
    xs, tgt, mems = x.reshape(S, D), loss_target.reshape(S, D), mem.reshape(mem.shape[1], D)
    g_final2 = g_final.reshape(1, D)

    big = [("w_in", w_in[0], m_w_in[0], v_w_in[0], True),
           ("w_branch_attn", w_branch_attn[0], m_w_branch_attn[0], v_w_branch_attn[0], True),
           ("w_branch_sgu", w_branch_sgu[0], m_w_branch_sgu[0], v_w_branch_sgu[0], True),
           ("w_out", w_out[0], m_w_out[0], v_w_out[0], False),
           ("w_q_cross", w_q_cross[0], m_w_q_cross[0], v_w_q_cross[0], False),
           ("w_kv_cross", w_kv_cross[0], m_w_kv_cross[0], v_w_kv_cross[0], False),
           ("w_o_cross", w_o_cross[0], m_w_o_cross[0], v_w_o_cross[0], True),
           ("w_gate_up", w_gate_up[0], m_w_gate_up[0], v_w_gate_up[0], True),
           ("w_down", w_down[0], m_w_down[0], v_w_down[0], False)]
    shards = [(w.T if tr else w).astype(BF16) for _, w, _, _, tr in big]
    wt_in, wt_ba, wt_bs, wo, wq, wkv, wt_o, wt_gu, wd = _gather_weights(shards)

    (loss, grad_x, grads, (dg_mix, db_gate, dws, dbs_acc, dg_sgu, dg_cross, dg_mem, dg_ffn, dg_final)) = _local_step(
        xs, tgt, mems, (wt_in, wt_ba, wt_bs, wo, wq, wkv, wt_o, wt_gu, wd),
        (g_mix, b_gate, w_sgu_spatial[0], b_sgu_spatial[0], g_sgu, g_cross, g_mem, g_ffn, g_final2))

    own, got = _swap_halves(grads)
    sums, sums_b = [], []
    for i, (o, t) in enumerate(zip(own, got)):
        shp = o.shape
        s_, sb_ = _elementwise(lambda p, q: (p + q, p + q), [o.reshape(-1, shp[2]), t.reshape(-1, shp[2])],
                               [F32, BF16], f"chip_sum_{big[i][0]}")
        sums.append(s_.reshape(shp))
        sums_b.append(sb_.reshape(shp))
    mine, parts = _scatter_partials(sums, sums_b)
    reduced = []
    for i, (o, p) in enumerate(zip(mine, parts)):
        reduced.append(_elementwise(lambda p0, p1, p2, p3: (p0 + p1.astype(F32) + p2.astype(F32) + p3.astype(F32),),
                                    [o.astype(F32)] + [p[k] for k in range(3)], [F32], f"mesh_sum_{big[i][0]}")[0])
    full = _share_halves(reduced)

    big_out = {}
    for (name, w, m, v, tr), gsh in zip(big, full):
        gsh = gsh.T if tr else gsh
        delta, nm, nv = _elementwise(_adamw, [w, gsh, m, v], [F32, F32, F32], f"adam_{name}")
        big_out[name] = tuple(t[None] for t in (gsh, delta, nm, nv))

    small = [("g_mix", g_mix, m_g_mix, v_g_mix, dg_mix), ("b_gate", b_gate, m_b_gate, v_b_gate, db_gate),
             ("w_sgu_spatial", w_sgu_spatial, m_w_sgu_spatial, v_w_sgu_spatial, jnp.tril(dws)),
             ("b_sgu_spatial", b_sgu_spatial, m_b_sgu_spatial, v_b_sgu_spatial, jnp.sum(dbs_acc, axis=-1)),
             ("g_sgu", g_sgu, m_g_sgu, v_g_sgu, dg_sgu), ("g_cross", g_cross, m_g_cross, v_g_cross, dg_cross),
             ("g_mem", g_mem, m_g_mem, v_g_mem, dg_mem), ("g_ffn", g_ffn, m_g_ffn, v_g_ffn, dg_ffn),
             ("g_final", g_final, m_g_final, v_g_final, dg_final)]

    def pack(parts_, tail):
        return jnp.concatenate([p.reshape(-1) for p in parts_] + [tail]).reshape(-1, 128)

    zeros = jnp.zeros((1024,), F32)
    gp = pack([s[4] for s in small], jnp.pad(loss.reshape(-1)[:1], (0, 1023)))
    wp, mp, vp = (pack([s[k] for s in small], zeros) for k in (1, 2, 3))
    gsum, dsm, nms, nvs = _adam_small(wp, mp, vp, _gather_small(gp))
    small_out, off = {}, 0
    for name, w, _, _, _ in small:
        n = w.size
        small_out[name] = tuple(t.reshape(-1)[off:off + n].reshape(w.shape) for t in (gsum, dsm, nms, nvs))
        off += n
    total_loss = gsum.reshape(-1)[off]

    order = ["g_mix", "w_in", "b_gate", "w_sgu_spatial", "b_sgu_spatial", "g_sgu", "w_branch_attn", "w_branch_sgu",
             "w_out", "g_cross", "g_mem", "w_q_cross", "w_kv_cross", "w_o_cross", "g_ffn", "w_gate_up", "w_down",
             "g_final"]
    res = {**big_out, **small_out}
    outs = [total_loss, grad_x.reshape(x.shape)]
    for k in range(4):
        outs += [res[nm][k] for nm in order]
    return tuple(outs)
```

```python
import math

import numpy as np
import jax
import jax.numpy as jnp
from jax import lax
from jax.experimental import pallas as pl
from jax.experimental.pallas import tpu as pltpu

F32, BF16 = jnp.float32, jnp.bfloat16
MESH = pl.DeviceIdType.MESH
ANY = pl.BlockSpec(memory_space=pl.ANY)
RES = pl.BlockSpec(memory_space=pltpu.VMEM)

D = 1024
HEAD = 64
GROUP_W = 256
DIL_GROUPS = ((128, 1), (512, 4), (2048, 16))
BLK = 128
SGU_W = 512
MEM_HEADS, MEM_HD, MEM_W = 4, 128, 512
D_FF = 2816
FF_CHUNK = 256
EPS = 1e-6
NEG = -1e30
LR, B1, B2, AEPS, WD, STEP = 0.001, 0.9, 0.999, 1e-08, 0.01, 10
GELU_K, GELU_C = 0.7978845608028654, 0.044715


def _dot(a, b):
    return jnp.dot(a, b, preferred_element_type=F32)


def _dot_nt(a, b):
    return lax.dot_general(a, b, (((1,), (1,)), ((), ())), preferred_element_type=F32)


def _dot_tn(a, b):
    return lax.dot_general(a, b, (((0,), (0,)), ((), ())), preferred_element_type=F32)


def _row(tm, w):
    return pl.BlockSpec((tm, w), lambda i: (i, 0))


def _acc(shape):
    return pl.BlockSpec(shape, lambda i: (0,) * len(shape))


def _params(sem, mb):
    return pltpu.CompilerParams(dimension_semantics=sem, vmem_limit_bytes=mb << 20)


def _sds(shape, dt):
    return jax.ShapeDtypeStruct(shape, dt)


def _rms(h):
    return lax.rsqrt(jnp.mean(h * h, axis=-1, keepdims=True) + EPS)


def _rms_bwd(dy, h, r, g):
    t = dy * g
    dh = r * t - h * (r * r * r) * jnp.mean(t * h, axis=-1, keepdims=True)
    return dh, dy * h * r


def _gelu(x):
    t = jnp.tanh(GELU_K * (x + GELU_C * x * x * x))
    return 0.5 * x * (1.0 + t), t


def _gelu_grad(x, t):
    return 0.5 * (1.0 + t) + 0.5 * x * (1.0 - t * t) * GELU_K * (1.0 + 3.0 * GELU_C * x * x)


def _alibi_slopes():
    def pow2(n):
        start = 2.0 ** (-8.0 / n)
        return [start ** (i + 1) for i in range(n)]
    n = 12
    c = 2 ** int(math.floor(math.log2(n)))
    s = pow2(c) + pow2(2 * c)[0::2][: n - c]
    return np.array(sorted(s, reverse=True), dtype=np.float32).reshape(3, 4)


def _attn_bias(g):
    win, dil = DIL_GROUPS[g]
    steps = (np.arange(BLK)[:, None] + BLK) - np.arange(2 * BLK)[None, :]
    valid = (steps >= 0) & (steps <= win // dil)
    dist = (np.clip(steps, 0, None) * dil).astype(np.float32)
    b = -_alibi_slopes()[g][:, None, None] * dist[None]
    return np.where(valid[None], b, NEG).astype(np.float32)


def _head_masks():
    lane = lax.broadcasted_iota(jnp.int32, (1, GROUP_W), 1)
    return lane, [(lane >= HEAD * h) & (lane < HEAD * (h + 1)) for h in range(4)]


ATT_NB = 4


def _stack_heads(t, masks):
    z = jnp.zeros_like(t)
    return jnp.concatenate([jnp.where(m, t, z) for m in masks], axis=0)


def _unstack_heads(t, masks):
    out = jnp.zeros((BLK, GROUP_W), t.dtype)
    for h, m in enumerate(masks):
        out = jnp.where(m, t[h * BLK:(h + 1) * BLK], out)
    return out


def _stack_cols(ref, rows):
    return jnp.concatenate([ref[rows, HEAD * h:HEAD * h + 1] for h in range(4)], axis=0)


def _dil_spec(d, tm, w):
    return pl.BlockSpec((d, tm // d, w), lambda i: (0, i, 0))


def _to_dilated(val, s_ref, d, write):
    tm, w = val.shape
    for j in range(w // 128):
        s_ref[j, pl.ds(0, tm), :] = val[:, j * 128:(j + 1) * 128]
    for r in range(d):
        for j in range(w // 128):
            write(r, j, s_ref[j, pl.ds(r, tm // d, stride=d), :])


def _from_dilated(ref, s_ref, d, tm, w):
    if d == 1:
        return ref[0].astype(F32)
    for r in range(d):
        for j in range(w // 128):
            s_ref[j, pl.ds(r, tm // d, stride=d), :] = ref[r, :, j * 128:(j + 1) * 128].astype(F32)
    return jnp.concatenate([s_ref[j, pl.ds(0, tm), :] for j in range(w // 128)], axis=1)


def _fwd_in(x, g_mix, wt_in, gather=None, tm=512):
    S = x.shape[0]
    dils = [d for _, d in DIL_GROUPS]
    n = 0 if gather is None else gather.n
    last = S // tm - 1

    def body(*refs):
        x_ref, g_ref, w_ref = refs[:3]
        a_ref, q0_ref, q1_ref, q2_ref, uv_ref, gl_ref = refs[3 + n:9 + n]
        s_ref = refs[9 + 2 * n]
        comm = (refs[3:3 + n], refs[9 + n:9 + 2 * n], refs[10 + 2 * n:])
        if gather is not None:
            pl.when(pl.program_id(0) == 0)(lambda: gather.start(*comm))
        xv = x_ref[...]
        a = (xv * _rms(xv) * g_ref[...]).astype(BF16)
        a_ref[...] = a
        for g, (d, out) in enumerate(zip(dils, (q0_ref, q1_ref, q2_ref))):
            for part in range(3):
                rows = part * 768 + g * 256
                val = _dot_nt(a, w_ref[rows:rows + 256, :])
                if d == 1:
                    out[0, :, part * 256:(part + 1) * 256] = val.astype(BF16)
                else:
                    def write(r, j, piece, out=out, part=part):
                        out[r, :, part * 256 + j * 128:part * 256 + (j + 1) * 128] = piece.astype(BF16)
                    _to_dilated(val, s_ref, d, write)
        uv_ref[...] = _dot_nt(a, w_ref[2304:3328, :]).astype(BF16)
        gl_ref[...] = _dot_nt(a, w_ref[3328:5376, :]).astype(BF16)
        if gather is not None:
            pl.when(pl.program_id(0) == last)(lambda: gather.finish(*comm))

    outs = pl.pallas_call(
        body, grid=(S // tm,), name="fwd_in",
        in_specs=[_row(tm, D), RES, RES] + [ANY] * n,
        out_specs=[_row(tm, D)] + [_dil_spec(d, tm, 768) for d in dils] + [_row(tm, 1024), _row(tm, 2048)] + [ANY] * n,
        out_shape=[_sds((S, D), BF16)] + [_sds((d, S // d, 768), BF16) for d in dils]
        + [_sds((S, 1024), BF16), _sds((S, 2048), BF16)] + ([] if gather is None else gather.out_shape),
        scratch_shapes=[pltpu.VMEM((2, tm, 128), F32)] + ([] if gather is None else gather.scratch),
        compiler_params=_params(("arbitrary",), 60),
    )(x, g_mix, wt_in, *([] if gather is None else gather.halves))
    return outs[:6], ([] if gather is None else gather.full(outs[6:]))


def _attn_fwd(qkv, g):
    d, L, _ = qkv.shape
    nb = L // BLK
    bias = jnp.asarray(_attn_bias(g).reshape(4 * BLK, 2 * BLK))
    NB = min(ATT_NB, nb)
    W = NB * BLK

    def body(q_ref, kc_ref, kp_ref, vc_ref, vp_ref, b_ref, o_ref, l_ref):
        st = pl.program_id(1)
        k_all = jnp.concatenate([kp_ref[...], kc_ref[...]], axis=0)
        v_all = jnp.concatenate([vp_ref[...], vc_ref[...]], axis=0)
        lane, masks = _head_masks()
        for b in range(NB):
            rows = slice(b * BLK, (b + 1) * BLK)
            kk, vv = k_all[b * BLK:(b + 2) * BLK], v_all[b * BLK:(b + 2) * BLK]
            s = _dot_nt(_stack_heads(q_ref[rows, :], masks), kk) * 0.125 + b_ref[...]
            if b == 0:
                s = s + jnp.where((st == 0) & (lane < BLK), NEG, 0.0).astype(F32)
            mx = jnp.max(s, axis=-1, keepdims=True)
            e = jnp.exp(s - mx)
            den = jnp.sum(e, axis=-1, keepdims=True)
            o_ref[rows, :] = _unstack_heads(_dot(e.astype(BF16), vv) / den, masks)
            l_ref[rows, :] = _unstack_heads(mx + jnp.log(den), masks)

    def wide(col):
        return pl.BlockSpec((None, W, GROUP_W), lambda r, s: (r, s, col))

    def before(col):
        return pl.BlockSpec((None, BLK, GROUP_W), lambda r, s: (r, jnp.maximum(s * NB - 1, 0), col))

    return pl.pallas_call(
        body, grid=(d, nb // NB), name=f"attn_fwd_g{g}",
        in_specs=[wide(0), wide(1), before(1), wide(2), before(2),
                  pl.BlockSpec((4 * BLK, 2 * BLK), lambda r, s: (0, 0))],
        out_specs=[wide(0), wide(0)],
        out_shape=[_sds((d, L, GROUP_W), F32), _sds((d, L, GROUP_W), F32)],
        compiler_params=_params(("parallel", "parallel"), 32),
    )(qkv, qkv, qkv, qkv, qkv, bias)


def _group_weights(l0, l1, l2):
    m = jnp.maximum(jnp.maximum(l0, l1), l2)
    e0, e1, e2 = jnp.exp(l0 - m), jnp.exp(l1 - m), jnp.exp(l2 - m)
    inv = 1.0 / (e0 + e1 + e2)
    return e0 * inv, e1 * inv, e2 * inv


def _sgu_forward(uvf, gs, wt_ref, bst_ref, mixed_s, tm):
    z, t = _gelu(uvf)
    u, v = z[:, :SGU_W], z[:, SGU_W:]
    rv = _rms(v)
    vnb = (v * rv * gs).astype(BF16)
    for ci in range(tm // 128):
        for g in range(4):
            rs, cs = slice(ci * 128, (ci + 1) * 128), slice(g * 128, (g + 1) * 128)
            mixed_s[rs, cs] = _dot(wt_ref[g], vnb[rs, cs]) + bst_ref[:, g:g + 1]
    return u, v, rv, vnb, t


def _fwd_mid(x, os_, ls_, uv, gl, wt, bst, g_sgu, b_gate, wt_ba, wt_bs, w_out, gather=None, tm=256):
    S = x.shape[0]
    dils = [d for _, d in DIL_GROUPS]
    n = 0 if gather is None else gather.n
    last = S // tm - 1

    def body(*refs):
        (x_ref, o0, o1, o2, l0, l1, l2, uv_ref, gl_ref, wt_ref, bst_ref, gs_ref, bg_ref, wba_ref, wbs_ref,
         wo_ref) = refs[:16]
        ya_ref, ys_ref, ba_ref, bs_ref, mg_ref, h1_ref = refs[16 + n:22 + n]
        mixed_s, il_s = refs[22 + 2 * n:24 + 2 * n]
        comm = (refs[16:16 + n], refs[22 + n:22 + 2 * n], refs[24 + 2 * n:])
        if gather is not None:
            pl.when(pl.program_id(0) == 0)(lambda: gather.start(*comm))
        ls = [_from_dilated(r, il_s, d, tm, GROUP_W) for r, d in zip((l0, l1, l2), dils)]
        alphas = _group_weights(*ls)
        ya = jnp.zeros((tm, GROUP_W), F32)
        for a, r, d in zip(alphas, (o0, o1, o2), dils):
            ya = ya + a * _from_dilated(r, il_s, d, tm, GROUP_W)
        yab = ya.astype(BF16)
        ya_ref[...] = yab
        u, _, _, _, _ = _sgu_forward(uv_ref[...].astype(F32), gs_ref[...], wt_ref, bst_ref, mixed_s, tm)
        ysb = (u * mixed_s[...]).astype(BF16)
        ys_ref[...] = ysb
        gates = jax.nn.sigmoid(gl_ref[...].astype(F32) + bg_ref[...])
        ba = _dot_nt(yab, wba_ref[...])
        bs = _dot_nt(ysb, wbs_ref[...])
        ba_ref[...] = ba.astype(BF16)
        bs_ref[...] = bs.astype(BF16)
        mgb = (gates[:, :D] * ba + gates[:, D:] * bs).astype(BF16)
        mg_ref[...] = mgb
        h1_ref[...] = x_ref[...] + _dot(mgb, wo_ref[...])
        if gather is not None:
            pl.when(pl.program_id(0) == last)(lambda: gather.finish(*comm))

    gw = _row(tm, GROUP_W)
    dil = [_dil_spec(d, tm, GROUP_W) for d in dils]
    outs = pl.pallas_call(
        body, grid=(S // tm,), name="fwd_mid",
        in_specs=[_row(tm, D)] + dil + dil + [_row(tm, 1024), _row(tm, 2048)] + [RES] * 7 + [ANY] * n,
        out_specs=[gw, _row(tm, SGU_W), _row(tm, D), _row(tm, D), _row(tm, D), _row(tm, D)] + [ANY] * n,
        out_shape=[_sds((S, GROUP_W), BF16), _sds((S, SGU_W), BF16), _sds((S, D), BF16), _sds((S, D), BF16),
                   _sds((S, D), BF16), _sds((S, D), F32)] + ([] if gather is None else gather.out_shape),
        scratch_shapes=[pltpu.VMEM((tm, SGU_W), F32), pltpu.VMEM((2, tm, 128), F32)]
        + ([] if gather is None else gather.scratch),
        compiler_params=_params(("arbitrary",), 48),
    )(x, *os_, *ls_, uv, gl, wt, bst, g_sgu, b_gate, wt_ba, wt_bs, w_out, *([] if gather is None else gather.halves))
    return outs[:6], ([] if gather is None else gather.full(outs[6:]))


def _mem_fwd(mem, g_mem, w_kv):
    def body(m_ref, g_ref, w_ref, mb_ref, kv_ref):
        mv = m_ref[...]
        mb = (mv * _rms(mv) * g_ref[...]).astype(BF16)
        mb_ref[...] = mb
        kv_ref[...] = _dot(mb, w_ref[...]).astype(BF16)

    return pl.pallas_call(
        body, name="mem_fwd", in_specs=[RES, RES, RES], out_specs=[RES, RES],
        out_shape=[_sds(mem.shape, BF16), _sds((mem.shape[0], 2 * MEM_W), BF16)],
        compiler_params=pltpu.CompilerParams(vmem_limit_bytes=32 << 20),
    )(mem, g_mem, w_kv)


def _cross_probs(qh, kh):
    s = _dot_nt(qh, kh) * (MEM_HD ** -0.5)
    e = jnp.exp(s - jnp.max(s, axis=-1, keepdims=True))
    return e / jnp.sum(e, axis=-1, keepdims=True)


def _fwd_cross(h1, g_cross, w_q, kv, wt_o, tm=512):
    S = h1.shape[0]

    def body(h_ref, g_ref, wq_ref, kv_ref, wo_ref, c_ref, qc_ref, oc_ref, h2_ref):
        hv = h_ref[...]
        cb = (hv * _rms(hv) * g_ref[...]).astype(BF16)
        c_ref[...] = cb
        qcb = _dot(cb, wq_ref[...]).astype(BF16)
        qc_ref[...] = qcb
        for h in range(MEM_HEADS):
            cs = slice(h * MEM_HD, (h + 1) * MEM_HD)
            p = _cross_probs(qcb[:, cs], kv_ref[:, cs])
            oc_ref[:, cs] = _dot(p.astype(BF16), kv_ref[:, MEM_W + h * MEM_HD:MEM_W + (h + 1) * MEM_HD]).astype(BF16)
        h2_ref[...] = hv + _dot_nt(oc_ref[...], wo_ref[...])

    return pl.pallas_call(
        body, grid=(S // tm,), name="fwd_cross",
        in_specs=[_row(tm, D), RES, RES, RES, RES],
        out_specs=[_row(tm, D), _row(tm, MEM_W), _row(tm, MEM_W), _row(tm, D)],
        out_shape=[_sds((S, D), BF16), _sds((S, MEM_W), BF16), _sds((S, MEM_W), BF16), _sds((S, D), F32)],
        compiler_params=_params(("parallel",), 40),
    )(h1, g_cross, w_q, kv, wt_o)


def _ffn_fwd_bwd(h2, target, g_ffn, g_final, wt_gu, w_down, tm=256):
    S = h2.shape[0]
    nch = D_FF // FF_CHUNK

    def body(h_ref, t_ref, gf_ref, gz_ref, wgu_ref, wd_ref,
             f_ref, act_ref, dgu_ref, dh3b_ref, dh2_ref, dh2b_ref, dgf_ref, dgz_ref, loss_ref, gu_s):
        i = pl.program_id(0)
        hv = h_ref[...]
        r2 = _rms(hv)
        gf = gf_ref[...]
        fb = (hv * r2 * gf).astype(BF16)
        f_ref[...] = fb
        h3 = hv
        for c in range(nch):
            cs = slice(c * FF_CHUNK, (c + 1) * FF_CHUNK)
            us = slice(D_FF + c * FF_CHUNK, D_FF + (c + 1) * FF_CHUNK)
            gt = _dot_nt(fb, wgu_ref[cs, :])
            up = _dot_nt(fb, wgu_ref[us, :])
            gu_s[:, cs] = gt
            gu_s[:, us] = up
            actb = (gt * jax.nn.sigmoid(gt) * up).astype(BF16)
            act_ref[:, cs] = actb
            h3 = h3 + _dot(actb, wd_ref[cs, :])
        r3 = _rms(h3)
        gz = gz_ref[...]
        diff = h3 * r3 * gz - t_ref[...]
        dy = diff * (1.0 / D)
        dh3, dgz_rows = _rms_bwd(dy, h3, r3, gz)
        dh3b = dh3.astype(BF16)
        dh3b_ref[...] = dh3b
        df = jnp.zeros((tm, D), F32)
        for c in range(nch):
            cs = slice(c * FF_CHUNK, (c + 1) * FF_CHUNK)
            us = slice(D_FF + c * FF_CHUNK, D_FF + (c + 1) * FF_CHUNK)
            dact = _dot_nt(dh3b, wd_ref[cs, :])
            gt, up = gu_s[:, cs], gu_s[:, us]
            sg = jax.nn.sigmoid(gt)
            dgt = (dact * up * (sg * (1.0 + gt * (1.0 - sg)))).astype(BF16)
            dup = (dact * (gt * sg)).astype(BF16)
            dgu_ref[:, cs] = dgt
            dgu_ref[:, us] = dup
            df = df + _dot(dgt, wgu_ref[cs, :]) + _dot(dup, wgu_ref[us, :])
        dhn, dgf_rows = _rms_bwd(df, hv, r2, gf)
        dh2 = dh3 + dhn
        dh2_ref[...] = dh2
        dh2b_ref[...] = dh2.astype(BF16)

        @pl.when(i == 0)
        def _():
            dgf_ref[...] = jnp.zeros_like(dgf_ref)
            dgz_ref[...] = jnp.zeros_like(dgz_ref)
            loss_ref[...] = jnp.zeros_like(loss_ref)

        dgf_ref[...] += jnp.sum(dgf_rows, axis=0, keepdims=True)
        dgz_ref[...] += jnp.sum(dgz_rows, axis=0, keepdims=True)
        loss_ref[...] += jnp.sum(jnp.sum(diff * diff, axis=0, keepdims=True), axis=1, keepdims=True) * (0.5 / D)

    return pl.pallas_call(
        body, grid=(S // tm,), name="ffn_fwd_bwd",
        in_specs=[_row(tm, D), _row(tm, D), RES, RES, RES, RES],
        out_specs=[_row(tm, D), _row(tm, D_FF), _row(tm, 2 * D_FF), _row(tm, D), _row(tm, D), _row(tm, D),
                   _acc((1, D)), _acc((1, D)), _acc((1, 128))],
        out_shape=[_sds((S, D), BF16), _sds((S, D_FF), BF16), _sds((S, 2 * D_FF), BF16), _sds((S, D), BF16),
                   _sds((S, D), F32), _sds((S, D), BF16), _sds((1, D), F32), _sds((1, D), F32), _sds((1, 128), F32)],
        scratch_shapes=[pltpu.VMEM((tm, 2 * D_FF), F32)],
        compiler_params=_params(("arbitrary",), 56),
    )(h2, target, g_ffn, g_final, wt_gu, w_down)


def _bwd_cross(dh2, h1, qc, g_cross, w_q, kv, wt_o, tm=256):
    S = h1.shape[0]

    def body(d_ref, h_ref, qc_ref, g_ref, wq_ref, kv_ref, wo_ref, dqc_ref, dh1_ref, dh1b_ref, dkv_ref, dg_ref):
        i = pl.program_id(0)

        @pl.when(i == 0)
        def _():
            dkv_ref[...] = jnp.zeros_like(dkv_ref)
            dg_ref[...] = jnp.zeros_like(dg_ref)

        dh2 = d_ref[...]
        doc = _dot(dh2.astype(BF16), wo_ref[...])
        qcb = qc_ref[...]
        for h in range(MEM_HEADS):
            cs = slice(h * MEM_HD, (h + 1) * MEM_HD)
            vs = slice(MEM_W + h * MEM_HD, MEM_W + (h + 1) * MEM_HD)
            qh, kh, vh = qcb[:, cs], kv_ref[:, cs], kv_ref[:, vs]
            p = _cross_probs(qh, kh)
            dohb = doc[:, cs].astype(BF16)
            dp = _dot_nt(dohb, vh)
            dsb = (p * (dp - jnp.sum(dp * p, axis=-1, keepdims=True)) * (MEM_HD ** -0.5)).astype(BF16)
            dqc_ref[:, cs] = _dot(dsb, kh).astype(BF16)
            dkv_ref[:, cs] += _dot_tn(dsb, qh)
            dkv_ref[:, vs] += _dot_tn(p.astype(BF16), dohb)
        dc = _dot_nt(dqc_ref[...], wq_ref[...])
        hv = h_ref[...]
        dhn, dg_rows = _rms_bwd(dc, hv, _rms(hv), g_ref[...])
        dh1 = dh2 + dhn
        dh1_ref[...] = dh1
        dh1b_ref[...] = dh1.astype(BF16)
        dg_ref[...] += jnp.sum(dg_rows, axis=0, keepdims=True)

    return pl.pallas_call(
        body, grid=(S // tm,), name="bwd_cross",
        in_specs=[_row(tm, D), _row(tm, D), _row(tm, MEM_W), RES, RES, RES, RES],
        out_specs=[_row(tm, MEM_W), _row(tm, D), _row(tm, D), _acc((256, 2 * MEM_W)), _acc((1, D))],
        out_shape=[_sds((S, MEM_W), BF16), _sds((S, D), F32), _sds((S, D), BF16), _sds((256, 2 * MEM_W), F32),
                   _sds((1, D), F32)],
        compiler_params=_params(("arbitrary",), 40),
    )(dh2, h1, qc, g_cross, w_q, kv, wt_o)


def _mem_bwd(dkv, mem, mb, g_mem, w_kv):
    def body(dkv_ref, m_ref, mb_ref, g_ref, w_ref, dw_ref, dg_ref):
        dkvb = dkv_ref[...].astype(BF16)
        dw_ref[...] = _dot_tn(mb_ref[...], dkvb)
        dm = _dot_nt(dkvb, w_ref[...])
        mv = m_ref[...]
        dg_ref[...] = jnp.sum(dm * mv * _rms(mv), axis=0, keepdims=True)

    return pl.pallas_call(
        body, name="mem_bwd", in_specs=[RES] * 5, out_specs=[RES, RES],
        out_shape=[_sds((D, 2 * MEM_W), F32), _sds((1, D), F32)],
        compiler_params=pltpu.CompilerParams(vmem_limit_bytes=32 << 20),
    )(dkv, mem, mb, g_mem, w_kv)


def _bwd_mid(dh1, gl, ba, bs, uv, ls_, ya, wt, bst, g_sgu, b_gate, wt_ba, wt_bs, w_out, tm=256):
    S = dh1.shape[0]
    dils = [d for _, d in DIL_GROUPS]

    def body(d_ref, gl_ref, ba_ref, bs_ref, uv_ref, l0, l1, l2, ya_ref,
             wt_ref, bst_ref, gs_ref, bg_ref, wba_ref, wbs_ref, wo_ref,
             dba_ref, dbs_ref, dgl_ref, duv_ref, do0, do1, do2, c0, c1, c2,
             dbg_ref, dgs_ref, dws_ref, dbsa_ref, mixed_s, dvn_s, il_s):
        i = pl.program_id(0)

        @pl.when(i == 0)
        def _():
            for r in (dbg_ref, dgs_ref, dws_ref, dbsa_ref):
                r[...] = jnp.zeros_like(r)

        dm = _dot_nt(d_ref[...].astype(BF16), wo_ref[...])
        gates = jax.nn.sigmoid(gl_ref[...].astype(F32) + bg_ref[...])
        g0, g1 = gates[:, :D], gates[:, D:]
        dbab = (dm * g0).astype(BF16)
        dbsb = (dm * g1).astype(BF16)
        dba_ref[...] = dbab
        dbs_ref[...] = dbsb
        dg0 = dm * ba_ref[...].astype(F32) * g0 * (1.0 - g0)
        dg1 = dm * bs_ref[...].astype(F32) * g1 * (1.0 - g1)
        dgl_ref[:, :D] = dg0.astype(BF16)
        dgl_ref[:, D:] = dg1.astype(BF16)
        dbg_ref[:, :D] += jnp.sum(dg0, axis=0, keepdims=True)
        dbg_ref[:, D:] += jnp.sum(dg1, axis=0, keepdims=True)
        dya = _dot(dbab, wba_ref[...])
        dys = _dot(dbsb, wbs_ref[...])

        uvf = uv_ref[...].astype(F32)
        gs = gs_ref[...]
        u, v, rv, vnb, t = _sgu_forward(uvf, gs, wt_ref, bst_ref, mixed_s, tm)
        du = dys * mixed_s[...]
        dmixed = dys * u
        for ci in range(tm // 128):
            for g in range(4):
                rs, cs = slice(ci * 128, (ci + 1) * 128), slice(g * 128, (g + 1) * 128)
                dmx = dmixed[rs, cs]
                dmxb = dmx.astype(BF16)
                dvn_s[rs, cs] = _dot_tn(wt_ref[g], dmxb)
                dws_ref[g] += _dot_nt(dmxb, vnb[rs, cs])
                dbsa_ref[g] += dmx
        dv, dgs_rows = _rms_bwd(dvn_s[...], v, rv, gs)
        dgs_ref[...] += jnp.sum(dgs_rows, axis=0, keepdims=True)
        gg = _gelu_grad(uvf, t)
        duv_ref[:, :SGU_W] = (du * gg[:, :SGU_W]).astype(BF16)
        duv_ref[:, SGU_W:] = (dv * gg[:, SGU_W:]).astype(BF16)

        alphas = _group_weights(*[_from_dilated(r, il_s, d, tm, GROUP_W) for r, d in zip((l0, l1, l2), dils)])
        prod = dya * ya_ref[...].astype(F32)
        _, masks = _head_masks()
        hs = jnp.zeros_like(prod)
        for h in range(4):
            sh = jnp.sum(jnp.where(masks[h], prod, 0.0), axis=-1, keepdims=True)
            hs = jnp.where(masks[h], sh, hs)
        for a, d, do_ref, c_ref in zip(alphas, dils, (do0, do1, do2), (c0, c1, c2)):
            for val, out in ((a * dya, do_ref), (a * hs, c_ref)):
                if d == 1:
                    out[0] = val.astype(out.dtype)
                else:
                    def write(r, j, piece, out=out):
                        out[r, :, j * 128:(j + 1) * 128] = piece.astype(out.dtype)
                    _to_dilated(val, il_s, d, write)

    gw = _row(tm, GROUP_W)
    dil = [_dil_spec(d, tm, GROUP_W) for d in dils]
    return pl.pallas_call(
        body, grid=(S // tm,), name="bwd_mid",
        in_specs=[_row(tm, D), _row(tm, 2048), _row(tm, D), _row(tm, D), _row(tm, 1024)] + dil + [gw] + [RES] * 7,
        out_specs=[_row(tm, D), _row(tm, D), _row(tm, 2048), _row(tm, 1024)] + dil + dil
        + [_acc((1, 2048)), _acc((1, SGU_W)), _acc((4, 128, 128)), _acc((4, 128, 128))],
        out_shape=[_sds((S, D), BF16), _sds((S, D), BF16), _sds((S, 2048), BF16), _sds((S, 1024), BF16)]
        + [_sds((d, S // d, GROUP_W), BF16) for d in dils] + [_sds((d, S // d, GROUP_W), F32) for d in dils]
        + [_sds((1, 2048), F32), _sds((1, SGU_W), F32), _sds((4, 128, 128), F32), _sds((4, 128, 128), F32)],
        scratch_shapes=[pltpu.VMEM((tm, SGU_W), F32), pltpu.VMEM((tm, SGU_W), F32), pltpu.VMEM((2, tm, 128), F32)],
        compiler_params=_params(("arbitrary",), 48),
    )(dh1, gl, ba, bs, uv, *ls_, ya, wt, bst, g_sgu, b_gate, wt_ba, wt_bs, w_out)


def _attn_bwd(qkv, do, lse, corr, g):
    d, L, _ = qkv.shape
    nb = L // BLK
    NB = min(ATT_NB, nb)
    W = NB * BLK
    nsteps = nb // NB
    bias = jnp.asarray(_attn_bias(g).reshape(4 * BLK, 2 * BLK))

    def body(q_ref, kc_ref, kp_ref, vc_ref, vp_ref, do_ref, l_ref, c_ref, qn_ref, don_ref, ln_ref, cn_ref, b_ref,
             out_ref, dk_s, dv_s):
        st = pl.program_id(1)
        k_all = jnp.concatenate([kp_ref[...], kc_ref[...]], axis=0)
        v_all = jnp.concatenate([vp_ref[...], vc_ref[...]], axis=0)
        lane, masks = _head_masks()
        dk_s[...] = jnp.zeros_like(dk_s)
        dv_s[...] = jnp.zeros_like(dv_s)

        def block_terms(qs, dos, kk, vv, bias_v, lse_c, corr_c):
            s = _dot_nt(qs, kk) * 0.125 + bias_v
            p = jnp.exp(s - lse_c)
            dsb = (p * (_dot_nt(dos, vv) - corr_c) * 0.125).astype(BF16)
            return dsb, p.astype(BF16)

        for b in range(NB):
            rows = slice(b * BLK, (b + 1) * BLK)
            keys = slice(b * BLK, (b + 2) * BLK)
            kk, vv = k_all[keys], v_all[keys]
            qs, dos = _stack_heads(q_ref[rows, :], masks), _stack_heads(do_ref[rows, :], masks)
            bias_v = b_ref[...]
            if b == 0:
                bias_v = bias_v + jnp.where((st == 0) & (lane < BLK), NEG, 0.0).astype(F32)
            dsb, pb = block_terms(qs, dos, kk, vv, bias_v, _stack_cols(l_ref, rows), _stack_cols(c_ref, rows))
            out_ref[rows, 0:GROUP_W] = _unstack_heads(_dot(dsb, kk), masks).astype(BF16)
            dk_s[keys, :] += _dot_tn(dsb, qs)
            dv_s[keys, :] += _dot_tn(pb, dos)

        @pl.when(st < nsteps - 1)
        def _():
            last = slice(NB * BLK, (NB + 1) * BLK)
            qs, dos = _stack_heads(qn_ref[...], masks), _stack_heads(don_ref[...], masks)
            every = slice(None)
            dsb, pb = block_terms(qs, dos, k_all[last], v_all[last], b_ref[:, :BLK],
                                  _stack_cols(ln_ref, every), _stack_cols(cn_ref, every))
            dk_s[last, :] += _dot_tn(dsb, qs)
            dv_s[last, :] += _dot_tn(pb, dos)

        out_ref[:, GROUP_W:2 * GROUP_W] = dk_s[BLK:, :].astype(BF16)
        out_ref[:, 2 * GROUP_W:] = dv_s[BLK:, :].astype(BF16)

    def wide(col, w=GROUP_W):
        return pl.BlockSpec((None, W, w), lambda r, s: (r, s, col))

    def before(col):
        return pl.BlockSpec((None, BLK, GROUP_W), lambda r, s: (r, jnp.maximum(s * NB - 1, 0), col))

    def after(col):
        return pl.BlockSpec((None, BLK, GROUP_W), lambda r, s: (r, jnp.minimum((s + 1) * NB, nb - 1), col))

    return pl.pallas_call(
        body, grid=(d, nsteps), name=f"attn_bwd_g{g}",
        in_specs=[wide(0), wide(1), before(1), wide(2), before(2), wide(0), wide(0), wide(0),
                  after(0), after(0), after(0), after(0), pl.BlockSpec((4 * BLK, 2 * BLK), lambda r, s: (0, 0))],
        out_specs=wide(0, 768),
        out_shape=_sds((d, L, 768), BF16),
        scratch_shapes=[pltpu.VMEM(((NB + 1) * BLK, GROUP_W), F32), pltpu.VMEM(((NB + 1) * BLK, GROUP_W), F32)],
        compiler_params=_params(("parallel", "parallel"), 32),
    )(qkv, qkv, qkv, qkv, qkv, do, lse, corr, qkv, do, lse, corr, bias)


def _bwd_in(dqkvs, duv, dgl, dh1, x, g_mix, wt_in, tm=512):
    S = x.shape[0]
    dils = [d for _, d in DIL_GROUPS]

    def body(q0_ref, q1_ref, q2_ref, duv_ref, dgl_ref, d_ref, x_ref, g_ref, w_ref, dx_ref, dp_ref, dg_ref, il_s):
        i = pl.program_id(0)

        @pl.when(i == 0)
        def _():
            dg_ref[...] = jnp.zeros_like(dg_ref)

        for g, (d, ref) in enumerate(zip(dils, (q0_ref, q1_ref, q2_ref))):
            nat = _from_dilated(ref, il_s, d, tm, 768).astype(BF16)
            for part in range(3):
                col = part * 768 + g * 256
                dp_ref[:, col:col + 256] = nat[:, part * 256:(part + 1) * 256]
        dp_ref[:, 2304:3328] = duv_ref[...]
        dp_ref[:, 3328:5376] = dgl_ref[...]
        da = _dot(dp_ref[...], w_ref[...])
        xv = x_ref[...]
        dxn, dg_rows = _rms_bwd(da, xv, _rms(xv), g_ref[...])
        dx_ref[...] = d_ref[...] + dxn
        dg_ref[...] += jnp.sum(dg_rows, axis=0, keepdims=True)

    return pl.pallas_call(
        body, grid=(S // tm,), name="bwd_in",
        in_specs=[_dil_spec(d, tm, 768) for d in dils] + [_row(tm, 1024), _row(tm, 2048), _row(tm, D), _row(tm, D),
                                                          RES, RES],
        out_specs=[_row(tm, D), _row(tm, 5376), _acc((1, D))],
        out_shape=[_sds((S, D), F32), _sds((S, 5376), BF16), _sds((1, D), F32)],
        scratch_shapes=[pltpu.VMEM((6, tm, 128), F32)],
        compiler_params=_params(("arbitrary",), 60),
    )(*dqkvs, duv, dgl, dh1, x, g_mix, wt_in)


def _tn_matmul(a, b, name, tk, ts=2048, into=None):
    S, K = a.shape
    N = b.shape[1]
    off = 0 if into is None else into[1] // tk

    def body(*refs):
        a_ref, b_ref, o_ref = refs[0], refs[1], refs[-1]

        @pl.when(pl.program_id(1) == 0)
        def _():
            o_ref[...] = jnp.zeros_like(o_ref)

        o_ref[...] += _dot_tn(a_ref[...], b_ref[...])

    ins = [a, b]
    in_specs = [pl.BlockSpec((ts, tk), lambda k, s: (s, k)), pl.BlockSpec((ts, N), lambda k, s: (s, 0))]
    out_shape = _sds((K, N), F32)
    aliases = {}
    if into is not None:
        ins.append(into[0])
        in_specs.append(ANY)
        out_shape = _sds(into[0].shape, F32)
        aliases = {2: 0}
    return pl.pallas_call(
        body, grid=(K // tk, S // ts), name=name, in_specs=in_specs,
        out_specs=pl.BlockSpec((tk, N), lambda k, s: (k + off, 0)), out_shape=out_shape,
        input_output_aliases=aliases,
        compiler_params=_params(("parallel", "arbitrary"), 48),
    )(*ins)


def _chip_peers(x, y):
    return [(1 - x, y), (x, 1 - y), (1 - x, 1 - y)]


STAGE_BYTES = 2 << 20


def _chunk_plan(shapes, itemsize):
    plan = []
    for i, (rows, w) in enumerate(shapes):
        ch = max(16, min(rows, (STAGE_BYTES // (w * itemsize)) // 16 * 16))
        while rows % ch:
            ch -= 16
        plan += [(i, r0, ch) for r0 in range(0, rows, ch)]
    return plan


def _remote(src, dst, ssem, rsem, dev):
    return pltpu.make_async_remote_copy(src_ref=src, dst_ref=dst, send_sem=ssem, recv_sem=rsem, device_id=dev,
                                        device_id_type=MESH)


class _Gather:
    def __init__(self, shards):
        self.n = len(shards)
        self.shards = shards
        self.halves = [s.reshape(2, s.shape[0] // 2, s.shape[1]) for s in shards]
        self.plan = _chunk_plan([h.shape[1:] for h in self.halves], 2)
        self.out_shape = [_sds((4,) + h.shape, BF16) for h in self.halves]
        n = self.n
        self.scratch = [pltpu.SemaphoreType.DMA((6 * n,)), pltpu.SemaphoreType.DMA((6 * n,)),
                        pltpu.SemaphoreType.DMA((2,)), pltpu.SemaphoreType.DMA((2,)),
                        pltpu.VMEM((2, max(p[2] for p in self.plan), max(h.shape[2] for h in self.halves)), BF16)]

    def full(self, outs):
        return [o.reshape(4 * s.shape[0], s.shape[1]) for o, s in zip(outs, self.shards)]

    def _sends(self, ins, outs, ssem, rsem):
        x, y, c = lax.axis_index("x"), lax.axis_index("y"), lax.axis_index("c")
        me = 2 * x + y
        return [_remote(ins[i].at[c], outs[i].at[me, c], ssem.at[6 * i + k], rsem.at[6 * i + k], (px, py, c))
                for i in range(self.n) for k, (px, py) in enumerate(_chip_peers(x, y))]

    def start(self, ins, outs, scratch):
        ssem, rsem, lsem, osem, buf = scratch
        me = 2 * lax.axis_index("x") + lax.axis_index("y")
        for cp in self._sends(ins, outs, ssem, rsem):
            cp.start()
        pending = {}
        for i, r0, ch in self.plan:
            for h in range(2):
                if h in pending:
                    pending[h].wait()
                stage = buf.at[h, pl.ds(0, ch), pl.ds(0, self.halves[i].shape[2])]
                ld = pltpu.make_async_copy(ins[i].at[h, pl.ds(r0, ch)], stage, lsem.at[h])
                ld.start()
                ld.wait()
                st = pltpu.make_async_copy(stage, outs[i].at[me, h, pl.ds(r0, ch)], osem.at[h])
                st.start()
                pending[h] = st
        for st in pending.values():
            st.wait()

    def finish(self, ins, outs, scratch):
        ssem, rsem = scratch[:2]
        x, y, c = lax.axis_index("x"), lax.axis_index("y"), lax.axis_index("c")
        chips = _chip_peers(x, y)
        sib = (x, y, 1 - c)
        forwards = []
        for i in range(self.n):
            for k, (px, py) in enumerate(chips):
                landed = outs[i].at[2 * px + py, c]
                _remote(landed, landed, ssem.at[6 * i + k], rsem.at[6 * i + k], (px, py, c)).wait_recv()
                cp = _remote(landed, landed, ssem.at[6 * i + 3 + k], rsem.at[6 * i + 3 + k], sib)
                cp.start()
                forwards.append(cp)
        for i in range(self.n):
            for k, (px, py) in enumerate(chips):
                passed = outs[i].at[2 * px + py, 1 - c]
                _remote(passed, passed, ssem.at[6 * i + 3 + k], rsem.at[6 * i + 3 + k], sib).wait_recv()
        for cp in self._sends(ins, outs, ssem, rsem) + forwards:
            cp.wait_send()


def _gather_weights(shards):
    gt = _Gather(shards)
    n = gt.n

    def body(*refs):
        ins, outs, scratch = refs[:n], refs[n:2 * n], refs[2 * n:]
        gt.start(ins, outs, scratch)
        gt.finish(ins, outs, scratch)

    outs = pl.pallas_call(
        body, name="gather_weights", in_specs=[ANY] * n, out_specs=[ANY] * n, out_shape=gt.out_shape,
        scratch_shapes=gt.scratch, compiler_params=pltpu.CompilerParams(vmem_limit_bytes=32 << 20),
    )(*gt.halves)
    return gt.full(outs)


def _swap_halves(grads):
    n = len(grads)
    g4 = [g.reshape(4, 2, g.shape[0] // 8, g.shape[1]) for g in grads]

    def body(*refs):
        ins, got = refs[:n], refs[n:2 * n]
        ssem, rsem = refs[2 * n:]
        x, y, c = lax.axis_index("x"), lax.axis_index("y"), lax.axis_index("c")
        sib = (x, y, 1 - c)
        cps = []
        for i in range(n):
            rc = _remote(ins[i].at[:, 1 - c], got[i], ssem.at[i], rsem.at[i], sib)
            rc.start()
            cps.append(rc)
        for cp in cps:
            cp.wait()

    half = [_sds((4, g.shape[2], g.shape[3]), F32) for g in g4]
    got = pl.pallas_call(
        body, name="swap_halves", in_specs=[ANY] * n, out_specs=[ANY] * n, out_shape=half,
        scratch_shapes=[pltpu.SemaphoreType.DMA((n,)), pltpu.SemaphoreType.DMA((n,))],
    )(*g4)
    c = lax.axis_index("c")
    own = [lax.dynamic_index_in_dim(g, c, axis=1, keepdims=False) for g in g4]
    return own, got


def _scatter_partials(sums, sums_b):
    n = len(sums)

    def body(*refs):
        hbs, parts = refs[:n], refs[n:2 * n]
        ssem, rsem = refs[2 * n:]
        x, y, c = lax.axis_index("x"), lax.axis_index("y"), lax.axis_index("c")
        cps = []
        for i in range(n):
            for k, (px, py) in enumerate(_chip_peers(x, y)):
                rc = _remote(hbs[i].at[2 * px + py], parts[i].at[k], ssem.at[3 * i + k], rsem.at[3 * i + k], (px, py, c))
                rc.start()
                cps.append(rc)
        for cp in cps:
            cp.wait()

    parts = pl.pallas_call(
        body, name="scatter_partials", in_specs=[ANY] * n, out_specs=[ANY] * n,
        out_shape=[_sds((3,) + s.shape[1:], BF16) for s in sums],
        scratch_shapes=[pltpu.SemaphoreType.DMA((3 * n,)), pltpu.SemaphoreType.DMA((3 * n,))],
    )(*sums_b)
    me = 2 * lax.axis_index("x") + lax.axis_index("y")
    mine = [lax.dynamic_index_in_dim(s, me, axis=0, keepdims=False) for s in sums]
    return mine, parts


def _share_halves(reduced):
    n = len(reduced)
    plan = _chunk_plan([r.shape for r in reduced], 4)
    max_rows = max(p[2] for p in plan)
    max_w = max(r.shape[1] for r in reduced)

    def body(*refs):
        ins, outs = refs[:n], refs[n:2 * n]
        ssem, rsem, lsem, osem, buf = refs[2 * n:]
        x, y, c = lax.axis_index("x"), lax.axis_index("y"), lax.axis_index("c")
        sib = (x, y, 1 - c)
        pending = {}
        for k, (i, r0, ch) in enumerate(plan):
            slot = k % 2
            if slot in pending:
                rc, lc = pending[slot]
                rc.wait_send()
                lc.wait()
            stage = buf.at[slot, pl.ds(0, ch), pl.ds(0, reduced[i].shape[1])]
            ld = pltpu.make_async_copy(ins[i].at[pl.ds(r0, ch)], stage, lsem.at[slot])
            ld.start()
            ld.wait()
            place = outs[i].at[c, pl.ds(r0, ch)]
            rc = _remote(stage, place, ssem.at[slot], rsem.at[i], sib)
            lc = pltpu.make_async_copy(stage, place, osem.at[slot])
            rc.start()
            lc.start()
            pending[slot] = (rc, lc)
        for rc, lc in pending.values():
            rc.wait_send()
            lc.wait()
        for i in range(n):
            theirs = outs[i].at[1 - c]
            _remote(theirs, theirs, ssem.at[0], rsem.at[i], sib).wait_recv()

    outs = pl.pallas_call(
        body, name="share_halves", in_specs=[ANY] * n, out_specs=[ANY] * n,
        out_shape=[_sds((2,) + r.shape, F32) for r in reduced],
        scratch_shapes=[pltpu.SemaphoreType.DMA((2,)), pltpu.SemaphoreType.DMA((n,)), pltpu.SemaphoreType.DMA((2,)),
                        pltpu.SemaphoreType.DMA((2,)), pltpu.VMEM((2, max_rows, max_w), F32)],
        compiler_params=pltpu.CompilerParams(vmem_limit_bytes=32 << 20),
    )(*reduced)
    return [o.reshape(2 * r.shape[0], r.shape[1]) for o, r in zip(outs, reduced)]


def _gather_small(pack):
    P = pack.shape[0]

    def body(in_ref, out_ref, ssem, rsem, lsem):
        x, y, c = lax.axis_index("x"), lax.axis_index("y"), lax.axis_index("c")
        me = 4 * x + 2 * y + c
        lc = pltpu.make_async_copy(in_ref, out_ref.at[me], lsem)
        lc.start()
        cps = []
        for rel in range(1, 8):
            px = 1 - x if rel & 4 else x
            py = 1 - y if rel & 2 else y
            pc = 1 - c if rel & 1 else c
            rc = _remote(in_ref, out_ref.at[me], ssem.at[rel - 1], rsem.at[rel - 1], (px, py, pc))
            rc.start()
            cps.append((rc, 4 * px + 2 * py + pc))
        for rel, (rc, peer) in enumerate(cps):
            rc.wait_send()
            _remote(in_ref, out_ref.at[peer], ssem.at[rel], rsem.at[rel], (x, y, c)).wait_recv()
        lc.wait()

    return pl.pallas_call(
        body, name="gather_small", in_specs=[ANY], out_specs=ANY, out_shape=_sds((8, P, 128), F32),
        scratch_shapes=[pltpu.SemaphoreType.DMA((7,)), pltpu.SemaphoreType.DMA((7,)), pltpu.SemaphoreType.DMA],
    )(pack)


def _tile(rows, cap=256):
    t = min(rows, cap) // 16 * 16
    while rows % t:
        t -= 16
    return t


def _elementwise(fn, ins, out_dtypes, name):
    R, W = ins[0].shape
    tr = _tile(R, max(8, min(512, (1 << 18) // W // 8 * 8)))

    def body(*refs):
        outs = fn(*[r[...] for r in refs[:len(ins)]])
        for o_ref, o in zip(refs[len(ins):], outs):
            o_ref[...] = o.astype(o_ref.dtype)

    return pl.pallas_call(
        body, grid=(R // tr,), name=name, in_specs=[_row(tr, W)] * len(ins), out_specs=[_row(tr, W)] * len(out_dtypes),
        out_shape=[_sds((R, W), dt) for dt in out_dtypes],
        compiler_params=_params(("parallel",), 48),
    )(*ins)


def _adamw(w, g, m, v):
    m = B1 * m + (1.0 - B1) * g
    v = B2 * v + (1.0 - B2) * (g * g)
    m_hat = m / (1.0 - B1 ** STEP)
    v_hat = v / (1.0 - B2 ** STEP)
    return -LR * (m_hat / (jnp.sqrt(v_hat) + AEPS) + WD * w), m, v


def _adam_small(w, m, v, packs):
    def body(w_ref, m_ref, v_ref, p_ref, g_ref, d_ref, nm_ref, nv_ref):
        g = p_ref[0]
        for k in range(1, 8):
            g = g + p_ref[k]
        g_ref[...] = g
        d_ref[...], nm_ref[...], nv_ref[...] = _adamw(w_ref[...], g, m_ref[...], v_ref[...])

    return pl.pallas_call(
        body, name="adam_small", in_specs=[RES] * 4, out_specs=[RES] * 4, out_shape=[_sds(w.shape, F32)] * 4,
        compiler_params=pltpu.CompilerParams(vmem_limit_bytes=32 << 20),
    )(w, m, v, packs)


def _local_step(xs, tgt, mems, weights, small, gather_mid=None, gather_ffn=None):
    wt_in, wt_ba, wt_bs, wo, wq, wkv, wt_o, wt_gu, wd = weights
    g_mix, b_gate, w_sgu, b_sgu, g_sgu, g_cross, g_mem, g_ffn, g_final = small
    wt = jnp.tril(w_sgu).astype(BF16)
    bst = b_sgu.T

    (a, qkv0, qkv1, qkv2, uv, gl), got = _fwd_in(xs, g_mix, wt_in, gather_mid)
    if gather_mid is not None:
        wt_ba, wt_bs, wo, wq, wkv, wt_o = got
    qkvs = (qkv0, qkv1, qkv2)
    os_, ls_ = zip(*[_attn_fwd(qkvs[g], g) for g in range(3)])
    (ya, ys, ba, bs, mg, h1), got = _fwd_mid(xs, os_, ls_, uv, gl, wt, bst, g_sgu, b_gate, wt_ba, wt_bs, wo, gather_ffn)
    if gather_ffn is not None:
        wt_gu, wd = got
    mb, kv = _mem_fwd(mems, g_mem, wkv)
    cb, qc, oc, h2 = _fwd_cross(h1, g_cross, wq, kv, wt_o)
    f, act, dgu, dh3b, dh2, dh2b, dg_ffn, dg_final, loss = _ffn_fwd_bwd(h2, tgt, g_ffn, g_final, wt_gu, wd)

    dqc, dh1, dh1b, dkv, dg_cross = _bwd_cross(dh2, h1, qc, g_cross, wq, kv, wt_o)
    dw_kv, dg_mem = _mem_bwd(dkv, mems, mb, g_mem, wkv)
    (dba, dbs, dgl, duv, do0, do1, do2, c0, c1, c2, db_gate, dg_sgu, dws, dbs_acc) = _bwd_mid(
        dh1, gl, ba, bs, uv, ls_, ya, wt, bst, g_sgu, b_gate, wt_ba, wt_bs, wo)
    dqkvs = [_attn_bwd(qkvs[g], do, ls_[g], corr, g) for g, (do, corr) in enumerate(((do0, c0), (do1, c1), (do2, c2)))]
    grad_x, dproj, dg_mix = _bwd_in(dqkvs, duv, dgl, dh1, xs, g_mix, wt_in)

    grads = [_tn_matmul(dproj, a, "dw_in", 768),
             _tn_matmul(dba, ya, "dw_branch_attn", 512),
             _tn_matmul(dbs, ys, "dw_branch_sgu", 512),
             _tn_matmul(mg, dh1b, "dw_out", 512),
             _tn_matmul(cb, dqc, "dw_q_cross", 512),
             dw_kv,
             _tn_matmul(dh2b, oc, "dw_o_cross", 512),
             _tn_matmul(dgu, f, "dw_gate_up", 512),
             _tn_matmul(act, dh3b, "dw_down", 256)]

    return loss, grad_x, grads, (dg_mix, db_gate, dws, dbs_acc, dg_sgu, dg_cross, dg_mem, dg_ffn, dg_final)


def kernel(x, mem, g_mix, w_in, b_gate, w_sgu_spatial, b_sgu_spatial, g_sgu, w_branch_attn, w_branch_sgu, w_out, g_cross, g_mem, w_q_cross, w_kv_cross, w_o_cross, g_ffn, w_gate_up, w_down, g_final, loss_target, m_g_mix, m_w_in, m_b_gate, m_w_sgu_spatial, m_b_sgu_spatial, m_g_sgu, m_w_branch_attn, m_w_branch_sgu, m_w_out, m_g_cross, m_g_mem, m_w_q_cross, m_w_kv_cross, m_w_o_cross, m_g_ffn, m_w_gate_up, m_w_down, m_g_final, v_g_mix, v_w_in, v_b_gate, v_w_sgu_spatial, v_b_sgu_spatial, v_g_sgu, v_w_branch_attn, v_w_branch_sgu, v_w_out, v_g_cross, v_g_mem, v_w_q_cross, v_w_kv_cross, v_w_o_cross, v_g_ffn, v_w_gate_up, v_w_down, v_g_final):
    S = x.shape[1]
    xs, tgt, mems = x.reshape(S, D), loss_target.reshape(S, D), mem.reshape(mem.shape[1], D)
    g_final2 = g_final.reshape(1, D)

    big = [("w_in", w_in[0], m_w_in[0], v_w_in[0], True),
           ("w_branch_attn", w_branch_attn[0], m_w_branch_attn[0], v_w_branch_attn[0], True),
           ("w_branch_sgu", w_branch_sgu[0], m_w_branch_sgu[0], v_w_branch_sgu[0], True),
           ("w_out", w_out[0], m_w_out[0], v_w_out[0], False),
           ("w_q_cross", w_q_cross[0], m_w_q_cross[0], v_w_q_cross[0], False),
           ("w_kv_cross", w_kv_cross[0], m_w_kv_cross[0], v_w_kv_cross[0], False),
           ("w_o_cross", w_o_cross[0], m_w_o_cross[0], v_w_o_cross[0], True),
           ("w_gate_up", w_gate_up[0], m_w_gate_up[0], v_w_gate_up[0], True),
           ("w_down", w_down[0], m_w_down[0], v_w_down[0], False)]
    shards = [(w.T if tr else w).astype(BF16) for _, w, _, _, tr in big]
    (wt_in,) = _gather_weights(shards[:1])
    (loss, grad_x, grads, (dg_mix, db_gate, dws, dbs_acc, dg_sgu, dg_cross, dg_mem, dg_ffn, dg_final)) = _local_step(
        xs, tgt, mems, (wt_in,) + (None,) * 8,
        (g_mix, b_gate, w_sgu_spatial[0], b_sgu_spatial[0], g_sgu, g_cross, g_mem, g_ffn, g_final2),
        _Gather(shards[1:7]), _Gather(shards[7:9]))

    own, got = _swap_halves(grads)
    sums, sums_b = [], []
    for i, (o, t) in enumerate(zip(own, got)):
        shp = o.shape
        s_, sb_ = _elementwise(lambda p, q: (p + q, p + q), [o.reshape(-1, shp[2]), t.reshape(-1, shp[2])],
                               [F32, BF16], f"chip_sum_{big[i][0]}")
        sums.append(s_.reshape(shp))
        sums_b.append(sb_.reshape(shp))
    mine, parts = _scatter_partials(sums, sums_b)
    reduced = []
    for i, (o, p) in enumerate(zip(mine, parts)):
        reduced.append(_elementwise(lambda p0, p1, p2, p3: (p0 + p1.astype(F32) + p2.astype(F32) + p3.astype(F32),),
                                    [o.astype(F32)] + [p[k] for k in range(3)], [F32], f"mesh_sum_{big[i][0]}")[0])
    full = _share_halves(reduced)

    big_out = {}
    for (name, w, m, v, tr), gsh in zip(big, full):
        gsh = gsh.T if tr else gsh
        delta, nm, nv = _elementwise(_adamw, [w, gsh, m, v], [F32, F32, F32], f"adam_{name}")
        big_out[name] = tuple(t[None] for t in (gsh, delta, nm, nv))

    small = [("g_mix", g_mix, m_g_mix, v_g_mix, dg_mix), ("b_gate", b_gate, m_b_gate, v_b_gate, db_gate),
             ("w_sgu_spatial", w_sgu_spatial, m_w_sgu_spatial, v_w_sgu_spatial, jnp.tril(dws)),
             ("b_sgu_spatial", b_sgu_spatial, m_b_sgu_spatial, v_b_sgu_spatial, jnp.sum(dbs_acc, axis=-1)),
             ("g_sgu", g_sgu, m_g_sgu, v_g_sgu, dg_sgu), ("g_cross", g_cross, m_g_cross, v_g_cross, dg_cross),
             ("g_mem", g_mem, m_g_mem, v_g_mem, dg_mem), ("g_ffn", g_ffn, m_g_ffn, v_g_ffn, dg_ffn),
             ("g_final", g_final, m_g_final, v_g_final, dg_final)]

    def pack(parts_, tail):
        return jnp.concatenate([p.reshape(-1) for p in parts_] + [tail]).reshape(-1, 128)

    zeros = jnp.zeros((1024,), F32)
    gp = pack([s[4] for s in small], jnp.pad(loss.reshape(-1)[:1], (0, 1023)))
    wp, mp, vp = (pack([s[k] for s in small], zeros) for k in (1, 2, 3))
    gsum, dsm, nms, nvs = _adam_small(wp, mp, vp, _gather_small(gp))
    small_out, off = {}, 0
    for name, w, _, _, _ in small:
        n = w.size
        small_out[name] = tuple(t.reshape(-1)[off:off + n].reshape(w.shape) for t in (gsum, dsm, nms, nvs))
        off += n
    total_loss = gsum.reshape(-1)[off]

    order = ["g_mix", "w_in", "b_gate", "w_sgu_spatial", "b_sgu_spatial", "g_sgu", "w_branch_attn", "w_branch_sgu",
             "w_out", "g_cross", "g_mem", "w_q_cross", "w_kv_cross", "w_o_cross", "g_ffn", "w_gate_up", "w_down",
             "g_final"]
    res = {**big_out, **small_out}
    outs = [total_loss, grad_x.reshape(x.shape)]
    for k in range(4):
        outs += [res[nm][k] for nm in order]
    return tuple(outs)
```

```python
import math

import numpy as np
import jax
import jax.numpy as jnp
from jax import lax
from jax.experimental import pallas as pl
from jax.experimental.pallas import tpu as pltpu

F32, BF16 = jnp.float32, jnp.bfloat16
MESH = pl.DeviceIdType.MESH
ANY = pl.BlockSpec(memory_space=pl.ANY)
RES = pl.BlockSpec(memory_space=pltpu.VMEM)

D = 1024
HEAD = 64
GROUP_W = 256
DIL_GROUPS = ((128, 1), (512, 4), (2048, 16))
BLK = 128
SGU_W = 512
MEM_HEADS, MEM_HD, MEM_W = 4, 128, 512
D_FF = 2816
FF_CHUNK = 256
EPS = 1e-6
NEG = -1e30
LR, B1, B2, AEPS, WD, STEP = 0.001, 0.9, 0.999, 1e-08, 0.01, 10
GELU_K, GELU_C = 0.7978845608028654, 0.044715


def _dot(a, b):
    return jnp.dot(a, b, preferred_element_type=F32)


def _dot_nt(a, b):
    return lax.dot_general(a, b, (((1,), (1,)), ((), ())), preferred_element_type=F32)


def _dot_tn(a, b):
    return lax.dot_general(a, b, (((0,), (0,)), ((), ())), preferred_element_type=F32)


def _row(tm, w):
    return pl.BlockSpec((tm, w), lambda i: (i, 0))


def _acc(shape):
    return pl.BlockSpec(shape, lambda i: (0,) * len(shape))


def _params(sem, mb):
    return pltpu.CompilerParams(dimension_semantics=sem, vmem_limit_bytes=mb << 20)


def _sds(shape, dt):
    return jax.ShapeDtypeStruct(shape, dt)


def _rms(h):
    return lax.rsqrt(jnp.mean(h * h, axis=-1, keepdims=True) + EPS)


def _rms_bwd(dy, h, r, g):
    t = dy * g
    dh = r * t - h * (r * r * r) * jnp.mean(t * h, axis=-1, keepdims=True)
    return dh, dy * h * r


def _gelu(x):
    t = jnp.tanh(GELU_K * (x + GELU_C * x * x * x))
    return 0.5 * x * (1.0 + t), t


def _gelu_grad(x, t):
    return 0.5 * (1.0 + t) + 0.5 * x * (1.0 - t * t) * GELU_K * (1.0 + 3.0 * GELU_C * x * x)


def _alibi_slopes():
    def pow2(n):
        start = 2.0 ** (-8.0 / n)
        return [start ** (i + 1) for i in range(n)]
    n = 12
    c = 2 ** int(math.floor(math.log2(n)))
    s = pow2(c) + pow2(2 * c)[0::2][: n - c]
    return np.array(sorted(s, reverse=True), dtype=np.float32).reshape(3, 4)


def _attn_bias(g):
    win, dil = DIL_GROUPS[g]
    steps = (np.arange(BLK)[:, None] + BLK) - np.arange(2 * BLK)[None, :]
    valid = (steps >= 0) & (steps <= win // dil)
    dist = (np.clip(steps, 0, None) * dil).astype(np.float32)
    b = -_alibi_slopes()[g][:, None, None] * dist[None]
    return np.where(valid[None], b, NEG).astype(np.float32)


def _head_masks():
    lane = lax.broadcasted_iota(jnp.int32, (1, GROUP_W), 1)
    return lane, [(lane >= HEAD * h) & (lane < HEAD * (h + 1)) for h in range(4)]


ATT_NB = 4


def _stack_heads(t, masks):
    z = jnp.zeros_like(t)
    return jnp.concatenate([jnp.where(m, t, z) for m in masks], axis=0)


def _unstack_heads(t, masks):
    out = jnp.zeros((BLK, GROUP_W), t.dtype)
    for h, m in enumerate(masks):
        out = jnp.where(m, t[h * BLK:(h + 1) * BLK], out)
    return out


def _stack_cols(ref, rows):
    return jnp.concatenate([ref[rows, HEAD * h:HEAD * h + 1] for h in range(4)], axis=0)


def _dil_spec(d, tm, w):
    return pl.BlockSpec((d, tm // d, w), lambda i: (0, i, 0))


def _to_dilated(val, s_ref, d, write):
    tm, w = val.shape
    for j in range(w // 128):
        s_ref[j, pl.ds(0, tm), :] = val[:, j * 128:(j + 1) * 128]
    for r in range(d):
        for j in range(w // 128):
            write(r, j, s_ref[j, pl.ds(r, tm // d, stride=d), :])


def _from_dilated(ref, s_ref, d, tm, w):
    if d == 1:
        return ref[0].astype(F32)
    for r in range(d):
        for j in range(w // 128):
            s_ref[j, pl.ds(r, tm // d, stride=d), :] = ref[r, :, j * 128:(j + 1) * 128].astype(F32)
    return jnp.concatenate([s_ref[j, pl.ds(0, tm), :] for j in range(w // 128)], axis=1)


def _fwd_in(x, g_mix, wt_in, gather=None, tm=512):
    S = x.shape[0]
    dils = [d for _, d in DIL_GROUPS]
    n = 0 if gather is None else gather.n
    last = S // tm - 1

    def body(*refs):
        x_ref, g_ref, w_ref = refs[:3]
        a_ref, q0_ref, q1_ref, q2_ref, uv_ref, gl_ref = refs[3 + n:9 + n]
        s_ref = refs[9 + 2 * n]
        comm = (refs[3:3 + n], refs[9 + n:9 + 2 * n], refs[10 + 2 * n:])
        if gather is not None:
            pl.when(pl.program_id(0) == 0)(lambda: gather.start(*comm))
        xv = x_ref[...]
        a = (xv * _rms(xv) * g_ref[...]).astype(BF16)
        a_ref[...] = a
        for g, (d, out) in enumerate(zip(dils, (q0_ref, q1_ref, q2_ref))):
            for part in range(3):
                rows = part * 768 + g * 256
                val = _dot_nt(a, w_ref[rows:rows + 256, :])
                if d == 1:
                    out[0, :, part * 256:(part + 1) * 256] = val.astype(BF16)
                else:
                    def write(r, j, piece, out=out, part=part):
                        out[r, :, part * 256 + j * 128:part * 256 + (j + 1) * 128] = piece.astype(BF16)
                    _to_dilated(val, s_ref, d, write)
        uv_ref[...] = _dot_nt(a, w_ref[2304:3328, :]).astype(BF16)
        gl_ref[...] = _dot_nt(a, w_ref[3328:5376, :]).astype(BF16)
        if gather is not None:
            pl.when(pl.program_id(0) == last)(lambda: gather.finish(*comm))

    outs = pl.pallas_call(
        body, grid=(S // tm,), name="fwd_in",
        in_specs=[_row(tm, D), RES, RES] + [ANY] * n,
        out_specs=[_row(tm, D)] + [_dil_spec(d, tm, 768) for d in dils] + [_row(tm, 1024), _row(tm, 2048)] + [ANY] * n,
        out_shape=[_sds((S, D), BF16)] + [_sds((d, S // d, 768), BF16) for d in dils]
        + [_sds((S, 1024), BF16), _sds((S, 2048), BF16)] + ([] if gather is None else gather.out_shape),
        scratch_shapes=[pltpu.VMEM((2, tm, 128), F32)] + ([] if gather is None else gather.scratch),
        compiler_params=_params(("arbitrary",), 60),
    )(x, g_mix, wt_in, *([] if gather is None else gather.halves))
    return outs[:6], ([] if gather is None else gather.full(outs[6:]))


def _attn_fwd(qkv, g):
    d, L, _ = qkv.shape
    nb = L // BLK
    bias = jnp.asarray(_attn_bias(g).reshape(4 * BLK, 2 * BLK))
    NB = min(ATT_NB, nb)
    W = NB * BLK

    def body(q_ref, kc_ref, kp_ref, vc_ref, vp_ref, b_ref, o_ref, l_ref):
        st = pl.program_id(1)
        k_all = jnp.concatenate([kp_ref[...], kc_ref[...]], axis=0)
        v_all = jnp.concatenate([vp_ref[...], vc_ref[...]], axis=0)
        lane, masks = _head_masks()
        for b in range(NB):
            rows = slice(b * BLK, (b + 1) * BLK)
            kk, vv = k_all[b * BLK:(b + 2) * BLK], v_all[b * BLK:(b + 2) * BLK]
            s = _dot_nt(_stack_heads(q_ref[rows, :], masks), kk) * 0.125 + b_ref[...]
            if b == 0:
                s = s + jnp.where((st == 0) & (lane < BLK), NEG, 0.0).astype(F32)
            mx = jnp.max(s, axis=-1, keepdims=True)
            e = jnp.exp(s - mx)
            den = jnp.sum(e, axis=-1, keepdims=True)
            o_ref[rows, :] = _unstack_heads(_dot(e.astype(BF16), vv) / den, masks)
            l_ref[rows, :] = _unstack_heads(mx + jnp.log(den), masks)

    def wide(col):
        return pl.BlockSpec((None, W, GROUP_W), lambda r, s: (r, s, col))

    def before(col):
        return pl.BlockSpec((None, BLK, GROUP_W), lambda r, s: (r, jnp.maximum(s * NB - 1, 0), col))

    return pl.pallas_call(
        body, grid=(d, nb // NB), name=f"attn_fwd_g{g}",
        in_specs=[wide(0), wide(1), before(1), wide(2), before(2),
                  pl.BlockSpec((4 * BLK, 2 * BLK), lambda r, s: (0, 0))],
        out_specs=[wide(0), wide(0)],
        out_shape=[_sds((d, L, GROUP_W), F32), _sds((d, L, GROUP_W), F32)],
        compiler_params=_params(("parallel", "parallel"), 32),
    )(qkv, qkv, qkv, qkv, qkv, bias)


def _group_weights(l0, l1, l2):
    m = jnp.maximum(jnp.maximum(l0, l1), l2)
    e0, e1, e2 = jnp.exp(l0 - m), jnp.exp(l1 - m), jnp.exp(l2 - m)
    inv = 1.0 / (e0 + e1 + e2)
    return e0 * inv, e1 * inv, e2 * inv


def _sgu_forward(uvf, gs, wt_ref, bst_ref, mixed_s, tm):
    z, t = _gelu(uvf)
    u, v = z[:, :SGU_W], z[:, SGU_W:]
    rv = _rms(v)
    vnb = (v * rv * gs).astype(BF16)
    for ci in range(tm // 128):
        for g in range(4):
            rs, cs = slice(ci * 128, (ci + 1) * 128), slice(g * 128, (g + 1) * 128)
            mixed_s[rs, cs] = _dot(wt_ref[g], vnb[rs, cs]) + bst_ref[:, g:g + 1]
    return u, v, rv, vnb, t


def _fwd_mid(x, os_, ls_, uv, gl, wt, bst, g_sgu, b_gate, wt_ba, wt_bs, w_out, gather=None, tm=256):
    S = x.shape[0]
    dils = [d for _, d in DIL_GROUPS]
    n = 0 if gather is None else gather.n
    last = S // tm - 1

    def body(*refs):
        (x_ref, o0, o1, o2, l0, l1, l2, uv_ref, gl_ref, wt_ref, bst_ref, gs_ref, bg_ref, wba_ref, wbs_ref,
         wo_ref) = refs[:16]
        ya_ref, ys_ref, ba_ref, bs_ref, mg_ref, h1_ref = refs[16 + n:22 + n]
        mixed_s, il_s = refs[22 + 2 * n:24 + 2 * n]
        comm = (refs[16:16 + n], refs[22 + n:22 + 2 * n], refs[24 + 2 * n:])
        if gather is not None:
            pl.when(pl.program_id(0) == 0)(lambda: gather.start(*comm))
        ls = [_from_dilated(r, il_s, d, tm, GROUP_W) for r, d in zip((l0, l1, l2), dils)]
        alphas = _group_weights(*ls)
        ya = jnp.zeros((tm, GROUP_W), F32)
        for a, r, d in zip(alphas, (o0, o1, o2), dils):
            ya = ya + a * _from_dilated(r, il_s, d, tm, GROUP_W)
        yab = ya.astype(BF16)
        ya_ref[...] = yab
        u, _, _, _, _ = _sgu_forward(uv_ref[...].astype(F32), gs_ref[...], wt_ref, bst_ref, mixed_s, tm)
        ysb = (u * mixed_s[...]).astype(BF16)
        ys_ref[...] = ysb
        gates = jax.nn.sigmoid(gl_ref[...].astype(F32) + bg_ref[...])
        ba = _dot_nt(yab, wba_ref[...])
        bs = _dot_nt(ysb, wbs_ref[...])
        ba_ref[...] = ba.astype(BF16)
        bs_ref[...] = bs.astype(BF16)
        mgb = (gates[:, :D] * ba + gates[:, D:] * bs).astype(BF16)
        mg_ref[...] = mgb
        h1_ref[...] = x_ref[...] + _dot(mgb, wo_ref[...])
        if gather is not None:
            pl.when(pl.program_id(0) == last)(lambda: gather.finish(*comm))

    gw = _row(tm, GROUP_W)
    dil = [_dil_spec(d, tm, GROUP_W) for d in dils]
    outs = pl.pallas_call(
        body, grid=(S // tm,), name="fwd_mid",
        in_specs=[_row(tm, D)] + dil + dil + [_row(tm, 1024), _row(tm, 2048)] + [RES] * 7 + [ANY] * n,
        out_specs=[gw, _row(tm, SGU_W), _row(tm, D), _row(tm, D), _row(tm, D), _row(tm, D)] + [ANY] * n,
        out_shape=[_sds((S, GROUP_W), BF16), _sds((S, SGU_W), BF16), _sds((S, D), BF16), _sds((S, D), BF16),
                   _sds((S, D), BF16), _sds((S, D), F32)] + ([] if gather is None else gather.out_shape),
        scratch_shapes=[pltpu.VMEM((tm, SGU_W), F32), pltpu.VMEM((2, tm, 128), F32)]
        + ([] if gather is None else gather.scratch),
        compiler_params=_params(("arbitrary",), 48),
    )(x, *os_, *ls_, uv, gl, wt, bst, g_sgu, b_gate, wt_ba, wt_bs, w_out, *([] if gather is None else gather.halves))
    return outs[:6], ([] if gather is None else gather.full(outs[6:]))


def _mem_fwd(mem, g_mem, w_kv):
    def body(m_ref, g_ref, w_ref, mb_ref, kv_ref):
        mv = m_ref[...]
        mb = (mv * _rms(mv) * g_ref[...]).astype(BF16)
        mb_ref[...] = mb
        kv_ref[...] = _dot(mb, w_ref[...]).astype(BF16)

    return pl.pallas_call(
        body, name="mem_fwd", in_specs=[RES, RES, RES], out_specs=[RES, RES],
        out_shape=[_sds(mem.shape, BF16), _sds((mem.shape[0], 2 * MEM_W), BF16)],
        compiler_params=pltpu.CompilerParams(vmem_limit_bytes=32 << 20),
    )(mem, g_mem, w_kv)


def _cross_probs(qh, kh):
    s = _dot_nt(qh, kh) * (MEM_HD ** -0.5)
    e = jnp.exp(s - jnp.max(s, axis=-1, keepdims=True))
    return e / jnp.sum(e, axis=-1, keepdims=True)


def _fwd_cross(h1, g_cross, w_q, kv, wt_o, tm=512):
    S = h1.shape[0]

    def body(h_ref, g_ref, wq_ref, kv_ref, wo_ref, c_ref, qc_ref, oc_ref, h2_ref):
        hv = h_ref[...]
        cb = (hv * _rms(hv) * g_ref[...]).astype(BF16)
        c_ref[...] = cb
        qcb = _dot(cb, wq_ref[...]).astype(BF16)
        qc_ref[...] = qcb
        for h in range(MEM_HEADS):
            cs = slice(h * MEM_HD, (h + 1) * MEM_HD)
            p = _cross_probs(qcb[:, cs], kv_ref[:, cs])
            oc_ref[:, cs] = _dot(p.astype(BF16), kv_ref[:, MEM_W + h * MEM_HD:MEM_W + (h + 1) * MEM_HD]).astype(BF16)
        h2_ref[...] = hv + _dot_nt(oc_ref[...], wo_ref[...])

    return pl.pallas_call(
        body, grid=(S // tm,), name="fwd_cross",
        in_specs=[_row(tm, D), RES, RES, RES, RES],
        out_specs=[_row(tm, D), _row(tm, MEM_W), _row(tm, MEM_W), _row(tm, D)],
        out_shape=[_sds((S, D), BF16), _sds((S, MEM_W), BF16), _sds((S, MEM_W), BF16), _sds((S, D), F32)],
        compiler_params=_params(("parallel",), 40),
    )(h1, g_cross, w_q, kv, wt_o)


def _ffn_fwd_bwd(h2, target, g_ffn, g_final, wt_gu, w_down, tm=256):
    S = h2.shape[0]
    nch = D_FF // FF_CHUNK

    def body(h_ref, t_ref, gf_ref, gz_ref, wgu_ref, wd_ref,
             f_ref, act_ref, dgu_ref, dh3b_ref, dh2_ref, dh2b_ref, dgf_ref, dgz_ref, loss_ref, gu_s):
        i = pl.program_id(0)
        hv = h_ref[...]
        r2 = _rms(hv)
        gf = gf_ref[...]
        fb = (hv * r2 * gf).astype(BF16)
        f_ref[...] = fb
        h3 = hv
        for c in range(nch):
            cs = slice(c * FF_CHUNK, (c + 1) * FF_CHUNK)
            us = slice(D_FF + c * FF_CHUNK, D_FF + (c + 1) * FF_CHUNK)
            gt = _dot_nt(fb, wgu_ref[cs, :])
            up = _dot_nt(fb, wgu_ref[us, :])
            gu_s[:, cs] = gt
            gu_s[:, us] = up
            actb = (gt * jax.nn.sigmoid(gt) * up).astype(BF16)
            act_ref[:, cs] = actb
            h3 = h3 + _dot(actb, wd_ref[cs, :])
        r3 = _rms(h3)
        gz = gz_ref[...]
        diff = h3 * r3 * gz - t_ref[...]
        dy = diff * (1.0 / D)
        dh3, dgz_rows = _rms_bwd(dy, h3, r3, gz)
        dh3b = dh3.astype(BF16)
        dh3b_ref[...] = dh3b
        df = jnp.zeros((tm, D), F32)
        for c in range(nch):
            cs = slice(c * FF_CHUNK, (c + 1) * FF_CHUNK)
            us = slice(D_FF + c * FF_CHUNK, D_FF + (c + 1) * FF_CHUNK)
            dact = _dot_nt(dh3b, wd_ref[cs, :])
            gt, up = gu_s[:, cs], gu_s[:, us]
            sg = jax.nn.sigmoid(gt)
            dgt = (dact * up * (sg * (1.0 + gt * (1.0 - sg)))).astype(BF16)
            dup = (dact * (gt * sg)).astype(BF16)
            dgu_ref[:, cs] = dgt
            dgu_ref[:, us] = dup
            df = df + _dot(dgt, wgu_ref[cs, :]) + _dot(dup, wgu_ref[us, :])
        dhn, dgf_rows = _rms_bwd(df, hv, r2, gf)
        dh2 = dh3 + dhn
        dh2_ref[...] = dh2
        dh2b_ref[...] = dh2.astype(BF16)

        @pl.when(i == 0)
        def _():
            dgf_ref[...] = jnp.zeros_like(dgf_ref)
            dgz_ref[...] = jnp.zeros_like(dgz_ref)
            loss_ref[...] = jnp.zeros_like(loss_ref)

        dgf_ref[...] += jnp.sum(dgf_rows, axis=0, keepdims=True)
        dgz_ref[...] += jnp.sum(dgz_rows, axis=0, keepdims=True)
        loss_ref[...] += jnp.sum(jnp.sum(diff * diff, axis=0, keepdims=True), axis=1, keepdims=True) * (0.5 / D)

    return pl.pallas_call(
        body, grid=(S // tm,), name="ffn_fwd_bwd",
        in_specs=[_row(tm, D), _row(tm, D), RES, RES, RES, RES],
        out_specs=[_row(tm, D), _row(tm, D_FF), _row(tm, 2 * D_FF), _row(tm, D), _row(tm, D), _row(tm, D),
                   _acc((1, D)), _acc((1, D)), _acc((1, 128))],
        out_shape=[_sds((S, D), BF16), _sds((S, D_FF), BF16), _sds((S, 2 * D_FF), BF16), _sds((S, D), BF16),
                   _sds((S, D), F32), _sds((S, D), BF16), _sds((1, D), F32), _sds((1, D), F32), _sds((1, 128), F32)],
        scratch_shapes=[pltpu.VMEM((tm, 2 * D_FF), F32)],
        compiler_params=_params(("arbitrary",), 56),
    )(h2, target, g_ffn, g_final, wt_gu, w_down)


def _bwd_cross(dh2, h1, qc, g_cross, w_q, kv, wt_o, comm=None, tm=256):
    S = h1.shape[0]
    n = 0 if comm is None else comm.n
    last = S // tm - 1

    def body(*refs):
        d_ref, h_ref, qc_ref, g_ref, wq_ref, kv_ref, wo_ref = refs[:7]
        dqc_ref, dh1_ref, dh1b_ref, dkv_ref, dg_ref = refs[7 + n:12 + n]
        cargs = (refs[7:7 + n], refs[12 + n:12 + 2 * n], refs[12 + 2 * n:])
        i = pl.program_id(0)

        @pl.when(i == 0)
        def _():
            dkv_ref[...] = jnp.zeros_like(dkv_ref)
            dg_ref[...] = jnp.zeros_like(dg_ref)
            if comm is not None:
                comm.start(*cargs)

        dh2 = d_ref[...]
        doc = _dot(dh2.astype(BF16), wo_ref[...])
        qcb = qc_ref[...]
        for h in range(MEM_HEADS):
            cs = slice(h * MEM_HD, (h + 1) * MEM_HD)
            vs = slice(MEM_W + h * MEM_HD, MEM_W + (h + 1) * MEM_HD)
            qh, kh, vh = qcb[:, cs], kv_ref[:, cs], kv_ref[:, vs]
            p = _cross_probs(qh, kh)
            dohb = doc[:, cs].astype(BF16)
            dp = _dot_nt(dohb, vh)
            dsb = (p * (dp - jnp.sum(dp * p, axis=-1, keepdims=True)) * (MEM_HD ** -0.5)).astype(BF16)
            dqc_ref[:, cs] = _dot(dsb, kh).astype(BF16)
            dkv_ref[:, cs] += _dot_tn(dsb, qh)
            dkv_ref[:, vs] += _dot_tn(p.astype(BF16), dohb)
        dc = _dot_nt(dqc_ref[...], wq_ref[...])
        hv = h_ref[...]
        dhn, dg_rows = _rms_bwd(dc, hv, _rms(hv), g_ref[...])
        dh1 = dh2 + dhn
        dh1_ref[...] = dh1
        dh1b_ref[...] = dh1.astype(BF16)
        dg_ref[...] += jnp.sum(dg_rows, axis=0, keepdims=True)
        if comm is not None:
            pl.when(i == last)(lambda: comm.finish(*cargs))

    outs = pl.pallas_call(
        body, grid=(S // tm,), name="bwd_cross",
        in_specs=[_row(tm, D), _row(tm, D), _row(tm, MEM_W), RES, RES, RES, RES] + [ANY] * n,
        out_specs=[_row(tm, MEM_W), _row(tm, D), _row(tm, D), _acc((256, 2 * MEM_W)), _acc((1, D))] + [ANY] * n,
        out_shape=[_sds((S, MEM_W), BF16), _sds((S, D), F32), _sds((S, D), BF16), _sds((256, 2 * MEM_W), F32),
                   _sds((1, D), F32)] + ([] if comm is None else comm.out_shape),
        scratch_shapes=[] if comm is None else comm.scratch,
        compiler_params=_params(("arbitrary",), 40),
    )(dh2, h1, qc, g_cross, w_q, kv, wt_o, *([] if comm is None else comm.ins))
    return outs[:5], outs[5:]


def _mem_bwd(dkv, mem, mb, g_mem, w_kv):
    def body(dkv_ref, m_ref, mb_ref, g_ref, w_ref, dw_ref, dg_ref):
        dkvb = dkv_ref[...].astype(BF16)
        dw_ref[...] = _dot_tn(mb_ref[...], dkvb)
        dm = _dot_nt(dkvb, w_ref[...])
        mv = m_ref[...]
        dg_ref[...] = jnp.sum(dm * mv * _rms(mv), axis=0, keepdims=True)

    return pl.pallas_call(
        body, name="mem_bwd", in_specs=[RES] * 5, out_specs=[RES, RES],
        out_shape=[_sds((D, 2 * MEM_W), F32), _sds((1, D), F32)],
        compiler_params=pltpu.CompilerParams(vmem_limit_bytes=32 << 20),
    )(dkv, mem, mb, g_mem, w_kv)


def _bwd_mid(dh1, gl, ba, bs, uv, ls_, ya, wt, bst, g_sgu, b_gate, wt_ba, wt_bs, w_out, tm=256):
    S = dh1.shape[0]
    dils = [d for _, d in DIL_GROUPS]

    def body(d_ref, gl_ref, ba_ref, bs_ref, uv_ref, l0, l1, l2, ya_ref,
             wt_ref, bst_ref, gs_ref, bg_ref, wba_ref, wbs_ref, wo_ref,
             dba_ref, dbs_ref, dgl_ref, duv_ref, do0, do1, do2, c0, c1, c2,
             dbg_ref, dgs_ref, dws_ref, dbsa_ref, mixed_s, dvn_s, il_s):
        i = pl.program_id(0)

        @pl.when(i == 0)
        def _():
            for r in (dbg_ref, dgs_ref, dws_ref, dbsa_ref):
                r[...] = jnp.zeros_like(r)

        dm = _dot_nt(d_ref[...].astype(BF16), wo_ref[...])
        gates = jax.nn.sigmoid(gl_ref[...].astype(F32) + bg_ref[...])
        g0, g1 = gates[:, :D], gates[:, D:]
        dbab = (dm * g0).astype(BF16)
        dbsb = (dm * g1).astype(BF16)
        dba_ref[...] = dbab
        dbs_ref[...] = dbsb
        dg0 = dm * ba_ref[...].astype(F32) * g0 * (1.0 - g0)
        dg1 = dm * bs_ref[...].astype(F32) * g1 * (1.0 - g1)
        dgl_ref[:, :D] = dg0.astype(BF16)
        dgl_ref[:, D:] = dg1.astype(BF16)
        dbg_ref[:, :D] += jnp.sum(dg0, axis=0, keepdims=True)
        dbg_ref[:, D:] += jnp.sum(dg1, axis=0, keepdims=True)
        dya = _dot(dbab, wba_ref[...])
        dys = _dot(dbsb, wbs_ref[...])

        uvf = uv_ref[...].astype(F32)
        gs = gs_ref[...]
        u, v, rv, vnb, t = _sgu_forward(uvf, gs, wt_ref, bst_ref, mixed_s, tm)
        du = dys * mixed_s[...]
        dmixed = dys * u
        for ci in range(tm // 128):
            for g in range(4):
                rs, cs = slice(ci * 128, (ci + 1) * 128), slice(g * 128, (g + 1) * 128)
                dmx = dmixed[rs, cs]
                dmxb = dmx.astype(BF16)
                dvn_s[rs, cs] = _dot_tn(wt_ref[g], dmxb)
                dws_ref[g] += _dot_nt(dmxb, vnb[rs, cs])
                dbsa_ref[g] += dmx
        dv, dgs_rows = _rms_bwd(dvn_s[...], v, rv, gs)
        dgs_ref[...] += jnp.sum(dgs_rows, axis=0, keepdims=True)
        gg = _gelu_grad(uvf, t)
        duv_ref[:, :SGU_W] = (du * gg[:, :SGU_W]).astype(BF16)
        duv_ref[:, SGU_W:] = (dv * gg[:, SGU_W:]).astype(BF16)

        alphas = _group_weights(*[_from_dilated(r, il_s, d, tm, GROUP_W) for r, d in zip((l0, l1, l2), dils)])
        prod = dya * ya_ref[...].astype(F32)
        _, masks = _head_masks()
        hs = jnp.zeros_like(prod)
        for h in range(4):
            sh = jnp.sum(jnp.where(masks[h], prod, 0.0), axis=-1, keepdims=True)
            hs = jnp.where(masks[h], sh, hs)
        for a, d, do_ref, c_ref in zip(alphas, dils, (do0, do1, do2), (c0, c1, c2)):
            for val, out in ((a * dya, do_ref), (a * hs, c_ref)):
                if d == 1:
                    out[0] = val.astype(out.dtype)
                else:
                    def write(r, j, piece, out=out):
                        out[r, :, j * 128:(j + 1) * 128] = piece.astype(out.dtype)
                    _to_dilated(val, il_s, d, write)

    gw = _row(tm, GROUP_W)
    dil = [_dil_spec(d, tm, GROUP_W) for d in dils]
    return pl.pallas_call(
        body, grid=(S // tm,), name="bwd_mid",
        in_specs=[_row(tm, D), _row(tm, 2048), _row(tm, D), _row(tm, D), _row(tm, 1024)] + dil + [gw] + [RES] * 7,
        out_specs=[_row(tm, D), _row(tm, D), _row(tm, 2048), _row(tm, 1024)] + dil + dil
        + [_acc((1, 2048)), _acc((1, SGU_W)), _acc((4, 128, 128)), _acc((4, 128, 128))],
        out_shape=[_sds((S, D), BF16), _sds((S, D), BF16), _sds((S, 2048), BF16), _sds((S, 1024), BF16)]
        + [_sds((d, S // d, GROUP_W), BF16) for d in dils] + [_sds((d, S // d, GROUP_W), F32) for d in dils]
        + [_sds((1, 2048), F32), _sds((1, SGU_W), F32), _sds((4, 128, 128), F32), _sds((4, 128, 128), F32)],
        scratch_shapes=[pltpu.VMEM((tm, SGU_W), F32), pltpu.VMEM((tm, SGU_W), F32), pltpu.VMEM((2, tm, 128), F32)],
        compiler_params=_params(("arbitrary",), 48),
    )(dh1, gl, ba, bs, uv, *ls_, ya, wt, bst, g_sgu, b_gate, wt_ba, wt_bs, w_out)


def _attn_bwd(qkv, do, lse, corr, g):
    d, L, _ = qkv.shape
    nb = L // BLK
    NB = min(ATT_NB, nb)
    W = NB * BLK
    nsteps = nb // NB
    bias = jnp.asarray(_attn_bias(g).reshape(4 * BLK, 2 * BLK))

    def body(q_ref, kc_ref, kp_ref, vc_ref, vp_ref, do_ref, l_ref, c_ref, qn_ref, don_ref, ln_ref, cn_ref, b_ref,
             out_ref, dk_s, dv_s):
        st = pl.program_id(1)
        k_all = jnp.concatenate([kp_ref[...], kc_ref[...]], axis=0)
        v_all = jnp.concatenate([vp_ref[...], vc_ref[...]], axis=0)
        lane, masks = _head_masks()
        dk_s[...] = jnp.zeros_like(dk_s)
        dv_s[...] = jnp.zeros_like(dv_s)

        def block_terms(qs, dos, kk, vv, bias_v, lse_c, corr_c):
            s = _dot_nt(qs, kk) * 0.125 + bias_v
            p = jnp.exp(s - lse_c)
            dsb = (p * (_dot_nt(dos, vv) - corr_c) * 0.125).astype(BF16)
            return dsb, p.astype(BF16)

        for b in range(NB):
            rows = slice(b * BLK, (b + 1) * BLK)
            keys = slice(b * BLK, (b + 2) * BLK)
            kk, vv = k_all[keys], v_all[keys]
            qs, dos = _stack_heads(q_ref[rows, :], masks), _stack_heads(do_ref[rows, :], masks)
            bias_v = b_ref[...]
            if b == 0:
                bias_v = bias_v + jnp.where((st == 0) & (lane < BLK), NEG, 0.0).astype(F32)
            dsb, pb = block_terms(qs, dos, kk, vv, bias_v, _stack_cols(l_ref, rows), _stack_cols(c_ref, rows))
            out_ref[rows, 0:GROUP_W] = _unstack_heads(_dot(dsb, kk), masks).astype(BF16)
            dk_s[keys, :] += _dot_tn(dsb, qs)
            dv_s[keys, :] += _dot_tn(pb, dos)

        @pl.when(st < nsteps - 1)
        def _():
            last = slice(NB * BLK, (NB + 1) * BLK)
            qs, dos = _stack_heads(qn_ref[...], masks), _stack_heads(don_ref[...], masks)
            every = slice(None)
            dsb, pb = block_terms(qs, dos, k_all[last], v_all[last], b_ref[:, :BLK],
                                  _stack_cols(ln_ref, every), _stack_cols(cn_ref, every))
            dk_s[last, :] += _dot_tn(dsb, qs)
            dv_s[last, :] += _dot_tn(pb, dos)

        out_ref[:, GROUP_W:2 * GROUP_W] = dk_s[BLK:, :].astype(BF16)
        out_ref[:, 2 * GROUP_W:] = dv_s[BLK:, :].astype(BF16)

    def wide(col, w=GROUP_W):
        return pl.BlockSpec((None, W, w), lambda r, s: (r, s, col))

    def before(col):
        return pl.BlockSpec((None, BLK, GROUP_W), lambda r, s: (r, jnp.maximum(s * NB - 1, 0), col))

    def after(col):
        return pl.BlockSpec((None, BLK, GROUP_W), lambda r, s: (r, jnp.minimum((s + 1) * NB, nb - 1), col))

    return pl.pallas_call(
        body, grid=(d, nsteps), name=f"attn_bwd_g{g}",
        in_specs=[wide(0), wide(1), before(1), wide(2), before(2), wide(0), wide(0), wide(0),
                  after(0), after(0), after(0), after(0), pl.BlockSpec((4 * BLK, 2 * BLK), lambda r, s: (0, 0))],
        out_specs=wide(0, 768),
        out_shape=_sds((d, L, 768), BF16),
        scratch_shapes=[pltpu.VMEM(((NB + 1) * BLK, GROUP_W), F32), pltpu.VMEM(((NB + 1) * BLK, GROUP_W), F32)],
        compiler_params=_params(("parallel", "parallel"), 32),
    )(qkv, qkv, qkv, qkv, qkv, do, lse, corr, qkv, do, lse, corr, bias)


def _bwd_in(dqkvs, duv, dgl, dh1, x, g_mix, wt_in, tm=512):
    S = x.shape[0]
    dils = [d for _, d in DIL_GROUPS]

    def body(q0_ref, q1_ref, q2_ref, duv_ref, dgl_ref, d_ref, x_ref, g_ref, w_ref, dx_ref, dp_ref, dg_ref, il_s):
        i = pl.program_id(0)

        @pl.when(i == 0)
        def _():
            dg_ref[...] = jnp.zeros_like(dg_ref)

        for g, (d, ref) in enumerate(zip(dils, (q0_ref, q1_ref, q2_ref))):
            nat = _from_dilated(ref, il_s, d, tm, 768).astype(BF16)
            for part in range(3):
                col = part * 768 + g * 256
                dp_ref[:, col:col + 256] = nat[:, part * 256:(part + 1) * 256]
        dp_ref[:, 2304:3328] = duv_ref[...]
        dp_ref[:, 3328:5376] = dgl_ref[...]
        da = _dot(dp_ref[...], w_ref[...])
        xv = x_ref[...]
        dxn, dg_rows = _rms_bwd(da, xv, _rms(xv), g_ref[...])
        dx_ref[...] = d_ref[...] + dxn
        dg_ref[...] += jnp.sum(dg_rows, axis=0, keepdims=True)

    return pl.pallas_call(
        body, grid=(S // tm,), name="bwd_in",
        in_specs=[_dil_spec(d, tm, 768) for d in dils] + [_row(tm, 1024), _row(tm, 2048), _row(tm, D), _row(tm, D),
                                                          RES, RES],
        out_specs=[_row(tm, D), _row(tm, 5376), _acc((1, D))],
        out_shape=[_sds((S, D), F32), _sds((S, 5376), BF16), _sds((1, D), F32)],
        scratch_shapes=[pltpu.VMEM((6, tm, 128), F32)],
        compiler_params=_params(("arbitrary",), 60),
    )(*dqkvs, duv, dgl, dh1, x, g_mix, wt_in)


def _tn_matmul(a, b, name, tk, ts=2048, comm=None):
    S, K = a.shape
    N = b.shape[1]
    n = 0 if comm is None else comm.n
    nk, ns = K // tk, S // ts

    def body(*refs):
        a_ref, b_ref, o_ref = refs[0], refs[1], refs[2 + n]
        cargs = (refs[2:2 + n], refs[3 + n:3 + 2 * n], refs[3 + 2 * n:])
        k, s = pl.program_id(0), pl.program_id(1)
        if comm is not None:
            pl.when((k == 0) & (s == 0))(lambda: comm.start(*cargs))

        @pl.when(s == 0)
        def _():
            o_ref[...] = jnp.zeros_like(o_ref)

        o_ref[...] += _dot_tn(a_ref[...], b_ref[...])
        if comm is not None:
            pl.when((k == nk - 1) & (s == ns - 1))(lambda: comm.finish(*cargs))

    outs = pl.pallas_call(
        body, grid=(nk, ns), name=name,
        in_specs=[pl.BlockSpec((ts, tk), lambda k, s: (s, k)), pl.BlockSpec((ts, N), lambda k, s: (s, 0))] + [ANY] * n,
        out_specs=[pl.BlockSpec((tk, N), lambda k, s: (k, 0))] + [ANY] * n,
        out_shape=[_sds((K, N), F32)] + ([] if comm is None else comm.out_shape),
        scratch_shapes=[] if comm is None else comm.scratch,
        compiler_params=_params(("arbitrary", "arbitrary"), 48),
    )(a, b, *([] if comm is None else comm.ins))
    return outs[0] if comm is None else (outs[0], outs[1:])


def _chip_peers(x, y):
    return [(1 - x, y), (x, 1 - y), (1 - x, 1 - y)]


STAGE_BYTES = 2 << 20


def _chunk_plan(shapes, itemsize):
    plan = []
    for i, (rows, w) in enumerate(shapes):
        ch = max(16, min(rows, (STAGE_BYTES // (w * itemsize)) // 16 * 16))
        while rows % ch:
            ch -= 16
        plan += [(i, r0, ch) for r0 in range(0, rows, ch)]
    return plan


def _remote(src, dst, ssem, rsem, dev):
    return pltpu.make_async_remote_copy(src_ref=src, dst_ref=dst, send_sem=ssem, recv_sem=rsem, device_id=dev,
                                        device_id_type=MESH)


class _Gather:
    def __init__(self, shards):
        self.n = len(shards)
        self.shards = shards
        self.halves = [s.reshape(2, s.shape[0] // 2, s.shape[1]) for s in shards]
        self.plan = _chunk_plan([h.shape[1:] for h in self.halves], 2)
        self.out_shape = [_sds((4,) + h.shape, BF16) for h in self.halves]
        n = self.n
        self.scratch = [pltpu.SemaphoreType.DMA((6 * n,)), pltpu.SemaphoreType.DMA((6 * n,)),
                        pltpu.SemaphoreType.DMA((2,)), pltpu.SemaphoreType.DMA((2,)),
                        pltpu.VMEM((2, max(p[2] for p in self.plan), max(h.shape[2] for h in self.halves)), BF16)]

    def full(self, outs):
        return [o.reshape(4 * s.shape[0], s.shape[1]) for o, s in zip(outs, self.shards)]

    def _sends(self, ins, outs, ssem, rsem):
        x, y, c = lax.axis_index("x"), lax.axis_index("y"), lax.axis_index("c")
        me = 2 * x + y
        return [_remote(ins[i].at[c], outs[i].at[me, c], ssem.at[6 * i + k], rsem.at[6 * i + k], (px, py, c))
                for i in range(self.n) for k, (px, py) in enumerate(_chip_peers(x, y))]

    def start(self, ins, outs, scratch):
        ssem, rsem, lsem, osem, buf = scratch
        me = 2 * lax.axis_index("x") + lax.axis_index("y")
        for cp in self._sends(ins, outs, ssem, rsem):
            cp.start()
        pending = {}
        for i, r0, ch in self.plan:
            for h in range(2):
                if h in pending:
                    pending[h].wait()
                stage = buf.at[h, pl.ds(0, ch), pl.ds(0, self.halves[i].shape[2])]
                ld = pltpu.make_async_copy(ins[i].at[h, pl.ds(r0, ch)], stage, lsem.at[h])
                ld.start()
                ld.wait()
                st = pltpu.make_async_copy(stage, outs[i].at[me, h, pl.ds(r0, ch)], osem.at[h])
                st.start()
                pending[h] = st
        for st in pending.values():
            st.wait()

    def finish(self, ins, outs, scratch):
        ssem, rsem = scratch[:2]
        x, y, c = lax.axis_index("x"), lax.axis_index("y"), lax.axis_index("c")
        chips = _chip_peers(x, y)
        sib = (x, y, 1 - c)
        forwards = []
        for i in range(self.n):
            for k, (px, py) in enumerate(chips):
                landed = outs[i].at[2 * px + py, c]
                _remote(landed, landed, ssem.at[6 * i + k], rsem.at[6 * i + k], (px, py, c)).wait_recv()
                cp = _remote(landed, landed, ssem.at[6 * i + 3 + k], rsem.at[6 * i + 3 + k], sib)
                cp.start()
                forwards.append(cp)
        for i in range(self.n):
            for k, (px, py) in enumerate(chips):
                passed = outs[i].at[2 * px + py, 1 - c]
                _remote(passed, passed, ssem.at[6 * i + 3 + k], rsem.at[6 * i + 3 + k], sib).wait_recv()
        for cp in self._sends(ins, outs, ssem, rsem) + forwards:
            cp.wait_send()


def _gather_weights(shards):
    gt = _Gather(shards)
    n = gt.n

    def body(*refs):
        ins, outs, scratch = refs[:n], refs[n:2 * n], refs[2 * n:]
        gt.start(ins, outs, scratch)
        gt.finish(ins, outs, scratch)

    outs = pl.pallas_call(
        body, name="gather_weights", in_specs=[ANY] * n, out_specs=[ANY] * n, out_shape=gt.out_shape,
        scratch_shapes=gt.scratch, compiler_params=pltpu.CompilerParams(vmem_limit_bytes=32 << 20),
    )(*gt.halves)
    return gt.full(outs)


def _swap_halves(grads):
    n = len(grads)
    g4 = [g.reshape(4, 2, g.shape[0] // 8, g.shape[1]) for g in grads]

    def body(*refs):
        ins, got = refs[:n], refs[n:2 * n]
        ssem, rsem = refs[2 * n:]
        x, y, c = lax.axis_index("x"), lax.axis_index("y"), lax.axis_index("c")
        sib = (x, y, 1 - c)
        cps = []
        for i in range(n):
            rc = _remote(ins[i].at[:, 1 - c], got[i], ssem.at[i], rsem.at[i], sib)
            rc.start()
            cps.append(rc)
        for cp in cps:
            cp.wait()

    half = [_sds((4, g.shape[2], g.shape[3]), F32) for g in g4]
    got = pl.pallas_call(
        body, name="swap_halves", in_specs=[ANY] * n, out_specs=[ANY] * n, out_shape=half,
        scratch_shapes=[pltpu.SemaphoreType.DMA((n,)), pltpu.SemaphoreType.DMA((n,))],
    )(*g4)
    c = lax.axis_index("c")
    own = [lax.dynamic_index_in_dim(g, c, axis=1, keepdims=False) for g in g4]
    return own, got


class _Scatter:
    def __init__(self, sums_b):
        self.n = len(sums_b)
        self.ins = list(sums_b)
        self.out_shape = [_sds((3,) + s.shape[1:], BF16) for s in sums_b]
        self.scratch = [pltpu.SemaphoreType.DMA((3 * self.n,)), pltpu.SemaphoreType.DMA((3 * self.n,))]

    def _copies(self, ins, outs, scratch):
        ssem, rsem = scratch
        x, y, c = lax.axis_index("x"), lax.axis_index("y"), lax.axis_index("c")
        return [_remote(ins[i].at[2 * px + py], outs[i].at[k], ssem.at[3 * i + k], rsem.at[3 * i + k], (px, py, c))
                for i in range(self.n) for k, (px, py) in enumerate(_chip_peers(x, y))]

    def start(self, ins, outs, scratch):
        for cp in self._copies(ins, outs, scratch):
            cp.start()

    def finish(self, ins, outs, scratch):
        for cp in self._copies(ins, outs, scratch):
            cp.wait()


def _scatter_partials(sc):
    n = sc.n

    def body(*refs):
        args = (refs[:n], refs[n:2 * n], refs[2 * n:])
        sc.start(*args)
        sc.finish(*args)

    return pl.pallas_call(
        body, name="scatter_partials", in_specs=[ANY] * n, out_specs=[ANY] * n, out_shape=sc.out_shape,
        scratch_shapes=sc.scratch,
    )(*sc.ins)


class _Reduce:
    def __init__(self, grads, names):
        self.names = names
        own, got = _swap_halves(grads)
        self.sums, sums_b = [], []
        for nm, o, t in zip(names, own, got):
            shp = o.shape
            s_, sb_ = _elementwise(lambda p, q: (p + q, p + q), [o.reshape(-1, shp[2]), t.reshape(-1, shp[2])],
                                   [F32, BF16], f"chip_sum_{nm}")
            self.sums.append(s_.reshape(shp))
            sums_b.append(sb_.reshape(shp))
        self.scatter = _Scatter(sums_b)

    def collect(self, parts):
        me = 2 * lax.axis_index("x") + lax.axis_index("y")
        out = []
        for nm, s, p in zip(self.names, self.sums, parts):
            mine = lax.dynamic_index_in_dim(s, me, axis=0, keepdims=False)
            out.append(_mesh_sum(mine, p, f"mesh_sum_{nm}"))
        return out


def _mesh_sum(mine, parts, name):
    R, W = mine.shape
    tr = _tile(R, max(16, min(512, (1 << 18) // W // 16 * 16)))

    def body(m_ref, p_ref, o_ref):
        o_ref[...] = m_ref[...] + p_ref[0].astype(F32) + p_ref[1].astype(F32) + p_ref[2].astype(F32)

    return pl.pallas_call(
        body, grid=(R // tr,), name=name,
        in_specs=[_row(tr, W), pl.BlockSpec((3, tr, W), lambda i: (0, i, 0))], out_specs=_row(tr, W),
        out_shape=_sds((R, W), F32), compiler_params=_params(("parallel",), 32),
    )(mine, parts)


def _share_halves(reduced):
    n = len(reduced)
    plan = _chunk_plan([r.shape for r in reduced], 4)
    max_rows = max(p[2] for p in plan)
    max_w = max(r.shape[1] for r in reduced)

    def body(*refs):
        ins, outs = refs[:n], refs[n:2 * n]
        ssem, rsem, lsem, osem, buf = refs[2 * n:]
        x, y, c = lax.axis_index("x"), lax.axis_index("y"), lax.axis_index("c")
        sib = (x, y, 1 - c)
        pending = {}
        for k, (i, r0, ch) in enumerate(plan):
            slot = k % 2
            if slot in pending:
                rc, lc = pending[slot]
                rc.wait_send()
                lc.wait()
            stage = buf.at[slot, pl.ds(0, ch), pl.ds(0, reduced[i].shape[1])]
            ld = pltpu.make_async_copy(ins[i].at[pl.ds(r0, ch)], stage, lsem.at[slot])
            ld.start()
            ld.wait()
            place = outs[i].at[c, pl.ds(r0, ch)]
            rc = _remote(stage, place, ssem.at[slot], rsem.at[i], sib)
            lc = pltpu.make_async_copy(stage, place, osem.at[slot])
            rc.start()
            lc.start()
            pending[slot] = (rc, lc)
        for rc, lc in pending.values():
            rc.wait_send()
            lc.wait()
        for i in range(n):
            theirs = outs[i].at[1 - c]
            _remote(theirs, theirs, ssem.at[0], rsem.at[i], sib).wait_recv()

    outs = pl.pallas_call(
        body, name="share_halves", in_specs=[ANY] * n, out_specs=[ANY] * n,
        out_shape=[_sds((2,) + r.shape, F32) for r in reduced],
        scratch_shapes=[pltpu.SemaphoreType.DMA((2,)), pltpu.SemaphoreType.DMA((n,)), pltpu.SemaphoreType.DMA((2,)),
                        pltpu.SemaphoreType.DMA((2,)), pltpu.VMEM((2, max_rows, max_w), F32)],
        compiler_params=pltpu.CompilerParams(vmem_limit_bytes=32 << 20),
    )(*reduced)
    return [o.reshape(2 * r.shape[0], r.shape[1]) for o, r in zip(outs, reduced)]


def _gather_small(pack):
    P = pack.shape[0]

    def body(in_ref, out_ref, ssem, rsem, lsem):
        x, y, c = lax.axis_index("x"), lax.axis_index("y"), lax.axis_index("c")
        me = 4 * x + 2 * y + c
        lc = pltpu.make_async_copy(in_ref, out_ref.at[me], lsem)
        lc.start()
        cps = []
        for rel in range(1, 8):
            px = 1 - x if rel & 4 else x
            py = 1 - y if rel & 2 else y
            pc = 1 - c if rel & 1 else c
            rc = _remote(in_ref, out_ref.at[me], ssem.at[rel - 1], rsem.at[rel - 1], (px, py, pc))
            rc.start()
            cps.append((rc, 4 * px + 2 * py + pc))
        for rel, (rc, peer) in enumerate(cps):
            rc.wait_send()
            _remote(in_ref, out_ref.at[peer], ssem.at[rel], rsem.at[rel], (x, y, c)).wait_recv()
        lc.wait()

    return pl.pallas_call(
        body, name="gather_small", in_specs=[ANY], out_specs=ANY, out_shape=_sds((8, P, 128), F32),
        scratch_shapes=[pltpu.SemaphoreType.DMA((7,)), pltpu.SemaphoreType.DMA((7,)), pltpu.SemaphoreType.DMA],
    )(pack)


def _tile(rows, cap=256):
    t = min(rows, cap) // 16 * 16
    while rows % t:
        t -= 16
    return t


def _elementwise(fn, ins, out_dtypes, name):
    R, W = ins[0].shape
    tr = _tile(R, max(8, min(512, (1 << 18) // W // 8 * 8)))

    def body(*refs):
        outs = fn(*[r[...] for r in refs[:len(ins)]])
        for o_ref, o in zip(refs[len(ins):], outs):
            o_ref[...] = o.astype(o_ref.dtype)

    return pl.pallas_call(
        body, grid=(R // tr,), name=name, in_specs=[_row(tr, W)] * len(ins), out_specs=[_row(tr, W)] * len(out_dtypes),
        out_shape=[_sds((R, W), dt) for dt in out_dtypes],
        compiler_params=_params(("parallel",), 48),
    )(*ins)


def _adamw(w, g, m, v):
    m = B1 * m + (1.0 - B1) * g
    v = B2 * v + (1.0 - B2) * (g * g)
    m_hat = m / (1.0 - B1 ** STEP)
    v_hat = v / (1.0 - B2 ** STEP)
    return -LR * (m_hat / (jnp.sqrt(v_hat) + AEPS) + WD * w), m, v


def _adam_small(w, m, v, packs):
    def body(w_ref, m_ref, v_ref, p_ref, g_ref, d_ref, nm_ref, nv_ref):
        g = p_ref[0]
        for k in range(1, 8):
            g = g + p_ref[k]
        g_ref[...] = g
        d_ref[...], nm_ref[...], nv_ref[...] = _adamw(w_ref[...], g, m_ref[...], v_ref[...])

    return pl.pallas_call(
        body, name="adam_small", in_specs=[RES] * 4, out_specs=[RES] * 4, out_shape=[_sds(w.shape, F32)] * 4,
        compiler_params=pltpu.CompilerParams(vmem_limit_bytes=32 << 20),
    )(w, m, v, packs)


def _local_step(xs, tgt, mems, weights, small, gather_mid=None, gather_ffn=None, reduce=False):
    wt_in, wt_ba, wt_bs, wo, wq, wkv, wt_o, wt_gu, wd = weights
    g_mix, b_gate, w_sgu, b_sgu, g_sgu, g_cross, g_mem, g_ffn, g_final = small
    wt = jnp.tril(w_sgu).astype(BF16)
    bst = b_sgu.T

    (a, qkv0, qkv1, qkv2, uv, gl), got = _fwd_in(xs, g_mix, wt_in, gather_mid)
    if gather_mid is not None:
        wt_ba, wt_bs, wo, wq, wkv, wt_o = got
    qkvs = (qkv0, qkv1, qkv2)
    os_, ls_ = zip(*[_attn_fwd(qkvs[g], g) for g in range(3)])
    (ya, ys, ba, bs, mg, h1), got = _fwd_mid(xs, os_, ls_, uv, gl, wt, bst, g_sgu, b_gate, wt_ba, wt_bs, wo, gather_ffn)
    if gather_ffn is not None:
        wt_gu, wd = got
    mb, kv = _mem_fwd(mems, g_mem, wkv)
    cb, qc, oc, h2 = _fwd_cross(h1, g_cross, wq, kv, wt_o)
    f, act, dgu, dh3b, dh2, dh2b, dg_ffn, dg_final, loss = _ffn_fwd_bwd(h2, tgt, g_ffn, g_final, wt_gu, wd)

    g_ffn_w = [_tn_matmul(dgu, f, "dw_gate_up", 512), _tn_matmul(act, dh3b, "dw_down", 256)]
    r_ffn = _Reduce(g_ffn_w, ["w_gate_up", "w_down"]) if reduce else None
    (dqc, dh1, dh1b, dkv, dg_cross), parts_ffn = _bwd_cross(dh2, h1, qc, g_cross, wq, kv, wt_o,
                                                           r_ffn.scatter if reduce else None)
    dw_kv, dg_mem = _mem_bwd(dkv, mems, mb, g_mem, wkv)
    (dba, dbs, dgl, duv, do0, do1, do2, c0, c1, c2, db_gate, dg_sgu, dws, dbs_acc) = _bwd_mid(
        dh1, gl, ba, bs, uv, ls_, ya, wt, bst, g_sgu, b_gate, wt_ba, wt_bs, wo)
    dqkvs = [_attn_bwd(qkvs[g], do, ls_[g], corr, g) for g, (do, corr) in enumerate(((do0, c0), (do1, c1), (do2, c2)))]
    grad_x, dproj, dg_mix = _bwd_in(dqkvs, duv, dgl, dh1, xs, g_mix, wt_in)
    g_mid_w = [_tn_matmul(dba, ya, "dw_branch_attn", 512),
               _tn_matmul(dbs, ys, "dw_branch_sgu", 512),
               _tn_matmul(mg, dh1b, "dw_out", 512),
               _tn_matmul(cb, dqc, "dw_q_cross", 512),
               dw_kv,
               _tn_matmul(dh2b, oc, "dw_o_cross", 512)]
    small_terms = (dg_mix, db_gate, dws, dbs_acc, dg_sgu, dg_cross, dg_mem, dg_ffn, dg_final)
    if not reduce:
        return loss, grad_x, [_tn_matmul(dproj, a, "dw_in", 768)] + g_mid_w + g_ffn_w, small_terms
    r_mid = _Reduce(g_mid_w, ["w_branch_attn", "w_branch_sgu", "w_out", "w_q_cross", "w_kv_cross", "w_o_cross"])
    g_in, parts_mid = _tn_matmul(dproj, a, "dw_in", 768, comm=r_mid.scatter)
    r_in = _Reduce([g_in], ["w_in"])
    halves = r_in.collect(_scatter_partials(r_in.scatter)) + r_mid.collect(parts_mid) + r_ffn.collect(parts_ffn)
    return loss, grad_x, halves, small_terms


def kernel(x, mem, g_mix, w_in, b_gate, w_sgu_spatial, b_sgu_spatial, g_sgu, w_branch_attn, w_branch_sgu, w_out, g_cross, g_mem, w_q_cross, w_kv_cross, w_o_cross, g_ffn, w_gate_up, w_down, g_final, loss_target, m_g_mix, m_w_in, m_b_gate, m_w_sgu_spatial, m_b_sgu_spatial, m_g_sgu, m_w_branch_attn, m_w_branch_sgu, m_w_out, m_g_cross, m_g_mem, m_w_q_cross, m_w_kv_cross, m_w_o_cross, m_g_ffn, m_w_gate_up, m_w_down, m_g_final, v_g_mix, v_w_in, v_b_gate, v_w_sgu_spatial, v_b_sgu_spatial, v_g_sgu, v_w_branch_attn, v_w_branch_sgu, v_w_out, v_g_cross, v_g_mem, v_w_q_cross, v_w_kv_cross, v_w_o_cross, v_g_ffn, v_w_gate_up, v_w_down, v_g_final):
    S = x.shape[1]
    xs, tgt, mems = x.reshape(S, D), loss_target.reshape(S, D), mem.reshape(mem.shape[1], D)
    g_final2 = g_final.reshape(1, D)

    big = [("w_in", w_in[0], m_w_in[0], v_w_in[0], True),
           ("w_branch_attn", w_branch_attn[0], m_w_branch_attn[0], v_w_branch_attn[0], True),
           ("w_branch_sgu", w_branch_sgu[0], m_w_branch_sgu[0], v_w_branch_sgu[0], True),
           ("w_out", w_out[0], m_w_out[0], v_w_out[0], False),
           ("w_q_cross", w_q_cross[0], m_w_q_cross[0], v_w_q_cross[0], False),
           ("w_kv_cross", w_kv_cross[0], m_w_kv_cross[0], v_w_kv_cross[0], False),
           ("w_o_cross", w_o_cross[0], m_w_o_cross[0], v_w_o_cross[0], True),
           ("w_gate_up", w_gate_up[0], m_w_gate_up[0], v_w_gate_up[0], True),
           ("w_down", w_down[0], m_w_down[0], v_w_down[0], False)]
    shards = [(w.T if tr else w).astype(BF16) for _, w, _, _, tr in big]
    (wt_in,) = _gather_weights(shards[:1])
    (loss, grad_x, reduced, (dg_mix, db_gate, dws, dbs_acc, dg_sgu, dg_cross, dg_mem, dg_ffn, dg_final)) = _local_step(
        xs, tgt, mems, (wt_in,) + (None,) * 8,
        (g_mix, b_gate, w_sgu_spatial[0], b_sgu_spatial[0], g_sgu, g_cross, g_mem, g_ffn, g_final2),
        _Gather(shards[1:7]), _Gather(shards[7:9]), reduce=True)
    full = _share_halves(reduced)

    big_out = {}
    for (name, w, m, v, tr), gsh in zip(big, full):
        gsh = gsh.T if tr else gsh
        delta, nm, nv = _elementwise(_adamw, [w, gsh, m, v], [F32, F32, F32], f"adam_{name}")
        big_out[name] = tuple(t[None] for t in (gsh, delta, nm, nv))

    small = [("g_mix", g_mix, m_g_mix, v_g_mix, dg_mix), ("b_gate", b_gate, m_b_gate, v_b_gate, db_gate),
             ("w_sgu_spatial", w_sgu_spatial, m_w_sgu_spatial, v_w_sgu_spatial, jnp.tril(dws)),
             ("b_sgu_spatial", b_sgu_spatial, m_b_sgu_spatial, v_b_sgu_spatial, jnp.sum(dbs_acc, axis=-1)),
             ("g_sgu", g_sgu, m_g_sgu, v_g_sgu, dg_sgu), ("g_cross", g_cross, m_g_cross, v_g_cross, dg_cross),
             ("g_mem", g_mem, m_g_mem, v_g_mem, dg_mem), ("g_ffn", g_ffn, m_g_ffn, v_g_ffn, dg_ffn),
             ("g_final", g_final, m_g_final, v_g_final, dg_final)]

    def pack(parts_, tail):
        return jnp.concatenate([p.reshape(-1) for p in parts_] + [tail]).reshape(-1, 128)

    zeros = jnp.zeros((1024,), F32)
    gp = pack([s[4] for s in small], jnp.pad(loss.reshape(-1)[:1], (0, 1023)))
    wp, mp, vp = (pack([s[k] for s in small], zeros) for k in (1, 2, 3))
    gsum, dsm, nms, nvs = _adam_small(wp, mp, vp, _gather_small(gp))
    small_out, off = {}, 0
    for name, w, _, _, _ in small:
        n = w.size
        small_out[name] = tuple(t.reshape(-1)[off:off + n].reshape(w.shape) for t in (gsum, dsm, nms, nvs))
        off += n
    total_loss = gsum.reshape(-1)[off]

    order = ["g_mix", "w_in", "b_gate", "w_sgu_spatial", "b_sgu_spatial", "g_sgu", "w_branch_attn", "w_branch_sgu",
             "w_out", "g_cross", "g_mem", "w_q_cross", "w_kv_cross", "w_o_cross", "g_ffn", "w_gate_up", "w_down",
             "g_final"]
    res = {**big_out, **small_out}
    outs = [total_loss, grad_x.reshape(x.shape)]
    for k in range(4):
        outs += [res[nm][k] for nm in order]
    return tuple(outs)
```

```python
import math

import numpy as np
import jax
import jax.numpy as jnp
from jax import lax
from jax.experimental import pallas as pl
from jax.experimental.pallas import tpu as pltpu

F32, BF16 = jnp.float32, jnp.bfloat16
MESH = pl.DeviceIdType.MESH
ANY = pl.BlockSpec(memory_space=pl.ANY)
RES = pl.BlockSpec(memory_space=pltpu.VMEM)


def _whole(arr):
    nd = arr.ndim
    return pl.BlockSpec(arr.shape, lambda *_: (0,) * nd)

D = 1024
HEAD = 64
GROUP_W = 256
DIL_GROUPS = ((128, 1), (512, 4), (2048, 16))
BLK = 128
SGU_W = 512
MEM_HEADS, MEM_HD, MEM_W = 4, 128, 512
D_FF = 2816
FF_CHUNK = 256
EPS = 1e-6
NEG = -1e30
LR, B1, B2, AEPS, WD, STEP = 0.001, 0.9, 0.999, 1e-08, 0.01, 10
GELU_K, GELU_C = 0.7978845608028654, 0.044715


def _dot(a, b):
    return jnp.dot(a, b, preferred_element_type=F32)


def _dot_nt(a, b):
    return lax.dot_general(a, b, (((1,), (1,)), ((), ())), preferred_element_type=F32)


def _dot_tn(a, b):
    return lax.dot_general(a, b, (((0,), (0,)), ((), ())), preferred_element_type=F32)


def _row(tm, w):
    return pl.BlockSpec((tm, w), lambda i: (i, 0))


def _acc(shape):
    return pl.BlockSpec(shape, lambda i: (0,) * len(shape))


def _params(sem, mb):
    return pltpu.CompilerParams(dimension_semantics=sem, vmem_limit_bytes=mb << 20)


def _sds(shape, dt):
    return jax.ShapeDtypeStruct(shape, dt)


def _rms(h):
    return lax.rsqrt(jnp.mean(h * h, axis=-1, keepdims=True) + EPS)


def _rms_bwd(dy, h, r, g):
    t = dy * g
    dh = r * t - h * (r * r * r) * jnp.mean(t * h, axis=-1, keepdims=True)
    return dh, dy * h * r


def _gelu(x):
    t = jnp.tanh(GELU_K * (x + GELU_C * x * x * x))
    return 0.5 * x * (1.0 + t), t


def _gelu_grad(x, t):
    return 0.5 * (1.0 + t) + 0.5 * x * (1.0 - t * t) * GELU_K * (1.0 + 3.0 * GELU_C * x * x)


def _alibi_slopes():
    def pow2(n):
        start = 2.0 ** (-8.0 / n)
        return [start ** (i + 1) for i in range(n)]
    n = 12
    c = 2 ** int(math.floor(math.log2(n)))
    s = pow2(c) + pow2(2 * c)[0::2][: n - c]
    return np.array(sorted(s, reverse=True), dtype=np.float32).reshape(3, 4)


def _attn_bias(g):
    win, dil = DIL_GROUPS[g]
    steps = (np.arange(BLK)[:, None] + BLK) - np.arange(2 * BLK)[None, :]
    valid = (steps >= 0) & (steps <= win // dil)
    dist = (np.clip(steps, 0, None) * dil).astype(np.float32)
    b = -_alibi_slopes()[g][:, None, None] * dist[None]
    return np.where(valid[None], b, NEG).astype(np.float32)


def _head_masks():
    lane = lax.broadcasted_iota(jnp.int32, (1, GROUP_W), 1)
    return lane, [(lane >= HEAD * h) & (lane < HEAD * (h + 1)) for h in range(4)]


ATT_NB = 4


def _stack_heads(t, masks):
    z = jnp.zeros_like(t)
    return jnp.concatenate([jnp.where(m, t, z) for m in masks], axis=0)


def _unstack_heads(t, masks):
    out = jnp.zeros((BLK, GROUP_W), t.dtype)
    for h, m in enumerate(masks):
        out = jnp.where(m, t[h * BLK:(h + 1) * BLK], out)
    return out


def _stack_cols(ref, rows):
    return jnp.concatenate([ref[rows, HEAD * h:HEAD * h + 1] for h in range(4)], axis=0)


def _dil_spec(d, tm, w):
    return pl.BlockSpec((d, tm // d, w), lambda i: (0, i, 0))


def _to_dilated(val, s_ref, d, write):
    tm, w = val.shape
    for j in range(w // 128):
        s_ref[j, pl.ds(0, tm), :] = val[:, j * 128:(j + 1) * 128]
    for r in range(d):
        for j in range(w // 128):
            write(r, j, s_ref[j, pl.ds(r, tm // d, stride=d), :])


def _from_dilated(ref, s_ref, d, tm, w):
    if d == 1:
        return ref[0].astype(F32)
    for r in range(d):
        for j in range(w // 128):
            s_ref[j, pl.ds(r, tm // d, stride=d), :] = ref[r, :, j * 128:(j + 1) * 128].astype(F32)
    return jnp.concatenate([s_ref[j, pl.ds(0, tm), :] for j in range(w // 128)], axis=1)


def _fwd_in(x, g_mix, wt_in, gather=None, tm=512):
    S = x.shape[0]
    dils = [d for _, d in DIL_GROUPS]
    n = 0 if gather is None else gather.n
    last = S // tm - 1

    def body(*refs):
        x_ref, g_ref, w_ref = refs[:3]
        a_ref, q0_ref, q1_ref, q2_ref, uv_ref, gl_ref = refs[3 + n:9 + n]
        s_ref = refs[9 + 2 * n]
        comm = (refs[3:3 + n], refs[9 + n:9 + 2 * n], refs[10 + 2 * n:])
        if gather is not None:
            pl.when(pl.program_id(0) == 0)(lambda: gather.start(*comm))
        xv = x_ref[...]
        a = (xv * _rms(xv) * g_ref[...]).astype(BF16)
        a_ref[...] = a
        for g, (d, out) in enumerate(zip(dils, (q0_ref, q1_ref, q2_ref))):
            for part in range(3):
                rows = part * 768 + g * 256
                val = _dot_nt(a, w_ref[rows:rows + 256, :])
                if d == 1:
                    out[0, :, part * 256:(part + 1) * 256] = val.astype(BF16)
                else:
                    def write(r, j, piece, out=out, part=part):
                        out[r, :, part * 256 + j * 128:part * 256 + (j + 1) * 128] = piece.astype(BF16)
                    _to_dilated(val, s_ref, d, write)
        uv_ref[...] = _dot_nt(a, w_ref[2304:3328, :]).astype(BF16)
        gl_ref[...] = _dot_nt(a, w_ref[3328:5376, :]).astype(BF16)
        if gather is not None:
            pl.when(pl.program_id(0) == last)(lambda: gather.finish(*comm))

    outs = pl.pallas_call(
        body, grid=(S // tm,), name="fwd_in",
        in_specs=[_row(tm, D), _whole(g_mix), RES] + [ANY] * n,
        out_specs=[_row(tm, D)] + [_dil_spec(d, tm, 768) for d in dils] + [_row(tm, 1024), _row(tm, 2048)] + [ANY] * n,
        out_shape=[_sds((S, D), BF16)] + [_sds((d, S // d, 768), BF16) for d in dils]
        + [_sds((S, 1024), BF16), _sds((S, 2048), BF16)] + ([] if gather is None else gather.out_shape),
        scratch_shapes=[pltpu.VMEM((2, tm, 128), F32)] + ([] if gather is None else gather.scratch),
        compiler_params=_params(("arbitrary",), 60),
    )(x, g_mix, wt_in, *([] if gather is None else gather.halves))
    return outs[:6], ([] if gather is None else gather.full(outs[6:]))


def _attn_fwd(qkv, g):
    d, L, _ = qkv.shape
    nb = L // BLK
    bias = jnp.asarray(_attn_bias(g).reshape(4 * BLK, 2 * BLK))
    NB = min(ATT_NB, nb)
    W = NB * BLK

    def body(q_ref, kc_ref, kp_ref, vc_ref, vp_ref, b_ref, o_ref, l_ref):
        st = pl.program_id(1)
        k_all = jnp.concatenate([kp_ref[...], kc_ref[...]], axis=0)
        v_all = jnp.concatenate([vp_ref[...], vc_ref[...]], axis=0)
        lane, masks = _head_masks()
        for b in range(NB):
            rows = slice(b * BLK, (b + 1) * BLK)
            kk, vv = k_all[b * BLK:(b + 2) * BLK], v_all[b * BLK:(b + 2) * BLK]
            s = _dot_nt(_stack_heads(q_ref[rows, :], masks), kk) * 0.125 + b_ref[...]
            if b == 0:
                s = s + jnp.where((st == 0) & (lane < BLK), NEG, 0.0).astype(F32)
            mx = jnp.max(s, axis=-1, keepdims=True)
            e = jnp.exp(s - mx)
            den = jnp.sum(e, axis=-1, keepdims=True)
            o_ref[rows, :] = _unstack_heads(_dot(e.astype(BF16), vv) / den, masks)
            l_ref[rows, :] = _unstack_heads(mx + jnp.log(den), masks)

    def wide(col):
        return pl.BlockSpec((None, W, GROUP_W), lambda r, s: (r, s, col))

    def before(col):
        return pl.BlockSpec((None, BLK, GROUP_W), lambda r, s: (r, jnp.maximum(s * NB - 1, 0), col))

    return pl.pallas_call(
        body, grid=(d, nb // NB), name=f"attn_fwd_g{g}",
        in_specs=[wide(0), wide(1), before(1), wide(2), before(2),
                  pl.BlockSpec((4 * BLK, 2 * BLK), lambda r, s: (0, 0))],
        out_specs=[wide(0), wide(0)],
        out_shape=[_sds((d, L, GROUP_W), F32), _sds((d, L, GROUP_W), F32)],
        compiler_params=_params(("parallel", "parallel"), 32),
    )(qkv, qkv, qkv, qkv, qkv, bias)


def _group_weights(l0, l1, l2):
    m = jnp.maximum(jnp.maximum(l0, l1), l2)
    e0, e1, e2 = jnp.exp(l0 - m), jnp.exp(l1 - m), jnp.exp(l2 - m)
    inv = 1.0 / (e0 + e1 + e2)
    return e0 * inv, e1 * inv, e2 * inv


def _sgu_forward(uvf, gs, wt_ref, bst_ref, mixed_s, tm):
    z, t = _gelu(uvf)
    u, v = z[:, :SGU_W], z[:, SGU_W:]
    rv = _rms(v)
    vnb = (v * rv * gs).astype(BF16)
    for ci in range(tm // 128):
        for g in range(4):
            rs, cs = slice(ci * 128, (ci + 1) * 128), slice(g * 128, (g + 1) * 128)
            mixed_s[rs, cs] = _dot(wt_ref[g], vnb[rs, cs]) + bst_ref[:, g:g + 1]
    return u, v, rv, vnb, t


def _fwd_mid(x, os_, ls_, uv, gl, wt, bst, g_sgu, b_gate, wt_ba, wt_bs, w_out, gather=None, tm=512):
    S = x.shape[0]
    dils = [d for _, d in DIL_GROUPS]
    n = 0 if gather is None else gather.n
    last = S // tm - 1

    def body(*refs):
        (x_ref, o0, o1, o2, l0, l1, l2, uv_ref, gl_ref, wt_ref, bst_ref, gs_ref, bg_ref, wba_ref, wbs_ref,
         wo_ref) = refs[:16]
        ya_ref, ys_ref, ba_ref, bs_ref, mg_ref, h1_ref = refs[16 + n:22 + n]
        mixed_s, il_s = refs[22 + 2 * n:24 + 2 * n]
        comm = (refs[16:16 + n], refs[22 + n:22 + 2 * n], refs[24 + 2 * n:])
        if gather is not None:
            pl.when(pl.program_id(0) == 0)(lambda: gather.start(*comm))
        ls = [_from_dilated(r, il_s, d, tm, GROUP_W) for r, d in zip((l0, l1, l2), dils)]
        alphas = _group_weights(*ls)
        ya = jnp.zeros((tm, GROUP_W), F32)
        for a, r, d in zip(alphas, (o0, o1, o2), dils):
            ya = ya + a * _from_dilated(r, il_s, d, tm, GROUP_W)
        yab = ya.astype(BF16)
        ya_ref[...] = yab
        u, _, _, _, _ = _sgu_forward(uv_ref[...].astype(F32), gs_ref[...], wt_ref, bst_ref, mixed_s, tm)
        ysb = (u * mixed_s[...]).astype(BF16)
        ys_ref[...] = ysb
        gates = jax.nn.sigmoid(gl_ref[...].astype(F32) + bg_ref[...])
        ba = _dot_nt(yab, wba_ref[...])
        bs = _dot_nt(ysb, wbs_ref[...])
        ba_ref[...] = ba.astype(BF16)
        bs_ref[...] = bs.astype(BF16)
        mgb = (gates[:, :D] * ba + gates[:, D:] * bs).astype(BF16)
        mg_ref[...] = mgb
        h1_ref[...] = x_ref[...] + _dot(mgb, wo_ref[...])
        if gather is not None:
            pl.when(pl.program_id(0) == last)(lambda: gather.finish(*comm))

    gw = _row(tm, GROUP_W)
    dil = [_dil_spec(d, tm, GROUP_W) for d in dils]
    outs = pl.pallas_call(
        body, grid=(S // tm,), name="fwd_mid",
        in_specs=[_row(tm, D)] + dil + dil + [_row(tm, 1024), _row(tm, 2048)]
        + [_whole(t) for t in (wt, bst, g_sgu, b_gate)] + [RES] * 3 + [ANY] * n,
        out_specs=[gw, _row(tm, SGU_W), _row(tm, D), _row(tm, D), _row(tm, D), _row(tm, D)] + [ANY] * n,
        out_shape=[_sds((S, GROUP_W), BF16), _sds((S, SGU_W), BF16), _sds((S, D), BF16), _sds((S, D), BF16),
                   _sds((S, D), BF16), _sds((S, D), F32)] + ([] if gather is None else gather.out_shape),
        scratch_shapes=[pltpu.VMEM((tm, SGU_W), F32), pltpu.VMEM((2, tm, 128), F32)]
        + ([] if gather is None else gather.scratch),
        compiler_params=_params(("arbitrary",), 56),
    )(x, *os_, *ls_, uv, gl, wt, bst, g_sgu, b_gate, wt_ba, wt_bs, w_out, *([] if gather is None else gather.halves))
    return outs[:6], ([] if gather is None else gather.full(outs[6:]))


def _mem_fwd(mem, g_mem, w_kv):
    def body(m_ref, g_ref, w_ref, mb_ref, kv_ref):
        mv = m_ref[...]
        mb = (mv * _rms(mv) * g_ref[...]).astype(BF16)
        mb_ref[...] = mb
        kv_ref[...] = _dot(mb, w_ref[...]).astype(BF16)

    return pl.pallas_call(
        body, name="mem_fwd",
        out_shape=[_sds(mem.shape, BF16), _sds((mem.shape[0], 2 * MEM_W), BF16)],
        compiler_params=pltpu.CompilerParams(vmem_limit_bytes=32 << 20),
    )(mem, g_mem, w_kv)


def _cross_probs(qh, kh):
    s = _dot_nt(qh, kh) * (MEM_HD ** -0.5)
    e = jnp.exp(s - jnp.max(s, axis=-1, keepdims=True))
    return e / jnp.sum(e, axis=-1, keepdims=True)


def _fwd_cross(h1, g_cross, w_q, kv, wt_o, tm=512):
    S = h1.shape[0]

    def body(h_ref, g_ref, wq_ref, kv_ref, wo_ref, c_ref, qc_ref, oc_ref, h2_ref):
        hv = h_ref[...]
        cb = (hv * _rms(hv) * g_ref[...]).astype(BF16)
        c_ref[...] = cb
        qcb = _dot(cb, wq_ref[...]).astype(BF16)
        qc_ref[...] = qcb
        for h in range(MEM_HEADS):
            cs = slice(h * MEM_HD, (h + 1) * MEM_HD)
            p = _cross_probs(qcb[:, cs], kv_ref[:, cs])
            oc_ref[:, cs] = _dot(p.astype(BF16), kv_ref[:, MEM_W + h * MEM_HD:MEM_W + (h + 1) * MEM_HD]).astype(BF16)
        h2_ref[...] = hv + _dot_nt(oc_ref[...], wo_ref[...])

    return pl.pallas_call(
        body, grid=(S // tm,), name="fwd_cross",
        in_specs=[_row(tm, D), _whole(g_cross), RES, _whole(kv), RES],
        out_specs=[_row(tm, D), _row(tm, MEM_W), _row(tm, MEM_W), _row(tm, D)],
        out_shape=[_sds((S, D), BF16), _sds((S, MEM_W), BF16), _sds((S, MEM_W), BF16), _sds((S, D), F32)],
        compiler_params=_params(("parallel",), 40),
    )(h1, g_cross, w_q, kv, wt_o)


def _ffn_fwd_bwd(h2, target, g_ffn, g_final, wt_gu, w_down, tm=256):
    S = h2.shape[0]
    nch = D_FF // FF_CHUNK

    def body(h_ref, t_ref, gf_ref, gz_ref, wgu_ref, wd_ref,
             f_ref, act_ref, dgu_ref, dh3b_ref, dh2_ref, dh2b_ref, dgf_ref, dgz_ref, loss_ref, gu_s):
        i = pl.program_id(0)

        @pl.when(i == 0)
        def _():
            dgf_ref[...] = jnp.zeros_like(dgf_ref)
            dgz_ref[...] = jnp.zeros_like(dgz_ref)
            loss_ref[...] = jnp.zeros_like(loss_ref)

        def weights(c):
            return (wgu_ref.at[pl.ds(c * FF_CHUNK, FF_CHUNK)], wgu_ref.at[pl.ds(D_FF + c * FF_CHUNK, FF_CHUNK)],
                    wd_ref.at[pl.ds(c * FF_CHUNK, FF_CHUNK)])

        hv = h_ref[...]
        r2 = _rms(hv)
        gf = gf_ref[...]
        fb = (hv * r2 * gf).astype(BF16)
        f_ref[...] = fb
        h3 = hv
        for c in range(nch):
            cs = slice(c * FF_CHUNK, (c + 1) * FF_CHUNK)
            us = slice(D_FF + c * FF_CHUNK, D_FF + (c + 1) * FF_CHUNK)
            wg, wu, wd = weights(c)
            gt = _dot_nt(fb, wg[...])
            up = _dot_nt(fb, wu[...])
            gu_s[:, cs] = gt
            gu_s[:, us] = up
            actb = (gt * jax.nn.sigmoid(gt) * up).astype(BF16)
            act_ref[:, cs] = actb
            h3 = h3 + _dot(actb, wd[...])
        r3 = _rms(h3)
        gz = gz_ref[...]
        diff = h3 * r3 * gz - t_ref[...]
        dy = diff * (1.0 / D)
        dh3, dgz_rows = _rms_bwd(dy, h3, r3, gz)
        dh3b = dh3.astype(BF16)
        dh3b_ref[...] = dh3b
        df = jnp.zeros((tm, D), F32)
        for c in range(nch):
            cs = slice(c * FF_CHUNK, (c + 1) * FF_CHUNK)
            us = slice(D_FF + c * FF_CHUNK, D_FF + (c + 1) * FF_CHUNK)
            wg, wu, wd = weights(c)
            dact = _dot_nt(dh3b, wd[...])
            gt, up = gu_s[:, cs], gu_s[:, us]
            sg = jax.nn.sigmoid(gt)
            dgt = (dact * up * (sg * (1.0 + gt * (1.0 - sg)))).astype(BF16)
            dup = (dact * (gt * sg)).astype(BF16)
            dgu_ref[:, cs] = dgt
            dgu_ref[:, us] = dup
            df = df + _dot(dgt, wg[...]) + _dot(dup, wu[...])
        dhn, dgf_rows = _rms_bwd(df, hv, r2, gf)
        dh2 = dh3 + dhn
        dh2_ref[...] = dh2
        dh2b_ref[...] = dh2.astype(BF16)
        dgf_ref[...] += jnp.sum(dgf_rows, axis=0, keepdims=True)
        dgz_ref[...] += jnp.sum(dgz_rows, axis=0, keepdims=True)
        loss_ref[...] += jnp.sum(jnp.sum(diff * diff, axis=0, keepdims=True), axis=1, keepdims=True) * (0.5 / D)

    return pl.pallas_call(
        body, grid=(S // tm,), name="ffn_fwd_bwd",
        in_specs=[_row(tm, D), _row(tm, D), _whole(g_ffn), _whole(g_final), RES, RES],
        out_specs=[_row(tm, D), _row(tm, D_FF), _row(tm, 2 * D_FF), _row(tm, D), _row(tm, D), _row(tm, D),
                   _acc((1, D)), _acc((1, D)), _acc((1, 128))],
        out_shape=[_sds((S, D), BF16), _sds((S, D_FF), BF16), _sds((S, 2 * D_FF), BF16), _sds((S, D), BF16),
                   _sds((S, D), F32), _sds((S, D), BF16), _sds((1, D), F32), _sds((1, D), F32), _sds((1, 128), F32)],
        scratch_shapes=[pltpu.VMEM((tm, 2 * D_FF), F32)],
        compiler_params=_params(("arbitrary",), 56),
    )(h2, target, g_ffn, g_final, wt_gu, w_down)


def _bwd_cross(dh2, h1, qc, g_cross, w_q, kv, wt_o, comm=None, tm=512):
    S = h1.shape[0]
    n = 0 if comm is None else comm.n
    last = S // tm - 1

    def body(*refs):
        d_ref, h_ref, qc_ref, g_ref, wq_ref, kv_ref, wo_ref = refs[:7]
        dqc_ref, dh1_ref, dh1b_ref, dkv_ref, dg_ref = refs[7 + n:12 + n]
        cargs = (refs[7:7 + n], refs[12 + n:12 + 2 * n], refs[12 + 2 * n:])
        i = pl.program_id(0)

        @pl.when(i == 0)
        def _():
            dkv_ref[...] = jnp.zeros_like(dkv_ref)
            dg_ref[...] = jnp.zeros_like(dg_ref)
            if comm is not None:
                comm.start(*cargs)

        dh2 = d_ref[...]
        doc = _dot(dh2.astype(BF16), wo_ref[...])
        qcb = qc_ref[...]
        for h in range(MEM_HEADS):
            cs = slice(h * MEM_HD, (h + 1) * MEM_HD)
            vs = slice(MEM_W + h * MEM_HD, MEM_W + (h + 1) * MEM_HD)
            qh, kh, vh = qcb[:, cs], kv_ref[:, cs], kv_ref[:, vs]
            p = _cross_probs(qh, kh)
            dohb = doc[:, cs].astype(BF16)
            dp = _dot_nt(dohb, vh)
            dsb = (p * (dp - jnp.sum(dp * p, axis=-1, keepdims=True)) * (MEM_HD ** -0.5)).astype(BF16)
            dqc_ref[:, cs] = _dot(dsb, kh).astype(BF16)
            dkv_ref[:, cs] += _dot_tn(dsb, qh)
            dkv_ref[:, vs] += _dot_tn(p.astype(BF16), dohb)
        dc = _dot_nt(dqc_ref[...], wq_ref[...])
        hv = h_ref[...]
        dhn, dg_rows = _rms_bwd(dc, hv, _rms(hv), g_ref[...])
        dh1 = dh2 + dhn
        dh1_ref[...] = dh1
        dh1b_ref[...] = dh1.astype(BF16)
        dg_ref[...] += jnp.sum(dg_rows, axis=0, keepdims=True)
        if comm is not None:
            pl.when(i == last)(lambda: comm.finish(*cargs))

    outs = pl.pallas_call(
        body, grid=(S // tm,), name="bwd_cross",
        in_specs=[_row(tm, D), _row(tm, D), _row(tm, MEM_W), _whole(g_cross), RES, _whole(kv), RES] + [ANY] * n,
        out_specs=[_row(tm, MEM_W), _row(tm, D), _row(tm, D), _acc((256, 2 * MEM_W)), _acc((1, D))] + [ANY] * n,
        out_shape=[_sds((S, MEM_W), BF16), _sds((S, D), F32), _sds((S, D), BF16), _sds((256, 2 * MEM_W), F32),
                   _sds((1, D), F32)] + ([] if comm is None else comm.out_shape),
        scratch_shapes=[] if comm is None else comm.scratch,
        compiler_params=_params(("arbitrary",), 48),
    )(dh2, h1, qc, g_cross, w_q, kv, wt_o, *([] if comm is None else comm.ins))
    return outs[:5], outs[5:]


def _mem_bwd(dkv, mem, mb, g_mem, w_kv):
    def body(dkv_ref, m_ref, mb_ref, g_ref, w_ref, dw_ref, dg_ref):
        dkvb = dkv_ref[...].astype(BF16)
        dw_ref[...] = _dot_tn(mb_ref[...], dkvb)
        dm = _dot_nt(dkvb, w_ref[...])
        mv = m_ref[...]
        dg_ref[...] = jnp.sum(dm * mv * _rms(mv), axis=0, keepdims=True)

    return pl.pallas_call(
        body, name="mem_bwd",
        out_shape=[_sds((D, 2 * MEM_W), F32), _sds((1, D), F32)],
        compiler_params=pltpu.CompilerParams(vmem_limit_bytes=32 << 20),
    )(dkv, mem, mb, g_mem, w_kv)


def _bwd_mid(dh1, gl, ba, bs, uv, ls_, ya, wt, bst, g_sgu, b_gate, wt_ba, wt_bs, w_out, tm=512):
    S = dh1.shape[0]
    dils = [d for _, d in DIL_GROUPS]

    def body(d_ref, gl_ref, ba_ref, bs_ref, uv_ref, l0, l1, l2, ya_ref,
             wt_ref, bst_ref, gs_ref, bg_ref, wba_ref, wbs_ref, wo_ref,
             dba_ref, dbs_ref, dgl_ref, duv_ref, do0, do1, do2, c0, c1, c2,
             dbg_ref, dgs_ref, dws_ref, dbsa_ref, mixed_s, dvn_s, il_s):
        i = pl.program_id(0)

        @pl.when(i == 0)
        def _():
            for r in (dbg_ref, dgs_ref, dws_ref, dbsa_ref):
                r[...] = jnp.zeros_like(r)

        dm = _dot_nt(d_ref[...].astype(BF16), wo_ref[...])
        gates = jax.nn.sigmoid(gl_ref[...].astype(F32) + bg_ref[...])
        g0, g1 = gates[:, :D], gates[:, D:]
        dbab = (dm * g0).astype(BF16)
        dbsb = (dm * g1).astype(BF16)
        dba_ref[...] = dbab
        dbs_ref[...] = dbsb
        dg0 = dm * ba_ref[...].astype(F32) * g0 * (1.0 - g0)
        dg1 = dm * bs_ref[...].astype(F32) * g1 * (1.0 - g1)
        dgl_ref[:, :D] = dg0.astype(BF16)
        dgl_ref[:, D:] = dg1.astype(BF16)
        dbg_ref[:, :D] += jnp.sum(dg0, axis=0, keepdims=True)
        dbg_ref[:, D:] += jnp.sum(dg1, axis=0, keepdims=True)
        dya = _dot(dbab, wba_ref[...])
        dys = _dot(dbsb, wbs_ref[...])

        uvf = uv_ref[...].astype(F32)
        gs = gs_ref[...]
        u, v, rv, vnb, t = _sgu_forward(uvf, gs, wt_ref, bst_ref, mixed_s, tm)
        du = dys * mixed_s[...]
        dmixed = dys * u
        for ci in range(tm // 128):
            for g in range(4):
                rs, cs = slice(ci * 128, (ci + 1) * 128), slice(g * 128, (g + 1) * 128)
                dmx = dmixed[rs, cs]
                dmxb = dmx.astype(BF16)
                dvn_s[rs, cs] = _dot_tn(wt_ref[g], dmxb)
                dws_ref[g] += _dot_nt(dmxb, vnb[rs, cs])
                dbsa_ref[g] += dmx
        dv, dgs_rows = _rms_bwd(dvn_s[...], v, rv, gs)
        dgs_ref[...] += jnp.sum(dgs_rows, axis=0, keepdims=True)
        gg = _gelu_grad(uvf, t)
        duv_ref[:, :SGU_W] = (du * gg[:, :SGU_W]).astype(BF16)
        duv_ref[:, SGU_W:] = (dv * gg[:, SGU_W:]).astype(BF16)

        alphas = _group_weights(*[_from_dilated(r, il_s, d, tm, GROUP_W) for r, d in zip((l0, l1, l2), dils)])
        prod = dya * ya_ref[...].astype(F32)
        _, masks = _head_masks()
        hs = jnp.zeros_like(prod)
        for h in range(4):
            sh = jnp.sum(jnp.where(masks[h], prod, 0.0), axis=-1, keepdims=True)
            hs = jnp.where(masks[h], sh, hs)
        for a, d, do_ref, c_ref in zip(alphas, dils, (do0, do1, do2), (c0, c1, c2)):
            for val, out in ((a * dya, do_ref), (a * hs, c_ref)):
                if d == 1:
                    out[0] = val.astype(out.dtype)
                else:
                    def write(r, j, piece, out=out):
                        out[r, :, j * 128:(j + 1) * 128] = piece.astype(out.dtype)
                    _to_dilated(val, il_s, d, write)

    gw = _row(tm, GROUP_W)
    dil = [_dil_spec(d, tm, GROUP_W) for d in dils]
    return pl.pallas_call(
        body, grid=(S // tm,), name="bwd_mid",
        in_specs=[_row(tm, D), _row(tm, 2048), _row(tm, D), _row(tm, D), _row(tm, 1024)] + dil + [gw]
        + [_whole(t) for t in (wt, bst, g_sgu, b_gate)] + [RES] * 3,
        out_specs=[_row(tm, D), _row(tm, D), _row(tm, 2048), _row(tm, 1024)] + dil + dil
        + [_acc((1, 2048)), _acc((1, SGU_W)), _acc((4, 128, 128)), _acc((4, 128, 128))],
        out_shape=[_sds((S, D), BF16), _sds((S, D), BF16), _sds((S, 2048), BF16), _sds((S, 1024), BF16)]
        + [_sds((d, S // d, GROUP_W), BF16) for d in dils] + [_sds((d, S // d, GROUP_W), F32) for d in dils]
        + [_sds((1, 2048), F32), _sds((1, SGU_W), F32), _sds((4, 128, 128), F32), _sds((4, 128, 128), F32)],
        scratch_shapes=[pltpu.VMEM((tm, SGU_W), F32), pltpu.VMEM((tm, SGU_W), F32), pltpu.VMEM((2, tm, 128), F32)],
        compiler_params=_params(("arbitrary",), 60),
    )(dh1, gl, ba, bs, uv, *ls_, ya, wt, bst, g_sgu, b_gate, wt_ba, wt_bs, w_out)


def _attn_bwd(qkv, do, lse, corr, g):
    d, L, _ = qkv.shape
    nb = L // BLK
    NB = min(ATT_NB, nb)
    W = NB * BLK
    nsteps = nb // NB
    bias = jnp.asarray(_attn_bias(g).reshape(4 * BLK, 2 * BLK))

    def body(q_ref, kc_ref, kp_ref, vc_ref, vp_ref, do_ref, l_ref, c_ref, qn_ref, don_ref, ln_ref, cn_ref, b_ref,
             out_ref, dk_s, dv_s):
        st = pl.program_id(1)
        k_all = jnp.concatenate([kp_ref[...], kc_ref[...]], axis=0)
        v_all = jnp.concatenate([vp_ref[...], vc_ref[...]], axis=0)
        lane, masks = _head_masks()
        dk_s[...] = jnp.zeros_like(dk_s)
        dv_s[...] = jnp.zeros_like(dv_s)

        def block_terms(qs, dos, kk, vv, bias_v, lse_c, corr_c):
            s = _dot_nt(qs, kk) * 0.125 + bias_v
            p = jnp.exp(s - lse_c)
            dsb = (p * (_dot_nt(dos, vv) - corr_c) * 0.125).astype(BF16)
            return dsb, p.astype(BF16)

        for b in range(NB):
            rows = slice(b * BLK, (b + 1) * BLK)
            keys = slice(b * BLK, (b + 2) * BLK)
            kk, vv = k_all[keys], v_all[keys]
            qs, dos = _stack_heads(q_ref[rows, :], masks), _stack_heads(do_ref[rows, :], masks)
            bias_v = b_ref[...]
            if b == 0:
                bias_v = bias_v + jnp.where((st == 0) & (lane < BLK), NEG, 0.0).astype(F32)
            dsb, pb = block_terms(qs, dos, kk, vv, bias_v, _stack_cols(l_ref, rows), _stack_cols(c_ref, rows))
            out_ref[rows, 0:GROUP_W] = _unstack_heads(_dot(dsb, kk), masks).astype(BF16)
            dk_s[keys, :] += _dot_tn(dsb, qs)
            dv_s[keys, :] += _dot_tn(pb, dos)

        @pl.when(st < nsteps - 1)
        def _():
            last = slice(NB * BLK, (NB + 1) * BLK)
            qs, dos = _stack_heads(qn_ref[...], masks), _stack_heads(don_ref[...], masks)
            every = slice(None)
            dsb, pb = block_terms(qs, dos, k_all[last], v_all[last], b_ref[:, :BLK],
                                  _stack_cols(ln_ref, every), _stack_cols(cn_ref, every))
            dk_s[last, :] += _dot_tn(dsb, qs)
            dv_s[last, :] += _dot_tn(pb, dos)

        out_ref[:, GROUP_W:2 * GROUP_W] = dk_s[BLK:, :].astype(BF16)
        out_ref[:, 2 * GROUP_W:] = dv_s[BLK:, :].astype(BF16)

    def wide(col, w=GROUP_W):
        return pl.BlockSpec((None, W, w), lambda r, s: (r, s, col))

    def before(col):
        return pl.BlockSpec((None, BLK, GROUP_W), lambda r, s: (r, jnp.maximum(s * NB - 1, 0), col))

    def after(col):
        return pl.BlockSpec((None, BLK, GROUP_W), lambda r, s: (r, jnp.minimum((s + 1) * NB, nb - 1), col))

    return pl.pallas_call(
        body, grid=(d, nsteps), name=f"attn_bwd_g{g}",
        in_specs=[wide(0), wide(1), before(1), wide(2), before(2), wide(0), wide(0), wide(0),
                  after(0), after(0), after(0), after(0), pl.BlockSpec((4 * BLK, 2 * BLK), lambda r, s: (0, 0))],
        out_specs=wide(0, 768),
        out_shape=_sds((d, L, 768), BF16),
        scratch_shapes=[pltpu.VMEM(((NB + 1) * BLK, GROUP_W), F32), pltpu.VMEM(((NB + 1) * BLK, GROUP_W), F32)],
        compiler_params=_params(("parallel", "parallel"), 32),
    )(qkv, qkv, qkv, qkv, qkv, do, lse, corr, qkv, do, lse, corr, bias)


def _bwd_in(dqkvs, duv, dgl, dh1, x, g_mix, wt_in, tm=512):
    S = x.shape[0]
    dils = [d for _, d in DIL_GROUPS]

    def body(q0_ref, q1_ref, q2_ref, duv_ref, dgl_ref, d_ref, x_ref, g_ref, w_ref, dx_ref, dp_ref, dg_ref, il_s):
        i = pl.program_id(0)

        @pl.when(i == 0)
        def _():
            dg_ref[...] = jnp.zeros_like(dg_ref)

        for g, (d, ref) in enumerate(zip(dils, (q0_ref, q1_ref, q2_ref))):
            nat = _from_dilated(ref, il_s, d, tm, 768).astype(BF16)
            for part in range(3):
                col = part * 768 + g * 256
                dp_ref[:, col:col + 256] = nat[:, part * 256:(part + 1) * 256]
        dp_ref[:, 2304:3328] = duv_ref[...]
        dp_ref[:, 3328:5376] = dgl_ref[...]
        da = _dot(dp_ref[...], w_ref[...])
        xv = x_ref[...]
        dxn, dg_rows = _rms_bwd(da, xv, _rms(xv), g_ref[...])
        dx_ref[...] = d_ref[...] + dxn
        dg_ref[...] += jnp.sum(dg_rows, axis=0, keepdims=True)

    return pl.pallas_call(
        body, grid=(S // tm,), name="bwd_in",
        in_specs=[_dil_spec(d, tm, 768) for d in dils] + [_row(tm, 1024), _row(tm, 2048), _row(tm, D), _row(tm, D),
                                                          _whole(g_mix), RES],
        out_specs=[_row(tm, D), _row(tm, 5376), _acc((1, D))],
        out_shape=[_sds((S, D), F32), _sds((S, 5376), BF16), _sds((1, D), F32)],
        scratch_shapes=[pltpu.VMEM((6, tm, 128), F32)],
        compiler_params=_params(("arbitrary",), 60),
    )(*dqkvs, duv, dgl, dh1, x, g_mix, wt_in)


def _tn_matmul(a, b, name, tk, ts=2048, comm=None):
    S, K = a.shape
    N = b.shape[1]
    n = 0 if comm is None else comm.n
    nk, ns = K // tk, S // ts

    def body(*refs):
        a_ref, b_ref, o_ref = refs[0], refs[1], refs[2 + n]
        cargs = (refs[2:2 + n], refs[3 + n:3 + 2 * n], refs[3 + 2 * n:])
        k, s = pl.program_id(0), pl.program_id(1)
        if comm is not None:
            pl.when((k == 0) & (s == 0))(lambda: comm.start(*cargs))

        @pl.when(s == 0)
        def _():
            o_ref[...] = jnp.zeros_like(o_ref)

        o_ref[...] += _dot_tn(a_ref[...], b_ref[...])
        if comm is not None:
            pl.when((k == nk - 1) & (s == ns - 1))(lambda: comm.finish(*cargs))

    outs = pl.pallas_call(
        body, grid=(nk, ns), name=name,
        in_specs=[pl.BlockSpec((ts, tk), lambda k, s: (s, k)), pl.BlockSpec((ts, N), lambda k, s: (s, 0))] + [ANY] * n,
        out_specs=[pl.BlockSpec((tk, N), lambda k, s: (k, 0))] + [ANY] * n,
        out_shape=[_sds((K, N), F32)] + ([] if comm is None else comm.out_shape),
        scratch_shapes=[] if comm is None else comm.scratch,
        compiler_params=_params(("arbitrary", "arbitrary"), 48),
    )(a, b, *([] if comm is None else comm.ins))
    return outs[0] if comm is None else (outs[0], outs[1:])


def _chip_peers(x, y):
    return [(1 - x, y), (x, 1 - y), (1 - x, 1 - y)]


STAGE_BYTES = 2 << 20


def _chunk_plan(shapes, itemsize):
    plan = []
    for i, (rows, w) in enumerate(shapes):
        ch = max(16, min(rows, (STAGE_BYTES // (w * itemsize)) // 16 * 16))
        while rows % ch:
            ch -= 16
        plan += [(i, r0, ch) for r0 in range(0, rows, ch)]
    return plan


def _remote(src, dst, ssem, rsem, dev):
    return pltpu.make_async_remote_copy(src_ref=src, dst_ref=dst, send_sem=ssem, recv_sem=rsem, device_id=dev,
                                        device_id_type=MESH)


class _Gather:
    def __init__(self, shards):
        self.n = len(shards)
        self.shards = shards
        self.halves = [s.reshape(2, s.shape[0] // 2, s.shape[1]) for s in shards]
        self.plan = _chunk_plan([h.shape[1:] for h in self.halves], 2)
        self.out_shape = [_sds((4,) + h.shape, BF16) for h in self.halves]
        n = self.n
        self.scratch = [pltpu.SemaphoreType.DMA((6 * n,)), pltpu.SemaphoreType.DMA((6 * n,)),
                        pltpu.SemaphoreType.DMA((2,)), pltpu.SemaphoreType.DMA((2,)),
                        pltpu.VMEM((2, max(p[2] for p in self.plan), max(h.shape[2] for h in self.halves)), BF16)]

    def full(self, outs):
        return [o.reshape(4 * s.shape[0], s.shape[1]) for o, s in zip(outs, self.shards)]

    def _sends(self, ins, outs, ssem, rsem):
        x, y, c = lax.axis_index("x"), lax.axis_index("y"), lax.axis_index("c")
        me = 2 * x + y
        return [_remote(ins[i].at[c], outs[i].at[me, c], ssem.at[6 * i + k], rsem.at[6 * i + k], (px, py, c))
                for i in range(self.n) for k, (px, py) in enumerate(_chip_peers(x, y))]

    def start(self, ins, outs, scratch):
        ssem, rsem, lsem, osem, buf = scratch
        me = 2 * lax.axis_index("x") + lax.axis_index("y")
        for cp in self._sends(ins, outs, ssem, rsem):
            cp.start()
        pending = {}
        for i, r0, ch in self.plan:
            for h in range(2):
                if h in pending:
                    pending[h].wait()
                stage = buf.at[h, pl.ds(0, ch), pl.ds(0, self.halves[i].shape[2])]
                ld = pltpu.make_async_copy(ins[i].at[h, pl.ds(r0, ch)], stage, lsem.at[h])
                ld.start()
                ld.wait()
                st = pltpu.make_async_copy(stage, outs[i].at[me, h, pl.ds(r0, ch)], osem.at[h])
                st.start()
                pending[h] = st
        for st in pending.values():
            st.wait()

    def finish(self, ins, outs, scratch):
        ssem, rsem = scratch[:2]
        x, y, c = lax.axis_index("x"), lax.axis_index("y"), lax.axis_index("c")
        chips = _chip_peers(x, y)
        sib = (x, y, 1 - c)
        forwards = []
        for i in range(self.n):
            for k, (px, py) in enumerate(chips):
                landed = outs[i].at[2 * px + py, c]
                _remote(landed, landed, ssem.at[6 * i + k], rsem.at[6 * i + k], (px, py, c)).wait_recv()
                cp = _remote(landed, landed, ssem.at[6 * i + 3 + k], rsem.at[6 * i + 3 + k], sib)
                cp.start()
                forwards.append(cp)
        for i in range(self.n):
            for k, (px, py) in enumerate(chips):
                passed = outs[i].at[2 * px + py, 1 - c]
                _remote(passed, passed, ssem.at[6 * i + 3 + k], rsem.at[6 * i + 3 + k], sib).wait_recv()
        for cp in self._sends(ins, outs, ssem, rsem) + forwards:
            cp.wait_send()


def _gather_weights(shards):
    gt = _Gather(shards)
    n = gt.n

    def body(*refs):
        ins, outs, scratch = refs[:n], refs[n:2 * n], refs[2 * n:]
        gt.start(ins, outs, scratch)
        gt.finish(ins, outs, scratch)

    outs = pl.pallas_call(
        body, name="gather_weights", in_specs=[ANY] * n, out_specs=[ANY] * n, out_shape=gt.out_shape,
        scratch_shapes=gt.scratch, compiler_params=pltpu.CompilerParams(vmem_limit_bytes=32 << 20),
    )(*gt.halves)
    return gt.full(outs)


def _swap_halves(grads):
    n = len(grads)
    g4 = [g.reshape(4, 2, g.shape[0] // 8, g.shape[1]) for g in grads]

    def body(*refs):
        ins, got = refs[:n], refs[n:2 * n]
        ssem, rsem = refs[2 * n:]
        x, y, c = lax.axis_index("x"), lax.axis_index("y"), lax.axis_index("c")
        sib = (x, y, 1 - c)
        cps = []
        for i in range(n):
            rc = _remote(ins[i].at[:, 1 - c], got[i], ssem.at[i], rsem.at[i], sib)
            rc.start()
            cps.append(rc)
        for cp in cps:
            cp.wait()

    half = [_sds((4, g.shape[2], g.shape[3]), F32) for g in g4]
    got = pl.pallas_call(
        body, name="swap_halves", in_specs=[ANY] * n, out_specs=[ANY] * n, out_shape=half,
        scratch_shapes=[pltpu.SemaphoreType.DMA((n,)), pltpu.SemaphoreType.DMA((n,))],
    )(*g4)
    return g4, got


def _chip_sum(g4, got, name):
    _, _, R, W = g4.shape
    tr = _tile(R, max(16, min(512, (1 << 18) // W // 16 * 16)))
    c = lax.axis_index("c").astype(jnp.int32).reshape(1)

    def body(c_ref, a_ref, b_ref, s_ref, sb_ref):
        s = a_ref[...] + b_ref[...]
        s_ref[...] = s
        sb_ref[...] = s.astype(BF16)

    plain = pl.BlockSpec((None, tr, W), lambda j, t, c_ref: (j, t, 0))
    return pl.pallas_call(
        body, name=name,
        grid_spec=pltpu.PrefetchScalarGridSpec(
            num_scalar_prefetch=1, grid=(4, R // tr),
            in_specs=[pl.BlockSpec((None, None, tr, W), lambda j, t, c_ref: (j, c_ref[0], t, 0)), plain],
            out_specs=[plain, plain]),
        out_shape=[_sds((4, R, W), F32), _sds((4, R, W), BF16)],
        compiler_params=_params(("parallel", "parallel"), 32),
    )(c, g4, got)


class _Scatter:
    def __init__(self, sums_b):
        self.n = len(sums_b)
        self.ins = list(sums_b)
        self.out_shape = [_sds((3,) + s.shape[1:], BF16) for s in sums_b]
        self.scratch = [pltpu.SemaphoreType.DMA((3 * self.n,)), pltpu.SemaphoreType.DMA((3 * self.n,))]

    def _copies(self, ins, outs, scratch):
        ssem, rsem = scratch
        x, y, c = lax.axis_index("x"), lax.axis_index("y"), lax.axis_index("c")
        return [_remote(ins[i].at[2 * px + py], outs[i].at[k], ssem.at[3 * i + k], rsem.at[3 * i + k], (px, py, c))
                for i in range(self.n) for k, (px, py) in enumerate(_chip_peers(x, y))]

    def start(self, ins, outs, scratch):
        for cp in self._copies(ins, outs, scratch):
            cp.start()

    def finish(self, ins, outs, scratch):
        for cp in self._copies(ins, outs, scratch):
            cp.wait()


def _scatter_partials(sc):
    n = sc.n

    def body(*refs):
        args = (refs[:n], refs[n:2 * n], refs[2 * n:])
        sc.start(*args)
        sc.finish(*args)

    return pl.pallas_call(
        body, name="scatter_partials", in_specs=[ANY] * n, out_specs=[ANY] * n, out_shape=sc.out_shape,
        scratch_shapes=sc.scratch,
    )(*sc.ins)


class _Reduce:
    def __init__(self, grads, names):
        self.names = names
        g4, got = _swap_halves(grads)
        self.sums, sums_b = [], []
        for nm, g, t in zip(names, g4, got):
            s_, sb_ = _chip_sum(g, t, f"chip_sum_{nm}")
            self.sums.append(s_)
            sums_b.append(sb_)
        self.scatter = _Scatter(sums_b)

    def collect(self, parts):
        return [_mesh_sum(s, p, f"mesh_sum_{nm}") for nm, s, p in zip(self.names, self.sums, parts)]


def _mesh_sum(sums, parts, name):
    _, R, W = sums.shape
    tr = _tile(R, max(16, min(512, (1 << 18) // W // 16 * 16)))
    me = (2 * lax.axis_index("x") + lax.axis_index("y")).astype(jnp.int32).reshape(1)

    def body(me_ref, m_ref, p_ref, o_ref):
        o_ref[...] = m_ref[...] + p_ref[0].astype(F32) + p_ref[1].astype(F32) + p_ref[2].astype(F32)

    return pl.pallas_call(
        body, name=name,
        grid_spec=pltpu.PrefetchScalarGridSpec(
            num_scalar_prefetch=1, grid=(R // tr,),
            in_specs=[pl.BlockSpec((None, tr, W), lambda i, me_ref: (me_ref[0], i, 0)),
                      pl.BlockSpec((3, tr, W), lambda i, me_ref: (0, i, 0))],
            out_specs=pl.BlockSpec((tr, W), lambda i, me_ref: (i, 0))),
        out_shape=_sds((R, W), F32), compiler_params=_params(("parallel",), 32),
    )(me, sums, parts)


def _share_halves(reduced):
    n = len(reduced)
    plan = _chunk_plan([r.shape for r in reduced], 4)
    max_rows = max(p[2] for p in plan)
    max_w = max(r.shape[1] for r in reduced)

    def body(*refs):
        ins, outs = refs[:n], refs[n:2 * n]
        ssem, rsem, lsem, osem, buf = refs[2 * n:]
        x, y, c = lax.axis_index("x"), lax.axis_index("y"), lax.axis_index("c")
        sib = (x, y, 1 - c)
        pending = {}
        for k, (i, r0, ch) in enumerate(plan):
            slot = k % 2
            if slot in pending:
                rc, lc = pending[slot]
                rc.wait_send()
                lc.wait()
            stage = buf.at[slot, pl.ds(0, ch), pl.ds(0, reduced[i].shape[1])]
            ld = pltpu.make_async_copy(ins[i].at[pl.ds(r0, ch)], stage, lsem.at[slot])
            ld.start()
            ld.wait()
            place = outs[i].at[c, pl.ds(r0, ch)]
            rc = _remote(stage, place, ssem.at[slot], rsem.at[i], sib)
            lc = pltpu.make_async_copy(stage, place, osem.at[slot])
            rc.start()
            lc.start()
            pending[slot] = (rc, lc)
        for rc, lc in pending.values():
            rc.wait_send()
            lc.wait()
        for i in range(n):
            theirs = outs[i].at[1 - c]
            _remote(theirs, theirs, ssem.at[0], rsem.at[i], sib).wait_recv()

    outs = pl.pallas_call(
        body, name="share_halves", in_specs=[ANY] * n, out_specs=[ANY] * n,
        out_shape=[_sds((2,) + r.shape, F32) for r in reduced],
        scratch_shapes=[pltpu.SemaphoreType.DMA((2,)), pltpu.SemaphoreType.DMA((n,)), pltpu.SemaphoreType.DMA((2,)),
                        pltpu.SemaphoreType.DMA((2,)), pltpu.VMEM((2, max_rows, max_w), F32)],
        compiler_params=pltpu.CompilerParams(vmem_limit_bytes=32 << 20),
    )(*reduced)
    return [o.reshape(2 * r.shape[0], r.shape[1]) for o, r in zip(outs, reduced)]


def _tile(rows, cap=256):
    t = min(rows, cap) // 16 * 16
    while rows % t:
        t -= 16
    return t


def _elementwise(fn, ins, out_dtypes, name):
    R, W = ins[0].shape
    tr = _tile(R, max(8, min(512, (1 << 18) // W // 8 * 8)))

    def body(*refs):
        outs = fn(*[r[...] for r in refs[:len(ins)]])
        for o_ref, o in zip(refs[len(ins):], outs):
            o_ref[...] = o.astype(o_ref.dtype)

    return pl.pallas_call(
        body, grid=(R // tr,), name=name, in_specs=[_row(tr, W)] * len(ins), out_specs=[_row(tr, W)] * len(out_dtypes),
        out_shape=[_sds((R, W), dt) for dt in out_dtypes],
        compiler_params=_params(("parallel",), 48),
    )(*ins)


def _adamw(w, g, m, v):
    m = B1 * m + (1.0 - B1) * g
    v = B2 * v + (1.0 - B2) * (g * g)
    m_hat = m / (1.0 - B1 ** STEP)
    v_hat = v / (1.0 - B2 ** STEP)
    return -LR * (m_hat / (jnp.sqrt(v_hat) + AEPS) + WD * w), m, v


def _adam_small(w, m, v, part):
    P = part.shape[0]

    def body(w_ref, m_ref, v_ref, p_ref, g_ref, d_ref, nm_ref, nv_ref, all_s, ssem, rsem):
        x, y, c = lax.axis_index("x"), lax.axis_index("y"), lax.axis_index("c")
        me = 4 * x + 2 * y + c
        all_s[me] = p_ref[...]
        cps = []
        for rel in range(1, 8):
            px = 1 - x if rel & 4 else x
            py = 1 - y if rel & 2 else y
            pc = 1 - c if rel & 1 else c
            rc = _remote(p_ref, all_s.at[me], ssem.at[rel - 1], rsem.at[rel - 1], (px, py, pc))
            rc.start()
            cps.append((rc, 4 * px + 2 * py + pc))
        for rel, (rc, peer) in enumerate(cps):
            rc.wait_send()
            _remote(p_ref, all_s.at[peer], ssem.at[rel], rsem.at[rel], (x, y, c)).wait_recv()
        g = all_s[0]
        for k in range(1, 8):
            g = g + all_s[k]
        g_ref[...] = g
        d_ref[...], nm_ref[...], nv_ref[...] = _adamw(w_ref[...], g, m_ref[...], v_ref[...])

    return pl.pallas_call(
        body, name="adam_small", out_shape=[_sds(w.shape, F32)] * 4,
        scratch_shapes=[pltpu.VMEM((8, P, 128), F32), pltpu.SemaphoreType.DMA((7,)), pltpu.SemaphoreType.DMA((7,))],
        compiler_params=pltpu.CompilerParams(vmem_limit_bytes=32 << 20),
    )(w, m, v, part)


def _local_step(xs, tgt, mems, weights, small, gather_mid=None, gather_ffn=None, reduce=False):
    wt_in, wt_ba, wt_bs, wo, wq, wkv, wt_o, wt_gu, wd = weights
    g_mix, b_gate, w_sgu, b_sgu, g_sgu, g_cross, g_mem, g_ffn, g_final = small
    wt = jnp.tril(w_sgu).astype(BF16)
    bst = b_sgu.T

    (a, qkv0, qkv1, qkv2, uv, gl), got = _fwd_in(xs, g_mix, wt_in, gather_mid)
    if gather_mid is not None:
        wt_ba, wt_bs, wo, wq, wkv, wt_o = got
    qkvs = (qkv0, qkv1, qkv2)
    os_, ls_ = zip(*[_attn_fwd(qkvs[g], g) for g in range(3)])
    (ya, ys, ba, bs, mg, h1), got = _fwd_mid(xs, os_, ls_, uv, gl, wt, bst, g_sgu, b_gate, wt_ba, wt_bs, wo, gather_ffn)
    if gather_ffn is not None:
        wt_gu, wd = got
    mb, kv = _mem_fwd(mems, g_mem, wkv)
    cb, qc, oc, h2 = _fwd_cross(h1, g_cross, wq, kv, wt_o)
    f, act, dgu, dh3b, dh2, dh2b, dg_ffn, dg_final, loss = _ffn_fwd_bwd(h2, tgt, g_ffn, g_final, wt_gu, wd)

    g_ffn_w = [_tn_matmul(dgu, f, "dw_gate_up", 512), _tn_matmul(act, dh3b, "dw_down", 256)]
    r_ffn = _Reduce(g_ffn_w, ["w_gate_up", "w_down"]) if reduce else None
    (dqc, dh1, dh1b, dkv, dg_cross), parts_ffn = _bwd_cross(dh2, h1, qc, g_cross, wq, kv, wt_o,
                                                           r_ffn.scatter if reduce else None)
    dw_kv, dg_mem = _mem_bwd(dkv, mems, mb, g_mem, wkv)
    (dba, dbs, dgl, duv, do0, do1, do2, c0, c1, c2, db_gate, dg_sgu, dws, dbs_acc) = _bwd_mid(
        dh1, gl, ba, bs, uv, ls_, ya, wt, bst, g_sgu, b_gate, wt_ba, wt_bs, wo)
    dqkvs = [_attn_bwd(qkvs[g], do, ls_[g], corr, g) for g, (do, corr) in enumerate(((do0, c0), (do1, c1), (do2, c2)))]
    grad_x, dproj, dg_mix = _bwd_in(dqkvs, duv, dgl, dh1, xs, g_mix, wt_in)
    g_mid_w = [_tn_matmul(dba, ya, "dw_branch_attn", 512),
               _tn_matmul(dbs, ys, "dw_branch_sgu", 512),
               _tn_matmul(mg, dh1b, "dw_out", 512),
               _tn_matmul(cb, dqc, "dw_q_cross", 512),
               dw_kv,
               _tn_matmul(dh2b, oc, "dw_o_cross", 512)]
    small_terms = (dg_mix, db_gate, dws, dbs_acc, dg_sgu, dg_cross, dg_mem, dg_ffn, dg_final)
    if not reduce:
        return loss, grad_x, [_tn_matmul(dproj, a, "dw_in", 768)] + g_mid_w + g_ffn_w, small_terms
    r_mid = _Reduce(g_mid_w, ["w_branch_attn", "w_branch_sgu", "w_out", "w_q_cross", "w_kv_cross", "w_o_cross"])
    g_in, parts_mid = _tn_matmul(dproj, a, "dw_in", 768, comm=r_mid.scatter)
    r_in = _Reduce([g_in], ["w_in"])
    halves = r_in.collect(_scatter_partials(r_in.scatter)) + r_mid.collect(parts_mid) + r_ffn.collect(parts_ffn)
    return loss, grad_x, halves, small_terms


def kernel(x, mem, g_mix, w_in, b_gate, w_sgu_spatial, b_sgu_spatial, g_sgu, w_branch_attn, w_branch_sgu, w_out, g_cross, g_mem, w_q_cross, w_kv_cross, w_o_cross, g_ffn, w_gate_up, w_down, g_final, loss_target, m_g_mix, m_w_in, m_b_gate, m_w_sgu_spatial, m_b_sgu_spatial, m_g_sgu, m_w_branch_attn, m_w_branch_sgu, m_w_out, m_g_cross, m_g_mem, m_w_q_cross, m_w_kv_cross, m_w_o_cross, m_g_ffn, m_w_gate_up, m_w_down, m_g_final, v_g_mix, v_w_in, v_b_gate, v_w_sgu_spatial, v_b_sgu_spatial, v_g_sgu, v_w_branch_attn, v_w_branch_sgu, v_w_out, v_g_cross, v_g_mem, v_w_q_cross, v_w_kv_cross, v_w_o_cross, v_g_ffn, v_w_gate_up, v_w_down, v_g_final):
    S = x.shape[1]
    xs, tgt, mems = x.reshape(S, D), loss_target.reshape(S, D), mem.reshape(mem.shape[1], D)
    g_final2 = g_final.reshape(1, D)

    big = [("w_in", w_in[0], m_w_in[0], v_w_in[0], True),
           ("w_branch_attn", w_branch_attn[0], m_w_branch_attn[0], v_w_branch_attn[0], True),
           ("w_branch_sgu", w_branch_sgu[0], m_w_branch_sgu[0], v_w_branch_sgu[0], True),
           ("w_out", w_out[0], m_w_out[0], v_w_out[0], False),
           ("w_q_cross", w_q_cross[0], m_w_q_cross[0], v_w_q_cross[0], False),
           ("w_kv_cross", w_kv_cross[0], m_w_kv_cross[0], v_w_kv_cross[0], False),
           ("w_o_cross", w_o_cross[0], m_w_o_cross[0], v_w_o_cross[0], True),
           ("w_gate_up", w_gate_up[0], m_w_gate_up[0], v_w_gate_up[0], True),
           ("w_down", w_down[0], m_w_down[0], v_w_down[0], False)]
    shards = [(w.T if tr else w).astype(BF16) for _, w, _, _, tr in big]
    (wt_in,) = _gather_weights(shards[:1])
    (loss, grad_x, reduced, (dg_mix, db_gate, dws, dbs_acc, dg_sgu, dg_cross, dg_mem, dg_ffn, dg_final)) = _local_step(
        xs, tgt, mems, (wt_in,) + (None,) * 8,
        (g_mix, b_gate, w_sgu_spatial[0], b_sgu_spatial[0], g_sgu, g_cross, g_mem, g_ffn, g_final2),
        _Gather(shards[1:7]), _Gather(shards[7:9]), reduce=True)
    full = _share_halves(reduced)

    big_out = {}
    for (name, w, m, v, tr), gsh in zip(big, full):
        gsh = gsh.T if tr else gsh
        delta, nm, nv = _elementwise(_adamw, [w, gsh, m, v], [F32, F32, F32], f"adam_{name}")
        big_out[name] = tuple(t[None] for t in (gsh, delta, nm, nv))

    small = [("g_mix", g_mix, m_g_mix, v_g_mix, dg_mix), ("b_gate", b_gate, m_b_gate, v_b_gate, db_gate),
             ("w_sgu_spatial", w_sgu_spatial, m_w_sgu_spatial, v_w_sgu_spatial, jnp.tril(dws)),
             ("b_sgu_spatial", b_sgu_spatial, m_b_sgu_spatial, v_b_sgu_spatial, jnp.sum(dbs_acc, axis=-1)),
             ("g_sgu", g_sgu, m_g_sgu, v_g_sgu, dg_sgu), ("g_cross", g_cross, m_g_cross, v_g_cross, dg_cross),
             ("g_mem", g_mem, m_g_mem, v_g_mem, dg_mem), ("g_ffn", g_ffn, m_g_ffn, v_g_ffn, dg_ffn),
             ("g_final", g_final, m_g_final, v_g_final, dg_final)]

    def pack(parts_, tail):
        return jnp.concatenate([p.reshape(-1) for p in parts_] + [tail]).reshape(-1, 128)

    zeros = jnp.zeros((1024,), F32)
    gp = pack([s[4] for s in small], jnp.pad(loss.reshape(-1)[:1], (0, 1023)))
    wp, mp, vp = (pack([s[k] for s in small], zeros) for k in (1, 2, 3))
    gsum, dsm, nms, nvs = _adam_small(wp, mp, vp, gp)
    small_out, off = {}, 0
    for name, w, _, _, _ in small:
        n = w.size
        small_out[name] = tuple(t.reshape(-1)[off:off + n].reshape(w.shape) for t in (gsum, dsm, nms, nvs))
        off += n
    total_loss = gsum.reshape(-1)[off]

    order = ["g_mix", "w_in", "b_gate", "w_sgu_spatial", "b_sgu_spatial", "g_sgu", "w_branch_attn", "w_branch_sgu",
             "w_out", "g_cross", "g_mem", "w_q_cross", "w_kv_cross", "w_o_cross", "g_ffn", "w_gate_up", "w_down",
             "g_final"]
    res = {**big_out, **small_out}
    outs = [total_loss, grad_x.reshape(x.shape)]
    for k in range(4):
        outs += [res[nm][k] for nm in order]
    return tuple(outs)
```

```python
import math

import numpy as np
import jax
import jax.numpy as jnp
from jax import lax
from jax.experimental import pallas as pl
from jax.experimental.pallas import tpu as pltpu

F32, BF16 = jnp.float32, jnp.bfloat16
MESH = pl.DeviceIdType.MESH
ANY = pl.BlockSpec(memory_space=pl.ANY)
RES = pl.BlockSpec(memory_space=pltpu.VMEM)


def _whole(arr):
    nd = arr.ndim
    return pl.BlockSpec(arr.shape, lambda *_: (0,) * nd)

D = 1024
HEAD = 64
GROUP_W = 256
DIL_GROUPS = ((128, 1), (512, 4), (2048, 16))
BLK = 128
SGU_W = 512
MEM_HEADS, MEM_HD, MEM_W = 4, 128, 512
D_FF = 2816
FF_CHUNK = 256
EPS = 1e-6
NEG = -1e30
LR, B1, B2, AEPS, WD, STEP = 0.001, 0.9, 0.999, 1e-08, 0.01, 10
GELU_K, GELU_C = 0.7978845608028654, 0.044715


def _dot(a, b):
    return jnp.dot(a, b, preferred_element_type=F32)


def _dot_nt(a, b):
    return lax.dot_general(a, b, (((1,), (1,)), ((), ())), preferred_element_type=F32)


def _dot_tn(a, b):
    return lax.dot_general(a, b, (((0,), (0,)), ((), ())), preferred_element_type=F32)


def _row(tm, w):
    return pl.BlockSpec((tm, w), lambda i: (i, 0))


def _acc(shape):
    return pl.BlockSpec(shape, lambda i: (0,) * len(shape))


def _params(sem, mb):
    return pltpu.CompilerParams(dimension_semantics=sem, vmem_limit_bytes=mb << 20)


def _sds(shape, dt):
    return jax.ShapeDtypeStruct(shape, dt)


def _rms(h):
    return lax.rsqrt(jnp.mean(h * h, axis=-1, keepdims=True) + EPS)


def _rms_bwd(dy, h, r, g):
    t = dy * g
    dh = r * t - h * (r * r * r) * jnp.mean(t * h, axis=-1, keepdims=True)
    return dh, dy * h * r


def _gelu(x):
    t = jnp.tanh(GELU_K * (x + GELU_C * x * x * x))
    return 0.5 * x * (1.0 + t), t


def _gelu_grad(x, t):
    return 0.5 * (1.0 + t) + 0.5 * x * (1.0 - t * t) * GELU_K * (1.0 + 3.0 * GELU_C * x * x)


def _alibi_slopes():
    def pow2(n):
        start = 2.0 ** (-8.0 / n)
        return [start ** (i + 1) for i in range(n)]
    n = 12
    c = 2 ** int(math.floor(math.log2(n)))
    s = pow2(c) + pow2(2 * c)[0::2][: n - c]
    return np.array(sorted(s, reverse=True), dtype=np.float32).reshape(3, 4)


def _attn_bias(g):
    win, dil = DIL_GROUPS[g]
    steps = (np.arange(BLK)[:, None] + BLK) - np.arange(2 * BLK)[None, :]
    valid = (steps >= 0) & (steps <= win // dil)
    dist = (np.clip(steps, 0, None) * dil).astype(np.float32)
    b = -_alibi_slopes()[g][:, None, None] * dist[None]
    return np.where(valid[None], b, NEG).astype(np.float32)


def _head_masks():
    lane = lax.broadcasted_iota(jnp.int32, (1, GROUP_W), 1)
    return lane, [(lane >= HEAD * h) & (lane < HEAD * (h + 1)) for h in range(4)]


ATT_NB = 4


def _stack_heads(t, masks):
    z = jnp.zeros_like(t)
    return jnp.concatenate([jnp.where(m, t, z) for m in masks], axis=0)


def _unstack_heads(t, masks):
    out = jnp.zeros((BLK, GROUP_W), t.dtype)
    for h, m in enumerate(masks):
        out = jnp.where(m, t[h * BLK:(h + 1) * BLK], out)
    return out


def _stack_cols(ref, rows):
    return jnp.concatenate([ref[rows, HEAD * h:HEAD * h + 1] for h in range(4)], axis=0)


def _dil_spec(d, tm, w):
    return pl.BlockSpec((d, tm // d, w), lambda i: (0, i, 0))


def _to_dilated(val, s_ref, d, write):
    tm, w = val.shape
    for j in range(w // 128):
        s_ref[j, pl.ds(0, tm), :] = val[:, j * 128:(j + 1) * 128]
    for r in range(d):
        for j in range(w // 128):
            write(r, j, s_ref[j, pl.ds(r, tm // d, stride=d), :])


def _from_dilated(ref, s_ref, d, tm, w):
    if d == 1:
        return ref[0].astype(F32)
    for r in range(d):
        for j in range(w // 128):
            s_ref[j, pl.ds(r, tm // d, stride=d), :] = ref[r, :, j * 128:(j + 1) * 128].astype(F32)
    return jnp.concatenate([s_ref[j, pl.ds(0, tm), :] for j in range(w // 128)], axis=1)


def _fwd_in(x, g_mix, wt_in, gather=None, tm=512):
    S = x.shape[0]
    dils = [d for _, d in DIL_GROUPS]
    n = 0 if gather is None else gather.n
    last = S // tm - 1

    def body(*refs):
        x_ref, g_ref, w_ref = refs[:3]
        a_ref, q0_ref, q1_ref, q2_ref, uv_ref, gl_ref = refs[3 + n:9 + n]
        s_ref = refs[9 + 2 * n]
        comm = (refs[3:3 + n], refs[9 + n:9 + 2 * n], refs[10 + 2 * n:])
        if gather is not None:
            pl.when(pl.program_id(0) == 0)(lambda: gather.start(*comm))
        xv = x_ref[...]
        a = (xv * _rms(xv) * g_ref[...]).astype(BF16)
        a_ref[...] = a
        for g, (d, out) in enumerate(zip(dils, (q0_ref, q1_ref, q2_ref))):
            for part in range(3):
                rows = part * 768 + g * 256
                val = _dot_nt(a, w_ref[rows:rows + 256, :])
                if d == 1:
                    out[0, :, part * 256:(part + 1) * 256] = val.astype(BF16)
                else:
                    def write(r, j, piece, out=out, part=part):
                        out[r, :, part * 256 + j * 128:part * 256 + (j + 1) * 128] = piece.astype(BF16)
                    _to_dilated(val, s_ref, d, write)
        uv_ref[...] = _dot_nt(a, w_ref[2304:3328, :]).astype(BF16)
        gl_ref[...] = _dot_nt(a, w_ref[3328:5376, :]).astype(BF16)
        if gather is not None:
            pl.when(pl.program_id(0) == last)(lambda: gather.finish(*comm))

    outs = pl.pallas_call(
        body, grid=(S // tm,), name="fwd_in",
        in_specs=[_row(tm, D), _whole(g_mix), RES] + [ANY] * n,
        out_specs=[_row(tm, D)] + [_dil_spec(d, tm, 768) for d in dils] + [_row(tm, 1024), _row(tm, 2048)] + [ANY] * n,
        out_shape=[_sds((S, D), BF16)] + [_sds((d, S // d, 768), BF16) for d in dils]
        + [_sds((S, 1024), BF16), _sds((S, 2048), BF16)] + ([] if gather is None else gather.out_shape),
        scratch_shapes=[pltpu.VMEM((2, tm, 128), F32)] + ([] if gather is None else gather.scratch),
        compiler_params=_params(("arbitrary",), 60),
    )(x, g_mix, wt_in, *([] if gather is None else gather.halves))
    return outs[:6], ([] if gather is None else gather.full(outs[6:]))


def _attn_fwd(qkv, g):
    d, L, _ = qkv.shape
    nb = L // BLK
    bias = jnp.asarray(_attn_bias(g).reshape(4 * BLK, 2 * BLK))
    NB = min(ATT_NB, nb)
    W = NB * BLK

    def body(q_ref, kc_ref, kp_ref, vc_ref, vp_ref, b_ref, o_ref, l_ref):
        st = pl.program_id(1)
        k_all = jnp.concatenate([kp_ref[...], kc_ref[...]], axis=0)
        v_all = jnp.concatenate([vp_ref[...], vc_ref[...]], axis=0)
        lane, masks = _head_masks()
        for b in range(NB):
            rows = slice(b * BLK, (b + 1) * BLK)
            kk, vv = k_all[b * BLK:(b + 2) * BLK], v_all[b * BLK:(b + 2) * BLK]
            s = _dot_nt(_stack_heads(q_ref[rows, :], masks), kk) * 0.125 + b_ref[...]
            if b == 0:
                s = s + jnp.where((st == 0) & (lane < BLK), NEG, 0.0).astype(F32)
            mx = jnp.max(s, axis=-1, keepdims=True)
            e = jnp.exp(s - mx)
            den = jnp.sum(e, axis=-1, keepdims=True)
            o_ref[rows, :] = _unstack_heads(_dot(e.astype(BF16), vv) / den, masks)
            l_ref[rows, :] = _unstack_heads(mx + jnp.log(den), masks)

    def wide(col):
        return pl.BlockSpec((None, W, GROUP_W), lambda r, s: (r, s, col))

    def before(col):
        return pl.BlockSpec((None, BLK, GROUP_W), lambda r, s: (r, jnp.maximum(s * NB - 1, 0), col))

    return pl.pallas_call(
        body, grid=(d, nb // NB), name=f"attn_fwd_g{g}",
        in_specs=[wide(0), wide(1), before(1), wide(2), before(2),
                  pl.BlockSpec((4 * BLK, 2 * BLK), lambda r, s: (0, 0))],
        out_specs=[wide(0), wide(0)],
        out_shape=[_sds((d, L, GROUP_W), F32), _sds((d, L, GROUP_W), F32)],
        compiler_params=_params(("parallel", "parallel"), 32),
    )(qkv, qkv, qkv, qkv, qkv, bias)


def _group_weights(l0, l1, l2):
    m = jnp.maximum(jnp.maximum(l0, l1), l2)
    e0, e1, e2 = jnp.exp(l0 - m), jnp.exp(l1 - m), jnp.exp(l2 - m)
    inv = 1.0 / (e0 + e1 + e2)
    return e0 * inv, e1 * inv, e2 * inv


def _sgu_forward(uvf, gs, wt_ref, bst_ref, mixed_s, tm):
    z, t = _gelu(uvf)
    u, v = z[:, :SGU_W], z[:, SGU_W:]
    rv = _rms(v)
    vnb = (v * rv * gs).astype(BF16)
    for ci in range(tm // 128):
        for g in range(4):
            rs, cs = slice(ci * 128, (ci + 1) * 128), slice(g * 128, (g + 1) * 128)
            mixed_s[rs, cs] = _dot(wt_ref[g], vnb[rs, cs]) + bst_ref[:, g:g + 1]
    return u, v, rv, vnb, t


def _fwd_mid(x, os_, ls_, uv, gl, wt, bst, g_sgu, b_gate, wt_ba, wt_bs, w_out, gather=None, tm=512):
    S = x.shape[0]
    dils = [d for _, d in DIL_GROUPS]
    n = 0 if gather is None else gather.n
    last = S // tm - 1

    def body(*refs):
        (x_ref, o0, o1, o2, l0, l1, l2, uv_ref, gl_ref, wt_ref, bst_ref, gs_ref, bg_ref, wba_ref, wbs_ref,
         wo_ref) = refs[:16]
        ya_ref, ys_ref, ba_ref, bs_ref, mg_ref, h1_ref = refs[16 + n:22 + n]
        mixed_s, il_s = refs[22 + 2 * n:24 + 2 * n]
        comm = (refs[16:16 + n], refs[22 + n:22 + 2 * n], refs[24 + 2 * n:])
        if gather is not None:
            pl.when(pl.program_id(0) == 0)(lambda: gather.start(*comm))
        ls = [_from_dilated(r, il_s, d, tm, GROUP_W) for r, d in zip((l0, l1, l2), dils)]
        alphas = _group_weights(*ls)
        ya = jnp.zeros((tm, GROUP_W), F32)
        for a, r, d in zip(alphas, (o0, o1, o2), dils):
            ya = ya + a * _from_dilated(r, il_s, d, tm, GROUP_W)
        yab = ya.astype(BF16)
        ya_ref[...] = yab
        u, _, _, _, _ = _sgu_forward(uv_ref[...].astype(F32), gs_ref[...], wt_ref, bst_ref, mixed_s, tm)
        ysb = (u * mixed_s[...]).astype(BF16)
        ys_ref[...] = ysb
        gates = jax.nn.sigmoid(gl_ref[...].astype(F32) + bg_ref[...])
        ba = _dot_nt(yab, wba_ref[...])
        bs = _dot_nt(ysb, wbs_ref[...])
        ba_ref[...] = ba.astype(BF16)
        bs_ref[...] = bs.astype(BF16)
        mgb = (gates[:, :D] * ba + gates[:, D:] * bs).astype(BF16)
        mg_ref[...] = mgb
        h1_ref[...] = x_ref[...] + _dot(mgb, wo_ref[...])
        if gather is not None:
            pl.when(pl.program_id(0) == last)(lambda: gather.finish(*comm))

    gw = _row(tm, GROUP_W)
    dil = [_dil_spec(d, tm, GROUP_W) for d in dils]
    outs = pl.pallas_call(
        body, grid=(S // tm,), name="fwd_mid",
        in_specs=[_row(tm, D)] + dil + dil + [_row(tm, 1024), _row(tm, 2048)]
        + [_whole(t) for t in (wt, bst, g_sgu, b_gate)] + [RES] * 3 + [ANY] * n,
        out_specs=[gw, _row(tm, SGU_W), _row(tm, D), _row(tm, D), _row(tm, D), _row(tm, D)] + [ANY] * n,
        out_shape=[_sds((S, GROUP_W), BF16), _sds((S, SGU_W), BF16), _sds((S, D), BF16), _sds((S, D), BF16),
                   _sds((S, D), BF16), _sds((S, D), F32)] + ([] if gather is None else gather.out_shape),
        scratch_shapes=[pltpu.VMEM((tm, SGU_W), F32), pltpu.VMEM((2, tm, 128), F32)]
        + ([] if gather is None else gather.scratch),
        compiler_params=_params(("arbitrary",), 56),
    )(x, *os_, *ls_, uv, gl, wt, bst, g_sgu, b_gate, wt_ba, wt_bs, w_out, *([] if gather is None else gather.halves))
    return outs[:6], ([] if gather is None else gather.full(outs[6:]))


def _mem_fwd(mem, g_mem, w_kv):
    def body(m_ref, g_ref, w_ref, mb_ref, kv_ref):
        mv = m_ref[...]
        mb = (mv * _rms(mv) * g_ref[...]).astype(BF16)
        mb_ref[...] = mb
        kv_ref[...] = _dot(mb, w_ref[...]).astype(BF16)

    return pl.pallas_call(
        body, name="mem_fwd",
        out_shape=[_sds(mem.shape, BF16), _sds((mem.shape[0], 2 * MEM_W), BF16)],
        compiler_params=pltpu.CompilerParams(vmem_limit_bytes=32 << 20),
    )(mem, g_mem, w_kv)


def _cross_probs(qh, kh):
    s = _dot_nt(qh, kh) * (MEM_HD ** -0.5)
    e = jnp.exp(s - jnp.max(s, axis=-1, keepdims=True))
    return e / jnp.sum(e, axis=-1, keepdims=True)


def _fwd_cross(h1, g_cross, w_q, kv, wt_o, tm=512):
    S = h1.shape[0]

    def body(h_ref, g_ref, wq_ref, kv_ref, wo_ref, c_ref, qc_ref, oc_ref, h2_ref):
        hv = h_ref[...]
        cb = (hv * _rms(hv) * g_ref[...]).astype(BF16)
        c_ref[...] = cb
        qcb = _dot(cb, wq_ref[...]).astype(BF16)
        qc_ref[...] = qcb
        for h in range(MEM_HEADS):
            cs = slice(h * MEM_HD, (h + 1) * MEM_HD)
            p = _cross_probs(qcb[:, cs], kv_ref[:, cs])
            oc_ref[:, cs] = _dot(p.astype(BF16), kv_ref[:, MEM_W + h * MEM_HD:MEM_W + (h + 1) * MEM_HD]).astype(BF16)
        h2_ref[...] = hv + _dot_nt(oc_ref[...], wo_ref[...])

    return pl.pallas_call(
        body, grid=(S // tm,), name="fwd_cross",
        in_specs=[_row(tm, D), _whole(g_cross), RES, _whole(kv), RES],
        out_specs=[_row(tm, D), _row(tm, MEM_W), _row(tm, MEM_W), _row(tm, D)],
        out_shape=[_sds((S, D), BF16), _sds((S, MEM_W), BF16), _sds((S, MEM_W), BF16), _sds((S, D), F32)],
        compiler_params=_params(("parallel",), 40),
    )(h1, g_cross, w_q, kv, wt_o)


def _ffn_fwd_bwd(h2, target, g_ffn, g_final, wt_gu, w_down, tm=256):
    S = h2.shape[0]
    nch = D_FF // FF_CHUNK

    def body(h_ref, t_ref, gf_ref, gz_ref, wgu_ref, wd_ref,
             f_ref, act_ref, dgu_ref, dh3b_ref, dh2_ref, dh2b_ref, dgf_ref, dgz_ref, loss_ref, gu_s):
        i = pl.program_id(0)

        @pl.when(i == 0)
        def _():
            dgf_ref[...] = jnp.zeros_like(dgf_ref)
            dgz_ref[...] = jnp.zeros_like(dgz_ref)
            loss_ref[...] = jnp.zeros_like(loss_ref)

        def weights(c):
            return (wgu_ref.at[pl.ds(c * FF_CHUNK, FF_CHUNK)], wgu_ref.at[pl.ds(D_FF + c * FF_CHUNK, FF_CHUNK)],
                    wd_ref.at[pl.ds(c * FF_CHUNK, FF_CHUNK)])

        hv = h_ref[...]
        r2 = _rms(hv)
        gf = gf_ref[...]
        fb = (hv * r2 * gf).astype(BF16)
        f_ref[...] = fb
        h3 = hv
        for c in range(nch):
            cs = slice(c * FF_CHUNK, (c + 1) * FF_CHUNK)
            us = slice(D_FF + c * FF_CHUNK, D_FF + (c + 1) * FF_CHUNK)
            wg, wu, wd = weights(c)
            gt = _dot_nt(fb, wg[...])
            up = _dot_nt(fb, wu[...])
            gu_s[:, cs] = gt
            gu_s[:, us] = up
            actb = (gt * jax.nn.sigmoid(gt) * up).astype(BF16)
            act_ref[:, cs] = actb
            h3 = h3 + _dot(actb, wd[...])
        r3 = _rms(h3)
        gz = gz_ref[...]
        diff = h3 * r3 * gz - t_ref[...]
        dy = diff * (1.0 / D)
        dh3, dgz_rows = _rms_bwd(dy, h3, r3, gz)
        dh3b = dh3.astype(BF16)
        dh3b_ref[...] = dh3b
        df = jnp.zeros((tm, D), F32)
        for c in range(nch):
            cs = slice(c * FF_CHUNK, (c + 1) * FF_CHUNK)
            us = slice(D_FF + c * FF_CHUNK, D_FF + (c + 1) * FF_CHUNK)
            wg, wu, wd = weights(c)
            dact = _dot_nt(dh3b, wd[...])
            gt, up = gu_s[:, cs], gu_s[:, us]
            sg = jax.nn.sigmoid(gt)
            dgt = (dact * up * (sg * (1.0 + gt * (1.0 - sg)))).astype(BF16)
            dup = (dact * (gt * sg)).astype(BF16)
            dgu_ref[:, cs] = dgt
            dgu_ref[:, us] = dup
            df = df + _dot(dgt, wg[...]) + _dot(dup, wu[...])
        dhn, dgf_rows = _rms_bwd(df, hv, r2, gf)
        dh2 = dh3 + dhn
        dh2_ref[...] = dh2
        dh2b_ref[...] = dh2.astype(BF16)
        dgf_ref[...] += jnp.sum(dgf_rows, axis=0, keepdims=True)
        dgz_ref[...] += jnp.sum(dgz_rows, axis=0, keepdims=True)
        loss_ref[...] += jnp.sum(jnp.sum(diff * diff, axis=0, keepdims=True), axis=1, keepdims=True) * (0.5 / D)

    return pl.pallas_call(
        body, grid=(S // tm,), name="ffn_fwd_bwd",
        in_specs=[_row(tm, D), _row(tm, D), _whole(g_ffn), _whole(g_final), RES, RES],
        out_specs=[_row(tm, D), _row(tm, D_FF), _row(tm, 2 * D_FF), _row(tm, D), _row(tm, D), _row(tm, D),
                   _acc((1, D)), _acc((1, D)), _acc((1, 128))],
        out_shape=[_sds((S, D), BF16), _sds((S, D_FF), BF16), _sds((S, 2 * D_FF), BF16), _sds((S, D), BF16),
                   _sds((S, D), F32), _sds((S, D), BF16), _sds((1, D), F32), _sds((1, D), F32), _sds((1, 128), F32)],
        scratch_shapes=[pltpu.VMEM((tm, 2 * D_FF), F32)],
        compiler_params=_params(("arbitrary",), 56),
    )(h2, target, g_ffn, g_final, wt_gu, w_down)


def _bwd_cross(dh2, h1, qc, g_cross, w_q, kv, wt_o, comm=None, tm=512):
    S = h1.shape[0]
    n = 0 if comm is None else comm.n
    last = S // tm - 1

    def body(*refs):
        d_ref, h_ref, qc_ref, g_ref, wq_ref, kv_ref, wo_ref = refs[:7]
        dqc_ref, dh1_ref, dh1b_ref, dkv_ref, dg_ref = refs[7 + n:12 + n]
        cargs = (refs[7:7 + n], refs[12 + n:12 + 2 * n], refs[12 + 2 * n:])
        i = pl.program_id(0)

        @pl.when(i == 0)
        def _():
            dkv_ref[...] = jnp.zeros_like(dkv_ref)
            dg_ref[...] = jnp.zeros_like(dg_ref)
            if comm is not None:
                comm.start(*cargs)

        dh2 = d_ref[...]
        doc = _dot(dh2.astype(BF16), wo_ref[...])
        qcb = qc_ref[...]
        for h in range(MEM_HEADS):
            cs = slice(h * MEM_HD, (h + 1) * MEM_HD)
            vs = slice(MEM_W + h * MEM_HD, MEM_W + (h + 1) * MEM_HD)
            qh, kh, vh = qcb[:, cs], kv_ref[:, cs], kv_ref[:, vs]
            p = _cross_probs(qh, kh)
            dohb = doc[:, cs].astype(BF16)
            dp = _dot_nt(dohb, vh)
            dsb = (p * (dp - jnp.sum(dp * p, axis=-1, keepdims=True)) * (MEM_HD ** -0.5)).astype(BF16)
            dqc_ref[:, cs] = _dot(dsb, kh).astype(BF16)
            dkv_ref[:, cs] += _dot_tn(dsb, qh)
            dkv_ref[:, vs] += _dot_tn(p.astype(BF16), dohb)
        dc = _dot_nt(dqc_ref[...], wq_ref[...])
        hv = h_ref[...]
        dhn, dg_rows = _rms_bwd(dc, hv, _rms(hv), g_ref[...])
        dh1 = dh2 + dhn
        dh1_ref[...] = dh1
        dh1b_ref[...] = dh1.astype(BF16)
        dg_ref[...] += jnp.sum(dg_rows, axis=0, keepdims=True)
        if comm is not None:
            pl.when(i == last)(lambda: comm.finish(*cargs))

    outs = pl.pallas_call(
        body, grid=(S // tm,), name="bwd_cross",
        in_specs=[_row(tm, D), _row(tm, D), _row(tm, MEM_W), _whole(g_cross), RES, _whole(kv), RES] + [ANY] * n,
        out_specs=[_row(tm, MEM_W), _row(tm, D), _row(tm, D), _acc((256, 2 * MEM_W)), _acc((1, D))] + [ANY] * n,
        out_shape=[_sds((S, MEM_W), BF16), _sds((S, D), F32), _sds((S, D), BF16), _sds((256, 2 * MEM_W), F32),
                   _sds((1, D), F32)] + ([] if comm is None else comm.out_shape),
        scratch_shapes=[] if comm is None else comm.scratch,
        compiler_params=_params(("arbitrary",), 48),
    )(dh2, h1, qc, g_cross, w_q, kv, wt_o, *([] if comm is None else comm.ins))
    return outs[:5], outs[5:]


def _mem_bwd(dkv, mem, mb, g_mem, w_kv):
    def body(dkv_ref, m_ref, mb_ref, g_ref, w_ref, dw_ref, dg_ref):
        dkvb = dkv_ref[...].astype(BF16)
        dw_ref[...] = _dot_tn(mb_ref[...], dkvb)
        dm = _dot_nt(dkvb, w_ref[...])
        mv = m_ref[...]
        dg_ref[...] = jnp.sum(dm * mv * _rms(mv), axis=0, keepdims=True)

    return pl.pallas_call(
        body, name="mem_bwd",
        out_shape=[_sds((D, 2 * MEM_W), F32), _sds((1, D), F32)],
        compiler_params=pltpu.CompilerParams(vmem_limit_bytes=32 << 20),
    )(dkv, mem, mb, g_mem, w_kv)


def _bwd_mid(dh1, gl, ba, bs, uv, ls_, ya, wt, bst, g_sgu, b_gate, wt_ba, wt_bs, w_out, tm=512):
    S = dh1.shape[0]
    dils = [d for _, d in DIL_GROUPS]

    def body(d_ref, gl_ref, ba_ref, bs_ref, uv_ref, l0, l1, l2, ya_ref,
             wt_ref, bst_ref, gs_ref, bg_ref, wba_ref, wbs_ref, wo_ref,
             dba_ref, dbs_ref, dgl_ref, duv_ref, do0, do1, do2, c0, c1, c2,
             dbg_ref, dgs_ref, dws_ref, dbsa_ref, mixed_s, dvn_s, il_s):
        i = pl.program_id(0)

        @pl.when(i == 0)
        def _():
            for r in (dbg_ref, dgs_ref, dws_ref, dbsa_ref):
                r[...] = jnp.zeros_like(r)

        dm = _dot_nt(d_ref[...].astype(BF16), wo_ref[...])
        gates = jax.nn.sigmoid(gl_ref[...].astype(F32) + bg_ref[...])
        g0, g1 = gates[:, :D], gates[:, D:]
        dbab = (dm * g0).astype(BF16)
        dbsb = (dm * g1).astype(BF16)
        dba_ref[...] = dbab
        dbs_ref[...] = dbsb
        dg0 = dm * ba_ref[...].astype(F32) * g0 * (1.0 - g0)
        dg1 = dm * bs_ref[...].astype(F32) * g1 * (1.0 - g1)
        dgl_ref[:, :D] = dg0.astype(BF16)
        dgl_ref[:, D:] = dg1.astype(BF16)
        dbg_ref[:, :D] += jnp.sum(dg0, axis=0, keepdims=True)
        dbg_ref[:, D:] += jnp.sum(dg1, axis=0, keepdims=True)
        dya = _dot(dbab, wba_ref[...])
        dys = _dot(dbsb, wbs_ref[...])

        uvf = uv_ref[...].astype(F32)
        gs = gs_ref[...]
        u, v, rv, vnb, t = _sgu_forward(uvf, gs, wt_ref, bst_ref, mixed_s, tm)
        du = dys * mixed_s[...]
        dmixed = dys * u
        for ci in range(tm // 128):
            for g in range(4):
                rs, cs = slice(ci * 128, (ci + 1) * 128), slice(g * 128, (g + 1) * 128)
                dmx = dmixed[rs, cs]
                dmxb = dmx.astype(BF16)
                dvn_s[rs, cs] = _dot_tn(wt_ref[g], dmxb)
                dws_ref[g] += _dot_nt(dmxb, vnb[rs, cs])
                dbsa_ref[g] += dmx
        dv, dgs_rows = _rms_bwd(dvn_s[...], v, rv, gs)
        dgs_ref[...] += jnp.sum(dgs_rows, axis=0, keepdims=True)
        gg = _gelu_grad(uvf, t)
        duv_ref[:, :SGU_W] = (du * gg[:, :SGU_W]).astype(BF16)
        duv_ref[:, SGU_W:] = (dv * gg[:, SGU_W:]).astype(BF16)

        alphas = _group_weights(*[_from_dilated(r, il_s, d, tm, GROUP_W) for r, d in zip((l0, l1, l2), dils)])
        prod = dya * ya_ref[...].astype(F32)
        _, masks = _head_masks()
        hs = jnp.zeros_like(prod)
        for h in range(4):
            sh = jnp.sum(jnp.where(masks[h], prod, 0.0), axis=-1, keepdims=True)
            hs = jnp.where(masks[h], sh, hs)
        for a, d, do_ref, c_ref in zip(alphas, dils, (do0, do1, do2), (c0, c1, c2)):
            for val, out in ((a * dya, do_ref), (a * hs, c_ref)):
                if d == 1:
                    out[0] = val.astype(out.dtype)
                else:
                    def write(r, j, piece, out=out):
                        out[r, :, j * 128:(j + 1) * 128] = piece.astype(out.dtype)
                    _to_dilated(val, il_s, d, write)

    gw = _row(tm, GROUP_W)
    dil = [_dil_spec(d, tm, GROUP_W) for d in dils]
    return pl.pallas_call(
        body, grid=(S // tm,), name="bwd_mid",
        in_specs=[_row(tm, D), _row(tm, 2048), _row(tm, D), _row(tm, D), _row(tm, 1024)] + dil + [gw]
        + [_whole(t) for t in (wt, bst, g_sgu, b_gate)] + [RES] * 3,
        out_specs=[_row(tm, D), _row(tm, D), _row(tm, 2048), _row(tm, 1024)] + dil + dil
        + [_acc((1, 2048)), _acc((1, SGU_W)), _acc((4, 128, 128)), _acc((4, 128, 128))],
        out_shape=[_sds((S, D), BF16), _sds((S, D), BF16), _sds((S, 2048), BF16), _sds((S, 1024), BF16)]
        + [_sds((d, S // d, GROUP_W), BF16) for d in dils] + [_sds((d, S // d, GROUP_W), F32) for d in dils]
        + [_sds((1, 2048), F32), _sds((1, SGU_W), F32), _sds((4, 128, 128), F32), _sds((4, 128, 128), F32)],
        scratch_shapes=[pltpu.VMEM((tm, SGU_W), F32), pltpu.VMEM((tm, SGU_W), F32), pltpu.VMEM((2, tm, 128), F32)],
        compiler_params=_params(("arbitrary",), 60),
    )(dh1, gl, ba, bs, uv, *ls_, ya, wt, bst, g_sgu, b_gate, wt_ba, wt_bs, w_out)


def _attn_bwd(qkv, do, lse, corr, g):
    d, L, _ = qkv.shape
    nb = L // BLK
    NB = min(ATT_NB, nb)
    W = NB * BLK
    nsteps = nb // NB
    bias = jnp.asarray(_attn_bias(g).reshape(4 * BLK, 2 * BLK))

    def body(q_ref, kc_ref, kp_ref, vc_ref, vp_ref, do_ref, l_ref, c_ref, qn_ref, don_ref, ln_ref, cn_ref, b_ref,
             out_ref, dk_s, dv_s):
        st = pl.program_id(1)
        k_all = jnp.concatenate([kp_ref[...], kc_ref[...]], axis=0)
        v_all = jnp.concatenate([vp_ref[...], vc_ref[...]], axis=0)
        lane, masks = _head_masks()
        dk_s[...] = jnp.zeros_like(dk_s)
        dv_s[...] = jnp.zeros_like(dv_s)

        def block_terms(qs, dos, kk, vv, bias_v, lse_c, corr_c):
            s = _dot_nt(qs, kk) * 0.125 + bias_v
            p = jnp.exp(s - lse_c)
            dsb = (p * (_dot_nt(dos, vv) - corr_c) * 0.125).astype(BF16)
            return dsb, p.astype(BF16)

        for b in range(NB):
            rows = slice(b * BLK, (b + 1) * BLK)
            keys = slice(b * BLK, (b + 2) * BLK)
            kk, vv = k_all[keys], v_all[keys]
            qs, dos = _stack_heads(q_ref[rows, :], masks), _stack_heads(do_ref[rows, :], masks)
            bias_v = b_ref[...]
            if b == 0:
                bias_v = bias_v + jnp.where((st == 0) & (lane < BLK), NEG, 0.0).astype(F32)
            dsb, pb = block_terms(qs, dos, kk, vv, bias_v, _stack_cols(l_ref, rows), _stack_cols(c_ref, rows))
            out_ref[rows, 0:GROUP_W] = _unstack_heads(_dot(dsb, kk), masks).astype(BF16)
            dk_s[keys, :] += _dot_tn(dsb, qs)
            dv_s[keys, :] += _dot_tn(pb, dos)

        @pl.when(st < nsteps - 1)
        def _():
            last = slice(NB * BLK, (NB + 1) * BLK)
            qs, dos = _stack_heads(qn_ref[...], masks), _stack_heads(don_ref[...], masks)
            every = slice(None)
            dsb, pb = block_terms(qs, dos, k_all[last], v_all[last], b_ref[:, :BLK],
                                  _stack_cols(ln_ref, every), _stack_cols(cn_ref, every))
            dk_s[last, :] += _dot_tn(dsb, qs)
            dv_s[last, :] += _dot_tn(pb, dos)

        out_ref[:, GROUP_W:2 * GROUP_W] = dk_s[BLK:, :].astype(BF16)
        out_ref[:, 2 * GROUP_W:] = dv_s[BLK:, :].astype(BF16)

    def wide(col, w=GROUP_W):
        return pl.BlockSpec((None, W, w), lambda r, s: (r, s, col))

    def before(col):
        return pl.BlockSpec((None, BLK, GROUP_W), lambda r, s: (r, jnp.maximum(s * NB - 1, 0), col))

    def after(col):
        return pl.BlockSpec((None, BLK, GROUP_W), lambda r, s: (r, jnp.minimum((s + 1) * NB, nb - 1), col))

    return pl.pallas_call(
        body, grid=(d, nsteps), name=f"attn_bwd_g{g}",
        in_specs=[wide(0), wide(1), before(1), wide(2), before(2), wide(0), wide(0), wide(0),
                  after(0), after(0), after(0), after(0), pl.BlockSpec((4 * BLK, 2 * BLK), lambda r, s: (0, 0))],
        out_specs=wide(0, 768),
        out_shape=_sds((d, L, 768), BF16),
        scratch_shapes=[pltpu.VMEM(((NB + 1) * BLK, GROUP_W), F32), pltpu.VMEM(((NB + 1) * BLK, GROUP_W), F32)],
        compiler_params=_params(("parallel", "parallel"), 32),
    )(qkv, qkv, qkv, qkv, qkv, do, lse, corr, qkv, do, lse, corr, bias)


def _bwd_in(dqkvs, duv, dgl, dh1, x, g_mix, wt_in, tm=512):
    S = x.shape[0]
    dils = [d for _, d in DIL_GROUPS]

    def body(q0_ref, q1_ref, q2_ref, duv_ref, dgl_ref, d_ref, x_ref, g_ref, w_ref, dx_ref, dp_ref, dg_ref, il_s):
        i = pl.program_id(0)

        @pl.when(i == 0)
        def _():
            dg_ref[...] = jnp.zeros_like(dg_ref)

        for g, (d, ref) in enumerate(zip(dils, (q0_ref, q1_ref, q2_ref))):
            nat = _from_dilated(ref, il_s, d, tm, 768).astype(BF16)
            for part in range(3):
                col = part * 768 + g * 256
                dp_ref[:, col:col + 256] = nat[:, part * 256:(part + 1) * 256]
        dp_ref[:, 2304:3328] = duv_ref[...]
        dp_ref[:, 3328:5376] = dgl_ref[...]
        da = _dot(dp_ref[...], w_ref[...])
        xv = x_ref[...]
        dxn, dg_rows = _rms_bwd(da, xv, _rms(xv), g_ref[...])
        dx_ref[...] = d_ref[...] + dxn
        dg_ref[...] += jnp.sum(dg_rows, axis=0, keepdims=True)

    return pl.pallas_call(
        body, grid=(S // tm,), name="bwd_in",
        in_specs=[_dil_spec(d, tm, 768) for d in dils] + [_row(tm, 1024), _row(tm, 2048), _row(tm, D), _row(tm, D),
                                                          _whole(g_mix), RES],
        out_specs=[_row(tm, D), _row(tm, 5376), _acc((1, D))],
        out_shape=[_sds((S, D), F32), _sds((S, 5376), BF16), _sds((1, D), F32)],
        scratch_shapes=[pltpu.VMEM((6, tm, 128), F32)],
        compiler_params=_params(("arbitrary",), 60),
    )(*dqkvs, duv, dgl, dh1, x, g_mix, wt_in)


def _tn_matmul(a, b, name, tk, ts=2048, comm=None):
    S, K = a.shape
    N = b.shape[1]
    n = 0 if comm is None else comm.n
    nk, ns = K // tk, S // ts

    def body(*refs):
        a_ref, b_ref, o_ref = refs[0], refs[1], refs[2 + n]
        cargs = (refs[2:2 + n], refs[3 + n:3 + 2 * n], refs[3 + 2 * n:])
        k, s = pl.program_id(0), pl.program_id(1)
        if comm is not None:
            pl.when((k == 0) & (s == 0))(lambda: comm.start(*cargs))

        @pl.when(s == 0)
        def _():
            o_ref[...] = jnp.zeros_like(o_ref)

        o_ref[...] += _dot_tn(a_ref[...], b_ref[...])
        if comm is not None:
            pl.when((k == nk - 1) & (s == ns - 1))(lambda: comm.finish(*cargs))

    outs = pl.pallas_call(
        body, grid=(nk, ns), name=name,
        in_specs=[pl.BlockSpec((ts, tk), lambda k, s: (s, k)), pl.BlockSpec((ts, N), lambda k, s: (s, 0))] + [ANY] * n,
        out_specs=[pl.BlockSpec((tk, N), lambda k, s: (k, 0))] + [ANY] * n,
        out_shape=[_sds((K, N), F32)] + ([] if comm is None else comm.out_shape),
        scratch_shapes=[] if comm is None else comm.scratch,
        compiler_params=_params(("arbitrary", "arbitrary"), 48),
    )(a, b, *([] if comm is None else comm.ins))
    return outs[0] if comm is None else (outs[0], outs[1:])


def _chip_peers(x, y):
    return [(1 - x, y), (x, 1 - y), (1 - x, 1 - y)]


STAGE_BYTES = 2 << 20


def _chunk_plan(shapes, itemsize):
    plan = []
    for i, (rows, w) in enumerate(shapes):
        ch = max(16, min(rows, (STAGE_BYTES // (w * itemsize)) // 16 * 16))
        while rows % ch:
            ch -= 16
        plan += [(i, r0, ch) for r0 in range(0, rows, ch)]
    return plan


def _remote(src, dst, ssem, rsem, dev):
    return pltpu.make_async_remote_copy(src_ref=src, dst_ref=dst, send_sem=ssem, recv_sem=rsem, device_id=dev,
                                        device_id_type=MESH)


class _Gather:
    def __init__(self, shards):
        self.n = len(shards)
        self.shards = shards
        self.halves = [s.reshape(2, s.shape[0] // 2, s.shape[1]) for s in shards]
        self.plan = _chunk_plan([h.shape[1:] for h in self.halves], 2)
        self.out_shape = [_sds((4,) + h.shape, BF16) for h in self.halves]
        n = self.n
        self.scratch = [pltpu.SemaphoreType.DMA((6 * n,)), pltpu.SemaphoreType.DMA((6 * n,)),
                        pltpu.SemaphoreType.DMA((2,)), pltpu.SemaphoreType.DMA((2,)),
                        pltpu.VMEM((2, max(p[2] for p in self.plan), max(h.shape[2] for h in self.halves)), BF16)]

    def full(self, outs):
        return [o.reshape(4 * s.shape[0], s.shape[1]) for o, s in zip(outs, self.shards)]

    def _sends(self, ins, outs, ssem, rsem):
        x, y, c = lax.axis_index("x"), lax.axis_index("y"), lax.axis_index("c")
        me = 2 * x + y
        return [_remote(ins[i].at[c], outs[i].at[me, c], ssem.at[6 * i + k], rsem.at[6 * i + k], (px, py, c))
                for i in range(self.n) for k, (px, py) in enumerate(_chip_peers(x, y))]

    def start(self, ins, outs, scratch):
        ssem, rsem, lsem, osem, buf = scratch
        me = 2 * lax.axis_index("x") + lax.axis_index("y")
        for cp in self._sends(ins, outs, ssem, rsem):
            cp.start()
        pending = {}
        for i, r0, ch in self.plan:
            for h in range(2):
                if h in pending:
                    pending[h].wait()
                stage = buf.at[h, pl.ds(0, ch), pl.ds(0, self.halves[i].shape[2])]
                ld = pltpu.make_async_copy(ins[i].at[h, pl.ds(r0, ch)], stage, lsem.at[h])
                ld.start()
                ld.wait()
                st = pltpu.make_async_copy(stage, outs[i].at[me, h, pl.ds(r0, ch)], osem.at[h])
                st.start()
                pending[h] = st
        for st in pending.values():
            st.wait()

    def finish(self, ins, outs, scratch):
        ssem, rsem = scratch[:2]
        x, y, c = lax.axis_index("x"), lax.axis_index("y"), lax.axis_index("c")
        chips = _chip_peers(x, y)
        sib = (x, y, 1 - c)
        forwards = []
        for i in range(self.n):
            for k, (px, py) in enumerate(chips):
                landed = outs[i].at[2 * px + py, c]
                _remote(landed, landed, ssem.at[6 * i + k], rsem.at[6 * i + k], (px, py, c)).wait_recv()
                cp = _remote(landed, landed, ssem.at[6 * i + 3 + k], rsem.at[6 * i + 3 + k], sib)
                cp.start()
                forwards.append(cp)
        for i in range(self.n):
            for k, (px, py) in enumerate(chips):
                passed = outs[i].at[2 * px + py, 1 - c]
                _remote(passed, passed, ssem.at[6 * i + 3 + k], rsem.at[6 * i + 3 + k], sib).wait_recv()
        for cp in self._sends(ins, outs, ssem, rsem) + forwards:
            cp.wait_send()


def _gather_weights(shards):
    gt = _Gather(shards)
    n = gt.n

    def body(*refs):
        ins, outs, scratch = refs[:n], refs[n:2 * n], refs[2 * n:]
        gt.start(ins, outs, scratch)
        gt.finish(ins, outs, scratch)

    outs = pl.pallas_call(
        body, name="gather_weights", in_specs=[ANY] * n, out_specs=[ANY] * n, out_shape=gt.out_shape,
        scratch_shapes=gt.scratch, compiler_params=pltpu.CompilerParams(vmem_limit_bytes=32 << 20),
    )(*gt.halves)
    return gt.full(outs)


def _swap_halves(grads):
    n = len(grads)
    g4 = [g.reshape(4, 2, g.shape[0] // 8, g.shape[1]) for g in grads]

    def body(*refs):
        ins, got = refs[:n], refs[n:2 * n]
        ssem, rsem = refs[2 * n:]
        x, y, c = lax.axis_index("x"), lax.axis_index("y"), lax.axis_index("c")
        sib = (x, y, 1 - c)
        cps = []
        for i in range(n):
            rc = _remote(ins[i].at[:, 1 - c], got[i], ssem.at[i], rsem.at[i], sib)
            rc.start()
            cps.append(rc)
        for cp in cps:
            cp.wait()

    half = [_sds((4, g.shape[2], g.shape[3]), F32) for g in g4]
    got = pl.pallas_call(
        body, name="swap_halves", in_specs=[ANY] * n, out_specs=[ANY] * n, out_shape=half,
        scratch_shapes=[pltpu.SemaphoreType.DMA((n,)), pltpu.SemaphoreType.DMA((n,))],
    )(*g4)
    return g4, got


def _chip_sum(g4, got, name):
    _, _, R, W = g4.shape
    tr = _tile(R, max(16, min(512, (1 << 18) // W // 16 * 16)))
    c = lax.axis_index("c").astype(jnp.int32).reshape(1)

    def body(c_ref, a_ref, b_ref, s_ref, sb_ref):
        s = a_ref[...] + b_ref[...]
        s_ref[...] = s
        sb_ref[...] = s.astype(BF16)

    plain = pl.BlockSpec((None, tr, W), lambda j, t, c_ref: (j, t, 0))
    return pl.pallas_call(
        body, name=name,
        grid_spec=pltpu.PrefetchScalarGridSpec(
            num_scalar_prefetch=1, grid=(4, R // tr),
            in_specs=[pl.BlockSpec((None, None, tr, W), lambda j, t, c_ref: (j, c_ref[0], t, 0)), plain],
            out_specs=[plain, plain]),
        out_shape=[_sds((4, R, W), F32), _sds((4, R, W), BF16)],
        compiler_params=_params(("parallel", "parallel"), 32),
    )(c, g4, got)


class _Scatter:
    def __init__(self, sums_b):
        self.n = len(sums_b)
        self.ins = list(sums_b)
        self.out_shape = [_sds((3,) + s.shape[1:], BF16) for s in sums_b]
        self.scratch = [pltpu.SemaphoreType.DMA((3 * self.n,)), pltpu.SemaphoreType.DMA((3 * self.n,))]

    def _copies(self, ins, outs, scratch):
        ssem, rsem = scratch
        x, y, c = lax.axis_index("x"), lax.axis_index("y"), lax.axis_index("c")
        return [_remote(ins[i].at[2 * px + py], outs[i].at[k], ssem.at[3 * i + k], rsem.at[3 * i + k], (px, py, c))
                for i in range(self.n) for k, (px, py) in enumerate(_chip_peers(x, y))]

    def start(self, ins, outs, scratch):
        for cp in self._copies(ins, outs, scratch):
            cp.start()

    def finish(self, ins, outs, scratch):
        for cp in self._copies(ins, outs, scratch):
            cp.wait()


def _scatter_partials(sc):
    n = sc.n

    def body(*refs):
        args = (refs[:n], refs[n:2 * n], refs[2 * n:])
        sc.start(*args)
        sc.finish(*args)

    return pl.pallas_call(
        body, name="scatter_partials", in_specs=[ANY] * n, out_specs=[ANY] * n, out_shape=sc.out_shape,
        scratch_shapes=sc.scratch,
    )(*sc.ins)


class _Reduce:
    def __init__(self, grads, names):
        self.names = names
        g4, got = _swap_halves(grads)
        self.sums, sums_b = [], []
        for nm, g, t in zip(names, g4, got):
            s_, sb_ = _chip_sum(g, t, f"chip_sum_{nm}")
            self.sums.append(s_)
            sums_b.append(sb_)
        self.scatter = _Scatter(sums_b)

    def collect(self, parts):
        return [_mesh_sum(s, p, f"mesh_sum_{nm}") for nm, s, p in zip(self.names, self.sums, parts)]


def _mesh_sum(sums, parts, name):
    _, R, W = sums.shape
    tr = _tile(R, max(16, min(512, (1 << 18) // W // 16 * 16)))
    me = (2 * lax.axis_index("x") + lax.axis_index("y")).astype(jnp.int32).reshape(1)

    def body(me_ref, m_ref, p_ref, o_ref):
        o_ref[...] = m_ref[...] + p_ref[0].astype(F32) + p_ref[1].astype(F32) + p_ref[2].astype(F32)

    return pl.pallas_call(
        body, name=name,
        grid_spec=pltpu.PrefetchScalarGridSpec(
            num_scalar_prefetch=1, grid=(R // tr,),
            in_specs=[pl.BlockSpec((None, tr, W), lambda i, me_ref: (me_ref[0], i, 0)),
                      pl.BlockSpec((3, tr, W), lambda i, me_ref: (0, i, 0))],
            out_specs=pl.BlockSpec((tr, W), lambda i, me_ref: (i, 0))),
        out_shape=_sds((R, W), F32), compiler_params=_params(("parallel",), 32),
    )(me, sums, parts)


def _share_halves(reduced):
    n = len(reduced)
    plan = _chunk_plan([r.shape for r in reduced], 4)
    max_rows = max(p[2] for p in plan)
    max_w = max(r.shape[1] for r in reduced)

    def body(*refs):
        ins, outs = refs[:n], refs[n:2 * n]
        ssem, rsem, lsem, osem, buf = refs[2 * n:]
        x, y, c = lax.axis_index("x"), lax.axis_index("y"), lax.axis_index("c")
        sib = (x, y, 1 - c)
        pending = {}
        for k, (i, r0, ch) in enumerate(plan):
            slot = k % 2
            if slot in pending:
                rc, lc = pending[slot]
                rc.wait_send()
                lc.wait()
            stage = buf.at[slot, pl.ds(0, ch), pl.ds(0, reduced[i].shape[1])]
            ld = pltpu.make_async_copy(ins[i].at[pl.ds(r0, ch)], stage, lsem.at[slot])
            ld.start()
            ld.wait()
            place = outs[i].at[c, pl.ds(r0, ch)]
            rc = _remote(stage, place, ssem.at[slot], rsem.at[i], sib)
            lc = pltpu.make_async_copy(stage, place, osem.at[slot])
            rc.start()
            lc.start()
            pending[slot] = (rc, lc)
        for rc, lc in pending.values():
            rc.wait_send()
            lc.wait()
        for i in range(n):
            theirs = outs[i].at[1 - c]
            _remote(theirs, theirs, ssem.at[0], rsem.at[i], sib).wait_recv()

    outs = pl.pallas_call(
        body, name="share_halves", in_specs=[ANY] * n, out_specs=[ANY] * n,
        out_shape=[_sds((2,) + r.shape, F32) for r in reduced],
        scratch_shapes=[pltpu.SemaphoreType.DMA((2,)), pltpu.SemaphoreType.DMA((n,)), pltpu.SemaphoreType.DMA((2,)),
                        pltpu.SemaphoreType.DMA((2,)), pltpu.VMEM((2, max_rows, max_w), F32)],
        compiler_params=pltpu.CompilerParams(vmem_limit_bytes=32 << 20),
    )(*reduced)
    return [o.reshape(2 * r.shape[0], r.shape[1]) for o, r in zip(outs, reduced)]


def _tile(rows, cap=256):
    t = min(rows, cap) // 16 * 16
    while rows % t:
        t -= 16
    return t


def _elementwise(fn, ins, out_dtypes, name):
    R, W = ins[0].shape
    tr = _tile(R, max(8, min(512, (1 << 18) // W // 8 * 8)))

    def body(*refs):
        outs = fn(*[r[...] for r in refs[:len(ins)]])
        for o_ref, o in zip(refs[len(ins):], outs):
            o_ref[...] = o.astype(o_ref.dtype)

    return pl.pallas_call(
        body, grid=(R // tr,), name=name, in_specs=[_row(tr, W)] * len(ins), out_specs=[_row(tr, W)] * len(out_dtypes),
        out_shape=[_sds((R, W), dt) for dt in out_dtypes],
        compiler_params=_params(("parallel",), 48),
    )(*ins)


def _adamw(w, g, m, v):
    m = B1 * m + (1.0 - B1) * g
    v = B2 * v + (1.0 - B2) * (g * g)
    m_hat = m / (1.0 - B1 ** STEP)
    v_hat = v / (1.0 - B2 ** STEP)
    return -LR * (m_hat / (jnp.sqrt(v_hat) + AEPS) + WD * w), m, v


def _adam_small(ws, ms, vs, parts, loss_part):
    n = len(ws)
    sent = list(parts) + [loss_part]
    ns = n + 1

    def body(*refs):
        w_refs, m_refs, v_refs = refs[:n], refs[n:2 * n], refs[2 * n:3 * n]
        p_refs = refs[3 * n:3 * n + ns]
        outs = refs[3 * n + ns:3 * n + ns + 4 * n + 1]
        g_refs, d_refs, nm_refs, nv_refs, loss_ref = outs[:n], outs[n:2 * n], outs[2 * n:3 * n], outs[3 * n:4 * n], outs[4 * n]
        all_s = refs[3 * n + ns + 4 * n + 1:3 * n + ns + 4 * n + 1 + ns]
        ssem, rsem = refs[-2:]
        x, y, c = lax.axis_index("x"), lax.axis_index("y"), lax.axis_index("c")
        me = 4 * x + 2 * y + c
        for i in range(ns):
            all_s[i][me] = p_refs[i][...]
        cps = []
        for rel in range(1, 8):
            peer = (1 - x if rel & 4 else x, 1 - y if rel & 2 else y, 1 - c if rel & 1 else c)
            for i in range(ns):
                k = (rel - 1) * ns + i
                rc = _remote(p_refs[i], all_s[i].at[me], ssem.at[k], rsem.at[k], peer)
                rc.start()
                cps.append((rc, i, k, 4 * peer[0] + 2 * peer[1] + peer[2]))
        for rc, i, k, peer_slot in cps:
            rc.wait_send()
            _remote(p_refs[i], all_s[i].at[peer_slot], ssem.at[k], rsem.at[k], (x, y, c)).wait_recv()

        def total(i):
            t = all_s[i][0]
            for k in range(1, 8):
                t = t + all_s[i][k]
            return t

        for i in range(n):
            g = total(i)
            g_refs[i][...] = g
            d_refs[i][...], nm_refs[i][...], nv_refs[i][...] = _adamw(w_refs[i][...], g, m_refs[i][...], v_refs[i][...])
        loss_ref[...] = total(n)

    shapes = [_sds(w.shape, F32) for w in ws]
    outs = pl.pallas_call(
        body, name="adam_small", out_shape=shapes * 4 + [_sds(loss_part.shape, F32)],
        scratch_shapes=[pltpu.VMEM((8,) + t.shape, F32) for t in sent]
        + [pltpu.SemaphoreType.DMA((7 * ns,)), pltpu.SemaphoreType.DMA((7 * ns,))],
        compiler_params=pltpu.CompilerParams(vmem_limit_bytes=32 << 20),
    )(*ws, *ms, *vs, *sent)
    return outs[:n], outs[n:2 * n], outs[2 * n:3 * n], outs[3 * n:4 * n], outs[4 * n]


def _local_step(xs, tgt, mems, weights, small, gather_mid=None, gather_ffn=None, reduce=False):
    wt_in, wt_ba, wt_bs, wo, wq, wkv, wt_o, wt_gu, wd = weights
    g_mix, b_gate, w_sgu, b_sgu, g_sgu, g_cross, g_mem, g_ffn, g_final = small
    wt = jnp.tril(w_sgu).astype(BF16)
    bst = b_sgu.T

    (a, qkv0, qkv1, qkv2, uv, gl), got = _fwd_in(xs, g_mix, wt_in, gather_mid)
    if gather_mid is not None:
        wt_ba, wt_bs, wo, wq, wkv, wt_o = got
    qkvs = (qkv0, qkv1, qkv2)
    os_, ls_ = zip(*[_attn_fwd(qkvs[g], g) for g in range(3)])
    (ya, ys, ba, bs, mg, h1), got = _fwd_mid(xs, os_, ls_, uv, gl, wt, bst, g_sgu, b_gate, wt_ba, wt_bs, wo, gather_ffn)
    if gather_ffn is not None:
        wt_gu, wd = got
    mb, kv = _mem_fwd(mems, g_mem, wkv)
    cb, qc, oc, h2 = _fwd_cross(h1, g_cross, wq, kv, wt_o)
    f, act, dgu, dh3b, dh2, dh2b, dg_ffn, dg_final, loss = _ffn_fwd_bwd(h2, tgt, g_ffn, g_final, wt_gu, wd)

    g_ffn_w = [_tn_matmul(dgu, f, "dw_gate_up", 512), _tn_matmul(act, dh3b, "dw_down", 256)]
    r_ffn = _Reduce(g_ffn_w, ["w_gate_up", "w_down"]) if reduce else None
    (dqc, dh1, dh1b, dkv, dg_cross), parts_ffn = _bwd_cross(dh2, h1, qc, g_cross, wq, kv, wt_o,
                                                           r_ffn.scatter if reduce else None)
    dw_kv, dg_mem = _mem_bwd(dkv, mems, mb, g_mem, wkv)
    (dba, dbs, dgl, duv, do0, do1, do2, c0, c1, c2, db_gate, dg_sgu, dws, dbs_acc) = _bwd_mid(
        dh1, gl, ba, bs, uv, ls_, ya, wt, bst, g_sgu, b_gate, wt_ba, wt_bs, wo)
    dqkvs = [_attn_bwd(qkvs[g], do, ls_[g], corr, g) for g, (do, corr) in enumerate(((do0, c0), (do1, c1), (do2, c2)))]
    grad_x, dproj, dg_mix = _bwd_in(dqkvs, duv, dgl, dh1, xs, g_mix, wt_in)
    g_mid_w = [_tn_matmul(dba, ya, "dw_branch_attn", 512),
               _tn_matmul(dbs, ys, "dw_branch_sgu", 512),
               _tn_matmul(mg, dh1b, "dw_out", 512),
               _tn_matmul(cb, dqc, "dw_q_cross", 512),
               dw_kv,
               _tn_matmul(dh2b, oc, "dw_o_cross", 512)]
    small_terms = (dg_mix, db_gate, dws, dbs_acc, dg_sgu, dg_cross, dg_mem, dg_ffn, dg_final)
    if not reduce:
        return loss, grad_x, [_tn_matmul(dproj, a, "dw_in", 768)] + g_mid_w + g_ffn_w, small_terms
    r_mid = _Reduce(g_mid_w, ["w_branch_attn", "w_branch_sgu", "w_out", "w_q_cross", "w_kv_cross", "w_o_cross"])
    g_in, parts_mid = _tn_matmul(dproj, a, "dw_in", 768, comm=r_mid.scatter)
    r_in = _Reduce([g_in], ["w_in"])
    halves = r_in.collect(_scatter_partials(r_in.scatter)) + r_mid.collect(parts_mid) + r_ffn.collect(parts_ffn)
    return loss, grad_x, halves, small_terms


def kernel(x, mem, g_mix, w_in, b_gate, w_sgu_spatial, b_sgu_spatial, g_sgu, w_branch_attn, w_branch_sgu, w_out, g_cross, g_mem, w_q_cross, w_kv_cross, w_o_cross, g_ffn, w_gate_up, w_down, g_final, loss_target, m_g_mix, m_w_in, m_b_gate, m_w_sgu_spatial, m_b_sgu_spatial, m_g_sgu, m_w_branch_attn, m_w_branch_sgu, m_w_out, m_g_cross, m_g_mem, m_w_q_cross, m_w_kv_cross, m_w_o_cross, m_g_ffn, m_w_gate_up, m_w_down, m_g_final, v_g_mix, v_w_in, v_b_gate, v_w_sgu_spatial, v_b_sgu_spatial, v_g_sgu, v_w_branch_attn, v_w_branch_sgu, v_w_out, v_g_cross, v_g_mem, v_w_q_cross, v_w_kv_cross, v_w_o_cross, v_g_ffn, v_w_gate_up, v_w_down, v_g_final):
    S = x.shape[1]
    xs, tgt, mems = x.reshape(S, D), loss_target.reshape(S, D), mem.reshape(mem.shape[1], D)
    g_final2 = g_final.reshape(1, D)

    big = [("w_in", w_in[0], m_w_in[0], v_w_in[0], True),
           ("w_branch_attn", w_branch_attn[0], m_w_branch_attn[0], v_w_branch_attn[0], True),
           ("w_branch_sgu", w_branch_sgu[0], m_w_branch_sgu[0], v_w_branch_sgu[0], True),
           ("w_out", w_out[0], m_w_out[0], v_w_out[0], False),
           ("w_q_cross", w_q_cross[0], m_w_q_cross[0], v_w_q_cross[0], False),
           ("w_kv_cross", w_kv_cross[0], m_w_kv_cross[0], v_w_kv_cross[0], False),
           ("w_o_cross", w_o_cross[0], m_w_o_cross[0], v_w_o_cross[0], True),
           ("w_gate_up", w_gate_up[0], m_w_gate_up[0], v_w_gate_up[0], True),
           ("w_down", w_down[0], m_w_down[0], v_w_down[0], False)]
    shards = [(w.T if tr else w).astype(BF16) for _, w, _, _, tr in big]
    (wt_in,) = _gather_weights(shards[:1])
    (loss, grad_x, reduced, (dg_mix, db_gate, dws, dbs_acc, dg_sgu, dg_cross, dg_mem, dg_ffn, dg_final)) = _local_step(
        xs, tgt, mems, (wt_in,) + (None,) * 8,
        (g_mix, b_gate, w_sgu_spatial[0], b_sgu_spatial[0], g_sgu, g_cross, g_mem, g_ffn, g_final2),
        _Gather(shards[1:7]), _Gather(shards[7:9]), reduce=True)
    full = _share_halves(reduced)

    big_out = {}
    for (name, w, m, v, tr), gsh in zip(big, full):
        gsh = gsh.T if tr else gsh
        delta, nm, nv = _elementwise(_adamw, [w, gsh, m, v], [F32, F32, F32], f"adam_{name}")
        big_out[name] = tuple(t[None] for t in (gsh, delta, nm, nv))

    small = [("g_mix", g_mix, m_g_mix, v_g_mix, dg_mix), ("b_gate", b_gate, m_b_gate, v_b_gate, db_gate),
             ("w_sgu_spatial", w_sgu_spatial, m_w_sgu_spatial, v_w_sgu_spatial, jnp.tril(dws)),
             ("b_sgu_spatial", b_sgu_spatial, m_b_sgu_spatial, v_b_sgu_spatial, jnp.sum(dbs_acc, axis=-1)),
             ("g_sgu", g_sgu, m_g_sgu, v_g_sgu, dg_sgu), ("g_cross", g_cross, m_g_cross, v_g_cross, dg_cross),
             ("g_mem", g_mem, m_g_mem, v_g_mem, dg_mem), ("g_ffn", g_ffn, m_g_ffn, v_g_ffn, dg_ffn),
             ("g_final", g_final, m_g_final, v_g_final, dg_final)]
    as_term = lambda s, t: t.reshape(s[4].shape)
    gs, ds, nms, nvs, loss_all = _adam_small(*[[as_term(s, s[k]) for s in small] for k in (1, 2, 3, 4)], loss)
    small_out = {s[0]: tuple(t[i].reshape(s[1].shape) for t in (gs, ds, nms, nvs)) for i, s in enumerate(small)}
    total_loss = loss_all[0, 0]

    order = ["g_mix", "w_in", "b_gate", "w_sgu_spatial", "b_sgu_spatial", "g_sgu", "w_branch_attn", "w_branch_sgu",
             "w_out", "g_cross", "g_mem", "w_q_cross", "w_kv_cross", "w_o_cross", "g_ffn", "w_gate_up", "w_down",
             "g_final"]
    res = {**big_out, **small_out}
    outs = [total_loss, grad_x.reshape(x.shape)]
    for k in range(4):
        outs += [res[nm][k] for nm in order]
    return tuple(outs)
```

```python
import math

import numpy as np
import jax
import jax.numpy as jnp
from jax import lax
from jax.experimental import pallas as pl
from jax.experimental.pallas import tpu as pltpu

F32, BF16 = jnp.float32, jnp.bfloat16
MESH = pl.DeviceIdType.MESH
ANY = pl.BlockSpec(memory_space=pl.ANY)
RES = pl.BlockSpec(memory_space=pltpu.VMEM)


def _pallas(body, **kw):
    call = pl.pallas_call(body, **kw)
    gs = kw.get("grid_spec")
    specs = kw.get("in_specs") if gs is None else [None] * gs.num_scalar_prefetch + list(gs.in_specs)

    def run(*args):
        if specs is not None:
            args = [a if (s is RES or s is None) else pltpu.with_memory_space_constraint(a, pltpu.HBM)
                    for a, s in zip(args, specs)]
        return call(*args)
    return run


def _whole(arr):
    nd = len(arr.shape)
    return pl.BlockSpec(arr.shape, lambda *_: (0,) * nd)

D = 1024
HEAD = 64
GROUP_W = 256
DIL_GROUPS = ((128, 1), (512, 4), (2048, 16))
BLK = 128
SGU_W = 512
MEM_HEADS, MEM_HD, MEM_W = 4, 128, 512
D_FF = 2816
FF_CHUNK = 256
EPS = 1e-6
NEG = -1e30
LR, B1, B2, AEPS, WD, STEP = 0.001, 0.9, 0.999, 1e-08, 0.01, 10
GELU_K, GELU_C = 0.7978845608028654, 0.044715


def _dot(a, b):
    return jnp.dot(a, b, preferred_element_type=F32)


def _dot_nt(a, b):
    return lax.dot_general(a, b, (((1,), (1,)), ((), ())), preferred_element_type=F32)


def _dot_tn(a, b):
    return lax.dot_general(a, b, (((0,), (0,)), ((), ())), preferred_element_type=F32)


def _row(tm, w):
    return pl.BlockSpec((tm, w), lambda i: (i, 0))


def _acc(shape):
    return pl.BlockSpec(shape, lambda i: (0,) * len(shape))


def _params(sem, mb):
    return pltpu.CompilerParams(dimension_semantics=sem, vmem_limit_bytes=mb << 20)


def _sds(shape, dt):
    return jax.ShapeDtypeStruct(shape, dt)


def _rms(h):
    return lax.rsqrt(jnp.mean(h * h, axis=-1, keepdims=True) + EPS)


def _rms_bwd(dy, h, r, g):
    t = dy * g
    dh = r * t - h * (r * r * r) * jnp.mean(t * h, axis=-1, keepdims=True)
    return dh, dy * h * r


def _gelu(x):
    t = jnp.tanh(GELU_K * (x + GELU_C * x * x * x))
    return 0.5 * x * (1.0 + t), t


def _gelu_grad(x, t):
    return 0.5 * (1.0 + t) + 0.5 * x * (1.0 - t * t) * GELU_K * (1.0 + 3.0 * GELU_C * x * x)


def _alibi_slopes():
    def pow2(n):
        start = 2.0 ** (-8.0 / n)
        return [start ** (i + 1) for i in range(n)]
    n = 12
    c = 2 ** int(math.floor(math.log2(n)))
    s = pow2(c) + pow2(2 * c)[0::2][: n - c]
    return np.array(sorted(s, reverse=True), dtype=np.float32).reshape(3, 4)


def _attn_bias(g):
    win, dil = DIL_GROUPS[g]
    steps = (np.arange(BLK)[:, None] + BLK) - np.arange(2 * BLK)[None, :]
    valid = (steps >= 0) & (steps <= win // dil)
    dist = (np.clip(steps, 0, None) * dil).astype(np.float32)
    b = -_alibi_slopes()[g][:, None, None] * dist[None]
    return np.where(valid[None], b, NEG).astype(np.float32)


def _head_masks():
    lane = lax.broadcasted_iota(jnp.int32, (1, GROUP_W), 1)
    return lane, [(lane >= HEAD * h) & (lane < HEAD * (h + 1)) for h in range(4)]


ATT_NB = 4


def _stack_heads(t, masks):
    z = jnp.zeros_like(t)
    return jnp.concatenate([jnp.where(m, t, z) for m in masks], axis=0)


def _unstack_heads(t, masks):
    out = jnp.zeros((BLK, GROUP_W), t.dtype)
    for h, m in enumerate(masks):
        out = jnp.where(m, t[h * BLK:(h + 1) * BLK], out)
    return out


def _stack_cols(ref, rows):
    return jnp.concatenate([ref[rows, HEAD * h:HEAD * h + 1] for h in range(4)], axis=0)


def _dil_spec(d, tm, w):
    return pl.BlockSpec((d, tm // d, w), lambda i: (0, i, 0))


def _to_dilated(val, s_ref, d, write):
    tm, w = val.shape
    for j in range(w // 128):
        s_ref[j, pl.ds(0, tm), :] = val[:, j * 128:(j + 1) * 128]
    for r in range(d):
        for j in range(w // 128):
            write(r, j, s_ref[j, pl.ds(r, tm // d, stride=d), :])


def _from_dilated(ref, s_ref, d, tm, w):
    if d == 1:
        return ref[0].astype(F32)
    for r in range(d):
        for j in range(w // 128):
            s_ref[j, pl.ds(r, tm // d, stride=d), :] = ref[r, :, j * 128:(j + 1) * 128].astype(F32)
    return jnp.concatenate([s_ref[j, pl.ds(0, tm), :] for j in range(w // 128)], axis=1)


def _fwd_in(x, g_mix, wt_in, gather=None, tm=512):
    S = x.shape[0]
    dils = [d for _, d in DIL_GROUPS]
    n = 0 if gather is None else gather.n
    last = S // tm - 1

    def body(*refs):
        x_ref, g_ref, w_ref = refs[:3]
        a_ref, q0_ref, q1_ref, q2_ref, uv_ref, gl_ref = refs[3 + n:9 + n]
        s_ref = refs[9 + 2 * n]
        comm = (refs[3:3 + n], refs[9 + n:9 + 2 * n], refs[10 + 2 * n:])
        if gather is not None:
            pl.when(pl.program_id(0) == 0)(lambda: gather.start(*comm))
        xv = x_ref[...]
        a = (xv * _rms(xv) * g_ref[...]).astype(BF16)
        a_ref[...] = a
        for g, (d, out) in enumerate(zip(dils, (q0_ref, q1_ref, q2_ref))):
            for part in range(3):
                rows = part * 768 + g * 256
                val = _dot_nt(a, w_ref[rows:rows + 256, :])
                if d == 1:
                    out[0, :, part * 256:(part + 1) * 256] = val.astype(BF16)
                else:
                    def write(r, j, piece, out=out, part=part):
                        out[r, :, part * 256 + j * 128:part * 256 + (j + 1) * 128] = piece.astype(BF16)
                    _to_dilated(val, s_ref, d, write)
        uv_ref[...] = _dot_nt(a, w_ref[2304:3328, :]).astype(BF16)
        gl_ref[...] = _dot_nt(a, w_ref[3328:5376, :]).astype(BF16)
        if gather is not None:
            pl.when(pl.program_id(0) == last)(lambda: gather.finish(*comm))

    outs = _pallas(
        body, grid=(S // tm,), name="fwd_in",
        in_specs=[_row(tm, D), _whole(g_mix), RES] + [ANY] * n,
        out_specs=[_row(tm, D)] + [_dil_spec(d, tm, 768) for d in dils] + [_row(tm, 1024), _row(tm, 2048)] + [ANY] * n,
        out_shape=[_sds((S, D), BF16)] + [_sds((d, S // d, 768), BF16) for d in dils]
        + [_sds((S, 1024), BF16), _sds((S, 2048), BF16)] + ([] if gather is None else gather.out_shape),
        scratch_shapes=[pltpu.VMEM((2, tm, 128), F32)] + ([] if gather is None else gather.scratch),
        compiler_params=_params(("arbitrary",), 60),
    )(x, g_mix, wt_in, *([] if gather is None else gather.halves))
    return outs[:6], ([] if gather is None else gather.full(outs[6:]))


def _attn_fwd(qkv, g):
    d, L, _ = qkv.shape
    nb = L // BLK
    bias = jnp.asarray(_attn_bias(g).reshape(4 * BLK, 2 * BLK))
    NB = min(ATT_NB, nb)
    W = NB * BLK

    def body(q_ref, kc_ref, kp_ref, vc_ref, vp_ref, b_ref, o_ref, l_ref):
        st = pl.program_id(1)
        k_all = jnp.concatenate([kp_ref[...], kc_ref[...]], axis=0)
        v_all = jnp.concatenate([vp_ref[...], vc_ref[...]], axis=0)
        lane, masks = _head_masks()
        for b in range(NB):
            rows = slice(b * BLK, (b + 1) * BLK)
            kk, vv = k_all[b * BLK:(b + 2) * BLK], v_all[b * BLK:(b + 2) * BLK]
            s = _dot_nt(_stack_heads(q_ref[rows, :], masks), kk) * 0.125 + b_ref[...]
            if b == 0:
                s = s + jnp.where((st == 0) & (lane < BLK), NEG, 0.0).astype(F32)
            mx = jnp.max(s, axis=-1, keepdims=True)
            e = jnp.exp(s - mx)
            den = jnp.sum(e, axis=-1, keepdims=True)
            o_ref[rows, :] = _unstack_heads(_dot(e.astype(BF16), vv) / den, masks)
            l_ref[rows, :] = _unstack_heads(mx + jnp.log(den), masks)

    def wide(col):
        return pl.BlockSpec((None, W, GROUP_W), lambda r, s: (r, s, col))

    def before(col):
        return pl.BlockSpec((None, BLK, GROUP_W), lambda r, s: (r, jnp.maximum(s * NB - 1, 0), col))

    return _pallas(
        body, grid=(d, nb // NB), name=f"attn_fwd_g{g}",
        in_specs=[wide(0), wide(1), before(1), wide(2), before(2),
                  pl.BlockSpec((4 * BLK, 2 * BLK), lambda r, s: (0, 0))],
        out_specs=[wide(0), wide(0)],
        out_shape=[_sds((d, L, GROUP_W), F32), _sds((d, L, GROUP_W), F32)],
        compiler_params=_params(("parallel", "parallel"), 32),
    )(qkv, qkv, qkv, qkv, qkv, bias)


def _group_weights(l0, l1, l2):
    m = jnp.maximum(jnp.maximum(l0, l1), l2)
    e0, e1, e2 = jnp.exp(l0 - m), jnp.exp(l1 - m), jnp.exp(l2 - m)
    inv = 1.0 / (e0 + e1 + e2)
    return e0 * inv, e1 * inv, e2 * inv


def _sgu_forward(uvf, gs, wt_ref, bst_ref, mixed_s, tm):
    z, t = _gelu(uvf)
    u, v = z[:, :SGU_W], z[:, SGU_W:]
    rv = _rms(v)
    vnb = (v * rv * gs).astype(BF16)
    for ci in range(tm // 128):
        for g in range(4):
            rs, cs = slice(ci * 128, (ci + 1) * 128), slice(g * 128, (g + 1) * 128)
            mixed_s[rs, cs] = _dot(wt_ref[g], vnb[rs, cs]) + bst_ref[:, g:g + 1]
    return u, v, rv, vnb, t


def _fwd_mid(x, os_, ls_, uv, gl, wt, bst, g_sgu, b_gate, wt_ba, wt_bs, w_out, gather=None, tm=512):
    S = x.shape[0]
    dils = [d for _, d in DIL_GROUPS]
    n = 0 if gather is None else gather.n
    last = S // tm - 1

    def body(*refs):
        (x_ref, o0, o1, o2, l0, l1, l2, uv_ref, gl_ref, wt_ref, bst_ref, gs_ref, bg_ref, wba_ref, wbs_ref,
         wo_ref) = refs[:16]
        ya_ref, ys_ref, ba_ref, bs_ref, mg_ref, h1_ref = refs[16 + n:22 + n]
        mixed_s, il_s = refs[22 + 2 * n:24 + 2 * n]
        comm = (refs[16:16 + n], refs[22 + n:22 + 2 * n], refs[24 + 2 * n:])
        if gather is not None:
            pl.when(pl.program_id(0) == 0)(lambda: gather.start(*comm))
        ls = [_from_dilated(r, il_s, d, tm, GROUP_W) for r, d in zip((l0, l1, l2), dils)]
        alphas = _group_weights(*ls)
        ya = jnp.zeros((tm, GROUP_W), F32)
        for a, r, d in zip(alphas, (o0, o1, o2), dils):
            ya = ya + a * _from_dilated(r, il_s, d, tm, GROUP_W)
        yab = ya.astype(BF16)
        ya_ref[...] = yab
        u, _, _, _, _ = _sgu_forward(uv_ref[...].astype(F32), gs_ref[...], wt_ref, bst_ref, mixed_s, tm)
        ysb = (u * mixed_s[...]).astype(BF16)
        ys_ref[...] = ysb
        gates = jax.nn.sigmoid(gl_ref[...].astype(F32) + bg_ref[...])
        ba = _dot_nt(yab, wba_ref[...])
        bs = _dot_nt(ysb, wbs_ref[...])
        ba_ref[...] = ba.astype(BF16)
        bs_ref[...] = bs.astype(BF16)
        mgb = (gates[:, :D] * ba + gates[:, D:] * bs).astype(BF16)
        mg_ref[...] = mgb
        h1_ref[...] = x_ref[...] + _dot(mgb, wo_ref[...])
        if gather is not None:
            pl.when(pl.program_id(0) == last)(lambda: gather.finish(*comm))

    gw = _row(tm, GROUP_W)
    dil = [_dil_spec(d, tm, GROUP_W) for d in dils]
    outs = _pallas(
        body, grid=(S // tm,), name="fwd_mid",
        in_specs=[_row(tm, D)] + dil + dil + [_row(tm, 1024), _row(tm, 2048)]
        + [_whole(t) for t in (wt, bst, g_sgu, b_gate)] + [RES] * 3 + [ANY] * n,
        out_specs=[gw, _row(tm, SGU_W), _row(tm, D), _row(tm, D), _row(tm, D), _row(tm, D)] + [ANY] * n,
        out_shape=[_sds((S, GROUP_W), BF16), _sds((S, SGU_W), BF16), _sds((S, D), BF16), _sds((S, D), BF16),
                   _sds((S, D), BF16), _sds((S, D), F32)] + ([] if gather is None else gather.out_shape),
        scratch_shapes=[pltpu.VMEM((tm, SGU_W), F32), pltpu.VMEM((2, tm, 128), F32)]
        + ([] if gather is None else gather.scratch),
        compiler_params=_params(("arbitrary",), 56),
    )(x, *os_, *ls_, uv, gl, wt, bst, g_sgu, b_gate, wt_ba, wt_bs, w_out, *([] if gather is None else gather.halves))
    return outs[:6], ([] if gather is None else gather.full(outs[6:]))


def _mem_fwd(mem, g_mem, w_kv):
    def body(m_ref, g_ref, w_ref, mb_ref, kv_ref):
        mv = m_ref[...]
        mb = (mv * _rms(mv) * g_ref[...]).astype(BF16)
        mb_ref[...] = mb
        kv_ref[...] = _dot(mb, w_ref[...]).astype(BF16)

    shapes = [_sds(mem.shape, BF16), _sds((mem.shape[0], 2 * MEM_W), BF16)]
    return _pallas(
        body, name="mem_fwd", grid=(1,), in_specs=[_whole(t) for t in (mem, g_mem, w_kv)],
        out_specs=[_whole(t) for t in shapes], out_shape=shapes, compiler_params=_params(("arbitrary",), 32),
    )(mem, g_mem, w_kv)


def _cross_probs(qh, kh):
    s = _dot_nt(qh, kh) * (MEM_HD ** -0.5)
    e = jnp.exp(s - jnp.max(s, axis=-1, keepdims=True))
    return e / jnp.sum(e, axis=-1, keepdims=True)


def _fwd_cross(h1, g_cross, w_q, kv, wt_o, tm=512):
    S = h1.shape[0]

    def body(h_ref, g_ref, wq_ref, kv_ref, wo_ref, c_ref, qc_ref, oc_ref, h2_ref):
        hv = h_ref[...]
        cb = (hv * _rms(hv) * g_ref[...]).astype(BF16)
        c_ref[...] = cb
        qcb = _dot(cb, wq_ref[...]).astype(BF16)
        qc_ref[...] = qcb
        for h in range(MEM_HEADS):
            cs = slice(h * MEM_HD, (h + 1) * MEM_HD)
            p = _cross_probs(qcb[:, cs], kv_ref[:, cs])
            oc_ref[:, cs] = _dot(p.astype(BF16), kv_ref[:, MEM_W + h * MEM_HD:MEM_W + (h + 1) * MEM_HD]).astype(BF16)
        h2_ref[...] = hv + _dot_nt(oc_ref[...], wo_ref[...])

    return _pallas(
        body, grid=(S // tm,), name="fwd_cross",
        in_specs=[_row(tm, D), _whole(g_cross), RES, _whole(kv), RES],
        out_specs=[_row(tm, D), _row(tm, MEM_W), _row(tm, MEM_W), _row(tm, D)],
        out_shape=[_sds((S, D), BF16), _sds((S, MEM_W), BF16), _sds((S, MEM_W), BF16), _sds((S, D), F32)],
        compiler_params=_params(("parallel",), 40),
    )(h1, g_cross, w_q, kv, wt_o)


def _ffn_fwd_bwd(h2, target, g_ffn, g_final, wt_gu, w_down, tm=256):
    S = h2.shape[0]
    nch = D_FF // FF_CHUNK

    def body(h_ref, t_ref, gf_ref, gz_ref, wgu_ref, wd_ref,
             f_ref, act_ref, dgu_ref, dh3b_ref, dh2_ref, dh2b_ref, dgf_ref, dgz_ref, loss_ref, gu_s):
        i = pl.program_id(0)

        @pl.when(i == 0)
        def _():
            dgf_ref[...] = jnp.zeros_like(dgf_ref)
            dgz_ref[...] = jnp.zeros_like(dgz_ref)
            loss_ref[...] = jnp.zeros_like(loss_ref)

        def weights(c):
            return (wgu_ref.at[pl.ds(c * FF_CHUNK, FF_CHUNK)], wgu_ref.at[pl.ds(D_FF + c * FF_CHUNK, FF_CHUNK)],
                    wd_ref.at[pl.ds(c * FF_CHUNK, FF_CHUNK)])

        hv = h_ref[...]
        r2 = _rms(hv)
        gf = gf_ref[...]
        fb = (hv * r2 * gf).astype(BF16)
        f_ref[...] = fb
        h3 = hv
        for c in range(nch):
            cs = slice(c * FF_CHUNK, (c + 1) * FF_CHUNK)
            us = slice(D_FF + c * FF_CHUNK, D_FF + (c + 1) * FF_CHUNK)
            wg, wu, wd = weights(c)
            gt = _dot_nt(fb, wg[...])
            up = _dot_nt(fb, wu[...])
            gu_s[:, cs] = gt
            gu_s[:, us] = up
            actb = (gt * jax.nn.sigmoid(gt) * up).astype(BF16)
            act_ref[:, cs] = actb
            h3 = h3 + _dot(actb, wd[...])
        r3 = _rms(h3)
        gz = gz_ref[...]
        diff = h3 * r3 * gz - t_ref[...]
        dy = diff * (1.0 / D)
        dh3, dgz_rows = _rms_bwd(dy, h3, r3, gz)
        dh3b = dh3.astype(BF16)
        dh3b_ref[...] = dh3b
        df = jnp.zeros((tm, D), F32)
        for c in range(nch):
            cs = slice(c * FF_CHUNK, (c + 1) * FF_CHUNK)
            us = slice(D_FF + c * FF_CHUNK, D_FF + (c + 1) * FF_CHUNK)
            wg, wu, wd = weights(c)
            dact = _dot_nt(dh3b, wd[...])
            gt, up = gu_s[:, cs], gu_s[:, us]
            sg = jax.nn.sigmoid(gt)
            dgt = (dact * up * (sg * (1.0 + gt * (1.0 - sg)))).astype(BF16)
            dup = (dact * (gt * sg)).astype(BF16)
            dgu_ref[:, cs] = dgt
            dgu_ref[:, us] = dup
            df = df + _dot(dgt, wg[...]) + _dot(dup, wu[...])
        dhn, dgf_rows = _rms_bwd(df, hv, r2, gf)
        dh2 = dh3 + dhn
        dh2_ref[...] = dh2
        dh2b_ref[...] = dh2.astype(BF16)
        dgf_ref[...] += jnp.sum(dgf_rows, axis=0, keepdims=True)
        dgz_ref[...] += jnp.sum(dgz_rows, axis=0, keepdims=True)
        loss_ref[...] += jnp.sum(jnp.sum(diff * diff, axis=0, keepdims=True), axis=1, keepdims=True) * (0.5 / D)

    return _pallas(
        body, grid=(S // tm,), name="ffn_fwd_bwd",
        in_specs=[_row(tm, D), _row(tm, D), _whole(g_ffn), _whole(g_final), RES, RES],
        out_specs=[_row(tm, D), _row(tm, D_FF), _row(tm, 2 * D_FF), _row(tm, D), _row(tm, D), _row(tm, D),
                   _acc((1, D)), _acc((1, D)), _acc((1, 128))],
        out_shape=[_sds((S, D), BF16), _sds((S, D_FF), BF16), _sds((S, 2 * D_FF), BF16), _sds((S, D), BF16),
                   _sds((S, D), F32), _sds((S, D), BF16), _sds((1, D), F32), _sds((1, D), F32), _sds((1, 128), F32)],
        scratch_shapes=[pltpu.VMEM((tm, 2 * D_FF), F32)],
        compiler_params=_params(("arbitrary",), 56),
    )(h2, target, g_ffn, g_final, wt_gu, w_down)


def _bwd_cross(dh2, h1, qc, g_cross, w_q, kv, wt_o, comm=None, tm=512):
    S = h1.shape[0]
    n = 0 if comm is None else comm.n
    last = S // tm - 1

    def body(*refs):
        d_ref, h_ref, qc_ref, g_ref, wq_ref, kv_ref, wo_ref = refs[:7]
        dqc_ref, dh1_ref, dh1b_ref, dkv_ref, dg_ref = refs[7 + n:12 + n]
        cargs = (refs[7:7 + n], refs[12 + n:12 + 2 * n], refs[12 + 2 * n:])
        i = pl.program_id(0)

        @pl.when(i == 0)
        def _():
            dkv_ref[...] = jnp.zeros_like(dkv_ref)
            dg_ref[...] = jnp.zeros_like(dg_ref)
            if comm is not None:
                comm.start(*cargs)

        dh2 = d_ref[...]
        doc = _dot(dh2.astype(BF16), wo_ref[...])
        qcb = qc_ref[...]
        for h in range(MEM_HEADS):
            cs = slice(h * MEM_HD, (h + 1) * MEM_HD)
            vs = slice(MEM_W + h * MEM_HD, MEM_W + (h + 1) * MEM_HD)
            qh, kh, vh = qcb[:, cs], kv_ref[:, cs], kv_ref[:, vs]
            p = _cross_probs(qh, kh)
            dohb = doc[:, cs].astype(BF16)
            dp = _dot_nt(dohb, vh)
            dsb = (p * (dp - jnp.sum(dp * p, axis=-1, keepdims=True)) * (MEM_HD ** -0.5)).astype(BF16)
            dqc_ref[:, cs] = _dot(dsb, kh).astype(BF16)
            dkv_ref[:, cs] += _dot_tn(dsb, qh)
            dkv_ref[:, vs] += _dot_tn(p.astype(BF16), dohb)
        dc = _dot_nt(dqc_ref[...], wq_ref[...])
        hv = h_ref[...]
        dhn, dg_rows = _rms_bwd(dc, hv, _rms(hv), g_ref[...])
        dh1 = dh2 + dhn
        dh1_ref[...] = dh1
        dh1b_ref[...] = dh1.astype(BF16)
        dg_ref[...] += jnp.sum(dg_rows, axis=0, keepdims=True)
        if comm is not None:
            pl.when(i == last)(lambda: comm.finish(*cargs))

    outs = _pallas(
        body, grid=(S // tm,), name="bwd_cross",
        in_specs=[_row(tm, D), _row(tm, D), _row(tm, MEM_W), _whole(g_cross), RES, _whole(kv), RES] + [ANY] * n,
        out_specs=[_row(tm, MEM_W), _row(tm, D), _row(tm, D), _acc((256, 2 * MEM_W)), _acc((1, D))] + [ANY] * n,
        out_shape=[_sds((S, MEM_W), BF16), _sds((S, D), F32), _sds((S, D), BF16), _sds((256, 2 * MEM_W), F32),
                   _sds((1, D), F32)] + ([] if comm is None else comm.out_shape),
        scratch_shapes=[] if comm is None else comm.scratch,
        compiler_params=_params(("arbitrary",), 48),
    )(dh2, h1, qc, g_cross, w_q, kv, wt_o, *([] if comm is None else comm.ins))
    return outs[:5], outs[5:]


def _mem_bwd(dkv, mem, mb, g_mem, w_kv):
    def body(dkv_ref, m_ref, mb_ref, g_ref, w_ref, dw_ref, dg_ref):
        dkvb = dkv_ref[...].astype(BF16)
        dw_ref[...] = _dot_tn(mb_ref[...], dkvb)
        dm = _dot_nt(dkvb, w_ref[...])
        mv = m_ref[...]
        dg_ref[...] = jnp.sum(dm * mv * _rms(mv), axis=0, keepdims=True)

    shapes = [_sds((D, 2 * MEM_W), F32), _sds((1, D), F32)]
    return _pallas(
        body, name="mem_bwd", grid=(1,), in_specs=[_whole(t) for t in (dkv, mem, mb, g_mem, w_kv)],
        out_specs=[_whole(t) for t in shapes], out_shape=shapes, compiler_params=_params(("arbitrary",), 40),
    )(dkv, mem, mb, g_mem, w_kv)


def _bwd_mid(dh1, gl, ba, bs, uv, ls_, ya, wt, bst, g_sgu, b_gate, wt_ba, wt_bs, w_out, tm=512):
    S = dh1.shape[0]
    dils = [d for _, d in DIL_GROUPS]

    def body(d_ref, gl_ref, ba_ref, bs_ref, uv_ref, l0, l1, l2, ya_ref,
             wt_ref, bst_ref, gs_ref, bg_ref, wba_ref, wbs_ref, wo_ref,
             dba_ref, dbs_ref, dgl_ref, duv_ref, do0, do1, do2, c0, c1, c2,
             dbg_ref, dgs_ref, dws_ref, dbsa_ref, mixed_s, dvn_s, il_s):
        i = pl.program_id(0)

        @pl.when(i == 0)
        def _():
            for r in (dbg_ref, dgs_ref, dws_ref, dbsa_ref):
                r[...] = jnp.zeros_like(r)

        dm = _dot_nt(d_ref[...].astype(BF16), wo_ref[...])
        gates = jax.nn.sigmoid(gl_ref[...].astype(F32) + bg_ref[...])
        g0, g1 = gates[:, :D], gates[:, D:]
        dbab = (dm * g0).astype(BF16)
        dbsb = (dm * g1).astype(BF16)
        dba_ref[...] = dbab
        dbs_ref[...] = dbsb
        dg0 = dm * ba_ref[...].astype(F32) * g0 * (1.0 - g0)
        dg1 = dm * bs_ref[...].astype(F32) * g1 * (1.0 - g1)
        dgl_ref[:, :D] = dg0.astype(BF16)
        dgl_ref[:, D:] = dg1.astype(BF16)
        dbg_ref[:, :D] += jnp.sum(dg0, axis=0, keepdims=True)
        dbg_ref[:, D:] += jnp.sum(dg1, axis=0, keepdims=True)
        dya = _dot(dbab, wba_ref[...])
        dys = _dot(dbsb, wbs_ref[...])

        uvf = uv_ref[...].astype(F32)
        gs = gs_ref[...]
        u, v, rv, vnb, t = _sgu_forward(uvf, gs, wt_ref, bst_ref, mixed_s, tm)
        du = dys * mixed_s[...]
        dmixed = dys * u
        for ci in range(tm // 128):
            for g in range(4):
                rs, cs = slice(ci * 128, (ci + 1) * 128), slice(g * 128, (g + 1) * 128)
                dmx = dmixed[rs, cs]
                dmxb = dmx.astype(BF16)
                dvn_s[rs, cs] = _dot_tn(wt_ref[g], dmxb)
                dws_ref[g] += _dot_nt(dmxb, vnb[rs, cs])
                dbsa_ref[g] += dmx
        dv, dgs_rows = _rms_bwd(dvn_s[...], v, rv, gs)
        dgs_ref[...] += jnp.sum(dgs_rows, axis=0, keepdims=True)
        gg = _gelu_grad(uvf, t)
        duv_ref[:, :SGU_W] = (du * gg[:, :SGU_W]).astype(BF16)
        duv_ref[:, SGU_W:] = (dv * gg[:, SGU_W:]).astype(BF16)

        alphas = _group_weights(*[_from_dilated(r, il_s, d, tm, GROUP_W) for r, d in zip((l0, l1, l2), dils)])
        prod = dya * ya_ref[...].astype(F32)
        _, masks = _head_masks()
        hs = jnp.zeros_like(prod)
        for h in range(4):
            sh = jnp.sum(jnp.where(masks[h], prod, 0.0), axis=-1, keepdims=True)
            hs = jnp.where(masks[h], sh, hs)
        for a, d, do_ref, c_ref in zip(alphas, dils, (do0, do1, do2), (c0, c1, c2)):
            for val, out in ((a * dya, do_ref), (a * hs, c_ref)):
                if d == 1:
                    out[0] = val.astype(out.dtype)
                else:
                    def write(r, j, piece, out=out):
                        out[r, :, j * 128:(j + 1) * 128] = piece.astype(out.dtype)
                    _to_dilated(val, il_s, d, write)

    gw = _row(tm, GROUP_W)
    dil = [_dil_spec(d, tm, GROUP_W) for d in dils]
    return _pallas(
        body, grid=(S // tm,), name="bwd_mid",
        in_specs=[_row(tm, D), _row(tm, 2048), _row(tm, D), _row(tm, D), _row(tm, 1024)] + dil + [gw]
        + [_whole(t) for t in (wt, bst, g_sgu, b_gate)] + [RES] * 3,
        out_specs=[_row(tm, D), _row(tm, D), _row(tm, 2048), _row(tm, 1024)] + dil + dil
        + [_acc((1, 2048)), _acc((1, SGU_W)), _acc((4, 128, 128)), _acc((4, 128, 128))],
        out_shape=[_sds((S, D), BF16), _sds((S, D), BF16), _sds((S, 2048), BF16), _sds((S, 1024), BF16)]
        + [_sds((d, S // d, GROUP_W), BF16) for d in dils] + [_sds((d, S // d, GROUP_W), F32) for d in dils]
        + [_sds((1, 2048), F32), _sds((1, SGU_W), F32), _sds((4, 128, 128), F32), _sds((4, 128, 128), F32)],
        scratch_shapes=[pltpu.VMEM((tm, SGU_W), F32), pltpu.VMEM((tm, SGU_W), F32), pltpu.VMEM((2, tm, 128), F32)],
        compiler_params=_params(("arbitrary",), 60),
    )(dh1, gl, ba, bs, uv, *ls_, ya, wt, bst, g_sgu, b_gate, wt_ba, wt_bs, w_out)


def _attn_bwd(qkv, do, lse, corr, g):
    d, L, _ = qkv.shape
    nb = L // BLK
    NB = min(ATT_NB, nb)
    W = NB * BLK
    nsteps = nb // NB
    bias = jnp.asarray(_attn_bias(g).reshape(4 * BLK, 2 * BLK))

    def body(q_ref, kc_ref, kp_ref, vc_ref, vp_ref, do_ref, l_ref, c_ref, qn_ref, don_ref, ln_ref, cn_ref, b_ref,
             out_ref, dk_s, dv_s):
        st = pl.program_id(1)
        k_all = jnp.concatenate([kp_ref[...], kc_ref[...]], axis=0)
        v_all = jnp.concatenate([vp_ref[...], vc_ref[...]], axis=0)
        lane, masks = _head_masks()
        dk_s[...] = jnp.zeros_like(dk_s)
        dv_s[...] = jnp.zeros_like(dv_s)

        def block_terms(qs, dos, kk, vv, bias_v, lse_c, corr_c):
            s = _dot_nt(qs, kk) * 0.125 + bias_v
            p = jnp.exp(s - lse_c)
            dsb = (p * (_dot_nt(dos, vv) - corr_c) * 0.125).astype(BF16)
            return dsb, p.astype(BF16)

        for b in range(NB):
            rows = slice(b * BLK, (b + 1) * BLK)
            keys = slice(b * BLK, (b + 2) * BLK)
            kk, vv = k_all[keys], v_all[keys]
            qs, dos = _stack_heads(q_ref[rows, :], masks), _stack_heads(do_ref[rows, :], masks)
            bias_v = b_ref[...]
            if b == 0:
                bias_v = bias_v + jnp.where((st == 0) & (lane < BLK), NEG, 0.0).astype(F32)
            dsb, pb = block_terms(qs, dos, kk, vv, bias_v, _stack_cols(l_ref, rows), _stack_cols(c_ref, rows))
            out_ref[rows, 0:GROUP_W] = _unstack_heads(_dot(dsb, kk), masks).astype(BF16)
            dk_s[keys, :] += _dot_tn(dsb, qs)
            dv_s[keys, :] += _dot_tn(pb, dos)

        @pl.when(st < nsteps - 1)
        def _():
            last = slice(NB * BLK, (NB + 1) * BLK)
            qs, dos = _stack_heads(qn_ref[...], masks), _stack_heads(don_ref[...], masks)
            every = slice(None)
            dsb, pb = block_terms(qs, dos, k_all[last], v_all[last], b_ref[:, :BLK],
                                  _stack_cols(ln_ref, every), _stack_cols(cn_ref, every))
            dk_s[last, :] += _dot_tn(dsb, qs)
            dv_s[last, :] += _dot_tn(pb, dos)

        out_ref[:, GROUP_W:2 * GROUP_W] = dk_s[BLK:, :].astype(BF16)
        out_ref[:, 2 * GROUP_W:] = dv_s[BLK:, :].astype(BF16)

    def wide(col, w=GROUP_W):
        return pl.BlockSpec((None, W, w), lambda r, s: (r, s, col))

    def before(col):
        return pl.BlockSpec((None, BLK, GROUP_W), lambda r, s: (r, jnp.maximum(s * NB - 1, 0), col))

    def after(col):
        return pl.BlockSpec((None, BLK, GROUP_W), lambda r, s: (r, jnp.minimum((s + 1) * NB, nb - 1), col))

    return _pallas(
        body, grid=(d, nsteps), name=f"attn_bwd_g{g}",
        in_specs=[wide(0), wide(1), before(1), wide(2), before(2), wide(0), wide(0), wide(0),
                  after(0), after(0), after(0), after(0), pl.BlockSpec((4 * BLK, 2 * BLK), lambda r, s: (0, 0))],
        out_specs=wide(0, 768),
        out_shape=_sds((d, L, 768), BF16),
        scratch_shapes=[pltpu.VMEM(((NB + 1) * BLK, GROUP_W), F32), pltpu.VMEM(((NB + 1) * BLK, GROUP_W), F32)],
        compiler_params=_params(("parallel", "parallel"), 32),
    )(qkv, qkv, qkv, qkv, qkv, do, lse, corr, qkv, do, lse, corr, bias)


def _bwd_in(dqkvs, duv, dgl, dh1, x, g_mix, wt_in, tm=512):
    S = x.shape[0]
    dils = [d for _, d in DIL_GROUPS]

    def body(q0_ref, q1_ref, q2_ref, duv_ref, dgl_ref, d_ref, x_ref, g_ref, w_ref, dx_ref, dp_ref, dg_ref, il_s):
        i = pl.program_id(0)

        @pl.when(i == 0)
        def _():
            dg_ref[...] = jnp.zeros_like(dg_ref)

        for g, (d, ref) in enumerate(zip(dils, (q0_ref, q1_ref, q2_ref))):
            nat = _from_dilated(ref, il_s, d, tm, 768).astype(BF16)
            for part in range(3):
                col = part * 768 + g * 256
                dp_ref[:, col:col + 256] = nat[:, part * 256:(part + 1) * 256]
        dp_ref[:, 2304:3328] = duv_ref[...]
        dp_ref[:, 3328:5376] = dgl_ref[...]
        da = _dot(dp_ref[...], w_ref[...])
        xv = x_ref[...]
        dxn, dg_rows = _rms_bwd(da, xv, _rms(xv), g_ref[...])
        dx_ref[...] = d_ref[...] + dxn
        dg_ref[...] += jnp.sum(dg_rows, axis=0, keepdims=True)

    return _pallas(
        body, grid=(S // tm,), name="bwd_in",
        in_specs=[_dil_spec(d, tm, 768) for d in dils] + [_row(tm, 1024), _row(tm, 2048), _row(tm, D), _row(tm, D),
                                                          _whole(g_mix), RES],
        out_specs=[_row(tm, D), _row(tm, 5376), _acc((1, D))],
        out_shape=[_sds((S, D), F32), _sds((S, 5376), BF16), _sds((1, D), F32)],
        scratch_shapes=[pltpu.VMEM((6, tm, 128), F32)],
        compiler_params=_params(("arbitrary",), 60),
    )(*dqkvs, duv, dgl, dh1, x, g_mix, wt_in)


def _tn_matmul(a, b, name, tk, ts=2048, comm=None):
    S, K = a.shape
    N = b.shape[1]
    n = 0 if comm is None else comm.n
    nk, ns = K // tk, S // ts

    def body(*refs):
        a_ref, b_ref, o_ref = refs[0], refs[1], refs[2 + n]
        cargs = (refs[2:2 + n], refs[3 + n:3 + 2 * n], refs[3 + 2 * n:])
        k, s = pl.program_id(0), pl.program_id(1)
        if comm is not None:
            pl.when((k == 0) & (s == 0))(lambda: comm.start(*cargs))

        @pl.when(s == 0)
        def _():
            o_ref[...] = jnp.zeros_like(o_ref)

        o_ref[...] += _dot_tn(a_ref[...], b_ref[...])
        if comm is not None:
            pl.when((k == nk - 1) & (s == ns - 1))(lambda: comm.finish(*cargs))

    outs = _pallas(
        body, grid=(nk, ns), name=name,
        in_specs=[pl.BlockSpec((ts, tk), lambda k, s: (s, k)), pl.BlockSpec((ts, N), lambda k, s: (s, 0))] + [ANY] * n,
        out_specs=[pl.BlockSpec((tk, N), lambda k, s: (k, 0))] + [ANY] * n,
        out_shape=[_sds((K, N), F32)] + ([] if comm is None else comm.out_shape),
        scratch_shapes=[] if comm is None else comm.scratch,
        compiler_params=_params(("arbitrary", "arbitrary"), 48),
    )(a, b, *([] if comm is None else comm.ins))
    return outs[0] if comm is None else (outs[0], outs[1:])


def _chip_peers(x, y):
    return [(1 - x, y), (x, 1 - y), (1 - x, 1 - y)]


STAGE_BYTES = 2 << 20


def _chunk_plan(shapes, itemsize):
    plan = []
    for i, (rows, w) in enumerate(shapes):
        ch = max(16, min(rows, (STAGE_BYTES // (w * itemsize)) // 16 * 16))
        while rows % ch:
            ch -= 16
        plan += [(i, r0, ch) for r0 in range(0, rows, ch)]
    return plan


def _remote(src, dst, ssem, rsem, dev):
    return pltpu.make_async_remote_copy(src_ref=src, dst_ref=dst, send_sem=ssem, recv_sem=rsem, device_id=dev,
                                        device_id_type=MESH)


class _Gather:
    def __init__(self, shards):
        self.n = len(shards)
        self.shards = shards
        self.halves = [s.reshape(2, s.shape[0] // 2, s.shape[1]) for s in shards]
        self.plan = _chunk_plan([h.shape[1:] for h in self.halves], 2)
        self.out_shape = [_sds((4,) + h.shape, BF16) for h in self.halves]
        n = self.n
        self.scratch = [pltpu.SemaphoreType.DMA((6 * n,)), pltpu.SemaphoreType.DMA((6 * n,)),
                        pltpu.SemaphoreType.DMA((2,)), pltpu.SemaphoreType.DMA((2,)),
                        pltpu.VMEM((2, max(p[2] for p in self.plan), max(h.shape[2] for h in self.halves)), BF16)]

    def full(self, outs):
        return [o.reshape(4 * s.shape[0], s.shape[1]) for o, s in zip(outs, self.shards)]

    def _sends(self, ins, outs, ssem, rsem):
        x, y, c = lax.axis_index("x"), lax.axis_index("y"), lax.axis_index("c")
        me = 2 * x + y
        return [_remote(ins[i].at[c], outs[i].at[me, c], ssem.at[6 * i + k], rsem.at[6 * i + k], (px, py, c))
                for i in range(self.n) for k, (px, py) in enumerate(_chip_peers(x, y))]

    def start(self, ins, outs, scratch):
        ssem, rsem, lsem, osem, buf = scratch
        me = 2 * lax.axis_index("x") + lax.axis_index("y")
        for cp in self._sends(ins, outs, ssem, rsem):
            cp.start()
        pending = {}
        for i, r0, ch in self.plan:
            for h in range(2):
                if h in pending:
                    pending[h].wait()
                stage = buf.at[h, pl.ds(0, ch), pl.ds(0, self.halves[i].shape[2])]
                ld = pltpu.make_async_copy(ins[i].at[h, pl.ds(r0, ch)], stage, lsem.at[h])
                ld.start()
                ld.wait()
                st = pltpu.make_async_copy(stage, outs[i].at[me, h, pl.ds(r0, ch)], osem.at[h])
                st.start()
                pending[h] = st
        for st in pending.values():
            st.wait()

    def finish(self, ins, outs, scratch):
        ssem, rsem = scratch[:2]
        x, y, c = lax.axis_index("x"), lax.axis_index("y"), lax.axis_index("c")
        chips = _chip_peers(x, y)
        sib = (x, y, 1 - c)
        forwards = []
        for i in range(self.n):
            for k, (px, py) in enumerate(chips):
                landed = outs[i].at[2 * px + py, c]
                _remote(landed, landed, ssem.at[6 * i + k], rsem.at[6 * i + k], (px, py, c)).wait_recv()
                cp = _remote(landed, landed, ssem.at[6 * i + 3 + k], rsem.at[6 * i + 3 + k], sib)
                cp.start()
                forwards.append(cp)
        for i in range(self.n):
            for k, (px, py) in enumerate(chips):
                passed = outs[i].at[2 * px + py, 1 - c]
                _remote(passed, passed, ssem.at[6 * i + 3 + k], rsem.at[6 * i + 3 + k], sib).wait_recv()
        for cp in self._sends(ins, outs, ssem, rsem) + forwards:
            cp.wait_send()


def _gather_weights(shards):
    gt = _Gather(shards)
    n = gt.n

    def body(*refs):
        ins, outs, scratch = refs[:n], refs[n:2 * n], refs[2 * n:]
        gt.start(ins, outs, scratch)
        gt.finish(ins, outs, scratch)

    outs = _pallas(
        body, name="gather_weights", in_specs=[ANY] * n, out_specs=[ANY] * n, out_shape=gt.out_shape,
        scratch_shapes=gt.scratch, compiler_params=pltpu.CompilerParams(vmem_limit_bytes=32 << 20),
    )(*gt.halves)
    return gt.full(outs)


def _swap_halves(grads):
    n = len(grads)
    g4 = [g.reshape(4, 2, g.shape[0] // 8, g.shape[1]) for g in grads]

    def body(*refs):
        ins, got = refs[:n], refs[n:2 * n]
        ssem, rsem = refs[2 * n:]
        x, y, c = lax.axis_index("x"), lax.axis_index("y"), lax.axis_index("c")
        sib = (x, y, 1 - c)
        cps = []
        for i in range(n):
            rc = _remote(ins[i].at[:, 1 - c], got[i], ssem.at[i], rsem.at[i], sib)
            rc.start()
            cps.append(rc)
        for cp in cps:
            cp.wait()

    half = [_sds((4, g.shape[2], g.shape[3]), F32) for g in g4]
    got = _pallas(
        body, name="swap_halves", in_specs=[ANY] * n, out_specs=[ANY] * n, out_shape=half,
        scratch_shapes=[pltpu.SemaphoreType.DMA((n,)), pltpu.SemaphoreType.DMA((n,))],
    )(*g4)
    return g4, got


def _chip_sum(g4, got, name):
    _, _, R, W = g4.shape
    tr = _tile(R, max(16, min(512, (1 << 18) // W // 16 * 16)))
    c = lax.axis_index("c").astype(jnp.int32).reshape(1)

    def body(c_ref, a_ref, b_ref, s_ref, sb_ref):
        s = a_ref[...] + b_ref[...]
        s_ref[...] = s
        sb_ref[...] = s.astype(BF16)

    plain = pl.BlockSpec((None, tr, W), lambda j, t, c_ref: (j, t, 0))
    return _pallas(
        body, name=name,
        grid_spec=pltpu.PrefetchScalarGridSpec(
            num_scalar_prefetch=1, grid=(4, R // tr),
            in_specs=[pl.BlockSpec((None, None, tr, W), lambda j, t, c_ref: (j, c_ref[0], t, 0)), plain],
            out_specs=[plain, plain]),
        out_shape=[_sds((4, R, W), F32), _sds((4, R, W), BF16)],
        compiler_params=_params(("parallel", "parallel"), 32),
    )(c, g4, got)


class _Scatter:
    def __init__(self, sums_b):
        self.n = len(sums_b)
        self.ins = list(sums_b)
        self.out_shape = [_sds((3,) + s.shape[1:], BF16) for s in sums_b]
        self.scratch = [pltpu.SemaphoreType.DMA((3 * self.n,)), pltpu.SemaphoreType.DMA((3 * self.n,))]

    def _copies(self, ins, outs, scratch):
        ssem, rsem = scratch
        x, y, c = lax.axis_index("x"), lax.axis_index("y"), lax.axis_index("c")
        return [_remote(ins[i].at[2 * px + py], outs[i].at[k], ssem.at[3 * i + k], rsem.at[3 * i + k], (px, py, c))
                for i in range(self.n) for k, (px, py) in enumerate(_chip_peers(x, y))]

    def start(self, ins, outs, scratch):
        for cp in self._copies(ins, outs, scratch):
            cp.start()

    def finish(self, ins, outs, scratch):
        for cp in self._copies(ins, outs, scratch):
            cp.wait()


def _scatter_partials(sc):
    n = sc.n

    def body(*refs):
        args = (refs[:n], refs[n:2 * n], refs[2 * n:])
        sc.start(*args)
        sc.finish(*args)

    return _pallas(
        body, name="scatter_partials", in_specs=[ANY] * n, out_specs=[ANY] * n, out_shape=sc.out_shape,
        scratch_shapes=sc.scratch,
    )(*sc.ins)


class _Reduce:
    def __init__(self, grads, names):
        self.names = names
        g4, got = _swap_halves(grads)
        self.sums, sums_b = [], []
        for nm, g, t in zip(names, g4, got):
            s_, sb_ = _chip_sum(g, t, f"chip_sum_{nm}")
            self.sums.append(s_)
            sums_b.append(sb_)
        self.scatter = _Scatter(sums_b)

    def collect(self, parts):
        return [_mesh_sum(s, p, f"mesh_sum_{nm}") for nm, s, p in zip(self.names, self.sums, parts)]


def _mesh_sum(sums, parts, name):
    _, R, W = sums.shape
    tr = _tile(R, max(16, min(512, (1 << 18) // W // 16 * 16)))
    me = (2 * lax.axis_index("x") + lax.axis_index("y")).astype(jnp.int32).reshape(1)

    def body(me_ref, m_ref, p_ref, o_ref):
        o_ref[...] = m_ref[...] + p_ref[0].astype(F32) + p_ref[1].astype(F32) + p_ref[2].astype(F32)

    return _pallas(
        body, name=name,
        grid_spec=pltpu.PrefetchScalarGridSpec(
            num_scalar_prefetch=1, grid=(R // tr,),
            in_specs=[pl.BlockSpec((None, tr, W), lambda i, me_ref: (me_ref[0], i, 0)),
                      pl.BlockSpec((3, tr, W), lambda i, me_ref: (0, i, 0))],
            out_specs=pl.BlockSpec((tr, W), lambda i, me_ref: (i, 0))),
        out_shape=_sds((R, W), F32), compiler_params=_params(("parallel",), 32),
    )(me, sums, parts)


def _share_halves(reduced):
    n = len(reduced)
    plan = _chunk_plan([r.shape for r in reduced], 4)
    max_rows = max(p[2] for p in plan)
    max_w = max(r.shape[1] for r in reduced)

    def body(*refs):
        ins, outs = refs[:n], refs[n:2 * n]
        ssem, rsem, lsem, osem, buf = refs[2 * n:]
        x, y, c = lax.axis_index("x"), lax.axis_index("y"), lax.axis_index("c")
        sib = (x, y, 1 - c)
        pending = {}
        for k, (i, r0, ch) in enumerate(plan):
            slot = k % 2
            if slot in pending:
                rc, lc = pending[slot]
                rc.wait_send()
                lc.wait()
            stage = buf.at[slot, pl.ds(0, ch), pl.ds(0, reduced[i].shape[1])]
            ld = pltpu.make_async_copy(ins[i].at[pl.ds(r0, ch)], stage, lsem.at[slot])
            ld.start()
            ld.wait()
            place = outs[i].at[c, pl.ds(r0, ch)]
            rc = _remote(stage, place, ssem.at[slot], rsem.at[i], sib)
            lc = pltpu.make_async_copy(stage, place, osem.at[slot])
            rc.start()
            lc.start()
            pending[slot] = (rc, lc)
        for rc, lc in pending.values():
            rc.wait_send()
            lc.wait()
        for i in range(n):
            theirs = outs[i].at[1 - c]
            _remote(theirs, theirs, ssem.at[0], rsem.at[i], sib).wait_recv()

    outs = _pallas(
        body, name="share_halves", in_specs=[ANY] * n, out_specs=[ANY] * n,
        out_shape=[_sds((2,) + r.shape, F32) for r in reduced],
        scratch_shapes=[pltpu.SemaphoreType.DMA((2,)), pltpu.SemaphoreType.DMA((n,)), pltpu.SemaphoreType.DMA((2,)),
                        pltpu.SemaphoreType.DMA((2,)), pltpu.VMEM((2, max_rows, max_w), F32)],
        compiler_params=pltpu.CompilerParams(vmem_limit_bytes=32 << 20),
    )(*reduced)
    return [o.reshape(2 * r.shape[0], r.shape[1]) for o, r in zip(outs, reduced)]


def _tile(rows, cap=256):
    t = min(rows, cap) // 16 * 16
    while rows % t:
        t -= 16
    return t


def _elementwise(fn, ins, out_dtypes, name):
    R, W = ins[0].shape
    tr = _tile(R, max(8, min(512, (1 << 18) // W // 8 * 8)))

    def body(*refs):
        outs = fn(*[r[...] for r in refs[:len(ins)]])
        for o_ref, o in zip(refs[len(ins):], outs):
            o_ref[...] = o.astype(o_ref.dtype)

    return _pallas(
        body, grid=(R // tr,), name=name, in_specs=[_row(tr, W)] * len(ins), out_specs=[_row(tr, W)] * len(out_dtypes),
        out_shape=[_sds((R, W), dt) for dt in out_dtypes],
        compiler_params=_params(("parallel",), 48),
    )(*ins)


def _adamw(w, g, m, v):
    m = B1 * m + (1.0 - B1) * g
    v = B2 * v + (1.0 - B2) * (g * g)
    m_hat = m / (1.0 - B1 ** STEP)
    v_hat = v / (1.0 - B2 ** STEP)
    return -LR * (m_hat / (jnp.sqrt(v_hat) + AEPS) + WD * w), m, v


def _adam_small(ws, ms, vs, parts, loss_part):
    n = len(ws)
    sent = list(parts) + [loss_part]
    ns = n + 1

    def body(*refs):
        w_refs, m_refs, v_refs = refs[:n], refs[n:2 * n], refs[2 * n:3 * n]
        p_refs = refs[3 * n:3 * n + ns]
        outs = refs[3 * n + ns:3 * n + ns + 4 * n + 1]
        g_refs, d_refs, nm_refs, nv_refs, loss_ref = outs[:n], outs[n:2 * n], outs[2 * n:3 * n], outs[3 * n:4 * n], outs[4 * n]
        all_s = refs[3 * n + ns + 4 * n + 1:3 * n + ns + 4 * n + 1 + ns]
        ssem, rsem = refs[-2:]
        x, y, c = lax.axis_index("x"), lax.axis_index("y"), lax.axis_index("c")
        me = 4 * x + 2 * y + c
        for i in range(ns):
            all_s[i][me] = p_refs[i][...]
        cps = []
        for rel in range(1, 8):
            peer = (1 - x if rel & 4 else x, 1 - y if rel & 2 else y, 1 - c if rel & 1 else c)
            for i in range(ns):
                k = (rel - 1) * ns + i
                mine = all_s[i].at[me]
                rc = _remote(mine, mine, ssem.at[k], rsem.at[k], peer)
                rc.start()
                cps.append((rc, i, k, 4 * peer[0] + 2 * peer[1] + peer[2]))
        for rc, i, k, peer_slot in cps:
            rc.wait_send()
            theirs = all_s[i].at[peer_slot]
            _remote(theirs, theirs, ssem.at[k], rsem.at[k], (x, y, c)).wait_recv()

        def total(i):
            t = all_s[i][0]
            for k in range(1, 8):
                t = t + all_s[i][k]
            return t

        for i in range(n):
            g = total(i)
            g_refs[i][...] = g
            d_refs[i][...], nm_refs[i][...], nv_refs[i][...] = _adamw(w_refs[i][...], g, m_refs[i][...], v_refs[i][...])
        loss_ref[...] = total(n)

    shapes = [_sds(w.shape, F32) for w in ws] * 4 + [_sds(loss_part.shape, F32)]
    ins = [*ws, *ms, *vs, *sent]
    outs = _pallas(
        body, name="adam_small", grid=(1,), in_specs=[_whole(t) for t in ins], out_specs=[_whole(t) for t in shapes],
        out_shape=shapes,
        scratch_shapes=[pltpu.VMEM((8,) + t.shape, F32) for t in sent]
        + [pltpu.SemaphoreType.DMA((7 * ns,)), pltpu.SemaphoreType.DMA((7 * ns,))],
        compiler_params=_params(("arbitrary",), 32),
    )(*ins)
    return outs[:n], outs[n:2 * n], outs[2 * n:3 * n], outs[3 * n:4 * n], outs[4 * n]


def _local_step(xs, tgt, mems, weights, small, gather_mid=None, gather_ffn=None, reduce=False):
    wt_in, wt_ba, wt_bs, wo, wq, wkv, wt_o, wt_gu, wd = weights
    g_mix, b_gate, w_sgu, b_sgu, g_sgu, g_cross, g_mem, g_ffn, g_final = small
    wt = jnp.tril(w_sgu).astype(BF16)
    bst = b_sgu.T

    (a, qkv0, qkv1, qkv2, uv, gl), got = _fwd_in(xs, g_mix, wt_in, gather_mid)
    if gather_mid is not None:
        wt_ba, wt_bs, wo, wq, wkv, wt_o = got
    qkvs = (qkv0, qkv1, qkv2)
    os_, ls_ = zip(*[_attn_fwd(qkvs[g], g) for g in range(3)])
    (ya, ys, ba, bs, mg, h1), got = _fwd_mid(xs, os_, ls_, uv, gl, wt, bst, g_sgu, b_gate, wt_ba, wt_bs, wo, gather_ffn)
    if gather_ffn is not None:
        wt_gu, wd = got
    mb, kv = _mem_fwd(mems, g_mem, wkv)
    cb, qc, oc, h2 = _fwd_cross(h1, g_cross, wq, kv, wt_o)
    f, act, dgu, dh3b, dh2, dh2b, dg_ffn, dg_final, loss = _ffn_fwd_bwd(h2, tgt, g_ffn, g_final, wt_gu, wd)

    g_ffn_w = [_tn_matmul(dgu, f, "dw_gate_up", 512), _tn_matmul(act, dh3b, "dw_down", 256)]
    r_ffn = _Reduce(g_ffn_w, ["w_gate_up", "w_down"]) if reduce else None
    (dqc, dh1, dh1b, dkv, dg_cross), parts_ffn = _bwd_cross(dh2, h1, qc, g_cross, wq, kv, wt_o,
                                                           r_ffn.scatter if reduce else None)
    dw_kv, dg_mem = _mem_bwd(dkv, mems, mb, g_mem, wkv)
    (dba, dbs, dgl, duv, do0, do1, do2, c0, c1, c2, db_gate, dg_sgu, dws, dbs_acc) = _bwd_mid(
        dh1, gl, ba, bs, uv, ls_, ya, wt, bst, g_sgu, b_gate, wt_ba, wt_bs, wo)
    dqkvs = [_attn_bwd(qkvs[g], do, ls_[g], corr, g) for g, (do, corr) in enumerate(((do0, c0), (do1, c1), (do2, c2)))]
    grad_x, dproj, dg_mix = _bwd_in(dqkvs, duv, dgl, dh1, xs, g_mix, wt_in)
    g_mid_w = [_tn_matmul(dba, ya, "dw_branch_attn", 512),
               _tn_matmul(dbs, ys, "dw_branch_sgu", 512),
               _tn_matmul(mg, dh1b, "dw_out", 512),
               _tn_matmul(cb, dqc, "dw_q_cross", 512),
               dw_kv,
               _tn_matmul(dh2b, oc, "dw_o_cross", 512)]
    small_terms = (dg_mix, db_gate, dws, dbs_acc, dg_sgu, dg_cross, dg_mem, dg_ffn, dg_final)
    if not reduce:
        return loss, grad_x, [_tn_matmul(dproj, a, "dw_in", 768)] + g_mid_w + g_ffn_w, small_terms
    r_mid = _Reduce(g_mid_w, ["w_branch_attn", "w_branch_sgu", "w_out", "w_q_cross", "w_kv_cross", "w_o_cross"])
    g_in, parts_mid = _tn_matmul(dproj, a, "dw_in", 768, comm=r_mid.scatter)
    r_in = _Reduce([g_in], ["w_in"])
    halves = r_in.collect(_scatter_partials(r_in.scatter)) + r_mid.collect(parts_mid) + r_ffn.collect(parts_ffn)
    return loss, grad_x, halves, small_terms


def kernel(x, mem, g_mix, w_in, b_gate, w_sgu_spatial, b_sgu_spatial, g_sgu, w_branch_attn, w_branch_sgu, w_out, g_cross, g_mem, w_q_cross, w_kv_cross, w_o_cross, g_ffn, w_gate_up, w_down, g_final, loss_target, m_g_mix, m_w_in, m_b_gate, m_w_sgu_spatial, m_b_sgu_spatial, m_g_sgu, m_w_branch_attn, m_w_branch_sgu, m_w_out, m_g_cross, m_g_mem, m_w_q_cross, m_w_kv_cross, m_w_o_cross, m_g_ffn, m_w_gate_up, m_w_down, m_g_final, v_g_mix, v_w_in, v_b_gate, v_w_sgu_spatial, v_b_sgu_spatial, v_g_sgu, v_w_branch_attn, v_w_branch_sgu, v_w_out, v_g_cross, v_g_mem, v_w_q_cross, v_w_kv_cross, v_w_o_cross, v_g_ffn, v_w_gate_up, v_w_down, v_g_final):
    S = x.shape[1]
    xs, tgt, mems = x.reshape(S, D), loss_target.reshape(S, D), mem.reshape(mem.shape[1], D)
    g_final2 = g_final.reshape(1, D)

    big = [("w_in", w_in[0], m_w_in[0], v_w_in[0], True),
           ("w_branch_attn", w_branch_attn[0], m_w_branch_attn[0], v_w_branch_attn[0], True),
           ("w_branch_sgu", w_branch_sgu[0], m_w_branch_sgu[0], v_w_branch_sgu[0], True),
           ("w_out", w_out[0], m_w_out[0], v_w_out[0], False),
           ("w_q_cross", w_q_cross[0], m_w_q_cross[0], v_w_q_cross[0], False),
           ("w_kv_cross", w_kv_cross[0], m_w_kv_cross[0], v_w_kv_cross[0], False),
           ("w_o_cross", w_o_cross[0], m_w_o_cross[0], v_w_o_cross[0], True),
           ("w_gate_up", w_gate_up[0], m_w_gate_up[0], v_w_gate_up[0], True),
           ("w_down", w_down[0], m_w_down[0], v_w_down[0], False)]
    shards = [(w.T if tr else w).astype(BF16) for _, w, _, _, tr in big]
    (wt_in,) = _gather_weights(shards[:1])
    (loss, grad_x, reduced, (dg_mix, db_gate, dws, dbs_acc, dg_sgu, dg_cross, dg_mem, dg_ffn, dg_final)) = _local_step(
        xs, tgt, mems, (wt_in,) + (None,) * 8,
        (g_mix, b_gate, w_sgu_spatial[0], b_sgu_spatial[0], g_sgu, g_cross, g_mem, g_ffn, g_final2),
        _Gather(shards[1:7]), _Gather(shards[7:9]), reduce=True)
    full = _share_halves(reduced)

    big_out = {}
    for (name, w, m, v, tr), gsh in zip(big, full):
        gsh = gsh.T if tr else gsh
        delta, nm, nv = _elementwise(_adamw, [w, gsh, m, v], [F32, F32, F32], f"adam_{name}")
        big_out[name] = tuple(t[None] for t in (gsh, delta, nm, nv))

    small = [("g_mix", g_mix, m_g_mix, v_g_mix, dg_mix), ("b_gate", b_gate, m_b_gate, v_b_gate, db_gate),
             ("w_sgu_spatial", w_sgu_spatial, m_w_sgu_spatial, v_w_sgu_spatial, jnp.tril(dws)),
             ("b_sgu_spatial", b_sgu_spatial, m_b_sgu_spatial, v_b_sgu_spatial, jnp.sum(dbs_acc, axis=-1)),
             ("g_sgu", g_sgu, m_g_sgu, v_g_sgu, dg_sgu), ("g_cross", g_cross, m_g_cross, v_g_cross, dg_cross),
             ("g_mem", g_mem, m_g_mem, v_g_mem, dg_mem), ("g_ffn", g_ffn, m_g_ffn, v_g_ffn, dg_ffn),
             ("g_final", g_final, m_g_final, v_g_final, dg_final)]
    as_term = lambda s, t: t.reshape(s[4].shape)
    gs, ds, nms, nvs, loss_all = _adam_small(*[[as_term(s, s[k]) for s in small] for k in (1, 2, 3, 4)], loss)
    small_out = {s[0]: tuple(t[i].reshape(s[1].shape) for t in (gs, ds, nms, nvs)) for i, s in enumerate(small)}
    total_loss = loss_all[0, 0]

    order = ["g_mix", "w_in", "b_gate", "w_sgu_spatial", "b_sgu_spatial", "g_sgu", "w_branch_attn", "w_branch_sgu",
             "w_out", "g_cross", "g_mem", "w_q_cross", "w_kv_cross", "w_o_cross", "g_ffn", "w_gate_up", "w_down",
             "g_final"]
    res = {**big_out, **small_out}
    outs = [total_loss, grad_x.reshape(x.shape)]
    for k in range(4):
        outs += [res[nm][k] for nm in order]
    return tuple(outs)
```

```python
import math

import numpy as np
import jax
import jax.numpy as jnp
from jax import lax
from jax.experimental import pallas as pl
from jax.experimental.pallas import tpu as pltpu

F32, BF16 = jnp.float32, jnp.bfloat16
MESH = pl.DeviceIdType.MESH
ANY = pl.BlockSpec(memory_space=pl.ANY)
RES = pl.BlockSpec(memory_space=pltpu.VMEM)


def _pallas(body, **kw):
    call = pl.pallas_call(body, **kw)
    gs = kw.get("grid_spec")
    specs = kw.get("in_specs") if gs is None else [None] * gs.num_scalar_prefetch + list(gs.in_specs)

    def run(*args):
        if specs is not None:
            args = [a if (s is RES or s is None) else pltpu.with_memory_space_constraint(a, pltpu.HBM)
                    for a, s in zip(args, specs)]
        return call(*args)
    return run


def _whole(arr):
    nd = len(arr.shape)
    return pl.BlockSpec(arr.shape, lambda *_: (0,) * nd)

D = 1024
HEAD = 64
GROUP_W = 256
DIL_GROUPS = ((128, 1), (512, 4), (2048, 16))
BLK = 128
SGU_W = 512
MEM_HEADS, MEM_HD, MEM_W = 4, 128, 512
D_FF = 2816
FF_CHUNK = 256
EPS = 1e-6
NEG = -1e30
LR, B1, B2, AEPS, WD, STEP = 0.001, 0.9, 0.999, 1e-08, 0.01, 10
GELU_K, GELU_C = 0.7978845608028654, 0.044715


def _dot(a, b):
    return jnp.dot(a, b, preferred_element_type=F32)


def _dot_nt(a, b):
    return lax.dot_general(a, b, (((1,), (1,)), ((), ())), preferred_element_type=F32)


def _dot_tn(a, b):
    return lax.dot_general(a, b, (((0,), (0,)), ((), ())), preferred_element_type=F32)


def _row(tm, w):
    return pl.BlockSpec((tm, w), lambda i: (i, 0))


def _acc(shape):
    return pl.BlockSpec(shape, lambda i: (0,) * len(shape))


def _params(sem, mb):
    return pltpu.CompilerParams(dimension_semantics=sem, vmem_limit_bytes=mb << 20)


def _sds(shape, dt):
    return jax.ShapeDtypeStruct(shape, dt)


def _rms(h):
    return lax.rsqrt(jnp.mean(h * h, axis=-1, keepdims=True) + EPS)


def _rms_bwd(dy, h, r, g):
    t = dy * g
    dh = r * t - h * (r * r * r) * jnp.mean(t * h, axis=-1, keepdims=True)
    return dh, dy * h * r


def _gelu(x):
    t = jnp.tanh(GELU_K * (x + GELU_C * x * x * x))
    return 0.5 * x * (1.0 + t), t


def _gelu_grad(x, t):
    return 0.5 * (1.0 + t) + 0.5 * x * (1.0 - t * t) * GELU_K * (1.0 + 3.0 * GELU_C * x * x)


def _alibi_slopes():
    def pow2(n):
        start = 2.0 ** (-8.0 / n)
        return [start ** (i + 1) for i in range(n)]
    n = 12
    c = 2 ** int(math.floor(math.log2(n)))
    s = pow2(c) + pow2(2 * c)[0::2][: n - c]
    return np.array(sorted(s, reverse=True), dtype=np.float32).reshape(3, 4)


def _attn_bias(g):
    win, dil = DIL_GROUPS[g]
    steps = (np.arange(BLK)[:, None] + BLK) - np.arange(2 * BLK)[None, :]
    valid = (steps >= 0) & (steps <= win // dil)
    dist = (np.clip(steps, 0, None) * dil).astype(np.float32)
    b = -_alibi_slopes()[g][:, None, None] * dist[None]
    return np.where(valid[None], b, NEG).astype(np.float32)


def _head_masks():
    lane = lax.broadcasted_iota(jnp.int32, (1, GROUP_W), 1)
    return lane, [(lane >= HEAD * h) & (lane < HEAD * (h + 1)) for h in range(4)]


ATT_NB = 4


def _stack_heads(t, masks):
    z = jnp.zeros_like(t)
    return jnp.concatenate([jnp.where(m, t, z) for m in masks], axis=0)


def _unstack_heads(t, masks):
    out = jnp.zeros((BLK, GROUP_W), t.dtype)
    for h, m in enumerate(masks):
        out = jnp.where(m, t[h * BLK:(h + 1) * BLK], out)
    return out


def _stack_cols(ref, rows):
    return jnp.concatenate([ref[rows, HEAD * h:HEAD * h + 1] for h in range(4)], axis=0)


def _dil_spec(d, tm, w):
    return pl.BlockSpec((d, tm // d, w), lambda i: (0, i, 0))


def _to_dilated(val, s_ref, d, write):
    tm, w = val.shape
    for j in range(w // 128):
        s_ref[j, pl.ds(0, tm), :] = val[:, j * 128:(j + 1) * 128]
    for r in range(d):
        for j in range(w // 128):
            write(r, j, s_ref[j, pl.ds(r, tm // d, stride=d), :])


def _from_dilated(ref, s_ref, d, tm, w):
    if d == 1:
        return ref[0].astype(F32)
    for r in range(d):
        for j in range(w // 128):
            s_ref[j, pl.ds(r, tm // d, stride=d), :] = ref[r, :, j * 128:(j + 1) * 128].astype(F32)
    return jnp.concatenate([s_ref[j, pl.ds(0, tm), :] for j in range(w // 128)], axis=1)


def _fwd_in(x, g_mix, wt_in, gather=None, tm=512):
    S = x.shape[0]
    dils = [d for _, d in DIL_GROUPS]
    n = 0 if gather is None else gather.n
    last = S // tm - 1

    def body(*refs):
        x_ref, g_ref, w_ref = refs[:3]
        a_ref, q0_ref, q1_ref, q2_ref, uv_ref, gl_ref = refs[3 + n:9 + n]
        s_ref = refs[9 + 2 * n]
        comm = (refs[3:3 + n], refs[9 + n:9 + 2 * n], refs[10 + 2 * n:])
        if gather is not None:
            pl.when(pl.program_id(0) == 0)(lambda: gather.start(*comm))
        xv = x_ref[...]
        a = (xv * _rms(xv) * g_ref[...]).astype(BF16)
        a_ref[...] = a
        for g, (d, out) in enumerate(zip(dils, (q0_ref, q1_ref, q2_ref))):
            for part in range(3):
                rows = part * 768 + g * 256
                val = _dot_nt(a, w_ref[rows:rows + 256, :])
                if d == 1:
                    out[0, :, part * 256:(part + 1) * 256] = val.astype(BF16)
                else:
                    def write(r, j, piece, out=out, part=part):
                        out[r, :, part * 256 + j * 128:part * 256 + (j + 1) * 128] = piece.astype(BF16)
                    _to_dilated(val, s_ref, d, write)
        uv_ref[...] = _dot_nt(a, w_ref[2304:3328, :]).astype(BF16)
        gl_ref[...] = _dot_nt(a, w_ref[3328:5376, :]).astype(BF16)
        if gather is not None:
            pl.when(pl.program_id(0) == last)(lambda: gather.finish(*comm))

    outs = _pallas(
        body, grid=(S // tm,), name="fwd_in",
        in_specs=[_row(tm, D), _whole(g_mix), RES] + [ANY] * n,
        out_specs=[_row(tm, D)] + [_dil_spec(d, tm, 768) for d in dils] + [_row(tm, 1024), _row(tm, 2048)] + [ANY] * n,
        out_shape=[_sds((S, D), BF16)] + [_sds((d, S // d, 768), BF16) for d in dils]
        + [_sds((S, 1024), BF16), _sds((S, 2048), BF16)] + ([] if gather is None else gather.out_shape),
        scratch_shapes=[pltpu.VMEM((2, tm, 128), F32)] + ([] if gather is None else gather.scratch),
        compiler_params=_params(("arbitrary",), 60),
    )(x, g_mix, wt_in, *([] if gather is None else gather.halves))
    return outs[:6], ([] if gather is None else gather.full(outs[6:]))


def _attn_fwd(qkv, g):
    d, L, _ = qkv.shape
    nb = L // BLK
    bias = jnp.asarray(_attn_bias(g).reshape(4 * BLK, 2 * BLK))
    NB = min(ATT_NB, nb)
    W = NB * BLK

    def body(q_ref, kc_ref, kp_ref, vc_ref, vp_ref, b_ref, o_ref, l_ref):
        st = pl.program_id(1)
        k_all = jnp.concatenate([kp_ref[...], kc_ref[...]], axis=0)
        v_all = jnp.concatenate([vp_ref[...], vc_ref[...]], axis=0)
        lane, masks = _head_masks()
        for b in range(NB):
            rows = slice(b * BLK, (b + 1) * BLK)
            kk, vv = k_all[b * BLK:(b + 2) * BLK], v_all[b * BLK:(b + 2) * BLK]
            s = _dot_nt(_stack_heads(q_ref[rows, :], masks), kk) * 0.125 + b_ref[...]
            if b == 0:
                s = s + jnp.where((st == 0) & (lane < BLK), NEG, 0.0).astype(F32)
            mx = jnp.max(s, axis=-1, keepdims=True)
            e = jnp.exp(s - mx)
            den = jnp.sum(e, axis=-1, keepdims=True)
            o_ref[rows, :] = _unstack_heads(_dot(e.astype(BF16), vv) / den, masks)
            l_ref[rows, :] = _unstack_heads(mx + jnp.log(den), masks)

    def wide(col):
        return pl.BlockSpec((None, W, GROUP_W), lambda r, s: (r, s, col))

    def before(col):
        return pl.BlockSpec((None, BLK, GROUP_W), lambda r, s: (r, jnp.maximum(s * NB - 1, 0), col))

    return _pallas(
        body, grid=(d, nb // NB), name=f"attn_fwd_g{g}",
        in_specs=[wide(0), wide(1), before(1), wide(2), before(2),
                  pl.BlockSpec((4 * BLK, 2 * BLK), lambda r, s: (0, 0))],
        out_specs=[wide(0), wide(0)],
        out_shape=[_sds((d, L, GROUP_W), F32), _sds((d, L, GROUP_W), F32)],
        compiler_params=_params(("parallel", "parallel"), 32),
    )(qkv, qkv, qkv, qkv, qkv, bias)


def _group_weights(l0, l1, l2):
    m = jnp.maximum(jnp.maximum(l0, l1), l2)
    e0, e1, e2 = jnp.exp(l0 - m), jnp.exp(l1 - m), jnp.exp(l2 - m)
    inv = 1.0 / (e0 + e1 + e2)
    return e0 * inv, e1 * inv, e2 * inv


def _sgu_forward(uvf, gs, wt_ref, bst_ref, mixed_s, tm):
    z, t = _gelu(uvf)
    u, v = z[:, :SGU_W], z[:, SGU_W:]
    rv = _rms(v)
    vnb = (v * rv * gs).astype(BF16)
    for ci in range(tm // 128):
        for g in range(4):
            rs, cs = slice(ci * 128, (ci + 1) * 128), slice(g * 128, (g + 1) * 128)
            mixed_s[rs, cs] = _dot(wt_ref[g], vnb[rs, cs]) + bst_ref[:, g:g + 1]
    return u, v, rv, vnb, t


def _fwd_mid(x, os_, ls_, uv, gl, wt, bst, g_sgu, b_gate, wt_ba, wt_bs, w_out, gather=None, tm=512):
    S = x.shape[0]
    dils = [d for _, d in DIL_GROUPS]
    n = 0 if gather is None else gather.n
    last = S // tm - 1

    def body(*refs):
        (x_ref, o0, o1, o2, l0, l1, l2, uv_ref, gl_ref, wt_ref, bst_ref, gs_ref, bg_ref, wba_ref, wbs_ref,
         wo_ref) = refs[:16]
        ya_ref, ys_ref, ba_ref, bs_ref, mg_ref, h1_ref = refs[16 + n:22 + n]
        mixed_s, il_s = refs[22 + 2 * n:24 + 2 * n]
        comm = (refs[16:16 + n], refs[22 + n:22 + 2 * n], refs[24 + 2 * n:])
        if gather is not None:
            pl.when(pl.program_id(0) == 0)(lambda: gather.start(*comm))
        ls = [_from_dilated(r, il_s, d, tm, GROUP_W) for r, d in zip((l0, l1, l2), dils)]
        alphas = _group_weights(*ls)
        ya = jnp.zeros((tm, GROUP_W), F32)
        for a, r, d in zip(alphas, (o0, o1, o2), dils):
            ya = ya + a * _from_dilated(r, il_s, d, tm, GROUP_W)
        yab = ya.astype(BF16)
        ya_ref[...] = yab
        u, _, _, _, _ = _sgu_forward(uv_ref[...].astype(F32), gs_ref[...], wt_ref, bst_ref, mixed_s, tm)
        ysb = (u * mixed_s[...]).astype(BF16)
        ys_ref[...] = ysb
        gates = jax.nn.sigmoid(gl_ref[...].astype(F32) + bg_ref[...])
        ba = _dot_nt(yab, wba_ref[...])
        bs = _dot_nt(ysb, wbs_ref[...])
        ba_ref[...] = ba.astype(BF16)
        bs_ref[...] = bs.astype(BF16)
        mgb = (gates[:, :D] * ba + gates[:, D:] * bs).astype(BF16)
        mg_ref[...] = mgb
        h1_ref[...] = x_ref[...] + _dot(mgb, wo_ref[...])
        if gather is not None:
            pl.when(pl.program_id(0) == last)(lambda: gather.finish(*comm))

    gw = _row(tm, GROUP_W)
    dil = [_dil_spec(d, tm, GROUP_W) for d in dils]
    outs = _pallas(
        body, grid=(S // tm,), name="fwd_mid",
        in_specs=[_row(tm, D)] + dil + dil + [_row(tm, 1024), _row(tm, 2048)]
        + [_whole(t) for t in (wt, bst, g_sgu, b_gate)] + [RES] * 3 + [ANY] * n,
        out_specs=[gw, _row(tm, SGU_W), _row(tm, D), _row(tm, D), _row(tm, D), _row(tm, D)] + [ANY] * n,
        out_shape=[_sds((S, GROUP_W), BF16), _sds((S, SGU_W), BF16), _sds((S, D), BF16), _sds((S, D), BF16),
                   _sds((S, D), BF16), _sds((S, D), F32)] + ([] if gather is None else gather.out_shape),
        scratch_shapes=[pltpu.VMEM((tm, SGU_W), F32), pltpu.VMEM((2, tm, 128), F32)]
        + ([] if gather is None else gather.scratch),
        compiler_params=_params(("arbitrary",), 56),
    )(x, *os_, *ls_, uv, gl, wt, bst, g_sgu, b_gate, wt_ba, wt_bs, w_out, *([] if gather is None else gather.halves))
    return outs[:6], ([] if gather is None else gather.full(outs[6:]))


def _mem_fwd(mem, g_mem, w_kv):
    def body(m_ref, g_ref, w_ref, mb_ref, kv_ref):
        mv = m_ref[...]
        mb = (mv * _rms(mv) * g_ref[...]).astype(BF16)
        mb_ref[...] = mb
        kv_ref[...] = _dot(mb, w_ref[...]).astype(BF16)

    shapes = [_sds(mem.shape, BF16), _sds((mem.shape[0], 2 * MEM_W), BF16)]
    return _pallas(
        body, name="mem_fwd", grid=(1,), in_specs=[_whole(t) for t in (mem, g_mem, w_kv)],
        out_specs=[_whole(t) for t in shapes], out_shape=shapes, compiler_params=_params(("arbitrary",), 32),
    )(mem, g_mem, w_kv)


def _cross_probs(qh, kh):
    s = _dot_nt(qh, kh) * (MEM_HD ** -0.5)
    e = jnp.exp(s - jnp.max(s, axis=-1, keepdims=True))
    return e / jnp.sum(e, axis=-1, keepdims=True)


def _fwd_cross(h1, g_cross, w_q, kv, wt_o, tm=512):
    S = h1.shape[0]

    def body(h_ref, g_ref, wq_ref, kv_ref, wo_ref, c_ref, qc_ref, oc_ref, h2_ref):
        hv = h_ref[...]
        cb = (hv * _rms(hv) * g_ref[...]).astype(BF16)
        c_ref[...] = cb
        qcb = _dot(cb, wq_ref[...]).astype(BF16)
        qc_ref[...] = qcb
        for h in range(MEM_HEADS):
            cs = slice(h * MEM_HD, (h + 1) * MEM_HD)
            p = _cross_probs(qcb[:, cs], kv_ref[:, cs])
            oc_ref[:, cs] = _dot(p.astype(BF16), kv_ref[:, MEM_W + h * MEM_HD:MEM_W + (h + 1) * MEM_HD]).astype(BF16)
        h2_ref[...] = hv + _dot_nt(oc_ref[...], wo_ref[...])

    return _pallas(
        body, grid=(S // tm,), name="fwd_cross",
        in_specs=[_row(tm, D), _whole(g_cross), RES, _whole(kv), RES],
        out_specs=[_row(tm, D), _row(tm, MEM_W), _row(tm, MEM_W), _row(tm, D)],
        out_shape=[_sds((S, D), BF16), _sds((S, MEM_W), BF16), _sds((S, MEM_W), BF16), _sds((S, D), F32)],
        compiler_params=_params(("parallel",), 40),
    )(h1, g_cross, w_q, kv, wt_o)


def _ffn_fwd_bwd(h2, target, g_ffn, g_final, wt_gu, w_down, tm=256):
    S = h2.shape[0]
    nch = D_FF // FF_CHUNK

    def body(h_ref, t_ref, gf_ref, gz_ref, wgu_ref, wd_ref,
             f_ref, act_ref, dgu_ref, dh3b_ref, dh2_ref, dh2b_ref, dgf_ref, dgz_ref, loss_ref, gu_s):
        i = pl.program_id(0)

        @pl.when(i == 0)
        def _():
            dgf_ref[...] = jnp.zeros_like(dgf_ref)
            dgz_ref[...] = jnp.zeros_like(dgz_ref)
            loss_ref[...] = jnp.zeros_like(loss_ref)

        def weights(c):
            return (wgu_ref.at[pl.ds(c * FF_CHUNK, FF_CHUNK)], wgu_ref.at[pl.ds(D_FF + c * FF_CHUNK, FF_CHUNK)],
                    wd_ref.at[pl.ds(c * FF_CHUNK, FF_CHUNK)])

        hv = h_ref[...]
        r2 = _rms(hv)
        gf = gf_ref[...]
        fb = (hv * r2 * gf).astype(BF16)
        f_ref[...] = fb
        h3 = hv
        for c in range(nch):
            cs = slice(c * FF_CHUNK, (c + 1) * FF_CHUNK)
            us = slice(D_FF + c * FF_CHUNK, D_FF + (c + 1) * FF_CHUNK)
            wg, wu, wd = weights(c)
            gt = _dot_nt(fb, wg[...])
            up = _dot_nt(fb, wu[...])
            gu_s[:, cs] = gt
            gu_s[:, us] = up
            actb = (gt * jax.nn.sigmoid(gt) * up).astype(BF16)
            act_ref[:, cs] = actb
            h3 = h3 + _dot(actb, wd[...])
        r3 = _rms(h3)
        gz = gz_ref[...]
        diff = h3 * r3 * gz - t_ref[...]
        dy = diff * (1.0 / D)
        dh3, dgz_rows = _rms_bwd(dy, h3, r3, gz)
        dh3b = dh3.astype(BF16)
        dh3b_ref[...] = dh3b
        df = jnp.zeros((tm, D), F32)
        for c in range(nch):
            cs = slice(c * FF_CHUNK, (c + 1) * FF_CHUNK)
            us = slice(D_FF + c * FF_CHUNK, D_FF + (c + 1) * FF_CHUNK)
            wg, wu, wd = weights(c)
            dact = _dot_nt(dh3b, wd[...])
            gt, up = gu_s[:, cs], gu_s[:, us]
            sg = jax.nn.sigmoid(gt)
            dgt = (dact * up * (sg * (1.0 + gt * (1.0 - sg)))).astype(BF16)
            dup = (dact * (gt * sg)).astype(BF16)
            dgu_ref[:, cs] = dgt
            dgu_ref[:, us] = dup
            df = df + _dot(dgt, wg[...]) + _dot(dup, wu[...])
        dhn, dgf_rows = _rms_bwd(df, hv, r2, gf)
        dh2 = dh3 + dhn
        dh2_ref[...] = dh2
        dh2b_ref[...] = dh2.astype(BF16)
        dgf_ref[...] += jnp.sum(dgf_rows, axis=0, keepdims=True)
        dgz_ref[...] += jnp.sum(dgz_rows, axis=0, keepdims=True)
        loss_ref[...] += jnp.sum(jnp.sum(diff * diff, axis=0, keepdims=True), axis=1, keepdims=True) * (0.5 / D)

    return _pallas(
        body, grid=(S // tm,), name="ffn_fwd_bwd",
        in_specs=[_row(tm, D), _row(tm, D), _whole(g_ffn), _whole(g_final), RES, RES],
        out_specs=[_row(tm, D), _row(tm, D_FF), _row(tm, 2 * D_FF), _row(tm, D), _row(tm, D), _row(tm, D),
                   _acc((1, D)), _acc((1, D)), _acc((1, 128))],
        out_shape=[_sds((S, D), BF16), _sds((S, D_FF), BF16), _sds((S, 2 * D_FF), BF16), _sds((S, D), BF16),
                   _sds((S, D), F32), _sds((S, D), BF16), _sds((1, D), F32), _sds((1, D), F32), _sds((1, 128), F32)],
        scratch_shapes=[pltpu.VMEM((tm, 2 * D_FF), F32)],
        compiler_params=_params(("arbitrary",), 56),
    )(h2, target, g_ffn, g_final, wt_gu, w_down)


def _bwd_cross(dh2, h1, qc, g_cross, w_q, kv, wt_o, comm=None, tm=512):
    S = h1.shape[0]
    n = 0 if comm is None else comm.n
    last = S // tm - 1

    def body(*refs):
        d_ref, h_ref, qc_ref, g_ref, wq_ref, kv_ref, wo_ref = refs[:7]
        dqc_ref, dh1_ref, dh1b_ref, dkv_ref, dg_ref = refs[7 + n:12 + n]
        cargs = (refs[7:7 + n], refs[12 + n:12 + 2 * n], refs[12 + 2 * n:])
        i = pl.program_id(0)

        @pl.when(i == 0)
        def _():
            dkv_ref[...] = jnp.zeros_like(dkv_ref)
            dg_ref[...] = jnp.zeros_like(dg_ref)
            if comm is not None:
                comm.start(*cargs)

        dh2 = d_ref[...]
        doc = _dot(dh2.astype(BF16), wo_ref[...])
        qcb = qc_ref[...]
        for h in range(MEM_HEADS):
            cs = slice(h * MEM_HD, (h + 1) * MEM_HD)
            vs = slice(MEM_W + h * MEM_HD, MEM_W + (h + 1) * MEM_HD)
            qh, kh, vh = qcb[:, cs], kv_ref[:, cs], kv_ref[:, vs]
            p = _cross_probs(qh, kh)
            dohb = doc[:, cs].astype(BF16)
            dp = _dot_nt(dohb, vh)
            dsb = (p * (dp - jnp.sum(dp * p, axis=-1, keepdims=True)) * (MEM_HD ** -0.5)).astype(BF16)
            dqc_ref[:, cs] = _dot(dsb, kh).astype(BF16)
            dkv_ref[:, cs] += _dot_tn(dsb, qh)
            dkv_ref[:, vs] += _dot_tn(p.astype(BF16), dohb)
        dc = _dot_nt(dqc_ref[...], wq_ref[...])
        hv = h_ref[...]
        dhn, dg_rows = _rms_bwd(dc, hv, _rms(hv), g_ref[...])
        dh1 = dh2 + dhn
        dh1_ref[...] = dh1
        dh1b_ref[...] = dh1.astype(BF16)
        dg_ref[...] += jnp.sum(dg_rows, axis=0, keepdims=True)
        if comm is not None:
            pl.when(i == last)(lambda: comm.finish(*cargs))

    outs = _pallas(
        body, grid=(S // tm,), name="bwd_cross",
        in_specs=[_row(tm, D), _row(tm, D), _row(tm, MEM_W), _whole(g_cross), RES, _whole(kv), RES] + [ANY] * n,
        out_specs=[_row(tm, MEM_W), _row(tm, D), _row(tm, D), _acc((256, 2 * MEM_W)), _acc((1, D))] + [ANY] * n,
        out_shape=[_sds((S, MEM_W), BF16), _sds((S, D), F32), _sds((S, D), BF16), _sds((256, 2 * MEM_W), F32),
                   _sds((1, D), F32)] + ([] if comm is None else comm.out_shape),
        scratch_shapes=[] if comm is None else comm.scratch,
        compiler_params=_params(("arbitrary",), 48),
    )(dh2, h1, qc, g_cross, w_q, kv, wt_o, *([] if comm is None else comm.ins))
    return outs[:5], outs[5:]


def _mem_bwd(dkv, mem, mb, g_mem, w_kv):
    def body(dkv_ref, m_ref, mb_ref, g_ref, w_ref, dw_ref, dwb_ref, dg_ref):
        dkvb = dkv_ref[...].astype(BF16)
        dw = _dot_tn(mb_ref[...], dkvb)
        dw_ref[...] = dw
        dwb_ref[...] = dw.astype(BF16)
        dm = _dot_nt(dkvb, w_ref[...])
        mv = m_ref[...]
        dg_ref[...] = jnp.sum(dm * mv * _rms(mv), axis=0, keepdims=True)

    shapes = [_sds((D, 2 * MEM_W), F32), _sds((D, 2 * MEM_W), BF16), _sds((1, D), F32)]
    return _pallas(
        body, name="mem_bwd", grid=(1,), in_specs=[_whole(t) for t in (dkv, mem, mb, g_mem, w_kv)],
        out_specs=[_whole(t) for t in shapes], out_shape=shapes, compiler_params=_params(("arbitrary",), 40),
    )(dkv, mem, mb, g_mem, w_kv)


def _bwd_mid(dh1, gl, ba, bs, uv, ls_, ya, wt, bst, g_sgu, b_gate, wt_ba, wt_bs, w_out, tm=512):
    S = dh1.shape[0]
    dils = [d for _, d in DIL_GROUPS]

    def body(d_ref, gl_ref, ba_ref, bs_ref, uv_ref, l0, l1, l2, ya_ref,
             wt_ref, bst_ref, gs_ref, bg_ref, wba_ref, wbs_ref, wo_ref,
             dba_ref, dbs_ref, dgl_ref, duv_ref, do0, do1, do2, c0, c1, c2,
             dbg_ref, dgs_ref, dws_ref, dbsa_ref, mixed_s, dvn_s, il_s):
        i = pl.program_id(0)

        @pl.when(i == 0)
        def _():
            for r in (dbg_ref, dgs_ref, dws_ref, dbsa_ref):
                r[...] = jnp.zeros_like(r)

        dm = _dot_nt(d_ref[...].astype(BF16), wo_ref[...])
        gates = jax.nn.sigmoid(gl_ref[...].astype(F32) + bg_ref[...])
        g0, g1 = gates[:, :D], gates[:, D:]
        dbab = (dm * g0).astype(BF16)
        dbsb = (dm * g1).astype(BF16)
        dba_ref[...] = dbab
        dbs_ref[...] = dbsb
        dg0 = dm * ba_ref[...].astype(F32) * g0 * (1.0 - g0)
        dg1 = dm * bs_ref[...].astype(F32) * g1 * (1.0 - g1)
        dgl_ref[:, :D] = dg0.astype(BF16)
        dgl_ref[:, D:] = dg1.astype(BF16)
        dbg_ref[:, :D] += jnp.sum(dg0, axis=0, keepdims=True)
        dbg_ref[:, D:] += jnp.sum(dg1, axis=0, keepdims=True)
        dya = _dot(dbab, wba_ref[...])
        dys = _dot(dbsb, wbs_ref[...])

        uvf = uv_ref[...].astype(F32)
        gs = gs_ref[...]
        u, v, rv, vnb, t = _sgu_forward(uvf, gs, wt_ref, bst_ref, mixed_s, tm)
        du = dys * mixed_s[...]
        dmixed = dys * u
        for ci in range(tm // 128):
            for g in range(4):
                rs, cs = slice(ci * 128, (ci + 1) * 128), slice(g * 128, (g + 1) * 128)
                dmx = dmixed[rs, cs]
                dmxb = dmx.astype(BF16)
                dvn_s[rs, cs] = _dot_tn(wt_ref[g], dmxb)
                dws_ref[g] += _dot_nt(dmxb, vnb[rs, cs])
                dbsa_ref[g] += dmx
        dv, dgs_rows = _rms_bwd(dvn_s[...], v, rv, gs)
        dgs_ref[...] += jnp.sum(dgs_rows, axis=0, keepdims=True)
        gg = _gelu_grad(uvf, t)
        duv_ref[:, :SGU_W] = (du * gg[:, :SGU_W]).astype(BF16)
        duv_ref[:, SGU_W:] = (dv * gg[:, SGU_W:]).astype(BF16)

        alphas = _group_weights(*[_from_dilated(r, il_s, d, tm, GROUP_W) for r, d in zip((l0, l1, l2), dils)])
        prod = dya * ya_ref[...].astype(F32)
        _, masks = _head_masks()
        hs = jnp.zeros_like(prod)
        for h in range(4):
            sh = jnp.sum(jnp.where(masks[h], prod, 0.0), axis=-1, keepdims=True)
            hs = jnp.where(masks[h], sh, hs)
        for a, d, do_ref, c_ref in zip(alphas, dils, (do0, do1, do2), (c0, c1, c2)):
            for val, out in ((a * dya, do_ref), (a * hs, c_ref)):
                if d == 1:
                    out[0] = val.astype(out.dtype)
                else:
                    def write(r, j, piece, out=out):
                        out[r, :, j * 128:(j + 1) * 128] = piece.astype(out.dtype)
                    _to_dilated(val, il_s, d, write)

    gw = _row(tm, GROUP_W)
    dil = [_dil_spec(d, tm, GROUP_W) for d in dils]
    return _pallas(
        body, grid=(S // tm,), name="bwd_mid",
        in_specs=[_row(tm, D), _row(tm, 2048), _row(tm, D), _row(tm, D), _row(tm, 1024)] + dil + [gw]
        + [_whole(t) for t in (wt, bst, g_sgu, b_gate)] + [RES] * 3,
        out_specs=[_row(tm, D), _row(tm, D), _row(tm, 2048), _row(tm, 1024)] + dil + dil
        + [_acc((1, 2048)), _acc((1, SGU_W)), _acc((4, 128, 128)), _acc((4, 128, 128))],
        out_shape=[_sds((S, D), BF16), _sds((S, D), BF16), _sds((S, 2048), BF16), _sds((S, 1024), BF16)]
        + [_sds((d, S // d, GROUP_W), BF16) for d in dils] + [_sds((d, S // d, GROUP_W), F32) for d in dils]
        + [_sds((1, 2048), F32), _sds((1, SGU_W), F32), _sds((4, 128, 128), F32), _sds((4, 128, 128), F32)],
        scratch_shapes=[pltpu.VMEM((tm, SGU_W), F32), pltpu.VMEM((tm, SGU_W), F32), pltpu.VMEM((2, tm, 128), F32)],
        compiler_params=_params(("arbitrary",), 60),
    )(dh1, gl, ba, bs, uv, *ls_, ya, wt, bst, g_sgu, b_gate, wt_ba, wt_bs, w_out)


def _attn_bwd(qkv, do, lse, corr, g):
    d, L, _ = qkv.shape
    nb = L // BLK
    NB = min(ATT_NB, nb)
    W = NB * BLK
    nsteps = nb // NB
    bias = jnp.asarray(_attn_bias(g).reshape(4 * BLK, 2 * BLK))

    def body(q_ref, kc_ref, kp_ref, vc_ref, vp_ref, do_ref, l_ref, c_ref, qn_ref, don_ref, ln_ref, cn_ref, b_ref,
             out_ref, dk_s, dv_s):
        st = pl.program_id(1)
        k_all = jnp.concatenate([kp_ref[...], kc_ref[...]], axis=0)
        v_all = jnp.concatenate([vp_ref[...], vc_ref[...]], axis=0)
        lane, masks = _head_masks()
        dk_s[...] = jnp.zeros_like(dk_s)
        dv_s[...] = jnp.zeros_like(dv_s)

        def block_terms(qs, dos, kk, vv, bias_v, lse_c, corr_c):
            s = _dot_nt(qs, kk) * 0.125 + bias_v
            p = jnp.exp(s - lse_c)
            dsb = (p * (_dot_nt(dos, vv) - corr_c) * 0.125).astype(BF16)
            return dsb, p.astype(BF16)

        for b in range(NB):
            rows = slice(b * BLK, (b + 1) * BLK)
            keys = slice(b * BLK, (b + 2) * BLK)
            kk, vv = k_all[keys], v_all[keys]
            qs, dos = _stack_heads(q_ref[rows, :], masks), _stack_heads(do_ref[rows, :], masks)
            bias_v = b_ref[...]
            if b == 0:
                bias_v = bias_v + jnp.where((st == 0) & (lane < BLK), NEG, 0.0).astype(F32)
            dsb, pb = block_terms(qs, dos, kk, vv, bias_v, _stack_cols(l_ref, rows), _stack_cols(c_ref, rows))
            out_ref[rows, 0:GROUP_W] = _unstack_heads(_dot(dsb, kk), masks).astype(BF16)
            dk_s[keys, :] += _dot_tn(dsb, qs)
            dv_s[keys, :] += _dot_tn(pb, dos)

        @pl.when(st < nsteps - 1)
        def _():
            last = slice(NB * BLK, (NB + 1) * BLK)
            qs, dos = _stack_heads(qn_ref[...], masks), _stack_heads(don_ref[...], masks)
            every = slice(None)
            dsb, pb = block_terms(qs, dos, k_all[last], v_all[last], b_ref[:, :BLK],
                                  _stack_cols(ln_ref, every), _stack_cols(cn_ref, every))
            dk_s[last, :] += _dot_tn(dsb, qs)
            dv_s[last, :] += _dot_tn(pb, dos)

        out_ref[:, GROUP_W:2 * GROUP_W] = dk_s[BLK:, :].astype(BF16)
        out_ref[:, 2 * GROUP_W:] = dv_s[BLK:, :].astype(BF16)

    def wide(col, w=GROUP_W):
        return pl.BlockSpec((None, W, w), lambda r, s: (r, s, col))

    def before(col):
        return pl.BlockSpec((None, BLK, GROUP_W), lambda r, s: (r, jnp.maximum(s * NB - 1, 0), col))

    def after(col):
        return pl.BlockSpec((None, BLK, GROUP_W), lambda r, s: (r, jnp.minimum((s + 1) * NB, nb - 1), col))

    return _pallas(
        body, grid=(d, nsteps), name=f"attn_bwd_g{g}",
        in_specs=[wide(0), wide(1), before(1), wide(2), before(2), wide(0), wide(0), wide(0),
                  after(0), after(0), after(0), after(0), pl.BlockSpec((4 * BLK, 2 * BLK), lambda r, s: (0, 0))],
        out_specs=wide(0, 768),
        out_shape=_sds((d, L, 768), BF16),
        scratch_shapes=[pltpu.VMEM(((NB + 1) * BLK, GROUP_W), F32), pltpu.VMEM(((NB + 1) * BLK, GROUP_W), F32)],
        compiler_params=_params(("parallel", "parallel"), 32),
    )(qkv, qkv, qkv, qkv, qkv, do, lse, corr, qkv, do, lse, corr, bias)


def _bwd_in(dqkvs, duv, dgl, dh1, x, g_mix, wt_in, tm=512):
    S = x.shape[0]
    dils = [d for _, d in DIL_GROUPS]

    def body(q0_ref, q1_ref, q2_ref, duv_ref, dgl_ref, d_ref, x_ref, g_ref, w_ref, dx_ref, dp_ref, dg_ref, il_s):
        i = pl.program_id(0)

        @pl.when(i == 0)
        def _():
            dg_ref[...] = jnp.zeros_like(dg_ref)

        for g, (d, ref) in enumerate(zip(dils, (q0_ref, q1_ref, q2_ref))):
            nat = _from_dilated(ref, il_s, d, tm, 768).astype(BF16)
            for part in range(3):
                col = part * 768 + g * 256
                dp_ref[:, col:col + 256] = nat[:, part * 256:(part + 1) * 256]
        dp_ref[:, 2304:3328] = duv_ref[...]
        dp_ref[:, 3328:5376] = dgl_ref[...]
        da = _dot(dp_ref[...], w_ref[...])
        xv = x_ref[...]
        dxn, dg_rows = _rms_bwd(da, xv, _rms(xv), g_ref[...])
        dx_ref[...] = d_ref[...] + dxn
        dg_ref[...] += jnp.sum(dg_rows, axis=0, keepdims=True)

    return _pallas(
        body, grid=(S // tm,), name="bwd_in",
        in_specs=[_dil_spec(d, tm, 768) for d in dils] + [_row(tm, 1024), _row(tm, 2048), _row(tm, D), _row(tm, D),
                                                          _whole(g_mix), RES],
        out_specs=[_row(tm, D), _row(tm, 5376), _acc((1, D))],
        out_shape=[_sds((S, D), F32), _sds((S, 5376), BF16), _sds((1, D), F32)],
        scratch_shapes=[pltpu.VMEM((6, tm, 128), F32)],
        compiler_params=_params(("arbitrary",), 60),
    )(*dqkvs, duv, dgl, dh1, x, g_mix, wt_in)


def _tn_matmul(a, b, name, tk, ts=2048, comm=None):
    S, K = a.shape
    N = b.shape[1]
    n = 0 if comm is None else comm.n
    nk, ns = K // tk, S // ts

    def body(*refs):
        a_ref, b_ref, o_ref, ob_ref = refs[0], refs[1], refs[2 + n], refs[3 + n]
        cargs = (refs[2:2 + n], refs[4 + n:4 + 2 * n], refs[4 + 2 * n:])
        k, s = pl.program_id(0), pl.program_id(1)
        if comm is not None:
            pl.when((k == 0) & (s == 0))(lambda: comm.start(*cargs))

        @pl.when(s == 0)
        def _():
            o_ref[...] = jnp.zeros_like(o_ref)

        o_ref[...] += _dot_tn(a_ref[...], b_ref[...])

        @pl.when(s == ns - 1)
        def _():
            ob_ref[...] = o_ref[...].astype(BF16)

        if comm is not None:
            pl.when((k == nk - 1) & (s == ns - 1))(lambda: comm.finish(*cargs))

    tile = pl.BlockSpec((tk, N), lambda k, s: (k, 0))
    outs = _pallas(
        body, grid=(nk, ns), name=name,
        in_specs=[pl.BlockSpec((ts, tk), lambda k, s: (s, k)), pl.BlockSpec((ts, N), lambda k, s: (s, 0))] + [ANY] * n,
        out_specs=[tile, tile] + [ANY] * n,
        out_shape=[_sds((K, N), F32), _sds((K, N), BF16)] + ([] if comm is None else comm.out_shape),
        scratch_shapes=[] if comm is None else comm.scratch,
        compiler_params=_params(("arbitrary", "arbitrary"), 48),
    )(a, b, *([] if comm is None else comm.ins))
    pair = (outs[0], outs[1])
    return pair if comm is None else (pair, outs[2:])


def _chip_peers(x, y):
    return [(1 - x, y), (x, 1 - y), (1 - x, 1 - y)]


STAGE_BYTES = 2 << 20


def _chunk_plan(shapes, itemsize):
    plan = []
    for i, (rows, w) in enumerate(shapes):
        ch = max(16, min(rows, (STAGE_BYTES // (w * itemsize)) // 16 * 16))
        while rows % ch:
            ch -= 16
        plan += [(i, r0, ch) for r0 in range(0, rows, ch)]
    return plan


def _remote(src, dst, ssem, rsem, dev):
    return pltpu.make_async_remote_copy(src_ref=src, dst_ref=dst, send_sem=ssem, recv_sem=rsem, device_id=dev,
                                        device_id_type=MESH)


class _Gather:
    def __init__(self, shards):
        self.n = len(shards)
        self.shards = shards
        self.halves = [s.reshape(2, s.shape[0] // 2, s.shape[1]) for s in shards]
        self.plan = _chunk_plan([h.shape[1:] for h in self.halves], 2)
        self.out_shape = [_sds((4,) + h.shape, BF16) for h in self.halves]
        n = self.n
        self.scratch = [pltpu.SemaphoreType.DMA((6 * n,)), pltpu.SemaphoreType.DMA((6 * n,)),
                        pltpu.SemaphoreType.DMA((2,)), pltpu.SemaphoreType.DMA((2,)),
                        pltpu.VMEM((2, max(p[2] for p in self.plan), max(h.shape[2] for h in self.halves)), BF16)]

    def full(self, outs):
        return [o.reshape(4 * s.shape[0], s.shape[1]) for o, s in zip(outs, self.shards)]

    def _sends(self, ins, outs, ssem, rsem):
        x, y, c = lax.axis_index("x"), lax.axis_index("y"), lax.axis_index("c")
        me = 2 * x + y
        return [_remote(ins[i].at[c], outs[i].at[me, c], ssem.at[6 * i + k], rsem.at[6 * i + k], (px, py, c))
                for i in range(self.n) for k, (px, py) in enumerate(_chip_peers(x, y))]

    def start(self, ins, outs, scratch):
        ssem, rsem, lsem, osem, buf = scratch
        me = 2 * lax.axis_index("x") + lax.axis_index("y")
        for cp in self._sends(ins, outs, ssem, rsem):
            cp.start()
        pending = {}
        for i, r0, ch in self.plan:
            for h in range(2):
                if h in pending:
                    pending[h].wait()
                stage = buf.at[h, pl.ds(0, ch), pl.ds(0, self.halves[i].shape[2])]
                ld = pltpu.make_async_copy(ins[i].at[h, pl.ds(r0, ch)], stage, lsem.at[h])
                ld.start()
                ld.wait()
                st = pltpu.make_async_copy(stage, outs[i].at[me, h, pl.ds(r0, ch)], osem.at[h])
                st.start()
                pending[h] = st
        for st in pending.values():
            st.wait()

    def finish(self, ins, outs, scratch):
        ssem, rsem = scratch[:2]
        x, y, c = lax.axis_index("x"), lax.axis_index("y"), lax.axis_index("c")
        chips = _chip_peers(x, y)
        sib = (x, y, 1 - c)
        forwards = []
        for i in range(self.n):
            for k, (px, py) in enumerate(chips):
                landed = outs[i].at[2 * px + py, c]
                _remote(landed, landed, ssem.at[6 * i + k], rsem.at[6 * i + k], (px, py, c)).wait_recv()
                cp = _remote(landed, landed, ssem.at[6 * i + 3 + k], rsem.at[6 * i + 3 + k], sib)
                cp.start()
                forwards.append(cp)
        for i in range(self.n):
            for k, (px, py) in enumerate(chips):
                passed = outs[i].at[2 * px + py, 1 - c]
                _remote(passed, passed, ssem.at[6 * i + 3 + k], rsem.at[6 * i + 3 + k], sib).wait_recv()
        for cp in self._sends(ins, outs, ssem, rsem) + forwards:
            cp.wait_send()


def _gather_weights(shards):
    gt = _Gather(shards)
    n = gt.n

    def body(*refs):
        ins, outs, scratch = refs[:n], refs[n:2 * n], refs[2 * n:]
        gt.start(ins, outs, scratch)
        gt.finish(ins, outs, scratch)

    outs = _pallas(
        body, name="gather_weights", in_specs=[ANY] * n, out_specs=[ANY] * n, out_shape=gt.out_shape,
        scratch_shapes=gt.scratch, compiler_params=pltpu.CompilerParams(vmem_limit_bytes=32 << 20),
    )(*gt.halves)
    return gt.full(outs)


def _swap_halves(grads):
    n = len(grads)
    view = lambda g: g.reshape(4, 2, g.shape[0] // 8, g.shape[1])
    g4f = [view(g) for g, _ in grads]
    g4 = [view(gb) for _, gb in grads]

    def body(*refs):
        ins, got = refs[:n], refs[n:2 * n]
        ssem, rsem = refs[2 * n:]
        x, y, c = lax.axis_index("x"), lax.axis_index("y"), lax.axis_index("c")
        sib = (x, y, 1 - c)
        cps = []
        for i in range(n):
            rc = _remote(ins[i].at[:, 1 - c], got[i], ssem.at[i], rsem.at[i], sib)
            rc.start()
            cps.append(rc)
        for cp in cps:
            cp.wait()

    half = [_sds((4, g.shape[2], g.shape[3]), BF16) for g in g4]
    got = _pallas(
        body, name="swap_halves", in_specs=[ANY] * n, out_specs=[ANY] * n, out_shape=half,
        scratch_shapes=[pltpu.SemaphoreType.DMA((n,)), pltpu.SemaphoreType.DMA((n,))],
    )(*g4)
    return g4f, got


def _chip_sum(g4, got, name):
    _, _, R, W = g4.shape
    tr = _tile(R, max(16, min(512, (1 << 18) // W // 16 * 16)))
    c = lax.axis_index("c").astype(jnp.int32).reshape(1)

    def body(c_ref, a_ref, b_ref, s_ref, sb_ref):
        s = a_ref[...] + b_ref[...].astype(F32)
        s_ref[...] = s
        sb_ref[...] = s.astype(BF16)

    plain = pl.BlockSpec((None, tr, W), lambda j, t, c_ref: (j, t, 0))
    return _pallas(
        body, name=name,
        grid_spec=pltpu.PrefetchScalarGridSpec(
            num_scalar_prefetch=1, grid=(4, R // tr),
            in_specs=[pl.BlockSpec((None, None, tr, W), lambda j, t, c_ref: (j, c_ref[0], t, 0)), plain],
            out_specs=[plain, plain]),
        out_shape=[_sds((4, R, W), F32), _sds((4, R, W), BF16)],
        compiler_params=_params(("parallel", "parallel"), 32),
    )(c, g4, got)


class _Scatter:
    def __init__(self, sums_b):
        self.n = len(sums_b)
        self.ins = list(sums_b)
        self.out_shape = [_sds((3,) + s.shape[1:], BF16) for s in sums_b]
        self.scratch = [pltpu.SemaphoreType.DMA((3 * self.n,)), pltpu.SemaphoreType.DMA((3 * self.n,))]

    def _copies(self, ins, outs, scratch):
        ssem, rsem = scratch
        x, y, c = lax.axis_index("x"), lax.axis_index("y"), lax.axis_index("c")
        return [_remote(ins[i].at[2 * px + py], outs[i].at[k], ssem.at[3 * i + k], rsem.at[3 * i + k], (px, py, c))
                for i in range(self.n) for k, (px, py) in enumerate(_chip_peers(x, y))]

    def start(self, ins, outs, scratch):
        for cp in self._copies(ins, outs, scratch):
            cp.start()

    def finish(self, ins, outs, scratch):
        for cp in self._copies(ins, outs, scratch):
            cp.wait()


def _scatter_partials(sc):
    n = sc.n

    def body(*refs):
        args = (refs[:n], refs[n:2 * n], refs[2 * n:])
        sc.start(*args)
        sc.finish(*args)

    return _pallas(
        body, name="scatter_partials", in_specs=[ANY] * n, out_specs=[ANY] * n, out_shape=sc.out_shape,
        scratch_shapes=sc.scratch,
    )(*sc.ins)


class _Reduce:
    def __init__(self, grads, names):
        self.names = names
        g4, got = _swap_halves(grads)
        self.sums, sums_b = [], []
        for nm, g, t in zip(names, g4, got):
            s_, sb_ = _chip_sum(g, t, f"chip_sum_{nm}")
            self.sums.append(s_)
            sums_b.append(sb_)
        self.scatter = _Scatter(sums_b)

    def collect(self, parts):
        return [_mesh_sum(s, p, f"mesh_sum_{nm}") for nm, s, p in zip(self.names, self.sums, parts)]


def _mesh_sum(sums, parts, name):
    _, R, W = sums.shape
    tr = _tile(R, max(16, min(512, (1 << 18) // W // 16 * 16)))
    me = (2 * lax.axis_index("x") + lax.axis_index("y")).astype(jnp.int32).reshape(1)

    def body(me_ref, m_ref, p_ref, o_ref):
        o_ref[...] = m_ref[...] + p_ref[0].astype(F32) + p_ref[1].astype(F32) + p_ref[2].astype(F32)

    return _pallas(
        body, name=name,
        grid_spec=pltpu.PrefetchScalarGridSpec(
            num_scalar_prefetch=1, grid=(R // tr,),
            in_specs=[pl.BlockSpec((None, tr, W), lambda i, me_ref: (me_ref[0], i, 0)),
                      pl.BlockSpec((3, tr, W), lambda i, me_ref: (0, i, 0))],
            out_specs=pl.BlockSpec((tr, W), lambda i, me_ref: (i, 0))),
        out_shape=_sds((R, W), F32), compiler_params=_params(("parallel",), 32),
    )(me, sums, parts)


def _share_halves(reduced):
    n = len(reduced)
    plan = _chunk_plan([r.shape for r in reduced], 4)
    max_rows = max(p[2] for p in plan)
    max_w = max(r.shape[1] for r in reduced)

    def body(*refs):
        ins, outs = refs[:n], refs[n:2 * n]
        ssem, rsem, lsem, osem, buf = refs[2 * n:]
        x, y, c = lax.axis_index("x"), lax.axis_index("y"), lax.axis_index("c")
        sib = (x, y, 1 - c)
        pending = {}
        for k, (i, r0, ch) in enumerate(plan):
            slot = k % 2
            if slot in pending:
                rc, lc = pending[slot]
                rc.wait_send()
                lc.wait()
            stage = buf.at[slot, pl.ds(0, ch), pl.ds(0, reduced[i].shape[1])]
            ld = pltpu.make_async_copy(ins[i].at[pl.ds(r0, ch)], stage, lsem.at[slot])
            ld.start()
            ld.wait()
            place = outs[i].at[c, pl.ds(r0, ch)]
            rc = _remote(stage, place, ssem.at[slot], rsem.at[i], sib)
            lc = pltpu.make_async_copy(stage, place, osem.at[slot])
            rc.start()
            lc.start()
            pending[slot] = (rc, lc)
        for rc, lc in pending.values():
            rc.wait_send()
            lc.wait()
        for i in range(n):
            theirs = outs[i].at[1 - c]
            _remote(theirs, theirs, ssem.at[0], rsem.at[i], sib).wait_recv()

    outs = _pallas(
        body, name="share_halves", in_specs=[ANY] * n, out_specs=[ANY] * n,
        out_shape=[_sds((2,) + r.shape, F32) for r in reduced],
        scratch_shapes=[pltpu.SemaphoreType.DMA((2,)), pltpu.SemaphoreType.DMA((n,)), pltpu.SemaphoreType.DMA((2,)),
                        pltpu.SemaphoreType.DMA((2,)), pltpu.VMEM((2, max_rows, max_w), F32)],
        compiler_params=pltpu.CompilerParams(vmem_limit_bytes=32 << 20),
    )(*reduced)
    return [o.reshape(2 * r.shape[0], r.shape[1]) for o, r in zip(outs, reduced)]


def _tile(rows, cap=256):
    t = min(rows, cap) // 16 * 16
    while rows % t:
        t -= 16
    return t


def _elementwise(fn, ins, out_dtypes, name):
    R, W = ins[0].shape
    tr = _tile(R, max(8, min(512, (1 << 18) // W // 8 * 8)))

    def body(*refs):
        outs = fn(*[r[...] for r in refs[:len(ins)]])
        for o_ref, o in zip(refs[len(ins):], outs):
            o_ref[...] = o.astype(o_ref.dtype)

    return _pallas(
        body, grid=(R // tr,), name=name, in_specs=[_row(tr, W)] * len(ins), out_specs=[_row(tr, W)] * len(out_dtypes),
        out_shape=[_sds((R, W), dt) for dt in out_dtypes],
        compiler_params=_params(("parallel",), 48),
    )(*ins)


def _adamw(w, g, m, v):
    m = B1 * m + (1.0 - B1) * g
    v = B2 * v + (1.0 - B2) * (g * g)
    m_hat = m / (1.0 - B1 ** STEP)
    v_hat = v / (1.0 - B2 ** STEP)
    return -LR * (m_hat / (jnp.sqrt(v_hat) + AEPS) + WD * w), m, v


def _adam_small(ws, ms, vs, parts, loss_part):
    n = len(ws)
    sent = list(parts) + [loss_part]
    ns = n + 1

    def body(*refs):
        w_refs, m_refs, v_refs = refs[:n], refs[n:2 * n], refs[2 * n:3 * n]
        p_refs = refs[3 * n:3 * n + ns]
        outs = refs[3 * n + ns:3 * n + ns + 4 * n + 1]
        g_refs, d_refs, nm_refs, nv_refs, loss_ref = outs[:n], outs[n:2 * n], outs[2 * n:3 * n], outs[3 * n:4 * n], outs[4 * n]
        all_s = refs[3 * n + ns + 4 * n + 1:3 * n + ns + 4 * n + 1 + ns]
        ssem, rsem = refs[-2:]
        x, y, c = lax.axis_index("x"), lax.axis_index("y"), lax.axis_index("c")
        me = 4 * x + 2 * y + c
        for i in range(ns):
            all_s[i][me] = p_refs[i][...]
        cps = []
        for rel in range(1, 8):
            peer = (1 - x if rel & 4 else x, 1 - y if rel & 2 else y, 1 - c if rel & 1 else c)
            for i in range(ns):
                k = (rel - 1) * ns + i
                mine = all_s[i].at[me]
                rc = _remote(mine, mine, ssem.at[k], rsem.at[k], peer)
                rc.start()
                cps.append((rc, i, k, 4 * peer[0] + 2 * peer[1] + peer[2]))
        for rc, i, k, peer_slot in cps:
            rc.wait_send()
            theirs = all_s[i].at[peer_slot]
            _remote(theirs, theirs, ssem.at[k], rsem.at[k], (x, y, c)).wait_recv()

        def total(i):
            t = all_s[i][0]
            for k in range(1, 8):
                t = t + all_s[i][k]
            return t

        for i in range(n):
            g = total(i)
            g_refs[i][...] = g
            d_refs[i][...], nm_refs[i][...], nv_refs[i][...] = _adamw(w_refs[i][...], g, m_refs[i][...], v_refs[i][...])
        loss_ref[...] = total(n)

    shapes = [_sds(w.shape, F32) for w in ws] * 4 + [_sds(loss_part.shape, F32)]
    ins = [*ws, *ms, *vs, *sent]
    outs = _pallas(
        body, name="adam_small", grid=(1,), in_specs=[_whole(t) for t in ins], out_specs=[_whole(t) for t in shapes],
        out_shape=shapes,
        scratch_shapes=[pltpu.VMEM((8,) + t.shape, F32) for t in sent]
        + [pltpu.SemaphoreType.DMA((7 * ns,)), pltpu.SemaphoreType.DMA((7 * ns,))],
        compiler_params=_params(("arbitrary",), 32),
    )(*ins)
    return outs[:n], outs[n:2 * n], outs[2 * n:3 * n], outs[3 * n:4 * n], outs[4 * n]


def _local_step(xs, tgt, mems, weights, small, gather_mid=None, gather_ffn=None, reduce=False):
    wt_in, wt_ba, wt_bs, wo, wq, wkv, wt_o, wt_gu, wd = weights
    g_mix, b_gate, w_sgu, b_sgu, g_sgu, g_cross, g_mem, g_ffn, g_final = small
    wt = jnp.tril(w_sgu).astype(BF16)
    bst = b_sgu.T

    (a, qkv0, qkv1, qkv2, uv, gl), got = _fwd_in(xs, g_mix, wt_in, gather_mid)
    if gather_mid is not None:
        wt_ba, wt_bs, wo, wq, wkv, wt_o = got
    qkvs = (qkv0, qkv1, qkv2)
    os_, ls_ = zip(*[_attn_fwd(qkvs[g], g) for g in range(3)])
    (ya, ys, ba, bs, mg, h1), got = _fwd_mid(xs, os_, ls_, uv, gl, wt, bst, g_sgu, b_gate, wt_ba, wt_bs, wo, gather_ffn)
    if gather_ffn is not None:
        wt_gu, wd = got
    mb, kv = _mem_fwd(mems, g_mem, wkv)
    cb, qc, oc, h2 = _fwd_cross(h1, g_cross, wq, kv, wt_o)
    f, act, dgu, dh3b, dh2, dh2b, dg_ffn, dg_final, loss = _ffn_fwd_bwd(h2, tgt, g_ffn, g_final, wt_gu, wd)

    g_ffn_w = [_tn_matmul(dgu, f, "dw_gate_up", 512), _tn_matmul(act, dh3b, "dw_down", 256)]
    r_ffn = _Reduce(g_ffn_w, ["w_gate_up", "w_down"]) if reduce else None
    (dqc, dh1, dh1b, dkv, dg_cross), parts_ffn = _bwd_cross(dh2, h1, qc, g_cross, wq, kv, wt_o,
                                                           r_ffn.scatter if reduce else None)
    dw_kv, dw_kvb, dg_mem = _mem_bwd(dkv, mems, mb, g_mem, wkv)
    (dba, dbs, dgl, duv, do0, do1, do2, c0, c1, c2, db_gate, dg_sgu, dws, dbs_acc) = _bwd_mid(
        dh1, gl, ba, bs, uv, ls_, ya, wt, bst, g_sgu, b_gate, wt_ba, wt_bs, wo)
    dqkvs = [_attn_bwd(qkvs[g], do, ls_[g], corr, g) for g, (do, corr) in enumerate(((do0, c0), (do1, c1), (do2, c2)))]
    grad_x, dproj, dg_mix = _bwd_in(dqkvs, duv, dgl, dh1, xs, g_mix, wt_in)
    g_mid_w = [_tn_matmul(dba, ya, "dw_branch_attn", 512),
               _tn_matmul(dbs, ys, "dw_branch_sgu", 512),
               _tn_matmul(mg, dh1b, "dw_out", 512),
               _tn_matmul(cb, dqc, "dw_q_cross", 512),
               (dw_kv, dw_kvb),
               _tn_matmul(dh2b, oc, "dw_o_cross", 512)]
    small_terms = (dg_mix, db_gate, dws, dbs_acc, dg_sgu, dg_cross, dg_mem, dg_ffn, dg_final)
    if not reduce:
        full = [_tn_matmul(dproj, a, "dw_in", 768)] + g_mid_w + g_ffn_w
        return loss, grad_x, [g for g, _ in full], small_terms
    r_mid = _Reduce(g_mid_w, ["w_branch_attn", "w_branch_sgu", "w_out", "w_q_cross", "w_kv_cross", "w_o_cross"])
    g_in, parts_mid = _tn_matmul(dproj, a, "dw_in", 768, comm=r_mid.scatter)
    r_in = _Reduce([g_in], ["w_in"])
    halves = r_in.collect(_scatter_partials(r_in.scatter)) + r_mid.collect(parts_mid) + r_ffn.collect(parts_ffn)
    return loss, grad_x, halves, small_terms


def kernel(x, mem, g_mix, w_in, b_gate, w_sgu_spatial, b_sgu_spatial, g_sgu, w_branch_attn, w_branch_sgu, w_out, g_cross, g_mem, w_q_cross, w_kv_cross, w_o_cross, g_ffn, w_gate_up, w_down, g_final, loss_target, m_g_mix, m_w_in, m_b_gate, m_w_sgu_spatial, m_b_sgu_spatial, m_g_sgu, m_w_branch_attn, m_w_branch_sgu, m_w_out, m_g_cross, m_g_mem, m_w_q_cross, m_w_kv_cross, m_w_o_cross, m_g_ffn, m_w_gate_up, m_w_down, m_g_final, v_g_mix, v_w_in, v_b_gate, v_w_sgu_spatial, v_b_sgu_spatial, v_g_sgu, v_w_branch_attn, v_w_branch_sgu, v_w_out, v_g_cross, v_g_mem, v_w_q_cross, v_w_kv_cross, v_w_o_cross, v_g_ffn, v_w_gate_up, v_w_down, v_g_final):
    S = x.shape[1]
    xs, tgt, mems = x.reshape(S, D), loss_target.reshape(S, D), mem.reshape(mem.shape[1], D)
    g_final2 = g_final.reshape(1, D)

    big = [("w_in", w_in[0], m_w_in[0], v_w_in[0], True),
           ("w_branch_attn", w_branch_attn[0], m_w_branch_attn[0], v_w_branch_attn[0], True),
           ("w_branch_sgu", w_branch_sgu[0], m_w_branch_sgu[0], v_w_branch_sgu[0], True),
           ("w_out", w_out[0], m_w_out[0], v_w_out[0], False),
           ("w_q_cross", w_q_cross[0], m_w_q_cross[0], v_w_q_cross[0], False),
           ("w_kv_cross", w_kv_cross[0], m_w_kv_cross[0], v_w_kv_cross[0], False),
           ("w_o_cross", w_o_cross[0], m_w_o_cross[0], v_w_o_cross[0], True),
           ("w_gate_up", w_gate_up[0], m_w_gate_up[0], v_w_gate_up[0], True),
           ("w_down", w_down[0], m_w_down[0], v_w_down[0], False)]
    shards = [(w.T if tr else w).astype(BF16) for _, w, _, _, tr in big]
    (wt_in,) = _gather_weights(shards[:1])
    (loss, grad_x, reduced, (dg_mix, db_gate, dws, dbs_acc, dg_sgu, dg_cross, dg_mem, dg_ffn, dg_final)) = _local_step(
        xs, tgt, mems, (wt_in,) + (None,) * 8,
        (g_mix, b_gate, w_sgu_spatial[0], b_sgu_spatial[0], g_sgu, g_cross, g_mem, g_ffn, g_final2),
        _Gather(shards[1:7]), _Gather(shards[7:9]), reduce=True)
    full = _share_halves(reduced)

    big_out = {}
    for (name, w, m, v, tr), gsh in zip(big, full):
        gsh = gsh.T if tr else gsh
        delta, nm, nv = _elementwise(_adamw, [w, gsh, m, v], [F32, F32, F32], f"adam_{name}")
        big_out[name] = tuple(t[None] for t in (gsh, delta, nm, nv))

    small = [("g_mix", g_mix, m_g_mix, v_g_mix, dg_mix), ("b_gate", b_gate, m_b_gate, v_b_gate, db_gate),
             ("w_sgu_spatial", w_sgu_spatial, m_w_sgu_spatial, v_w_sgu_spatial, jnp.tril(dws)),
             ("b_sgu_spatial", b_sgu_spatial, m_b_sgu_spatial, v_b_sgu_spatial, jnp.sum(dbs_acc, axis=-1)),
             ("g_sgu", g_sgu, m_g_sgu, v_g_sgu, dg_sgu), ("g_cross", g_cross, m_g_cross, v_g_cross, dg_cross),
             ("g_mem", g_mem, m_g_mem, v_g_mem, dg_mem), ("g_ffn", g_ffn, m_g_ffn, v_g_ffn, dg_ffn),
             ("g_final", g_final, m_g_final, v_g_final, dg_final)]
    as_term = lambda s, t: t.reshape(s[4].shape)
    gs, ds, nms, nvs, loss_all = _adam_small(*[[as_term(s, s[k]) for s in small] for k in (1, 2, 3, 4)], loss)
    small_out = {s[0]: tuple(t[i].reshape(s[1].shape) for t in (gs, ds, nms, nvs)) for i, s in enumerate(small)}
    total_loss = loss_all[0, 0]

    order = ["g_mix", "w_in", "b_gate", "w_sgu_spatial", "b_sgu_spatial", "g_sgu", "w_branch_attn", "w_branch_sgu",
             "w_out", "g_cross", "g_mem", "w_q_cross", "w_kv_cross", "w_o_cross", "g_ffn", "w_gate_up", "w_down",
             "g_final"]
    res = {**big_out, **small_out}
    outs = [total_loss, grad_x.reshape(x.shape)]
    for k in range(4):
        outs += [res[nm][k] for nm in order]
    return tuple(outs)
```

```python
import math

import numpy as np
import jax
import jax.numpy as jnp
from jax import lax
from jax.experimental import pallas as pl
from jax.experimental.pallas import tpu as pltpu

F32, BF16 = jnp.float32, jnp.bfloat16
MESH = pl.DeviceIdType.MESH
ANY = pl.BlockSpec(memory_space=pl.ANY)
RES = pl.BlockSpec(memory_space=pltpu.VMEM)


def _pallas(body, **kw):
    call = pl.pallas_call(body, **kw)
    gs = kw.get("grid_spec")
    specs = kw.get("in_specs") if gs is None else [None] * gs.num_scalar_prefetch + list(gs.in_specs)

    def run(*args):
        if specs is not None:
            args = [a if (s is RES or s is None) else pltpu.with_memory_space_constraint(a, pltpu.HBM)
                    for a, s in zip(args, specs)]
        return call(*args)
    return run


def _whole(arr):
    nd = len(arr.shape)
    return pl.BlockSpec(arr.shape, lambda *_: (0,) * nd)

D = 1024
HEAD = 64
GROUP_W = 256
DIL_GROUPS = ((128, 1), (512, 4), (2048, 16))
BLK = 128
SGU_W = 512
MEM_HEADS, MEM_HD, MEM_W = 4, 128, 512
D_FF = 2816
FF_CHUNK = 256
EPS = 1e-6
NEG = -1e30
LR, B1, B2, AEPS, WD, STEP = 0.001, 0.9, 0.999, 1e-08, 0.01, 10
GELU_K, GELU_C = 0.7978845608028654, 0.044715


def _dot(a, b):
    return jnp.dot(a, b, preferred_element_type=F32)


def _dot_nt(a, b):
    return lax.dot_general(a, b, (((1,), (1,)), ((), ())), preferred_element_type=F32)


def _dot_tn(a, b):
    return lax.dot_general(a, b, (((0,), (0,)), ((), ())), preferred_element_type=F32)


def _row(tm, w):
    return pl.BlockSpec((tm, w), lambda i: (i, 0))


def _acc(shape):
    return pl.BlockSpec(shape, lambda i: (0,) * len(shape))


def _params(sem, mb):
    return pltpu.CompilerParams(dimension_semantics=sem, vmem_limit_bytes=mb << 20)


def _sds(shape, dt):
    return jax.ShapeDtypeStruct(shape, dt)


def _rms(h):
    return lax.rsqrt(jnp.mean(h * h, axis=-1, keepdims=True) + EPS)


def _rms_bwd(dy, h, r, g):
    t = dy * g
    dh = r * t - h * (r * r * r) * jnp.mean(t * h, axis=-1, keepdims=True)
    return dh, dy * h * r


def _gelu(x):
    t = jnp.tanh(GELU_K * (x + GELU_C * x * x * x))
    return 0.5 * x * (1.0 + t), t


def _gelu_grad(x, t):
    return 0.5 * (1.0 + t) + 0.5 * x * (1.0 - t * t) * GELU_K * (1.0 + 3.0 * GELU_C * x * x)


def _alibi_slopes():
    def pow2(n):
        start = 2.0 ** (-8.0 / n)
        return [start ** (i + 1) for i in range(n)]
    n = 12
    c = 2 ** int(math.floor(math.log2(n)))
    s = pow2(c) + pow2(2 * c)[0::2][: n - c]
    return np.array(sorted(s, reverse=True), dtype=np.float32).reshape(3, 4)


def _attn_bias(g):
    win, dil = DIL_GROUPS[g]
    steps = (np.arange(BLK)[:, None] + BLK) - np.arange(2 * BLK)[None, :]
    valid = (steps >= 0) & (steps <= win // dil)
    dist = (np.clip(steps, 0, None) * dil).astype(np.float32)
    b = -_alibi_slopes()[g][:, None, None] * dist[None]
    return np.where(valid[None], b, NEG).astype(np.float32)


def _head_masks():
    lane = lax.broadcasted_iota(jnp.int32, (1, GROUP_W), 1)
    return lane, [(lane >= HEAD * h) & (lane < HEAD * (h + 1)) for h in range(4)]


ATT_NB = 8


def _stack_heads(t, masks):
    z = jnp.zeros_like(t)
    return jnp.concatenate([jnp.where(m, t, z) for m in masks], axis=0)


def _unstack_heads(t, masks):
    out = jnp.zeros((BLK, GROUP_W), t.dtype)
    for h, m in enumerate(masks):
        out = jnp.where(m, t[h * BLK:(h + 1) * BLK], out)
    return out


def _stack_cols(ref, rows):
    return jnp.concatenate([ref[rows, HEAD * h:HEAD * h + 1] for h in range(4)], axis=0)


def _dil_spec(d, tm, w):
    return pl.BlockSpec((d, tm // d, w), lambda i: (0, i, 0))


def _to_dilated(val, s_ref, d, write):
    tm, w = val.shape
    for j in range(w // 128):
        s_ref[j, pl.ds(0, tm), :] = val[:, j * 128:(j + 1) * 128]
    for r in range(d):
        for j in range(w // 128):
            write(r, j, s_ref[j, pl.ds(r, tm // d, stride=d), :])


def _from_dilated(ref, s_ref, d, tm, w):
    if d == 1:
        return ref[0].astype(F32)
    for r in range(d):
        for j in range(w // 128):
            s_ref[j, pl.ds(r, tm // d, stride=d), :] = ref[r, :, j * 128:(j + 1) * 128].astype(F32)
    return jnp.concatenate([s_ref[j, pl.ds(0, tm), :] for j in range(w // 128)], axis=1)


def _fwd_in(x, g_mix, wt_in, gather=None, tm=512):
    S = x.shape[0]
    dils = [d for _, d in DIL_GROUPS]
    n = 0 if gather is None else gather.n
    last = S // tm - 1

    def body(*refs):
        x_ref, g_ref, w_ref = refs[:3]
        a_ref, q0_ref, q1_ref, q2_ref, uv_ref, gl_ref = refs[3 + n:9 + n]
        s_ref = refs[9 + 2 * n]
        comm = (refs[3:3 + n], refs[9 + n:9 + 2 * n], refs[10 + 2 * n:])
        if gather is not None:
            pl.when(pl.program_id(0) == 0)(lambda: gather.start(*comm))
        xv = x_ref[...]
        a = (xv * _rms(xv) * g_ref[...]).astype(BF16)
        a_ref[...] = a
        for g, (d, out) in enumerate(zip(dils, (q0_ref, q1_ref, q2_ref))):
            for part in range(3):
                rows = part * 768 + g * 256
                val = _dot_nt(a, w_ref[rows:rows + 256, :])
                if d == 1:
                    out[0, :, part * 256:(part + 1) * 256] = val.astype(BF16)
                else:
                    def write(r, j, piece, out=out, part=part):
                        out[r, :, part * 256 + j * 128:part * 256 + (j + 1) * 128] = piece.astype(BF16)
                    _to_dilated(val, s_ref, d, write)
        uv_ref[...] = _dot_nt(a, w_ref[2304:3328, :]).astype(BF16)
        gl_ref[...] = _dot_nt(a, w_ref[3328:5376, :]).astype(BF16)
        if gather is not None:
            pl.when(pl.program_id(0) == last)(lambda: gather.finish(*comm))

    outs = _pallas(
        body, grid=(S // tm,), name="fwd_in",
        in_specs=[_row(tm, D), _whole(g_mix), RES] + [ANY] * n,
        out_specs=[_row(tm, D)] + [_dil_spec(d, tm, 768) for d in dils] + [_row(tm, 1024), _row(tm, 2048)] + [ANY] * n,
        out_shape=[_sds((S, D), BF16)] + [_sds((d, S // d, 768), BF16) for d in dils]
        + [_sds((S, 1024), BF16), _sds((S, 2048), BF16)] + ([] if gather is None else gather.out_shape),
        scratch_shapes=[pltpu.VMEM((2, tm, 128), F32)] + ([] if gather is None else gather.scratch),
        compiler_params=_params(("arbitrary",), 60),
    )(x, g_mix, wt_in, *([] if gather is None else gather.halves))
    return outs[:6], ([] if gather is None else gather.full(outs[6:]))


def _attn_fwd(qkv, g):
    d, L, _ = qkv.shape
    nb = L // BLK
    bias = jnp.asarray(_attn_bias(g).reshape(4 * BLK, 2 * BLK))
    NB = min(ATT_NB, nb)
    W = NB * BLK

    def body(q_ref, kc_ref, kp_ref, vc_ref, vp_ref, b_ref, o_ref, l_ref):
        st = pl.program_id(1)
        k_all = jnp.concatenate([kp_ref[...], kc_ref[...]], axis=0)
        v_all = jnp.concatenate([vp_ref[...], vc_ref[...]], axis=0)
        lane, masks = _head_masks()
        for b in range(NB):
            rows = slice(b * BLK, (b + 1) * BLK)
            kk, vv = k_all[b * BLK:(b + 2) * BLK], v_all[b * BLK:(b + 2) * BLK]
            s = _dot_nt(_stack_heads(q_ref[rows, :], masks), kk) * 0.125 + b_ref[...]
            if b == 0:
                s = s + jnp.where((st == 0) & (lane < BLK), NEG, 0.0).astype(F32)
            mx = jnp.max(s, axis=-1, keepdims=True)
            e = jnp.exp(s - mx)
            den = jnp.sum(e, axis=-1, keepdims=True)
            o_ref[rows, :] = _unstack_heads(_dot(e.astype(BF16), vv) / den, masks)
            l_ref[rows, :] = _unstack_heads(mx + jnp.log(den), masks)

    def wide(col):
        return pl.BlockSpec((None, W, GROUP_W), lambda r, s: (r, s, col))

    def before(col):
        return pl.BlockSpec((None, BLK, GROUP_W), lambda r, s: (r, jnp.maximum(s * NB - 1, 0), col))

    return _pallas(
        body, grid=(d, nb // NB), name=f"attn_fwd_g{g}",
        in_specs=[wide(0), wide(1), before(1), wide(2), before(2),
                  pl.BlockSpec((4 * BLK, 2 * BLK), lambda r, s: (0, 0))],
        out_specs=[wide(0), wide(0)],
        out_shape=[_sds((d, L, GROUP_W), F32), _sds((d, L, GROUP_W), F32)],
        compiler_params=_params(("parallel", "parallel"), 32),
    )(qkv, qkv, qkv, qkv, qkv, bias)


def _group_weights(l0, l1, l2):
    m = jnp.maximum(jnp.maximum(l0, l1), l2)
    e0, e1, e2 = jnp.exp(l0 - m), jnp.exp(l1 - m), jnp.exp(l2 - m)
    inv = 1.0 / (e0 + e1 + e2)
    return e0 * inv, e1 * inv, e2 * inv


def _sgu_forward(uvf, gs, wt_ref, bst_ref, mixed_s, tm):
    z, t = _gelu(uvf)
    u, v = z[:, :SGU_W], z[:, SGU_W:]
    rv = _rms(v)
    vnb = (v * rv * gs).astype(BF16)
    for ci in range(tm // 128):
        for g in range(4):
            rs, cs = slice(ci * 128, (ci + 1) * 128), slice(g * 128, (g + 1) * 128)
            mixed_s[rs, cs] = _dot(wt_ref[g], vnb[rs, cs]) + bst_ref[:, g:g + 1]
    return u, v, rv, vnb, t


def _fwd_mid(x, os_, ls_, uv, gl, wt, bst, g_sgu, b_gate, wt_ba, wt_bs, w_out, gather=None, tm=512):
    S = x.shape[0]
    dils = [d for _, d in DIL_GROUPS]
    n = 0 if gather is None else gather.n
    last = S // tm - 1

    def body(*refs):
        (x_ref, o0, o1, o2, l0, l1, l2, uv_ref, gl_ref, wt_ref, bst_ref, gs_ref, bg_ref, wba_ref, wbs_ref,
         wo_ref) = refs[:16]
        ya_ref, ys_ref, ba_ref, bs_ref, mg_ref, h1_ref = refs[16 + n:22 + n]
        mixed_s, il_s = refs[22 + 2 * n:24 + 2 * n]
        comm = (refs[16:16 + n], refs[22 + n:22 + 2 * n], refs[24 + 2 * n:])
        if gather is not None:
            pl.when(pl.program_id(0) == 0)(lambda: gather.start(*comm))
        ls = [_from_dilated(r, il_s, d, tm, GROUP_W) for r, d in zip((l0, l1, l2), dils)]
        alphas = _group_weights(*ls)
        ya = jnp.zeros((tm, GROUP_W), F32)
        for a, r, d in zip(alphas, (o0, o1, o2), dils):
            ya = ya + a * _from_dilated(r, il_s, d, tm, GROUP_W)
        yab = ya.astype(BF16)
        ya_ref[...] = yab
        u, _, _, _, _ = _sgu_forward(uv_ref[...].astype(F32), gs_ref[...], wt_ref, bst_ref, mixed_s, tm)
        ysb = (u * mixed_s[...]).astype(BF16)
        ys_ref[...] = ysb
        gates = jax.nn.sigmoid(gl_ref[...].astype(F32) + bg_ref[...])
        ba = _dot_nt(yab, wba_ref[...])
        bs = _dot_nt(ysb, wbs_ref[...])
        ba_ref[...] = ba.astype(BF16)
        bs_ref[...] = bs.astype(BF16)
        mgb = (gates[:, :D] * ba + gates[:, D:] * bs).astype(BF16)
        mg_ref[...] = mgb
        h1_ref[...] = x_ref[...] + _dot(mgb, wo_ref[...])
        if gather is not None:
            pl.when(pl.program_id(0) == last)(lambda: gather.finish(*comm))

    gw = _row(tm, GROUP_W)
    dil = [_dil_spec(d, tm, GROUP_W) for d in dils]
    outs = _pallas(
        body, grid=(S // tm,), name="fwd_mid",
        in_specs=[_row(tm, D)] + dil + dil + [_row(tm, 1024), _row(tm, 2048)]
        + [_whole(t) for t in (wt, bst, g_sgu, b_gate)] + [RES] * 3 + [ANY] * n,
        out_specs=[gw, _row(tm, SGU_W), _row(tm, D), _row(tm, D), _row(tm, D), _row(tm, D)] + [ANY] * n,
        out_shape=[_sds((S, GROUP_W), BF16), _sds((S, SGU_W), BF16), _sds((S, D), BF16), _sds((S, D), BF16),
                   _sds((S, D), BF16), _sds((S, D), F32)] + ([] if gather is None else gather.out_shape),
        scratch_shapes=[pltpu.VMEM((tm, SGU_W), F32), pltpu.VMEM((2, tm, 128), F32)]
        + ([] if gather is None else gather.scratch),
        compiler_params=_params(("arbitrary",), 56),
    )(x, *os_, *ls_, uv, gl, wt, bst, g_sgu, b_gate, wt_ba, wt_bs, w_out, *([] if gather is None else gather.halves))
    return outs[:6], ([] if gather is None else gather.full(outs[6:]))


def _mem_fwd(mem, g_mem, w_kv):
    def body(m_ref, g_ref, w_ref, mb_ref, kv_ref):
        mv = m_ref[...]
        mb = (mv * _rms(mv) * g_ref[...]).astype(BF16)
        mb_ref[...] = mb
        kv_ref[...] = _dot(mb, w_ref[...]).astype(BF16)

    shapes = [_sds(mem.shape, BF16), _sds((mem.shape[0], 2 * MEM_W), BF16)]
    return _pallas(
        body, name="mem_fwd", grid=(1,), in_specs=[_whole(t) for t in (mem, g_mem, w_kv)],
        out_specs=[_whole(t) for t in shapes], out_shape=shapes, compiler_params=_params(("arbitrary",), 32),
    )(mem, g_mem, w_kv)


def _cross_probs(qh, kh):
    s = _dot_nt(qh, kh) * (MEM_HD ** -0.5)
    e = jnp.exp(s - jnp.max(s, axis=-1, keepdims=True))
    return e / jnp.sum(e, axis=-1, keepdims=True)


def _fwd_cross(h1, g_cross, w_q, kv, wt_o, tm=512):
    S = h1.shape[0]

    def body(h_ref, g_ref, wq_ref, kv_ref, wo_ref, c_ref, qc_ref, oc_ref, h2_ref):
        hv = h_ref[...]
        cb = (hv * _rms(hv) * g_ref[...]).astype(BF16)
        c_ref[...] = cb
        qcb = _dot(cb, wq_ref[...]).astype(BF16)
        qc_ref[...] = qcb
        for h in range(MEM_HEADS):
            cs = slice(h * MEM_HD, (h + 1) * MEM_HD)
            p = _cross_probs(qcb[:, cs], kv_ref[:, cs])
            oc_ref[:, cs] = _dot(p.astype(BF16), kv_ref[:, MEM_W + h * MEM_HD:MEM_W + (h + 1) * MEM_HD]).astype(BF16)
        h2_ref[...] = hv + _dot_nt(oc_ref[...], wo_ref[...])

    return _pallas(
        body, grid=(S // tm,), name="fwd_cross",
        in_specs=[_row(tm, D), _whole(g_cross), RES, _whole(kv), RES],
        out_specs=[_row(tm, D), _row(tm, MEM_W), _row(tm, MEM_W), _row(tm, D)],
        out_shape=[_sds((S, D), BF16), _sds((S, MEM_W), BF16), _sds((S, MEM_W), BF16), _sds((S, D), F32)],
        compiler_params=_params(("parallel",), 40),
    )(h1, g_cross, w_q, kv, wt_o)


def _ffn_fwd_bwd(h2, target, g_ffn, g_final, wt_gu, w_down, tm=256):
    S = h2.shape[0]
    nch = D_FF // FF_CHUNK

    def body(h_ref, t_ref, gf_ref, gz_ref, wgu_ref, wd_ref,
             f_ref, act_ref, dgu_ref, dh3b_ref, dh2_ref, dh2b_ref, dgf_ref, dgz_ref, loss_ref, gu_s):
        i = pl.program_id(0)

        @pl.when(i == 0)
        def _():
            dgf_ref[...] = jnp.zeros_like(dgf_ref)
            dgz_ref[...] = jnp.zeros_like(dgz_ref)
            loss_ref[...] = jnp.zeros_like(loss_ref)

        def weights(c):
            return (wgu_ref.at[pl.ds(c * FF_CHUNK, FF_CHUNK)], wgu_ref.at[pl.ds(D_FF + c * FF_CHUNK, FF_CHUNK)],
                    wd_ref.at[pl.ds(c * FF_CHUNK, FF_CHUNK)])

        hv = h_ref[...]
        r2 = _rms(hv)
        gf = gf_ref[...]
        fb = (hv * r2 * gf).astype(BF16)
        f_ref[...] = fb
        h3 = hv
        for c in range(nch):
            cs = slice(c * FF_CHUNK, (c + 1) * FF_CHUNK)
            us = slice(D_FF + c * FF_CHUNK, D_FF + (c + 1) * FF_CHUNK)
            wg, wu, wd = weights(c)
            gt = _dot_nt(fb, wg[...])
            up = _dot_nt(fb, wu[...])
            gu_s[:, cs] = gt
            gu_s[:, us] = up
            actb = (gt * jax.nn.sigmoid(gt) * up).astype(BF16)
            act_ref[:, cs] = actb
            h3 = h3 + _dot(actb, wd[...])
        r3 = _rms(h3)
        gz = gz_ref[...]
        diff = h3 * r3 * gz - t_ref[...]
        dy = diff * (1.0 / D)
        dh3, dgz_rows = _rms_bwd(dy, h3, r3, gz)
        dh3b = dh3.astype(BF16)
        dh3b_ref[...] = dh3b
        df = jnp.zeros((tm, D), F32)
        for c in range(nch):
            cs = slice(c * FF_CHUNK, (c + 1) * FF_CHUNK)
            us = slice(D_FF + c * FF_CHUNK, D_FF + (c + 1) * FF_CHUNK)
            wg, wu, wd = weights(c)
            dact = _dot_nt(dh3b, wd[...])
            gt, up = gu_s[:, cs], gu_s[:, us]
            sg = jax.nn.sigmoid(gt)
            dgt = (dact * up * (sg * (1.0 + gt * (1.0 - sg)))).astype(BF16)
            dup = (dact * (gt * sg)).astype(BF16)
            dgu_ref[:, cs] = dgt
            dgu_ref[:, us] = dup
            df = df + _dot(dgt, wg[...]) + _dot(dup, wu[...])
        dhn, dgf_rows = _rms_bwd(df, hv, r2, gf)
        dh2 = dh3 + dhn
        dh2_ref[...] = dh2
        dh2b_ref[...] = dh2.astype(BF16)
        dgf_ref[...] += jnp.sum(dgf_rows, axis=0, keepdims=True)
        dgz_ref[...] += jnp.sum(dgz_rows, axis=0, keepdims=True)
        loss_ref[...] += jnp.sum(jnp.sum(diff * diff, axis=0, keepdims=True), axis=1, keepdims=True) * (0.5 / D)

    return _pallas(
        body, grid=(S // tm,), name="ffn_fwd_bwd",
        in_specs=[_row(tm, D), _row(tm, D), _whole(g_ffn), _whole(g_final), RES, RES],
        out_specs=[_row(tm, D), _row(tm, D_FF), _row(tm, 2 * D_FF), _row(tm, D), _row(tm, D), _row(tm, D),
                   _acc((1, D)), _acc((1, D)), _acc((1, 128))],
        out_shape=[_sds((S, D), BF16), _sds((S, D_FF), BF16), _sds((S, 2 * D_FF), BF16), _sds((S, D), BF16),
                   _sds((S, D), F32), _sds((S, D), BF16), _sds((1, D), F32), _sds((1, D), F32), _sds((1, 128), F32)],
        scratch_shapes=[pltpu.VMEM((tm, 2 * D_FF), F32)],
        compiler_params=_params(("arbitrary",), 56),
    )(h2, target, g_ffn, g_final, wt_gu, w_down)


def _bwd_cross(dh2, h1, qc, g_cross, w_q, kv, wt_o, comm=None, tm=512):
    S = h1.shape[0]
    n = 0 if comm is None else comm.n
    last = S // tm - 1

    def body(*refs):
        d_ref, h_ref, qc_ref, g_ref, wq_ref, kv_ref, wo_ref = refs[:7]
        dqc_ref, dh1_ref, dh1b_ref, dkv_ref, dg_ref = refs[7 + n:12 + n]
        cargs = (refs[7:7 + n], refs[12 + n:12 + 2 * n], refs[12 + 2 * n:])
        i = pl.program_id(0)

        @pl.when(i == 0)
        def _():
            dkv_ref[...] = jnp.zeros_like(dkv_ref)
            dg_ref[...] = jnp.zeros_like(dg_ref)
            if comm is not None:
                comm.start(*cargs)

        dh2 = d_ref[...]
        doc = _dot(dh2.astype(BF16), wo_ref[...])
        qcb = qc_ref[...]
        for h in range(MEM_HEADS):
            cs = slice(h * MEM_HD, (h + 1) * MEM_HD)
            vs = slice(MEM_W + h * MEM_HD, MEM_W + (h + 1) * MEM_HD)
            qh, kh, vh = qcb[:, cs], kv_ref[:, cs], kv_ref[:, vs]
            p = _cross_probs(qh, kh)
            dohb = doc[:, cs].astype(BF16)
            dp = _dot_nt(dohb, vh)
            dsb = (p * (dp - jnp.sum(dp * p, axis=-1, keepdims=True)) * (MEM_HD ** -0.5)).astype(BF16)
            dqc_ref[:, cs] = _dot(dsb, kh).astype(BF16)
            dkv_ref[:, cs] += _dot_tn(dsb, qh)
            dkv_ref[:, vs] += _dot_tn(p.astype(BF16), dohb)
        dc = _dot_nt(dqc_ref[...], wq_ref[...])
        hv = h_ref[...]
        dhn, dg_rows = _rms_bwd(dc, hv, _rms(hv), g_ref[...])
        dh1 = dh2 + dhn
        dh1_ref[...] = dh1
        dh1b_ref[...] = dh1.astype(BF16)
        dg_ref[...] += jnp.sum(dg_rows, axis=0, keepdims=True)
        if comm is not None:
            pl.when(i == last)(lambda: comm.finish(*cargs))

    outs = _pallas(
        body, grid=(S // tm,), name="bwd_cross",
        in_specs=[_row(tm, D), _row(tm, D), _row(tm, MEM_W), _whole(g_cross), RES, _whole(kv), RES] + [ANY] * n,
        out_specs=[_row(tm, MEM_W), _row(tm, D), _row(tm, D), _acc((256, 2 * MEM_W)), _acc((1, D))] + [ANY] * n,
        out_shape=[_sds((S, MEM_W), BF16), _sds((S, D), F32), _sds((S, D), BF16), _sds((256, 2 * MEM_W), F32),
                   _sds((1, D), F32)] + ([] if comm is None else comm.out_shape),
        scratch_shapes=[] if comm is None else comm.scratch,
        compiler_params=_params(("arbitrary",), 48),
    )(dh2, h1, qc, g_cross, w_q, kv, wt_o, *([] if comm is None else comm.ins))
    return outs[:5], outs[5:]


def _mem_bwd(dkv, mem, mb, g_mem, w_kv):
    def body(dkv_ref, m_ref, mb_ref, g_ref, w_ref, dw_ref, dwb_ref, dg_ref):
        dkvb = dkv_ref[...].astype(BF16)
        dw = _dot_tn(mb_ref[...], dkvb)
        dw_ref[...] = dw
        dwb_ref[...] = dw.astype(BF16)
        dm = _dot_nt(dkvb, w_ref[...])
        mv = m_ref[...]
        dg_ref[...] = jnp.sum(dm * mv * _rms(mv), axis=0, keepdims=True)

    shapes = [_sds((D, 2 * MEM_W), F32), _sds((D, 2 * MEM_W), BF16), _sds((1, D), F32)]
    return _pallas(
        body, name="mem_bwd", grid=(1,), in_specs=[_whole(t) for t in (dkv, mem, mb, g_mem, w_kv)],
        out_specs=[_whole(t) for t in shapes], out_shape=shapes, compiler_params=_params(("arbitrary",), 40),
    )(dkv, mem, mb, g_mem, w_kv)


def _bwd_mid(dh1, gl, ba, bs, uv, ls_, ya, wt, bst, g_sgu, b_gate, wt_ba, wt_bs, w_out, tm=512):
    S = dh1.shape[0]
    dils = [d for _, d in DIL_GROUPS]

    def body(d_ref, gl_ref, ba_ref, bs_ref, uv_ref, l0, l1, l2, ya_ref,
             wt_ref, bst_ref, gs_ref, bg_ref, wba_ref, wbs_ref, wo_ref,
             dba_ref, dbs_ref, dgl_ref, duv_ref, do0, do1, do2, c0, c1, c2,
             dbg_ref, dgs_ref, dws_ref, dbsa_ref, mixed_s, dvn_s, il_s):
        i = pl.program_id(0)

        @pl.when(i == 0)
        def _():
            for r in (dbg_ref, dgs_ref, dws_ref, dbsa_ref):
                r[...] = jnp.zeros_like(r)

        dm = _dot_nt(d_ref[...].astype(BF16), wo_ref[...])
        gates = jax.nn.sigmoid(gl_ref[...].astype(F32) + bg_ref[...])
        g0, g1 = gates[:, :D], gates[:, D:]
        dbab = (dm * g0).astype(BF16)
        dbsb = (dm * g1).astype(BF16)
        dba_ref[...] = dbab
        dbs_ref[...] = dbsb
        dg0 = dm * ba_ref[...].astype(F32) * g0 * (1.0 - g0)
        dg1 = dm * bs_ref[...].astype(F32) * g1 * (1.0 - g1)
        dgl_ref[:, :D] = dg0.astype(BF16)
        dgl_ref[:, D:] = dg1.astype(BF16)
        dbg_ref[:, :D] += jnp.sum(dg0, axis=0, keepdims=True)
        dbg_ref[:, D:] += jnp.sum(dg1, axis=0, keepdims=True)
        dya = _dot(dbab, wba_ref[...])
        dys = _dot(dbsb, wbs_ref[...])

        uvf = uv_ref[...].astype(F32)
        gs = gs_ref[...]
        u, v, rv, vnb, t = _sgu_forward(uvf, gs, wt_ref, bst_ref, mixed_s, tm)
        du = dys * mixed_s[...]
        dmixed = dys * u
        for ci in range(tm // 128):
            for g in range(4):
                rs, cs = slice(ci * 128, (ci + 1) * 128), slice(g * 128, (g + 1) * 128)
                dmx = dmixed[rs, cs]
                dmxb = dmx.astype(BF16)
                dvn_s[rs, cs] = _dot_tn(wt_ref[g], dmxb)
                dws_ref[g] += _dot_nt(dmxb, vnb[rs, cs])
                dbsa_ref[g] += dmx
        dv, dgs_rows = _rms_bwd(dvn_s[...], v, rv, gs)
        dgs_ref[...] += jnp.sum(dgs_rows, axis=0, keepdims=True)
        gg = _gelu_grad(uvf, t)
        duv_ref[:, :SGU_W] = (du * gg[:, :SGU_W]).astype(BF16)
        duv_ref[:, SGU_W:] = (dv * gg[:, SGU_W:]).astype(BF16)

        alphas = _group_weights(*[_from_dilated(r, il_s, d, tm, GROUP_W) for r, d in zip((l0, l1, l2), dils)])
        prod = dya * ya_ref[...].astype(F32)
        _, masks = _head_masks()
        hs = jnp.zeros_like(prod)
        for h in range(4):
            sh = jnp.sum(jnp.where(masks[h], prod, 0.0), axis=-1, keepdims=True)
            hs = jnp.where(masks[h], sh, hs)
        for a, d, do_ref, c_ref in zip(alphas, dils, (do0, do1, do2), (c0, c1, c2)):
            for val, out in ((a * dya, do_ref), (a * hs, c_ref)):
                if d == 1:
                    out[0] = val.astype(out.dtype)
                else:
                    def write(r, j, piece, out=out):
                        out[r, :, j * 128:(j + 1) * 128] = piece.astype(out.dtype)
                    _to_dilated(val, il_s, d, write)

    gw = _row(tm, GROUP_W)
    dil = [_dil_spec(d, tm, GROUP_W) for d in dils]
    return _pallas(
        body, grid=(S // tm,), name="bwd_mid",
        in_specs=[_row(tm, D), _row(tm, 2048), _row(tm, D), _row(tm, D), _row(tm, 1024)] + dil + [gw]
        + [_whole(t) for t in (wt, bst, g_sgu, b_gate)] + [RES] * 3,
        out_specs=[_row(tm, D), _row(tm, D), _row(tm, 2048), _row(tm, 1024)] + dil + dil
        + [_acc((1, 2048)), _acc((1, SGU_W)), _acc((4, 128, 128)), _acc((4, 128, 128))],
        out_shape=[_sds((S, D), BF16), _sds((S, D), BF16), _sds((S, 2048), BF16), _sds((S, 1024), BF16)]
        + [_sds((d, S // d, GROUP_W), BF16) for d in dils] + [_sds((d, S // d, GROUP_W), F32) for d in dils]
        + [_sds((1, 2048), F32), _sds((1, SGU_W), F32), _sds((4, 128, 128), F32), _sds((4, 128, 128), F32)],
        scratch_shapes=[pltpu.VMEM((tm, SGU_W), F32), pltpu.VMEM((tm, SGU_W), F32), pltpu.VMEM((2, tm, 128), F32)],
        compiler_params=_params(("arbitrary",), 60),
    )(dh1, gl, ba, bs, uv, *ls_, ya, wt, bst, g_sgu, b_gate, wt_ba, wt_bs, w_out)


def _attn_bwd(qkv, do, lse, corr, g):
    d, L, _ = qkv.shape
    nb = L // BLK
    NB = min(ATT_NB, nb)
    W = NB * BLK
    nsteps = nb // NB
    bias = jnp.asarray(_attn_bias(g).reshape(4 * BLK, 2 * BLK))

    def body(q_ref, kc_ref, kp_ref, vc_ref, vp_ref, do_ref, l_ref, c_ref, qn_ref, don_ref, ln_ref, cn_ref, b_ref,
             out_ref, dk_s, dv_s):
        st = pl.program_id(1)
        k_all = jnp.concatenate([kp_ref[...], kc_ref[...]], axis=0)
        v_all = jnp.concatenate([vp_ref[...], vc_ref[...]], axis=0)
        lane, masks = _head_masks()
        dk_s[...] = jnp.zeros_like(dk_s)
        dv_s[...] = jnp.zeros_like(dv_s)

        def block_terms(qs, dos, kk, vv, bias_v, lse_c, corr_c):
            s = _dot_nt(qs, kk) * 0.125 + bias_v
            p = jnp.exp(s - lse_c)
            dsb = (p * (_dot_nt(dos, vv) - corr_c) * 0.125).astype(BF16)
            return dsb, p.astype(BF16)

        for b in range(NB):
            rows = slice(b * BLK, (b + 1) * BLK)
            keys = slice(b * BLK, (b + 2) * BLK)
            kk, vv = k_all[keys], v_all[keys]
            qs, dos = _stack_heads(q_ref[rows, :], masks), _stack_heads(do_ref[rows, :], masks)
            bias_v = b_ref[...]
            if b == 0:
                bias_v = bias_v + jnp.where((st == 0) & (lane < BLK), NEG, 0.0).astype(F32)
            dsb, pb = block_terms(qs, dos, kk, vv, bias_v, _stack_cols(l_ref, rows), _stack_cols(c_ref, rows))
            out_ref[rows, 0:GROUP_W] = _unstack_heads(_dot(dsb, kk), masks).astype(BF16)
            dk_s[keys, :] += _dot_tn(dsb, qs)
            dv_s[keys, :] += _dot_tn(pb, dos)

        @pl.when(st < nsteps - 1)
        def _():
            last = slice(NB * BLK, (NB + 1) * BLK)
            qs, dos = _stack_heads(qn_ref[...], masks), _stack_heads(don_ref[...], masks)
            every = slice(None)
            dsb, pb = block_terms(qs, dos, k_all[last], v_all[last], b_ref[:, :BLK],
                                  _stack_cols(ln_ref, every), _stack_cols(cn_ref, every))
            dk_s[last, :] += _dot_tn(dsb, qs)
            dv_s[last, :] += _dot_tn(pb, dos)

        out_ref[:, GROUP_W:2 * GROUP_W] = dk_s[BLK:, :].astype(BF16)
        out_ref[:, 2 * GROUP_W:] = dv_s[BLK:, :].astype(BF16)

    def wide(col, w=GROUP_W):
        return pl.BlockSpec((None, W, w), lambda r, s: (r, s, col))

    def before(col):
        return pl.BlockSpec((None, BLK, GROUP_W), lambda r, s: (r, jnp.maximum(s * NB - 1, 0), col))

    def after(col):
        return pl.BlockSpec((None, BLK, GROUP_W), lambda r, s: (r, jnp.minimum((s + 1) * NB, nb - 1), col))

    return _pallas(
        body, grid=(d, nsteps), name=f"attn_bwd_g{g}",
        in_specs=[wide(0), wide(1), before(1), wide(2), before(2), wide(0), wide(0), wide(0),
                  after(0), after(0), after(0), after(0), pl.BlockSpec((4 * BLK, 2 * BLK), lambda r, s: (0, 0))],
        out_specs=wide(0, 768),
        out_shape=_sds((d, L, 768), BF16),
        scratch_shapes=[pltpu.VMEM(((NB + 1) * BLK, GROUP_W), F32), pltpu.VMEM(((NB + 1) * BLK, GROUP_W), F32)],
        compiler_params=_params(("parallel", "parallel"), 32),
    )(qkv, qkv, qkv, qkv, qkv, do, lse, corr, qkv, do, lse, corr, bias)


def _bwd_in(dqkvs, duv, dgl, dh1, x, g_mix, wt_in, tm=512):
    S = x.shape[0]
    dils = [d for _, d in DIL_GROUPS]

    def body(q0_ref, q1_ref, q2_ref, duv_ref, dgl_ref, d_ref, x_ref, g_ref, w_ref, dx_ref, dp_ref, dg_ref, il_s):
        i = pl.program_id(0)

        @pl.when(i == 0)
        def _():
            dg_ref[...] = jnp.zeros_like(dg_ref)

        for g, (d, ref) in enumerate(zip(dils, (q0_ref, q1_ref, q2_ref))):
            nat = _from_dilated(ref, il_s, d, tm, 768).astype(BF16)
            for part in range(3):
                col = part * 768 + g * 256
                dp_ref[:, col:col + 256] = nat[:, part * 256:(part + 1) * 256]
        dp_ref[:, 2304:3328] = duv_ref[...]
        dp_ref[:, 3328:5376] = dgl_ref[...]
        da = _dot(dp_ref[...], w_ref[...])
        xv = x_ref[...]
        dxn, dg_rows = _rms_bwd(da, xv, _rms(xv), g_ref[...])
        dx_ref[...] = d_ref[...] + dxn
        dg_ref[...] += jnp.sum(dg_rows, axis=0, keepdims=True)

    return _pallas(
        body, grid=(S // tm,), name="bwd_in",
        in_specs=[_dil_spec(d, tm, 768) for d in dils] + [_row(tm, 1024), _row(tm, 2048), _row(tm, D), _row(tm, D),
                                                          _whole(g_mix), RES],
        out_specs=[_row(tm, D), _row(tm, 5376), _acc((1, D))],
        out_shape=[_sds((S, D), F32), _sds((S, 5376), BF16), _sds((1, D), F32)],
        scratch_shapes=[pltpu.VMEM((6, tm, 128), F32)],
        compiler_params=_params(("arbitrary",), 60),
    )(*dqkvs, duv, dgl, dh1, x, g_mix, wt_in)


def _tn_matmul(a, b, name, tk, ts=2048, comm=None):
    S, K = a.shape
    N = b.shape[1]
    n = 0 if comm is None else comm.n
    nk, ns = K // tk, S // ts

    def body(*refs):
        a_ref, b_ref, o_ref, ob_ref = refs[0], refs[1], refs[2 + n], refs[3 + n]
        cargs = (refs[2:2 + n], refs[4 + n:4 + 2 * n], refs[4 + 2 * n:])
        k, s = pl.program_id(0), pl.program_id(1)
        if comm is not None:
            pl.when((k == 0) & (s == 0))(lambda: comm.start(*cargs))

        @pl.when(s == 0)
        def _():
            o_ref[...] = jnp.zeros_like(o_ref)

        o_ref[...] += _dot_tn(a_ref[...], b_ref[...])

        @pl.when(s == ns - 1)
        def _():
            ob_ref[...] = o_ref[...].astype(BF16)

        if comm is not None:
            pl.when((k == nk - 1) & (s == ns - 1))(lambda: comm.finish(*cargs))

    tile = pl.BlockSpec((tk, N), lambda k, s: (k, 0))
    outs = _pallas(
        body, grid=(nk, ns), name=name,
        in_specs=[pl.BlockSpec((ts, tk), lambda k, s: (s, k)), pl.BlockSpec((ts, N), lambda k, s: (s, 0))] + [ANY] * n,
        out_specs=[tile, tile] + [ANY] * n,
        out_shape=[_sds((K, N), F32), _sds((K, N), BF16)] + ([] if comm is None else comm.out_shape),
        scratch_shapes=[] if comm is None else comm.scratch,
        compiler_params=_params(("arbitrary", "arbitrary"), 56),
    )(a, b, *([] if comm is None else comm.ins))
    pair = (outs[0], outs[1])
    return pair if comm is None else (pair, outs[2:])


def _chip_peers(x, y):
    return [(1 - x, y), (x, 1 - y), (1 - x, 1 - y)]


STAGE_BYTES = 2 << 20


def _chunk_plan(shapes, itemsize):
    plan = []
    for i, (rows, w) in enumerate(shapes):
        ch = max(16, min(rows, (STAGE_BYTES // (w * itemsize)) // 16 * 16))
        while rows % ch:
            ch -= 16
        plan += [(i, r0, ch) for r0 in range(0, rows, ch)]
    return plan


def _remote(src, dst, ssem, rsem, dev):
    return pltpu.make_async_remote_copy(src_ref=src, dst_ref=dst, send_sem=ssem, recv_sem=rsem, device_id=dev,
                                        device_id_type=MESH)


class _Gather:
    def __init__(self, shards):
        self.n = len(shards)
        self.shards = shards
        self.halves = [s.reshape(2, s.shape[0] // 2, s.shape[1]) for s in shards]
        self.plan = _chunk_plan([h.shape[1:] for h in self.halves], 2)
        self.out_shape = [_sds((4,) + h.shape, BF16) for h in self.halves]
        n = self.n
        self.scratch = [pltpu.SemaphoreType.DMA((6 * n,)), pltpu.SemaphoreType.DMA((6 * n,)),
                        pltpu.SemaphoreType.DMA((2,)), pltpu.SemaphoreType.DMA((2,)),
                        pltpu.VMEM((2, max(p[2] for p in self.plan), max(h.shape[2] for h in self.halves)), BF16)]

    def full(self, outs):
        return [o.reshape(4 * s.shape[0], s.shape[1]) for o, s in zip(outs, self.shards)]

    def _sends(self, ins, outs, ssem, rsem):
        x, y, c = lax.axis_index("x"), lax.axis_index("y"), lax.axis_index("c")
        me = 2 * x + y
        return [_remote(ins[i].at[c], outs[i].at[me, c], ssem.at[6 * i + k], rsem.at[6 * i + k], (px, py, c))
                for i in range(self.n) for k, (px, py) in enumerate(_chip_peers(x, y))]

    def start(self, ins, outs, scratch):
        ssem, rsem, lsem, osem, buf = scratch
        me = 2 * lax.axis_index("x") + lax.axis_index("y")
        for cp in self._sends(ins, outs, ssem, rsem):
            cp.start()
        pending = {}
        for i, r0, ch in self.plan:
            for h in range(2):
                if h in pending:
                    pending[h].wait()
                stage = buf.at[h, pl.ds(0, ch), pl.ds(0, self.halves[i].shape[2])]
                ld = pltpu.make_async_copy(ins[i].at[h, pl.ds(r0, ch)], stage, lsem.at[h])
                ld.start()
                ld.wait()
                st = pltpu.make_async_copy(stage, outs[i].at[me, h, pl.ds(r0, ch)], osem.at[h])
                st.start()
                pending[h] = st
        for st in pending.values():
            st.wait()

    def finish(self, ins, outs, scratch):
        ssem, rsem = scratch[:2]
        x, y, c = lax.axis_index("x"), lax.axis_index("y"), lax.axis_index("c")
        chips = _chip_peers(x, y)
        sib = (x, y, 1 - c)
        forwards = []
        for i in range(self.n):
            for k, (px, py) in enumerate(chips):
                landed = outs[i].at[2 * px + py, c]
                _remote(landed, landed, ssem.at[6 * i + k], rsem.at[6 * i + k], (px, py, c)).wait_recv()
                cp = _remote(landed, landed, ssem.at[6 * i + 3 + k], rsem.at[6 * i + 3 + k], sib)
                cp.start()
                forwards.append(cp)
        for i in range(self.n):
            for k, (px, py) in enumerate(chips):
                passed = outs[i].at[2 * px + py, 1 - c]
                _remote(passed, passed, ssem.at[6 * i + 3 + k], rsem.at[6 * i + 3 + k], sib).wait_recv()
        for cp in self._sends(ins, outs, ssem, rsem) + forwards:
            cp.wait_send()


def _gather_weights(shards):
    gt = _Gather(shards)
    n = gt.n

    def body(*refs):
        ins, outs, scratch = refs[:n], refs[n:2 * n], refs[2 * n:]
        gt.start(ins, outs, scratch)
        gt.finish(ins, outs, scratch)

    outs = _pallas(
        body, name="gather_weights", in_specs=[ANY] * n, out_specs=[ANY] * n, out_shape=gt.out_shape,
        scratch_shapes=gt.scratch, compiler_params=pltpu.CompilerParams(vmem_limit_bytes=32 << 20),
    )(*gt.halves)
    return gt.full(outs)


def _swap_halves(grads):
    n = len(grads)
    view = lambda g: g.reshape(4, 2, g.shape[0] // 8, g.shape[1])
    g4f = [view(g) for g, _ in grads]
    g4 = [view(gb) for _, gb in grads]

    def body(*refs):
        ins, got = refs[:n], refs[n:2 * n]
        ssem, rsem = refs[2 * n:]
        x, y, c = lax.axis_index("x"), lax.axis_index("y"), lax.axis_index("c")
        sib = (x, y, 1 - c)
        cps = []
        for i in range(n):
            rc = _remote(ins[i].at[:, 1 - c], got[i], ssem.at[i], rsem.at[i], sib)
            rc.start()
            cps.append(rc)
        for cp in cps:
            cp.wait()

    half = [_sds((4, g.shape[2], g.shape[3]), BF16) for g in g4]
    got = _pallas(
        body, name="swap_halves", in_specs=[ANY] * n, out_specs=[ANY] * n, out_shape=half,
        scratch_shapes=[pltpu.SemaphoreType.DMA((n,)), pltpu.SemaphoreType.DMA((n,))],
    )(*g4)
    return g4f, got


def _chip_sum(g4, got, name):
    _, _, R, W = g4.shape
    tr = _tile(R, max(16, min(512, (1 << 18) // W // 16 * 16)))
    c = lax.axis_index("c").astype(jnp.int32).reshape(1)

    def body(c_ref, a_ref, b_ref, s_ref, sb_ref):
        s = a_ref[...] + b_ref[...].astype(F32)
        s_ref[...] = s
        sb_ref[...] = s.astype(BF16)

    plain = pl.BlockSpec((None, tr, W), lambda j, t, c_ref: (j, t, 0))
    return _pallas(
        body, name=name,
        grid_spec=pltpu.PrefetchScalarGridSpec(
            num_scalar_prefetch=1, grid=(4, R // tr),
            in_specs=[pl.BlockSpec((None, None, tr, W), lambda j, t, c_ref: (j, c_ref[0], t, 0)), plain],
            out_specs=[plain, plain]),
        out_shape=[_sds((4, R, W), F32), _sds((4, R, W), BF16)],
        compiler_params=_params(("parallel", "parallel"), 32),
    )(c, g4, got)


class _Scatter:
    def __init__(self, sums_b):
        self.n = len(sums_b)
        self.ins = list(sums_b)
        self.out_shape = [_sds((3,) + s.shape[1:], BF16) for s in sums_b]
        self.scratch = [pltpu.SemaphoreType.DMA((3 * self.n,)), pltpu.SemaphoreType.DMA((3 * self.n,))]

    def _copies(self, ins, outs, scratch):
        ssem, rsem = scratch
        x, y, c = lax.axis_index("x"), lax.axis_index("y"), lax.axis_index("c")
        return [_remote(ins[i].at[2 * px + py], outs[i].at[k], ssem.at[3 * i + k], rsem.at[3 * i + k], (px, py, c))
                for i in range(self.n) for k, (px, py) in enumerate(_chip_peers(x, y))]

    def start(self, ins, outs, scratch):
        for cp in self._copies(ins, outs, scratch):
            cp.start()

    def finish(self, ins, outs, scratch):
        for cp in self._copies(ins, outs, scratch):
            cp.wait()


def _scatter_partials(sc):
    n = sc.n

    def body(*refs):
        args = (refs[:n], refs[n:2 * n], refs[2 * n:])
        sc.start(*args)
        sc.finish(*args)

    return _pallas(
        body, name="scatter_partials", in_specs=[ANY] * n, out_specs=[ANY] * n, out_shape=sc.out_shape,
        scratch_shapes=sc.scratch,
    )(*sc.ins)


class _Reduce:
    def __init__(self, grads, names):
        self.names = names
        g4, got = _swap_halves(grads)
        self.sums, sums_b = [], []
        for nm, g, t in zip(names, g4, got):
            s_, sb_ = _chip_sum(g, t, f"chip_sum_{nm}")
            self.sums.append(s_)
            sums_b.append(sb_)
        self.scatter = _Scatter(sums_b)

    def collect(self, parts):
        return [_mesh_sum(s, p, f"mesh_sum_{nm}") for nm, s, p in zip(self.names, self.sums, parts)]


def _mesh_sum(sums, parts, name):
    _, R, W = sums.shape
    tr = _tile(R, max(16, min(512, (1 << 18) // W // 16 * 16)))
    me = (2 * lax.axis_index("x") + lax.axis_index("y")).astype(jnp.int32).reshape(1)

    def body(me_ref, m_ref, p_ref, o_ref):
        o_ref[...] = m_ref[...] + p_ref[0].astype(F32) + p_ref[1].astype(F32) + p_ref[2].astype(F32)

    return _pallas(
        body, name=name,
        grid_spec=pltpu.PrefetchScalarGridSpec(
            num_scalar_prefetch=1, grid=(R // tr,),
            in_specs=[pl.BlockSpec((None, tr, W), lambda i, me_ref: (me_ref[0], i, 0)),
                      pl.BlockSpec((3, tr, W), lambda i, me_ref: (0, i, 0))],
            out_specs=pl.BlockSpec((tr, W), lambda i, me_ref: (i, 0))),
        out_shape=_sds((R, W), F32), compiler_params=_params(("parallel",), 32),
    )(me, sums, parts)


def _share_halves(reduced):
    n = len(reduced)
    plan = _chunk_plan([r.shape for r in reduced], 4)
    max_rows = max(p[2] for p in plan)
    max_w = max(r.shape[1] for r in reduced)

    def body(*refs):
        ins, outs = refs[:n], refs[n:2 * n]
        ssem, rsem, lsem, osem, buf = refs[2 * n:]
        x, y, c = lax.axis_index("x"), lax.axis_index("y"), lax.axis_index("c")
        sib = (x, y, 1 - c)
        pending = {}
        for k, (i, r0, ch) in enumerate(plan):
            slot = k % 2
            if slot in pending:
                rc, lc = pending[slot]
                rc.wait_send()
                lc.wait()
            stage = buf.at[slot, pl.ds(0, ch), pl.ds(0, reduced[i].shape[1])]
            ld = pltpu.make_async_copy(ins[i].at[pl.ds(r0, ch)], stage, lsem.at[slot])
            ld.start()
            ld.wait()
            place = outs[i].at[c, pl.ds(r0, ch)]
            rc = _remote(stage, place, ssem.at[slot], rsem.at[i], sib)
            lc = pltpu.make_async_copy(stage, place, osem.at[slot])
            rc.start()
            lc.start()
            pending[slot] = (rc, lc)
        for rc, lc in pending.values():
            rc.wait_send()
            lc.wait()
        for i in range(n):
            theirs = outs[i].at[1 - c]
            _remote(theirs, theirs, ssem.at[0], rsem.at[i], sib).wait_recv()

    outs = _pallas(
        body, name="share_halves", in_specs=[ANY] * n, out_specs=[ANY] * n,
        out_shape=[_sds((2,) + r.shape, F32) for r in reduced],
        scratch_shapes=[pltpu.SemaphoreType.DMA((2,)), pltpu.SemaphoreType.DMA((n,)), pltpu.SemaphoreType.DMA((2,)),
                        pltpu.SemaphoreType.DMA((2,)), pltpu.VMEM((2, max_rows, max_w), F32)],
        compiler_params=pltpu.CompilerParams(vmem_limit_bytes=32 << 20),
    )(*reduced)
    return [o.reshape(2 * r.shape[0], r.shape[1]) for o, r in zip(outs, reduced)]


def _tile(rows, cap=256):
    t = min(rows, cap) // 16 * 16
    while rows % t:
        t -= 16
    return t


def _elementwise(fn, ins, out_dtypes, name):
    R, W = ins[0].shape
    tr = _tile(R, max(8, min(512, (1 << 18) // W // 8 * 8)))

    def body(*refs):
        outs = fn(*[r[...] for r in refs[:len(ins)]])
        for o_ref, o in zip(refs[len(ins):], outs):
            o_ref[...] = o.astype(o_ref.dtype)

    return _pallas(
        body, grid=(R // tr,), name=name, in_specs=[_row(tr, W)] * len(ins), out_specs=[_row(tr, W)] * len(out_dtypes),
        out_shape=[_sds((R, W), dt) for dt in out_dtypes],
        compiler_params=_params(("parallel",), 48),
    )(*ins)


def _adamw(w, g, m, v):
    m = B1 * m + (1.0 - B1) * g
    v = B2 * v + (1.0 - B2) * (g * g)
    m_hat = m / (1.0 - B1 ** STEP)
    v_hat = v / (1.0 - B2 ** STEP)
    return -LR * (m_hat / (jnp.sqrt(v_hat) + AEPS) + WD * w), m, v


def _adam_small(ws, ms, vs, parts, loss_part):
    n = len(ws)
    sent = list(parts) + [loss_part]
    ns = n + 1

    def body(*refs):
        w_refs, m_refs, v_refs = refs[:n], refs[n:2 * n], refs[2 * n:3 * n]
        p_refs = refs[3 * n:3 * n + ns]
        outs = refs[3 * n + ns:3 * n + ns + 4 * n + 1]
        g_refs, d_refs, nm_refs, nv_refs, loss_ref = outs[:n], outs[n:2 * n], outs[2 * n:3 * n], outs[3 * n:4 * n], outs[4 * n]
        all_s = refs[3 * n + ns + 4 * n + 1:3 * n + ns + 4 * n + 1 + ns]
        ssem, rsem = refs[-2:]
        x, y, c = lax.axis_index("x"), lax.axis_index("y"), lax.axis_index("c")
        me = 4 * x + 2 * y + c
        for i in range(ns):
            all_s[i][me] = p_refs[i][...]
        cps = []
        for rel in range(1, 8):
            peer = (1 - x if rel & 4 else x, 1 - y if rel & 2 else y, 1 - c if rel & 1 else c)
            for i in range(ns):
                k = (rel - 1) * ns + i
                mine = all_s[i].at[me]
                rc = _remote(mine, mine, ssem.at[k], rsem.at[k], peer)
                rc.start()
                cps.append((rc, i, k, 4 * peer[0] + 2 * peer[1] + peer[2]))
        for rc, i, k, peer_slot in cps:
            rc.wait_send()
            theirs = all_s[i].at[peer_slot]
            _remote(theirs, theirs, ssem.at[k], rsem.at[k], (x, y, c)).wait_recv()

        def total(i):
            t = all_s[i][0]
            for k in range(1, 8):
                t = t + all_s[i][k]
            return t

        for i in range(n):
            g = total(i)
            g_refs[i][...] = g
            d_refs[i][...], nm_refs[i][...], nv_refs[i][...] = _adamw(w_refs[i][...], g, m_refs[i][...], v_refs[i][...])
        loss_ref[...] = total(n)

    shapes = [_sds(w.shape, F32) for w in ws] * 4 + [_sds(loss_part.shape, F32)]
    ins = [*ws, *ms, *vs, *sent]
    outs = _pallas(
        body, name="adam_small", grid=(1,), in_specs=[_whole(t) for t in ins], out_specs=[_whole(t) for t in shapes],
        out_shape=shapes,
        scratch_shapes=[pltpu.VMEM((8,) + t.shape, F32) for t in sent]
        + [pltpu.SemaphoreType.DMA((7 * ns,)), pltpu.SemaphoreType.DMA((7 * ns,))],
        compiler_params=_params(("arbitrary",), 32),
    )(*ins)
    return outs[:n], outs[n:2 * n], outs[2 * n:3 * n], outs[3 * n:4 * n], outs[4 * n]


def _local_step(xs, tgt, mems, weights, small, gather_mid=None, gather_ffn=None, reduce=False):
    wt_in, wt_ba, wt_bs, wo, wq, wkv, wt_o, wt_gu, wd = weights
    g_mix, b_gate, w_sgu, b_sgu, g_sgu, g_cross, g_mem, g_ffn, g_final = small
    wt = jnp.tril(w_sgu).astype(BF16)
    bst = b_sgu.T

    (a, qkv0, qkv1, qkv2, uv, gl), got = _fwd_in(xs, g_mix, wt_in, gather_mid)
    if gather_mid is not None:
        wt_ba, wt_bs, wo, wq, wkv, wt_o = got
    qkvs = (qkv0, qkv1, qkv2)
    os_, ls_ = zip(*[_attn_fwd(qkvs[g], g) for g in range(3)])
    (ya, ys, ba, bs, mg, h1), got = _fwd_mid(xs, os_, ls_, uv, gl, wt, bst, g_sgu, b_gate, wt_ba, wt_bs, wo, gather_ffn)
    if gather_ffn is not None:
        wt_gu, wd = got
    mb, kv = _mem_fwd(mems, g_mem, wkv)
    cb, qc, oc, h2 = _fwd_cross(h1, g_cross, wq, kv, wt_o)
    f, act, dgu, dh3b, dh2, dh2b, dg_ffn, dg_final, loss = _ffn_fwd_bwd(h2, tgt, g_ffn, g_final, wt_gu, wd)

    g_ffn_w = [_tn_matmul(dgu, f, "dw_gate_up", 1408), _tn_matmul(act, dh3b, "dw_down", 1408)]
    r_ffn = _Reduce(g_ffn_w, ["w_gate_up", "w_down"]) if reduce else None
    (dqc, dh1, dh1b, dkv, dg_cross), parts_ffn = _bwd_cross(dh2, h1, qc, g_cross, wq, kv, wt_o,
                                                           r_ffn.scatter if reduce else None)
    dw_kv, dw_kvb, dg_mem = _mem_bwd(dkv, mems, mb, g_mem, wkv)
    (dba, dbs, dgl, duv, do0, do1, do2, c0, c1, c2, db_gate, dg_sgu, dws, dbs_acc) = _bwd_mid(
        dh1, gl, ba, bs, uv, ls_, ya, wt, bst, g_sgu, b_gate, wt_ba, wt_bs, wo)
    dqkvs = [_attn_bwd(qkvs[g], do, ls_[g], corr, g) for g, (do, corr) in enumerate(((do0, c0), (do1, c1), (do2, c2)))]
    grad_x, dproj, dg_mix = _bwd_in(dqkvs, duv, dgl, dh1, xs, g_mix, wt_in)
    g_mid_w = [_tn_matmul(dba, ya, "dw_branch_attn", 1024),
               _tn_matmul(dbs, ys, "dw_branch_sgu", 1024),
               _tn_matmul(mg, dh1b, "dw_out", 1024),
               _tn_matmul(cb, dqc, "dw_q_cross", 1024),
               (dw_kv, dw_kvb),
               _tn_matmul(dh2b, oc, "dw_o_cross", 1024)]
    small_terms = (dg_mix, db_gate, dws, dbs_acc, dg_sgu, dg_cross, dg_mem, dg_ffn, dg_final)
    if not reduce:
        full = [_tn_matmul(dproj, a, "dw_in", 1792)] + g_mid_w + g_ffn_w
        return loss, grad_x, [g for g, _ in full], small_terms
    r_mid = _Reduce(g_mid_w, ["w_branch_attn", "w_branch_sgu", "w_out", "w_q_cross", "w_kv_cross", "w_o_cross"])
    g_in, parts_mid = _tn_matmul(dproj, a, "dw_in", 1792, comm=r_mid.scatter)
    r_in = _Reduce([g_in], ["w_in"])
    halves = r_in.collect(_scatter_partials(r_in.scatter)) + r_mid.collect(parts_mid) + r_ffn.collect(parts_ffn)
    return loss, grad_x, halves, small_terms


def kernel(x, mem, g_mix, w_in, b_gate, w_sgu_spatial, b_sgu_spatial, g_sgu, w_branch_attn, w_branch_sgu, w_out, g_cross, g_mem, w_q_cross, w_kv_cross, w_o_cross, g_ffn, w_gate_up, w_down, g_final, loss_target, m_g_mix, m_w_in, m_b_gate, m_w_sgu_spatial, m_b_sgu_spatial, m_g_sgu, m_w_branch_attn, m_w_branch_sgu, m_w_out, m_g_cross, m_g_mem, m_w_q_cross, m_w_kv_cross, m_w_o_cross, m_g_ffn, m_w_gate_up, m_w_down, m_g_final, v_g_mix, v_w_in, v_b_gate, v_w_sgu_spatial, v_b_sgu_spatial, v_g_sgu, v_w_branch_attn, v_w_branch_sgu, v_w_out, v_g_cross, v_g_mem, v_w_q_cross, v_w_kv_cross, v_w_o_cross, v_g_ffn, v_w_gate_up, v_w_down, v_g_final):
    S = x.shape[1]
    xs, tgt, mems = x.reshape(S, D), loss_target.reshape(S, D), mem.reshape(mem.shape[1], D)
    g_final2 = g_final.reshape(1, D)

    big = [("w_in", w_in[0], m_w_in[0], v_w_in[0], True),
           ("w_branch_attn", w_branch_attn[0], m_w_branch_attn[0], v_w_branch_attn[0], True),
           ("w_branch_sgu", w_branch_sgu[0], m_w_branch_sgu[0], v_w_branch_sgu[0], True),
           ("w_out", w_out[0], m_w_out[0], v_w_out[0], False),
           ("w_q_cross", w_q_cross[0], m_w_q_cross[0], v_w_q_cross[0], False),
           ("w_kv_cross", w_kv_cross[0], m_w_kv_cross[0], v_w_kv_cross[0], False),
           ("w_o_cross", w_o_cross[0], m_w_o_cross[0], v_w_o_cross[0], True),
           ("w_gate_up", w_gate_up[0], m_w_gate_up[0], v_w_gate_up[0], True),
           ("w_down", w_down[0], m_w_down[0], v_w_down[0], False)]
    shards = [(w.T if tr else w).astype(BF16) for _, w, _, _, tr in big]
    (wt_in,) = _gather_weights(shards[:1])
    (loss, grad_x, reduced, (dg_mix, db_gate, dws, dbs_acc, dg_sgu, dg_cross, dg_mem, dg_ffn, dg_final)) = _local_step(
        xs, tgt, mems, (wt_in,) + (None,) * 8,
        (g_mix, b_gate, w_sgu_spatial[0], b_sgu_spatial[0], g_sgu, g_cross, g_mem, g_ffn, g_final2),
        _Gather(shards[1:7]), _Gather(shards[7:9]), reduce=True)
    full = _share_halves(reduced)

    big_out = {}
    for (name, w, m, v, tr), gsh in zip(big, full):
        gsh = gsh.T if tr else gsh
        delta, nm, nv = _elementwise(_adamw, [w, gsh, m, v], [F32, F32, F32], f"adam_{name}")
        big_out[name] = tuple(t[None] for t in (gsh, delta, nm, nv))

    small = [("g_mix", g_mix, m_g_mix, v_g_mix, dg_mix), ("b_gate", b_gate, m_b_gate, v_b_gate, db_gate),
             ("w_sgu_spatial", w_sgu_spatial, m_w_sgu_spatial, v_w_sgu_spatial, jnp.tril(dws)),
             ("b_sgu_spatial", b_sgu_spatial, m_b_sgu_spatial, v_b_sgu_spatial, jnp.sum(dbs_acc, axis=-1)),
             ("g_sgu", g_sgu, m_g_sgu, v_g_sgu, dg_sgu), ("g_cross", g_cross, m_g_cross, v_g_cross, dg_cross),
             ("g_mem", g_mem, m_g_mem, v_g_mem, dg_mem), ("g_ffn", g_ffn, m_g_ffn, v_g_ffn, dg_ffn),
             ("g_final", g_final, m_g_final, v_g_final, dg_final)]
    as_term = lambda s, t: t.reshape(s[4].shape)
    gs, ds, nms, nvs, loss_all = _adam_small(*[[as_term(s, s[k]) for s in small] for k in (1, 2, 3, 4)], loss)
    small_out = {s[0]: tuple(t[i].reshape(s[1].shape) for t in (gs, ds, nms, nvs)) for i, s in enumerate(small)}
    total_loss = loss_all[0, 0]

    order = ["g_mix", "w_in", "b_gate", "w_sgu_spatial", "b_sgu_spatial", "g_sgu", "w_branch_attn", "w_branch_sgu",
             "w_out", "g_cross", "g_mem", "w_q_cross", "w_kv_cross", "w_o_cross", "g_ffn", "w_gate_up", "w_down",
             "g_final"]
    res = {**big_out, **small_out}
    outs = [total_loss, grad_x.reshape(x.shape)]
    for k in range(4):
        outs += [res[nm][k] for nm in order]
    return tuple(outs)
```

```python
import math

import numpy as np
import jax
import jax.numpy as jnp
from jax import lax
from jax.experimental import pallas as pl
from jax.experimental.pallas import tpu as pltpu

F32, BF16 = jnp.float32, jnp.bfloat16
MESH = pl.DeviceIdType.MESH
ANY = pl.BlockSpec(memory_space=pl.ANY)
RES = pl.BlockSpec(memory_space=pltpu.VMEM)


def _pallas(body, **kw):
    call = pl.pallas_call(body, **kw)
    gs = kw.get("grid_spec")
    specs = kw.get("in_specs") if gs is None else [None] * gs.num_scalar_prefetch + list(gs.in_specs)

    def run(*args):
        if specs is not None:
            args = [a if (s is RES or s is None) else pltpu.with_memory_space_constraint(a, pltpu.HBM)
                    for a, s in zip(args, specs)]
        return call(*args)
    return run


def _whole(arr):
    nd = len(arr.shape)
    return pl.BlockSpec(arr.shape, lambda *_: (0,) * nd)

D = 1024
HEAD = 64
GROUP_W = 256
DIL_GROUPS = ((128, 1), (512, 4), (2048, 16))
BLK = 128
SGU_W = 512
MEM_HEADS, MEM_HD, MEM_W = 4, 128, 512
D_FF = 2816
FF_CHUNK = 256
EPS = 1e-6
NEG = -1e30
LR, B1, B2, AEPS, WD, STEP = 0.001, 0.9, 0.999, 1e-08, 0.01, 10
GELU_K, GELU_C = 0.7978845608028654, 0.044715


def _dot(a, b):
    return jnp.dot(a, b, preferred_element_type=F32)


def _dot_nt(a, b):
    return lax.dot_general(a, b, (((1,), (1,)), ((), ())), preferred_element_type=F32)


def _dot_tn(a, b):
    return lax.dot_general(a, b, (((0,), (0,)), ((), ())), preferred_element_type=F32)


def _row(tm, w):
    return pl.BlockSpec((tm, w), lambda i: (i, 0))


def _acc(shape):
    return pl.BlockSpec(shape, lambda i: (0,) * len(shape))


def _params(sem, mb):
    return pltpu.CompilerParams(dimension_semantics=sem, vmem_limit_bytes=mb << 20)


def _sds(shape, dt):
    return jax.ShapeDtypeStruct(shape, dt)


def _rms(h):
    return lax.rsqrt(jnp.mean(h * h, axis=-1, keepdims=True) + EPS)


def _rms_bwd(dy, h, r, g):
    t = dy * g
    dh = r * t - h * (r * r * r) * jnp.mean(t * h, axis=-1, keepdims=True)
    return dh, dy * h * r


def _gelu(x):
    t = jnp.tanh(GELU_K * (x + GELU_C * x * x * x))
    return 0.5 * x * (1.0 + t), t


def _gelu_grad(x, t):
    return 0.5 * (1.0 + t) + 0.5 * x * (1.0 - t * t) * GELU_K * (1.0 + 3.0 * GELU_C * x * x)


def _alibi_slopes():
    def pow2(n):
        start = 2.0 ** (-8.0 / n)
        return [start ** (i + 1) for i in range(n)]
    n = 12
    c = 2 ** int(math.floor(math.log2(n)))
    s = pow2(c) + pow2(2 * c)[0::2][: n - c]
    return np.array(sorted(s, reverse=True), dtype=np.float32).reshape(3, 4)


def _attn_bias(g):
    win, dil = DIL_GROUPS[g]
    steps = (np.arange(BLK)[:, None] + BLK) - np.arange(2 * BLK)[None, :]
    valid = (steps >= 0) & (steps <= win // dil)
    dist = (np.clip(steps, 0, None) * dil).astype(np.float32)
    b = -_alibi_slopes()[g][:, None, None] * dist[None]
    return np.where(valid[None], b, NEG).astype(np.float32)


def _head_masks():
    lane = lax.broadcasted_iota(jnp.int32, (1, GROUP_W), 1)
    return lane, [(lane >= HEAD * h) & (lane < HEAD * (h + 1)) for h in range(4)]


ATT_NB = 8


def _stack_heads(t, masks):
    z = jnp.zeros_like(t)
    return jnp.concatenate([jnp.where(m, t, z) for m in masks], axis=0)


def _unstack_heads(t, masks):
    out = jnp.zeros((BLK, GROUP_W), t.dtype)
    for h, m in enumerate(masks):
        out = jnp.where(m, t[h * BLK:(h + 1) * BLK], out)
    return out


def _stack_cols(ref, rows):
    return jnp.concatenate([ref[rows, HEAD * h:HEAD * h + 1] for h in range(4)], axis=0)


def _dil_spec(d, tm, w):
    return pl.BlockSpec((d, tm // d, w), lambda i: (0, i, 0))


def _to_dilated(val, s_ref, d, write):
    tm, w = val.shape
    for j in range(w // 128):
        s_ref[j, pl.ds(0, tm), :] = val[:, j * 128:(j + 1) * 128]
    for r in range(d):
        for j in range(w // 128):
            write(r, j, s_ref[j, pl.ds(r, tm // d, stride=d), :])


def _from_dilated(ref, s_ref, d, tm, w):
    if d == 1:
        return ref[0].astype(F32)
    for r in range(d):
        for j in range(w // 128):
            s_ref[j, pl.ds(r, tm // d, stride=d), :] = ref[r, :, j * 128:(j + 1) * 128].astype(F32)
    return jnp.concatenate([s_ref[j, pl.ds(0, tm), :] for j in range(w // 128)], axis=1)


def _fwd_in(x, g_mix, wt_in, gather=None, tm=512):
    S = x.shape[0]
    dils = [d for _, d in DIL_GROUPS]
    n = 0 if gather is None else gather.n
    last = S // tm - 1

    def body(*refs):
        x_ref, g_ref, w_ref = refs[:3]
        a_ref, q0_ref, q1_ref, q2_ref, uv_ref, gl_ref = refs[3 + n:9 + n]
        s_ref = refs[9 + 2 * n]
        comm = (refs[3:3 + n], refs[9 + n:9 + 2 * n], refs[10 + 2 * n:])
        if gather is not None:
            pl.when(pl.program_id(0) == 0)(lambda: gather.start(*comm))
        xv = x_ref[...]
        a = (xv * _rms(xv) * g_ref[...]).astype(BF16)
        a_ref[...] = a
        for g, (d, out) in enumerate(zip(dils, (q0_ref, q1_ref, q2_ref))):
            for part in range(3):
                rows = part * 768 + g * 256
                val = _dot_nt(a, w_ref[rows:rows + 256, :])
                if d == 1:
                    out[0, :, part * 256:(part + 1) * 256] = val.astype(BF16)
                else:
                    def write(r, j, piece, out=out, part=part):
                        out[r, :, part * 256 + j * 128:part * 256 + (j + 1) * 128] = piece.astype(BF16)
                    _to_dilated(val, s_ref, d, write)
        uv_ref[...] = _dot_nt(a, w_ref[2304:3328, :]).astype(BF16)
        gl_ref[...] = _dot_nt(a, w_ref[3328:5376, :]).astype(BF16)
        if gather is not None:
            pl.when(pl.program_id(0) == last)(lambda: gather.finish(*comm))

    outs = _pallas(
        body, grid=(S // tm,), name="fwd_in",
        in_specs=[_row(tm, D), _whole(g_mix), RES] + [ANY] * n,
        out_specs=[_row(tm, D)] + [_dil_spec(d, tm, 768) for d in dils] + [_row(tm, 1024), _row(tm, 2048)] + [ANY] * n,
        out_shape=[_sds((S, D), BF16)] + [_sds((d, S // d, 768), BF16) for d in dils]
        + [_sds((S, 1024), BF16), _sds((S, 2048), BF16)] + ([] if gather is None else gather.out_shape),
        scratch_shapes=[pltpu.VMEM((2, tm, 128), F32)] + ([] if gather is None else gather.scratch),
        compiler_params=_params(("arbitrary",), 60),
    )(x, g_mix, wt_in, *([] if gather is None else gather.halves))
    return outs[:6], ([] if gather is None else gather.full(outs[6:]))


def _attn_fwd(qkv, g):
    d, L, _ = qkv.shape
    nb = L // BLK
    bias = jnp.asarray(_attn_bias(g).reshape(4 * BLK, 2 * BLK))
    NB = min(ATT_NB, nb)
    W = NB * BLK

    def body(q_ref, kc_ref, kp_ref, vc_ref, vp_ref, b_ref, o_ref, l_ref):
        st = pl.program_id(1)
        k_all = jnp.concatenate([kp_ref[...], kc_ref[...]], axis=0)
        v_all = jnp.concatenate([vp_ref[...], vc_ref[...]], axis=0)
        lane, masks = _head_masks()
        for b in range(NB):
            rows = slice(b * BLK, (b + 1) * BLK)
            kk, vv = k_all[b * BLK:(b + 2) * BLK], v_all[b * BLK:(b + 2) * BLK]
            s = _dot_nt(_stack_heads(q_ref[rows, :], masks), kk) * 0.125 + b_ref[...]
            if b == 0:
                s = s + jnp.where((st == 0) & (lane < BLK), NEG, 0.0).astype(F32)
            mx = jnp.max(s, axis=-1, keepdims=True)
            e = jnp.exp(s - mx)
            den = jnp.sum(e, axis=-1, keepdims=True)
            o_ref[rows, :] = _unstack_heads(_dot(e.astype(BF16), vv) / den, masks)
            l_ref[rows, :] = _unstack_heads(mx + jnp.log(den), masks)

    def wide(col):
        return pl.BlockSpec((None, W, GROUP_W), lambda r, s: (r, s, col))

    def before(col):
        return pl.BlockSpec((None, BLK, GROUP_W), lambda r, s: (r, jnp.maximum(s * NB - 1, 0), col))

    return _pallas(
        body, grid=(d, nb // NB), name=f"attn_fwd_g{g}",
        in_specs=[wide(0), wide(1), before(1), wide(2), before(2),
                  pl.BlockSpec((4 * BLK, 2 * BLK), lambda r, s: (0, 0))],
        out_specs=[wide(0), wide(0)],
        out_shape=[_sds((d, L, GROUP_W), F32), _sds((d, L, GROUP_W), F32)],
        compiler_params=_params(("parallel", "parallel"), 32),
    )(qkv, qkv, qkv, qkv, qkv, bias)


def _group_weights(l0, l1, l2):
    m = jnp.maximum(jnp.maximum(l0, l1), l2)
    e0, e1, e2 = jnp.exp(l0 - m), jnp.exp(l1 - m), jnp.exp(l2 - m)
    inv = 1.0 / (e0 + e1 + e2)
    return e0 * inv, e1 * inv, e2 * inv


def _sgu_forward(uvf, gs, wt_ref, bst_ref, mixed_s, tm):
    z, t = _gelu(uvf)
    u, v = z[:, :SGU_W], z[:, SGU_W:]
    rv = _rms(v)
    vnb = (v * rv * gs).astype(BF16)
    for ci in range(tm // 128):
        for g in range(4):
            rs, cs = slice(ci * 128, (ci + 1) * 128), slice(g * 128, (g + 1) * 128)
            mixed_s[rs, cs] = _dot(wt_ref[g], vnb[rs, cs]) + bst_ref[:, g:g + 1]
    return u, v, rv, vnb, t


def _fwd_mid(x, os_, ls_, uv, gl, wt, bst, g_sgu, b_gate, wt_ba, wt_bs, w_out, gather=None, tm=512):
    S = x.shape[0]
    dils = [d for _, d in DIL_GROUPS]
    n = 0 if gather is None else gather.n
    last = S // tm - 1

    def body(*refs):
        (x_ref, o0, o1, o2, l0, l1, l2, uv_ref, gl_ref, wt_ref, bst_ref, gs_ref, bg_ref, wba_ref, wbs_ref,
         wo_ref) = refs[:16]
        ya_ref, ys_ref, ba_ref, bs_ref, mg_ref, h1_ref = refs[16 + n:22 + n]
        mixed_s, il_s = refs[22 + 2 * n:24 + 2 * n]
        comm = (refs[16:16 + n], refs[22 + n:22 + 2 * n], refs[24 + 2 * n:])
        if gather is not None:
            pl.when(pl.program_id(0) == 0)(lambda: gather.start(*comm))
        ls = [_from_dilated(r, il_s, d, tm, GROUP_W) for r, d in zip((l0, l1, l2), dils)]
        alphas = _group_weights(*ls)
        ya = jnp.zeros((tm, GROUP_W), F32)
        for a, r, d in zip(alphas, (o0, o1, o2), dils):
            ya = ya + a * _from_dilated(r, il_s, d, tm, GROUP_W)
        yab = ya.astype(BF16)
        ya_ref[...] = yab
        u, _, _, _, _ = _sgu_forward(uv_ref[...].astype(F32), gs_ref[...], wt_ref, bst_ref, mixed_s, tm)
        ysb = (u * mixed_s[...]).astype(BF16)
        ys_ref[...] = ysb
        gates = jax.nn.sigmoid(gl_ref[...].astype(F32) + bg_ref[...])
        ba = _dot_nt(yab, wba_ref[...])
        bs = _dot_nt(ysb, wbs_ref[...])
        ba_ref[...] = ba.astype(BF16)
        bs_ref[...] = bs.astype(BF16)
        mgb = (gates[:, :D] * ba + gates[:, D:] * bs).astype(BF16)
        mg_ref[...] = mgb
        h1_ref[...] = x_ref[...] + _dot(mgb, wo_ref[...])
        if gather is not None:
            pl.when(pl.program_id(0) == last)(lambda: gather.finish(*comm))

    gw = _row(tm, GROUP_W)
    dil = [_dil_spec(d, tm, GROUP_W) for d in dils]
    outs = _pallas(
        body, grid=(S // tm,), name="fwd_mid",
        in_specs=[_row(tm, D)] + dil + dil + [_row(tm, 1024), _row(tm, 2048)]
        + [_whole(t) for t in (wt, bst, g_sgu, b_gate)] + [RES] * 3 + [ANY] * n,
        out_specs=[gw, _row(tm, SGU_W), _row(tm, D), _row(tm, D), _row(tm, D), _row(tm, D)] + [ANY] * n,
        out_shape=[_sds((S, GROUP_W), BF16), _sds((S, SGU_W), BF16), _sds((S, D), BF16), _sds((S, D), BF16),
                   _sds((S, D), BF16), _sds((S, D), F32)] + ([] if gather is None else gather.out_shape),
        scratch_shapes=[pltpu.VMEM((tm, SGU_W), F32), pltpu.VMEM((2, tm, 128), F32)]
        + ([] if gather is None else gather.scratch),
        compiler_params=_params(("arbitrary",), 56),
    )(x, *os_, *ls_, uv, gl, wt, bst, g_sgu, b_gate, wt_ba, wt_bs, w_out, *([] if gather is None else gather.halves))
    return outs[:6], ([] if gather is None else gather.full(outs[6:]))


def _mem_fwd(mem, g_mem, w_kv):
    def body(m_ref, g_ref, w_ref, mb_ref, kv_ref):
        mv = m_ref[...]
        mb = (mv * _rms(mv) * g_ref[...]).astype(BF16)
        mb_ref[...] = mb
        kv_ref[...] = _dot(mb, w_ref[...]).astype(BF16)

    shapes = [_sds(mem.shape, BF16), _sds((mem.shape[0], 2 * MEM_W), BF16)]
    return _pallas(
        body, name="mem_fwd", grid=(1,), in_specs=[_whole(t) for t in (mem, g_mem, w_kv)],
        out_specs=[_whole(t) for t in shapes], out_shape=shapes, compiler_params=_params(("arbitrary",), 32),
    )(mem, g_mem, w_kv)


def _cross_probs(qh, kh):
    s = _dot_nt(qh, kh) * (MEM_HD ** -0.5)
    e = jnp.exp(s - jnp.max(s, axis=-1, keepdims=True))
    return e / jnp.sum(e, axis=-1, keepdims=True)


def _fwd_cross(h1, g_cross, w_q, kv, wt_o, tm=512):
    S = h1.shape[0]

    def body(h_ref, g_ref, wq_ref, kv_ref, wo_ref, c_ref, qc_ref, oc_ref, h2_ref):
        hv = h_ref[...]
        cb = (hv * _rms(hv) * g_ref[...]).astype(BF16)
        c_ref[...] = cb
        qcb = _dot(cb, wq_ref[...]).astype(BF16)
        qc_ref[...] = qcb
        for h in range(MEM_HEADS):
            cs = slice(h * MEM_HD, (h + 1) * MEM_HD)
            p = _cross_probs(qcb[:, cs], kv_ref[:, cs])
            oc_ref[:, cs] = _dot(p.astype(BF16), kv_ref[:, MEM_W + h * MEM_HD:MEM_W + (h + 1) * MEM_HD]).astype(BF16)
        h2_ref[...] = hv + _dot_nt(oc_ref[...], wo_ref[...])

    return _pallas(
        body, grid=(S // tm,), name="fwd_cross",
        in_specs=[_row(tm, D), _whole(g_cross), RES, _whole(kv), RES],
        out_specs=[_row(tm, D), _row(tm, MEM_W), _row(tm, MEM_W), _row(tm, D)],
        out_shape=[_sds((S, D), BF16), _sds((S, MEM_W), BF16), _sds((S, MEM_W), BF16), _sds((S, D), F32)],
        compiler_params=_params(("parallel",), 40),
    )(h1, g_cross, w_q, kv, wt_o)


def _stream_ffn_weights(wgu_hbm, wd_hbm, wbuf, wsem, i, last, visits):
    def fetch(k):
        c, slot = visits[k], k % 2
        return [pltpu.make_async_copy(src.at[pl.ds(r0, FF_CHUNK)], wbuf.at[slot, j], wsem.at[slot, j])
                for j, (src, r0) in enumerate(((wgu_hbm, c * FF_CHUNK), (wgu_hbm, D_FF + c * FF_CHUNK),
                                               (wd_hbm, c * FF_CHUNK)))]

    def prime():
        @pl.when(i == 0)
        def _():
            for cp in fetch(0):
                cp.start()

    def weights(k):
        if k + 1 < len(visits):
            for cp in fetch(k + 1):
                cp.start()
        else:
            @pl.when(i < last)
            def _():
                for cp in fetch(0):
                    cp.start()
        for cp in fetch(k):
            cp.wait()
        slot = k % 2
        return wbuf.at[slot, 0], wbuf.at[slot, 1], wbuf.at[slot, 2]

    return prime, weights


def _ffn_fwd(h2, target, g_ffn, g_final, wt_gu, w_down, tm=512):
    S = h2.shape[0]
    nch = D_FF // FF_CHUNK
    last = S // tm - 1
    assert nch % 2 == 1

    def body(h_ref, t_ref, gf_ref, gz_ref, wgu_hbm, wd_hbm,
             f_ref, act_ref, gu_ref, dh3_ref, dh3b_ref, dgz_ref, loss_ref, wbuf, wsem):
        i = pl.program_id(0)
        prime, weights = _stream_ffn_weights(wgu_hbm, wd_hbm, wbuf, wsem, i, last, list(range(nch)) + [0])
        prime()

        @pl.when(i == 0)
        def _():
            dgz_ref[...] = jnp.zeros_like(dgz_ref)
            loss_ref[...] = jnp.zeros_like(loss_ref)

        hv = h_ref[...]
        fb = (hv * _rms(hv) * gf_ref[...]).astype(BF16)
        f_ref[...] = fb
        h3 = hv
        for c in range(nch):
            cs = slice(c * FF_CHUNK, (c + 1) * FF_CHUNK)
            us = slice(D_FF + c * FF_CHUNK, D_FF + (c + 1) * FF_CHUNK)
            wg, wu, wd = weights(c)
            gt = _dot_nt(fb, wg[...])
            up = _dot_nt(fb, wu[...])
            gu_ref[:, cs] = gt.astype(BF16)
            gu_ref[:, us] = up.astype(BF16)
            actb = (gt * jax.nn.sigmoid(gt) * up).astype(BF16)
            act_ref[:, cs] = actb
            h3 = h3 + _dot(actb, wd[...])
        weights(nch)
        r3 = _rms(h3)
        gz = gz_ref[...]
        diff = h3 * r3 * gz - t_ref[...]
        dh3, dgz_rows = _rms_bwd(diff * (1.0 / D), h3, r3, gz)
        dh3_ref[...] = dh3
        dh3b_ref[...] = dh3.astype(BF16)
        dgz_ref[...] += jnp.sum(dgz_rows, axis=0, keepdims=True)
        loss_ref[...] += jnp.sum(jnp.sum(diff * diff, axis=0, keepdims=True), axis=1, keepdims=True) * (0.5 / D)

    return _pallas(
        body, grid=(S // tm,), name="ffn_fwd",
        in_specs=[_row(tm, D), _row(tm, D), _whole(g_ffn), _whole(g_final), ANY, ANY],
        out_specs=[_row(tm, D), _row(tm, D_FF), _row(tm, 2 * D_FF), _row(tm, D), _row(tm, D), _acc((1, D)),
                   _acc((1, 128))],
        out_shape=[_sds((S, D), BF16), _sds((S, D_FF), BF16), _sds((S, 2 * D_FF), BF16), _sds((S, D), F32),
                   _sds((S, D), BF16), _sds((1, D), F32), _sds((1, 128), F32)],
        scratch_shapes=[pltpu.VMEM((2, 3, FF_CHUNK, D), BF16), pltpu.SemaphoreType.DMA((2, 3))],
        compiler_params=_params(("arbitrary",), 60),
    )(h2, target, g_ffn, g_final, wt_gu, w_down)


def _ffn_bwd(h2, dh3, gu, g_ffn, wt_gu, w_down, tm=512):
    S = h2.shape[0]
    nch = D_FF // FF_CHUNK
    last = S // tm - 1

    def body(h_ref, d_ref, gu_ref, gf_ref, wgu_hbm, wd_hbm, dgu_ref, dh2_ref, dgf_ref, wbuf, wsem):
        i = pl.program_id(0)
        prime, weights = _stream_ffn_weights(wgu_hbm, wd_hbm, wbuf, wsem, i, last, list(range(nch)) + [0])
        prime()

        @pl.when(i == 0)
        def _():
            dgf_ref[...] = jnp.zeros_like(dgf_ref)

        dh3b = d_ref[...].astype(BF16)
        df = jnp.zeros((tm, D), F32)
        for c in range(nch):
            cs = slice(c * FF_CHUNK, (c + 1) * FF_CHUNK)
            us = slice(D_FF + c * FF_CHUNK, D_FF + (c + 1) * FF_CHUNK)
            wg, wu, wd = weights(c)
            dact = _dot_nt(dh3b, wd[...])
            gt, up = gu_ref[:, cs].astype(F32), gu_ref[:, us].astype(F32)
            sg = jax.nn.sigmoid(gt)
            dgt = (dact * up * (sg * (1.0 + gt * (1.0 - sg)))).astype(BF16)
            dup = (dact * (gt * sg)).astype(BF16)
            dgu_ref[:, cs] = dgt
            dgu_ref[:, us] = dup
            df = df + _dot(dgt, wg[...]) + _dot(dup, wu[...])
        weights(nch)
        hv = h_ref[...]
        gf = gf_ref[...]
        dhn, dgf_rows = _rms_bwd(df, hv, _rms(hv), gf)
        dh2_ref[...] = d_ref[...] + dhn
        dgf_ref[...] += jnp.sum(dgf_rows, axis=0, keepdims=True)

    return _pallas(
        body, grid=(S // tm,), name="ffn_bwd",
        in_specs=[_row(tm, D), _row(tm, D), _row(tm, 2 * D_FF), _whole(g_ffn), ANY, ANY],
        out_specs=[_row(tm, 2 * D_FF), _row(tm, D), _acc((1, D))],
        out_shape=[_sds((S, 2 * D_FF), BF16), _sds((S, D), F32), _sds((1, D), F32)],
        scratch_shapes=[pltpu.VMEM((2, 3, FF_CHUNK, D), BF16), pltpu.SemaphoreType.DMA((2, 3))],
        compiler_params=_params(("arbitrary",), 63),
    )(h2, dh3, gu, g_ffn, wt_gu, w_down)


def _ffn_fwd_bwd(h2, target, g_ffn, g_final, wt_gu, w_down, tm=256):
    S = h2.shape[0]
    nch = D_FF // FF_CHUNK

    def body(h_ref, t_ref, gf_ref, gz_ref, wgu_ref, wd_ref,
             f_ref, act_ref, dgu_ref, dh3b_ref, dh2_ref, dh2b_ref, dgf_ref, dgz_ref, loss_ref, gu_s):
        i = pl.program_id(0)

        @pl.when(i == 0)
        def _():
            dgf_ref[...] = jnp.zeros_like(dgf_ref)
            dgz_ref[...] = jnp.zeros_like(dgz_ref)
            loss_ref[...] = jnp.zeros_like(loss_ref)

        def weights(c):
            return (wgu_ref.at[pl.ds(c * FF_CHUNK, FF_CHUNK)], wgu_ref.at[pl.ds(D_FF + c * FF_CHUNK, FF_CHUNK)],
                    wd_ref.at[pl.ds(c * FF_CHUNK, FF_CHUNK)])

        hv = h_ref[...]
        r2 = _rms(hv)
        gf = gf_ref[...]
        fb = (hv * r2 * gf).astype(BF16)
        f_ref[...] = fb
        h3 = hv
        for c in range(nch):
            cs = slice(c * FF_CHUNK, (c + 1) * FF_CHUNK)
            us = slice(D_FF + c * FF_CHUNK, D_FF + (c + 1) * FF_CHUNK)
            wg, wu, wd = weights(c)
            gt = _dot_nt(fb, wg[...])
            up = _dot_nt(fb, wu[...])
            gu_s[:, cs] = gt
            gu_s[:, us] = up
            actb = (gt * jax.nn.sigmoid(gt) * up).astype(BF16)
            act_ref[:, cs] = actb
            h3 = h3 + _dot(actb, wd[...])
        r3 = _rms(h3)
        gz = gz_ref[...]
        diff = h3 * r3 * gz - t_ref[...]
        dy = diff * (1.0 / D)
        dh3, dgz_rows = _rms_bwd(dy, h3, r3, gz)
        dh3b = dh3.astype(BF16)
        dh3b_ref[...] = dh3b
        df = jnp.zeros((tm, D), F32)
        for c in range(nch):
            cs = slice(c * FF_CHUNK, (c + 1) * FF_CHUNK)
            us = slice(D_FF + c * FF_CHUNK, D_FF + (c + 1) * FF_CHUNK)
            wg, wu, wd = weights(c)
            dact = _dot_nt(dh3b, wd[...])
            gt, up = gu_s[:, cs], gu_s[:, us]
            sg = jax.nn.sigmoid(gt)
            dgt = (dact * up * (sg * (1.0 + gt * (1.0 - sg)))).astype(BF16)
            dup = (dact * (gt * sg)).astype(BF16)
            dgu_ref[:, cs] = dgt
            dgu_ref[:, us] = dup
            df = df + _dot(dgt, wg[...]) + _dot(dup, wu[...])
        dhn, dgf_rows = _rms_bwd(df, hv, r2, gf)
        dh2 = dh3 + dhn
        dh2_ref[...] = dh2
        dh2b_ref[...] = dh2.astype(BF16)
        dgf_ref[...] += jnp.sum(dgf_rows, axis=0, keepdims=True)
        dgz_ref[...] += jnp.sum(dgz_rows, axis=0, keepdims=True)
        loss_ref[...] += jnp.sum(jnp.sum(diff * diff, axis=0, keepdims=True), axis=1, keepdims=True) * (0.5 / D)

    return _pallas(
        body, grid=(S // tm,), name="ffn_fwd_bwd",
        in_specs=[_row(tm, D), _row(tm, D), _whole(g_ffn), _whole(g_final), RES, RES],
        out_specs=[_row(tm, D), _row(tm, D_FF), _row(tm, 2 * D_FF), _row(tm, D), _row(tm, D), _row(tm, D),
                   _acc((1, D)), _acc((1, D)), _acc((1, 128))],
        out_shape=[_sds((S, D), BF16), _sds((S, D_FF), BF16), _sds((S, 2 * D_FF), BF16), _sds((S, D), BF16),
                   _sds((S, D), F32), _sds((S, D), BF16), _sds((1, D), F32), _sds((1, D), F32), _sds((1, 128), F32)],
        scratch_shapes=[pltpu.VMEM((tm, 2 * D_FF), F32)],
        compiler_params=_params(("arbitrary",), 56),
    )(h2, target, g_ffn, g_final, wt_gu, w_down)


def _bwd_cross(dh2, h1, qc, g_cross, w_q, kv, wt_o, comm=None, tm=512):
    S = h1.shape[0]
    n = 0 if comm is None else comm.n
    last = S // tm - 1

    def body(*refs):
        d_ref, h_ref, qc_ref, g_ref, wq_ref, kv_ref, wo_ref = refs[:7]
        dqc_ref, dh1_ref, dh1b_ref, dkv_ref, dg_ref = refs[7 + n:12 + n]
        cargs = (refs[7:7 + n], refs[12 + n:12 + 2 * n], refs[12 + 2 * n:])
        i = pl.program_id(0)

        @pl.when(i == 0)
        def _():
            dkv_ref[...] = jnp.zeros_like(dkv_ref)
            dg_ref[...] = jnp.zeros_like(dg_ref)
            if comm is not None:
                comm.start(*cargs)

        dh2 = d_ref[...]
        doc = _dot(dh2.astype(BF16), wo_ref[...])
        qcb = qc_ref[...]
        for h in range(MEM_HEADS):
            cs = slice(h * MEM_HD, (h + 1) * MEM_HD)
            vs = slice(MEM_W + h * MEM_HD, MEM_W + (h + 1) * MEM_HD)
            qh, kh, vh = qcb[:, cs], kv_ref[:, cs], kv_ref[:, vs]
            p = _cross_probs(qh, kh)
            dohb = doc[:, cs].astype(BF16)
            dp = _dot_nt(dohb, vh)
            dsb = (p * (dp - jnp.sum(dp * p, axis=-1, keepdims=True)) * (MEM_HD ** -0.5)).astype(BF16)
            dqc_ref[:, cs] = _dot(dsb, kh).astype(BF16)
            dkv_ref[:, cs] += _dot_tn(dsb, qh)
            dkv_ref[:, vs] += _dot_tn(p.astype(BF16), dohb)
        dc = _dot_nt(dqc_ref[...], wq_ref[...])
        hv = h_ref[...]
        dhn, dg_rows = _rms_bwd(dc, hv, _rms(hv), g_ref[...])
        dh1 = dh2 + dhn
        dh1_ref[...] = dh1
        dh1b_ref[...] = dh1.astype(BF16)
        dg_ref[...] += jnp.sum(dg_rows, axis=0, keepdims=True)
        if comm is not None:
            pl.when(i == last)(lambda: comm.finish(*cargs))

    outs = _pallas(
        body, grid=(S // tm,), name="bwd_cross",
        in_specs=[_row(tm, D), _row(tm, D), _row(tm, MEM_W), _whole(g_cross), RES, _whole(kv), RES] + [ANY] * n,
        out_specs=[_row(tm, MEM_W), _row(tm, D), _row(tm, D), _acc((256, 2 * MEM_W)), _acc((1, D))] + [ANY] * n,
        out_shape=[_sds((S, MEM_W), BF16), _sds((S, D), F32), _sds((S, D), BF16), _sds((256, 2 * MEM_W), F32),
                   _sds((1, D), F32)] + ([] if comm is None else comm.out_shape),
        scratch_shapes=[] if comm is None else comm.scratch,
        compiler_params=_params(("arbitrary",), 48),
    )(dh2, h1, qc, g_cross, w_q, kv, wt_o, *([] if comm is None else comm.ins))
    return outs[:5], outs[5:]


def _mem_bwd(dkv, mem, mb, g_mem, w_kv):
    def body(dkv_ref, m_ref, mb_ref, g_ref, w_ref, dw_ref, dwb_ref, dg_ref):
        dkvb = dkv_ref[...].astype(BF16)
        dw = _dot_tn(mb_ref[...], dkvb)
        dw_ref[...] = dw
        dwb_ref[...] = dw.astype(BF16)
        dm = _dot_nt(dkvb, w_ref[...])
        mv = m_ref[...]
        dg_ref[...] = jnp.sum(dm * mv * _rms(mv), axis=0, keepdims=True)

    shapes = [_sds((D, 2 * MEM_W), F32), _sds((D, 2 * MEM_W), BF16), _sds((1, D), F32)]
    return _pallas(
        body, name="mem_bwd", grid=(1,), in_specs=[_whole(t) for t in (dkv, mem, mb, g_mem, w_kv)],
        out_specs=[_whole(t) for t in shapes], out_shape=shapes, compiler_params=_params(("arbitrary",), 40),
    )(dkv, mem, mb, g_mem, w_kv)


def _bwd_mid(dh1, gl, ba, bs, uv, ls_, ya, wt, bst, g_sgu, b_gate, wt_ba, wt_bs, w_out, tm=512):
    S = dh1.shape[0]
    dils = [d for _, d in DIL_GROUPS]

    def body(d_ref, gl_ref, ba_ref, bs_ref, uv_ref, l0, l1, l2, ya_ref,
             wt_ref, bst_ref, gs_ref, bg_ref, wba_ref, wbs_ref, wo_ref,
             dba_ref, dbs_ref, dgl_ref, duv_ref, do0, do1, do2, c0, c1, c2,
             dbg_ref, dgs_ref, dws_ref, dbsa_ref, mixed_s, dvn_s, il_s):
        i = pl.program_id(0)

        @pl.when(i == 0)
        def _():
            for r in (dbg_ref, dgs_ref, dws_ref, dbsa_ref):
                r[...] = jnp.zeros_like(r)

        dm = _dot_nt(d_ref[...].astype(BF16), wo_ref[...])
        gates = jax.nn.sigmoid(gl_ref[...].astype(F32) + bg_ref[...])
        g0, g1 = gates[:, :D], gates[:, D:]
        dbab = (dm * g0).astype(BF16)
        dbsb = (dm * g1).astype(BF16)
        dba_ref[...] = dbab
        dbs_ref[...] = dbsb
        dg0 = dm * ba_ref[...].astype(F32) * g0 * (1.0 - g0)
        dg1 = dm * bs_ref[...].astype(F32) * g1 * (1.0 - g1)
        dgl_ref[:, :D] = dg0.astype(BF16)
        dgl_ref[:, D:] = dg1.astype(BF16)
        dbg_ref[:, :D] += jnp.sum(dg0, axis=0, keepdims=True)
        dbg_ref[:, D:] += jnp.sum(dg1, axis=0, keepdims=True)
        dya = _dot(dbab, wba_ref[...])
        dys = _dot(dbsb, wbs_ref[...])

        uvf = uv_ref[...].astype(F32)
        gs = gs_ref[...]
        u, v, rv, vnb, t = _sgu_forward(uvf, gs, wt_ref, bst_ref, mixed_s, tm)
        du = dys * mixed_s[...]
        dmixed = dys * u
        for ci in range(tm // 128):
            for g in range(4):
                rs, cs = slice(ci * 128, (ci + 1) * 128), slice(g * 128, (g + 1) * 128)
                dmx = dmixed[rs, cs]
                dmxb = dmx.astype(BF16)
                dvn_s[rs, cs] = _dot_tn(wt_ref[g], dmxb)
                dws_ref[g] += _dot_nt(dmxb, vnb[rs, cs])
                dbsa_ref[g] += dmx
        dv, dgs_rows = _rms_bwd(dvn_s[...], v, rv, gs)
        dgs_ref[...] += jnp.sum(dgs_rows, axis=0, keepdims=True)
        gg = _gelu_grad(uvf, t)
        duv_ref[:, :SGU_W] = (du * gg[:, :SGU_W]).astype(BF16)
        duv_ref[:, SGU_W:] = (dv * gg[:, SGU_W:]).astype(BF16)

        alphas = _group_weights(*[_from_dilated(r, il_s, d, tm, GROUP_W) for r, d in zip((l0, l1, l2), dils)])
        prod = dya * ya_ref[...].astype(F32)
        _, masks = _head_masks()
        hs = jnp.zeros_like(prod)
        for h in range(4):
            sh = jnp.sum(jnp.where(masks[h], prod, 0.0), axis=-1, keepdims=True)
            hs = jnp.where(masks[h], sh, hs)
        for a, d, do_ref, c_ref in zip(alphas, dils, (do0, do1, do2), (c0, c1, c2)):
            for val, out in ((a * dya, do_ref), (a * hs, c_ref)):
                if d == 1:
                    out[0] = val.astype(out.dtype)
                else:
                    def write(r, j, piece, out=out):
                        out[r, :, j * 128:(j + 1) * 128] = piece.astype(out.dtype)
                    _to_dilated(val, il_s, d, write)

    gw = _row(tm, GROUP_W)
    dil = [_dil_spec(d, tm, GROUP_W) for d in dils]
    return _pallas(
        body, grid=(S // tm,), name="bwd_mid",
        in_specs=[_row(tm, D), _row(tm, 2048), _row(tm, D), _row(tm, D), _row(tm, 1024)] + dil + [gw]
        + [_whole(t) for t in (wt, bst, g_sgu, b_gate)] + [RES] * 3,
        out_specs=[_row(tm, D), _row(tm, D), _row(tm, 2048), _row(tm, 1024)] + dil + dil
        + [_acc((1, 2048)), _acc((1, SGU_W)), _acc((4, 128, 128)), _acc((4, 128, 128))],
        out_shape=[_sds((S, D), BF16), _sds((S, D), BF16), _sds((S, 2048), BF16), _sds((S, 1024), BF16)]
        + [_sds((d, S // d, GROUP_W), BF16) for d in dils] + [_sds((d, S // d, GROUP_W), F32) for d in dils]
        + [_sds((1, 2048), F32), _sds((1, SGU_W), F32), _sds((4, 128, 128), F32), _sds((4, 128, 128), F32)],
        scratch_shapes=[pltpu.VMEM((tm, SGU_W), F32), pltpu.VMEM((tm, SGU_W), F32), pltpu.VMEM((2, tm, 128), F32)],
        compiler_params=_params(("arbitrary",), 60),
    )(dh1, gl, ba, bs, uv, *ls_, ya, wt, bst, g_sgu, b_gate, wt_ba, wt_bs, w_out)


def _attn_bwd(qkv, do, lse, corr, g):
    d, L, _ = qkv.shape
    nb = L // BLK
    NB = min(ATT_NB, nb)
    W = NB * BLK
    nsteps = nb // NB
    bias = jnp.asarray(_attn_bias(g).reshape(4 * BLK, 2 * BLK))

    def body(q_ref, kc_ref, kp_ref, vc_ref, vp_ref, do_ref, l_ref, c_ref, qn_ref, don_ref, ln_ref, cn_ref, b_ref,
             out_ref, dk_s, dv_s):
        st = pl.program_id(1)
        k_all = jnp.concatenate([kp_ref[...], kc_ref[...]], axis=0)
        v_all = jnp.concatenate([vp_ref[...], vc_ref[...]], axis=0)
        lane, masks = _head_masks()
        dk_s[...] = jnp.zeros_like(dk_s)
        dv_s[...] = jnp.zeros_like(dv_s)

        def block_terms(qs, dos, kk, vv, bias_v, lse_c, corr_c):
            s = _dot_nt(qs, kk) * 0.125 + bias_v
            p = jnp.exp(s - lse_c)
            dsb = (p * (_dot_nt(dos, vv) - corr_c) * 0.125).astype(BF16)
            return dsb, p.astype(BF16)

        for b in range(NB):
            rows = slice(b * BLK, (b + 1) * BLK)
            keys = slice(b * BLK, (b + 2) * BLK)
            kk, vv = k_all[keys], v_all[keys]
            qs, dos = _stack_heads(q_ref[rows, :], masks), _stack_heads(do_ref[rows, :], masks)
            bias_v = b_ref[...]
            if b == 0:
                bias_v = bias_v + jnp.where((st == 0) & (lane < BLK), NEG, 0.0).astype(F32)
            dsb, pb = block_terms(qs, dos, kk, vv, bias_v, _stack_cols(l_ref, rows), _stack_cols(c_ref, rows))
            out_ref[rows, 0:GROUP_W] = _unstack_heads(_dot(dsb, kk), masks).astype(BF16)
            dk_s[keys, :] += _dot_tn(dsb, qs)
            dv_s[keys, :] += _dot_tn(pb, dos)

        @pl.when(st < nsteps - 1)
        def _():
            last = slice(NB * BLK, (NB + 1) * BLK)
            qs, dos = _stack_heads(qn_ref[...], masks), _stack_heads(don_ref[...], masks)
            every = slice(None)
            dsb, pb = block_terms(qs, dos, k_all[last], v_all[last], b_ref[:, :BLK],
                                  _stack_cols(ln_ref, every), _stack_cols(cn_ref, every))
            dk_s[last, :] += _dot_tn(dsb, qs)
            dv_s[last, :] += _dot_tn(pb, dos)

        out_ref[:, GROUP_W:2 * GROUP_W] = dk_s[BLK:, :].astype(BF16)
        out_ref[:, 2 * GROUP_W:] = dv_s[BLK:, :].astype(BF16)

    def wide(col, w=GROUP_W):
        return pl.BlockSpec((None, W, w), lambda r, s: (r, s, col))

    def before(col):
        return pl.BlockSpec((None, BLK, GROUP_W), lambda r, s: (r, jnp.maximum(s * NB - 1, 0), col))

    def after(col):
        return pl.BlockSpec((None, BLK, GROUP_W), lambda r, s: (r, jnp.minimum((s + 1) * NB, nb - 1), col))

    return _pallas(
        body, grid=(d, nsteps), name=f"attn_bwd_g{g}",
        in_specs=[wide(0), wide(1), before(1), wide(2), before(2), wide(0), wide(0), wide(0),
                  after(0), after(0), after(0), after(0), pl.BlockSpec((4 * BLK, 2 * BLK), lambda r, s: (0, 0))],
        out_specs=wide(0, 768),
        out_shape=_sds((d, L, 768), BF16),
        scratch_shapes=[pltpu.VMEM(((NB + 1) * BLK, GROUP_W), F32), pltpu.VMEM(((NB + 1) * BLK, GROUP_W), F32)],
        compiler_params=_params(("parallel", "parallel"), 32),
    )(qkv, qkv, qkv, qkv, qkv, do, lse, corr, qkv, do, lse, corr, bias)


def _bwd_in(dqkvs, duv, dgl, dh1, x, g_mix, wt_in, tm=512):
    S = x.shape[0]
    dils = [d for _, d in DIL_GROUPS]

    def body(q0_ref, q1_ref, q2_ref, duv_ref, dgl_ref, d_ref, x_ref, g_ref, w_ref, dx_ref, dp_ref, dg_ref, il_s):
        i = pl.program_id(0)

        @pl.when(i == 0)
        def _():
            dg_ref[...] = jnp.zeros_like(dg_ref)

        for g, (d, ref) in enumerate(zip(dils, (q0_ref, q1_ref, q2_ref))):
            nat = _from_dilated(ref, il_s, d, tm, 768).astype(BF16)
            for part in range(3):
                col = part * 768 + g * 256
                dp_ref[:, col:col + 256] = nat[:, part * 256:(part + 1) * 256]
        dp_ref[:, 2304:3328] = duv_ref[...]
        dp_ref[:, 3328:5376] = dgl_ref[...]
        da = _dot(dp_ref[...], w_ref[...])
        xv = x_ref[...]
        dxn, dg_rows = _rms_bwd(da, xv, _rms(xv), g_ref[...])
        dx_ref[...] = d_ref[...] + dxn
        dg_ref[...] += jnp.sum(dg_rows, axis=0, keepdims=True)

    return _pallas(
        body, grid=(S // tm,), name="bwd_in",
        in_specs=[_dil_spec(d, tm, 768) for d in dils] + [_row(tm, 1024), _row(tm, 2048), _row(tm, D), _row(tm, D),
                                                          _whole(g_mix), RES],
        out_specs=[_row(tm, D), _row(tm, 5376), _acc((1, D))],
        out_shape=[_sds((S, D), F32), _sds((S, 5376), BF16), _sds((1, D), F32)],
        scratch_shapes=[pltpu.VMEM((6, tm, 128), F32)],
        compiler_params=_params(("arbitrary",), 60),
    )(*dqkvs, duv, dgl, dh1, x, g_mix, wt_in)


def _tn_matmul(a, b, name, tk, ts=2048, comm=None):
    S, K = a.shape
    N = b.shape[1]
    n = 0 if comm is None else comm.n
    nk, ns = K // tk, S // ts

    def body(*refs):
        a_ref, b_ref, o_ref, ob_ref = refs[0], refs[1], refs[2 + n], refs[3 + n]
        cargs = (refs[2:2 + n], refs[4 + n:4 + 2 * n], refs[4 + 2 * n:])
        k, s = pl.program_id(0), pl.program_id(1)
        if comm is not None:
            pl.when((k == 0) & (s == 0))(lambda: comm.start(*cargs))

        @pl.when(s == 0)
        def _():
            o_ref[...] = jnp.zeros_like(o_ref)

        o_ref[...] += _dot_tn(a_ref[...].astype(BF16), b_ref[...].astype(BF16))

        @pl.when(s == ns - 1)
        def _():
            ob_ref[...] = o_ref[...].astype(BF16)

        if comm is not None:
            pl.when((k == nk - 1) & (s == ns - 1))(lambda: comm.finish(*cargs))

    tile = pl.BlockSpec((tk, N), lambda k, s: (k, 0))
    outs = _pallas(
        body, grid=(nk, ns), name=name,
        in_specs=[pl.BlockSpec((ts, tk), lambda k, s: (s, k)), pl.BlockSpec((ts, N), lambda k, s: (s, 0))] + [ANY] * n,
        out_specs=[tile, tile] + [ANY] * n,
        out_shape=[_sds((K, N), F32), _sds((K, N), BF16)] + ([] if comm is None else comm.out_shape),
        scratch_shapes=[] if comm is None else comm.scratch,
        compiler_params=_params(("arbitrary", "arbitrary"), 56),
    )(a, b, *([] if comm is None else comm.ins))
    pair = (outs[0], outs[1])
    return pair if comm is None else (pair, outs[2:])


def _chip_peers(x, y):
    return [(1 - x, y), (x, 1 - y), (1 - x, 1 - y)]


STAGE_BYTES = 2 << 20


def _chunk_plan(shapes, itemsize):
    plan = []
    for i, (rows, w) in enumerate(shapes):
        ch = max(16, min(rows, (STAGE_BYTES // (w * itemsize)) // 16 * 16))
        while rows % ch:
            ch -= 16
        plan += [(i, r0, ch) for r0 in range(0, rows, ch)]
    return plan


def _remote(src, dst, ssem, rsem, dev):
    return pltpu.make_async_remote_copy(src_ref=src, dst_ref=dst, send_sem=ssem, recv_sem=rsem, device_id=dev,
                                        device_id_type=MESH)


class _Gather:
    def __init__(self, shards):
        self.n = len(shards)
        self.shards = shards
        self.halves = [s.reshape(2, s.shape[0] // 2, s.shape[1]) for s in shards]
        self.plan = _chunk_plan([h.shape[1:] for h in self.halves], 2)
        self.out_shape = [_sds((4,) + h.shape, BF16) for h in self.halves]
        n = self.n
        self.scratch = [pltpu.SemaphoreType.DMA((6 * n,)), pltpu.SemaphoreType.DMA((6 * n,)),
                        pltpu.SemaphoreType.DMA((2,)), pltpu.SemaphoreType.DMA((2,)),
                        pltpu.VMEM((2, max(p[2] for p in self.plan), max(h.shape[2] for h in self.halves)), BF16)]

    def full(self, outs):
        return [o.reshape(4 * s.shape[0], s.shape[1]) for o, s in zip(outs, self.shards)]

    def _sends(self, ins, outs, ssem, rsem):
        x, y, c = lax.axis_index("x"), lax.axis_index("y"), lax.axis_index("c")
        me = 2 * x + y
        return [_remote(ins[i].at[c], outs[i].at[me, c], ssem.at[6 * i + k], rsem.at[6 * i + k], (px, py, c))
                for i in range(self.n) for k, (px, py) in enumerate(_chip_peers(x, y))]

    def start(self, ins, outs, scratch):
        ssem, rsem, lsem, osem, buf = scratch
        me = 2 * lax.axis_index("x") + lax.axis_index("y")
        for cp in self._sends(ins, outs, ssem, rsem):
            cp.start()
        pending = {}
        for i, r0, ch in self.plan:
            for h in range(2):
                if h in pending:
                    pending[h].wait()
                stage = buf.at[h, pl.ds(0, ch), pl.ds(0, self.halves[i].shape[2])]
                ld = pltpu.make_async_copy(ins[i].at[h, pl.ds(r0, ch)], stage, lsem.at[h])
                ld.start()
                ld.wait()
                st = pltpu.make_async_copy(stage, outs[i].at[me, h, pl.ds(r0, ch)], osem.at[h])
                st.start()
                pending[h] = st
        for st in pending.values():
            st.wait()

    def finish(self, ins, outs, scratch):
        ssem, rsem = scratch[:2]
        x, y, c = lax.axis_index("x"), lax.axis_index("y"), lax.axis_index("c")
        chips = _chip_peers(x, y)
        sib = (x, y, 1 - c)
        forwards = []
        for i in range(self.n):
            for k, (px, py) in enumerate(chips):
                landed = outs[i].at[2 * px + py, c]
                _remote(landed, landed, ssem.at[6 * i + k], rsem.at[6 * i + k], (px, py, c)).wait_recv()
                cp = _remote(landed, landed, ssem.at[6 * i + 3 + k], rsem.at[6 * i + 3 + k], sib)
                cp.start()
                forwards.append(cp)
        for i in range(self.n):
            for k, (px, py) in enumerate(chips):
                passed = outs[i].at[2 * px + py, 1 - c]
                _remote(passed, passed, ssem.at[6 * i + 3 + k], rsem.at[6 * i + 3 + k], sib).wait_recv()
        for cp in self._sends(ins, outs, ssem, rsem) + forwards:
            cp.wait_send()


def _gather_weights(shards):
    gt = _Gather(shards)
    n = gt.n

    def body(*refs):
        ins, outs, scratch = refs[:n], refs[n:2 * n], refs[2 * n:]
        gt.start(ins, outs, scratch)
        gt.finish(ins, outs, scratch)

    outs = _pallas(
        body, name="gather_weights", in_specs=[ANY] * n, out_specs=[ANY] * n, out_shape=gt.out_shape,
        scratch_shapes=gt.scratch, compiler_params=pltpu.CompilerParams(vmem_limit_bytes=32 << 20),
    )(*gt.halves)
    return gt.full(outs)


def _swap_halves(grads):
    n = len(grads)
    view = lambda g: g.reshape(4, 2, g.shape[0] // 8, g.shape[1])
    g4f = [view(g) for g, _ in grads]
    g4 = [view(gb) for _, gb in grads]

    def body(*refs):
        ins, got = refs[:n], refs[n:2 * n]
        ssem, rsem = refs[2 * n:]
        x, y, c = lax.axis_index("x"), lax.axis_index("y"), lax.axis_index("c")
        sib = (x, y, 1 - c)
        cps = []
        for i in range(n):
            rc = _remote(ins[i].at[:, 1 - c], got[i], ssem.at[i], rsem.at[i], sib)
            rc.start()
            cps.append(rc)
        for cp in cps:
            cp.wait()

    half = [_sds((4, g.shape[2], g.shape[3]), BF16) for g in g4]
    got = _pallas(
        body, name="swap_halves", in_specs=[ANY] * n, out_specs=[ANY] * n, out_shape=half,
        scratch_shapes=[pltpu.SemaphoreType.DMA((n,)), pltpu.SemaphoreType.DMA((n,))],
    )(*g4)
    return g4f, got


def _chip_sum(g4, got, name):
    _, _, R, W = g4.shape
    tr = _tile(R, max(16, min(512, (1 << 18) // W // 16 * 16)))
    c = lax.axis_index("c").astype(jnp.int32).reshape(1)

    def body(c_ref, a_ref, b_ref, s_ref, sb_ref):
        s = a_ref[...] + b_ref[...].astype(F32)
        s_ref[...] = s
        sb_ref[...] = s.astype(BF16)

    plain = pl.BlockSpec((None, tr, W), lambda j, t, c_ref: (j, t, 0))
    return _pallas(
        body, name=name,
        grid_spec=pltpu.PrefetchScalarGridSpec(
            num_scalar_prefetch=1, grid=(4, R // tr),
            in_specs=[pl.BlockSpec((None, None, tr, W), lambda j, t, c_ref: (j, c_ref[0], t, 0)), plain],
            out_specs=[plain, plain]),
        out_shape=[_sds((4, R, W), F32), _sds((4, R, W), BF16)],
        compiler_params=_params(("parallel", "parallel"), 32),
    )(c, g4, got)


class _Scatter:
    def __init__(self, sums_b):
        self.n = len(sums_b)
        self.ins = list(sums_b)
        self.out_shape = [_sds((3,) + s.shape[1:], BF16) for s in sums_b]
        self.scratch = [pltpu.SemaphoreType.DMA((3 * self.n,)), pltpu.SemaphoreType.DMA((3 * self.n,))]

    def _copies(self, ins, outs, scratch):
        ssem, rsem = scratch
        x, y, c = lax.axis_index("x"), lax.axis_index("y"), lax.axis_index("c")
        return [_remote(ins[i].at[2 * px + py], outs[i].at[k], ssem.at[3 * i + k], rsem.at[3 * i + k], (px, py, c))
                for i in range(self.n) for k, (px, py) in enumerate(_chip_peers(x, y))]

    def start(self, ins, outs, scratch):
        for cp in self._copies(ins, outs, scratch):
            cp.start()

    def finish(self, ins, outs, scratch):
        for cp in self._copies(ins, outs, scratch):
            cp.wait()


def _scatter_partials(sc):
    n = sc.n

    def body(*refs):
        args = (refs[:n], refs[n:2 * n], refs[2 * n:])
        sc.start(*args)
        sc.finish(*args)

    return _pallas(
        body, name="scatter_partials", in_specs=[ANY] * n, out_specs=[ANY] * n, out_shape=sc.out_shape,
        scratch_shapes=sc.scratch,
    )(*sc.ins)


class _Reduce:
    def __init__(self, grads, names):
        self.names = names
        g4, got = _swap_halves(grads)
        self.sums, sums_b = [], []
        for nm, g, t in zip(names, g4, got):
            s_, sb_ = _chip_sum(g, t, f"chip_sum_{nm}")
            self.sums.append(s_)
            sums_b.append(sb_)
        self.scatter = _Scatter(sums_b)

    def collect(self, parts):
        return [_mesh_sum(s, p, f"mesh_sum_{nm}") for nm, s, p in zip(self.names, self.sums, parts)]


def _mesh_sum(sums, parts, name):
    _, R, W = sums.shape
    tr = _tile(R, max(16, min(512, (1 << 18) // W // 16 * 16)))
    me = (2 * lax.axis_index("x") + lax.axis_index("y")).astype(jnp.int32).reshape(1)

    def body(me_ref, m_ref, p_ref, o_ref):
        o_ref[...] = m_ref[...] + p_ref[0].astype(F32) + p_ref[1].astype(F32) + p_ref[2].astype(F32)

    return _pallas(
        body, name=name,
        grid_spec=pltpu.PrefetchScalarGridSpec(
            num_scalar_prefetch=1, grid=(R // tr,),
            in_specs=[pl.BlockSpec((None, tr, W), lambda i, me_ref: (me_ref[0], i, 0)),
                      pl.BlockSpec((3, tr, W), lambda i, me_ref: (0, i, 0))],
            out_specs=pl.BlockSpec((tr, W), lambda i, me_ref: (i, 0))),
        out_shape=_sds((R, W), F32), compiler_params=_params(("parallel",), 32),
    )(me, sums, parts)


def _share_halves(reduced):
    n = len(reduced)
    plan = _chunk_plan([r.shape for r in reduced], 4)
    max_rows = max(p[2] for p in plan)
    max_w = max(r.shape[1] for r in reduced)

    def body(*refs):
        ins, outs = refs[:n], refs[n:2 * n]
        ssem, rsem, lsem, osem, buf = refs[2 * n:]
        x, y, c = lax.axis_index("x"), lax.axis_index("y"), lax.axis_index("c")
        sib = (x, y, 1 - c)
        pending = {}
        for k, (i, r0, ch) in enumerate(plan):
            slot = k % 2
            if slot in pending:
                rc, lc = pending[slot]
                rc.wait_send()
                lc.wait()
            stage = buf.at[slot, pl.ds(0, ch), pl.ds(0, reduced[i].shape[1])]
            ld = pltpu.make_async_copy(ins[i].at[pl.ds(r0, ch)], stage, lsem.at[slot])
            ld.start()
            ld.wait()
            place = outs[i].at[c, pl.ds(r0, ch)]
            rc = _remote(stage, place, ssem.at[slot], rsem.at[i], sib)
            lc = pltpu.make_async_copy(stage, place, osem.at[slot])
            rc.start()
            lc.start()
            pending[slot] = (rc, lc)
        for rc, lc in pending.values():
            rc.wait_send()
            lc.wait()
        for i in range(n):
            theirs = outs[i].at[1 - c]
            _remote(theirs, theirs, ssem.at[0], rsem.at[i], sib).wait_recv()

    outs = _pallas(
        body, name="share_halves", in_specs=[ANY] * n, out_specs=[ANY] * n,
        out_shape=[_sds((2,) + r.shape, F32) for r in reduced],
        scratch_shapes=[pltpu.SemaphoreType.DMA((2,)), pltpu.SemaphoreType.DMA((n,)), pltpu.SemaphoreType.DMA((2,)),
                        pltpu.SemaphoreType.DMA((2,)), pltpu.VMEM((2, max_rows, max_w), F32)],
        compiler_params=pltpu.CompilerParams(vmem_limit_bytes=32 << 20),
    )(*reduced)
    return [o.reshape(2 * r.shape[0], r.shape[1]) for o, r in zip(outs, reduced)]


def _tile(rows, cap=256):
    t = min(rows, cap) // 16 * 16
    while rows % t:
        t -= 16
    return t


def _elementwise(fn, ins, out_dtypes, name):
    R, W = ins[0].shape
    tr = _tile(R, max(8, min(512, (1 << 18) // W // 8 * 8)))

    def body(*refs):
        outs = fn(*[r[...] for r in refs[:len(ins)]])
        for o_ref, o in zip(refs[len(ins):], outs):
            o_ref[...] = o.astype(o_ref.dtype)

    return _pallas(
        body, grid=(R // tr,), name=name, in_specs=[_row(tr, W)] * len(ins), out_specs=[_row(tr, W)] * len(out_dtypes),
        out_shape=[_sds((R, W), dt) for dt in out_dtypes],
        compiler_params=_params(("parallel",), 48),
    )(*ins)


def _adamw(w, g, m, v):
    m = B1 * m + (1.0 - B1) * g
    v = B2 * v + (1.0 - B2) * (g * g)
    m_hat = m / (1.0 - B1 ** STEP)
    v_hat = v / (1.0 - B2 ** STEP)
    return -LR * (m_hat / (jnp.sqrt(v_hat) + AEPS) + WD * w), m, v


def _adam_small(ws, ms, vs, parts, loss_part):
    n = len(ws)
    sent = list(parts) + [loss_part]
    ns = n + 1

    def body(*refs):
        w_refs, m_refs, v_refs = refs[:n], refs[n:2 * n], refs[2 * n:3 * n]
        p_refs = refs[3 * n:3 * n + ns]
        outs = refs[3 * n + ns:3 * n + ns + 4 * n + 1]
        g_refs, d_refs, nm_refs, nv_refs, loss_ref = outs[:n], outs[n:2 * n], outs[2 * n:3 * n], outs[3 * n:4 * n], outs[4 * n]
        all_s = refs[3 * n + ns + 4 * n + 1:3 * n + ns + 4 * n + 1 + ns]
        ssem, rsem = refs[-2:]
        x, y, c = lax.axis_index("x"), lax.axis_index("y"), lax.axis_index("c")
        me = 4 * x + 2 * y + c
        for i in range(ns):
            all_s[i][me] = p_refs[i][...]
        cps = []
        for rel in range(1, 8):
            peer = (1 - x if rel & 4 else x, 1 - y if rel & 2 else y, 1 - c if rel & 1 else c)
            for i in range(ns):
                k = (rel - 1) * ns + i
                mine = all_s[i].at[me]
                rc = _remote(mine, mine, ssem.at[k], rsem.at[k], peer)
                rc.start()
                cps.append((rc, i, k, 4 * peer[0] + 2 * peer[1] + peer[2]))
        for rc, i, k, peer_slot in cps:
            rc.wait_send()
            theirs = all_s[i].at[peer_slot]
            _remote(theirs, theirs, ssem.at[k], rsem.at[k], (x, y, c)).wait_recv()

        def total(i):
            t = all_s[i][0]
            for k in range(1, 8):
                t = t + all_s[i][k]
            return t

        for i in range(n):
            g = total(i)
            g_refs[i][...] = g
            d_refs[i][...], nm_refs[i][...], nv_refs[i][...] = _adamw(w_refs[i][...], g, m_refs[i][...], v_refs[i][...])
        loss_ref[...] = total(n)

    shapes = [_sds(w.shape, F32) for w in ws] * 4 + [_sds(loss_part.shape, F32)]
    ins = [*ws, *ms, *vs, *sent]
    outs = _pallas(
        body, name="adam_small", grid=(1,), in_specs=[_whole(t) for t in ins], out_specs=[_whole(t) for t in shapes],
        out_shape=shapes,
        scratch_shapes=[pltpu.VMEM((8,) + t.shape, F32) for t in sent]
        + [pltpu.SemaphoreType.DMA((7 * ns,)), pltpu.SemaphoreType.DMA((7 * ns,))],
        compiler_params=_params(("arbitrary",), 32),
    )(*ins)
    return outs[:n], outs[n:2 * n], outs[2 * n:3 * n], outs[3 * n:4 * n], outs[4 * n]


def _local_step(xs, tgt, mems, weights, small, gather_mid=None, gather_ffn=None, reduce=False):
    wt_in, wt_ba, wt_bs, wo, wq, wkv, wt_o, wt_gu, wd = weights
    g_mix, b_gate, w_sgu, b_sgu, g_sgu, g_cross, g_mem, g_ffn, g_final = small
    wt = jnp.tril(w_sgu).astype(BF16)
    bst = b_sgu.T

    (a, qkv0, qkv1, qkv2, uv, gl), got = _fwd_in(xs, g_mix, wt_in, gather_mid)
    if gather_mid is not None:
        wt_ba, wt_bs, wo, wq, wkv, wt_o = got
    qkvs = (qkv0, qkv1, qkv2)
    os_, ls_ = zip(*[_attn_fwd(qkvs[g], g) for g in range(3)])
    (ya, ys, ba, bs, mg, h1), got = _fwd_mid(xs, os_, ls_, uv, gl, wt, bst, g_sgu, b_gate, wt_ba, wt_bs, wo, gather_ffn)
    if gather_ffn is not None:
        wt_gu, wd = got
    mb, kv = _mem_fwd(mems, g_mem, wkv)
    cb, qc, oc, h2 = _fwd_cross(h1, g_cross, wq, kv, wt_o)
    f, act, gu, dh3, dh3b, dg_final, loss = _ffn_fwd(h2, tgt, g_ffn, g_final, wt_gu, wd)
    dgu, dh2, dg_ffn = _ffn_bwd(h2, dh3, gu, g_ffn, wt_gu, wd)

    g_ffn_w = [_tn_matmul(dgu, f, "dw_gate_up", 1408), _tn_matmul(act, dh3b, "dw_down", 1408)]
    r_ffn = _Reduce(g_ffn_w, ["w_gate_up", "w_down"]) if reduce else None
    (dqc, dh1, dh1b, dkv, dg_cross), parts_ffn = _bwd_cross(dh2, h1, qc, g_cross, wq, kv, wt_o,
                                                           r_ffn.scatter if reduce else None)
    dw_kv, dw_kvb, dg_mem = _mem_bwd(dkv, mems, mb, g_mem, wkv)
    (dba, dbs, dgl, duv, do0, do1, do2, c0, c1, c2, db_gate, dg_sgu, dws, dbs_acc) = _bwd_mid(
        dh1, gl, ba, bs, uv, ls_, ya, wt, bst, g_sgu, b_gate, wt_ba, wt_bs, wo)
    dqkvs = [_attn_bwd(qkvs[g], do, ls_[g], corr, g) for g, (do, corr) in enumerate(((do0, c0), (do1, c1), (do2, c2)))]
    grad_x, dproj, dg_mix = _bwd_in(dqkvs, duv, dgl, dh1, xs, g_mix, wt_in)
    g_mid_w = [_tn_matmul(dba, ya, "dw_branch_attn", 1024),
               _tn_matmul(dbs, ys, "dw_branch_sgu", 1024),
               _tn_matmul(mg, dh1b, "dw_out", 1024),
               _tn_matmul(cb, dqc, "dw_q_cross", 1024),
               (dw_kv, dw_kvb),
               _tn_matmul(dh2, oc, "dw_o_cross", 1024)]
    small_terms = (dg_mix, db_gate, dws, dbs_acc, dg_sgu, dg_cross, dg_mem, dg_ffn, dg_final)
    if not reduce:
        full = [_tn_matmul(dproj, a, "dw_in", 1792)] + g_mid_w + g_ffn_w
        return loss, grad_x, [g for g, _ in full], small_terms
    r_mid = _Reduce(g_mid_w, ["w_branch_attn", "w_branch_sgu", "w_out", "w_q_cross", "w_kv_cross", "w_o_cross"])
    g_in, parts_mid = _tn_matmul(dproj, a, "dw_in", 1792, comm=r_mid.scatter)
    r_in = _Reduce([g_in], ["w_in"])
    halves = r_in.collect(_scatter_partials(r_in.scatter)) + r_mid.collect(parts_mid) + r_ffn.collect(parts_ffn)
    return loss, grad_x, halves, small_terms


def kernel(x, mem, g_mix, w_in, b_gate, w_sgu_spatial, b_sgu_spatial, g_sgu, w_branch_attn, w_branch_sgu, w_out, g_cross, g_mem, w_q_cross, w_kv_cross, w_o_cross, g_ffn, w_gate_up, w_down, g_final, loss_target, m_g_mix, m_w_in, m_b_gate, m_w_sgu_spatial, m_b_sgu_spatial, m_g_sgu, m_w_branch_attn, m_w_branch_sgu, m_w_out, m_g_cross, m_g_mem, m_w_q_cross, m_w_kv_cross, m_w_o_cross, m_g_ffn, m_w_gate_up, m_w_down, m_g_final, v_g_mix, v_w_in, v_b_gate, v_w_sgu_spatial, v_b_sgu_spatial, v_g_sgu, v_w_branch_attn, v_w_branch_sgu, v_w_out, v_g_cross, v_g_mem, v_w_q_cross, v_w_kv_cross, v_w_o_cross, v_g_ffn, v_w_gate_up, v_w_down, v_g_final):
    S = x.shape[1]
    xs, tgt, mems = x.reshape(S, D), loss_target.reshape(S, D), mem.reshape(mem.shape[1], D)
    g_final2 = g_final.reshape(1, D)

    big = [("w_in", w_in[0], m_w_in[0], v_w_in[0], True),
           ("w_branch_attn", w_branch_attn[0], m_w_branch_attn[0], v_w_branch_attn[0], True),
           ("w_branch_sgu", w_branch_sgu[0], m_w_branch_sgu[0], v_w_branch_sgu[0], True),
           ("w_out", w_out[0], m_w_out[0], v_w_out[0], False),
           ("w_q_cross", w_q_cross[0], m_w_q_cross[0], v_w_q_cross[0], False),
           ("w_kv_cross", w_kv_cross[0], m_w_kv_cross[0], v_w_kv_cross[0], False),
           ("w_o_cross", w_o_cross[0], m_w_o_cross[0], v_w_o_cross[0], True),
           ("w_gate_up", w_gate_up[0], m_w_gate_up[0], v_w_gate_up[0], True),
           ("w_down", w_down[0], m_w_down[0], v_w_down[0], False)]
    shards = [(w.T if tr else w).astype(BF16) for _, w, _, _, tr in big]
    (wt_in,) = _gather_weights(shards[:1])
    (loss, grad_x, reduced, (dg_mix, db_gate, dws, dbs_acc, dg_sgu, dg_cross, dg_mem, dg_ffn, dg_final)) = _local_step(
        xs, tgt, mems, (wt_in,) + (None,) * 8,
        (g_mix, b_gate, w_sgu_spatial[0], b_sgu_spatial[0], g_sgu, g_cross, g_mem, g_ffn, g_final2),
        _Gather(shards[1:7]), _Gather(shards[7:9]), reduce=True)
    full = _share_halves(reduced)

    big_out = {}
    for (name, w, m, v, tr), gsh in zip(big, full):
        gsh = gsh.T if tr else gsh
        delta, nm, nv = _elementwise(_adamw, [w, gsh, m, v], [F32, F32, F32], f"adam_{name}")
        big_out[name] = tuple(t[None] for t in (gsh, delta, nm, nv))

    small = [("g_mix", g_mix, m_g_mix, v_g_mix, dg_mix), ("b_gate", b_gate, m_b_gate, v_b_gate, db_gate),
             ("w_sgu_spatial", w_sgu_spatial, m_w_sgu_spatial, v_w_sgu_spatial, jnp.tril(dws)),
             ("b_sgu_spatial", b_sgu_spatial, m_b_sgu_spatial, v_b_sgu_spatial, jnp.sum(dbs_acc, axis=-1)),
             ("g_sgu", g_sgu, m_g_sgu, v_g_sgu, dg_sgu), ("g_cross", g_cross, m_g_cross, v_g_cross, dg_cross),
             ("g_mem", g_mem, m_g_mem, v_g_mem, dg_mem), ("g_ffn", g_ffn, m_g_ffn, v_g_ffn, dg_ffn),
             ("g_final", g_final, m_g_final, v_g_final, dg_final)]
    as_term = lambda s, t: t.reshape(s[4].shape)
    gs, ds, nms, nvs, loss_all = _adam_small(*[[as_term(s, s[k]) for s in small] for k in (1, 2, 3, 4)], loss)
    small_out = {s[0]: tuple(t[i].reshape(s[1].shape) for t in (gs, ds, nms, nvs)) for i, s in enumerate(small)}
    total_loss = loss_all[0, 0]

    order = ["g_mix", "w_in", "b_gate", "w_sgu_spatial", "b_sgu_spatial", "g_sgu", "w_branch_attn", "w_branch_sgu",
             "w_out", "g_cross", "g_mem", "w_q_cross", "w_kv_cross", "w_o_cross", "g_ffn", "w_gate_up", "w_down",
             "g_final"]
    res = {**big_out, **small_out}
    outs = [total_loss, grad_x.reshape(x.shape)]
    for k in range(4):
        outs += [res[nm][k] for nm in order]
    return tuple(outs)
```

```python
import math

import numpy as np
import jax
import jax.numpy as jnp
from jax import lax
from jax.experimental import pallas as pl
from jax.experimental.pallas import tpu as pltpu

F32, BF16 = jnp.float32, jnp.bfloat16
MESH = pl.DeviceIdType.MESH
ANY = pl.BlockSpec(memory_space=pl.ANY)
RES = pl.BlockSpec(memory_space=pltpu.VMEM)


def _pallas(body, **kw):
    call = pl.pallas_call(body, **kw)
    gs = kw.get("grid_spec")
    specs = kw.get("in_specs") if gs is None else [None] * gs.num_scalar_prefetch + list(gs.in_specs)

    def run(*args):
        if specs is not None:
            args = [a if (s is RES or s is None) else pltpu.with_memory_space_constraint(a, pltpu.HBM)
                    for a, s in zip(args, specs)]
        return call(*args)
    return run


def _whole(arr):
    nd = len(arr.shape)
    return pl.BlockSpec(arr.shape, lambda *_: (0,) * nd)

D = 1024
HEAD = 64
GROUP_W = 256
DIL_GROUPS = ((128, 1), (512, 4), (2048, 16))
BLK = 128
SGU_W = 512
MEM_HEADS, MEM_HD, MEM_W = 4, 128, 512
D_FF = 2816
FF_CHUNK = 256
EPS = 1e-6
NEG = -1e30
LR, B1, B2, AEPS, WD, STEP = 0.001, 0.9, 0.999, 1e-08, 0.01, 10
GELU_K, GELU_C = 0.7978845608028654, 0.044715


def _dot(a, b):
    return jnp.dot(a, b, preferred_element_type=F32)


def _dot_nt(a, b):
    return lax.dot_general(a, b, (((1,), (1,)), ((), ())), preferred_element_type=F32)


def _dot_tn(a, b):
    return lax.dot_general(a, b, (((0,), (0,)), ((), ())), preferred_element_type=F32)


def _row(tm, w):
    return pl.BlockSpec((tm, w), lambda i: (i, 0))


def _acc(shape):
    return pl.BlockSpec(shape, lambda i: (0,) * len(shape))


def _params(sem, mb):
    return pltpu.CompilerParams(dimension_semantics=sem, vmem_limit_bytes=mb << 20)


def _sds(shape, dt):
    return jax.ShapeDtypeStruct(shape, dt)


def _rms(h):
    return lax.rsqrt(jnp.mean(h * h, axis=-1, keepdims=True) + EPS)


def _rms_bwd(dy, h, r, g):
    t = dy * g
    dh = r * t - h * (r * r * r) * jnp.mean(t * h, axis=-1, keepdims=True)
    return dh, dy * h * r


def _gelu(x):
    t = jnp.tanh(GELU_K * (x + GELU_C * x * x * x))
    return 0.5 * x * (1.0 + t), t


def _gelu_grad(x, t):
    return 0.5 * (1.0 + t) + 0.5 * x * (1.0 - t * t) * GELU_K * (1.0 + 3.0 * GELU_C * x * x)


def _alibi_slopes():
    def pow2(n):
        start = 2.0 ** (-8.0 / n)
        return [start ** (i + 1) for i in range(n)]
    n = 12
    c = 2 ** int(math.floor(math.log2(n)))
    s = pow2(c) + pow2(2 * c)[0::2][: n - c]
    return np.array(sorted(s, reverse=True), dtype=np.float32).reshape(3, 4)


def _attn_bias(g):
    win, dil = DIL_GROUPS[g]
    steps = (np.arange(BLK)[:, None] + BLK) - np.arange(2 * BLK)[None, :]
    valid = (steps >= 0) & (steps <= win // dil)
    dist = (np.clip(steps, 0, None) * dil).astype(np.float32)
    b = -_alibi_slopes()[g][:, None, None] * dist[None]
    return np.where(valid[None], b, NEG).astype(np.float32)


def _head_masks():
    lane = lax.broadcasted_iota(jnp.int32, (1, GROUP_W), 1)
    return lane, [(lane >= HEAD * h) & (lane < HEAD * (h + 1)) for h in range(4)]


ATT_NB = 8


def _stack_heads(t, masks):
    z = jnp.zeros_like(t)
    return jnp.concatenate([jnp.where(m, t, z) for m in masks], axis=0)


def _unstack_heads(t, masks):
    out = jnp.zeros((BLK, GROUP_W), t.dtype)
    for h, m in enumerate(masks):
        out = jnp.where(m, t[h * BLK:(h + 1) * BLK], out)
    return out


def _stack_cols(ref, rows):
    return jnp.concatenate([ref[rows, HEAD * h:HEAD * h + 1] for h in range(4)], axis=0)


def _dil_spec(d, tm, w):
    return pl.BlockSpec((d, tm // d, w), lambda i: (0, i, 0))


def _to_dilated(val, s_ref, d, write):
    tm, w = val.shape
    for j in range(w // 128):
        s_ref[j, pl.ds(0, tm), :] = val[:, j * 128:(j + 1) * 128]
    for r in range(d):
        for j in range(w // 128):
            write(r, j, s_ref[j, pl.ds(r, tm // d, stride=d), :])


def _from_dilated(ref, s_ref, d, tm, w):
    if d == 1:
        return ref[0].astype(F32)
    for r in range(d):
        for j in range(w // 128):
            s_ref[j, pl.ds(r, tm // d, stride=d), :] = ref[r, :, j * 128:(j + 1) * 128].astype(F32)
    return jnp.concatenate([s_ref[j, pl.ds(0, tm), :] for j in range(w // 128)], axis=1)


def _fwd_in(x, g_mix, wt_in, gather=None, tm=512):
    S = x.shape[0]
    dils = [d for _, d in DIL_GROUPS]
    n = 0 if gather is None else gather.n
    last = S // tm - 1

    def body(*refs):
        x_ref, g_ref, w_ref = refs[:3]
        a_ref, q0_ref, q1_ref, q2_ref, uv_ref, gl_ref = refs[3 + n:9 + n]
        s_ref = refs[9 + 2 * n]
        comm = (refs[3:3 + n], refs[9 + n:9 + 2 * n], refs[10 + 2 * n:])
        if gather is not None:
            pl.when(pl.program_id(0) == 0)(lambda: gather.start(*comm))
        xv = x_ref[...]
        a = (xv * _rms(xv) * g_ref[...]).astype(BF16)
        a_ref[...] = a
        for g, (d, out) in enumerate(zip(dils, (q0_ref, q1_ref, q2_ref))):
            for part in range(3):
                rows = part * 768 + g * 256
                val = _dot_nt(a, w_ref[rows:rows + 256, :])
                if d == 1:
                    out[0, :, part * 256:(part + 1) * 256] = val.astype(BF16)
                else:
                    def write(r, j, piece, out=out, part=part):
                        out[r, :, part * 256 + j * 128:part * 256 + (j + 1) * 128] = piece.astype(BF16)
                    _to_dilated(val, s_ref, d, write)
        uv_ref[...] = _dot_nt(a, w_ref[2304:3328, :]).astype(BF16)
        gl_ref[...] = _dot_nt(a, w_ref[3328:5376, :]).astype(BF16)
        if gather is not None:
            pl.when(pl.program_id(0) == last)(lambda: gather.finish(*comm))

    outs = _pallas(
        body, grid=(S // tm,), name="fwd_in",
        in_specs=[_row(tm, D), _whole(g_mix), RES] + [ANY] * n,
        out_specs=[_row(tm, D)] + [_dil_spec(d, tm, 768) for d in dils] + [_row(tm, 1024), _row(tm, 2048)] + [ANY] * n,
        out_shape=[_sds((S, D), BF16)] + [_sds((d, S // d, 768), BF16) for d in dils]
        + [_sds((S, 1024), BF16), _sds((S, 2048), BF16)] + ([] if gather is None else gather.out_shape),
        scratch_shapes=[pltpu.VMEM((2, tm, 128), F32)] + ([] if gather is None else gather.scratch),
        compiler_params=_params(("arbitrary",), 60),
    )(x, g_mix, wt_in, *([] if gather is None else gather.halves))
    return outs[:6], ([] if gather is None else gather.full(outs[6:]))


def _attn_fwd(qkv, g):
    d, L, _ = qkv.shape
    nb = L // BLK
    bias = jnp.asarray(_attn_bias(g).reshape(4 * BLK, 2 * BLK))
    NB = min(ATT_NB, nb)
    W = NB * BLK

    def body(q_ref, kc_ref, kp_ref, vc_ref, vp_ref, b_ref, o_ref, l_ref):
        st = pl.program_id(1)
        k_all = jnp.concatenate([kp_ref[...], kc_ref[...]], axis=0)
        v_all = jnp.concatenate([vp_ref[...], vc_ref[...]], axis=0)
        lane, masks = _head_masks()
        for b in range(NB):
            rows = slice(b * BLK, (b + 1) * BLK)
            kk, vv = k_all[b * BLK:(b + 2) * BLK], v_all[b * BLK:(b + 2) * BLK]
            s = _dot_nt(_stack_heads(q_ref[rows, :], masks), kk) * 0.125 + b_ref[...]
            if b == 0:
                s = s + jnp.where((st == 0) & (lane < BLK), NEG, 0.0).astype(F32)
            mx = jnp.max(s, axis=-1, keepdims=True)
            e = jnp.exp(s - mx)
            den = jnp.sum(e, axis=-1, keepdims=True)
            o_ref[rows, :] = _unstack_heads(_dot(e.astype(BF16), vv) / den, masks)
            l_ref[rows, :] = _unstack_heads(mx + jnp.log(den), masks)

    def wide(col):
        return pl.BlockSpec((None, W, GROUP_W), lambda r, s: (r, s, col))

    def before(col):
        return pl.BlockSpec((None, BLK, GROUP_W), lambda r, s: (r, jnp.maximum(s * NB - 1, 0), col))

    return _pallas(
        body, grid=(d, nb // NB), name=f"attn_fwd_g{g}",
        in_specs=[wide(0), wide(1), before(1), wide(2), before(2),
                  pl.BlockSpec((4 * BLK, 2 * BLK), lambda r, s: (0, 0))],
        out_specs=[wide(0), wide(0)],
        out_shape=[_sds((d, L, GROUP_W), F32), _sds((d, L, GROUP_W), F32)],
        compiler_params=_params(("parallel", "parallel"), 32),
    )(qkv, qkv, qkv, qkv, qkv, bias)


def _group_weights(l0, l1, l2):
    m = jnp.maximum(jnp.maximum(l0, l1), l2)
    e0, e1, e2 = jnp.exp(l0 - m), jnp.exp(l1 - m), jnp.exp(l2 - m)
    inv = 1.0 / (e0 + e1 + e2)
    return e0 * inv, e1 * inv, e2 * inv


def _sgu_forward(uvf, gs, wt_ref, bst_ref, mixed_s, tm):
    z, t = _gelu(uvf)
    u, v = z[:, :SGU_W], z[:, SGU_W:]
    rv = _rms(v)
    vnb = (v * rv * gs).astype(BF16)
    for ci in range(tm // 128):
        for g in range(4):
            rs, cs = slice(ci * 128, (ci + 1) * 128), slice(g * 128, (g + 1) * 128)
            mixed_s[rs, cs] = _dot(wt_ref[g], vnb[rs, cs]) + bst_ref[:, g:g + 1]
    return u, v, rv, vnb, t


def _fwd_mid(x, os_, ls_, uv, gl, wt, bst, g_sgu, b_gate, wt_ba, wt_bs, w_out, gather=None, tm=512):
    S = x.shape[0]
    dils = [d for _, d in DIL_GROUPS]
    n = 0 if gather is None else gather.n
    last = S // tm - 1

    def body(*refs):
        (x_ref, o0, o1, o2, l0, l1, l2, uv_ref, gl_ref, wt_ref, bst_ref, gs_ref, bg_ref, wba_ref, wbs_ref,
         wo_ref) = refs[:16]
        ya_ref, ys_ref, ba_ref, bs_ref, mg_ref, h1_ref = refs[16 + n:22 + n]
        mixed_s, il_s = refs[22 + 2 * n:24 + 2 * n]
        comm = (refs[16:16 + n], refs[22 + n:22 + 2 * n], refs[24 + 2 * n:])
        if gather is not None:
            pl.when(pl.program_id(0) == 0)(lambda: gather.start(*comm))
        ls = [_from_dilated(r, il_s, d, tm, GROUP_W) for r, d in zip((l0, l1, l2), dils)]
        alphas = _group_weights(*ls)
        ya = jnp.zeros((tm, GROUP_W), F32)
        for a, r, d in zip(alphas, (o0, o1, o2), dils):
            ya = ya + a * _from_dilated(r, il_s, d, tm, GROUP_W)
        yab = ya.astype(BF16)
        ya_ref[...] = yab
        u, _, _, _, _ = _sgu_forward(uv_ref[...].astype(F32), gs_ref[...], wt_ref, bst_ref, mixed_s, tm)
        ysb = (u * mixed_s[...]).astype(BF16)
        ys_ref[...] = ysb
        gates = jax.nn.sigmoid(gl_ref[...].astype(F32) + bg_ref[...])
        ba = _dot_nt(yab, wba_ref[...])
        bs = _dot_nt(ysb, wbs_ref[...])
        ba_ref[...] = ba.astype(BF16)
        bs_ref[...] = bs.astype(BF16)
        mgb = (gates[:, :D] * ba + gates[:, D:] * bs).astype(BF16)
        mg_ref[...] = mgb
        h1_ref[...] = x_ref[...] + _dot(mgb, wo_ref[...])
        if gather is not None:
            pl.when(pl.program_id(0) == last)(lambda: gather.finish(*comm))

    gw = _row(tm, GROUP_W)
    dil = [_dil_spec(d, tm, GROUP_W) for d in dils]
    outs = _pallas(
        body, grid=(S // tm,), name="fwd_mid",
        in_specs=[_row(tm, D)] + dil + dil + [_row(tm, 1024), _row(tm, 2048)]
        + [_whole(t) for t in (wt, bst, g_sgu, b_gate)] + [RES] * 3 + [ANY] * n,
        out_specs=[gw, _row(tm, SGU_W), _row(tm, D), _row(tm, D), _row(tm, D), _row(tm, D)] + [ANY] * n,
        out_shape=[_sds((S, GROUP_W), BF16), _sds((S, SGU_W), BF16), _sds((S, D), BF16), _sds((S, D), BF16),
                   _sds((S, D), BF16), _sds((S, D), F32)] + ([] if gather is None else gather.out_shape),
        scratch_shapes=[pltpu.VMEM((tm, SGU_W), F32), pltpu.VMEM((2, tm, 128), F32)]
        + ([] if gather is None else gather.scratch),
        compiler_params=_params(("arbitrary",), 56),
    )(x, *os_, *ls_, uv, gl, wt, bst, g_sgu, b_gate, wt_ba, wt_bs, w_out, *([] if gather is None else gather.halves))
    return outs[:6], ([] if gather is None else gather.full(outs[6:]))


def _mem_fwd(mem, g_mem, w_kv):
    def body(m_ref, g_ref, w_ref, mb_ref, kv_ref):
        mv = m_ref[...]
        mb = (mv * _rms(mv) * g_ref[...]).astype(BF16)
        mb_ref[...] = mb
        kv_ref[...] = _dot(mb, w_ref[...]).astype(BF16)

    shapes = [_sds(mem.shape, BF16), _sds((mem.shape[0], 2 * MEM_W), BF16)]
    return _pallas(
        body, name="mem_fwd", grid=(1,), in_specs=[_whole(t) for t in (mem, g_mem, w_kv)],
        out_specs=[_whole(t) for t in shapes], out_shape=shapes, compiler_params=_params(("arbitrary",), 32),
    )(mem, g_mem, w_kv)


def _cross_probs(qh, kh):
    s = _dot_nt(qh, kh) * (MEM_HD ** -0.5)
    e = jnp.exp(s - jnp.max(s, axis=-1, keepdims=True))
    return e / jnp.sum(e, axis=-1, keepdims=True)


def _fwd_cross(h1, g_cross, w_q, kv, wt_o, tm=512):
    S = h1.shape[0]

    def body(h_ref, g_ref, wq_ref, kv_ref, wo_ref, c_ref, qc_ref, oc_ref, h2_ref):
        hv = h_ref[...]
        cb = (hv * _rms(hv) * g_ref[...]).astype(BF16)
        c_ref[...] = cb
        qcb = _dot(cb, wq_ref[...]).astype(BF16)
        qc_ref[...] = qcb
        for h in range(MEM_HEADS):
            cs = slice(h * MEM_HD, (h + 1) * MEM_HD)
            p = _cross_probs(qcb[:, cs], kv_ref[:, cs])
            oc_ref[:, cs] = _dot(p.astype(BF16), kv_ref[:, MEM_W + h * MEM_HD:MEM_W + (h + 1) * MEM_HD]).astype(BF16)
        h2_ref[...] = hv + _dot_nt(oc_ref[...], wo_ref[...])

    return _pallas(
        body, grid=(S // tm,), name="fwd_cross",
        in_specs=[_row(tm, D), _whole(g_cross), RES, _whole(kv), RES],
        out_specs=[_row(tm, D), _row(tm, MEM_W), _row(tm, MEM_W), _row(tm, D)],
        out_shape=[_sds((S, D), BF16), _sds((S, MEM_W), BF16), _sds((S, MEM_W), BF16), _sds((S, D), F32)],
        compiler_params=_params(("parallel",), 40),
    )(h1, g_cross, w_q, kv, wt_o)


def _ffn_fwd_bwd(h2, target, g_ffn, g_final, wt_gu, w_down, tm=256):
    S = h2.shape[0]
    nch = D_FF // FF_CHUNK

    def body(h_ref, t_ref, gf_ref, gz_ref, wgu_ref, wd_ref,
             f_ref, act_ref, dgu_ref, dh3b_ref, dh2_ref, dh2b_ref, dgf_ref, dgz_ref, loss_ref, gu_s, dact_s):
        i = pl.program_id(0)

        @pl.when(i == 0)
        def _():
            dgf_ref[...] = jnp.zeros_like(dgf_ref)
            dgz_ref[...] = jnp.zeros_like(dgz_ref)
            loss_ref[...] = jnp.zeros_like(loss_ref)

        hv = h_ref[...]
        r2 = _rms(hv)
        gf = gf_ref[...]
        fb = (hv * r2 * gf).astype(BF16)
        f_ref[...] = fb
        gu_s[...] = _dot_nt(fb, wgu_ref[...])
        for c in range(nch):
            cs = slice(c * FF_CHUNK, (c + 1) * FF_CHUNK)
            us = slice(D_FF + c * FF_CHUNK, D_FF + (c + 1) * FF_CHUNK)
            gt = gu_s[:, cs]
            act_ref[:, cs] = (gt * jax.nn.sigmoid(gt) * gu_s[:, us]).astype(BF16)
        h3 = hv + _dot(act_ref[...], wd_ref[...])
        r3 = _rms(h3)
        gz = gz_ref[...]
        diff = h3 * r3 * gz - t_ref[...]
        dy = diff * (1.0 / D)
        dh3, dgz_rows = _rms_bwd(dy, h3, r3, gz)
        dh3b = dh3.astype(BF16)
        dh3b_ref[...] = dh3b
        dact_s[...] = _dot_nt(dh3b, wd_ref[...])
        for c in range(nch):
            cs = slice(c * FF_CHUNK, (c + 1) * FF_CHUNK)
            us = slice(D_FF + c * FF_CHUNK, D_FF + (c + 1) * FF_CHUNK)
            dact, gt, up = dact_s[:, cs], gu_s[:, cs], gu_s[:, us]
            sg = jax.nn.sigmoid(gt)
            dgu_ref[:, cs] = (dact * up * (sg * (1.0 + gt * (1.0 - sg)))).astype(BF16)
            dgu_ref[:, us] = (dact * (gt * sg)).astype(BF16)
        df = _dot(dgu_ref[...], wgu_ref[...])
        dhn, dgf_rows = _rms_bwd(df, hv, r2, gf)
        dh2 = dh3 + dhn
        dh2_ref[...] = dh2
        dh2b_ref[...] = dh2.astype(BF16)
        dgf_ref[...] += jnp.sum(dgf_rows, axis=0, keepdims=True)
        dgz_ref[...] += jnp.sum(dgz_rows, axis=0, keepdims=True)
        loss_ref[...] += jnp.sum(jnp.sum(diff * diff, axis=0, keepdims=True), axis=1, keepdims=True) * (0.5 / D)

    return _pallas(
        body, grid=(S // tm,), name="ffn_fwd_bwd",
        in_specs=[_row(tm, D), _row(tm, D), _whole(g_ffn), _whole(g_final), RES, RES],
        out_specs=[_row(tm, D), _row(tm, D_FF), _row(tm, 2 * D_FF), _row(tm, D), _row(tm, D), _row(tm, D),
                   _acc((1, D)), _acc((1, D)), _acc((1, 128))],
        out_shape=[_sds((S, D), BF16), _sds((S, D_FF), BF16), _sds((S, 2 * D_FF), BF16), _sds((S, D), BF16),
                   _sds((S, D), F32), _sds((S, D), BF16), _sds((1, D), F32), _sds((1, D), F32), _sds((1, 128), F32)],
        scratch_shapes=[pltpu.VMEM((tm, 2 * D_FF), F32), pltpu.VMEM((tm, D_FF), F32)],
        compiler_params=_params(("arbitrary",), 60),
    )(h2, target, g_ffn, g_final, wt_gu, w_down)


def _bwd_cross(dh2, h1, qc, g_cross, w_q, kv, wt_o, comm=None, tm=512):
    S = h1.shape[0]
    n = 0 if comm is None else comm.n
    last = S // tm - 1

    def body(*refs):
        d_ref, h_ref, qc_ref, g_ref, wq_ref, kv_ref, wo_ref = refs[:7]
        dqc_ref, dh1_ref, dh1b_ref, dkv_ref, dg_ref = refs[7 + n:12 + n]
        cargs = (refs[7:7 + n], refs[12 + n:12 + 2 * n], refs[12 + 2 * n:])
        i = pl.program_id(0)

        @pl.when(i == 0)
        def _():
            dkv_ref[...] = jnp.zeros_like(dkv_ref)
            dg_ref[...] = jnp.zeros_like(dg_ref)
            if comm is not None:
                comm.start(*cargs)

        dh2 = d_ref[...]
        doc = _dot(dh2.astype(BF16), wo_ref[...])
        qcb = qc_ref[...]
        for h in range(MEM_HEADS):
            cs = slice(h * MEM_HD, (h + 1) * MEM_HD)
            vs = slice(MEM_W + h * MEM_HD, MEM_W + (h + 1) * MEM_HD)
            qh, kh, vh = qcb[:, cs], kv_ref[:, cs], kv_ref[:, vs]
            p = _cross_probs(qh, kh)
            dohb = doc[:, cs].astype(BF16)
            dp = _dot_nt(dohb, vh)
            dsb = (p * (dp - jnp.sum(dp * p, axis=-1, keepdims=True)) * (MEM_HD ** -0.5)).astype(BF16)
            dqc_ref[:, cs] = _dot(dsb, kh).astype(BF16)
            dkv_ref[:, cs] += _dot_tn(dsb, qh)
            dkv_ref[:, vs] += _dot_tn(p.astype(BF16), dohb)
        dc = _dot_nt(dqc_ref[...], wq_ref[...])
        hv = h_ref[...]
        dhn, dg_rows = _rms_bwd(dc, hv, _rms(hv), g_ref[...])
        dh1 = dh2 + dhn
        dh1_ref[...] = dh1
        dh1b_ref[...] = dh1.astype(BF16)
        dg_ref[...] += jnp.sum(dg_rows, axis=0, keepdims=True)
        if comm is not None:
            pl.when(i == last)(lambda: comm.finish(*cargs))

    outs = _pallas(
        body, grid=(S // tm,), name="bwd_cross",
        in_specs=[_row(tm, D), _row(tm, D), _row(tm, MEM_W), _whole(g_cross), RES, _whole(kv), RES] + [ANY] * n,
        out_specs=[_row(tm, MEM_W), _row(tm, D), _row(tm, D), _acc((256, 2 * MEM_W)), _acc((1, D))] + [ANY] * n,
        out_shape=[_sds((S, MEM_W), BF16), _sds((S, D), F32), _sds((S, D), BF16), _sds((256, 2 * MEM_W), F32),
                   _sds((1, D), F32)] + ([] if comm is None else comm.out_shape),
        scratch_shapes=[] if comm is None else comm.scratch,
        compiler_params=_params(("arbitrary",), 48),
    )(dh2, h1, qc, g_cross, w_q, kv, wt_o, *([] if comm is None else comm.ins))
    return outs[:5], outs[5:]


def _mem_bwd(dkv, mem, mb, g_mem, w_kv):
    def body(dkv_ref, m_ref, mb_ref, g_ref, w_ref, dw_ref, dwb_ref, dg_ref):
        dkvb = dkv_ref[...].astype(BF16)
        dw = _dot_tn(mb_ref[...], dkvb)
        dw_ref[...] = dw
        dwb_ref[...] = dw.astype(BF16)
        dm = _dot_nt(dkvb, w_ref[...])
        mv = m_ref[...]
        dg_ref[...] = jnp.sum(dm * mv * _rms(mv), axis=0, keepdims=True)

    shapes = [_sds((D, 2 * MEM_W), F32), _sds((D, 2 * MEM_W), BF16), _sds((1, D), F32)]
    return _pallas(
        body, name="mem_bwd", grid=(1,), in_specs=[_whole(t) for t in (dkv, mem, mb, g_mem, w_kv)],
        out_specs=[_whole(t) for t in shapes], out_shape=shapes, compiler_params=_params(("arbitrary",), 40),
    )(dkv, mem, mb, g_mem, w_kv)


def _bwd_mid(dh1, gl, ba, bs, uv, ls_, ya, wt, bst, g_sgu, b_gate, wt_ba, wt_bs, w_out, tm=512):
    S = dh1.shape[0]
    dils = [d for _, d in DIL_GROUPS]

    def body(d_ref, gl_ref, ba_ref, bs_ref, uv_ref, l0, l1, l2, ya_ref,
             wt_ref, bst_ref, gs_ref, bg_ref, wba_ref, wbs_ref, wo_ref,
             dba_ref, dbs_ref, dgl_ref, duv_ref, do0, do1, do2, c0, c1, c2,
             dbg_ref, dgs_ref, dws_ref, dbsa_ref, mixed_s, dvn_s, il_s):
        i = pl.program_id(0)

        @pl.when(i == 0)
        def _():
            for r in (dbg_ref, dgs_ref, dws_ref, dbsa_ref):
                r[...] = jnp.zeros_like(r)

        dm = _dot_nt(d_ref[...].astype(BF16), wo_ref[...])
        gates = jax.nn.sigmoid(gl_ref[...].astype(F32) + bg_ref[...])
        g0, g1 = gates[:, :D], gates[:, D:]
        dbab = (dm * g0).astype(BF16)
        dbsb = (dm * g1).astype(BF16)
        dba_ref[...] = dbab
        dbs_ref[...] = dbsb
        dg0 = dm * ba_ref[...].astype(F32) * g0 * (1.0 - g0)
        dg1 = dm * bs_ref[...].astype(F32) * g1 * (1.0 - g1)
        dgl_ref[:, :D] = dg0.astype(BF16)
        dgl_ref[:, D:] = dg1.astype(BF16)
        dbg_ref[:, :D] += jnp.sum(dg0, axis=0, keepdims=True)
        dbg_ref[:, D:] += jnp.sum(dg1, axis=0, keepdims=True)
        dya = _dot(dbab, wba_ref[...])
        dys = _dot(dbsb, wbs_ref[...])

        uvf = uv_ref[...].astype(F32)
        gs = gs_ref[...]
        u, v, rv, vnb, t = _sgu_forward(uvf, gs, wt_ref, bst_ref, mixed_s, tm)
        du = dys * mixed_s[...]
        dmixed = dys * u
        for ci in range(tm // 128):
            for g in range(4):
                rs, cs = slice(ci * 128, (ci + 1) * 128), slice(g * 128, (g + 1) * 128)
                dmx = dmixed[rs, cs]
                dmxb = dmx.astype(BF16)
                dvn_s[rs, cs] = _dot_tn(wt_ref[g], dmxb)
                dws_ref[g] += _dot_nt(dmxb, vnb[rs, cs])
                dbsa_ref[g] += dmx
        dv, dgs_rows = _rms_bwd(dvn_s[...], v, rv, gs)
        dgs_ref[...] += jnp.sum(dgs_rows, axis=0, keepdims=True)
        gg = _gelu_grad(uvf, t)
        duv_ref[:, :SGU_W] = (du * gg[:, :SGU_W]).astype(BF16)
        duv_ref[:, SGU_W:] = (dv * gg[:, SGU_W:]).astype(BF16)

        alphas = _group_weights(*[_from_dilated(r, il_s, d, tm, GROUP_W) for r, d in zip((l0, l1, l2), dils)])
        prod = dya * ya_ref[...].astype(F32)
        _, masks = _head_masks()
        hs = jnp.zeros_like(prod)
        for h in range(4):
            sh = jnp.sum(jnp.where(masks[h], prod, 0.0), axis=-1, keepdims=True)
            hs = jnp.where(masks[h], sh, hs)
        for a, d, do_ref, c_ref in zip(alphas, dils, (do0, do1, do2), (c0, c1, c2)):
            for val, out in ((a * dya, do_ref), (a * hs, c_ref)):
                if d == 1:
                    out[0] = val.astype(out.dtype)
                else:
                    def write(r, j, piece, out=out):
                        out[r, :, j * 128:(j + 1) * 128] = piece.astype(out.dtype)
                    _to_dilated(val, il_s, d, write)

    gw = _row(tm, GROUP_W)
    dil = [_dil_spec(d, tm, GROUP_W) for d in dils]
    return _pallas(
        body, grid=(S // tm,), name="bwd_mid",
        in_specs=[_row(tm, D), _row(tm, 2048), _row(tm, D), _row(tm, D), _row(tm, 1024)] + dil + [gw]
        + [_whole(t) for t in (wt, bst, g_sgu, b_gate)] + [RES] * 3,
        out_specs=[_row(tm, D), _row(tm, D), _row(tm, 2048), _row(tm, 1024)] + dil + dil
        + [_acc((1, 2048)), _acc((1, SGU_W)), _acc((4, 128, 128)), _acc((4, 128, 128))],
        out_shape=[_sds((S, D), BF16), _sds((S, D), BF16), _sds((S, 2048), BF16), _sds((S, 1024), BF16)]
        + [_sds((d, S // d, GROUP_W), BF16) for d in dils] + [_sds((d, S // d, GROUP_W), F32) for d in dils]
        + [_sds((1, 2048), F32), _sds((1, SGU_W), F32), _sds((4, 128, 128), F32), _sds((4, 128, 128), F32)],
        scratch_shapes=[pltpu.VMEM((tm, SGU_W), F32), pltpu.VMEM((tm, SGU_W), F32), pltpu.VMEM((2, tm, 128), F32)],
        compiler_params=_params(("arbitrary",), 60),
    )(dh1, gl, ba, bs, uv, *ls_, ya, wt, bst, g_sgu, b_gate, wt_ba, wt_bs, w_out)


def _attn_bwd(qkv, do, lse, corr, g):
    d, L, _ = qkv.shape
    nb = L // BLK
    NB = min(ATT_NB, nb)
    W = NB * BLK
    nsteps = nb // NB
    bias = jnp.asarray(_attn_bias(g).reshape(4 * BLK, 2 * BLK))

    def body(q_ref, kc_ref, kp_ref, vc_ref, vp_ref, do_ref, l_ref, c_ref, qn_ref, don_ref, ln_ref, cn_ref, b_ref,
             out_ref, dk_s, dv_s):
        st = pl.program_id(1)
        k_all = jnp.concatenate([kp_ref[...], kc_ref[...]], axis=0)
        v_all = jnp.concatenate([vp_ref[...], vc_ref[...]], axis=0)
        lane, masks = _head_masks()
        dk_s[...] = jnp.zeros_like(dk_s)
        dv_s[...] = jnp.zeros_like(dv_s)

        def block_terms(qs, dos, kk, vv, bias_v, lse_c, corr_c):
            s = _dot_nt(qs, kk) * 0.125 + bias_v
            p = jnp.exp(s - lse_c)
            dsb = (p * (_dot_nt(dos, vv) - corr_c) * 0.125).astype(BF16)
            return dsb, p.astype(BF16)

        for b in range(NB):
            rows = slice(b * BLK, (b + 1) * BLK)
            keys = slice(b * BLK, (b + 2) * BLK)
            kk, vv = k_all[keys], v_all[keys]
            qs, dos = _stack_heads(q_ref[rows, :], masks), _stack_heads(do_ref[rows, :], masks)
            bias_v = b_ref[...]
            if b == 0:
                bias_v = bias_v + jnp.where((st == 0) & (lane < BLK), NEG, 0.0).astype(F32)
            dsb, pb = block_terms(qs, dos, kk, vv, bias_v, _stack_cols(l_ref, rows), _stack_cols(c_ref, rows))
            out_ref[rows, 0:GROUP_W] = _unstack_heads(_dot(dsb, kk), masks).astype(BF16)
            dk_s[keys, :] += _dot_tn(dsb, qs)
            dv_s[keys, :] += _dot_tn(pb, dos)

        @pl.when(st < nsteps - 1)
        def _():
            last = slice(NB * BLK, (NB + 1) * BLK)
            qs, dos = _stack_heads(qn_ref[...], masks), _stack_heads(don_ref[...], masks)
            every = slice(None)
            dsb, pb = block_terms(qs, dos, k_all[last], v_all[last], b_ref[:, :BLK],
                                  _stack_cols(ln_ref, every), _stack_cols(cn_ref, every))
            dk_s[last, :] += _dot_tn(dsb, qs)
            dv_s[last, :] += _dot_tn(pb, dos)

        out_ref[:, GROUP_W:2 * GROUP_W] = dk_s[BLK:, :].astype(BF16)
        out_ref[:, 2 * GROUP_W:] = dv_s[BLK:, :].astype(BF16)

    def wide(col, w=GROUP_W):
        return pl.BlockSpec((None, W, w), lambda r, s: (r, s, col))

    def before(col):
        return pl.BlockSpec((None, BLK, GROUP_W), lambda r, s: (r, jnp.maximum(s * NB - 1, 0), col))

    def after(col):
        return pl.BlockSpec((None, BLK, GROUP_W), lambda r, s: (r, jnp.minimum((s + 1) * NB, nb - 1), col))

    return _pallas(
        body, grid=(d, nsteps), name=f"attn_bwd_g{g}",
        in_specs=[wide(0), wide(1), before(1), wide(2), before(2), wide(0), wide(0), wide(0),
                  after(0), after(0), after(0), after(0), pl.BlockSpec((4 * BLK, 2 * BLK), lambda r, s: (0, 0))],
        out_specs=wide(0, 768),
        out_shape=_sds((d, L, 768), BF16),
        scratch_shapes=[pltpu.VMEM(((NB + 1) * BLK, GROUP_W), F32), pltpu.VMEM(((NB + 1) * BLK, GROUP_W), F32)],
        compiler_params=_params(("parallel", "parallel"), 32),
    )(qkv, qkv, qkv, qkv, qkv, do, lse, corr, qkv, do, lse, corr, bias)


def _bwd_in(dqkvs, duv, dgl, dh1, x, g_mix, wt_in, tm=512):
    S = x.shape[0]
    dils = [d for _, d in DIL_GROUPS]

    def body(q0_ref, q1_ref, q2_ref, duv_ref, dgl_ref, d_ref, x_ref, g_ref, w_ref, dx_ref, dp_ref, dg_ref, il_s):
        i = pl.program_id(0)

        @pl.when(i == 0)
        def _():
            dg_ref[...] = jnp.zeros_like(dg_ref)

        for g, (d, ref) in enumerate(zip(dils, (q0_ref, q1_ref, q2_ref))):
            nat = _from_dilated(ref, il_s, d, tm, 768).astype(BF16)
            for part in range(3):
                col = part * 768 + g * 256
                dp_ref[:, col:col + 256] = nat[:, part * 256:(part + 1) * 256]
        dp_ref[:, 2304:3328] = duv_ref[...]
        dp_ref[:, 3328:5376] = dgl_ref[...]
        da = _dot(dp_ref[...], w_ref[...])
        xv = x_ref[...]
        dxn, dg_rows = _rms_bwd(da, xv, _rms(xv), g_ref[...])
        dx_ref[...] = d_ref[...] + dxn
        dg_ref[...] += jnp.sum(dg_rows, axis=0, keepdims=True)

    return _pallas(
        body, grid=(S // tm,), name="bwd_in",
        in_specs=[_dil_spec(d, tm, 768) for d in dils] + [_row(tm, 1024), _row(tm, 2048), _row(tm, D), _row(tm, D),
                                                          _whole(g_mix), RES],
        out_specs=[_row(tm, D), _row(tm, 5376), _acc((1, D))],
        out_shape=[_sds((S, D), F32), _sds((S, 5376), BF16), _sds((1, D), F32)],
        scratch_shapes=[pltpu.VMEM((6, tm, 128), F32)],
        compiler_params=_params(("arbitrary",), 60),
    )(*dqkvs, duv, dgl, dh1, x, g_mix, wt_in)


def _tn_matmul(a, b, name, tk, ts=2048, comm=None):
    S, K = a.shape
    N = b.shape[1]
    n = 0 if comm is None else comm.n
    nk, ns = K // tk, S // ts

    def body(*refs):
        a_ref, b_ref, o_ref, ob_ref = refs[0], refs[1], refs[2 + n], refs[3 + n]
        cargs = (refs[2:2 + n], refs[4 + n:4 + 2 * n], refs[4 + 2 * n:])
        k, s = pl.program_id(0), pl.program_id(1)
        if comm is not None:
            pl.when((k == 0) & (s == 0))(lambda: comm.start(*cargs))

        @pl.when(s == 0)
        def _():
            o_ref[...] = jnp.zeros_like(o_ref)

        o_ref[...] += _dot_tn(a_ref[...], b_ref[...])

        @pl.when(s == ns - 1)
        def _():
            ob_ref[...] = o_ref[...].astype(BF16)

        if comm is not None:
            pl.when((k == nk - 1) & (s == ns - 1))(lambda: comm.finish(*cargs))

    tile = pl.BlockSpec((tk, N), lambda k, s: (k, 0))
    outs = _pallas(
        body, grid=(nk, ns), name=name,
        in_specs=[pl.BlockSpec((ts, tk), lambda k, s: (s, k)), pl.BlockSpec((ts, N), lambda k, s: (s, 0))] + [ANY] * n,
        out_specs=[tile, tile] + [ANY] * n,
        out_shape=[_sds((K, N), F32), _sds((K, N), BF16)] + ([] if comm is None else comm.out_shape),
        scratch_shapes=[] if comm is None else comm.scratch,
        compiler_params=_params(("arbitrary", "arbitrary"), 56),
    )(a, b, *([] if comm is None else comm.ins))
    pair = (outs[0], outs[1])
    return pair if comm is None else (pair, outs[2:])


def _chip_peers(x, y):
    return [(1 - x, y), (x, 1 - y), (1 - x, 1 - y)]


STAGE_BYTES = 2 << 20


def _chunk_plan(shapes, itemsize):
    plan = []
    for i, (rows, w) in enumerate(shapes):
        ch = max(16, min(rows, (STAGE_BYTES // (w * itemsize)) // 16 * 16))
        while rows % ch:
            ch -= 16
        plan += [(i, r0, ch) for r0 in range(0, rows, ch)]
    return plan


def _remote(src, dst, ssem, rsem, dev):
    return pltpu.make_async_remote_copy(src_ref=src, dst_ref=dst, send_sem=ssem, recv_sem=rsem, device_id=dev,
                                        device_id_type=MESH)


class _Gather:
    def __init__(self, shards):
        self.n = len(shards)
        self.shards = shards
        self.halves = [s.reshape(2, s.shape[0] // 2, s.shape[1]) for s in shards]
        self.plan = _chunk_plan([h.shape[1:] for h in self.halves], 2)
        self.out_shape = [_sds((4,) + h.shape, BF16) for h in self.halves]
        n = self.n
        self.scratch = [pltpu.SemaphoreType.DMA((6 * n,)), pltpu.SemaphoreType.DMA((6 * n,)),
                        pltpu.SemaphoreType.DMA((2,)), pltpu.SemaphoreType.DMA((2,)),
                        pltpu.VMEM((2, max(p[2] for p in self.plan), max(h.shape[2] for h in self.halves)), BF16)]

    def full(self, outs):
        return [o.reshape(4 * s.shape[0], s.shape[1]) for o, s in zip(outs, self.shards)]

    def _sends(self, ins, outs, ssem, rsem):
        x, y, c = lax.axis_index("x"), lax.axis_index("y"), lax.axis_index("c")
        me = 2 * x + y
        return [_remote(ins[i].at[c], outs[i].at[me, c], ssem.at[6 * i + k], rsem.at[6 * i + k], (px, py, c))
                for i in range(self.n) for k, (px, py) in enumerate(_chip_peers(x, y))]

    def start(self, ins, outs, scratch):
        ssem, rsem, lsem, osem, buf = scratch
        me = 2 * lax.axis_index("x") + lax.axis_index("y")
        for cp in self._sends(ins, outs, ssem, rsem):
            cp.start()
        pending = {}
        for i, r0, ch in self.plan:
            for h in range(2):
                if h in pending:
                    pending[h].wait()
                stage = buf.at[h, pl.ds(0, ch), pl.ds(0, self.halves[i].shape[2])]
                ld = pltpu.make_async_copy(ins[i].at[h, pl.ds(r0, ch)], stage, lsem.at[h])
                ld.start()
                ld.wait()
                st = pltpu.make_async_copy(stage, outs[i].at[me, h, pl.ds(r0, ch)], osem.at[h])
                st.start()
                pending[h] = st
        for st in pending.values():
            st.wait()

    def finish(self, ins, outs, scratch):
        ssem, rsem = scratch[:2]
        x, y, c = lax.axis_index("x"), lax.axis_index("y"), lax.axis_index("c")
        chips = _chip_peers(x, y)
        sib = (x, y, 1 - c)
        forwards = []
        for i in range(self.n):
            for k, (px, py) in enumerate(chips):
                landed = outs[i].at[2 * px + py, c]
                _remote(landed, landed, ssem.at[6 * i + k], rsem.at[6 * i + k], (px, py, c)).wait_recv()
                cp = _remote(landed, landed, ssem.at[6 * i + 3 + k], rsem.at[6 * i + 3 + k], sib)
                cp.start()
                forwards.append(cp)
        for i in range(self.n):
            for k, (px, py) in enumerate(chips):
                passed = outs[i].at[2 * px + py, 1 - c]
                _remote(passed, passed, ssem.at[6 * i + 3 + k], rsem.at[6 * i + 3 + k], sib).wait_recv()
        for cp in self._sends(ins, outs, ssem, rsem) + forwards:
            cp.wait_send()


def _gather_weights(shards):
    gt = _Gather(shards)
    n = gt.n

    def body(*refs):
        ins, outs, scratch = refs[:n], refs[n:2 * n], refs[2 * n:]
        gt.start(ins, outs, scratch)
        gt.finish(ins, outs, scratch)

    outs = _pallas(
        body, name="gather_weights", in_specs=[ANY] * n, out_specs=[ANY] * n, out_shape=gt.out_shape,
        scratch_shapes=gt.scratch, compiler_params=pltpu.CompilerParams(vmem_limit_bytes=32 << 20),
    )(*gt.halves)
    return gt.full(outs)


def _swap_halves(grads):
    n = len(grads)
    view = lambda g: g.reshape(4, 2, g.shape[0] // 8, g.shape[1])
    g4f = [view(g) for g, _ in grads]
    g4 = [view(gb) for _, gb in grads]

    def body(*refs):
        ins, got = refs[:n], refs[n:2 * n]
        ssem, rsem = refs[2 * n:]
        x, y, c = lax.axis_index("x"), lax.axis_index("y"), lax.axis_index("c")
        sib = (x, y, 1 - c)
        cps = []
        for i in range(n):
            rc = _remote(ins[i].at[:, 1 - c], got[i], ssem.at[i], rsem.at[i], sib)
            rc.start()
            cps.append(rc)
        for cp in cps:
            cp.wait()

    half = [_sds((4, g.shape[2], g.shape[3]), BF16) for g in g4]
    got = _pallas(
        body, name="swap_halves", in_specs=[ANY] * n, out_specs=[ANY] * n, out_shape=half,
        scratch_shapes=[pltpu.SemaphoreType.DMA((n,)), pltpu.SemaphoreType.DMA((n,))],
    )(*g4)
    return g4f, got


def _chip_sum(g4, got, name):
    _, _, R, W = g4.shape
    tr = _tile(R, max(16, min(512, (1 << 18) // W // 16 * 16)))
    c = lax.axis_index("c").astype(jnp.int32).reshape(1)

    def body(c_ref, a_ref, b_ref, s_ref, sb_ref):
        s = a_ref[...] + b_ref[...].astype(F32)
        s_ref[...] = s
        sb_ref[...] = s.astype(BF16)

    plain = pl.BlockSpec((None, tr, W), lambda j, t, c_ref: (j, t, 0))
    return _pallas(
        body, name=name,
        grid_spec=pltpu.PrefetchScalarGridSpec(
            num_scalar_prefetch=1, grid=(4, R // tr),
            in_specs=[pl.BlockSpec((None, None, tr, W), lambda j, t, c_ref: (j, c_ref[0], t, 0)), plain],
            out_specs=[plain, plain]),
        out_shape=[_sds((4, R, W), F32), _sds((4, R, W), BF16)],
        compiler_params=_params(("parallel", "parallel"), 32),
    )(c, g4, got)


class _Scatter:
    def __init__(self, sums_b):
        self.n = len(sums_b)
        self.ins = list(sums_b)
        self.out_shape = [_sds((3,) + s.shape[1:], BF16) for s in sums_b]
        self.scratch = [pltpu.SemaphoreType.DMA((3 * self.n,)), pltpu.SemaphoreType.DMA((3 * self.n,))]

    def _copies(self, ins, outs, scratch):
        ssem, rsem = scratch
        x, y, c = lax.axis_index("x"), lax.axis_index("y"), lax.axis_index("c")
        return [_remote(ins[i].at[2 * px + py], outs[i].at[k], ssem.at[3 * i + k], rsem.at[3 * i + k], (px, py, c))
                for i in range(self.n) for k, (px, py) in enumerate(_chip_peers(x, y))]

    def start(self, ins, outs, scratch):
        for cp in self._copies(ins, outs, scratch):
            cp.start()

    def finish(self, ins, outs, scratch):
        for cp in self._copies(ins, outs, scratch):
            cp.wait()


def _scatter_partials(sc):
    n = sc.n

    def body(*refs):
        args = (refs[:n], refs[n:2 * n], refs[2 * n:])
        sc.start(*args)
        sc.finish(*args)

    return _pallas(
        body, name="scatter_partials", in_specs=[ANY] * n, out_specs=[ANY] * n, out_shape=sc.out_shape,
        scratch_shapes=sc.scratch,
    )(*sc.ins)


class _Reduce:
    def __init__(self, grads, names):
        self.names = names
        g4, got = _swap_halves(grads)
        self.sums, sums_b = [], []
        for nm, g, t in zip(names, g4, got):
            s_, sb_ = _chip_sum(g, t, f"chip_sum_{nm}")
            self.sums.append(s_)
            sums_b.append(sb_)
        self.scatter = _Scatter(sums_b)

    def collect(self, parts):
        return [_mesh_sum(s, p, f"mesh_sum_{nm}") for nm, s, p in zip(self.names, self.sums, parts)]


def _mesh_sum(sums, parts, name):
    _, R, W = sums.shape
    tr = _tile(R, max(16, min(512, (1 << 18) // W // 16 * 16)))
    me = (2 * lax.axis_index("x") + lax.axis_index("y")).astype(jnp.int32).reshape(1)

    def body(me_ref, m_ref, p_ref, o_ref):
        o_ref[...] = m_ref[...] + p_ref[0].astype(F32) + p_ref[1].astype(F32) + p_ref[2].astype(F32)

    return _pallas(
        body, name=name,
        grid_spec=pltpu.PrefetchScalarGridSpec(
            num_scalar_prefetch=1, grid=(R // tr,),
            in_specs=[pl.BlockSpec((None, tr, W), lambda i, me_ref: (me_ref[0], i, 0)),
                      pl.BlockSpec((3, tr, W), lambda i, me_ref: (0, i, 0))],
            out_specs=pl.BlockSpec((tr, W), lambda i, me_ref: (i, 0))),
        out_shape=_sds((R, W), F32), compiler_params=_params(("parallel",), 32),
    )(me, sums, parts)


def _share_halves(reduced):
    n = len(reduced)
    plan = _chunk_plan([r.shape for r in reduced], 4)
    max_rows = max(p[2] for p in plan)
    max_w = max(r.shape[1] for r in reduced)

    def body(*refs):
        ins, outs = refs[:n], refs[n:2 * n]
        ssem, rsem, lsem, osem, buf = refs[2 * n:]
        x, y, c = lax.axis_index("x"), lax.axis_index("y"), lax.axis_index("c")
        sib = (x, y, 1 - c)
        pending = {}
        for k, (i, r0, ch) in enumerate(plan):
            slot = k % 2
            if slot in pending:
                rc, lc = pending[slot]
                rc.wait_send()
                lc.wait()
            stage = buf.at[slot, pl.ds(0, ch), pl.ds(0, reduced[i].shape[1])]
            ld = pltpu.make_async_copy(ins[i].at[pl.ds(r0, ch)], stage, lsem.at[slot])
            ld.start()
            ld.wait()
            place = outs[i].at[c, pl.ds(r0, ch)]
            rc = _remote(stage, place, ssem.at[slot], rsem.at[i], sib)
            lc = pltpu.make_async_copy(stage, place, osem.at[slot])
            rc.start()
            lc.start()
            pending[slot] = (rc, lc)
        for rc, lc in pending.values():
            rc.wait_send()
            lc.wait()
        for i in range(n):
            theirs = outs[i].at[1 - c]
            _remote(theirs, theirs, ssem.at[0], rsem.at[i], sib).wait_recv()

    outs = _pallas(
        body, name="share_halves", in_specs=[ANY] * n, out_specs=[ANY] * n,
        out_shape=[_sds((2,) + r.shape, F32) for r in reduced],
        scratch_shapes=[pltpu.SemaphoreType.DMA((2,)), pltpu.SemaphoreType.DMA((n,)), pltpu.SemaphoreType.DMA((2,)),
                        pltpu.SemaphoreType.DMA((2,)), pltpu.VMEM((2, max_rows, max_w), F32)],
        compiler_params=pltpu.CompilerParams(vmem_limit_bytes=32 << 20),
    )(*reduced)
    return [o.reshape(2 * r.shape[0], r.shape[1]) for o, r in zip(outs, reduced)]


def _tile(rows, cap=256):
    t = min(rows, cap) // 16 * 16
    while rows % t:
        t -= 16
    return t


def _elementwise(fn, ins, out_dtypes, name):
    R, W = ins[0].shape
    tr = _tile(R, max(8, min(512, (1 << 18) // W // 8 * 8)))

    def body(*refs):
        outs = fn(*[r[...] for r in refs[:len(ins)]])
        for o_ref, o in zip(refs[len(ins):], outs):
            o_ref[...] = o.astype(o_ref.dtype)

    return _pallas(
        body, grid=(R // tr,), name=name, in_specs=[_row(tr, W)] * len(ins), out_specs=[_row(tr, W)] * len(out_dtypes),
        out_shape=[_sds((R, W), dt) for dt in out_dtypes],
        compiler_params=_params(("parallel",), 48),
    )(*ins)


def _adamw(w, g, m, v):
    m = B1 * m + (1.0 - B1) * g
    v = B2 * v + (1.0 - B2) * (g * g)
    m_hat = m / (1.0 - B1 ** STEP)
    v_hat = v / (1.0 - B2 ** STEP)
    return -LR * (m_hat / (jnp.sqrt(v_hat) + AEPS) + WD * w), m, v


def _adam_small(ws, ms, vs, parts, loss_part):
    n = len(ws)
    sent = list(parts) + [loss_part]
    ns = n + 1

    def body(*refs):
        w_refs, m_refs, v_refs = refs[:n], refs[n:2 * n], refs[2 * n:3 * n]
        p_refs = refs[3 * n:3 * n + ns]
        outs = refs[3 * n + ns:3 * n + ns + 4 * n + 1]
        g_refs, d_refs, nm_refs, nv_refs, loss_ref = outs[:n], outs[n:2 * n], outs[2 * n:3 * n], outs[3 * n:4 * n], outs[4 * n]
        all_s = refs[3 * n + ns + 4 * n + 1:3 * n + ns + 4 * n + 1 + ns]
        ssem, rsem = refs[-2:]
        x, y, c = lax.axis_index("x"), lax.axis_index("y"), lax.axis_index("c")
        me = 4 * x + 2 * y + c
        for i in range(ns):
            all_s[i][me] = p_refs[i][...]
        cps = []
        for rel in range(1, 8):
            peer = (1 - x if rel & 4 else x, 1 - y if rel & 2 else y, 1 - c if rel & 1 else c)
            for i in range(ns):
                k = (rel - 1) * ns + i
                mine = all_s[i].at[me]
                rc = _remote(mine, mine, ssem.at[k], rsem.at[k], peer)
                rc.start()
                cps.append((rc, i, k, 4 * peer[0] + 2 * peer[1] + peer[2]))
        for rc, i, k, peer_slot in cps:
            rc.wait_send()
            theirs = all_s[i].at[peer_slot]
            _remote(theirs, theirs, ssem.at[k], rsem.at[k], (x, y, c)).wait_recv()

        def total(i):
            t = all_s[i][0]
            for k in range(1, 8):
                t = t + all_s[i][k]
            return t

        for i in range(n):
            g = total(i)
            g_refs[i][...] = g
            d_refs[i][...], nm_refs[i][...], nv_refs[i][...] = _adamw(w_refs[i][...], g, m_refs[i][...], v_refs[i][...])
        loss_ref[...] = total(n)

    shapes = [_sds(w.shape, F32) for w in ws] * 4 + [_sds(loss_part.shape, F32)]
    ins = [*ws, *ms, *vs, *sent]
    outs = _pallas(
        body, name="adam_small", grid=(1,), in_specs=[_whole(t) for t in ins], out_specs=[_whole(t) for t in shapes],
        out_shape=shapes,
        scratch_shapes=[pltpu.VMEM((8,) + t.shape, F32) for t in sent]
        + [pltpu.SemaphoreType.DMA((7 * ns,)), pltpu.SemaphoreType.DMA((7 * ns,))],
        compiler_params=_params(("arbitrary",), 32),
    )(*ins)
    return outs[:n], outs[n:2 * n], outs[2 * n:3 * n], outs[3 * n:4 * n], outs[4 * n]


def _local_step(xs, tgt, mems, weights, small, gather_mid=None, gather_ffn=None, reduce=False):
    wt_in, wt_ba, wt_bs, wo, wq, wkv, wt_o, wt_gu, wd = weights
    g_mix, b_gate, w_sgu, b_sgu, g_sgu, g_cross, g_mem, g_ffn, g_final = small
    wt = jnp.tril(w_sgu).astype(BF16)
    bst = b_sgu.T

    (a, qkv0, qkv1, qkv2, uv, gl), got = _fwd_in(xs, g_mix, wt_in, gather_mid)
    if gather_mid is not None:
        wt_ba, wt_bs, wo, wq, wkv, wt_o = got
    qkvs = (qkv0, qkv1, qkv2)
    os_, ls_ = zip(*[_attn_fwd(qkvs[g], g) for g in range(3)])
    (ya, ys, ba, bs, mg, h1), got = _fwd_mid(xs, os_, ls_, uv, gl, wt, bst, g_sgu, b_gate, wt_ba, wt_bs, wo, gather_ffn)
    if gather_ffn is not None:
        wt_gu, wd = got
    mb, kv = _mem_fwd(mems, g_mem, wkv)
    cb, qc, oc, h2 = _fwd_cross(h1, g_cross, wq, kv, wt_o)
    f, act, dgu, dh3b, dh2, dh2b, dg_ffn, dg_final, loss = _ffn_fwd_bwd(h2, tgt, g_ffn, g_final, wt_gu, wd)

    g_ffn_w = [_tn_matmul(dgu, f, "dw_gate_up", 1408), _tn_matmul(act, dh3b, "dw_down", 1408)]
    r_ffn = _Reduce(g_ffn_w, ["w_gate_up", "w_down"]) if reduce else None
    (dqc, dh1, dh1b, dkv, dg_cross), parts_ffn = _bwd_cross(dh2, h1, qc, g_cross, wq, kv, wt_o,
                                                           r_ffn.scatter if reduce else None)
    dw_kv, dw_kvb, dg_mem = _mem_bwd(dkv, mems, mb, g_mem, wkv)
    (dba, dbs, dgl, duv, do0, do1, do2, c0, c1, c2, db_gate, dg_sgu, dws, dbs_acc) = _bwd_mid(
        dh1, gl, ba, bs, uv, ls_, ya, wt, bst, g_sgu, b_gate, wt_ba, wt_bs, wo)
    dqkvs = [_attn_bwd(qkvs[g], do, ls_[g], corr, g) for g, (do, corr) in enumerate(((do0, c0), (do1, c1), (do2, c2)))]
    grad_x, dproj, dg_mix = _bwd_in(dqkvs, duv, dgl, dh1, xs, g_mix, wt_in)
    g_mid_w = [_tn_matmul(dba, ya, "dw_branch_attn", 1024),
               _tn_matmul(dbs, ys, "dw_branch_sgu", 1024),
               _tn_matmul(mg, dh1b, "dw_out", 1024),
               _tn_matmul(cb, dqc, "dw_q_cross", 1024),
               (dw_kv, dw_kvb),
               _tn_matmul(dh2b, oc, "dw_o_cross", 1024)]
    small_terms = (dg_mix, db_gate, dws, dbs_acc, dg_sgu, dg_cross, dg_mem, dg_ffn, dg_final)
    if not reduce:
        full = [_tn_matmul(dproj, a, "dw_in", 1792)] + g_mid_w + g_ffn_w
        return loss, grad_x, [g for g, _ in full], small_terms
    r_mid = _Reduce(g_mid_w, ["w_branch_attn", "w_branch_sgu", "w_out", "w_q_cross", "w_kv_cross", "w_o_cross"])
    g_in, parts_mid = _tn_matmul(dproj, a, "dw_in", 1792, comm=r_mid.scatter)
    r_in = _Reduce([g_in], ["w_in"])
    halves = r_in.collect(_scatter_partials(r_in.scatter)) + r_mid.collect(parts_mid) + r_ffn.collect(parts_ffn)
    return loss, grad_x, halves, small_terms


def kernel(x, mem, g_mix, w_in, b_gate, w_sgu_spatial, b_sgu_spatial, g_sgu, w_branch_attn, w_branch_sgu, w_out, g_cross, g_mem, w_q_cross, w_kv_cross, w_o_cross, g_ffn, w_gate_up, w_down, g_final, loss_target, m_g_mix, m_w_in, m_b_gate, m_w_sgu_spatial, m_b_sgu_spatial, m_g_sgu, m_w_branch_attn, m_w_branch_sgu, m_w_out, m_g_cross, m_g_mem, m_w_q_cross, m_w_kv_cross, m_w_o_cross, m_g_ffn, m_w_gate_up, m_w_down, m_g_final, v_g_mix, v_w_in, v_b_gate, v_w_sgu_spatial, v_b_sgu_spatial, v_g_sgu, v_w_branch_attn, v_w_branch_sgu, v_w_out, v_g_cross, v_g_mem, v_w_q_cross, v_w_kv_cross, v_w_o_cross, v_g_ffn, v_w_gate_up, v_w_down, v_g_final):
    S = x.shape[1]
    xs, tgt, mems = x.reshape(S, D), loss_target.reshape(S, D), mem.reshape(mem.shape[1], D)
    g_final2 = g_final.reshape(1, D)

    big = [("w_in", w_in[0], m_w_in[0], v_w_in[0], True),
           ("w_branch_attn", w_branch_attn[0], m_w_branch_attn[0], v_w_branch_attn[0], True),
           ("w_branch_sgu", w_branch_sgu[0], m_w_branch_sgu[0], v_w_branch_sgu[0], True),
           ("w_out", w_out[0], m_w_out[0], v_w_out[0], False),
           ("w_q_cross", w_q_cross[0], m_w_q_cross[0], v_w_q_cross[0], False),
           ("w_kv_cross", w_kv_cross[0], m_w_kv_cross[0], v_w_kv_cross[0], False),
           ("w_o_cross", w_o_cross[0], m_w_o_cross[0], v_w_o_cross[0], True),
           ("w_gate_up", w_gate_up[0], m_w_gate_up[0], v_w_gate_up[0], True),
           ("w_down", w_down[0], m_w_down[0], v_w_down[0], False)]
    shards = [(w.T if tr else w).astype(BF16) for _, w, _, _, tr in big]
    (wt_in,) = _gather_weights(shards[:1])
    (loss, grad_x, reduced, (dg_mix, db_gate, dws, dbs_acc, dg_sgu, dg_cross, dg_mem, dg_ffn, dg_final)) = _local_step(
        xs, tgt, mems, (wt_in,) + (None,) * 8,
        (g_mix, b_gate, w_sgu_spatial[0], b_sgu_spatial[0], g_sgu, g_cross, g_mem, g_ffn, g_final2),
        _Gather(shards[1:7]), _Gather(shards[7:9]), reduce=True)
    full = _share_halves(reduced)

    big_out = {}
    for (name, w, m, v, tr), gsh in zip(big, full):
        gsh = gsh.T if tr else gsh
        delta, nm, nv = _elementwise(_adamw, [w, gsh, m, v], [F32, F32, F32], f"adam_{name}")
        big_out[name] = tuple(t[None] for t in (gsh, delta, nm, nv))

    small = [("g_mix", g_mix, m_g_mix, v_g_mix, dg_mix), ("b_gate", b_gate, m_b_gate, v_b_gate, db_gate),
             ("w_sgu_spatial", w_sgu_spatial, m_w_sgu_spatial, v_w_sgu_spatial, jnp.tril(dws)),
             ("b_sgu_spatial", b_sgu_spatial, m_b_sgu_spatial, v_b_sgu_spatial, jnp.sum(dbs_acc, axis=-1)),
             ("g_sgu", g_sgu, m_g_sgu, v_g_sgu, dg_sgu), ("g_cross", g_cross, m_g_cross, v_g_cross, dg_cross),
             ("g_mem", g_mem, m_g_mem, v_g_mem, dg_mem), ("g_ffn", g_ffn, m_g_ffn, v_g_ffn, dg_ffn),
             ("g_final", g_final, m_g_final, v_g_final, dg_final)]
    as_term = lambda s, t: t.reshape(s[4].shape)
    gs, ds, nms, nvs, loss_all = _adam_small(*[[as_term(s, s[k]) for s in small] for k in (1, 2, 3, 4)], loss)
    small_out = {s[0]: tuple(t[i].reshape(s[1].shape) for t in (gs, ds, nms, nvs)) for i, s in enumerate(small)}
    total_loss = loss_all[0, 0]

    order = ["g_mix", "w_in", "b_gate", "w_sgu_spatial", "b_sgu_spatial", "g_sgu", "w_branch_attn", "w_branch_sgu",
             "w_out", "g_cross", "g_mem", "w_q_cross", "w_kv_cross", "w_o_cross", "g_ffn", "w_gate_up", "w_down",
             "g_final"]
    res = {**big_out, **small_out}
    outs = [total_loss, grad_x.reshape(x.shape)]
    for k in range(4):
        outs += [res[nm][k] for nm in order]
    return tuple(outs)
```

```python
import math

import numpy as np
import jax
import jax.numpy as jnp
from jax import lax
from jax.experimental import pallas as pl
from jax.experimental.pallas import tpu as pltpu

F32, BF16 = jnp.float32, jnp.bfloat16
MESH = pl.DeviceIdType.MESH
ANY = pl.BlockSpec(memory_space=pl.ANY)
RES = pl.BlockSpec(memory_space=pltpu.VMEM)


def _pallas(body, **kw):
    call = pl.pallas_call(body, **kw)
    gs = kw.get("grid_spec")
    specs = kw.get("in_specs") if gs is None else [None] * gs.num_scalar_prefetch + list(gs.in_specs)

    def run(*args):
        if specs is not None:
            args = [a if (s is RES or s is None) else pltpu.with_memory_space_constraint(a, pltpu.HBM)
                    for a, s in zip(args, specs)]
        return call(*args)
    return run


def _whole(arr):
    nd = len(arr.shape)
    return pl.BlockSpec(arr.shape, lambda *_: (0,) * nd)

D = 1024
HEAD = 64
GROUP_W = 256
DIL_GROUPS = ((128, 1), (512, 4), (2048, 16))
BLK = 128
SGU_W = 512
MEM_HEADS, MEM_HD, MEM_W = 4, 128, 512
D_FF = 2816
FF_CHUNK = 256
EPS = 1e-6
NEG = -1e30
LR, B1, B2, AEPS, WD, STEP = 0.001, 0.9, 0.999, 1e-08, 0.01, 10
GELU_K, GELU_C = 0.7978845608028654, 0.044715


def _dot(a, b):
    return jnp.dot(a, b, preferred_element_type=F32)


def _dot_nt(a, b):
    return lax.dot_general(a, b, (((1,), (1,)), ((), ())), preferred_element_type=F32)


def _dot_tn(a, b):
    return lax.dot_general(a, b, (((0,), (0,)), ((), ())), preferred_element_type=F32)


def _row(tm, w):
    return pl.BlockSpec((tm, w), lambda i: (i, 0))


def _acc(shape):
    return pl.BlockSpec(shape, lambda i: (0,) * len(shape))


def _params(sem, mb):
    return pltpu.CompilerParams(dimension_semantics=sem, vmem_limit_bytes=mb << 20)


def _sds(shape, dt):
    return jax.ShapeDtypeStruct(shape, dt)


def _rms(h):
    return lax.rsqrt(jnp.mean(h * h, axis=-1, keepdims=True) + EPS)


def _rms_bwd(dy, h, r, g):
    t = dy * g
    dh = r * t - h * (r * r * r) * jnp.mean(t * h, axis=-1, keepdims=True)
    return dh, dy * h * r


def _gelu(x):
    t = jnp.tanh(GELU_K * (x + GELU_C * x * x * x))
    return 0.5 * x * (1.0 + t), t


def _gelu_grad(x, t):
    return 0.5 * (1.0 + t) + 0.5 * x * (1.0 - t * t) * GELU_K * (1.0 + 3.0 * GELU_C * x * x)


def _alibi_slopes():
    def pow2(n):
        start = 2.0 ** (-8.0 / n)
        return [start ** (i + 1) for i in range(n)]
    n = 12
    c = 2 ** int(math.floor(math.log2(n)))
    s = pow2(c) + pow2(2 * c)[0::2][: n - c]
    return np.array(sorted(s, reverse=True), dtype=np.float32).reshape(3, 4)


def _attn_bias(g):
    win, dil = DIL_GROUPS[g]
    steps = (np.arange(BLK)[:, None] + BLK) - np.arange(2 * BLK)[None, :]
    valid = (steps >= 0) & (steps <= win // dil)
    dist = (np.clip(steps, 0, None) * dil).astype(np.float32)
    b = -_alibi_slopes()[g][:, None, None] * dist[None]
    return np.where(valid[None], b, NEG).astype(np.float32)


def _head_masks():
    lane = lax.broadcasted_iota(jnp.int32, (1, GROUP_W), 1)
    return lane, [(lane >= HEAD * h) & (lane < HEAD * (h + 1)) for h in range(4)]


ATT_NB = 8


def _stack_heads(t, masks):
    z = jnp.zeros_like(t)
    return jnp.concatenate([jnp.where(m, t, z) for m in masks], axis=0)


def _unstack_heads(t, masks):
    out = jnp.zeros((BLK, GROUP_W), t.dtype)
    for h, m in enumerate(masks):
        out = jnp.where(m, t[h * BLK:(h + 1) * BLK], out)
    return out


def _stack_cols(ref, rows):
    return jnp.concatenate([ref[rows, HEAD * h:HEAD * h + 1] for h in range(4)], axis=0)


def _dil_spec(d, tm, w):
    return pl.BlockSpec((d, tm // d, w), lambda i: (0, i, 0))


def _to_dilated(val, s_ref, d, write):
    tm, w = val.shape
    for j in range(w // 128):
        s_ref[j, pl.ds(0, tm), :] = val[:, j * 128:(j + 1) * 128]
    for r in range(d):
        for j in range(w // 128):
            write(r, j, s_ref[j, pl.ds(r, tm // d, stride=d), :])


def _from_dilated(ref, s_ref, d, tm, w):
    if d == 1:
        return ref[0].astype(F32)
    for r in range(d):
        for j in range(w // 128):
            s_ref[j, pl.ds(r, tm // d, stride=d), :] = ref[r, :, j * 128:(j + 1) * 128].astype(F32)
    return jnp.concatenate([s_ref[j, pl.ds(0, tm), :] for j in range(w // 128)], axis=1)


def _fwd_in(x, g_mix, wt_in, gather=None, tm=512):
    S = x.shape[0]
    dils = [d for _, d in DIL_GROUPS]
    n = 0 if gather is None else gather.n
    last = S // tm - 1

    def body(*refs):
        x_ref, g_ref, w_ref = refs[:3]
        a_ref, q0_ref, q1_ref, q2_ref, uv_ref, gl_ref = refs[3 + n:9 + n]
        s_ref = refs[9 + 2 * n]
        comm = (refs[3:3 + n], refs[9 + n:9 + 2 * n], refs[10 + 2 * n:])
        if gather is not None:
            pl.when(pl.program_id(0) == 0)(lambda: gather.start(*comm))
        xv = x_ref[...]
        a = (xv * _rms(xv) * g_ref[...]).astype(BF16)
        a_ref[...] = a
        for g, (d, out) in enumerate(zip(dils, (q0_ref, q1_ref, q2_ref))):
            for part in range(3):
                rows = part * 768 + g * 256
                val = _dot_nt(a, w_ref[rows:rows + 256, :])
                if d == 1:
                    out[0, :, part * 256:(part + 1) * 256] = val.astype(BF16)
                else:
                    def write(r, j, piece, out=out, part=part):
                        out[r, :, part * 256 + j * 128:part * 256 + (j + 1) * 128] = piece.astype(BF16)
                    _to_dilated(val, s_ref, d, write)
        uv_ref[...] = _dot_nt(a, w_ref[2304:3328, :]).astype(BF16)
        gl_ref[...] = _dot_nt(a, w_ref[3328:5376, :]).astype(BF16)
        if gather is not None:
            pl.when(pl.program_id(0) == last)(lambda: gather.finish(*comm))

    outs = _pallas(
        body, grid=(S // tm,), name="fwd_in",
        in_specs=[_row(tm, D), _whole(g_mix), RES] + [ANY] * n,
        out_specs=[_row(tm, D)] + [_dil_spec(d, tm, 768) for d in dils] + [_row(tm, 1024), _row(tm, 2048)] + [ANY] * n,
        out_shape=[_sds((S, D), BF16)] + [_sds((d, S // d, 768), BF16) for d in dils]
        + [_sds((S, 1024), BF16), _sds((S, 2048), BF16)] + ([] if gather is None else gather.out_shape),
        scratch_shapes=[pltpu.VMEM((2, tm, 128), F32)] + ([] if gather is None else gather.scratch),
        compiler_params=_params(("arbitrary",), 60),
    )(x, g_mix, wt_in, *([] if gather is None else gather.halves))
    return outs[:6], ([] if gather is None else gather.full(outs[6:]))


def _attn_fwd(qkv, g):
    d, L, _ = qkv.shape
    nb = L // BLK
    bias = jnp.asarray(_attn_bias(g).reshape(4 * BLK, 2 * BLK))
    NB = min(ATT_NB, nb)
    W = NB * BLK

    def body(q_ref, kc_ref, kp_ref, vc_ref, vp_ref, b_ref, o_ref, l_ref):
        st = pl.program_id(1)
        k_all = jnp.concatenate([kp_ref[...], kc_ref[...]], axis=0)
        v_all = jnp.concatenate([vp_ref[...], vc_ref[...]], axis=0)
        lane, masks = _head_masks()
        for b in range(NB):
            rows = slice(b * BLK, (b + 1) * BLK)
            kk, vv = k_all[b * BLK:(b + 2) * BLK], v_all[b * BLK:(b + 2) * BLK]
            s = _dot_nt(_stack_heads(q_ref[rows, :], masks), kk) * 0.125 + b_ref[...]
            if b == 0:
                s = s + jnp.where((st == 0) & (lane < BLK), NEG, 0.0).astype(F32)
            mx = jnp.max(s, axis=-1, keepdims=True)
            e = jnp.exp(s - mx)
            den = jnp.sum(e, axis=-1, keepdims=True)
            o_ref[rows, :] = _unstack_heads(_dot(e.astype(BF16), vv) / den, masks)
            l_ref[rows, :] = _unstack_heads(mx + jnp.log(den), masks)

    def wide(col):
        return pl.BlockSpec((None, W, GROUP_W), lambda r, s: (r, s, col))

    def before(col):
        return pl.BlockSpec((None, BLK, GROUP_W), lambda r, s: (r, jnp.maximum(s * NB - 1, 0), col))

    return _pallas(
        body, grid=(d, nb // NB), name=f"attn_fwd_g{g}",
        in_specs=[wide(0), wide(1), before(1), wide(2), before(2),
                  pl.BlockSpec((4 * BLK, 2 * BLK), lambda r, s: (0, 0))],
        out_specs=[wide(0), wide(0)],
        out_shape=[_sds((d, L, GROUP_W), F32), _sds((d, L, GROUP_W), F32)],
        compiler_params=_params(("parallel", "parallel"), 32),
    )(qkv, qkv, qkv, qkv, qkv, bias)


def _group_weights(l0, l1, l2):
    m = jnp.maximum(jnp.maximum(l0, l1), l2)
    e0, e1, e2 = jnp.exp(l0 - m), jnp.exp(l1 - m), jnp.exp(l2 - m)
    inv = 1.0 / (e0 + e1 + e2)
    return e0 * inv, e1 * inv, e2 * inv


def _sgu_forward(uvf, gs, wt_ref, bst_ref, mixed_s, tm):
    z, t = _gelu(uvf)
    u, v = z[:, :SGU_W], z[:, SGU_W:]
    rv = _rms(v)
    vnb = (v * rv * gs).astype(BF16)
    for ci in range(tm // 128):
        for g in range(4):
            rs, cs = slice(ci * 128, (ci + 1) * 128), slice(g * 128, (g + 1) * 128)
            mixed_s[rs, cs] = _dot(wt_ref[g], vnb[rs, cs]) + bst_ref[:, g:g + 1]
    return u, v, rv, vnb, t


def _fwd_mid(x, os_, ls_, uv, gl, wt, bst, g_sgu, b_gate, wt_ba, wt_bs, w_out, gather=None, tm=512):
    S = x.shape[0]
    dils = [d for _, d in DIL_GROUPS]
    n = 0 if gather is None else gather.n
    last = S // tm - 1

    def body(*refs):
        (x_ref, o0, o1, o2, l0, l1, l2, uv_ref, gl_ref, wt_ref, bst_ref, gs_ref, bg_ref, wba_ref, wbs_ref,
         wo_ref) = refs[:16]
        ya_ref, ys_ref, ba_ref, bs_ref, mg_ref, h1_ref = refs[16 + n:22 + n]
        mixed_s, il_s = refs[22 + 2 * n:24 + 2 * n]
        comm = (refs[16:16 + n], refs[22 + n:22 + 2 * n], refs[24 + 2 * n:])
        if gather is not None:
            pl.when(pl.program_id(0) == 0)(lambda: gather.start(*comm))
        ls = [_from_dilated(r, il_s, d, tm, GROUP_W) for r, d in zip((l0, l1, l2), dils)]
        alphas = _group_weights(*ls)
        ya = jnp.zeros((tm, GROUP_W), F32)
        for a, r, d in zip(alphas, (o0, o1, o2), dils):
            ya = ya + a * _from_dilated(r, il_s, d, tm, GROUP_W)
        yab = ya.astype(BF16)
        ya_ref[...] = yab
        u, _, _, _, _ = _sgu_forward(uv_ref[...].astype(F32), gs_ref[...], wt_ref, bst_ref, mixed_s, tm)
        ysb = (u * mixed_s[...]).astype(BF16)
        ys_ref[...] = ysb
        gates = jax.nn.sigmoid(gl_ref[...].astype(F32) + bg_ref[...])
        ba = _dot_nt(yab, wba_ref[...])
        bs = _dot_nt(ysb, wbs_ref[...])
        ba_ref[...] = ba.astype(BF16)
        bs_ref[...] = bs.astype(BF16)
        mgb = (gates[:, :D] * ba + gates[:, D:] * bs).astype(BF16)
        mg_ref[...] = mgb
        h1_ref[...] = x_ref[...] + _dot(mgb, wo_ref[...])
        if gather is not None:
            pl.when(pl.program_id(0) == last)(lambda: gather.finish(*comm))

    gw = _row(tm, GROUP_W)
    dil = [_dil_spec(d, tm, GROUP_W) for d in dils]
    outs = _pallas(
        body, grid=(S // tm,), name="fwd_mid",
        in_specs=[_row(tm, D)] + dil + dil + [_row(tm, 1024), _row(tm, 2048)]
        + [_whole(t) for t in (wt, bst, g_sgu, b_gate)] + [RES] * 3 + [ANY] * n,
        out_specs=[gw, _row(tm, SGU_W), _row(tm, D), _row(tm, D), _row(tm, D), _row(tm, D)] + [ANY] * n,
        out_shape=[_sds((S, GROUP_W), BF16), _sds((S, SGU_W), BF16), _sds((S, D), BF16), _sds((S, D), BF16),
                   _sds((S, D), BF16), _sds((S, D), F32)] + ([] if gather is None else gather.out_shape),
        scratch_shapes=[pltpu.VMEM((tm, SGU_W), F32), pltpu.VMEM((2, tm, 128), F32)]
        + ([] if gather is None else gather.scratch),
        compiler_params=_params(("arbitrary",), 56),
    )(x, *os_, *ls_, uv, gl, wt, bst, g_sgu, b_gate, wt_ba, wt_bs, w_out, *([] if gather is None else gather.halves))
    return outs[:6], ([] if gather is None else gather.full(outs[6:]))


def _mem_fwd(mem, g_mem, w_kv):
    def body(m_ref, g_ref, w_ref, mb_ref, kv_ref):
        mv = m_ref[...]
        mb = (mv * _rms(mv) * g_ref[...]).astype(BF16)
        mb_ref[...] = mb
        kv_ref[...] = _dot(mb, w_ref[...]).astype(BF16)

    shapes = [_sds(mem.shape, BF16), _sds((mem.shape[0], 2 * MEM_W), BF16)]
    return _pallas(
        body, name="mem_fwd", grid=(1,), in_specs=[_whole(t) for t in (mem, g_mem, w_kv)],
        out_specs=[_whole(t) for t in shapes], out_shape=shapes, compiler_params=_params(("arbitrary",), 32),
    )(mem, g_mem, w_kv)


def _cross_probs(qh, kh):
    s = _dot_nt(qh, kh) * (MEM_HD ** -0.5)
    e = jnp.exp(s - jnp.max(s, axis=-1, keepdims=True))
    return e / jnp.sum(e, axis=-1, keepdims=True)


def _fwd_cross(h1, g_cross, w_q, kv, wt_o, tm=512):
    S = h1.shape[0]

    def body(h_ref, g_ref, wq_ref, kv_ref, wo_ref, c_ref, qc_ref, oc_ref, h2_ref):
        hv = h_ref[...]
        cb = (hv * _rms(hv) * g_ref[...]).astype(BF16)
        c_ref[...] = cb
        qcb = _dot(cb, wq_ref[...]).astype(BF16)
        qc_ref[...] = qcb
        for h in range(MEM_HEADS):
            cs = slice(h * MEM_HD, (h + 1) * MEM_HD)
            p = _cross_probs(qcb[:, cs], kv_ref[:, cs])
            oc_ref[:, cs] = _dot(p.astype(BF16), kv_ref[:, MEM_W + h * MEM_HD:MEM_W + (h + 1) * MEM_HD]).astype(BF16)
        h2_ref[...] = hv + _dot_nt(oc_ref[...], wo_ref[...])

    return _pallas(
        body, grid=(S // tm,), name="fwd_cross",
        in_specs=[_row(tm, D), _whole(g_cross), RES, _whole(kv), RES],
        out_specs=[_row(tm, D), _row(tm, MEM_W), _row(tm, MEM_W), _row(tm, D)],
        out_shape=[_sds((S, D), BF16), _sds((S, MEM_W), BF16), _sds((S, MEM_W), BF16), _sds((S, D), F32)],
        compiler_params=_params(("parallel",), 40),
    )(h1, g_cross, w_q, kv, wt_o)


def _ffn_fwd_bwd(h2, target, g_ffn, g_final, wt_gu, w_down, tm=256):
    S = h2.shape[0]
    nch = D_FF // FF_CHUNK

    def body(h_ref, t_ref, gf_ref, gz_ref, wgu_ref, wd_ref,
             f_ref, act_ref, dgu_ref, dh3b_ref, dh2_ref, dh2b_ref, dgf_ref, dgz_ref, loss_ref, gu_s, dact_s):
        i = pl.program_id(0)

        @pl.when(i == 0)
        def _():
            dgf_ref[...] = jnp.zeros_like(dgf_ref)
            dgz_ref[...] = jnp.zeros_like(dgz_ref)
            loss_ref[...] = jnp.zeros_like(loss_ref)

        hv = h_ref[...]
        r2 = _rms(hv)
        gf = gf_ref[...]
        fb = (hv * r2 * gf).astype(BF16)
        f_ref[...] = fb
        gu_s[...] = _dot_nt(fb, wgu_ref[...])
        for c in range(nch):
            cs = slice(c * FF_CHUNK, (c + 1) * FF_CHUNK)
            us = slice(D_FF + c * FF_CHUNK, D_FF + (c + 1) * FF_CHUNK)
            gt = gu_s[:, cs]
            act_ref[:, cs] = (gt * jax.nn.sigmoid(gt) * gu_s[:, us]).astype(BF16)
        h3 = hv + _dot(act_ref[...], wd_ref[...])
        r3 = _rms(h3)
        gz = gz_ref[...]
        diff = h3 * r3 * gz - t_ref[...]
        dy = diff * (1.0 / D)
        dh3, dgz_rows = _rms_bwd(dy, h3, r3, gz)
        dh3b = dh3.astype(BF16)
        dh3b_ref[...] = dh3b
        dact_s[...] = _dot_nt(dh3b, wd_ref[...])
        for c in range(nch):
            cs = slice(c * FF_CHUNK, (c + 1) * FF_CHUNK)
            us = slice(D_FF + c * FF_CHUNK, D_FF + (c + 1) * FF_CHUNK)
            dact, gt, up = dact_s[:, cs], gu_s[:, cs], gu_s[:, us]
            sg = jax.nn.sigmoid(gt)
            dgu_ref[:, cs] = (dact * up * (sg * (1.0 + gt * (1.0 - sg)))).astype(BF16)
            dgu_ref[:, us] = (dact * (gt * sg)).astype(BF16)
        df = _dot(dgu_ref[...], wgu_ref[...])
        dhn, dgf_rows = _rms_bwd(df, hv, r2, gf)
        dh2 = dh3 + dhn
        dh2_ref[...] = dh2
        dh2b_ref[...] = dh2.astype(BF16)
        dgf_ref[...] += jnp.sum(dgf_rows, axis=0, keepdims=True)
        dgz_ref[...] += jnp.sum(dgz_rows, axis=0, keepdims=True)
        loss_ref[...] += jnp.sum(jnp.sum(diff * diff, axis=0, keepdims=True), axis=1, keepdims=True) * (0.5 / D)

    return _pallas(
        body, grid=(S // tm,), name="ffn_fwd_bwd",
        in_specs=[_row(tm, D), _row(tm, D), _whole(g_ffn), _whole(g_final), RES, RES],
        out_specs=[_row(tm, D), _row(tm, D_FF), _row(tm, 2 * D_FF), _row(tm, D), _row(tm, D), _row(tm, D),
                   _acc((1, D)), _acc((1, D)), _acc((1, 128))],
        out_shape=[_sds((S, D), BF16), _sds((S, D_FF), BF16), _sds((S, 2 * D_FF), BF16), _sds((S, D), BF16),
                   _sds((S, D), F32), _sds((S, D), BF16), _sds((1, D), F32), _sds((1, D), F32), _sds((1, 128), F32)],
        scratch_shapes=[pltpu.VMEM((tm, 2 * D_FF), F32), pltpu.VMEM((tm, D_FF), F32)],
        compiler_params=_params(("arbitrary",), 60),
    )(h2, target, g_ffn, g_final, wt_gu, w_down)


def _bwd_cross(dh2, h1, qc, g_cross, w_q, kv, wt_o, comm=None, tm=512):
    S = h1.shape[0]
    n = 0 if comm is None else comm.n
    last = S // tm - 1

    def body(*refs):
        d_ref, h_ref, qc_ref, g_ref, wq_ref, kv_ref, wo_ref = refs[:7]
        dqc_ref, dh1_ref, dh1b_ref, dkv_ref, dg_ref = refs[7 + n:12 + n]
        cargs = (refs[7:7 + n], refs[12 + n:12 + 2 * n], refs[12 + 2 * n:])
        i = pl.program_id(0)

        @pl.when(i == 0)
        def _():
            dkv_ref[...] = jnp.zeros_like(dkv_ref)
            dg_ref[...] = jnp.zeros_like(dg_ref)
            if comm is not None:
                comm.start(*cargs)

        dh2 = d_ref[...]
        doc = _dot(dh2.astype(BF16), wo_ref[...])
        qcb = qc_ref[...]
        for h in range(MEM_HEADS):
            cs = slice(h * MEM_HD, (h + 1) * MEM_HD)
            vs = slice(MEM_W + h * MEM_HD, MEM_W + (h + 1) * MEM_HD)
            qh, kh, vh = qcb[:, cs], kv_ref[:, cs], kv_ref[:, vs]
            p = _cross_probs(qh, kh)
            dohb = doc[:, cs].astype(BF16)
            dp = _dot_nt(dohb, vh)
            dsb = (p * (dp - jnp.sum(dp * p, axis=-1, keepdims=True)) * (MEM_HD ** -0.5)).astype(BF16)
            dqc_ref[:, cs] = _dot(dsb, kh).astype(BF16)
            dkv_ref[:, cs] += _dot_tn(dsb, qh)
            dkv_ref[:, vs] += _dot_tn(p.astype(BF16), dohb)
        dc = _dot_nt(dqc_ref[...], wq_ref[...])
        hv = h_ref[...]
        dhn, dg_rows = _rms_bwd(dc, hv, _rms(hv), g_ref[...])
        dh1 = dh2 + dhn
        dh1_ref[...] = dh1
        dh1b_ref[...] = dh1.astype(BF16)
        dg_ref[...] += jnp.sum(dg_rows, axis=0, keepdims=True)
        if comm is not None:
            pl.when(i == last)(lambda: comm.finish(*cargs))

    outs = _pallas(
        body, grid=(S // tm,), name="bwd_cross",
        in_specs=[_row(tm, D), _row(tm, D), _row(tm, MEM_W), _whole(g_cross), RES, _whole(kv), RES] + [ANY] * n,
        out_specs=[_row(tm, MEM_W), _row(tm, D), _row(tm, D), _acc((256, 2 * MEM_W)), _acc((1, D))] + [ANY] * n,
        out_shape=[_sds((S, MEM_W), BF16), _sds((S, D), F32), _sds((S, D), BF16), _sds((256, 2 * MEM_W), F32),
                   _sds((1, D), F32)] + ([] if comm is None else comm.out_shape),
        scratch_shapes=[] if comm is None else comm.scratch,
        compiler_params=_params(("arbitrary",), 48),
    )(dh2, h1, qc, g_cross, w_q, kv, wt_o, *([] if comm is None else comm.ins))
    return outs[:5], outs[5:]


def _mem_bwd(dkv, mem, mb, g_mem, w_kv):
    def body(dkv_ref, m_ref, mb_ref, g_ref, w_ref, dw_ref, dwb_ref, dg_ref):
        dkvb = dkv_ref[...].astype(BF16)
        dw = _dot_tn(mb_ref[...], dkvb)
        dw_ref[...] = dw
        dwb_ref[...] = dw.astype(BF16)
        dm = _dot_nt(dkvb, w_ref[...])
        mv = m_ref[...]
        dg_ref[...] = jnp.sum(dm * mv * _rms(mv), axis=0, keepdims=True)

    shapes = [_sds((D, 2 * MEM_W), F32), _sds((D, 2 * MEM_W), BF16), _sds((1, D), F32)]
    return _pallas(
        body, name="mem_bwd", grid=(1,), in_specs=[_whole(t) for t in (dkv, mem, mb, g_mem, w_kv)],
        out_specs=[_whole(t) for t in shapes], out_shape=shapes, compiler_params=_params(("arbitrary",), 40),
    )(dkv, mem, mb, g_mem, w_kv)


def _bwd_mid(dh1, gl, ba, bs, uv, ls_, ya, wt, bst, g_sgu, b_gate, wt_ba, wt_bs, w_out, tm=512):
    S = dh1.shape[0]
    dils = [d for _, d in DIL_GROUPS]

    def body(d_ref, gl_ref, ba_ref, bs_ref, uv_ref, l0, l1, l2, ya_ref,
             wt_ref, bst_ref, gs_ref, bg_ref, wba_ref, wbs_ref, wo_ref,
             dba_ref, dbs_ref, dgl_ref, duv_ref, do0, do1, do2, c0, c1, c2,
             dbg_ref, dgs_ref, dws_ref, dbsa_ref, mixed_s, dvn_s, il_s):
        i = pl.program_id(0)

        @pl.when(i == 0)
        def _():
            for r in (dbg_ref, dgs_ref, dws_ref, dbsa_ref):
                r[...] = jnp.zeros_like(r)

        dm = _dot_nt(d_ref[...].astype(BF16), wo_ref[...])
        gates = jax.nn.sigmoid(gl_ref[...].astype(F32) + bg_ref[...])
        g0, g1 = gates[:, :D], gates[:, D:]
        dbab = (dm * g0).astype(BF16)
        dbsb = (dm * g1).astype(BF16)
        dba_ref[...] = dbab
        dbs_ref[...] = dbsb
        dg0 = dm * ba_ref[...].astype(F32) * g0 * (1.0 - g0)
        dg1 = dm * bs_ref[...].astype(F32) * g1 * (1.0 - g1)
        dgl_ref[:, :D] = dg0.astype(BF16)
        dgl_ref[:, D:] = dg1.astype(BF16)
        dbg_ref[:, :D] += jnp.sum(dg0, axis=0, keepdims=True)
        dbg_ref[:, D:] += jnp.sum(dg1, axis=0, keepdims=True)
        dya = _dot(dbab, wba_ref[...])
        dys = _dot(dbsb, wbs_ref[...])

        uvf = uv_ref[...].astype(F32)
        gs = gs_ref[...]
        u, v, rv, vnb, t = _sgu_forward(uvf, gs, wt_ref, bst_ref, mixed_s, tm)
        du = dys * mixed_s[...]
        dmixed = dys * u
        for ci in range(tm // 128):
            for g in range(4):
                rs, cs = slice(ci * 128, (ci + 1) * 128), slice(g * 128, (g + 1) * 128)
                dmx = dmixed[rs, cs]
                dmxb = dmx.astype(BF16)
                dvn_s[rs, cs] = _dot_tn(wt_ref[g], dmxb)
                dws_ref[g] += _dot_nt(dmxb, vnb[rs, cs])
                dbsa_ref[g] += dmx
        dv, dgs_rows = _rms_bwd(dvn_s[...], v, rv, gs)
        dgs_ref[...] += jnp.sum(dgs_rows, axis=0, keepdims=True)
        gg = _gelu_grad(uvf, t)
        duv_ref[:, :SGU_W] = (du * gg[:, :SGU_W]).astype(BF16)
        duv_ref[:, SGU_W:] = (dv * gg[:, SGU_W:]).astype(BF16)

        alphas = _group_weights(*[_from_dilated(r, il_s, d, tm, GROUP_W) for r, d in zip((l0, l1, l2), dils)])
        prod = dya * ya_ref[...].astype(F32)
        _, masks = _head_masks()
        hs = jnp.zeros_like(prod)
        for h in range(4):
            sh = jnp.sum(jnp.where(masks[h], prod, 0.0), axis=-1, keepdims=True)
            hs = jnp.where(masks[h], sh, hs)
        for a, d, do_ref, c_ref in zip(alphas, dils, (do0, do1, do2), (c0, c1, c2)):
            for val, out in ((a * dya, do_ref), (a * hs, c_ref)):
                if d == 1:
                    out[0] = val.astype(out.dtype)
                else:
                    def write(r, j, piece, out=out):
                        out[r, :, j * 128:(j + 1) * 128] = piece.astype(out.dtype)
                    _to_dilated(val, il_s, d, write)

    gw = _row(tm, GROUP_W)
    dil = [_dil_spec(d, tm, GROUP_W) for d in dils]
    return _pallas(
        body, grid=(S // tm,), name="bwd_mid",
        in_specs=[_row(tm, D), _row(tm, 2048), _row(tm, D), _row(tm, D), _row(tm, 1024)] + dil + [gw]
        + [_whole(t) for t in (wt, bst, g_sgu, b_gate)] + [RES] * 3,
        out_specs=[_row(tm, D), _row(tm, D), _row(tm, 2048), _row(tm, 1024)] + dil + dil
        + [_acc((1, 2048)), _acc((1, SGU_W)), _acc((4, 128, 128)), _acc((4, 128, 128))],
        out_shape=[_sds((S, D), BF16), _sds((S, D), BF16), _sds((S, 2048), BF16), _sds((S, 1024), BF16)]
        + [_sds((d, S // d, GROUP_W), BF16) for d in dils] + [_sds((d, S // d, GROUP_W), F32) for d in dils]
        + [_sds((1, 2048), F32), _sds((1, SGU_W), F32), _sds((4, 128, 128), F32), _sds((4, 128, 128), F32)],
        scratch_shapes=[pltpu.VMEM((tm, SGU_W), F32), pltpu.VMEM((tm, SGU_W), F32), pltpu.VMEM((2, tm, 128), F32)],
        compiler_params=_params(("arbitrary",), 60),
    )(dh1, gl, ba, bs, uv, *ls_, ya, wt, bst, g_sgu, b_gate, wt_ba, wt_bs, w_out)


def _attn_bwd(qkv, do, lse, corr, g):
    d, L, _ = qkv.shape
    nb = L // BLK
    NB = min(ATT_NB, nb)
    W = NB * BLK
    nsteps = nb // NB
    bias = jnp.asarray(_attn_bias(g).reshape(4 * BLK, 2 * BLK))

    def body(q_ref, kc_ref, kp_ref, vc_ref, vp_ref, do_ref, l_ref, c_ref, qn_ref, don_ref, ln_ref, cn_ref, b_ref,
             out_ref, dk_s, dv_s):
        st = pl.program_id(1)
        k_all = jnp.concatenate([kp_ref[...], kc_ref[...]], axis=0)
        v_all = jnp.concatenate([vp_ref[...], vc_ref[...]], axis=0)
        lane, masks = _head_masks()
        dk_s[...] = jnp.zeros_like(dk_s)
        dv_s[...] = jnp.zeros_like(dv_s)

        def block_terms(qs, dos, kk, vv, bias_v, lse_c, corr_c):
            s = _dot_nt(qs, kk) * 0.125 + bias_v
            p = jnp.exp(s - lse_c)
            dsb = (p * (_dot_nt(dos, vv) - corr_c) * 0.125).astype(BF16)
            return dsb, p.astype(BF16)

        for b in range(NB):
            rows = slice(b * BLK, (b + 1) * BLK)
            keys = slice(b * BLK, (b + 2) * BLK)
            kk, vv = k_all[keys], v_all[keys]
            qs, dos = _stack_heads(q_ref[rows, :], masks), _stack_heads(do_ref[rows, :], masks)
            bias_v = b_ref[...]
            if b == 0:
                bias_v = bias_v + jnp.where((st == 0) & (lane < BLK), NEG, 0.0).astype(F32)
            dsb, pb = block_terms(qs, dos, kk, vv, bias_v, _stack_cols(l_ref, rows), _stack_cols(c_ref, rows))
            out_ref[rows, 0:GROUP_W] = _unstack_heads(_dot(dsb, kk), masks).astype(BF16)
            dk_s[keys, :] += _dot_tn(dsb, qs)
            dv_s[keys, :] += _dot_tn(pb, dos)

        @pl.when(st < nsteps - 1)
        def _():
            last = slice(NB * BLK, (NB + 1) * BLK)
            qs, dos = _stack_heads(qn_ref[...], masks), _stack_heads(don_ref[...], masks)
            every = slice(None)
            dsb, pb = block_terms(qs, dos, k_all[last], v_all[last], b_ref[:, :BLK],
                                  _stack_cols(ln_ref, every), _stack_cols(cn_ref, every))
            dk_s[last, :] += _dot_tn(dsb, qs)
            dv_s[last, :] += _dot_tn(pb, dos)

        out_ref[:, GROUP_W:2 * GROUP_W] = dk_s[BLK:, :].astype(BF16)
        out_ref[:, 2 * GROUP_W:] = dv_s[BLK:, :].astype(BF16)

    def wide(col, w=GROUP_W):
        return pl.BlockSpec((None, W, w), lambda r, s: (r, s, col))

    def before(col):
        return pl.BlockSpec((None, BLK, GROUP_W), lambda r, s: (r, jnp.maximum(s * NB - 1, 0), col))

    def after(col):
        return pl.BlockSpec((None, BLK, GROUP_W), lambda r, s: (r, jnp.minimum((s + 1) * NB, nb - 1), col))

    return _pallas(
        body, grid=(d, nsteps), name=f"attn_bwd_g{g}",
        in_specs=[wide(0), wide(1), before(1), wide(2), before(2), wide(0), wide(0), wide(0),
                  after(0), after(0), after(0), after(0), pl.BlockSpec((4 * BLK, 2 * BLK), lambda r, s: (0, 0))],
        out_specs=wide(0, 768),
        out_shape=_sds((d, L, 768), BF16),
        scratch_shapes=[pltpu.VMEM(((NB + 1) * BLK, GROUP_W), F32), pltpu.VMEM(((NB + 1) * BLK, GROUP_W), F32)],
        compiler_params=_params(("parallel", "parallel"), 32),
    )(qkv, qkv, qkv, qkv, qkv, do, lse, corr, qkv, do, lse, corr, bias)


def _bwd_in(dqkvs, duv, dgl, dh1, x, g_mix, wt_in, tm=512):
    S = x.shape[0]
    dils = [d for _, d in DIL_GROUPS]

    def body(q0_ref, q1_ref, q2_ref, duv_ref, dgl_ref, d_ref, x_ref, g_ref, w_ref, dx_ref, dp_ref, dg_ref, il_s):
        i = pl.program_id(0)

        @pl.when(i == 0)
        def _():
            dg_ref[...] = jnp.zeros_like(dg_ref)

        for g, (d, ref) in enumerate(zip(dils, (q0_ref, q1_ref, q2_ref))):
            nat = _from_dilated(ref, il_s, d, tm, 768).astype(BF16)
            for part in range(3):
                col = part * 768 + g * 256
                dp_ref[:, col:col + 256] = nat[:, part * 256:(part + 1) * 256]
        dp_ref[:, 2304:3328] = duv_ref[...]
        dp_ref[:, 3328:5376] = dgl_ref[...]
        da = _dot(dp_ref[...], w_ref[...])
        xv = x_ref[...]
        dxn, dg_rows = _rms_bwd(da, xv, _rms(xv), g_ref[...])
        dx_ref[...] = d_ref[...] + dxn
        dg_ref[...] += jnp.sum(dg_rows, axis=0, keepdims=True)

    return _pallas(
        body, grid=(S // tm,), name="bwd_in",
        in_specs=[_dil_spec(d, tm, 768) for d in dils] + [_row(tm, 1024), _row(tm, 2048), _row(tm, D), _row(tm, D),
                                                          _whole(g_mix), RES],
        out_specs=[_row(tm, D), _row(tm, 5376), _acc((1, D))],
        out_shape=[_sds((S, D), F32), _sds((S, 5376), BF16), _sds((1, D), F32)],
        scratch_shapes=[pltpu.VMEM((6, tm, 128), F32)],
        compiler_params=_params(("arbitrary",), 60),
    )(*dqkvs, duv, dgl, dh1, x, g_mix, wt_in)


def _tn_matmul(a, b, name, tk, ts=2048, comm=None):
    S, K = a.shape
    N = b.shape[1]
    n = 0 if comm is None else comm.n
    nk, ns = K // tk, S // ts

    def body(*refs):
        a_ref, b_ref, o_ref, ob_ref = refs[0], refs[1], refs[2 + n], refs[3 + n]
        cargs = (refs[2:2 + n], refs[4 + n:4 + 2 * n], refs[4 + 2 * n:])
        k, s = pl.program_id(0), pl.program_id(1)
        if comm is not None:
            pl.when((k == 0) & (s == 0))(lambda: comm.start(*cargs))

        @pl.when(s == 0)
        def _():
            o_ref[...] = jnp.zeros_like(o_ref)

        o_ref[...] += _dot_tn(a_ref[...], b_ref[...])

        @pl.when(s == ns - 1)
        def _():
            ob_ref[...] = o_ref[...].astype(BF16)

        if comm is not None:
            pl.when((k == nk - 1) & (s == ns - 1))(lambda: comm.finish(*cargs))

    tile = pl.BlockSpec((tk, N), lambda k, s: (k, 0))
    outs = _pallas(
        body, grid=(nk, ns), name=name,
        in_specs=[pl.BlockSpec((ts, tk), lambda k, s: (s, k)), pl.BlockSpec((ts, N), lambda k, s: (s, 0))] + [ANY] * n,
        out_specs=[tile, tile] + [ANY] * n,
        out_shape=[_sds((K, N), F32), _sds((K, N), BF16)] + ([] if comm is None else comm.out_shape),
        scratch_shapes=[] if comm is None else comm.scratch,
        compiler_params=_params(("arbitrary", "arbitrary"), 56),
    )(a, b, *([] if comm is None else comm.ins))
    pair = (outs[0], outs[1])
    return pair if comm is None else (pair, outs[2:])


def _chip_peers(x, y):
    return [(1 - x, y), (x, 1 - y), (1 - x, 1 - y)]


STAGE_BYTES = 2 << 20


def _chunk_plan(shapes, itemsize):
    plan = []
    for i, (rows, w) in enumerate(shapes):
        ch = max(16, min(rows, (STAGE_BYTES // (w * itemsize)) // 16 * 16))
        while rows % ch:
            ch -= 16
        plan += [(i, r0, ch) for r0 in range(0, rows, ch)]
    return plan


def _remote(src, dst, ssem, rsem, dev):
    return pltpu.make_async_remote_copy(src_ref=src, dst_ref=dst, send_sem=ssem, recv_sem=rsem, device_id=dev,
                                        device_id_type=MESH)


class _Gather:
    def __init__(self, shards):
        self.n = len(shards)
        self.shards = shards
        self.halves = [s.reshape(2, s.shape[0] // 2, s.shape[1]) for s in shards]
        self.plan = _chunk_plan([h.shape[1:] for h in self.halves], 2)
        self.out_shape = [_sds((4,) + h.shape, BF16) for h in self.halves]
        n = self.n
        self.scratch = [pltpu.SemaphoreType.DMA((6 * n,)), pltpu.SemaphoreType.DMA((6 * n,)),
                        pltpu.SemaphoreType.DMA((2,)), pltpu.SemaphoreType.DMA((2,)),
                        pltpu.VMEM((2, max(p[2] for p in self.plan), max(h.shape[2] for h in self.halves)), BF16)]

    def full(self, outs):
        return [o.reshape(4 * s.shape[0], s.shape[1]) for o, s in zip(outs, self.shards)]

    def _sends(self, ins, outs, ssem, rsem):
        x, y, c = lax.axis_index("x"), lax.axis_index("y"), lax.axis_index("c")
        me = 2 * x + y
        return [_remote(ins[i].at[c], outs[i].at[me, c], ssem.at[6 * i + k], rsem.at[6 * i + k], (px, py, c))
                for i in range(self.n) for k, (px, py) in enumerate(_chip_peers(x, y))]

    def start(self, ins, outs, scratch):
        ssem, rsem, lsem, osem, buf = scratch
        me = 2 * lax.axis_index("x") + lax.axis_index("y")
        for cp in self._sends(ins, outs, ssem, rsem):
            cp.start()
        pending = {}
        for i, r0, ch in self.plan:
            for h in range(2):
                if h in pending:
                    pending[h].wait()
                stage = buf.at[h, pl.ds(0, ch), pl.ds(0, self.halves[i].shape[2])]
                ld = pltpu.make_async_copy(ins[i].at[h, pl.ds(r0, ch)], stage, lsem.at[h])
                ld.start()
                ld.wait()
                st = pltpu.make_async_copy(stage, outs[i].at[me, h, pl.ds(r0, ch)], osem.at[h])
                st.start()
                pending[h] = st
        for st in pending.values():
            st.wait()

    def finish(self, ins, outs, scratch):
        ssem, rsem = scratch[:2]
        x, y, c = lax.axis_index("x"), lax.axis_index("y"), lax.axis_index("c")
        chips = _chip_peers(x, y)
        sib = (x, y, 1 - c)
        forwards = []
        for i in range(self.n):
            for k, (px, py) in enumerate(chips):
                landed = outs[i].at[2 * px + py, c]
                _remote(landed, landed, ssem.at[6 * i + k], rsem.at[6 * i + k], (px, py, c)).wait_recv()
                cp = _remote(landed, landed, ssem.at[6 * i + 3 + k], rsem.at[6 * i + 3 + k], sib)
                cp.start()
                forwards.append(cp)
        for i in range(self.n):
            for k, (px, py) in enumerate(chips):
                passed = outs[i].at[2 * px + py, 1 - c]
                _remote(passed, passed, ssem.at[6 * i + 3 + k], rsem.at[6 * i + 3 + k], sib).wait_recv()
        for cp in self._sends(ins, outs, ssem, rsem) + forwards:
            cp.wait_send()


def _gather_weights(shards):
    gt = _Gather(shards)
    n = gt.n

    def body(*refs):
        ins, outs, scratch = refs[:n], refs[n:2 * n], refs[2 * n:]
        gt.start(ins, outs, scratch)
        gt.finish(ins, outs, scratch)

    outs = _pallas(
        body, name="gather_weights", in_specs=[ANY] * n, out_specs=[ANY] * n, out_shape=gt.out_shape,
        scratch_shapes=gt.scratch, compiler_params=pltpu.CompilerParams(vmem_limit_bytes=32 << 20),
    )(*gt.halves)
    return gt.full(outs)


def _swap_halves(grads):
    n = len(grads)
    view = lambda g: g.reshape(4, 2, g.shape[0] // 8, g.shape[1])
    g4f = [view(g) for g, _ in grads]
    g4 = [view(gb) for _, gb in grads]

    def body(*refs):
        ins, got = refs[:n], refs[n:2 * n]
        ssem, rsem = refs[2 * n:]
        x, y, c = lax.axis_index("x"), lax.axis_index("y"), lax.axis_index("c")
        sib = (x, y, 1 - c)
        cps = []
        for i in range(n):
            rc = _remote(ins[i].at[:, 1 - c], got[i], ssem.at[i], rsem.at[i], sib)
            rc.start()
            cps.append(rc)
        for cp in cps:
            cp.wait()

    half = [_sds((4, g.shape[2], g.shape[3]), BF16) for g in g4]
    got = _pallas(
        body, name="swap_halves", in_specs=[ANY] * n, out_specs=[ANY] * n, out_shape=half,
        scratch_shapes=[pltpu.SemaphoreType.DMA((n,)), pltpu.SemaphoreType.DMA((n,))],
    )(*g4)
    return g4f, got


def _chip_sum(g4, got, name):
    _, _, R, W = g4.shape
    tr = _tile(R, max(16, min(512, (1 << 18) // W // 16 * 16)))
    c = lax.axis_index("c").astype(jnp.int32).reshape(1)

    def body(c_ref, a_ref, b_ref, s_ref, sb_ref):
        s = a_ref[...] + b_ref[...].astype(F32)
        s_ref[...] = s
        sb_ref[...] = s.astype(BF16)

    plain = pl.BlockSpec((None, tr, W), lambda j, t, c_ref: (j, t, 0))
    return _pallas(
        body, name=name,
        grid_spec=pltpu.PrefetchScalarGridSpec(
            num_scalar_prefetch=1, grid=(4, R // tr),
            in_specs=[pl.BlockSpec((None, None, tr, W), lambda j, t, c_ref: (j, c_ref[0], t, 0)), plain],
            out_specs=[plain, plain]),
        out_shape=[_sds((4, R, W), F32), _sds((4, R, W), BF16)],
        compiler_params=_params(("parallel", "parallel"), 32),
    )(c, g4, got)


class _Scatter:
    def __init__(self, sums_b):
        self.n = len(sums_b)
        self.ins = list(sums_b)
        self.out_shape = [_sds((3,) + s.shape[1:], BF16) for s in sums_b]
        self.scratch = [pltpu.SemaphoreType.DMA((3 * self.n,)), pltpu.SemaphoreType.DMA((3 * self.n,))]

    def _copies(self, ins, outs, scratch):
        ssem, rsem = scratch
        x, y, c = lax.axis_index("x"), lax.axis_index("y"), lax.axis_index("c")
        return [_remote(ins[i].at[2 * px + py], outs[i].at[k], ssem.at[3 * i + k], rsem.at[3 * i + k], (px, py, c))
                for i in range(self.n) for k, (px, py) in enumerate(_chip_peers(x, y))]

    def start(self, ins, outs, scratch):
        for cp in self._copies(ins, outs, scratch):
            cp.start()

    def finish(self, ins, outs, scratch):
        for cp in self._copies(ins, outs, scratch):
            cp.wait()


def _scatter_partials(sc):
    n = sc.n

    def body(*refs):
        args = (refs[:n], refs[n:2 * n], refs[2 * n:])
        sc.start(*args)
        sc.finish(*args)

    return _pallas(
        body, name="scatter_partials", in_specs=[ANY] * n, out_specs=[ANY] * n, out_shape=sc.out_shape,
        scratch_shapes=sc.scratch,
    )(*sc.ins)


class _Reduce:
    def __init__(self, grads, names):
        self.names = names
        g4, got = _swap_halves(grads)
        self.sums, sums_b = [], []
        for nm, g, t in zip(names, g4, got):
            s_, sb_ = _chip_sum(g, t, f"chip_sum_{nm}")
            self.sums.append(s_)
            sums_b.append(sb_)
        self.scatter = _Scatter(sums_b)

    def collect(self, parts):
        return [_mesh_sum(s, p, f"mesh_sum_{nm}") for nm, s, p in zip(self.names, self.sums, parts)]


def _mesh_sum(sums, parts, name):
    _, R, W = sums.shape
    tr = _tile(R, max(16, min(512, (1 << 18) // W // 16 * 16)))
    me = (2 * lax.axis_index("x") + lax.axis_index("y")).astype(jnp.int32).reshape(1)

    def body(me_ref, m_ref, p_ref, o_ref):
        o_ref[...] = m_ref[...] + p_ref[0].astype(F32) + p_ref[1].astype(F32) + p_ref[2].astype(F32)

    return _pallas(
        body, name=name,
        grid_spec=pltpu.PrefetchScalarGridSpec(
            num_scalar_prefetch=1, grid=(R // tr,),
            in_specs=[pl.BlockSpec((None, tr, W), lambda i, me_ref: (me_ref[0], i, 0)),
                      pl.BlockSpec((3, tr, W), lambda i, me_ref: (0, i, 0))],
            out_specs=pl.BlockSpec((tr, W), lambda i, me_ref: (i, 0))),
        out_shape=_sds((R, W), F32), compiler_params=_params(("parallel",), 32),
    )(me, sums, parts)


def _share_halves(reduced):
    n = len(reduced)
    plan = _chunk_plan([r.shape for r in reduced], 4)
    max_rows = max(p[2] for p in plan)
    max_w = max(r.shape[1] for r in reduced)

    def body(*refs):
        ins, outs = refs[:n], refs[n:2 * n]
        ssem, rsem, lsem, osem, buf = refs[2 * n:]
        x, y, c = lax.axis_index("x"), lax.axis_index("y"), lax.axis_index("c")
        sib = (x, y, 1 - c)
        pending = {}
        for k, (i, r0, ch) in enumerate(plan):
            slot = k % 2
            if slot in pending:
                rc, lc = pending[slot]
                rc.wait_send()
                lc.wait()
            stage = buf.at[slot, pl.ds(0, ch), pl.ds(0, reduced[i].shape[1])]
            ld = pltpu.make_async_copy(ins[i].at[pl.ds(r0, ch)], stage, lsem.at[slot])
            ld.start()
            ld.wait()
            place = outs[i].at[c, pl.ds(r0, ch)]
            rc = _remote(stage, place, ssem.at[slot], rsem.at[i], sib)
            lc = pltpu.make_async_copy(stage, place, osem.at[slot])
            rc.start()
            lc.start()
            pending[slot] = (rc, lc)
        for rc, lc in pending.values():
            rc.wait_send()
            lc.wait()
        for i in range(n):
            theirs = outs[i].at[1 - c]
            _remote(theirs, theirs, ssem.at[0], rsem.at[i], sib).wait_recv()

    outs = _pallas(
        body, name="share_halves", in_specs=[ANY] * n, out_specs=[ANY] * n,
        out_shape=[_sds((2,) + r.shape, F32) for r in reduced],
        scratch_shapes=[pltpu.SemaphoreType.DMA((2,)), pltpu.SemaphoreType.DMA((n,)), pltpu.SemaphoreType.DMA((2,)),
                        pltpu.SemaphoreType.DMA((2,)), pltpu.VMEM((2, max_rows, max_w), F32)],
        compiler_params=pltpu.CompilerParams(vmem_limit_bytes=32 << 20),
    )(*reduced)
    return [o.reshape(2 * r.shape[0], r.shape[1]) for o, r in zip(outs, reduced)]


def _tile(rows, cap=256):
    t = min(rows, cap) // 16 * 16
    while rows % t:
        t -= 16
    return t


def _elementwise(fn, ins, out_dtypes, name):
    R, W = ins[0].shape
    tr = _tile(R, max(8, min(512, (1 << 18) // W // 8 * 8)))

    def body(*refs):
        outs = fn(*[r[...] for r in refs[:len(ins)]])
        for o_ref, o in zip(refs[len(ins):], outs):
            o_ref[...] = o.astype(o_ref.dtype)

    return _pallas(
        body, grid=(R // tr,), name=name, in_specs=[_row(tr, W)] * len(ins), out_specs=[_row(tr, W)] * len(out_dtypes),
        out_shape=[_sds((R, W), dt) for dt in out_dtypes],
        compiler_params=_params(("parallel",), 48),
    )(*ins)


def _adamw(w, g, m, v):
    m = B1 * m + (1.0 - B1) * g
    v = B2 * v + (1.0 - B2) * (g * g)
    m_hat = m / (1.0 - B1 ** STEP)
    v_hat = v / (1.0 - B2 ** STEP)
    return -LR * (m_hat / (jnp.sqrt(v_hat) + AEPS) + WD * w), m, v


def _exchange_small(sent):
    ns = len(sent)

    def body(*refs):
        p_refs, o_refs, all_s = refs[:ns], refs[ns:2 * ns], refs[2 * ns:3 * ns]
        ssem, rsem = refs[-2:]
        x, y, c = lax.axis_index("x"), lax.axis_index("y"), lax.axis_index("c")
        me = 4 * x + 2 * y + c
        for i in range(ns):
            all_s[i][me] = p_refs[i][...]
        cps = []
        for rel in range(1, 8):
            peer = (1 - x if rel & 4 else x, 1 - y if rel & 2 else y, 1 - c if rel & 1 else c)
            for i in range(ns):
                k = (rel - 1) * ns + i
                mine = all_s[i].at[me]
                rc = _remote(mine, mine, ssem.at[k], rsem.at[k], peer)
                rc.start()
                cps.append((rc, i, k, 4 * peer[0] + 2 * peer[1] + peer[2]))
        for rc, i, k, peer_slot in cps:
            rc.wait_send()
            theirs = all_s[i].at[peer_slot]
            _remote(theirs, theirs, ssem.at[k], rsem.at[k], (x, y, c)).wait_recv()
        for i in range(ns):
            o_refs[i][...] = all_s[i][...]

    shapes = [_sds((8,) + t.shape, F32) for t in sent]
    return _pallas(
        body, name="exchange_small", grid=(1,), in_specs=[_whole(t) for t in sent],
        out_specs=[_whole(t) for t in shapes], out_shape=shapes,
        scratch_shapes=[pltpu.VMEM(t.shape, F32) for t in shapes]
        + [pltpu.SemaphoreType.DMA((7 * ns,)), pltpu.SemaphoreType.DMA((7 * ns,))],
        compiler_params=_params(("arbitrary",), 32),
    )(*sent)


def _adam_small(ws, ms, vs, parts, loss_part):
    n = len(ws)
    gathered = _exchange_small(list(parts) + [loss_part])

    def body(*refs):
        w_refs, m_refs, v_refs, a_refs = refs[:n], refs[n:2 * n], refs[2 * n:3 * n], refs[3 * n:4 * n + 1]
        outs = refs[4 * n + 1:]
        g_refs, d_refs, nm_refs, nv_refs, loss_ref = outs[:n], outs[n:2 * n], outs[2 * n:3 * n], outs[3 * n:4 * n], outs[4 * n]

        def total(i):
            t = a_refs[i][0]
            for k in range(1, 8):
                t = t + a_refs[i][k]
            return t

        for i in range(n):
            g = total(i)
            g_refs[i][...] = g
            d_refs[i][...], nm_refs[i][...], nv_refs[i][...] = _adamw(w_refs[i][...], g, m_refs[i][...], v_refs[i][...])
        loss_ref[...] = total(n)

    shapes = [_sds(w.shape, F32) for w in ws] * 4 + [_sds(loss_part.shape, F32)]
    ins = [*ws, *ms, *vs, *gathered]
    outs = _pallas(
        body, name="adam_small", grid=(1,), in_specs=[_whole(t) for t in ins], out_specs=[_whole(t) for t in shapes],
        out_shape=shapes, compiler_params=_params(("arbitrary",), 32),
    )(*ins)
    return outs[:n], outs[n:2 * n], outs[2 * n:3 * n], outs[3 * n:4 * n], outs[4 * n]


def _local_step(xs, tgt, mems, weights, small, gather_mid=None, gather_ffn=None, reduce=False):
    wt_in, wt_ba, wt_bs, wo, wq, wkv, wt_o, wt_gu, wd = weights
    g_mix, b_gate, w_sgu, b_sgu, g_sgu, g_cross, g_mem, g_ffn, g_final = small
    wt = jnp.tril(w_sgu).astype(BF16)
    bst = b_sgu.T

    (a, qkv0, qkv1, qkv2, uv, gl), got = _fwd_in(xs, g_mix, wt_in, gather_mid)
    if gather_mid is not None:
        wt_ba, wt_bs, wo, wq, wkv, wt_o = got
    qkvs = (qkv0, qkv1, qkv2)
    os_, ls_ = zip(*[_attn_fwd(qkvs[g], g) for g in range(3)])
    (ya, ys, ba, bs, mg, h1), got = _fwd_mid(xs, os_, ls_, uv, gl, wt, bst, g_sgu, b_gate, wt_ba, wt_bs, wo, gather_ffn)
    if gather_ffn is not None:
        wt_gu, wd = got
    mb, kv = _mem_fwd(mems, g_mem, wkv)
    cb, qc, oc, h2 = _fwd_cross(h1, g_cross, wq, kv, wt_o)
    f, act, dgu, dh3b, dh2, dh2b, dg_ffn, dg_final, loss = _ffn_fwd_bwd(h2, tgt, g_ffn, g_final, wt_gu, wd)

    g_ffn_w = [_tn_matmul(dgu, f, "dw_gate_up", 1408), _tn_matmul(act, dh3b, "dw_down", 1408)]
    r_ffn = _Reduce(g_ffn_w, ["w_gate_up", "w_down"]) if reduce else None
    (dqc, dh1, dh1b, dkv, dg_cross), parts_ffn = _bwd_cross(dh2, h1, qc, g_cross, wq, kv, wt_o,
                                                           r_ffn.scatter if reduce else None)
    dw_kv, dw_kvb, dg_mem = _mem_bwd(dkv, mems, mb, g_mem, wkv)
    (dba, dbs, dgl, duv, do0, do1, do2, c0, c1, c2, db_gate, dg_sgu, dws, dbs_acc) = _bwd_mid(
        dh1, gl, ba, bs, uv, ls_, ya, wt, bst, g_sgu, b_gate, wt_ba, wt_bs, wo)
    dqkvs = [_attn_bwd(qkvs[g], do, ls_[g], corr, g) for g, (do, corr) in enumerate(((do0, c0), (do1, c1), (do2, c2)))]
    grad_x, dproj, dg_mix = _bwd_in(dqkvs, duv, dgl, dh1, xs, g_mix, wt_in)
    g_mid_w = [_tn_matmul(dba, ya, "dw_branch_attn", 1024),
               _tn_matmul(dbs, ys, "dw_branch_sgu", 1024),
               _tn_matmul(mg, dh1b, "dw_out", 1024),
               _tn_matmul(cb, dqc, "dw_q_cross", 1024),
               (dw_kv, dw_kvb),
               _tn_matmul(dh2b, oc, "dw_o_cross", 1024)]
    small_terms = (dg_mix, db_gate, dws, dbs_acc, dg_sgu, dg_cross, dg_mem, dg_ffn, dg_final)
    if not reduce:
        full = [_tn_matmul(dproj, a, "dw_in", 1792)] + g_mid_w + g_ffn_w
        return loss, grad_x, [g for g, _ in full], small_terms
    r_mid = _Reduce(g_mid_w, ["w_branch_attn", "w_branch_sgu", "w_out", "w_q_cross", "w_kv_cross", "w_o_cross"])
    g_in, parts_mid = _tn_matmul(dproj, a, "dw_in", 1792, comm=r_mid.scatter)
    r_in = _Reduce([g_in], ["w_in"])
    halves = r_in.collect(_scatter_partials(r_in.scatter)) + r_mid.collect(parts_mid) + r_ffn.collect(parts_ffn)
    return loss, grad_x, halves, small_terms


def kernel(x, mem, g_mix, w_in, b_gate, w_sgu_spatial, b_sgu_spatial, g_sgu, w_branch_attn, w_branch_sgu, w_out, g_cross, g_mem, w_q_cross, w_kv_cross, w_o_cross, g_ffn, w_gate_up, w_down, g_final, loss_target, m_g_mix, m_w_in, m_b_gate, m_w_sgu_spatial, m_b_sgu_spatial, m_g_sgu, m_w_branch_attn, m_w_branch_sgu, m_w_out, m_g_cross, m_g_mem, m_w_q_cross, m_w_kv_cross, m_w_o_cross, m_g_ffn, m_w_gate_up, m_w_down, m_g_final, v_g_mix, v_w_in, v_b_gate, v_w_sgu_spatial, v_b_sgu_spatial, v_g_sgu, v_w_branch_attn, v_w_branch_sgu, v_w_out, v_g_cross, v_g_mem, v_w_q_cross, v_w_kv_cross, v_w_o_cross, v_g_ffn, v_w_gate_up, v_w_down, v_g_final):
    S = x.shape[1]
    xs, tgt, mems = x.reshape(S, D), loss_target.reshape(S, D), mem.reshape(mem.shape[1], D)
    g_final2 = g_final.reshape(1, D)

    big = [("w_in", w_in[0], m_w_in[0], v_w_in[0], True),
           ("w_branch_attn", w_branch_attn[0], m_w_branch_attn[0], v_w_branch_attn[0], True),
           ("w_branch_sgu", w_branch_sgu[0], m_w_branch_sgu[0], v_w_branch_sgu[0], True),
           ("w_out", w_out[0], m_w_out[0], v_w_out[0], False),
           ("w_q_cross", w_q_cross[0], m_w_q_cross[0], v_w_q_cross[0], False),
           ("w_kv_cross", w_kv_cross[0], m_w_kv_cross[0], v_w_kv_cross[0], False),
           ("w_o_cross", w_o_cross[0], m_w_o_cross[0], v_w_o_cross[0], True),
           ("w_gate_up", w_gate_up[0], m_w_gate_up[0], v_w_gate_up[0], True),
           ("w_down", w_down[0], m_w_down[0], v_w_down[0], False)]
    shards = [(w.T if tr else w).astype(BF16) for _, w, _, _, tr in big]
    (wt_in,) = _gather_weights(shards[:1])
    (loss, grad_x, reduced, (dg_mix, db_gate, dws, dbs_acc, dg_sgu, dg_cross, dg_mem, dg_ffn, dg_final)) = _local_step(
        xs, tgt, mems, (wt_in,) + (None,) * 8,
        (g_mix, b_gate, w_sgu_spatial[0], b_sgu_spatial[0], g_sgu, g_cross, g_mem, g_ffn, g_final2),
        _Gather(shards[1:7]), _Gather(shards[7:9]), reduce=True)
    full = _share_halves(reduced)

    big_out = {}
    for (name, w, m, v, tr), gsh in zip(big, full):
        gsh = gsh.T if tr else gsh
        delta, nm, nv = _elementwise(_adamw, [w, gsh, m, v], [F32, F32, F32], f"adam_{name}")
        big_out[name] = tuple(t[None] for t in (gsh, delta, nm, nv))

    small = [("g_mix", g_mix, m_g_mix, v_g_mix, dg_mix), ("b_gate", b_gate, m_b_gate, v_b_gate, db_gate),
             ("w_sgu_spatial", w_sgu_spatial, m_w_sgu_spatial, v_w_sgu_spatial, jnp.tril(dws)),
             ("b_sgu_spatial", b_sgu_spatial, m_b_sgu_spatial, v_b_sgu_spatial, jnp.sum(dbs_acc, axis=-1)),
             ("g_sgu", g_sgu, m_g_sgu, v_g_sgu, dg_sgu), ("g_cross", g_cross, m_g_cross, v_g_cross, dg_cross),
             ("g_mem", g_mem, m_g_mem, v_g_mem, dg_mem), ("g_ffn", g_ffn, m_g_ffn, v_g_ffn, dg_ffn),
             ("g_final", g_final, m_g_final, v_g_final, dg_final)]
    as_term = lambda s, t: t.reshape(s[4].shape)
    gs, ds, nms, nvs, loss_all = _adam_small(*[[as_term(s, s[k]) for s in small] for k in (1, 2, 3, 4)], loss)
    small_out = {s[0]: tuple(t[i].reshape(s[1].shape) for t in (gs, ds, nms, nvs)) for i, s in enumerate(small)}
    total_loss = loss_all[0, 0]

    order = ["g_mix", "w_in", "b_gate", "w_sgu_spatial", "b_sgu_spatial", "g_sgu", "w_branch_attn", "w_branch_sgu",
             "w_out", "g_cross", "g_mem", "w_q_cross", "w_kv_cross", "w_o_cross", "g_ffn", "w_gate_up", "w_down",
             "g_final"]
    res = {**big_out, **small_out}
    outs = [total_loss, grad_x.reshape(x.shape)]
    for k in range(4):
        outs += [res[nm][k] for nm in order]
    return tuple(outs)
```

```python
import math

import numpy as np
import jax
import jax.numpy as jnp
from jax import lax
from jax.experimental import pallas as pl
from jax.experimental.pallas import tpu as pltpu

F32, BF16 = jnp.float32, jnp.bfloat16
MESH = pl.DeviceIdType.MESH
ANY = pl.BlockSpec(memory_space=pl.ANY)
RES = pl.BlockSpec(memory_space=pltpu.VMEM)


def _pallas(body, **kw):
    call = pl.pallas_call(body, **kw)
    gs = kw.get("grid_spec")
    specs = kw.get("in_specs") if gs is None else [None] * gs.num_scalar_prefetch + list(gs.in_specs)

    def run(*args):
        if specs is not None:
            args = [a if (s is RES or s is None) else pltpu.with_memory_space_constraint(a, pltpu.HBM)
                    for a, s in zip(args, specs)]
        return call(*args)
    return run


def _whole(arr):
    nd = len(arr.shape)
    return pl.BlockSpec(arr.shape, lambda *_: (0,) * nd)

D = 1024
HEAD = 64
GROUP_W = 256
DIL_GROUPS = ((128, 1), (512, 4), (2048, 16))
BLK = 128
SGU_W = 512
MEM_HEADS, MEM_HD, MEM_W = 4, 128, 512
D_FF = 2816
FF_CHUNK = 256
EPS = 1e-6
NEG = -1e30
LR, B1, B2, AEPS, WD, STEP = 0.001, 0.9, 0.999, 1e-08, 0.01, 10
GELU_K, GELU_C = 0.7978845608028654, 0.044715


def _dot(a, b):
    return jnp.dot(a, b, preferred_element_type=F32)


def _dot_nt(a, b):
    return lax.dot_general(a, b, (((1,), (1,)), ((), ())), preferred_element_type=F32)


def _dot_tn(a, b):
    return lax.dot_general(a, b, (((0,), (0,)), ((), ())), preferred_element_type=F32)


def _row(tm, w):
    return pl.BlockSpec((tm, w), lambda i: (i, 0))


def _acc(shape):
    return pl.BlockSpec(shape, lambda i: (0,) * len(shape))


def _params(sem, mb):
    return pltpu.CompilerParams(dimension_semantics=sem, vmem_limit_bytes=mb << 20)


def _sds(shape, dt):
    return jax.ShapeDtypeStruct(shape, dt)


def _rms(h):
    return lax.rsqrt(jnp.mean(h * h, axis=-1, keepdims=True) + EPS)


def _rms_bwd(dy, h, r, g):
    t = dy * g
    dh = r * t - h * (r * r * r) * jnp.mean(t * h, axis=-1, keepdims=True)
    return dh, dy * h * r


def _gelu(x):
    t = jnp.tanh(GELU_K * (x + GELU_C * x * x * x))
    return 0.5 * x * (1.0 + t), t


def _gelu_grad(x, t):
    return 0.5 * (1.0 + t) + 0.5 * x * (1.0 - t * t) * GELU_K * (1.0 + 3.0 * GELU_C * x * x)


def _alibi_slopes():
    def pow2(n):
        start = 2.0 ** (-8.0 / n)
        return [start ** (i + 1) for i in range(n)]
    n = 12
    c = 2 ** int(math.floor(math.log2(n)))
    s = pow2(c) + pow2(2 * c)[0::2][: n - c]
    return np.array(sorted(s, reverse=True), dtype=np.float32).reshape(3, 4)


def _attn_bias(g):
    win, dil = DIL_GROUPS[g]
    steps = (np.arange(BLK)[:, None] + BLK) - np.arange(2 * BLK)[None, :]
    valid = (steps >= 0) & (steps <= win // dil)
    dist = (np.clip(steps, 0, None) * dil).astype(np.float32)
    b = -_alibi_slopes()[g][:, None, None] * dist[None]
    return np.where(valid[None], b, NEG).astype(np.float32)


def _head_masks():
    lane = lax.broadcasted_iota(jnp.int32, (1, GROUP_W), 1)
    return lane, [(lane >= HEAD * h) & (lane < HEAD * (h + 1)) for h in range(4)]


ATT_NB = 8


def _stack_heads(t, masks):
    z = jnp.zeros_like(t)
    return jnp.concatenate([jnp.where(m, t, z) for m in masks], axis=0)


def _unstack_heads(t, masks):
    out = jnp.zeros((BLK, GROUP_W), t.dtype)
    for h, m in enumerate(masks):
        out = jnp.where(m, t[h * BLK:(h + 1) * BLK], out)
    return out


def _stack_cols(ref, rows):
    return jnp.concatenate([ref[rows, HEAD * h:HEAD * h + 1] for h in range(4)], axis=0)


def _dil_spec(d, tm, w):
    return pl.BlockSpec((d, tm // d, w), lambda i: (0, i, 0))


def _to_dilated(val, s_ref, d, write):
    tm, w = val.shape
    for j in range(w // 128):
        s_ref[j, pl.ds(0, tm), :] = val[:, j * 128:(j + 1) * 128]
    for r in range(d):
        for j in range(w // 128):
            write(r, j, s_ref[j, pl.ds(r, tm // d, stride=d), :])


def _from_dilated(ref, s_ref, d, tm, w):
    if d == 1:
        return ref[0].astype(F32)
    for r in range(d):
        for j in range(w // 128):
            s_ref[j, pl.ds(r, tm // d, stride=d), :] = ref[r, :, j * 128:(j + 1) * 128].astype(F32)
    return jnp.concatenate([s_ref[j, pl.ds(0, tm), :] for j in range(w // 128)], axis=1)


def _fwd_in(x, g_mix, wt_in, gather=None, tm=512):
    S = x.shape[0]
    dils = [d for _, d in DIL_GROUPS]
    n = 0 if gather is None else gather.n
    last = S // tm - 1

    def body(*refs):
        x_ref, g_ref, w_ref = refs[:3]
        a_ref, q0_ref, q1_ref, q2_ref, uv_ref, gl_ref = refs[3 + n:9 + n]
        s_ref = refs[9 + 2 * n]
        comm = (refs[3:3 + n], refs[9 + n:9 + 2 * n], refs[10 + 2 * n:])
        if gather is not None:
            pl.when(pl.program_id(0) == 0)(lambda: gather.start(*comm))
        xv = x_ref[...]
        a = (xv * _rms(xv) * g_ref[...]).astype(BF16)
        a_ref[...] = a
        for g, (d, out) in enumerate(zip(dils, (q0_ref, q1_ref, q2_ref))):
            for part in range(3):
                rows = part * 768 + g * 256
                val = _dot_nt(a, w_ref[rows:rows + 256, :])
                if d == 1:
                    out[0, :, part * 256:(part + 1) * 256] = val.astype(BF16)
                else:
                    def write(r, j, piece, out=out, part=part):
                        out[r, :, part * 256 + j * 128:part * 256 + (j + 1) * 128] = piece.astype(BF16)
                    _to_dilated(val, s_ref, d, write)
        uv_ref[...] = _dot_nt(a, w_ref[2304:3328, :]).astype(BF16)
        gl_ref[...] = _dot_nt(a, w_ref[3328:5376, :]).astype(BF16)
        if gather is not None:
            pl.when(pl.program_id(0) == last)(lambda: gather.finish(*comm))

    outs = _pallas(
        body, grid=(S // tm,), name="fwd_in",
        in_specs=[_row(tm, D), _whole(g_mix), RES] + [ANY] * n,
        out_specs=[_row(tm, D)] + [_dil_spec(d, tm, 768) for d in dils] + [_row(tm, 1024), _row(tm, 2048)] + [ANY] * n,
        out_shape=[_sds((S, D), BF16)] + [_sds((d, S // d, 768), BF16) for d in dils]
        + [_sds((S, 1024), BF16), _sds((S, 2048), BF16)] + ([] if gather is None else gather.out_shape),
        scratch_shapes=[pltpu.VMEM((2, tm, 128), F32)] + ([] if gather is None else gather.scratch),
        compiler_params=_params(("arbitrary",), 60),
    )(x, g_mix, wt_in, *([] if gather is None else gather.halves))
    return outs[:6], ([] if gather is None else gather.full(outs[6:]))


def _attn_fwd(qkv, g):
    d, L, _ = qkv.shape
    nb = L // BLK
    bias = jnp.asarray(_attn_bias(g).reshape(4 * BLK, 2 * BLK))
    NB = min(ATT_NB, nb)
    W = NB * BLK

    def body(q_ref, kc_ref, kp_ref, vc_ref, vp_ref, b_ref, o_ref, l_ref):
        st = pl.program_id(1)
        k_all = jnp.concatenate([kp_ref[...], kc_ref[...]], axis=0)
        v_all = jnp.concatenate([vp_ref[...], vc_ref[...]], axis=0)
        lane, masks = _head_masks()
        for b in range(NB):
            rows = slice(b * BLK, (b + 1) * BLK)
            kk, vv = k_all[b * BLK:(b + 2) * BLK], v_all[b * BLK:(b + 2) * BLK]
            s = _dot_nt(_stack_heads(q_ref[rows, :], masks), kk) * 0.125 + b_ref[...]
            if b == 0:
                s = s + jnp.where((st == 0) & (lane < BLK), NEG, 0.0).astype(F32)
            mx = jnp.max(s, axis=-1, keepdims=True)
            e = jnp.exp(s - mx)
            den = jnp.sum(e, axis=-1, keepdims=True)
            o_ref[rows, :] = _unstack_heads(_dot(e.astype(BF16), vv) / den, masks)
            l_ref[rows, :] = _unstack_heads(mx + jnp.log(den), masks)

    def wide(col):
        return pl.BlockSpec((None, W, GROUP_W), lambda r, s: (r, s, col))

    def before(col):
        return pl.BlockSpec((None, BLK, GROUP_W), lambda r, s: (r, jnp.maximum(s * NB - 1, 0), col))

    return _pallas(
        body, grid=(d, nb // NB), name=f"attn_fwd_g{g}",
        in_specs=[wide(0), wide(1), before(1), wide(2), before(2),
                  pl.BlockSpec((4 * BLK, 2 * BLK), lambda r, s: (0, 0))],
        out_specs=[wide(0), wide(0)],
        out_shape=[_sds((d, L, GROUP_W), F32), _sds((d, L, GROUP_W), F32)],
        compiler_params=_params(("parallel", "parallel"), 32),
    )(qkv, qkv, qkv, qkv, qkv, bias)


def _group_weights(l0, l1, l2):
    m = jnp.maximum(jnp.maximum(l0, l1), l2)
    e0, e1, e2 = jnp.exp(l0 - m), jnp.exp(l1 - m), jnp.exp(l2 - m)
    inv = 1.0 / (e0 + e1 + e2)
    return e0 * inv, e1 * inv, e2 * inv


def _sgu_forward(uvf, gs, wt_ref, bst_ref, mixed_s, tm):
    z, t = _gelu(uvf)
    u, v = z[:, :SGU_W], z[:, SGU_W:]
    rv = _rms(v)
    vnb = (v * rv * gs).astype(BF16)
    for ci in range(tm // 128):
        for g in range(4):
            rs, cs = slice(ci * 128, (ci + 1) * 128), slice(g * 128, (g + 1) * 128)
            mixed_s[rs, cs] = _dot(wt_ref[g], vnb[rs, cs]) + bst_ref[:, g:g + 1]
    return u, v, rv, vnb, t


def _fwd_mid(x, os_, ls_, uv, gl, wt, bst, g_sgu, b_gate, wt_ba, wt_bs, w_out, gather=None, tm=512):
    S = x.shape[0]
    dils = [d for _, d in DIL_GROUPS]
    n = 0 if gather is None else gather.n
    last = S // tm - 1

    def body(*refs):
        (x_ref, o0, o1, o2, l0, l1, l2, uv_ref, gl_ref, wt_ref, bst_ref, gs_ref, bg_ref, wba_ref, wbs_ref,
         wo_ref) = refs[:16]
        ya_ref, ys_ref, ba_ref, bs_ref, mg_ref, h1_ref = refs[16 + n:22 + n]
        mixed_s, il_s = refs[22 + 2 * n:24 + 2 * n]
        comm = (refs[16:16 + n], refs[22 + n:22 + 2 * n], refs[24 + 2 * n:])
        if gather is not None:
            pl.when(pl.program_id(0) == 0)(lambda: gather.start(*comm))
        ls = [_from_dilated(r, il_s, d, tm, GROUP_W) for r, d in zip((l0, l1, l2), dils)]
        alphas = _group_weights(*ls)
        ya = jnp.zeros((tm, GROUP_W), F32)
        for a, r, d in zip(alphas, (o0, o1, o2), dils):
            ya = ya + a * _from_dilated(r, il_s, d, tm, GROUP_W)
        yab = ya.astype(BF16)
        ya_ref[...] = yab
        u, _, _, _, _ = _sgu_forward(uv_ref[...].astype(F32), gs_ref[...], wt_ref, bst_ref, mixed_s, tm)
        ysb = (u * mixed_s[...]).astype(BF16)
        ys_ref[...] = ysb
        gates = jax.nn.sigmoid(gl_ref[...].astype(F32) + bg_ref[...])
        ba = _dot_nt(yab, wba_ref[...])
        bs = _dot_nt(ysb, wbs_ref[...])
        ba_ref[...] = ba.astype(BF16)
        bs_ref[...] = bs.astype(BF16)
        mgb = (gates[:, :D] * ba + gates[:, D:] * bs).astype(BF16)
        mg_ref[...] = mgb
        h1_ref[...] = x_ref[...] + _dot(mgb, wo_ref[...])
        if gather is not None:
            pl.when(pl.program_id(0) == last)(lambda: gather.finish(*comm))

    gw = _row(tm, GROUP_W)
    dil = [_dil_spec(d, tm, GROUP_W) for d in dils]
    outs = _pallas(
        body, grid=(S // tm,), name="fwd_mid",
        in_specs=[_row(tm, D)] + dil + dil + [_row(tm, 1024), _row(tm, 2048)]
        + [_whole(t) for t in (wt, bst, g_sgu, b_gate)] + [RES] * 3 + [ANY] * n,
        out_specs=[gw, _row(tm, SGU_W), _row(tm, D), _row(tm, D), _row(tm, D), _row(tm, D)] + [ANY] * n,
        out_shape=[_sds((S, GROUP_W), BF16), _sds((S, SGU_W), BF16), _sds((S, D), BF16), _sds((S, D), BF16),
                   _sds((S, D), BF16), _sds((S, D), F32)] + ([] if gather is None else gather.out_shape),
        scratch_shapes=[pltpu.VMEM((tm, SGU_W), F32), pltpu.VMEM((2, tm, 128), F32)]
        + ([] if gather is None else gather.scratch),
        compiler_params=_params(("arbitrary",), 56),
    )(x, *os_, *ls_, uv, gl, wt, bst, g_sgu, b_gate, wt_ba, wt_bs, w_out, *([] if gather is None else gather.halves))
    return outs[:6], ([] if gather is None else gather.full(outs[6:]))


def _mem_fwd(mem, g_mem, w_kv):
    def body(m_ref, g_ref, w_ref, mb_ref, kv_ref):
        mv = m_ref[...]
        mb = (mv * _rms(mv) * g_ref[...]).astype(BF16)
        mb_ref[...] = mb
        kv_ref[...] = _dot(mb, w_ref[...]).astype(BF16)

    shapes = [_sds(mem.shape, BF16), _sds((mem.shape[0], 2 * MEM_W), BF16)]
    return _pallas(
        body, name="mem_fwd", grid=(1,), in_specs=[_whole(t) for t in (mem, g_mem, w_kv)],
        out_specs=[_whole(t) for t in shapes], out_shape=shapes, compiler_params=_params(("arbitrary",), 32),
    )(mem, g_mem, w_kv)


def _cross_probs(qh, kh):
    s = _dot_nt(qh, kh) * (MEM_HD ** -0.5)
    e = jnp.exp(s - jnp.max(s, axis=-1, keepdims=True))
    return e / jnp.sum(e, axis=-1, keepdims=True)


def _fwd_cross(h1, g_cross, w_q, kv, wt_o, tm=512):
    S = h1.shape[0]

    def body(h_ref, g_ref, wq_ref, kv_ref, wo_ref, c_ref, qc_ref, oc_ref, h2_ref):
        hv = h_ref[...]
        cb = (hv * _rms(hv) * g_ref[...]).astype(BF16)
        c_ref[...] = cb
        qcb = _dot(cb, wq_ref[...]).astype(BF16)
        qc_ref[...] = qcb
        for h in range(MEM_HEADS):
            cs = slice(h * MEM_HD, (h + 1) * MEM_HD)
            p = _cross_probs(qcb[:, cs], kv_ref[:, cs])
            oc_ref[:, cs] = _dot(p.astype(BF16), kv_ref[:, MEM_W + h * MEM_HD:MEM_W + (h + 1) * MEM_HD]).astype(BF16)
        h2_ref[...] = hv + _dot_nt(oc_ref[...], wo_ref[...])

    return _pallas(
        body, grid=(S // tm,), name="fwd_cross",
        in_specs=[_row(tm, D), _whole(g_cross), RES, _whole(kv), RES],
        out_specs=[_row(tm, D), _row(tm, MEM_W), _row(tm, MEM_W), _row(tm, D)],
        out_shape=[_sds((S, D), BF16), _sds((S, MEM_W), BF16), _sds((S, MEM_W), BF16), _sds((S, D), F32)],
        compiler_params=_params(("parallel",), 40),
    )(h1, g_cross, w_q, kv, wt_o)


def _ffn_fwd_bwd(h2, target, g_ffn, g_final, wt_gu, w_down, tm=256):
    S = h2.shape[0]
    nch = D_FF // FF_CHUNK

    def body(h_ref, t_ref, gf_ref, gz_ref, wgu_ref, wd_ref,
             f_ref, act_ref, dgu_ref, dh3b_ref, dh2_ref, dh2b_ref, dgf_ref, dgz_ref, loss_ref, gu_s, dact_s):
        i = pl.program_id(0)

        @pl.when(i == 0)
        def _():
            dgf_ref[...] = jnp.zeros_like(dgf_ref)
            dgz_ref[...] = jnp.zeros_like(dgz_ref)
            loss_ref[...] = jnp.zeros_like(loss_ref)

        hv = h_ref[...]
        r2 = _rms(hv)
        gf = gf_ref[...]
        fb = (hv * r2 * gf).astype(BF16)
        f_ref[...] = fb
        gu_s[...] = _dot_nt(fb, wgu_ref[...])
        for c in range(nch):
            cs = slice(c * FF_CHUNK, (c + 1) * FF_CHUNK)
            us = slice(D_FF + c * FF_CHUNK, D_FF + (c + 1) * FF_CHUNK)
            gt = gu_s[:, cs]
            act_ref[:, cs] = (gt * jax.nn.sigmoid(gt) * gu_s[:, us]).astype(BF16)
        h3 = hv + _dot(act_ref[...], wd_ref[...])
        r3 = _rms(h3)
        gz = gz_ref[...]
        diff = h3 * r3 * gz - t_ref[...]
        dy = diff * (1.0 / D)
        dh3, dgz_rows = _rms_bwd(dy, h3, r3, gz)
        dh3b = dh3.astype(BF16)
        dh3b_ref[...] = dh3b
        dact_s[...] = _dot_nt(dh3b, wd_ref[...])
        for c in range(nch):
            cs = slice(c * FF_CHUNK, (c + 1) * FF_CHUNK)
            us = slice(D_FF + c * FF_CHUNK, D_FF + (c + 1) * FF_CHUNK)
            dact, gt, up = dact_s[:, cs], gu_s[:, cs], gu_s[:, us]
            sg = jax.nn.sigmoid(gt)
            dgu_ref[:, cs] = (dact * up * (sg * (1.0 + gt * (1.0 - sg)))).astype(BF16)
            dgu_ref[:, us] = (dact * (gt * sg)).astype(BF16)
        df = _dot(dgu_ref[...], wgu_ref[...])
        dhn, dgf_rows = _rms_bwd(df, hv, r2, gf)
        dh2 = dh3 + dhn
        dh2_ref[...] = dh2
        dh2b_ref[...] = dh2.astype(BF16)
        dgf_ref[...] += jnp.sum(dgf_rows, axis=0, keepdims=True)
        dgz_ref[...] += jnp.sum(dgz_rows, axis=0, keepdims=True)
        loss_ref[...] += jnp.sum(jnp.sum(diff * diff, axis=0, keepdims=True), axis=1, keepdims=True) * (0.5 / D)

    return _pallas(
        body, grid=(S // tm,), name="ffn_fwd_bwd",
        in_specs=[_row(tm, D), _row(tm, D), _whole(g_ffn), _whole(g_final), RES, RES],
        out_specs=[_row(tm, D), _row(tm, D_FF), _row(tm, 2 * D_FF), _row(tm, D), _row(tm, D), _row(tm, D),
                   _acc((1, D)), _acc((1, D)), _acc((1, 128))],
        out_shape=[_sds((S, D), BF16), _sds((S, D_FF), BF16), _sds((S, 2 * D_FF), BF16), _sds((S, D), BF16),
                   _sds((S, D), F32), _sds((S, D), BF16), _sds((1, D), F32), _sds((1, D), F32), _sds((1, 128), F32)],
        scratch_shapes=[pltpu.VMEM((tm, 2 * D_FF), F32), pltpu.VMEM((tm, D_FF), F32)],
        compiler_params=_params(("arbitrary",), 60),
    )(h2, target, g_ffn, g_final, wt_gu, w_down)


def _bwd_cross(dh2, h1, qc, g_cross, w_q, kv, wt_o, comm=None, tm=512):
    S = h1.shape[0]
    n = 0 if comm is None else comm.n
    last = S // tm - 1

    def body(*refs):
        d_ref, h_ref, qc_ref, g_ref, wq_ref, kv_ref, wo_ref = refs[:7]
        dqc_ref, dh1_ref, dh1b_ref, dkv_ref, dg_ref = refs[7 + n:12 + n]
        cargs = (refs[7:7 + n], refs[12 + n:12 + 2 * n], refs[12 + 2 * n:])
        i = pl.program_id(0)

        @pl.when(i == 0)
        def _():
            dkv_ref[...] = jnp.zeros_like(dkv_ref)
            dg_ref[...] = jnp.zeros_like(dg_ref)
            if comm is not None:
                comm.start(*cargs)

        dh2 = d_ref[...]
        doc = _dot(dh2.astype(BF16), wo_ref[...])
        qcb = qc_ref[...]
        for h in range(MEM_HEADS):
            cs = slice(h * MEM_HD, (h + 1) * MEM_HD)
            vs = slice(MEM_W + h * MEM_HD, MEM_W + (h + 1) * MEM_HD)
            qh, kh, vh = qcb[:, cs], kv_ref[:, cs], kv_ref[:, vs]
            p = _cross_probs(qh, kh)
            dohb = doc[:, cs].astype(BF16)
            dp = _dot_nt(dohb, vh)
            dsb = (p * (dp - jnp.sum(dp * p, axis=-1, keepdims=True)) * (MEM_HD ** -0.5)).astype(BF16)
            dqc_ref[:, cs] = _dot(dsb, kh).astype(BF16)
            dkv_ref[:, cs] += _dot_tn(dsb, qh)
            dkv_ref[:, vs] += _dot_tn(p.astype(BF16), dohb)
        dc = _dot_nt(dqc_ref[...], wq_ref[...])
        hv = h_ref[...]
        dhn, dg_rows = _rms_bwd(dc, hv, _rms(hv), g_ref[...])
        dh1 = dh2 + dhn
        dh1_ref[...] = dh1
        dh1b_ref[...] = dh1.astype(BF16)
        dg_ref[...] += jnp.sum(dg_rows, axis=0, keepdims=True)
        if comm is not None:
            pl.when(i == last)(lambda: comm.finish(*cargs))

    outs = _pallas(
        body, grid=(S // tm,), name="bwd_cross",
        in_specs=[_row(tm, D), _row(tm, D), _row(tm, MEM_W), _whole(g_cross), RES, _whole(kv), RES] + [ANY] * n,
        out_specs=[_row(tm, MEM_W), _row(tm, D), _row(tm, D), _acc((256, 2 * MEM_W)), _acc((1, D))] + [ANY] * n,
        out_shape=[_sds((S, MEM_W), BF16), _sds((S, D), F32), _sds((S, D), BF16), _sds((256, 2 * MEM_W), F32),
                   _sds((1, D), F32)] + ([] if comm is None else comm.out_shape),
        scratch_shapes=[] if comm is None else comm.scratch,
        compiler_params=_params(("arbitrary",), 48),
    )(dh2, h1, qc, g_cross, w_q, kv, wt_o, *([] if comm is None else comm.ins))
    return outs[:5], outs[5:]


def _mem_bwd(dkv, mem, mb, g_mem, w_kv):
    def body(dkv_ref, m_ref, mb_ref, g_ref, w_ref, dw_ref, dwb_ref, dg_ref):
        dkvb = dkv_ref[...].astype(BF16)
        dw = _dot_tn(mb_ref[...], dkvb)
        dw_ref[...] = dw
        dwb_ref[...] = dw.astype(BF16)
        dm = _dot_nt(dkvb, w_ref[...])
        mv = m_ref[...]
        dg_ref[...] = jnp.sum(dm * mv * _rms(mv), axis=0, keepdims=True)

    shapes = [_sds((D, 2 * MEM_W), F32), _sds((D, 2 * MEM_W), BF16), _sds((1, D), F32)]
    return _pallas(
        body, name="mem_bwd", grid=(1,), in_specs=[_whole(t) for t in (dkv, mem, mb, g_mem, w_kv)],
        out_specs=[_whole(t) for t in shapes], out_shape=shapes, compiler_params=_params(("arbitrary",), 40),
    )(dkv, mem, mb, g_mem, w_kv)


def _bwd_mid(dh1, gl, ba, bs, uv, ls_, ya, wt, bst, g_sgu, b_gate, wt_ba, wt_bs, w_out, tm=512):
    S = dh1.shape[0]
    dils = [d for _, d in DIL_GROUPS]

    def body(d_ref, gl_ref, ba_ref, bs_ref, uv_ref, l0, l1, l2, ya_ref,
             wt_ref, bst_ref, gs_ref, bg_ref, wba_ref, wbs_ref, wo_ref,
             dba_ref, dbs_ref, dgl_ref, duv_ref, do0, do1, do2, c0, c1, c2,
             dbg_ref, dgs_ref, dws_ref, dbsa_ref, mixed_s, dvn_s, il_s):
        i = pl.program_id(0)

        @pl.when(i == 0)
        def _():
            for r in (dbg_ref, dgs_ref, dws_ref, dbsa_ref):
                r[...] = jnp.zeros_like(r)

        dm = _dot_nt(d_ref[...].astype(BF16), wo_ref[...])
        gates = jax.nn.sigmoid(gl_ref[...].astype(F32) + bg_ref[...])
        g0, g1 = gates[:, :D], gates[:, D:]
        dbab = (dm * g0).astype(BF16)
        dbsb = (dm * g1).astype(BF16)
        dba_ref[...] = dbab
        dbs_ref[...] = dbsb
        dg0 = dm * ba_ref[...].astype(F32) * g0 * (1.0 - g0)
        dg1 = dm * bs_ref[...].astype(F32) * g1 * (1.0 - g1)
        dgl_ref[:, :D] = dg0.astype(BF16)
        dgl_ref[:, D:] = dg1.astype(BF16)
        dbg_ref[:, :D] += jnp.sum(dg0, axis=0, keepdims=True)
        dbg_ref[:, D:] += jnp.sum(dg1, axis=0, keepdims=True)
        dya = _dot(dbab, wba_ref[...])
        dys = _dot(dbsb, wbs_ref[...])

        uvf = uv_ref[...].astype(F32)
        gs = gs_ref[...]
        u, v, rv, vnb, t = _sgu_forward(uvf, gs, wt_ref, bst_ref, mixed_s, tm)
        du = dys * mixed_s[...]
        dmixed = dys * u
        for ci in range(tm // 128):
            for g in range(4):
                rs, cs = slice(ci * 128, (ci + 1) * 128), slice(g * 128, (g + 1) * 128)
                dmx = dmixed[rs, cs]
                dmxb = dmx.astype(BF16)
                dvn_s[rs, cs] = _dot_tn(wt_ref[g], dmxb)
                dws_ref[g] += _dot_nt(dmxb, vnb[rs, cs])
                dbsa_ref[g] += dmx
        dv, dgs_rows = _rms_bwd(dvn_s[...], v, rv, gs)
        dgs_ref[...] += jnp.sum(dgs_rows, axis=0, keepdims=True)
        gg = _gelu_grad(uvf, t)
        duv_ref[:, :SGU_W] = (du * gg[:, :SGU_W]).astype(BF16)
        duv_ref[:, SGU_W:] = (dv * gg[:, SGU_W:]).astype(BF16)

        alphas = _group_weights(*[_from_dilated(r, il_s, d, tm, GROUP_W) for r, d in zip((l0, l1, l2), dils)])
        prod = dya * ya_ref[...].astype(F32)
        _, masks = _head_masks()
        hs = jnp.zeros_like(prod)
        for h in range(4):
            sh = jnp.sum(jnp.where(masks[h], prod, 0.0), axis=-1, keepdims=True)
            hs = jnp.where(masks[h], sh, hs)
        for a, d, do_ref, c_ref in zip(alphas, dils, (do0, do1, do2), (c0, c1, c2)):
            for val, out in ((a * dya, do_ref), (a * hs, c_ref)):
                if d == 1:
                    out[0] = val.astype(out.dtype)
                else:
                    def write(r, j, piece, out=out):
                        out[r, :, j * 128:(j + 1) * 128] = piece.astype(out.dtype)
                    _to_dilated(val, il_s, d, write)

    gw = _row(tm, GROUP_W)
    dil = [_dil_spec(d, tm, GROUP_W) for d in dils]
    return _pallas(
        body, grid=(S // tm,), name="bwd_mid",
        in_specs=[_row(tm, D), _row(tm, 2048), _row(tm, D), _row(tm, D), _row(tm, 1024)] + dil + [gw]
        + [_whole(t) for t in (wt, bst, g_sgu, b_gate)] + [RES] * 3,
        out_specs=[_row(tm, D), _row(tm, D), _row(tm, 2048), _row(tm, 1024)] + dil + dil
        + [_acc((1, 2048)), _acc((1, SGU_W)), _acc((4, 128, 128)), _acc((4, 128, 128))],
        out_shape=[_sds((S, D), BF16), _sds((S, D), BF16), _sds((S, 2048), BF16), _sds((S, 1024), BF16)]
        + [_sds((d, S // d, GROUP_W), BF16) for d in dils] + [_sds((d, S // d, GROUP_W), F32) for d in dils]
        + [_sds((1, 2048), F32), _sds((1, SGU_W), F32), _sds((4, 128, 128), F32), _sds((4, 128, 128), F32)],
        scratch_shapes=[pltpu.VMEM((tm, SGU_W), F32), pltpu.VMEM((tm, SGU_W), F32), pltpu.VMEM((2, tm, 128), F32)],
        compiler_params=_params(("arbitrary",), 60),
    )(dh1, gl, ba, bs, uv, *ls_, ya, wt, bst, g_sgu, b_gate, wt_ba, wt_bs, w_out)


def _attn_bwd(qkv, do, lse, corr, g):
    d, L, _ = qkv.shape
    nb = L // BLK
    NB = min(ATT_NB, nb)
    W = NB * BLK
    nsteps = nb // NB
    bias = jnp.asarray(_attn_bias(g).reshape(4 * BLK, 2 * BLK))

    def body(q_ref, kc_ref, kp_ref, vc_ref, vp_ref, do_ref, l_ref, c_ref, qn_ref, don_ref, ln_ref, cn_ref, b_ref,
             out_ref, dk_s, dv_s):
        st = pl.program_id(1)
        k_all = jnp.concatenate([kp_ref[...], kc_ref[...]], axis=0)
        v_all = jnp.concatenate([vp_ref[...], vc_ref[...]], axis=0)
        lane, masks = _head_masks()
        dk_s[...] = jnp.zeros_like(dk_s)
        dv_s[...] = jnp.zeros_like(dv_s)

        def block_terms(qs, dos, kk, vv, bias_v, lse_c, corr_c):
            s = _dot_nt(qs, kk) * 0.125 + bias_v
            p = jnp.exp(s - lse_c)
            dsb = (p * (_dot_nt(dos, vv) - corr_c) * 0.125).astype(BF16)
            return dsb, p.astype(BF16)

        for b in range(NB):
            rows = slice(b * BLK, (b + 1) * BLK)
            keys = slice(b * BLK, (b + 2) * BLK)
            kk, vv = k_all[keys], v_all[keys]
            qs, dos = _stack_heads(q_ref[rows, :], masks), _stack_heads(do_ref[rows, :], masks)
            bias_v = b_ref[...]
            if b == 0:
                bias_v = bias_v + jnp.where((st == 0) & (lane < BLK), NEG, 0.0).astype(F32)
            dsb, pb = block_terms(qs, dos, kk, vv, bias_v, _stack_cols(l_ref, rows), _stack_cols(c_ref, rows))
            out_ref[rows, 0:GROUP_W] = _unstack_heads(_dot(dsb, kk), masks).astype(BF16)
            dk_s[keys, :] += _dot_tn(dsb, qs)
            dv_s[keys, :] += _dot_tn(pb, dos)

        @pl.when(st < nsteps - 1)
        def _():
            last = slice(NB * BLK, (NB + 1) * BLK)
            qs, dos = _stack_heads(qn_ref[...], masks), _stack_heads(don_ref[...], masks)
            every = slice(None)
            dsb, pb = block_terms(qs, dos, k_all[last], v_all[last], b_ref[:, :BLK],
                                  _stack_cols(ln_ref, every), _stack_cols(cn_ref, every))
            dk_s[last, :] += _dot_tn(dsb, qs)
            dv_s[last, :] += _dot_tn(pb, dos)

        out_ref[:, GROUP_W:2 * GROUP_W] = dk_s[BLK:, :].astype(BF16)
        out_ref[:, 2 * GROUP_W:] = dv_s[BLK:, :].astype(BF16)

    def wide(col, w=GROUP_W):
        return pl.BlockSpec((None, W, w), lambda r, s: (r, s, col))

    def before(col):
        return pl.BlockSpec((None, BLK, GROUP_W), lambda r, s: (r, jnp.maximum(s * NB - 1, 0), col))

    def after(col):
        return pl.BlockSpec((None, BLK, GROUP_W), lambda r, s: (r, jnp.minimum((s + 1) * NB, nb - 1), col))

    return _pallas(
        body, grid=(d, nsteps), name=f"attn_bwd_g{g}",
        in_specs=[wide(0), wide(1), before(1), wide(2), before(2), wide(0), wide(0), wide(0),
                  after(0), after(0), after(0), after(0), pl.BlockSpec((4 * BLK, 2 * BLK), lambda r, s: (0, 0))],
        out_specs=wide(0, 768),
        out_shape=_sds((d, L, 768), BF16),
        scratch_shapes=[pltpu.VMEM(((NB + 1) * BLK, GROUP_W), F32), pltpu.VMEM(((NB + 1) * BLK, GROUP_W), F32)],
        compiler_params=_params(("parallel", "parallel"), 32),
    )(qkv, qkv, qkv, qkv, qkv, do, lse, corr, qkv, do, lse, corr, bias)


def _bwd_in(dqkvs, duv, dgl, dh1, x, g_mix, wt_in, tm=512):
    S = x.shape[0]
    dils = [d for _, d in DIL_GROUPS]

    def body(q0_ref, q1_ref, q2_ref, duv_ref, dgl_ref, d_ref, x_ref, g_ref, w_ref, dx_ref, dp_ref, dg_ref, il_s):
        i = pl.program_id(0)

        @pl.when(i == 0)
        def _():
            dg_ref[...] = jnp.zeros_like(dg_ref)

        for g, (d, ref) in enumerate(zip(dils, (q0_ref, q1_ref, q2_ref))):
            nat = _from_dilated(ref, il_s, d, tm, 768).astype(BF16)
            for part in range(3):
                col = part * 768 + g * 256
                dp_ref[:, col:col + 256] = nat[:, part * 256:(part + 1) * 256]
        dp_ref[:, 2304:3328] = duv_ref[...]
        dp_ref[:, 3328:5376] = dgl_ref[...]
        da = _dot(dp_ref[...], w_ref[...])
        xv = x_ref[...]
        dxn, dg_rows = _rms_bwd(da, xv, _rms(xv), g_ref[...])
        dx_ref[...] = d_ref[...] + dxn
        dg_ref[...] += jnp.sum(dg_rows, axis=0, keepdims=True)

    return _pallas(
        body, grid=(S // tm,), name="bwd_in",
        in_specs=[_dil_spec(d, tm, 768) for d in dils] + [_row(tm, 1024), _row(tm, 2048), _row(tm, D), _row(tm, D),
                                                          _whole(g_mix), RES],
        out_specs=[_row(tm, D), _row(tm, 5376), _acc((1, D))],
        out_shape=[_sds((S, D), F32), _sds((S, 5376), BF16), _sds((1, D), F32)],
        scratch_shapes=[pltpu.VMEM((6, tm, 128), F32)],
        compiler_params=_params(("arbitrary",), 60),
    )(*dqkvs, duv, dgl, dh1, x, g_mix, wt_in)


def _tn_matmul(a, b, name, tk, ts=2048, comm=None):
    S, K = a.shape
    N = b.shape[1]
    n = 0 if comm is None else comm.n
    nk, ns = K // tk, S // ts

    def body(*refs):
        a_ref, b_ref, o_ref, ob_ref = refs[0], refs[1], refs[2 + n], refs[3 + n]
        cargs = (refs[2:2 + n], refs[4 + n:4 + 2 * n], refs[4 + 2 * n:])
        k, s = pl.program_id(0), pl.program_id(1)
        if comm is not None:
            pl.when((k == 0) & (s == 0))(lambda: comm.start(*cargs))

        @pl.when(s == 0)
        def _():
            o_ref[...] = jnp.zeros_like(o_ref)

        o_ref[...] += _dot_tn(a_ref[...], b_ref[...])

        @pl.when(s == ns - 1)
        def _():
            ob_ref[...] = o_ref[...].astype(BF16)

        if comm is not None:
            pl.when((k == nk - 1) & (s == ns - 1))(lambda: comm.finish(*cargs))

    tile = pl.BlockSpec((tk, N), lambda k, s: (k, 0))
    outs = _pallas(
        body, grid=(nk, ns), name=name,
        in_specs=[pl.BlockSpec((ts, tk), lambda k, s: (s, k)), pl.BlockSpec((ts, N), lambda k, s: (s, 0))] + [ANY] * n,
        out_specs=[tile, tile] + [ANY] * n,
        out_shape=[_sds((K, N), F32), _sds((K, N), BF16)] + ([] if comm is None else comm.out_shape),
        scratch_shapes=[] if comm is None else comm.scratch,
        compiler_params=_params(("arbitrary", "arbitrary"), 56),
    )(a, b, *([] if comm is None else comm.ins))
    pair = (outs[0], outs[1])
    return pair if comm is None else (pair, outs[2:])


def _chip_peers(x, y):
    return [(1 - x, y), (x, 1 - y), (1 - x, 1 - y)]


STAGE_BYTES = 2 << 20


def _chunk_plan(shapes, itemsize):
    plan = []
    for i, (rows, w) in enumerate(shapes):
        ch = max(16, min(rows, (STAGE_BYTES // (w * itemsize)) // 16 * 16))
        while rows % ch:
            ch -= 16
        plan += [(i, r0, ch) for r0 in range(0, rows, ch)]
    return plan


def _remote(src, dst, ssem, rsem, dev):
    return pltpu.make_async_remote_copy(src_ref=src, dst_ref=dst, send_sem=ssem, recv_sem=rsem, device_id=dev,
                                        device_id_type=MESH)


class _Gather:
    def __init__(self, shards):
        self.n = len(shards)
        self.shards = shards
        self.halves = [s.reshape(2, s.shape[0] // 2, s.shape[1]) for s in shards]
        self.plan = _chunk_plan([h.shape[1:] for h in self.halves], 2)
        self.out_shape = [_sds((4,) + h.shape, BF16) for h in self.halves]
        n = self.n
        self.scratch = [pltpu.SemaphoreType.DMA((6 * n,)), pltpu.SemaphoreType.DMA((6 * n,)),
                        pltpu.SemaphoreType.DMA((2,)), pltpu.SemaphoreType.DMA((2,)),
                        pltpu.VMEM((2, max(p[2] for p in self.plan), max(h.shape[2] for h in self.halves)), BF16)]

    def full(self, outs):
        return [o.reshape(4 * s.shape[0], s.shape[1]) for o, s in zip(outs, self.shards)]

    def _sends(self, ins, outs, ssem, rsem):
        x, y, c = lax.axis_index("x"), lax.axis_index("y"), lax.axis_index("c")
        me = 2 * x + y
        return [_remote(ins[i].at[c], outs[i].at[me, c], ssem.at[6 * i + k], rsem.at[6 * i + k], (px, py, c))
                for i in range(self.n) for k, (px, py) in enumerate(_chip_peers(x, y))]

    def start(self, ins, outs, scratch):
        ssem, rsem, lsem, osem, buf = scratch
        me = 2 * lax.axis_index("x") + lax.axis_index("y")
        for cp in self._sends(ins, outs, ssem, rsem):
            cp.start()
        pending = {}
        for i, r0, ch in self.plan:
            for h in range(2):
                if h in pending:
                    pending[h].wait()
                stage = buf.at[h, pl.ds(0, ch), pl.ds(0, self.halves[i].shape[2])]
                ld = pltpu.make_async_copy(ins[i].at[h, pl.ds(r0, ch)], stage, lsem.at[h])
                ld.start()
                ld.wait()
                st = pltpu.make_async_copy(stage, outs[i].at[me, h, pl.ds(r0, ch)], osem.at[h])
                st.start()
                pending[h] = st
        for st in pending.values():
            st.wait()

    def finish(self, ins, outs, scratch):
        ssem, rsem = scratch[:2]
        x, y, c = lax.axis_index("x"), lax.axis_index("y"), lax.axis_index("c")
        chips = _chip_peers(x, y)
        sib = (x, y, 1 - c)
        forwards = []
        for i in range(self.n):
            for k, (px, py) in enumerate(chips):
                landed = outs[i].at[2 * px + py, c]
                _remote(landed, landed, ssem.at[6 * i + k], rsem.at[6 * i + k], (px, py, c)).wait_recv()
                cp = _remote(landed, landed, ssem.at[6 * i + 3 + k], rsem.at[6 * i + 3 + k], sib)
                cp.start()
                forwards.append(cp)
        for i in range(self.n):
            for k, (px, py) in enumerate(chips):
                passed = outs[i].at[2 * px + py, 1 - c]
                _remote(passed, passed, ssem.at[6 * i + 3 + k], rsem.at[6 * i + 3 + k], sib).wait_recv()
        for cp in self._sends(ins, outs, ssem, rsem) + forwards:
            cp.wait_send()


def _gather_weights(shards):
    gt = _Gather(shards)
    n = gt.n

    def body(*refs):
        ins, outs, scratch = refs[:n], refs[n:2 * n], refs[2 * n:]
        gt.start(ins, outs, scratch)
        gt.finish(ins, outs, scratch)

    outs = _pallas(
        body, name="gather_weights", in_specs=[ANY] * n, out_specs=[ANY] * n, out_shape=gt.out_shape,
        scratch_shapes=gt.scratch, compiler_params=pltpu.CompilerParams(vmem_limit_bytes=32 << 20),
    )(*gt.halves)
    return gt.full(outs)


def _swap_halves(grads):
    n = len(grads)
    view = lambda g: g.reshape(4, 2, g.shape[0] // 8, g.shape[1])
    g4f = [view(g) for g, _ in grads]
    g4 = [view(gb) for _, gb in grads]

    def body(*refs):
        ins, got = refs[:n], refs[n:2 * n]
        ssem, rsem = refs[2 * n:]
        x, y, c = lax.axis_index("x"), lax.axis_index("y"), lax.axis_index("c")
        sib = (x, y, 1 - c)
        cps = []
        for i in range(n):
            rc = _remote(ins[i].at[:, 1 - c], got[i], ssem.at[i], rsem.at[i], sib)
            rc.start()
            cps.append(rc)
        for cp in cps:
            cp.wait()

    half = [_sds((4, g.shape[2], g.shape[3]), BF16) for g in g4]
    got = _pallas(
        body, name="swap_halves", in_specs=[ANY] * n, out_specs=[ANY] * n, out_shape=half,
        scratch_shapes=[pltpu.SemaphoreType.DMA((n,)), pltpu.SemaphoreType.DMA((n,))],
    )(*g4)
    return g4f, got


def _chip_sum(g4, got, name):
    _, _, R, W = g4.shape
    tr = _tile(R, max(16, min(512, (1 << 18) // W // 16 * 16)))
    c = lax.axis_index("c").astype(jnp.int32).reshape(1)

    def body(c_ref, a_ref, b_ref, s_ref, sb_ref):
        s = a_ref[...] + b_ref[...].astype(F32)
        s_ref[...] = s
        sb_ref[...] = s.astype(BF16)

    plain = pl.BlockSpec((None, tr, W), lambda j, t, c_ref: (j, t, 0))
    return _pallas(
        body, name=name,
        grid_spec=pltpu.PrefetchScalarGridSpec(
            num_scalar_prefetch=1, grid=(4, R // tr),
            in_specs=[pl.BlockSpec((None, None, tr, W), lambda j, t, c_ref: (j, c_ref[0], t, 0)), plain],
            out_specs=[plain, plain]),
        out_shape=[_sds((4, R, W), F32), _sds((4, R, W), BF16)],
        compiler_params=_params(("parallel", "parallel"), 32),
    )(c, g4, got)


class _Scatter:
    def __init__(self, sums_b):
        self.n = len(sums_b)
        self.ins = list(sums_b)
        self.out_shape = [_sds((3,) + s.shape[1:], BF16) for s in sums_b]
        self.scratch = [pltpu.SemaphoreType.DMA((3 * self.n,)), pltpu.SemaphoreType.DMA((3 * self.n,))]

    def _copies(self, ins, outs, scratch):
        ssem, rsem = scratch
        x, y, c = lax.axis_index("x"), lax.axis_index("y"), lax.axis_index("c")
        return [_remote(ins[i].at[2 * px + py], outs[i].at[k], ssem.at[3 * i + k], rsem.at[3 * i + k], (px, py, c))
                for i in range(self.n) for k, (px, py) in enumerate(_chip_peers(x, y))]

    def start(self, ins, outs, scratch):
        for cp in self._copies(ins, outs, scratch):
            cp.start()

    def finish(self, ins, outs, scratch):
        for cp in self._copies(ins, outs, scratch):
            cp.wait()


class _Reduce:
    def __init__(self, grads, names):
        self.names = names
        g4, got = _swap_halves(grads)
        self.sums, sums_b = [], []
        for nm, g, t in zip(names, g4, got):
            s_, sb_ = _chip_sum(g, t, f"chip_sum_{nm}")
            self.sums.append(s_)
            sums_b.append(sb_)
        self.scatter = _Scatter(sums_b)

    def collect(self, parts):
        return [_mesh_sum(s, p, f"mesh_sum_{nm}") for nm, s, p in zip(self.names, self.sums, parts)]


def _mesh_sum(sums, parts, name):
    _, R, W = sums.shape
    tr = _tile(R, max(16, min(512, (1 << 18) // W // 16 * 16)))
    me = (2 * lax.axis_index("x") + lax.axis_index("y")).astype(jnp.int32).reshape(1)

    def body(me_ref, m_ref, p_ref, o_ref):
        o_ref[...] = m_ref[...] + p_ref[0].astype(F32) + p_ref[1].astype(F32) + p_ref[2].astype(F32)

    return _pallas(
        body, name=name,
        grid_spec=pltpu.PrefetchScalarGridSpec(
            num_scalar_prefetch=1, grid=(R // tr,),
            in_specs=[pl.BlockSpec((None, tr, W), lambda i, me_ref: (me_ref[0], i, 0)),
                      pl.BlockSpec((3, tr, W), lambda i, me_ref: (0, i, 0))],
            out_specs=pl.BlockSpec((tr, W), lambda i, me_ref: (i, 0))),
        out_shape=_sds((R, W), F32), compiler_params=_params(("parallel",), 32),
    )(me, sums, parts)


def _share_halves(reduced):
    n = len(reduced)
    plan = _chunk_plan([r.shape for r in reduced], 4)
    max_rows = max(p[2] for p in plan)
    max_w = max(r.shape[1] for r in reduced)

    def body(*refs):
        ins, outs = refs[:n], refs[n:2 * n]
        ssem, rsem, lsem, osem, buf = refs[2 * n:]
        x, y, c = lax.axis_index("x"), lax.axis_index("y"), lax.axis_index("c")
        sib = (x, y, 1 - c)
        pending = {}
        for k, (i, r0, ch) in enumerate(plan):
            slot = k % 2
            if slot in pending:
                rc, lc = pending[slot]
                rc.wait_send()
                lc.wait()
            stage = buf.at[slot, pl.ds(0, ch), pl.ds(0, reduced[i].shape[1])]
            ld = pltpu.make_async_copy(ins[i].at[pl.ds(r0, ch)], stage, lsem.at[slot])
            ld.start()
            ld.wait()
            place = outs[i].at[c, pl.ds(r0, ch)]
            rc = _remote(stage, place, ssem.at[slot], rsem.at[i], sib)
            lc = pltpu.make_async_copy(stage, place, osem.at[slot])
            rc.start()
            lc.start()
            pending[slot] = (rc, lc)
        for rc, lc in pending.values():
            rc.wait_send()
            lc.wait()
        for i in range(n):
            theirs = outs[i].at[1 - c]
            _remote(theirs, theirs, ssem.at[0], rsem.at[i], sib).wait_recv()

    outs = _pallas(
        body, name="share_halves", in_specs=[ANY] * n, out_specs=[ANY] * n,
        out_shape=[_sds((2,) + r.shape, F32) for r in reduced],
        scratch_shapes=[pltpu.SemaphoreType.DMA((2,)), pltpu.SemaphoreType.DMA((n,)), pltpu.SemaphoreType.DMA((2,)),
                        pltpu.SemaphoreType.DMA((2,)), pltpu.VMEM((2, max_rows, max_w), F32)],
        compiler_params=pltpu.CompilerParams(vmem_limit_bytes=32 << 20),
    )(*reduced)
    return [o.reshape(2 * r.shape[0], r.shape[1]) for o, r in zip(outs, reduced)]


def _tile(rows, cap=256):
    t = min(rows, cap) // 16 * 16
    while rows % t:
        t -= 16
    return t


def _elementwise(fn, ins, out_dtypes, name):
    R, W = ins[0].shape
    tr = _tile(R, max(8, min(512, (1 << 18) // W // 8 * 8)))

    def body(*refs):
        outs = fn(*[r[...] for r in refs[:len(ins)]])
        for o_ref, o in zip(refs[len(ins):], outs):
            o_ref[...] = o.astype(o_ref.dtype)

    return _pallas(
        body, grid=(R // tr,), name=name, in_specs=[_row(tr, W)] * len(ins), out_specs=[_row(tr, W)] * len(out_dtypes),
        out_shape=[_sds((R, W), dt) for dt in out_dtypes],
        compiler_params=_params(("parallel",), 48),
    )(*ins)


def _adamw(w, g, m, v):
    m = B1 * m + (1.0 - B1) * g
    v = B2 * v + (1.0 - B2) * (g * g)
    m_hat = m / (1.0 - B1 ** STEP)
    v_hat = v / (1.0 - B2 ** STEP)
    return -LR * (m_hat / (jnp.sqrt(v_hat) + AEPS) + WD * w), m, v


def _exchange_small(sent, comm):
    ns = len(sent)
    n = comm.n

    def body(*refs):
        p_refs, o_refs = refs[:ns], refs[ns + n:2 * ns + n]
        all_s = refs[2 * ns + 2 * n:3 * ns + 2 * n]
        ssem, rsem = refs[3 * ns + 2 * n:3 * ns + 2 * n + 2]
        cargs = (refs[ns:ns + n], refs[2 * ns + n:2 * ns + 2 * n], refs[3 * ns + 2 * n + 2:])
        comm.start(*cargs)
        x, y, c = lax.axis_index("x"), lax.axis_index("y"), lax.axis_index("c")
        me = 4 * x + 2 * y + c
        for i in range(ns):
            all_s[i][me] = p_refs[i][...]
        cps = []
        for rel in range(1, 8):
            peer = (1 - x if rel & 4 else x, 1 - y if rel & 2 else y, 1 - c if rel & 1 else c)
            for i in range(ns):
                k = (rel - 1) * ns + i
                mine = all_s[i].at[me]
                rc = _remote(mine, mine, ssem.at[k], rsem.at[k], peer)
                rc.start()
                cps.append((rc, i, k, 4 * peer[0] + 2 * peer[1] + peer[2]))
        for rc, i, k, peer_slot in cps:
            rc.wait_send()
            theirs = all_s[i].at[peer_slot]
            _remote(theirs, theirs, ssem.at[k], rsem.at[k], (x, y, c)).wait_recv()
        for i in range(ns):
            o_refs[i][...] = all_s[i][...]
        comm.finish(*cargs)

    shapes = [_sds((8,) + t.shape, F32) for t in sent]
    outs = _pallas(
        body, name="exchange_small", grid=(1,), in_specs=[_whole(t) for t in sent] + [ANY] * n,
        out_specs=[_whole(t) for t in shapes] + [ANY] * n, out_shape=shapes + comm.out_shape,
        scratch_shapes=[pltpu.VMEM(t.shape, F32) for t in shapes]
        + [pltpu.SemaphoreType.DMA((7 * ns,)), pltpu.SemaphoreType.DMA((7 * ns,))] + comm.scratch,
        compiler_params=_params(("arbitrary",), 32),
    )(*sent, *comm.ins)
    return outs[:ns], outs[ns:]


def _adam_small(ws, ms, vs, parts, loss_part, comm):
    n = len(ws)
    gathered, received = _exchange_small(list(parts) + [loss_part], comm)

    def body(*refs):
        w_refs, m_refs, v_refs, a_refs = refs[:n], refs[n:2 * n], refs[2 * n:3 * n], refs[3 * n:4 * n + 1]
        outs = refs[4 * n + 1:]
        g_refs, d_refs, nm_refs, nv_refs, loss_ref = outs[:n], outs[n:2 * n], outs[2 * n:3 * n], outs[3 * n:4 * n], outs[4 * n]

        def total(i):
            t = a_refs[i][0]
            for k in range(1, 8):
                t = t + a_refs[i][k]
            return t

        for i in range(n):
            g = total(i)
            g_refs[i][...] = g
            d_refs[i][...], nm_refs[i][...], nv_refs[i][...] = _adamw(w_refs[i][...], g, m_refs[i][...], v_refs[i][...])
        loss_ref[...] = total(n)

    shapes = [_sds(w.shape, F32) for w in ws] * 4 + [_sds(loss_part.shape, F32)]
    ins = [*ws, *ms, *vs, *gathered]
    outs = _pallas(
        body, name="adam_small", grid=(1,), in_specs=[_whole(t) for t in ins], out_specs=[_whole(t) for t in shapes],
        out_shape=shapes, compiler_params=_params(("arbitrary",), 32),
    )(*ins)
    return outs[:n], outs[n:2 * n], outs[2 * n:3 * n], outs[3 * n:4 * n], outs[4 * n], received


def _local_step(xs, tgt, mems, weights, small, gather_mid=None, gather_ffn=None, reduce=False):
    wt_in, wt_ba, wt_bs, wo, wq, wkv, wt_o, wt_gu, wd = weights
    g_mix, b_gate, w_sgu, b_sgu, g_sgu, g_cross, g_mem, g_ffn, g_final = small
    wt = jnp.tril(w_sgu).astype(BF16)
    bst = b_sgu.T

    (a, qkv0, qkv1, qkv2, uv, gl), got = _fwd_in(xs, g_mix, wt_in, gather_mid)
    if gather_mid is not None:
        wt_ba, wt_bs, wo, wq, wkv, wt_o = got
    qkvs = (qkv0, qkv1, qkv2)
    os_, ls_ = zip(*[_attn_fwd(qkvs[g], g) for g in range(3)])
    (ya, ys, ba, bs, mg, h1), got = _fwd_mid(xs, os_, ls_, uv, gl, wt, bst, g_sgu, b_gate, wt_ba, wt_bs, wo, gather_ffn)
    if gather_ffn is not None:
        wt_gu, wd = got
    mb, kv = _mem_fwd(mems, g_mem, wkv)
    cb, qc, oc, h2 = _fwd_cross(h1, g_cross, wq, kv, wt_o)
    f, act, dgu, dh3b, dh2, dh2b, dg_ffn, dg_final, loss = _ffn_fwd_bwd(h2, tgt, g_ffn, g_final, wt_gu, wd)

    g_ffn_w = [_tn_matmul(dgu, f, "dw_gate_up", 1408), _tn_matmul(act, dh3b, "dw_down", 1408)]
    r_ffn = _Reduce(g_ffn_w, ["w_gate_up", "w_down"]) if reduce else None
    (dqc, dh1, dh1b, dkv, dg_cross), parts_ffn = _bwd_cross(dh2, h1, qc, g_cross, wq, kv, wt_o,
                                                           r_ffn.scatter if reduce else None)
    dw_kv, dw_kvb, dg_mem = _mem_bwd(dkv, mems, mb, g_mem, wkv)
    (dba, dbs, dgl, duv, do0, do1, do2, c0, c1, c2, db_gate, dg_sgu, dws, dbs_acc) = _bwd_mid(
        dh1, gl, ba, bs, uv, ls_, ya, wt, bst, g_sgu, b_gate, wt_ba, wt_bs, wo)
    dqkvs = [_attn_bwd(qkvs[g], do, ls_[g], corr, g) for g, (do, corr) in enumerate(((do0, c0), (do1, c1), (do2, c2)))]
    grad_x, dproj, dg_mix = _bwd_in(dqkvs, duv, dgl, dh1, xs, g_mix, wt_in)
    g_mid_w = [_tn_matmul(dba, ya, "dw_branch_attn", 1024),
               _tn_matmul(dbs, ys, "dw_branch_sgu", 1024),
               _tn_matmul(mg, dh1b, "dw_out", 1024),
               _tn_matmul(cb, dqc, "dw_q_cross", 1024),
               (dw_kv, dw_kvb),
               _tn_matmul(dh2b, oc, "dw_o_cross", 1024)]
    small_terms = (dg_mix, db_gate, dws, dbs_acc, dg_sgu, dg_cross, dg_mem, dg_ffn, dg_final)
    if not reduce:
        full = [_tn_matmul(dproj, a, "dw_in", 1792)] + g_mid_w + g_ffn_w
        return loss, grad_x, [g for g, _ in full], small_terms
    r_mid = _Reduce(g_mid_w, ["w_branch_attn", "w_branch_sgu", "w_out", "w_q_cross", "w_kv_cross", "w_o_cross"])
    g_in, parts_mid = _tn_matmul(dproj, a, "dw_in", 1792, comm=r_mid.scatter)
    r_in = _Reduce([g_in], ["w_in"])

    def finish(parts_in):
        return r_in.collect(parts_in) + r_mid.collect(parts_mid) + r_ffn.collect(parts_ffn)
    return loss, grad_x, (r_in.scatter, finish), small_terms


def kernel(x, mem, g_mix, w_in, b_gate, w_sgu_spatial, b_sgu_spatial, g_sgu, w_branch_attn, w_branch_sgu, w_out, g_cross, g_mem, w_q_cross, w_kv_cross, w_o_cross, g_ffn, w_gate_up, w_down, g_final, loss_target, m_g_mix, m_w_in, m_b_gate, m_w_sgu_spatial, m_b_sgu_spatial, m_g_sgu, m_w_branch_attn, m_w_branch_sgu, m_w_out, m_g_cross, m_g_mem, m_w_q_cross, m_w_kv_cross, m_w_o_cross, m_g_ffn, m_w_gate_up, m_w_down, m_g_final, v_g_mix, v_w_in, v_b_gate, v_w_sgu_spatial, v_b_sgu_spatial, v_g_sgu, v_w_branch_attn, v_w_branch_sgu, v_w_out, v_g_cross, v_g_mem, v_w_q_cross, v_w_kv_cross, v_w_o_cross, v_g_ffn, v_w_gate_up, v_w_down, v_g_final):
    S = x.shape[1]
    xs, tgt, mems = x.reshape(S, D), loss_target.reshape(S, D), mem.reshape(mem.shape[1], D)
    g_final2 = g_final.reshape(1, D)

    big = [("w_in", w_in[0], m_w_in[0], v_w_in[0], True),
           ("w_branch_attn", w_branch_attn[0], m_w_branch_attn[0], v_w_branch_attn[0], True),
           ("w_branch_sgu", w_branch_sgu[0], m_w_branch_sgu[0], v_w_branch_sgu[0], True),
           ("w_out", w_out[0], m_w_out[0], v_w_out[0], False),
           ("w_q_cross", w_q_cross[0], m_w_q_cross[0], v_w_q_cross[0], False),
           ("w_kv_cross", w_kv_cross[0], m_w_kv_cross[0], v_w_kv_cross[0], False),
           ("w_o_cross", w_o_cross[0], m_w_o_cross[0], v_w_o_cross[0], True),
           ("w_gate_up", w_gate_up[0], m_w_gate_up[0], v_w_gate_up[0], True),
           ("w_down", w_down[0], m_w_down[0], v_w_down[0], False)]
    shards = [(w.T if tr else w).astype(BF16) for _, w, _, _, tr in big]
    (wt_in,) = _gather_weights(shards[:1])
    (loss, grad_x, (scatter_in, finish_reduce),
     (dg_mix, db_gate, dws, dbs_acc, dg_sgu, dg_cross, dg_mem, dg_ffn, dg_final)) = _local_step(
        xs, tgt, mems, (wt_in,) + (None,) * 8,
        (g_mix, b_gate, w_sgu_spatial[0], b_sgu_spatial[0], g_sgu, g_cross, g_mem, g_ffn, g_final2),
        _Gather(shards[1:7]), _Gather(shards[7:9]), reduce=True)

    small = [("g_mix", g_mix, m_g_mix, v_g_mix, dg_mix), ("b_gate", b_gate, m_b_gate, v_b_gate, db_gate),
             ("w_sgu_spatial", w_sgu_spatial, m_w_sgu_spatial, v_w_sgu_spatial, jnp.tril(dws)),
             ("b_sgu_spatial", b_sgu_spatial, m_b_sgu_spatial, v_b_sgu_spatial, jnp.sum(dbs_acc, axis=-1)),
             ("g_sgu", g_sgu, m_g_sgu, v_g_sgu, dg_sgu), ("g_cross", g_cross, m_g_cross, v_g_cross, dg_cross),
             ("g_mem", g_mem, m_g_mem, v_g_mem, dg_mem), ("g_ffn", g_ffn, m_g_ffn, v_g_ffn, dg_ffn),
             ("g_final", g_final, m_g_final, v_g_final, dg_final)]
    as_term = lambda s, t: t.reshape(s[4].shape)
    gs, ds, nms, nvs, loss_all, parts_in = _adam_small(
        *[[as_term(s, s[k]) for s in small] for k in (1, 2, 3, 4)], loss, scatter_in)
    small_out = {s[0]: tuple(t[i].reshape(s[1].shape) for t in (gs, ds, nms, nvs)) for i, s in enumerate(small)}
    total_loss = loss_all[0, 0]

    full = _share_halves(finish_reduce(parts_in))
    big_out = {}
    for (name, w, m, v, tr), gsh in zip(big, full):
        gsh = gsh.T if tr else gsh
        delta, nm, nv = _elementwise(_adamw, [w, gsh, m, v], [F32, F32, F32], f"adam_{name}")
        big_out[name] = tuple(t[None] for t in (gsh, delta, nm, nv))

    order = ["g_mix", "w_in", "b_gate", "w_sgu_spatial", "b_sgu_spatial", "g_sgu", "w_branch_attn", "w_branch_sgu",
             "w_out", "g_cross", "g_mem", "w_q_cross", "w_kv_cross", "w_o_cross", "g_ffn", "w_gate_up", "w_down",
             "g_final"]
    res = {**big_out, **small_out}
    outs = [total_loss, grad_x.reshape(x.shape)]
    for k in range(4):
        outs += [res[nm][k] for nm in order]
    return tuple(outs)
```

```python
import math

import numpy as np
import jax
import jax.numpy as jnp
from jax import lax
from jax.experimental import pallas as pl
from jax.experimental.pallas import tpu as pltpu

F32, BF16 = jnp.float32, jnp.bfloat16
MESH = pl.DeviceIdType.MESH
ANY = pl.BlockSpec(memory_space=pl.ANY)
RES = pl.BlockSpec(memory_space=pltpu.VMEM)


def _pallas(body, **kw):
    call = pl.pallas_call(body, **kw)
    gs = kw.get("grid_spec")
    specs = kw.get("in_specs") if gs is None else [None] * gs.num_scalar_prefetch + list(gs.in_specs)

    def run(*args):
        if specs is not None:
            args = [a if (s is RES or s is None) else pltpu.with_memory_space_constraint(a, pltpu.HBM)
                    for a, s in zip(args, specs)]
        return call(*args)
    return run


def _whole(arr):
    nd = len(arr.shape)
    return pl.BlockSpec(arr.shape, lambda *_: (0,) * nd)

D = 1024
HEAD = 64
GROUP_W = 256
DIL_GROUPS = ((128, 1), (512, 4), (2048, 16))
BLK = 128
SGU_W = 512
MEM_HEADS, MEM_HD, MEM_W = 4, 128, 512
D_FF = 2816
FF_CHUNK = 256
EPS = 1e-6
NEG = -1e30
LR, B1, B2, AEPS, WD, STEP = 0.001, 0.9, 0.999, 1e-08, 0.01, 10
GELU_K, GELU_C = 0.7978845608028654, 0.044715


def _dot(a, b):
    return jnp.dot(a, b, preferred_element_type=F32)


def _dot_nt(a, b):
    return lax.dot_general(a, b, (((1,), (1,)), ((), ())), preferred_element_type=F32)


def _dot_tn(a, b):
    return lax.dot_general(a, b, (((0,), (0,)), ((), ())), preferred_element_type=F32)


def _row(tm, w):
    return pl.BlockSpec((tm, w), lambda i: (i, 0))


def _acc(shape):
    return pl.BlockSpec(shape, lambda i: (0,) * len(shape))


def _params(sem, mb):
    return pltpu.CompilerParams(dimension_semantics=sem, vmem_limit_bytes=mb << 20)


def _sds(shape, dt):
    return jax.ShapeDtypeStruct(shape, dt)


def _rms(h):
    return lax.rsqrt(jnp.mean(h * h, axis=-1, keepdims=True) + EPS)


def _rms_bwd(dy, h, r, g):
    t = dy * g
    dh = r * t - h * (r * r * r) * jnp.mean(t * h, axis=-1, keepdims=True)
    return dh, dy * h * r


def _gelu(x):
    t = jnp.tanh(GELU_K * (x + GELU_C * x * x * x))
    return 0.5 * x * (1.0 + t), t


def _gelu_grad(x, t):
    return 0.5 * (1.0 + t) + 0.5 * x * (1.0 - t * t) * GELU_K * (1.0 + 3.0 * GELU_C * x * x)


def _alibi_slopes():
    def pow2(n):
        start = 2.0 ** (-8.0 / n)
        return [start ** (i + 1) for i in range(n)]
    n = 12
    c = 2 ** int(math.floor(math.log2(n)))
    s = pow2(c) + pow2(2 * c)[0::2][: n - c]
    return np.array(sorted(s, reverse=True), dtype=np.float32).reshape(3, 4)


def _attn_bias(g):
    win, dil = DIL_GROUPS[g]
    steps = (np.arange(BLK)[:, None] + BLK) - np.arange(2 * BLK)[None, :]
    valid = (steps >= 0) & (steps <= win // dil)
    dist = (np.clip(steps, 0, None) * dil).astype(np.float32)
    b = -_alibi_slopes()[g][:, None, None] * dist[None]
    return np.where(valid[None], b, NEG).astype(np.float32)


def _head_masks():
    lane = lax.broadcasted_iota(jnp.int32, (1, GROUP_W), 1)
    return lane, [(lane >= HEAD * h) & (lane < HEAD * (h + 1)) for h in range(4)]


ATT_NB = 8


def _stack_heads(t, masks):
    z = jnp.zeros_like(t)
    return jnp.concatenate([jnp.where(m, t, z) for m in masks], axis=0)


def _unstack_heads(t, masks):
    out = jnp.zeros((BLK, GROUP_W), t.dtype)
    for h, m in enumerate(masks):
        out = jnp.where(m, t[h * BLK:(h + 1) * BLK], out)
    return out


def _stack_cols(ref, rows):
    return jnp.concatenate([ref[rows, HEAD * h:HEAD * h + 1] for h in range(4)], axis=0)


def _dil_spec(d, tm, w):
    return pl.BlockSpec((d, tm // d, w), lambda i: (0, i, 0))


def _to_dilated(val, s_ref, d, write):
    tm, w = val.shape
    for j in range(w // 128):
        s_ref[j, pl.ds(0, tm), :] = val[:, j * 128:(j + 1) * 128]
    for r in range(d):
        for j in range(w // 128):
            write(r, j, s_ref[j, pl.ds(r, tm // d, stride=d), :])


def _from_dilated(ref, s_ref, d, tm, w):
    if d == 1:
        return ref[0].astype(F32)
    for r in range(d):
        for j in range(w // 128):
            s_ref[j, pl.ds(r, tm // d, stride=d), :] = ref[r, :, j * 128:(j + 1) * 128].astype(F32)
    return jnp.concatenate([s_ref[j, pl.ds(0, tm), :] for j in range(w // 128)], axis=1)


def _fwd_in(x, g_mix, wt_in, gather=None, tm=512):
    S = x.shape[0]
    dils = [d for _, d in DIL_GROUPS]
    n = 0 if gather is None else gather.n
    last = S // tm - 1

    def body(*refs):
        x_ref, g_ref, w_ref = refs[:3]
        a_ref, q0_ref, q1_ref, q2_ref, uv_ref, gl_ref = refs[3 + n:9 + n]
        s_ref = refs[9 + 2 * n]
        comm = (refs[3:3 + n], refs[9 + n:9 + 2 * n], refs[10 + 2 * n:])
        if gather is not None:
            pl.when(pl.program_id(0) == 0)(lambda: gather.start(*comm))
        xv = x_ref[...]
        a = (xv * _rms(xv) * g_ref[...]).astype(BF16)
        a_ref[...] = a
        for g, (d, out) in enumerate(zip(dils, (q0_ref, q1_ref, q2_ref))):
            for part in range(3):
                rows = part * 768 + g * 256
                val = _dot_nt(a, w_ref[rows:rows + 256, :])
                if d == 1:
                    out[0, :, part * 256:(part + 1) * 256] = val.astype(BF16)
                else:
                    def write(r, j, piece, out=out, part=part):
                        out[r, :, part * 256 + j * 128:part * 256 + (j + 1) * 128] = piece.astype(BF16)
                    _to_dilated(val, s_ref, d, write)
        uv_ref[...] = _dot_nt(a, w_ref[2304:3328, :]).astype(BF16)
        gl_ref[...] = _dot_nt(a, w_ref[3328:5376, :]).astype(BF16)
        if gather is not None:
            pl.when(pl.program_id(0) == last)(lambda: gather.finish(*comm))

    outs = _pallas(
        body, grid=(S // tm,), name="fwd_in",
        in_specs=[_row(tm, D), _whole(g_mix), RES] + [ANY] * n,
        out_specs=[_row(tm, D)] + [_dil_spec(d, tm, 768) for d in dils] + [_row(tm, 1024), _row(tm, 2048)] + [ANY] * n,
        out_shape=[_sds((S, D), BF16)] + [_sds((d, S // d, 768), BF16) for d in dils]
        + [_sds((S, 1024), BF16), _sds((S, 2048), BF16)] + ([] if gather is None else gather.out_shape),
        scratch_shapes=[pltpu.VMEM((2, tm, 128), F32)] + ([] if gather is None else gather.scratch),
        compiler_params=_params(("arbitrary",), 60),
    )(x, g_mix, wt_in, *([] if gather is None else gather.halves))
    return outs[:6], ([] if gather is None else gather.full(outs[6:]))


def _attn_fwd(qkv, g):
    d, L, _ = qkv.shape
    nb = L // BLK
    bias = jnp.asarray(_attn_bias(g).reshape(4 * BLK, 2 * BLK))
    NB = min(ATT_NB, nb)
    W = NB * BLK

    def body(q_ref, kc_ref, kp_ref, vc_ref, vp_ref, b_ref, o_ref, l_ref):
        st = pl.program_id(1)
        k_all = jnp.concatenate([kp_ref[...], kc_ref[...]], axis=0)
        v_all = jnp.concatenate([vp_ref[...], vc_ref[...]], axis=0)
        lane, masks = _head_masks()
        for b in range(NB):
            rows = slice(b * BLK, (b + 1) * BLK)
            kk, vv = k_all[b * BLK:(b + 2) * BLK], v_all[b * BLK:(b + 2) * BLK]
            s = _dot_nt(_stack_heads(q_ref[rows, :], masks), kk) * 0.125 + b_ref[...]
            if b == 0:
                s = s + jnp.where((st == 0) & (lane < BLK), NEG, 0.0).astype(F32)
            mx = jnp.max(s, axis=-1, keepdims=True)
            e = jnp.exp(s - mx)
            den = jnp.sum(e, axis=-1, keepdims=True)
            o_ref[rows, :] = _unstack_heads(_dot(e.astype(BF16), vv) / den, masks)
            l_ref[rows, :] = _unstack_heads(mx + jnp.log(den), masks)

    def wide(col):
        return pl.BlockSpec((None, W, GROUP_W), lambda r, s: (r, s, col))

    def before(col):
        return pl.BlockSpec((None, BLK, GROUP_W), lambda r, s: (r, jnp.maximum(s * NB - 1, 0), col))

    return _pallas(
        body, grid=(d, nb // NB), name=f"attn_fwd_g{g}",
        in_specs=[wide(0), wide(1), before(1), wide(2), before(2),
                  pl.BlockSpec((4 * BLK, 2 * BLK), lambda r, s: (0, 0))],
        out_specs=[wide(0), wide(0)],
        out_shape=[_sds((d, L, GROUP_W), F32), _sds((d, L, GROUP_W), F32)],
        compiler_params=_params(("parallel", "parallel"), 32),
    )(qkv, qkv, qkv, qkv, qkv, bias)


def _group_weights(l0, l1, l2):
    m = jnp.maximum(jnp.maximum(l0, l1), l2)
    e0, e1, e2 = jnp.exp(l0 - m), jnp.exp(l1 - m), jnp.exp(l2 - m)
    inv = 1.0 / (e0 + e1 + e2)
    return e0 * inv, e1 * inv, e2 * inv


def _sgu_forward(uvf, gs, wt_ref, bst_ref, mixed_s, tm):
    z, t = _gelu(uvf)
    u, v = z[:, :SGU_W], z[:, SGU_W:]
    rv = _rms(v)
    vnb = (v * rv * gs).astype(BF16)
    for ci in range(tm // 128):
        for g in range(4):
            rs, cs = slice(ci * 128, (ci + 1) * 128), slice(g * 128, (g + 1) * 128)
            mixed_s[rs, cs] = _dot(wt_ref[g], vnb[rs, cs]) + bst_ref[:, g:g + 1]
    return u, v, rv, vnb, t


def _fwd_mid(x, os_, ls_, uv, gl, wt, bst, g_sgu, b_gate, wt_ba, wt_bs, w_out, gather=None, tm=512):
    S = x.shape[0]
    dils = [d for _, d in DIL_GROUPS]
    n = 0 if gather is None else gather.n
    last = S // tm - 1

    def body(*refs):
        (x_ref, o0, o1, o2, l0, l1, l2, uv_ref, gl_ref, wt_ref, bst_ref, gs_ref, bg_ref, wba_ref, wbs_ref,
         wo_ref) = refs[:16]
        ya_ref, ys_ref, ba_ref, bs_ref, mg_ref, h1_ref = refs[16 + n:22 + n]
        mixed_s, il_s = refs[22 + 2 * n:24 + 2 * n]
        comm = (refs[16:16 + n], refs[22 + n:22 + 2 * n], refs[24 + 2 * n:])
        if gather is not None:
            pl.when(pl.program_id(0) == 0)(lambda: gather.start(*comm))
        ls = [_from_dilated(r, il_s, d, tm, GROUP_W) for r, d in zip((l0, l1, l2), dils)]
        alphas = _group_weights(*ls)
        ya = jnp.zeros((tm, GROUP_W), F32)
        for a, r, d in zip(alphas, (o0, o1, o2), dils):
            ya = ya + a * _from_dilated(r, il_s, d, tm, GROUP_W)
        yab = ya.astype(BF16)
        ya_ref[...] = yab
        u, _, _, _, _ = _sgu_forward(uv_ref[...].astype(F32), gs_ref[...], wt_ref, bst_ref, mixed_s, tm)
        ysb = (u * mixed_s[...]).astype(BF16)
        ys_ref[...] = ysb
        gates = jax.nn.sigmoid(gl_ref[...].astype(F32) + bg_ref[...])
        ba = _dot_nt(yab, wba_ref[...])
        bs = _dot_nt(ysb, wbs_ref[...])
        ba_ref[...] = ba.astype(BF16)
        bs_ref[...] = bs.astype(BF16)
        mgb = (gates[:, :D] * ba + gates[:, D:] * bs).astype(BF16)
        mg_ref[...] = mgb
        h1_ref[...] = x_ref[...] + _dot(mgb, wo_ref[...])
        if gather is not None:
            pl.when(pl.program_id(0) == last)(lambda: gather.finish(*comm))

    gw = _row(tm, GROUP_W)
    dil = [_dil_spec(d, tm, GROUP_W) for d in dils]
    outs = _pallas(
        body, grid=(S // tm,), name="fwd_mid",
        in_specs=[_row(tm, D)] + dil + dil + [_row(tm, 1024), _row(tm, 2048)]
        + [_whole(t) for t in (wt, bst, g_sgu, b_gate)] + [RES] * 3 + [ANY] * n,
        out_specs=[gw, _row(tm, SGU_W), _row(tm, D), _row(tm, D), _row(tm, D), _row(tm, D)] + [ANY] * n,
        out_shape=[_sds((S, GROUP_W), BF16), _sds((S, SGU_W), BF16), _sds((S, D), BF16), _sds((S, D), BF16),
                   _sds((S, D), BF16), _sds((S, D), F32)] + ([] if gather is None else gather.out_shape),
        scratch_shapes=[pltpu.VMEM((tm, SGU_W), F32), pltpu.VMEM((2, tm, 128), F32)]
        + ([] if gather is None else gather.scratch),
        compiler_params=_params(("arbitrary",), 56),
    )(x, *os_, *ls_, uv, gl, wt, bst, g_sgu, b_gate, wt_ba, wt_bs, w_out, *([] if gather is None else gather.halves))
    return outs[:6], ([] if gather is None else gather.full(outs[6:]))


def _mem_fwd(mem, g_mem, w_kv):
    def body(m_ref, g_ref, w_ref, mb_ref, kv_ref):
        mv = m_ref[...]
        mb = (mv * _rms(mv) * g_ref[...]).astype(BF16)
        mb_ref[...] = mb
        kv_ref[...] = _dot(mb, w_ref[...]).astype(BF16)

    shapes = [_sds(mem.shape, BF16), _sds((mem.shape[0], 2 * MEM_W), BF16)]
    return _pallas(
        body, name="mem_fwd", grid=(1,), in_specs=[_whole(t) for t in (mem, g_mem, w_kv)],
        out_specs=[_whole(t) for t in shapes], out_shape=shapes, compiler_params=_params(("arbitrary",), 32),
    )(mem, g_mem, w_kv)


def _cross_probs(qh, kh):
    s = _dot_nt(qh, kh) * (MEM_HD ** -0.5)
    e = jnp.exp(s - jnp.max(s, axis=-1, keepdims=True))
    return e / jnp.sum(e, axis=-1, keepdims=True)


def _fwd_cross(h1, g_cross, w_q, kv, wt_o, tm=512):
    S = h1.shape[0]

    def body(h_ref, g_ref, wq_ref, kv_ref, wo_ref, c_ref, qc_ref, oc_ref, h2_ref):
        hv = h_ref[...]
        cb = (hv * _rms(hv) * g_ref[...]).astype(BF16)
        c_ref[...] = cb
        qcb = _dot(cb, wq_ref[...]).astype(BF16)
        qc_ref[...] = qcb
        for h in range(MEM_HEADS):
            cs = slice(h * MEM_HD, (h + 1) * MEM_HD)
            p = _cross_probs(qcb[:, cs], kv_ref[:, cs])
            oc_ref[:, cs] = _dot(p.astype(BF16), kv_ref[:, MEM_W + h * MEM_HD:MEM_W + (h + 1) * MEM_HD]).astype(BF16)
        h2_ref[...] = hv + _dot_nt(oc_ref[...], wo_ref[...])

    return _pallas(
        body, grid=(S // tm,), name="fwd_cross",
        in_specs=[_row(tm, D), _whole(g_cross), RES, _whole(kv), RES],
        out_specs=[_row(tm, D), _row(tm, MEM_W), _row(tm, MEM_W), _row(tm, D)],
        out_shape=[_sds((S, D), BF16), _sds((S, MEM_W), BF16), _sds((S, MEM_W), BF16), _sds((S, D), F32)],
        compiler_params=_params(("parallel",), 40),
    )(h1, g_cross, w_q, kv, wt_o)


def _ffn_fwd_bwd(h2, target, g_ffn, g_final, wt_gu, w_down, tm=256):
    S = h2.shape[0]
    nch = D_FF // FF_CHUNK

    def body(h_ref, t_ref, gf_ref, gz_ref, wgu_ref, wd_ref,
             f_ref, act_ref, dgu_ref, dh3b_ref, dh2_ref, dh2b_ref, dgf_ref, dgz_ref, loss_ref, gu_s, dact_s):
        i = pl.program_id(0)

        @pl.when(i == 0)
        def _():
            dgf_ref[...] = jnp.zeros_like(dgf_ref)
            dgz_ref[...] = jnp.zeros_like(dgz_ref)
            loss_ref[...] = jnp.zeros_like(loss_ref)

        hv = h_ref[...]
        r2 = _rms(hv)
        gf = gf_ref[...]
        fb = (hv * r2 * gf).astype(BF16)
        f_ref[...] = fb
        gu_s[...] = _dot_nt(fb, wgu_ref[...])
        for c in range(nch):
            cs = slice(c * FF_CHUNK, (c + 1) * FF_CHUNK)
            us = slice(D_FF + c * FF_CHUNK, D_FF + (c + 1) * FF_CHUNK)
            gt = gu_s[:, cs]
            act_ref[:, cs] = (gt * jax.nn.sigmoid(gt) * gu_s[:, us]).astype(BF16)
        h3 = hv + _dot(act_ref[...], wd_ref[...])
        r3 = _rms(h3)
        gz = gz_ref[...]
        diff = h3 * r3 * gz - t_ref[...]
        dy = diff * (1.0 / D)
        dh3, dgz_rows = _rms_bwd(dy, h3, r3, gz)
        dh3b = dh3.astype(BF16)
        dh3b_ref[...] = dh3b
        dact_s[...] = _dot_nt(dh3b, wd_ref[...])
        for c in range(nch):
            cs = slice(c * FF_CHUNK, (c + 1) * FF_CHUNK)
            us = slice(D_FF + c * FF_CHUNK, D_FF + (c + 1) * FF_CHUNK)
            dact, gt, up = dact_s[:, cs], gu_s[:, cs], gu_s[:, us]
            sg = jax.nn.sigmoid(gt)
            dgu_ref[:, cs] = (dact * up * (sg * (1.0 + gt * (1.0 - sg)))).astype(BF16)
            dgu_ref[:, us] = (dact * (gt * sg)).astype(BF16)
        df = _dot(dgu_ref[...], wgu_ref[...])
        dhn, dgf_rows = _rms_bwd(df, hv, r2, gf)
        dh2 = dh3 + dhn
        dh2_ref[...] = dh2
        dh2b_ref[...] = dh2.astype(BF16)
        dgf_ref[...] += jnp.sum(dgf_rows, axis=0, keepdims=True)
        dgz_ref[...] += jnp.sum(dgz_rows, axis=0, keepdims=True)
        loss_ref[...] += jnp.sum(jnp.sum(diff * diff, axis=0, keepdims=True), axis=1, keepdims=True) * (0.5 / D)

    return _pallas(
        body, grid=(S // tm,), name="ffn_fwd_bwd",
        in_specs=[_row(tm, D), _row(tm, D), _whole(g_ffn), _whole(g_final), RES, RES],
        out_specs=[_row(tm, D), _row(tm, D_FF), _row(tm, 2 * D_FF), _row(tm, D), _row(tm, D), _row(tm, D),
                   _acc((1, D)), _acc((1, D)), _acc((1, 128))],
        out_shape=[_sds((S, D), BF16), _sds((S, D_FF), BF16), _sds((S, 2 * D_FF), BF16), _sds((S, D), BF16),
                   _sds((S, D), F32), _sds((S, D), BF16), _sds((1, D), F32), _sds((1, D), F32), _sds((1, 128), F32)],
        scratch_shapes=[pltpu.VMEM((tm, 2 * D_FF), F32), pltpu.VMEM((tm, D_FF), F32)],
        compiler_params=_params(("arbitrary",), 60),
    )(h2, target, g_ffn, g_final, wt_gu, w_down)


def _bwd_cross(dh2, h1, qc, g_cross, w_q, kv, wt_o, comm=None, tm=512):
    S = h1.shape[0]
    n = 0 if comm is None else comm.n
    last = S // tm - 1

    def body(*refs):
        d_ref, h_ref, qc_ref, g_ref, wq_ref, kv_ref, wo_ref = refs[:7]
        dqc_ref, dh1_ref, dh1b_ref, dkv_ref, dg_ref = refs[7 + n:12 + n]
        cargs = (refs[7:7 + n], refs[12 + n:12 + 2 * n], refs[12 + 2 * n:])
        i = pl.program_id(0)

        @pl.when(i == 0)
        def _():
            dkv_ref[...] = jnp.zeros_like(dkv_ref)
            dg_ref[...] = jnp.zeros_like(dg_ref)
            if comm is not None:
                comm.start(*cargs)

        dh2 = d_ref[...]
        doc = _dot(dh2.astype(BF16), wo_ref[...])
        qcb = qc_ref[...]
        for h in range(MEM_HEADS):
            cs = slice(h * MEM_HD, (h + 1) * MEM_HD)
            vs = slice(MEM_W + h * MEM_HD, MEM_W + (h + 1) * MEM_HD)
            qh, kh, vh = qcb[:, cs], kv_ref[:, cs], kv_ref[:, vs]
            p = _cross_probs(qh, kh)
            dohb = doc[:, cs].astype(BF16)
            dp = _dot_nt(dohb, vh)
            dsb = (p * (dp - jnp.sum(dp * p, axis=-1, keepdims=True)) * (MEM_HD ** -0.5)).astype(BF16)
            dqc_ref[:, cs] = _dot(dsb, kh).astype(BF16)
            dkv_ref[:, cs] += _dot_tn(dsb, qh)
            dkv_ref[:, vs] += _dot_tn(p.astype(BF16), dohb)
        dc = _dot_nt(dqc_ref[...], wq_ref[...])
        hv = h_ref[...]
        dhn, dg_rows = _rms_bwd(dc, hv, _rms(hv), g_ref[...])
        dh1 = dh2 + dhn
        dh1_ref[...] = dh1
        dh1b_ref[...] = dh1.astype(BF16)
        dg_ref[...] += jnp.sum(dg_rows, axis=0, keepdims=True)
        if comm is not None:
            pl.when(i == last)(lambda: comm.finish(*cargs))

    outs = _pallas(
        body, grid=(S // tm,), name="bwd_cross",
        in_specs=[_row(tm, D), _row(tm, D), _row(tm, MEM_W), _whole(g_cross), RES, _whole(kv), RES] + [ANY] * n,
        out_specs=[_row(tm, MEM_W), _row(tm, D), _row(tm, D), _acc((256, 2 * MEM_W)), _acc((1, D))] + [ANY] * n,
        out_shape=[_sds((S, MEM_W), BF16), _sds((S, D), F32), _sds((S, D), BF16), _sds((256, 2 * MEM_W), F32),
                   _sds((1, D), F32)] + ([] if comm is None else comm.out_shape),
        scratch_shapes=[] if comm is None else comm.scratch,
        compiler_params=_params(("arbitrary",), 48),
    )(dh2, h1, qc, g_cross, w_q, kv, wt_o, *([] if comm is None else comm.ins))
    return outs[:5], outs[5:]


def _mem_bwd(dkv, mem, mb, g_mem, w_kv):
    def body(dkv_ref, m_ref, mb_ref, g_ref, w_ref, dw_ref, dwb_ref, dg_ref):
        dkvb = dkv_ref[...].astype(BF16)
        dw = _dot_tn(mb_ref[...], dkvb)
        dw_ref[...] = dw
        dwb_ref[...] = dw.astype(BF16)
        dm = _dot_nt(dkvb, w_ref[...])
        mv = m_ref[...]
        dg_ref[...] = jnp.sum(dm * mv * _rms(mv), axis=0, keepdims=True)

    shapes = [_sds((D, 2 * MEM_W), F32), _sds((D, 2 * MEM_W), BF16), _sds((1, D), F32)]
    return _pallas(
        body, name="mem_bwd", grid=(1,), in_specs=[_whole(t) for t in (dkv, mem, mb, g_mem, w_kv)],
        out_specs=[_whole(t) for t in shapes], out_shape=shapes, compiler_params=_params(("arbitrary",), 40),
    )(dkv, mem, mb, g_mem, w_kv)


def _bwd_mid(dh1, gl, ba, bs, uv, ls_, ya, wt, bst, g_sgu, b_gate, wt_ba, wt_bs, w_out, tm=512):
    S = dh1.shape[0]
    dils = [d for _, d in DIL_GROUPS]

    def body(d_ref, gl_ref, ba_ref, bs_ref, uv_ref, l0, l1, l2, ya_ref,
             wt_ref, bst_ref, gs_ref, bg_ref, wba_ref, wbs_ref, wo_ref,
             dba_ref, dbs_ref, dgl_ref, duv_ref, do0, do1, do2, c0, c1, c2,
             dbg_ref, dgs_ref, dws_ref, dbsa_ref, mixed_s, dvn_s, il_s):
        i = pl.program_id(0)

        @pl.when(i == 0)
        def _():
            for r in (dbg_ref, dgs_ref, dws_ref, dbsa_ref):
                r[...] = jnp.zeros_like(r)

        dm = _dot_nt(d_ref[...].astype(BF16), wo_ref[...])
        gates = jax.nn.sigmoid(gl_ref[...].astype(F32) + bg_ref[...])
        g0, g1 = gates[:, :D], gates[:, D:]
        dbab = (dm * g0).astype(BF16)
        dbsb = (dm * g1).astype(BF16)
        dba_ref[...] = dbab
        dbs_ref[...] = dbsb
        dg0 = dm * ba_ref[...].astype(F32) * g0 * (1.0 - g0)
        dg1 = dm * bs_ref[...].astype(F32) * g1 * (1.0 - g1)
        dgl_ref[:, :D] = dg0.astype(BF16)
        dgl_ref[:, D:] = dg1.astype(BF16)
        dbg_ref[:, :D] += jnp.sum(dg0, axis=0, keepdims=True)
        dbg_ref[:, D:] += jnp.sum(dg1, axis=0, keepdims=True)
        dya = _dot(dbab, wba_ref[...])
        dys = _dot(dbsb, wbs_ref[...])

        uvf = uv_ref[...].astype(F32)
        gs = gs_ref[...]
        u, v, rv, vnb, t = _sgu_forward(uvf, gs, wt_ref, bst_ref, mixed_s, tm)
        du = dys * mixed_s[...]
        dmixed = dys * u
        for ci in range(tm // 128):
            for g in range(4):
                rs, cs = slice(ci * 128, (ci + 1) * 128), slice(g * 128, (g + 1) * 128)
                dmx = dmixed[rs, cs]
                dmxb = dmx.astype(BF16)
                dvn_s[rs, cs] = _dot_tn(wt_ref[g], dmxb)
                dws_ref[g] += _dot_nt(dmxb, vnb[rs, cs])
                dbsa_ref[g] += dmx
        dv, dgs_rows = _rms_bwd(dvn_s[...], v, rv, gs)
        dgs_ref[...] += jnp.sum(dgs_rows, axis=0, keepdims=True)
        gg = _gelu_grad(uvf, t)
        duv_ref[:, :SGU_W] = (du * gg[:, :SGU_W]).astype(BF16)
        duv_ref[:, SGU_W:] = (dv * gg[:, SGU_W:]).astype(BF16)

        alphas = _group_weights(*[_from_dilated(r, il_s, d, tm, GROUP_W) for r, d in zip((l0, l1, l2), dils)])
        prod = dya * ya_ref[...].astype(F32)
        _, masks = _head_masks()
        hs = jnp.zeros_like(prod)
        for h in range(4):
            sh = jnp.sum(jnp.where(masks[h], prod, 0.0), axis=-1, keepdims=True)
            hs = jnp.where(masks[h], sh, hs)
        for a, d, do_ref, c_ref in zip(alphas, dils, (do0, do1, do2), (c0, c1, c2)):
            for val, out in ((a * dya, do_ref), (a * hs, c_ref)):
                if d == 1:
                    out[0] = val.astype(out.dtype)
                else:
                    def write(r, j, piece, out=out):
                        out[r, :, j * 128:(j + 1) * 128] = piece.astype(out.dtype)
                    _to_dilated(val, il_s, d, write)

    gw = _row(tm, GROUP_W)
    dil = [_dil_spec(d, tm, GROUP_W) for d in dils]
    return _pallas(
        body, grid=(S // tm,), name="bwd_mid",
        in_specs=[_row(tm, D), _row(tm, 2048), _row(tm, D), _row(tm, D), _row(tm, 1024)] + dil + [gw]
        + [_whole(t) for t in (wt, bst, g_sgu, b_gate)] + [RES] * 3,
        out_specs=[_row(tm, D), _row(tm, D), _row(tm, 2048), _row(tm, 1024)] + dil + dil
        + [_acc((1, 2048)), _acc((1, SGU_W)), _acc((4, 128, 128)), _acc((4, 128, 128))],
        out_shape=[_sds((S, D), BF16), _sds((S, D), BF16), _sds((S, 2048), BF16), _sds((S, 1024), BF16)]
        + [_sds((d, S // d, GROUP_W), BF16) for d in dils] + [_sds((d, S // d, GROUP_W), F32) for d in dils]
        + [_sds((1, 2048), F32), _sds((1, SGU_W), F32), _sds((4, 128, 128), F32), _sds((4, 128, 128), F32)],
        scratch_shapes=[pltpu.VMEM((tm, SGU_W), F32), pltpu.VMEM((tm, SGU_W), F32), pltpu.VMEM((2, tm, 128), F32)],
        compiler_params=_params(("arbitrary",), 60),
    )(dh1, gl, ba, bs, uv, *ls_, ya, wt, bst, g_sgu, b_gate, wt_ba, wt_bs, w_out)


def _attn_bwd(qkv, do, lse, corr, g):
    d, L, _ = qkv.shape
    nb = L // BLK
    NB = min(ATT_NB, nb)
    W = NB * BLK
    nsteps = nb // NB
    bias = jnp.asarray(_attn_bias(g).reshape(4 * BLK, 2 * BLK))

    def body(q_ref, kc_ref, kp_ref, vc_ref, vp_ref, do_ref, l_ref, c_ref, qn_ref, don_ref, ln_ref, cn_ref, b_ref,
             out_ref, dk_s, dv_s):
        st = pl.program_id(1)
        k_all = jnp.concatenate([kp_ref[...], kc_ref[...]], axis=0)
        v_all = jnp.concatenate([vp_ref[...], vc_ref[...]], axis=0)
        lane, masks = _head_masks()
        dk_s[...] = jnp.zeros_like(dk_s)
        dv_s[...] = jnp.zeros_like(dv_s)

        def block_terms(qs, dos, kk, vv, bias_v, lse_c, corr_c):
            s = _dot_nt(qs, kk) * 0.125 + bias_v
            p = jnp.exp(s - lse_c)
            dsb = (p * (_dot_nt(dos, vv) - corr_c) * 0.125).astype(BF16)
            return dsb, p.astype(BF16)

        for b in range(NB):
            rows = slice(b * BLK, (b + 1) * BLK)
            keys = slice(b * BLK, (b + 2) * BLK)
            kk, vv = k_all[keys], v_all[keys]
            qs, dos = _stack_heads(q_ref[rows, :], masks), _stack_heads(do_ref[rows, :], masks)
            bias_v = b_ref[...]
            if b == 0:
                bias_v = bias_v + jnp.where((st == 0) & (lane < BLK), NEG, 0.0).astype(F32)
            dsb, pb = block_terms(qs, dos, kk, vv, bias_v, _stack_cols(l_ref, rows), _stack_cols(c_ref, rows))
            out_ref[rows, 0:GROUP_W] = _unstack_heads(_dot(dsb, kk), masks).astype(BF16)
            dk_s[keys, :] += _dot_tn(dsb, qs)
            dv_s[keys, :] += _dot_tn(pb, dos)

        @pl.when(st < nsteps - 1)
        def _():
            last = slice(NB * BLK, (NB + 1) * BLK)
            qs, dos = _stack_heads(qn_ref[...], masks), _stack_heads(don_ref[...], masks)
            every = slice(None)
            dsb, pb = block_terms(qs, dos, k_all[last], v_all[last], b_ref[:, :BLK],
                                  _stack_cols(ln_ref, every), _stack_cols(cn_ref, every))
            dk_s[last, :] += _dot_tn(dsb, qs)
            dv_s[last, :] += _dot_tn(pb, dos)

        out_ref[:, GROUP_W:2 * GROUP_W] = dk_s[BLK:, :].astype(BF16)
        out_ref[:, 2 * GROUP_W:] = dv_s[BLK:, :].astype(BF16)

    def wide(col, w=GROUP_W):
        return pl.BlockSpec((None, W, w), lambda r, s: (r, s, col))

    def before(col):
        return pl.BlockSpec((None, BLK, GROUP_W), lambda r, s: (r, jnp.maximum(s * NB - 1, 0), col))

    def after(col):
        return pl.BlockSpec((None, BLK, GROUP_W), lambda r, s: (r, jnp.minimum((s + 1) * NB, nb - 1), col))

    return _pallas(
        body, grid=(d, nsteps), name=f"attn_bwd_g{g}",
        in_specs=[wide(0), wide(1), before(1), wide(2), before(2), wide(0), wide(0), wide(0),
                  after(0), after(0), after(0), after(0), pl.BlockSpec((4 * BLK, 2 * BLK), lambda r, s: (0, 0))],
        out_specs=wide(0, 768),
        out_shape=_sds((d, L, 768), BF16),
        scratch_shapes=[pltpu.VMEM(((NB + 1) * BLK, GROUP_W), F32), pltpu.VMEM(((NB + 1) * BLK, GROUP_W), F32)],
        compiler_params=_params(("parallel", "parallel"), 32),
    )(qkv, qkv, qkv, qkv, qkv, do, lse, corr, qkv, do, lse, corr, bias)


def _bwd_in(dqkvs, duv, dgl, dh1, x, g_mix, wt_in, tm=512):
    S = x.shape[0]
    dils = [d for _, d in DIL_GROUPS]

    def body(q0_ref, q1_ref, q2_ref, duv_ref, dgl_ref, d_ref, x_ref, g_ref, w_ref, dx_ref, dp_ref, dg_ref, il_s):
        i = pl.program_id(0)

        @pl.when(i == 0)
        def _():
            dg_ref[...] = jnp.zeros_like(dg_ref)

        for g, (d, ref) in enumerate(zip(dils, (q0_ref, q1_ref, q2_ref))):
            nat = _from_dilated(ref, il_s, d, tm, 768).astype(BF16)
            for part in range(3):
                col = part * 768 + g * 256
                dp_ref[:, col:col + 256] = nat[:, part * 256:(part + 1) * 256]
        dp_ref[:, 2304:3328] = duv_ref[...]
        dp_ref[:, 3328:5376] = dgl_ref[...]
        da = _dot(dp_ref[...], w_ref[...])
        xv = x_ref[...]
        dxn, dg_rows = _rms_bwd(da, xv, _rms(xv), g_ref[...])
        dx_ref[...] = d_ref[...] + dxn
        dg_ref[...] += jnp.sum(dg_rows, axis=0, keepdims=True)

    return _pallas(
        body, grid=(S // tm,), name="bwd_in",
        in_specs=[_dil_spec(d, tm, 768) for d in dils] + [_row(tm, 1024), _row(tm, 2048), _row(tm, D), _row(tm, D),
                                                          _whole(g_mix), RES],
        out_specs=[_row(tm, D), _row(tm, 5376), _acc((1, D))],
        out_shape=[_sds((S, D), F32), _sds((S, 5376), BF16), _sds((1, D), F32)],
        scratch_shapes=[pltpu.VMEM((6, tm, 128), F32)],
        compiler_params=_params(("arbitrary",), 60),
    )(*dqkvs, duv, dgl, dh1, x, g_mix, wt_in)


def _tn_matmul(a, b, name, tk, ts=2048, comm=None):
    S, K = a.shape
    N = b.shape[1]
    n = 0 if comm is None else comm.n
    nk, ns = K // tk, S // ts

    def body(*refs):
        a_ref, b_ref, o_ref, ob_ref = refs[0], refs[1], refs[2 + n], refs[3 + n]
        cargs = (refs[2:2 + n], refs[4 + n:4 + 2 * n], refs[4 + 2 * n:])
        k, s = pl.program_id(0), pl.program_id(1)
        if comm is not None:
            pl.when((k == 0) & (s == 0))(lambda: comm.start(*cargs))

        @pl.when(s == 0)
        def _():
            o_ref[...] = jnp.zeros_like(o_ref)

        o_ref[...] += _dot_tn(a_ref[...], b_ref[...])

        @pl.when(s == ns - 1)
        def _():
            ob_ref[...] = o_ref[...].astype(BF16)

        if comm is not None:
            pl.when((k == nk - 1) & (s == ns - 1))(lambda: comm.finish(*cargs))

    tile = pl.BlockSpec((tk, N), lambda k, s: (k, 0))
    outs = _pallas(
        body, grid=(nk, ns), name=name,
        in_specs=[pl.BlockSpec((ts, tk), lambda k, s: (s, k)), pl.BlockSpec((ts, N), lambda k, s: (s, 0))] + [ANY] * n,
        out_specs=[tile, tile] + [ANY] * n,
        out_shape=[_sds((K, N), F32), _sds((K, N), BF16)] + ([] if comm is None else comm.out_shape),
        scratch_shapes=[] if comm is None else comm.scratch,
        compiler_params=_params(("arbitrary", "arbitrary"), 56),
    )(a, b, *([] if comm is None else comm.ins))
    pair = (outs[0], outs[1])
    return pair if comm is None else (pair, outs[2:])


def _chip_peers(x, y):
    return [(1 - x, y), (x, 1 - y), (1 - x, 1 - y)]


STAGE_BYTES = 2 << 20


def _chunk_plan(shapes, itemsize):
    plan = []
    for i, (rows, w) in enumerate(shapes):
        ch = max(16, min(rows, (STAGE_BYTES // (w * itemsize)) // 16 * 16))
        while rows % ch:
            ch -= 16
        plan += [(i, r0, ch) for r0 in range(0, rows, ch)]
    return plan


def _remote(src, dst, ssem, rsem, dev):
    return pltpu.make_async_remote_copy(src_ref=src, dst_ref=dst, send_sem=ssem, recv_sem=rsem, device_id=dev,
                                        device_id_type=MESH)


class _Gather:
    def __init__(self, shards):
        self.n = len(shards)
        self.shards = shards
        self.halves = [s.reshape(2, s.shape[0] // 2, s.shape[1]) for s in shards]
        self.plan = _chunk_plan([h.shape[1:] for h in self.halves], 2)
        self.out_shape = [_sds((4,) + h.shape, BF16) for h in self.halves]
        n = self.n
        self.scratch = [pltpu.SemaphoreType.DMA((6 * n,)), pltpu.SemaphoreType.DMA((6 * n,)),
                        pltpu.SemaphoreType.DMA((2,)), pltpu.SemaphoreType.DMA((2,)),
                        pltpu.VMEM((2, max(p[2] for p in self.plan), max(h.shape[2] for h in self.halves)), BF16)]

    def full(self, outs):
        return [o.reshape(4 * s.shape[0], s.shape[1]) for o, s in zip(outs, self.shards)]

    def _sends(self, ins, outs, ssem, rsem):
        x, y, c = lax.axis_index("x"), lax.axis_index("y"), lax.axis_index("c")
        me = 2 * x + y
        return [_remote(ins[i].at[c], outs[i].at[me, c], ssem.at[6 * i + k], rsem.at[6 * i + k], (px, py, c))
                for i in range(self.n) for k, (px, py) in enumerate(_chip_peers(x, y))]

    def start(self, ins, outs, scratch):
        ssem, rsem, lsem, osem, buf = scratch
        me = 2 * lax.axis_index("x") + lax.axis_index("y")
        for cp in self._sends(ins, outs, ssem, rsem):
            cp.start()
        pending = {}
        for i, r0, ch in self.plan:
            for h in range(2):
                if h in pending:
                    pending[h].wait()
                stage = buf.at[h, pl.ds(0, ch), pl.ds(0, self.halves[i].shape[2])]
                ld = pltpu.make_async_copy(ins[i].at[h, pl.ds(r0, ch)], stage, lsem.at[h])
                ld.start()
                ld.wait()
                st = pltpu.make_async_copy(stage, outs[i].at[me, h, pl.ds(r0, ch)], osem.at[h])
                st.start()
                pending[h] = st
        for st in pending.values():
            st.wait()

    def finish(self, ins, outs, scratch):
        ssem, rsem = scratch[:2]
        x, y, c = lax.axis_index("x"), lax.axis_index("y"), lax.axis_index("c")
        chips = _chip_peers(x, y)
        sib = (x, y, 1 - c)
        forwards = []
        for i in range(self.n):
            for k, (px, py) in enumerate(chips):
                landed = outs[i].at[2 * px + py, c]
                _remote(landed, landed, ssem.at[6 * i + k], rsem.at[6 * i + k], (px, py, c)).wait_recv()
                cp = _remote(landed, landed, ssem.at[6 * i + 3 + k], rsem.at[6 * i + 3 + k], sib)
                cp.start()
                forwards.append(cp)
        for i in range(self.n):
            for k, (px, py) in enumerate(chips):
                passed = outs[i].at[2 * px + py, 1 - c]
                _remote(passed, passed, ssem.at[6 * i + 3 + k], rsem.at[6 * i + 3 + k], sib).wait_recv()
        for cp in self._sends(ins, outs, ssem, rsem) + forwards:
            cp.wait_send()


def _gather_weights(shards):
    gt = _Gather(shards)
    n = gt.n

    def body(*refs):
        ins, outs, scratch = refs[:n], refs[n:2 * n], refs[2 * n:]
        gt.start(ins, outs, scratch)
        gt.finish(ins, outs, scratch)

    outs = _pallas(
        body, name="gather_weights", in_specs=[ANY] * n, out_specs=[ANY] * n, out_shape=gt.out_shape,
        scratch_shapes=gt.scratch, compiler_params=pltpu.CompilerParams(vmem_limit_bytes=32 << 20),
    )(*gt.halves)
    return gt.full(outs)


def _swap_halves(grads):
    n = len(grads)
    view = lambda g: g.reshape(4, 2, g.shape[0] // 8, g.shape[1])
    g4f = [view(g) for g, _ in grads]
    g4 = [view(gb) for _, gb in grads]

    def body(*refs):
        ins, got = refs[:n], refs[n:2 * n]
        ssem, rsem = refs[2 * n:]
        x, y, c = lax.axis_index("x"), lax.axis_index("y"), lax.axis_index("c")
        sib = (x, y, 1 - c)
        cps = []
        for i in range(n):
            rc = _remote(ins[i].at[:, 1 - c], got[i], ssem.at[i], rsem.at[i], sib)
            rc.start()
            cps.append(rc)
        for cp in cps:
            cp.wait()

    half = [_sds((4, g.shape[2], g.shape[3]), BF16) for g in g4]
    got = _pallas(
        body, name="swap_halves", in_specs=[ANY] * n, out_specs=[ANY] * n, out_shape=half,
        scratch_shapes=[pltpu.SemaphoreType.DMA((n,)), pltpu.SemaphoreType.DMA((n,))],
    )(*g4)
    return g4f, got


def _chip_sum(g4, got, name):
    _, _, R, W = g4.shape
    tr = _tile(R, max(16, min(512, (1 << 18) // W // 16 * 16)))
    c = lax.axis_index("c").astype(jnp.int32).reshape(1)

    def body(c_ref, a_ref, b_ref, s_ref, sb_ref):
        s = a_ref[...] + b_ref[...].astype(F32)
        s_ref[...] = s
        sb_ref[...] = s.astype(BF16)

    plain = pl.BlockSpec((None, tr, W), lambda j, t, c_ref: (j, t, 0))
    return _pallas(
        body, name=name,
        grid_spec=pltpu.PrefetchScalarGridSpec(
            num_scalar_prefetch=1, grid=(4, R // tr),
            in_specs=[pl.BlockSpec((None, None, tr, W), lambda j, t, c_ref: (j, c_ref[0], t, 0)), plain],
            out_specs=[plain, plain]),
        out_shape=[_sds((4, R, W), F32), _sds((4, R, W), BF16)],
        compiler_params=_params(("parallel", "parallel"), 32),
    )(c, g4, got)


class _Scatter:
    def __init__(self, sums_b):
        self.n = len(sums_b)
        self.ins = list(sums_b)
        self.out_shape = [_sds((3,) + s.shape[1:], BF16) for s in sums_b]
        self.scratch = [pltpu.SemaphoreType.DMA((3 * self.n,)), pltpu.SemaphoreType.DMA((3 * self.n,))]

    def _copies(self, ins, outs, scratch):
        ssem, rsem = scratch
        x, y, c = lax.axis_index("x"), lax.axis_index("y"), lax.axis_index("c")
        return [_remote(ins[i].at[2 * px + py], outs[i].at[k], ssem.at[3 * i + k], rsem.at[3 * i + k], (px, py, c))
                for i in range(self.n) for k, (px, py) in enumerate(_chip_peers(x, y))]

    def start(self, ins, outs, scratch):
        for cp in self._copies(ins, outs, scratch):
            cp.start()

    def finish(self, ins, outs, scratch):
        for cp in self._copies(ins, outs, scratch):
            cp.wait()


class _Reduce:
    def __init__(self, grads, names):
        self.names = names
        g4, got = _swap_halves(grads)
        self.sums, sums_b = [], []
        for nm, g, t in zip(names, g4, got):
            s_, sb_ = _chip_sum(g, t, f"chip_sum_{nm}")
            self.sums.append(s_)
            sums_b.append(sb_)
        self.scatter = _Scatter(sums_b)

    def collect(self, parts):
        return [_mesh_sum(s, p, f"mesh_sum_{nm}") for nm, s, p in zip(self.names, self.sums, parts)]


def _mesh_sum(sums, parts, name):
    _, R, W = sums.shape
    tr = _tile(R, max(16, min(512, (1 << 18) // W // 16 * 16)))
    me = (2 * lax.axis_index("x") + lax.axis_index("y")).astype(jnp.int32).reshape(1)

    def body(me_ref, m_ref, p_ref, o_ref):
        o_ref[...] = m_ref[...] + p_ref[0].astype(F32) + p_ref[1].astype(F32) + p_ref[2].astype(F32)

    return _pallas(
        body, name=name,
        grid_spec=pltpu.PrefetchScalarGridSpec(
            num_scalar_prefetch=1, grid=(R // tr,),
            in_specs=[pl.BlockSpec((None, tr, W), lambda i, me_ref: (me_ref[0], i, 0)),
                      pl.BlockSpec((3, tr, W), lambda i, me_ref: (0, i, 0))],
            out_specs=pl.BlockSpec((tr, W), lambda i, me_ref: (i, 0))),
        out_shape=_sds((R, W), F32), compiler_params=_params(("parallel",), 32),
    )(me, sums, parts)


def _share_halves(reduced):
    n = len(reduced)
    plan = _chunk_plan([r.shape for r in reduced], 4)
    max_rows = max(p[2] for p in plan)
    max_w = max(r.shape[1] for r in reduced)

    def body(*refs):
        ins, outs = refs[:n], refs[n:2 * n]
        ssem, rsem, lsem, osem, buf = refs[2 * n:]
        x, y, c = lax.axis_index("x"), lax.axis_index("y"), lax.axis_index("c")
        sib = (x, y, 1 - c)
        pending = {}
        for k, (i, r0, ch) in enumerate(plan):
            slot = k % 2
            if slot in pending:
                rc, lc = pending[slot]
                rc.wait_send()
                lc.wait()
            stage = buf.at[slot, pl.ds(0, ch), pl.ds(0, reduced[i].shape[1])]
            ld = pltpu.make_async_copy(ins[i].at[pl.ds(r0, ch)], stage, lsem.at[slot])
            ld.start()
            ld.wait()
            place = outs[i].at[c, pl.ds(r0, ch)]
            rc = _remote(stage, place, ssem.at[slot], rsem.at[i], sib)
            lc = pltpu.make_async_copy(stage, place, osem.at[slot])
            rc.start()
            lc.start()
            pending[slot] = (rc, lc)
        for rc, lc in pending.values():
            rc.wait_send()
            lc.wait()
        for i in range(n):
            theirs = outs[i].at[1 - c]
            _remote(theirs, theirs, ssem.at[0], rsem.at[i], sib).wait_recv()

    outs = _pallas(
        body, name="share_halves", in_specs=[ANY] * n, out_specs=[ANY] * n,
        out_shape=[_sds((2,) + r.shape, F32) for r in reduced],
        scratch_shapes=[pltpu.SemaphoreType.DMA((2,)), pltpu.SemaphoreType.DMA((n,)), pltpu.SemaphoreType.DMA((2,)),
                        pltpu.SemaphoreType.DMA((2,)), pltpu.VMEM((2, max_rows, max_w), F32)],
        compiler_params=pltpu.CompilerParams(vmem_limit_bytes=32 << 20),
    )(*reduced)
    return [o.reshape(2 * r.shape[0], r.shape[1]) for o, r in zip(outs, reduced)]


def _tile(rows, cap=256):
    t = min(rows, cap) // 16 * 16
    while rows % t:
        t -= 16
    return t


def _adam_shard(w, g, m, v, name):
    _, R, W = w.shape
    tr = _tile(R, max(8, min(512, (1 << 18) // W // 8 * 8)))

    def body(w_ref, g_ref, m_ref, v_ref, go_ref, d_ref, nm_ref, nv_ref):
        gv = g_ref[...]
        go_ref[...] = gv
        d_ref[...], nm_ref[...], nv_ref[...] = _adamw(w_ref[...], gv, m_ref[...], v_ref[...])

    lead = pl.BlockSpec((None, tr, W), lambda i: (0, i, 0))
    return _pallas(
        body, grid=(R // tr,), name=name, in_specs=[lead, _row(tr, W), lead, lead], out_specs=[lead] * 4,
        out_shape=[_sds((1, R, W), F32)] * 4, compiler_params=_params(("parallel",), 48),
    )(w, g, m, v)


def _adamw(w, g, m, v):
    m = B1 * m + (1.0 - B1) * g
    v = B2 * v + (1.0 - B2) * (g * g)
    m_hat = m / (1.0 - B1 ** STEP)
    v_hat = v / (1.0 - B2 ** STEP)
    return -LR * (m_hat / (jnp.sqrt(v_hat) + AEPS) + WD * w), m, v


def _exchange_small(sent, comm):
    ns = len(sent)
    n = comm.n

    def body(*refs):
        p_refs, o_refs = refs[:ns], refs[ns + n:2 * ns + n]
        all_s = refs[2 * ns + 2 * n:3 * ns + 2 * n]
        ssem, rsem = refs[3 * ns + 2 * n:3 * ns + 2 * n + 2]
        cargs = (refs[ns:ns + n], refs[2 * ns + n:2 * ns + 2 * n], refs[3 * ns + 2 * n + 2:])
        comm.start(*cargs)
        x, y, c = lax.axis_index("x"), lax.axis_index("y"), lax.axis_index("c")
        me = 4 * x + 2 * y + c
        for i in range(ns):
            all_s[i][me] = p_refs[i][...]
        cps = []
        for rel in range(1, 8):
            peer = (1 - x if rel & 4 else x, 1 - y if rel & 2 else y, 1 - c if rel & 1 else c)
            for i in range(ns):
                k = (rel - 1) * ns + i
                mine = all_s[i].at[me]
                rc = _remote(mine, mine, ssem.at[k], rsem.at[k], peer)
                rc.start()
                cps.append((rc, i, k, 4 * peer[0] + 2 * peer[1] + peer[2]))
        for rc, i, k, peer_slot in cps:
            rc.wait_send()
            theirs = all_s[i].at[peer_slot]
            _remote(theirs, theirs, ssem.at[k], rsem.at[k], (x, y, c)).wait_recv()
        for i in range(ns):
            o_refs[i][...] = all_s[i][...]
        comm.finish(*cargs)

    shapes = [_sds((8,) + t.shape, F32) for t in sent]
    outs = _pallas(
        body, name="exchange_small", grid=(1,), in_specs=[_whole(t) for t in sent] + [ANY] * n,
        out_specs=[_whole(t) for t in shapes] + [ANY] * n, out_shape=shapes + comm.out_shape,
        scratch_shapes=[pltpu.VMEM(t.shape, F32) for t in shapes]
        + [pltpu.SemaphoreType.DMA((7 * ns,)), pltpu.SemaphoreType.DMA((7 * ns,))] + comm.scratch,
        compiler_params=_params(("arbitrary",), 32),
    )(*sent, *comm.ins)
    return outs[:ns], outs[ns:]


def _adam_small(ws, ms, vs, parts, loss_part, comm):
    n = len(ws)
    gathered, received = _exchange_small(list(parts) + [loss_part], comm)

    def body(*refs):
        w_refs, m_refs, v_refs, a_refs = refs[:n], refs[n:2 * n], refs[2 * n:3 * n], refs[3 * n:4 * n + 1]
        outs = refs[4 * n + 1:]
        g_refs, d_refs, nm_refs, nv_refs, loss_ref = outs[:n], outs[n:2 * n], outs[2 * n:3 * n], outs[3 * n:4 * n], outs[4 * n]

        def total(i):
            t = a_refs[i][0]
            for k in range(1, 8):
                t = t + a_refs[i][k]
            return t

        for i in range(n):
            g = total(i)
            g_refs[i][...] = g
            d_refs[i][...], nm_refs[i][...], nv_refs[i][...] = _adamw(w_refs[i][...], g, m_refs[i][...], v_refs[i][...])
        loss_ref[...] = total(n)

    shapes = [_sds(w.shape, F32) for w in ws] * 4 + [_sds(loss_part.shape, F32)]
    ins = [*ws, *ms, *vs, *gathered]
    outs = _pallas(
        body, name="adam_small", grid=(1,), in_specs=[_whole(t) for t in ins], out_specs=[_whole(t) for t in shapes],
        out_shape=shapes, compiler_params=_params(("arbitrary",), 32),
    )(*ins)
    return outs[:n], outs[n:2 * n], outs[2 * n:3 * n], outs[3 * n:4 * n], outs[4 * n], received


def _local_step(xs, tgt, mems, weights, small, gather_mid=None, gather_ffn=None, reduce=False):
    wt_in, wt_ba, wt_bs, wo, wq, wkv, wt_o, wt_gu, wd = weights
    g_mix, b_gate, w_sgu, b_sgu, g_sgu, g_cross, g_mem, g_ffn, g_final = small
    wt = jnp.tril(w_sgu).astype(BF16)
    bst = b_sgu.T

    (a, qkv0, qkv1, qkv2, uv, gl), got = _fwd_in(xs, g_mix, wt_in, gather_mid)
    if gather_mid is not None:
        wt_ba, wt_bs, wo, wq, wkv, wt_o = got
    qkvs = (qkv0, qkv1, qkv2)
    os_, ls_ = zip(*[_attn_fwd(qkvs[g], g) for g in range(3)])
    (ya, ys, ba, bs, mg, h1), got = _fwd_mid(xs, os_, ls_, uv, gl, wt, bst, g_sgu, b_gate, wt_ba, wt_bs, wo, gather_ffn)
    if gather_ffn is not None:
        wt_gu, wd = got
    mb, kv = _mem_fwd(mems, g_mem, wkv)
    cb, qc, oc, h2 = _fwd_cross(h1, g_cross, wq, kv, wt_o)
    f, act, dgu, dh3b, dh2, dh2b, dg_ffn, dg_final, loss = _ffn_fwd_bwd(h2, tgt, g_ffn, g_final, wt_gu, wd)

    g_ffn_w = [_tn_matmul(dgu, f, "dw_gate_up", 1408), _tn_matmul(act, dh3b, "dw_down", 1408)]
    r_ffn = _Reduce(g_ffn_w, ["w_gate_up", "w_down"]) if reduce else None
    (dqc, dh1, dh1b, dkv, dg_cross), parts_ffn = _bwd_cross(dh2, h1, qc, g_cross, wq, kv, wt_o,
                                                           r_ffn.scatter if reduce else None)
    dw_kv, dw_kvb, dg_mem = _mem_bwd(dkv, mems, mb, g_mem, wkv)
    (dba, dbs, dgl, duv, do0, do1, do2, c0, c1, c2, db_gate, dg_sgu, dws, dbs_acc) = _bwd_mid(
        dh1, gl, ba, bs, uv, ls_, ya, wt, bst, g_sgu, b_gate, wt_ba, wt_bs, wo)
    dqkvs = [_attn_bwd(qkvs[g], do, ls_[g], corr, g) for g, (do, corr) in enumerate(((do0, c0), (do1, c1), (do2, c2)))]
    grad_x, dproj, dg_mix = _bwd_in(dqkvs, duv, dgl, dh1, xs, g_mix, wt_in)
    g_mid_w = [_tn_matmul(dba, ya, "dw_branch_attn", 1024),
               _tn_matmul(dbs, ys, "dw_branch_sgu", 1024),
               _tn_matmul(mg, dh1b, "dw_out", 1024),
               _tn_matmul(cb, dqc, "dw_q_cross", 1024),
               (dw_kv, dw_kvb),
               _tn_matmul(dh2b, oc, "dw_o_cross", 1024)]
    small_terms = (dg_mix, db_gate, dws, dbs_acc, dg_sgu, dg_cross, dg_mem, dg_ffn, dg_final)
    if not reduce:
        full = [_tn_matmul(dproj, a, "dw_in", 1792)] + g_mid_w + g_ffn_w
        return loss, grad_x, [g for g, _ in full], small_terms
    r_mid = _Reduce(g_mid_w, ["w_branch_attn", "w_branch_sgu", "w_out", "w_q_cross", "w_kv_cross", "w_o_cross"])
    g_in, parts_mid = _tn_matmul(dproj, a, "dw_in", 1792, comm=r_mid.scatter)
    r_in = _Reduce([g_in], ["w_in"])

    def finish(parts_in):
        return r_in.collect(parts_in) + r_mid.collect(parts_mid) + r_ffn.collect(parts_ffn)
    return loss, grad_x, (r_in.scatter, finish), small_terms


def kernel(x, mem, g_mix, w_in, b_gate, w_sgu_spatial, b_sgu_spatial, g_sgu, w_branch_attn, w_branch_sgu, w_out, g_cross, g_mem, w_q_cross, w_kv_cross, w_o_cross, g_ffn, w_gate_up, w_down, g_final, loss_target, m_g_mix, m_w_in, m_b_gate, m_w_sgu_spatial, m_b_sgu_spatial, m_g_sgu, m_w_branch_attn, m_w_branch_sgu, m_w_out, m_g_cross, m_g_mem, m_w_q_cross, m_w_kv_cross, m_w_o_cross, m_g_ffn, m_w_gate_up, m_w_down, m_g_final, v_g_mix, v_w_in, v_b_gate, v_w_sgu_spatial, v_b_sgu_spatial, v_g_sgu, v_w_branch_attn, v_w_branch_sgu, v_w_out, v_g_cross, v_g_mem, v_w_q_cross, v_w_kv_cross, v_w_o_cross, v_g_ffn, v_w_gate_up, v_w_down, v_g_final):
    S = x.shape[1]
    xs, tgt, mems = x.reshape(S, D), loss_target.reshape(S, D), mem.reshape(mem.shape[1], D)
    g_final2 = g_final.reshape(1, D)

    big = [("w_in", w_in, m_w_in, v_w_in, True),
           ("w_branch_attn", w_branch_attn, m_w_branch_attn, v_w_branch_attn, True),
           ("w_branch_sgu", w_branch_sgu, m_w_branch_sgu, v_w_branch_sgu, True),
           ("w_out", w_out, m_w_out, v_w_out, False),
           ("w_q_cross", w_q_cross, m_w_q_cross, v_w_q_cross, False),
           ("w_kv_cross", w_kv_cross, m_w_kv_cross, v_w_kv_cross, False),
           ("w_o_cross", w_o_cross, m_w_o_cross, v_w_o_cross, True),
           ("w_gate_up", w_gate_up, m_w_gate_up, v_w_gate_up, True),
           ("w_down", w_down, m_w_down, v_w_down, False)]
    shards = [(w[0].T if tr else w[0]).astype(BF16) for _, w, _, _, tr in big]
    (wt_in,) = _gather_weights(shards[:1])
    (loss, grad_x, (scatter_in, finish_reduce),
     (dg_mix, db_gate, dws, dbs_acc, dg_sgu, dg_cross, dg_mem, dg_ffn, dg_final)) = _local_step(
        xs, tgt, mems, (wt_in,) + (None,) * 8,
        (g_mix, b_gate, w_sgu_spatial[0], b_sgu_spatial[0], g_sgu, g_cross, g_mem, g_ffn, g_final2),
        _Gather(shards[1:7]), _Gather(shards[7:9]), reduce=True)

    small = [("g_mix", g_mix, m_g_mix, v_g_mix, dg_mix), ("b_gate", b_gate, m_b_gate, v_b_gate, db_gate),
             ("w_sgu_spatial", w_sgu_spatial, m_w_sgu_spatial, v_w_sgu_spatial, jnp.tril(dws)),
             ("b_sgu_spatial", b_sgu_spatial, m_b_sgu_spatial, v_b_sgu_spatial, jnp.sum(dbs_acc, axis=-1)),
             ("g_sgu", g_sgu, m_g_sgu, v_g_sgu, dg_sgu), ("g_cross", g_cross, m_g_cross, v_g_cross, dg_cross),
             ("g_mem", g_mem, m_g_mem, v_g_mem, dg_mem), ("g_ffn", g_ffn, m_g_ffn, v_g_ffn, dg_ffn),
             ("g_final", g_final, m_g_final, v_g_final, dg_final)]
    as_term = lambda s, t: t.reshape(s[4].shape)
    gs, ds, nms, nvs, loss_all, parts_in = _adam_small(
        *[[as_term(s, s[k]) for s in small] for k in (1, 2, 3, 4)], loss, scatter_in)
    small_out = {s[0]: tuple(t[i].reshape(s[1].shape) for t in (gs, ds, nms, nvs)) for i, s in enumerate(small)}
    total_loss = loss_all[0, 0]

    full = _share_halves(finish_reduce(parts_in))
    big_out = {}
    for (name, w, m, v, tr), gsh in zip(big, full):
        if tr and w.shape[2] % 128:
            outs = _adam_shard(*(jnp.swapaxes(t, 1, 2) for t in (w,)), gsh, *(jnp.swapaxes(t, 1, 2) for t in (m, v)),
                               f"adam_{name}")
            big_out[name] = tuple(jnp.swapaxes(t, 1, 2) for t in outs)
        else:
            big_out[name] = tuple(_adam_shard(w, gsh.T if tr else gsh, m, v, f"adam_{name}"))

    order = ["g_mix", "w_in", "b_gate", "w_sgu_spatial", "b_sgu_spatial", "g_sgu", "w_branch_attn", "w_branch_sgu",
             "w_out", "g_cross", "g_mem", "w_q_cross", "w_kv_cross", "w_o_cross", "g_ffn", "w_gate_up", "w_down",
             "g_final"]
    res = {**big_out, **small_out}
    outs = [total_loss, grad_x.reshape(x.shape)]
    for k in range(4):
        outs += [res[nm][k] for nm in order]
    return tuple(outs)
```

```python
import math

import numpy as np
import jax
import jax.numpy as jnp
from jax import lax
from jax.experimental import pallas as pl
from jax.experimental.pallas import tpu as pltpu

F32, BF16 = jnp.float32, jnp.bfloat16
MESH = pl.DeviceIdType.MESH
ANY = pl.BlockSpec(memory_space=pl.ANY)
RES = pl.BlockSpec(memory_space=pltpu.VMEM)


def _pallas(body, **kw):
    call = pl.pallas_call(body, **kw)
    gs = kw.get("grid_spec")
    specs = kw.get("in_specs") if gs is None else [None] * gs.num_scalar_prefetch + list(gs.in_specs)

    def run(*args):
        if specs is not None:
            args = [a if (s is RES or s is None) else pltpu.with_memory_space_constraint(a, pltpu.HBM)
                    for a, s in zip(args, specs)]
        return call(*args)
    return run


def _whole(arr):
    nd = len(arr.shape)
    return pl.BlockSpec(arr.shape, lambda *_: (0,) * nd)

D = 1024
HEAD = 64
GROUP_W = 256
DIL_GROUPS = ((128, 1), (512, 4), (2048, 16))
BLK = 128
SGU_W = 512
MEM_HEADS, MEM_HD, MEM_W = 4, 128, 512
D_FF = 2816
FF_CHUNK = 256
EPS = 1e-6
NEG = -1e30
LR, B1, B2, AEPS, WD, STEP = 0.001, 0.9, 0.999, 1e-08, 0.01, 10
GELU_K, GELU_C = 0.7978845608028654, 0.044715


def _dot(a, b):
    return jnp.dot(a, b, preferred_element_type=F32)


def _dot_nt(a, b):
    return lax.dot_general(a, b, (((1,), (1,)), ((), ())), preferred_element_type=F32)


def _dot_tn(a, b):
    return lax.dot_general(a, b, (((0,), (0,)), ((), ())), preferred_element_type=F32)


def _row(tm, w):
    return pl.BlockSpec((tm, w), lambda i: (i, 0))


def _acc(shape):
    return pl.BlockSpec(shape, lambda i: (0,) * len(shape))


def _params(sem, mb):
    return pltpu.CompilerParams(dimension_semantics=sem, vmem_limit_bytes=mb << 20)


def _sds(shape, dt):
    return jax.ShapeDtypeStruct(shape, dt)


def _rms(h):
    return lax.rsqrt(jnp.mean(h * h, axis=-1, keepdims=True) + EPS)


def _rms_bwd(dy, h, r, g):
    t = dy * g
    dh = r * t - h * (r * r * r) * jnp.mean(t * h, axis=-1, keepdims=True)
    return dh, dy * h * r


def _gelu(x):
    t = jnp.tanh(GELU_K * (x + GELU_C * x * x * x))
    return 0.5 * x * (1.0 + t), t


def _gelu_grad(x, t):
    return 0.5 * (1.0 + t) + 0.5 * x * (1.0 - t * t) * GELU_K * (1.0 + 3.0 * GELU_C * x * x)


def _alibi_slopes():
    def pow2(n):
        start = 2.0 ** (-8.0 / n)
        return [start ** (i + 1) for i in range(n)]
    n = 12
    c = 2 ** int(math.floor(math.log2(n)))
    s = pow2(c) + pow2(2 * c)[0::2][: n - c]
    return np.array(sorted(s, reverse=True), dtype=np.float32).reshape(3, 4)


def _attn_bias(g):
    win, dil = DIL_GROUPS[g]
    steps = (np.arange(BLK)[:, None] + BLK) - np.arange(2 * BLK)[None, :]
    valid = (steps >= 0) & (steps <= win // dil)
    dist = (np.clip(steps, 0, None) * dil).astype(np.float32)
    b = -_alibi_slopes()[g][:, None, None] * dist[None]
    return np.where(valid[None], b, NEG).astype(np.float32)


def _head_masks():
    lane = lax.broadcasted_iota(jnp.int32, (1, GROUP_W), 1)
    return lane, [(lane >= HEAD * h) & (lane < HEAD * (h + 1)) for h in range(4)]


ATT_NB = 8


def _stack_heads(t, masks):
    z = jnp.zeros_like(t)
    return jnp.concatenate([jnp.where(m, t, z) for m in masks], axis=0)


def _unstack_heads(t, masks):
    out = jnp.zeros((BLK, GROUP_W), t.dtype)
    for h, m in enumerate(masks):
        out = jnp.where(m, t[h * BLK:(h + 1) * BLK], out)
    return out


def _stack_cols(ref, rows):
    return jnp.concatenate([ref[rows, HEAD * h:HEAD * h + 1] for h in range(4)], axis=0)


def _dil_spec(d, tm, w):
    return pl.BlockSpec((d, tm // d, w), lambda i: (0, i, 0))


def _to_dilated(val, s_ref, d, write):
    tm, w = val.shape
    for j in range(w // 128):
        s_ref[j, pl.ds(0, tm), :] = val[:, j * 128:(j + 1) * 128]
    for r in range(d):
        for j in range(w // 128):
            write(r, j, s_ref[j, pl.ds(r, tm // d, stride=d), :])


def _from_dilated(ref, s_ref, d, tm, w):
    if d == 1:
        return ref[0].astype(F32)
    for r in range(d):
        for j in range(w // 128):
            s_ref[j, pl.ds(r, tm // d, stride=d), :] = ref[r, :, j * 128:(j + 1) * 128].astype(F32)
    return jnp.concatenate([s_ref[j, pl.ds(0, tm), :] for j in range(w // 128)], axis=1)


def _fwd_in(x, g_mix, wt_in, gather=None, tm=512):
    S = x.shape[0]
    dils = [d for _, d in DIL_GROUPS]
    n = 0 if gather is None else gather.n
    last = S // tm - 1

    def body(*refs):
        x_ref, g_ref, w_ref = refs[:3]
        a_ref, q0_ref, q1_ref, q2_ref, uv_ref, gl_ref = refs[3 + n:9 + n]
        s_ref = refs[9 + 2 * n]
        comm = (refs[3:3 + n], refs[9 + n:9 + 2 * n], refs[10 + 2 * n:])
        if gather is not None:
            pl.when(pl.program_id(0) == 0)(lambda: gather.start(*comm))
        xv = x_ref[...]
        a = (xv * _rms(xv) * g_ref[...]).astype(BF16)
        a_ref[...] = a
        for g, (d, out) in enumerate(zip(dils, (q0_ref, q1_ref, q2_ref))):
            for part in range(3):
                rows = part * 768 + g * 256
                val = _dot_nt(a, w_ref[rows:rows + 256, :])
                if d == 1:
                    out[0, :, part * 256:(part + 1) * 256] = val.astype(BF16)
                else:
                    def write(r, j, piece, out=out, part=part):
                        out[r, :, part * 256 + j * 128:part * 256 + (j + 1) * 128] = piece.astype(BF16)
                    _to_dilated(val, s_ref, d, write)
        uv_ref[...] = _dot_nt(a, w_ref[2304:3328, :]).astype(BF16)
        gl_ref[...] = _dot_nt(a, w_ref[3328:5376, :]).astype(BF16)
        if gather is not None:
            pl.when(pl.program_id(0) == last)(lambda: gather.finish(*comm))

    outs = _pallas(
        body, grid=(S // tm,), name="fwd_in",
        in_specs=[_row(tm, D), _whole(g_mix), RES] + [ANY] * n,
        out_specs=[_row(tm, D)] + [_dil_spec(d, tm, 768) for d in dils] + [_row(tm, 1024), _row(tm, 2048)] + [ANY] * n,
        out_shape=[_sds((S, D), BF16)] + [_sds((d, S // d, 768), BF16) for d in dils]
        + [_sds((S, 1024), BF16), _sds((S, 2048), BF16)] + ([] if gather is None else gather.out_shape),
        scratch_shapes=[pltpu.VMEM((2, tm, 128), F32)] + ([] if gather is None else gather.scratch),
        compiler_params=_params(("arbitrary",), 60),
    )(x, g_mix, wt_in, *([] if gather is None else gather.halves))
    return outs[:6], ([] if gather is None else gather.full(outs[6:]))


def _attn_fwd(qkv, g):
    d, L, _ = qkv.shape
    nb = L // BLK
    bias = jnp.asarray(_attn_bias(g).reshape(4 * BLK, 2 * BLK))
    NB = min(ATT_NB, nb)
    W = NB * BLK

    def body(q_ref, kc_ref, kp_ref, vc_ref, vp_ref, b_ref, o_ref, l_ref):
        st = pl.program_id(1)
        k_all = jnp.concatenate([kp_ref[...], kc_ref[...]], axis=0)
        v_all = jnp.concatenate([vp_ref[...], vc_ref[...]], axis=0)
        lane, masks = _head_masks()
        for b in range(NB):
            rows = slice(b * BLK, (b + 1) * BLK)
            kk, vv = k_all[b * BLK:(b + 2) * BLK], v_all[b * BLK:(b + 2) * BLK]
            s = _dot_nt(_stack_heads(q_ref[rows, :], masks), kk) * 0.125 + b_ref[...]
            if b == 0:
                s = s + jnp.where((st == 0) & (lane < BLK), NEG, 0.0).astype(F32)
            mx = jnp.max(s, axis=-1, keepdims=True)
            e = jnp.exp(s - mx)
            den = jnp.sum(e, axis=-1, keepdims=True)
            o_ref[rows, :] = _unstack_heads(_dot(e.astype(BF16), vv) / den, masks)
            l_ref[rows, :] = _unstack_heads(mx + jnp.log(den), masks)

    def wide(col):
        return pl.BlockSpec((None, W, GROUP_W), lambda r, s: (r, s, col))

    def before(col):
        return pl.BlockSpec((None, BLK, GROUP_W), lambda r, s: (r, jnp.maximum(s * NB - 1, 0), col))

    return _pallas(
        body, grid=(d, nb // NB), name=f"attn_fwd_g{g}",
        in_specs=[wide(0), wide(1), before(1), wide(2), before(2),
                  pl.BlockSpec((4 * BLK, 2 * BLK), lambda r, s: (0, 0))],
        out_specs=[wide(0), wide(0)],
        out_shape=[_sds((d, L, GROUP_W), F32), _sds((d, L, GROUP_W), F32)],
        compiler_params=_params(("parallel", "parallel"), 32),
    )(qkv, qkv, qkv, qkv, qkv, bias)


def _group_weights(l0, l1, l2):
    m = jnp.maximum(jnp.maximum(l0, l1), l2)
    e0, e1, e2 = jnp.exp(l0 - m), jnp.exp(l1 - m), jnp.exp(l2 - m)
    inv = 1.0 / (e0 + e1 + e2)
    return e0 * inv, e1 * inv, e2 * inv


def _sgu_forward(uvf, gs, wt_ref, bst_ref, mixed_s, tm):
    z, t = _gelu(uvf)
    u, v = z[:, :SGU_W], z[:, SGU_W:]
    rv = _rms(v)
    vnb = (v * rv * gs).astype(BF16)
    for ci in range(tm // 128):
        for g in range(4):
            rs, cs = slice(ci * 128, (ci + 1) * 128), slice(g * 128, (g + 1) * 128)
            mixed_s[rs, cs] = _dot(wt_ref[g], vnb[rs, cs]) + bst_ref[:, g:g + 1]
    return u, v, rv, vnb, t


def _fwd_mid(x, os_, ls_, uv, gl, wt, bst, g_sgu, b_gate, wt_ba, wt_bs, w_out, gather=None, tm=512):
    S = x.shape[0]
    dils = [d for _, d in DIL_GROUPS]
    n = 0 if gather is None else gather.n
    last = S // tm - 1

    def body(*refs):
        (x_ref, o0, o1, o2, l0, l1, l2, uv_ref, gl_ref, wt_ref, bst_ref, gs_ref, bg_ref, wba_ref, wbs_ref,
         wo_ref) = refs[:16]
        ya_ref, ys_ref, ba_ref, bs_ref, mg_ref, h1_ref = refs[16 + n:22 + n]
        mixed_s, il_s = refs[22 + 2 * n:24 + 2 * n]
        comm = (refs[16:16 + n], refs[22 + n:22 + 2 * n], refs[24 + 2 * n:])
        if gather is not None:
            pl.when(pl.program_id(0) == 0)(lambda: gather.start(*comm))
        ls = [_from_dilated(r, il_s, d, tm, GROUP_W) for r, d in zip((l0, l1, l2), dils)]
        alphas = _group_weights(*ls)
        ya = jnp.zeros((tm, GROUP_W), F32)
        for a, r, d in zip(alphas, (o0, o1, o2), dils):
            ya = ya + a * _from_dilated(r, il_s, d, tm, GROUP_W)
        yab = ya.astype(BF16)
        ya_ref[...] = yab
        u, _, _, _, _ = _sgu_forward(uv_ref[...].astype(F32), gs_ref[...], wt_ref, bst_ref, mixed_s, tm)
        ysb = (u * mixed_s[...]).astype(BF16)
        ys_ref[...] = ysb
        gates = jax.nn.sigmoid(gl_ref[...].astype(F32) + bg_ref[...])
        ba = _dot_nt(yab, wba_ref[...])
        bs = _dot_nt(ysb, wbs_ref[...])
        ba_ref[...] = ba.astype(BF16)
        bs_ref[...] = bs.astype(BF16)
        mgb = (gates[:, :D] * ba + gates[:, D:] * bs).astype(BF16)
        mg_ref[...] = mgb
        h1_ref[...] = x_ref[...] + _dot(mgb, wo_ref[...])
        if gather is not None:
            pl.when(pl.program_id(0) == last)(lambda: gather.finish(*comm))

    gw = _row(tm, GROUP_W)
    dil = [_dil_spec(d, tm, GROUP_W) for d in dils]
    outs = _pallas(
        body, grid=(S // tm,), name="fwd_mid",
        in_specs=[_row(tm, D)] + dil + dil + [_row(tm, 1024), _row(tm, 2048)]
        + [_whole(t) for t in (wt, bst, g_sgu, b_gate)] + [RES] * 3 + [ANY] * n,
        out_specs=[gw, _row(tm, SGU_W), _row(tm, D), _row(tm, D), _row(tm, D), _row(tm, D)] + [ANY] * n,
        out_shape=[_sds((S, GROUP_W), BF16), _sds((S, SGU_W), BF16), _sds((S, D), BF16), _sds((S, D), BF16),
                   _sds((S, D), BF16), _sds((S, D), F32)] + ([] if gather is None else gather.out_shape),
        scratch_shapes=[pltpu.VMEM((tm, SGU_W), F32), pltpu.VMEM((2, tm, 128), F32)]
        + ([] if gather is None else gather.scratch),
        compiler_params=_params(("arbitrary",), 56),
    )(x, *os_, *ls_, uv, gl, wt, bst, g_sgu, b_gate, wt_ba, wt_bs, w_out, *([] if gather is None else gather.halves))
    return outs[:6], ([] if gather is None else gather.full(outs[6:]))


def _mem_fwd(mem, g_mem, w_kv):
    def body(m_ref, g_ref, w_ref, mb_ref, kv_ref):
        mv = m_ref[...]
        mb = (mv * _rms(mv) * g_ref[...]).astype(BF16)
        mb_ref[...] = mb
        kv_ref[...] = _dot(mb, w_ref[...]).astype(BF16)

    shapes = [_sds(mem.shape, BF16), _sds((mem.shape[0], 2 * MEM_W), BF16)]
    return _pallas(
        body, name="mem_fwd", grid=(1,), in_specs=[_whole(t) for t in (mem, g_mem, w_kv)],
        out_specs=[_whole(t) for t in shapes], out_shape=shapes, compiler_params=_params(("arbitrary",), 32),
    )(mem, g_mem, w_kv)


def _cross_probs(qh, kh):
    s = _dot_nt(qh, kh) * (MEM_HD ** -0.5)
    e = jnp.exp(s - jnp.max(s, axis=-1, keepdims=True))
    return e / jnp.sum(e, axis=-1, keepdims=True)


def _fwd_cross(h1, g_cross, w_q, kv, wt_o, tm=512):
    S = h1.shape[0]

    def body(h_ref, g_ref, wq_ref, kv_ref, wo_ref, c_ref, qc_ref, oc_ref, h2_ref):
        hv = h_ref[...]
        cb = (hv * _rms(hv) * g_ref[...]).astype(BF16)
        c_ref[...] = cb
        qcb = _dot(cb, wq_ref[...]).astype(BF16)
        qc_ref[...] = qcb
        for h in range(MEM_HEADS):
            cs = slice(h * MEM_HD, (h + 1) * MEM_HD)
            p = _cross_probs(qcb[:, cs], kv_ref[:, cs])
            oc_ref[:, cs] = _dot(p.astype(BF16), kv_ref[:, MEM_W + h * MEM_HD:MEM_W + (h + 1) * MEM_HD]).astype(BF16)
        h2_ref[...] = hv + _dot_nt(oc_ref[...], wo_ref[...])

    return _pallas(
        body, grid=(S // tm,), name="fwd_cross",
        in_specs=[_row(tm, D), _whole(g_cross), RES, _whole(kv), RES],
        out_specs=[_row(tm, D), _row(tm, MEM_W), _row(tm, MEM_W), _row(tm, D)],
        out_shape=[_sds((S, D), BF16), _sds((S, MEM_W), BF16), _sds((S, MEM_W), BF16), _sds((S, D), F32)],
        compiler_params=_params(("parallel",), 40),
    )(h1, g_cross, w_q, kv, wt_o)


def _ffn_fwd_bwd(h2, target, g_ffn, g_final, wt_gu, w_down, tm=256):
    S = h2.shape[0]
    nch = D_FF // FF_CHUNK

    def body(h_ref, t_ref, gf_ref, gz_ref, wgu_ref, wd_ref,
             f_ref, act_ref, dgu_ref, dh3b_ref, dh2_ref, dh2b_ref, dgf_ref, dgz_ref, loss_ref, gu_s, dact_s):
        i = pl.program_id(0)

        @pl.when(i == 0)
        def _():
            dgf_ref[...] = jnp.zeros_like(dgf_ref)
            dgz_ref[...] = jnp.zeros_like(dgz_ref)
            loss_ref[...] = jnp.zeros_like(loss_ref)

        hv = h_ref[...]
        r2 = _rms(hv)
        gf = gf_ref[...]
        fb = (hv * r2 * gf).astype(BF16)
        f_ref[...] = fb
        gu_s[...] = _dot_nt(fb, wgu_ref[...])
        for c in range(nch):
            cs = slice(c * FF_CHUNK, (c + 1) * FF_CHUNK)
            us = slice(D_FF + c * FF_CHUNK, D_FF + (c + 1) * FF_CHUNK)
            gt = gu_s[:, cs]
            act_ref[:, cs] = (gt * jax.nn.sigmoid(gt) * gu_s[:, us]).astype(BF16)
        h3 = hv + _dot(act_ref[...], wd_ref[...])
        r3 = _rms(h3)
        gz = gz_ref[...]
        diff = h3 * r3 * gz - t_ref[...]
        dy = diff * (1.0 / D)
        dh3, dgz_rows = _rms_bwd(dy, h3, r3, gz)
        dh3b = dh3.astype(BF16)
        dh3b_ref[...] = dh3b
        dact_s[...] = _dot_nt(dh3b, wd_ref[...])
        for c in range(nch):
            cs = slice(c * FF_CHUNK, (c + 1) * FF_CHUNK)
            us = slice(D_FF + c * FF_CHUNK, D_FF + (c + 1) * FF_CHUNK)
            dact, gt, up = dact_s[:, cs], gu_s[:, cs], gu_s[:, us]
            sg = jax.nn.sigmoid(gt)
            dgu_ref[:, cs] = (dact * up * (sg * (1.0 + gt * (1.0 - sg)))).astype(BF16)
            dgu_ref[:, us] = (dact * (gt * sg)).astype(BF16)
        df = _dot(dgu_ref[...], wgu_ref[...])
        dhn, dgf_rows = _rms_bwd(df, hv, r2, gf)
        dh2 = dh3 + dhn
        dh2_ref[...] = dh2
        dh2b_ref[...] = dh2.astype(BF16)
        dgf_ref[...] += jnp.sum(dgf_rows, axis=0, keepdims=True)
        dgz_ref[...] += jnp.sum(dgz_rows, axis=0, keepdims=True)
        loss_ref[...] += jnp.sum(jnp.sum(diff * diff, axis=0, keepdims=True), axis=1, keepdims=True) * (0.5 / D)

    return _pallas(
        body, grid=(S // tm,), name="ffn_fwd_bwd",
        in_specs=[_row(tm, D), _row(tm, D), _whole(g_ffn), _whole(g_final), RES, RES],
        out_specs=[_row(tm, D), _row(tm, D_FF), _row(tm, 2 * D_FF), _row(tm, D), _row(tm, D), _row(tm, D),
                   _acc((1, D)), _acc((1, D)), _acc((1, 128))],
        out_shape=[_sds((S, D), BF16), _sds((S, D_FF), BF16), _sds((S, 2 * D_FF), BF16), _sds((S, D), BF16),
                   _sds((S, D), F32), _sds((S, D), BF16), _sds((1, D), F32), _sds((1, D), F32), _sds((1, 128), F32)],
        scratch_shapes=[pltpu.VMEM((tm, 2 * D_FF), F32), pltpu.VMEM((tm, D_FF), F32)],
        compiler_params=_params(("arbitrary",), 60),
    )(h2, target, g_ffn, g_final, wt_gu, w_down)


def _bwd_cross(dh2, h1, qc, g_cross, w_q, kv, wt_o, comm=None, tm=512):
    S = h1.shape[0]
    n = 0 if comm is None else comm.n
    last = S // tm - 1

    def body(*refs):
        d_ref, h_ref, qc_ref, g_ref, wq_ref, kv_ref, wo_ref = refs[:7]
        dqc_ref, dh1_ref, dh1b_ref, dkv_ref, dg_ref = refs[7 + n:12 + n]
        cargs = (refs[7:7 + n], refs[12 + n:12 + 2 * n], refs[12 + 2 * n:])
        i = pl.program_id(0)

        @pl.when(i == 0)
        def _():
            dkv_ref[...] = jnp.zeros_like(dkv_ref)
            dg_ref[...] = jnp.zeros_like(dg_ref)
            if comm is not None:
                comm.start(*cargs)

        dh2 = d_ref[...]
        doc = _dot(dh2.astype(BF16), wo_ref[...])
        qcb = qc_ref[...]
        for h in range(MEM_HEADS):
            cs = slice(h * MEM_HD, (h + 1) * MEM_HD)
            vs = slice(MEM_W + h * MEM_HD, MEM_W + (h + 1) * MEM_HD)
            qh, kh, vh = qcb[:, cs], kv_ref[:, cs], kv_ref[:, vs]
            p = _cross_probs(qh, kh)
            dohb = doc[:, cs].astype(BF16)
            dp = _dot_nt(dohb, vh)
            dsb = (p * (dp - jnp.sum(dp * p, axis=-1, keepdims=True)) * (MEM_HD ** -0.5)).astype(BF16)
            dqc_ref[:, cs] = _dot(dsb, kh).astype(BF16)
            dkv_ref[:, cs] += _dot_tn(dsb, qh)
            dkv_ref[:, vs] += _dot_tn(p.astype(BF16), dohb)
        dc = _dot_nt(dqc_ref[...], wq_ref[...])
        hv = h_ref[...]
        dhn, dg_rows = _rms_bwd(dc, hv, _rms(hv), g_ref[...])
        dh1 = dh2 + dhn
        dh1_ref[...] = dh1
        dh1b_ref[...] = dh1.astype(BF16)
        dg_ref[...] += jnp.sum(dg_rows, axis=0, keepdims=True)
        if comm is not None:
            pl.when(i == last)(lambda: comm.finish(*cargs))

    outs = _pallas(
        body, grid=(S // tm,), name="bwd_cross",
        in_specs=[_row(tm, D), _row(tm, D), _row(tm, MEM_W), _whole(g_cross), RES, _whole(kv), RES] + [ANY] * n,
        out_specs=[_row(tm, MEM_W), _row(tm, D), _row(tm, D), _acc((256, 2 * MEM_W)), _acc((1, D))] + [ANY] * n,
        out_shape=[_sds((S, MEM_W), BF16), _sds((S, D), F32), _sds((S, D), BF16), _sds((256, 2 * MEM_W), F32),
                   _sds((1, D), F32)] + ([] if comm is None else comm.out_shape),
        scratch_shapes=[] if comm is None else comm.scratch,
        compiler_params=_params(("arbitrary",), 48),
    )(dh2, h1, qc, g_cross, w_q, kv, wt_o, *([] if comm is None else comm.ins))
    return outs[:5], outs[5:]


def _mem_bwd(dkv, mem, mb, g_mem, w_kv):
    def body(dkv_ref, m_ref, mb_ref, g_ref, w_ref, dw_ref, dwb_ref, dg_ref):
        dkvb = dkv_ref[...].astype(BF16)
        dw = _dot_tn(mb_ref[...], dkvb)
        dw_ref[...] = dw
        dwb_ref[...] = dw.astype(BF16)
        dm = _dot_nt(dkvb, w_ref[...])
        mv = m_ref[...]
        dg_ref[...] = jnp.sum(dm * mv * _rms(mv), axis=0, keepdims=True)

    shapes = [_sds((D, 2 * MEM_W), F32), _sds((D, 2 * MEM_W), BF16), _sds((1, D), F32)]
    return _pallas(
        body, name="mem_bwd", grid=(1,), in_specs=[_whole(t) for t in (dkv, mem, mb, g_mem, w_kv)],
        out_specs=[_whole(t) for t in shapes], out_shape=shapes, compiler_params=_params(("arbitrary",), 40),
    )(dkv, mem, mb, g_mem, w_kv)


def _bwd_mid(dh1, gl, ba, bs, uv, ls_, ya, wt, bst, g_sgu, b_gate, wt_ba, wt_bs, w_out, tm=512):
    S = dh1.shape[0]
    dils = [d for _, d in DIL_GROUPS]

    def body(d_ref, gl_ref, ba_ref, bs_ref, uv_ref, l0, l1, l2, ya_ref,
             wt_ref, bst_ref, gs_ref, bg_ref, wba_ref, wbs_ref, wo_ref,
             dba_ref, dbs_ref, dgl_ref, duv_ref, do0, do1, do2, c0, c1, c2,
             dbg_ref, dgs_ref, dws_ref, dbsa_ref, mixed_s, dvn_s, il_s):
        i = pl.program_id(0)

        @pl.when(i == 0)
        def _():
            for r in (dbg_ref, dgs_ref, dws_ref, dbsa_ref):
                r[...] = jnp.zeros_like(r)

        dm = _dot_nt(d_ref[...].astype(BF16), wo_ref[...])
        gates = jax.nn.sigmoid(gl_ref[...].astype(F32) + bg_ref[...])
        g0, g1 = gates[:, :D], gates[:, D:]
        dbab = (dm * g0).astype(BF16)
        dbsb = (dm * g1).astype(BF16)
        dba_ref[...] = dbab
        dbs_ref[...] = dbsb
        dg0 = dm * ba_ref[...].astype(F32) * g0 * (1.0 - g0)
        dg1 = dm * bs_ref[...].astype(F32) * g1 * (1.0 - g1)
        dgl_ref[:, :D] = dg0.astype(BF16)
        dgl_ref[:, D:] = dg1.astype(BF16)
        dbg_ref[:, :D] += jnp.sum(dg0, axis=0, keepdims=True)
        dbg_ref[:, D:] += jnp.sum(dg1, axis=0, keepdims=True)
        dya = _dot(dbab, wba_ref[...])
        dys = _dot(dbsb, wbs_ref[...])

        uvf = uv_ref[...].astype(F32)
        gs = gs_ref[...]
        u, v, rv, vnb, t = _sgu_forward(uvf, gs, wt_ref, bst_ref, mixed_s, tm)
        du = dys * mixed_s[...]
        dmixed = dys * u
        for ci in range(tm // 128):
            for g in range(4):
                rs, cs = slice(ci * 128, (ci + 1) * 128), slice(g * 128, (g + 1) * 128)
                dmx = dmixed[rs, cs]
                dmxb = dmx.astype(BF16)
                dvn_s[rs, cs] = _dot_tn(wt_ref[g], dmxb)
                dws_ref[g] += _dot_nt(dmxb, vnb[rs, cs])
                dbsa_ref[g] += dmx
        dv, dgs_rows = _rms_bwd(dvn_s[...], v, rv, gs)
        dgs_ref[...] += jnp.sum(dgs_rows, axis=0, keepdims=True)
        gg = _gelu_grad(uvf, t)
        duv_ref[:, :SGU_W] = (du * gg[:, :SGU_W]).astype(BF16)
        duv_ref[:, SGU_W:] = (dv * gg[:, SGU_W:]).astype(BF16)

        alphas = _group_weights(*[_from_dilated(r, il_s, d, tm, GROUP_W) for r, d in zip((l0, l1, l2), dils)])
        prod = dya * ya_ref[...].astype(F32)
        _, masks = _head_masks()
        hs = jnp.zeros_like(prod)
        for h in range(4):
            sh = jnp.sum(jnp.where(masks[h], prod, 0.0), axis=-1, keepdims=True)
            hs = jnp.where(masks[h], sh, hs)
        for a, d, do_ref, c_ref in zip(alphas, dils, (do0, do1, do2), (c0, c1, c2)):
            for val, out in ((a * dya, do_ref), (a * hs, c_ref)):
                if d == 1:
                    out[0] = val.astype(out.dtype)
                else:
                    def write(r, j, piece, out=out):
                        out[r, :, j * 128:(j + 1) * 128] = piece.astype(out.dtype)
                    _to_dilated(val, il_s, d, write)

    gw = _row(tm, GROUP_W)
    dil = [_dil_spec(d, tm, GROUP_W) for d in dils]
    return _pallas(
        body, grid=(S // tm,), name="bwd_mid",
        in_specs=[_row(tm, D), _row(tm, 2048), _row(tm, D), _row(tm, D), _row(tm, 1024)] + dil + [gw]
        + [_whole(t) for t in (wt, bst, g_sgu, b_gate)] + [RES] * 3,
        out_specs=[_row(tm, D), _row(tm, D), _row(tm, 2048), _row(tm, 1024)] + dil + dil
        + [_acc((1, 2048)), _acc((1, SGU_W)), _acc((4, 128, 128)), _acc((4, 128, 128))],
        out_shape=[_sds((S, D), BF16), _sds((S, D), BF16), _sds((S, 2048), BF16), _sds((S, 1024), BF16)]
        + [_sds((d, S // d, GROUP_W), BF16) for d in dils] + [_sds((d, S // d, GROUP_W), F32) for d in dils]
        + [_sds((1, 2048), F32), _sds((1, SGU_W), F32), _sds((4, 128, 128), F32), _sds((4, 128, 128), F32)],
        scratch_shapes=[pltpu.VMEM((tm, SGU_W), F32), pltpu.VMEM((tm, SGU_W), F32), pltpu.VMEM((2, tm, 128), F32)],
        compiler_params=_params(("arbitrary",), 60),
    )(dh1, gl, ba, bs, uv, *ls_, ya, wt, bst, g_sgu, b_gate, wt_ba, wt_bs, w_out)


def _attn_bwd(qkv, do, lse, corr, g):
    d, L, _ = qkv.shape
    nb = L // BLK
    NB = min(ATT_NB, nb)
    W = NB * BLK
    nsteps = nb // NB
    bias = jnp.asarray(_attn_bias(g).reshape(4 * BLK, 2 * BLK))

    def body(q_ref, kc_ref, kp_ref, vc_ref, vp_ref, do_ref, l_ref, c_ref, qn_ref, don_ref, ln_ref, cn_ref, b_ref,
             out_ref, dk_s, dv_s):
        st = pl.program_id(1)
        k_all = jnp.concatenate([kp_ref[...], kc_ref[...]], axis=0)
        v_all = jnp.concatenate([vp_ref[...], vc_ref[...]], axis=0)
        lane, masks = _head_masks()
        dk_s[...] = jnp.zeros_like(dk_s)
        dv_s[...] = jnp.zeros_like(dv_s)

        def block_terms(qs, dos, kk, vv, bias_v, lse_c, corr_c):
            s = _dot_nt(qs, kk) * 0.125 + bias_v
            p = jnp.exp(s - lse_c)
            dsb = (p * (_dot_nt(dos, vv) - corr_c) * 0.125).astype(BF16)
            return dsb, p.astype(BF16)

        for b in range(NB):
            rows = slice(b * BLK, (b + 1) * BLK)
            keys = slice(b * BLK, (b + 2) * BLK)
            kk, vv = k_all[keys], v_all[keys]
            qs, dos = _stack_heads(q_ref[rows, :], masks), _stack_heads(do_ref[rows, :], masks)
            bias_v = b_ref[...]
            if b == 0:
                bias_v = bias_v + jnp.where((st == 0) & (lane < BLK), NEG, 0.0).astype(F32)
            dsb, pb = block_terms(qs, dos, kk, vv, bias_v, _stack_cols(l_ref, rows), _stack_cols(c_ref, rows))
            out_ref[rows, 0:GROUP_W] = _unstack_heads(_dot(dsb, kk), masks).astype(BF16)
            dk_s[keys, :] += _dot_tn(dsb, qs)
            dv_s[keys, :] += _dot_tn(pb, dos)

        @pl.when(st < nsteps - 1)
        def _():
            last = slice(NB * BLK, (NB + 1) * BLK)
            qs, dos = _stack_heads(qn_ref[...], masks), _stack_heads(don_ref[...], masks)
            every = slice(None)
            dsb, pb = block_terms(qs, dos, k_all[last], v_all[last], b_ref[:, :BLK],
                                  _stack_cols(ln_ref, every), _stack_cols(cn_ref, every))
            dk_s[last, :] += _dot_tn(dsb, qs)
            dv_s[last, :] += _dot_tn(pb, dos)

        out_ref[:, GROUP_W:2 * GROUP_W] = dk_s[BLK:, :].astype(BF16)
        out_ref[:, 2 * GROUP_W:] = dv_s[BLK:, :].astype(BF16)

    def wide(col, w=GROUP_W):
        return pl.BlockSpec((None, W, w), lambda r, s: (r, s, col))

    def before(col):
        return pl.BlockSpec((None, BLK, GROUP_W), lambda r, s: (r, jnp.maximum(s * NB - 1, 0), col))

    def after(col):
        return pl.BlockSpec((None, BLK, GROUP_W), lambda r, s: (r, jnp.minimum((s + 1) * NB, nb - 1), col))

    return _pallas(
        body, grid=(d, nsteps), name=f"attn_bwd_g{g}",
        in_specs=[wide(0), wide(1), before(1), wide(2), before(2), wide(0), wide(0), wide(0),
                  after(0), after(0), after(0), after(0), pl.BlockSpec((4 * BLK, 2 * BLK), lambda r, s: (0, 0))],
        out_specs=wide(0, 768),
        out_shape=_sds((d, L, 768), BF16),
        scratch_shapes=[pltpu.VMEM(((NB + 1) * BLK, GROUP_W), F32), pltpu.VMEM(((NB + 1) * BLK, GROUP_W), F32)],
        compiler_params=_params(("parallel", "parallel"), 32),
    )(qkv, qkv, qkv, qkv, qkv, do, lse, corr, qkv, do, lse, corr, bias)


def _bwd_in(dqkvs, duv, dgl, dh1, x, g_mix, wt_in, tm=512):
    S = x.shape[0]
    dils = [d for _, d in DIL_GROUPS]

    def body(q0_ref, q1_ref, q2_ref, duv_ref, dgl_ref, d_ref, x_ref, g_ref, w_ref, dx_ref, dp_ref, dg_ref, il_s):
        i = pl.program_id(0)

        @pl.when(i == 0)
        def _():
            dg_ref[...] = jnp.zeros_like(dg_ref)

        for g, (d, ref) in enumerate(zip(dils, (q0_ref, q1_ref, q2_ref))):
            nat = _from_dilated(ref, il_s, d, tm, 768).astype(BF16)
            for part in range(3):
                col = part * 768 + g * 256
                dp_ref[:, col:col + 256] = nat[:, part * 256:(part + 1) * 256]
        dp_ref[:, 2304:3328] = duv_ref[...]
        dp_ref[:, 3328:5376] = dgl_ref[...]
        da = _dot(dp_ref[...], w_ref[...])
        xv = x_ref[...]
        dxn, dg_rows = _rms_bwd(da, xv, _rms(xv), g_ref[...])
        dx_ref[...] = d_ref[...] + dxn
        dg_ref[...] += jnp.sum(dg_rows, axis=0, keepdims=True)

    return _pallas(
        body, grid=(S // tm,), name="bwd_in",
        in_specs=[_dil_spec(d, tm, 768) for d in dils] + [_row(tm, 1024), _row(tm, 2048), _row(tm, D), _row(tm, D),
                                                          _whole(g_mix), RES],
        out_specs=[_row(tm, D), _row(tm, 5376), _acc((1, D))],
        out_shape=[_sds((S, D), F32), _sds((S, 5376), BF16), _sds((1, D), F32)],
        scratch_shapes=[pltpu.VMEM((6, tm, 128), F32)],
        compiler_params=_params(("arbitrary",), 60),
    )(*dqkvs, duv, dgl, dh1, x, g_mix, wt_in)


def _tn_matmul(a, b, name, tk, ts=2048, comm=None):
    S, K = a.shape
    N = b.shape[1]
    n = 0 if comm is None else comm.n
    nk, ns = K // tk, S // ts

    def body(*refs):
        a_ref, b_ref, o_ref, ob_ref = refs[0], refs[1], refs[2 + n], refs[3 + n]
        cargs = (refs[2:2 + n], refs[4 + n:4 + 2 * n], refs[4 + 2 * n:])
        k, s = pl.program_id(0), pl.program_id(1)
        if comm is not None:
            pl.when((k == 0) & (s == 0))(lambda: comm.start(*cargs))

        @pl.when(s == 0)
        def _():
            o_ref[...] = jnp.zeros_like(o_ref)

        o_ref[...] += _dot_tn(a_ref[...], b_ref[...])

        @pl.when(s == ns - 1)
        def _():
            ob_ref[...] = o_ref[...].astype(BF16)

        if comm is not None:
            pl.when((k == nk - 1) & (s == ns - 1))(lambda: comm.finish(*cargs))

    tile = pl.BlockSpec((tk, N), lambda k, s: (k, 0))
    outs = _pallas(
        body, grid=(nk, ns), name=name,
        in_specs=[pl.BlockSpec((ts, tk), lambda k, s: (s, k)), pl.BlockSpec((ts, N), lambda k, s: (s, 0))] + [ANY] * n,
        out_specs=[tile, tile] + [ANY] * n,
        out_shape=[_sds((K, N), F32), _sds((K, N), BF16)] + ([] if comm is None else comm.out_shape),
        scratch_shapes=[] if comm is None else comm.scratch,
        compiler_params=_params(("arbitrary", "arbitrary"), 56),
    )(a, b, *([] if comm is None else comm.ins))
    pair = (outs[0], outs[1])
    return pair if comm is None else (pair, outs[2:])


def _chip_peers(x, y):
    return [(1 - x, y), (x, 1 - y), (1 - x, 1 - y)]


STAGE_BYTES = 2 << 20


def _chunk_plan(shapes, itemsize):
    plan = []
    for i, (rows, w) in enumerate(shapes):
        ch = max(16, min(rows, (STAGE_BYTES // (w * itemsize)) // 16 * 16))
        while rows % ch:
            ch -= 16
        plan += [(i, r0, ch) for r0 in range(0, rows, ch)]
    return plan


def _remote(src, dst, ssem, rsem, dev):
    return pltpu.make_async_remote_copy(src_ref=src, dst_ref=dst, send_sem=ssem, recv_sem=rsem, device_id=dev,
                                        device_id_type=MESH)


class _Gather:
    def __init__(self, shards):
        self.n = len(shards)
        self.shards = shards
        self.halves = [s.reshape(2, s.shape[0] // 2, s.shape[1]) for s in shards]
        self.plan = _chunk_plan([h.shape[1:] for h in self.halves], 2)
        self.out_shape = [_sds((4,) + h.shape, BF16) for h in self.halves]
        n = self.n
        self.scratch = [pltpu.SemaphoreType.DMA((6 * n,)), pltpu.SemaphoreType.DMA((6 * n,)),
                        pltpu.SemaphoreType.DMA((2,)), pltpu.SemaphoreType.DMA((2,)),
                        pltpu.VMEM((2, max(p[2] for p in self.plan), max(h.shape[2] for h in self.halves)), BF16)]

    def full(self, outs):
        return [o.reshape(4 * s.shape[0], s.shape[1]) for o, s in zip(outs, self.shards)]

    def _sends(self, ins, outs, ssem, rsem):
        x, y, c = lax.axis_index("x"), lax.axis_index("y"), lax.axis_index("c")
        me = 2 * x + y
        return [_remote(ins[i].at[c], outs[i].at[me, c], ssem.at[6 * i + k], rsem.at[6 * i + k], (px, py, c))
                for i in range(self.n) for k, (px, py) in enumerate(_chip_peers(x, y))]

    def start(self, ins, outs, scratch):
        ssem, rsem, lsem, osem, buf = scratch
        me = 2 * lax.axis_index("x") + lax.axis_index("y")
        for cp in self._sends(ins, outs, ssem, rsem):
            cp.start()
        pending = {}
        for i, r0, ch in self.plan:
            for h in range(2):
                if h in pending:
                    pending[h].wait()
                stage = buf.at[h, pl.ds(0, ch), pl.ds(0, self.halves[i].shape[2])]
                ld = pltpu.make_async_copy(ins[i].at[h, pl.ds(r0, ch)], stage, lsem.at[h])
                ld.start()
                ld.wait()
                st = pltpu.make_async_copy(stage, outs[i].at[me, h, pl.ds(r0, ch)], osem.at[h])
                st.start()
                pending[h] = st
        for st in pending.values():
            st.wait()

    def finish(self, ins, outs, scratch):
        ssem, rsem = scratch[:2]
        x, y, c = lax.axis_index("x"), lax.axis_index("y"), lax.axis_index("c")
        chips = _chip_peers(x, y)
        sib = (x, y, 1 - c)
        forwards = []
        for i in range(self.n):
            for k, (px, py) in enumerate(chips):
                landed = outs[i].at[2 * px + py, c]
                _remote(landed, landed, ssem.at[6 * i + k], rsem.at[6 * i + k], (px, py, c)).wait_recv()
                cp = _remote(landed, landed, ssem.at[6 * i + 3 + k], rsem.at[6 * i + 3 + k], sib)
                cp.start()
                forwards.append(cp)
        for i in range(self.n):
            for k, (px, py) in enumerate(chips):
                passed = outs[i].at[2 * px + py, 1 - c]
                _remote(passed, passed, ssem.at[6 * i + 3 + k], rsem.at[6 * i + 3 + k], sib).wait_recv()
        for cp in self._sends(ins, outs, ssem, rsem) + forwards:
            cp.wait_send()


def _gather_weights(shards):
    gt = _Gather(shards)
    n = gt.n

    def body(*refs):
        ins, outs, scratch = refs[:n], refs[n:2 * n], refs[2 * n:]
        gt.start(ins, outs, scratch)
        gt.finish(ins, outs, scratch)

    outs = _pallas(
        body, name="gather_weights", in_specs=[ANY] * n, out_specs=[ANY] * n, out_shape=gt.out_shape,
        scratch_shapes=gt.scratch, compiler_params=pltpu.CompilerParams(vmem_limit_bytes=32 << 20),
    )(*gt.halves)
    return gt.full(outs)


def _swap_halves(grads):
    n = len(grads)
    view = lambda g: g.reshape(4, 2, g.shape[0] // 8, g.shape[1])
    g4f = [view(g) for g, _ in grads]
    g4 = [view(gb) for _, gb in grads]

    def body(*refs):
        ins, got = refs[:n], refs[n:2 * n]
        ssem, rsem = refs[2 * n:]
        x, y, c = lax.axis_index("x"), lax.axis_index("y"), lax.axis_index("c")
        sib = (x, y, 1 - c)
        cps = []
        for i in range(n):
            rc = _remote(ins[i].at[:, 1 - c], got[i], ssem.at[i], rsem.at[i], sib)
            rc.start()
            cps.append(rc)
        for cp in cps:
            cp.wait()

    half = [_sds((4, g.shape[2], g.shape[3]), BF16) for g in g4]
    got = _pallas(
        body, name="swap_halves", in_specs=[ANY] * n, out_specs=[ANY] * n, out_shape=half,
        scratch_shapes=[pltpu.SemaphoreType.DMA((n,)), pltpu.SemaphoreType.DMA((n,))],
    )(*g4)
    return g4f, got


def _chip_sum(g4, got, name):
    _, _, R, W = g4.shape
    tr = _tile(R, max(16, min(512, (1 << 18) // W // 16 * 16)))
    c = lax.axis_index("c").astype(jnp.int32).reshape(1)

    def body(c_ref, a_ref, b_ref, s_ref, sb_ref):
        s = a_ref[...] + b_ref[...].astype(F32)
        s_ref[...] = s
        sb_ref[...] = s.astype(BF16)

    plain = pl.BlockSpec((None, tr, W), lambda j, t, c_ref: (j, t, 0))
    return _pallas(
        body, name=name,
        grid_spec=pltpu.PrefetchScalarGridSpec(
            num_scalar_prefetch=1, grid=(4, R // tr),
            in_specs=[pl.BlockSpec((None, None, tr, W), lambda j, t, c_ref: (j, c_ref[0], t, 0)), plain],
            out_specs=[plain, plain]),
        out_shape=[_sds((4, R, W), F32), _sds((4, R, W), BF16)],
        compiler_params=_params(("parallel", "parallel"), 32),
    )(c, g4, got)


class _Scatter:
    def __init__(self, sums_b):
        self.n = len(sums_b)
        self.ins = list(sums_b)
        self.out_shape = [_sds((3,) + s.shape[1:], BF16) for s in sums_b]
        self.scratch = [pltpu.SemaphoreType.DMA((3 * self.n,)), pltpu.SemaphoreType.DMA((3 * self.n,))]

    def _copies(self, ins, outs, scratch):
        ssem, rsem = scratch
        x, y, c = lax.axis_index("x"), lax.axis_index("y"), lax.axis_index("c")
        return [_remote(ins[i].at[2 * px + py], outs[i].at[k], ssem.at[3 * i + k], rsem.at[3 * i + k], (px, py, c))
                for i in range(self.n) for k, (px, py) in enumerate(_chip_peers(x, y))]

    def start(self, ins, outs, scratch):
        for cp in self._copies(ins, outs, scratch):
            cp.start()

    def finish(self, ins, outs, scratch):
        for cp in self._copies(ins, outs, scratch):
            cp.wait()


class _Reduce:
    def __init__(self, grads, names):
        self.names = names
        g4, got = _swap_halves(grads)
        self.sums, sums_b = [], []
        for nm, g, t in zip(names, g4, got):
            s_, sb_ = _chip_sum(g, t, f"chip_sum_{nm}")
            self.sums.append(s_)
            sums_b.append(sb_)
        self.scatter = _Scatter(sums_b)

    def collect(self, parts):
        return [_mesh_sum(s, p, f"mesh_sum_{nm}") for nm, s, p in zip(self.names, self.sums, parts)]


def _mesh_sum(sums, parts, name):
    _, R, W = sums.shape
    tr = _tile(R, max(16, min(512, (1 << 18) // W // 16 * 16)))
    me = (2 * lax.axis_index("x") + lax.axis_index("y")).astype(jnp.int32).reshape(1)

    def body(me_ref, m_ref, p_ref, o_ref):
        o_ref[...] = m_ref[...] + p_ref[0].astype(F32) + p_ref[1].astype(F32) + p_ref[2].astype(F32)

    return _pallas(
        body, name=name,
        grid_spec=pltpu.PrefetchScalarGridSpec(
            num_scalar_prefetch=1, grid=(R // tr,),
            in_specs=[pl.BlockSpec((None, tr, W), lambda i, me_ref: (me_ref[0], i, 0)),
                      pl.BlockSpec((3, tr, W), lambda i, me_ref: (0, i, 0))],
            out_specs=pl.BlockSpec((tr, W), lambda i, me_ref: (i, 0))),
        out_shape=_sds((R, W), F32), compiler_params=_params(("parallel",), 32),
    )(me, sums, parts)


def _share_halves(reduced):
    n = len(reduced)
    plan = _chunk_plan([r.shape for r in reduced], 4)
    max_rows = max(p[2] for p in plan)
    max_w = max(r.shape[1] for r in reduced)

    def body(*refs):
        ins, outs = refs[:n], refs[n:2 * n]
        ssem, rsem, lsem, osem, buf = refs[2 * n:]
        x, y, c = lax.axis_index("x"), lax.axis_index("y"), lax.axis_index("c")
        sib = (x, y, 1 - c)
        pending = {}
        for k, (i, r0, ch) in enumerate(plan):
            slot = k % 2
            if slot in pending:
                rc, lc = pending[slot]
                rc.wait_send()
                lc.wait()
            stage = buf.at[slot, pl.ds(0, ch), pl.ds(0, reduced[i].shape[1])]
            ld = pltpu.make_async_copy(ins[i].at[pl.ds(r0, ch)], stage, lsem.at[slot])
            ld.start()
            ld.wait()
            place = outs[i].at[c, pl.ds(r0, ch)]
            rc = _remote(stage, place, ssem.at[slot], rsem.at[i], sib)
            lc = pltpu.make_async_copy(stage, place, osem.at[slot])
            rc.start()
            lc.start()
            pending[slot] = (rc, lc)
        for rc, lc in pending.values():
            rc.wait_send()
            lc.wait()
        for i in range(n):
            theirs = outs[i].at[1 - c]
            _remote(theirs, theirs, ssem.at[0], rsem.at[i], sib).wait_recv()

    outs = _pallas(
        body, name="share_halves", in_specs=[ANY] * n, out_specs=[ANY] * n,
        out_shape=[_sds((2,) + r.shape, F32) for r in reduced],
        scratch_shapes=[pltpu.SemaphoreType.DMA((2,)), pltpu.SemaphoreType.DMA((n,)), pltpu.SemaphoreType.DMA((2,)),
                        pltpu.SemaphoreType.DMA((2,)), pltpu.VMEM((2, max_rows, max_w), F32)],
        compiler_params=pltpu.CompilerParams(vmem_limit_bytes=32 << 20),
    )(*reduced)
    return [o.reshape(2 * r.shape[0], r.shape[1]) for o, r in zip(outs, reduced)]


def _tile(rows, cap=256):
    t = min(rows, cap) // 16 * 16
    while rows % t:
        t -= 16
    return t


def _adam_shard(w, g, m, v, name):
    _, R, W = w.shape
    tr = _tile(R, max(8, min(512, (1 << 18) // W // 8 * 8)))

    def body(w_ref, g_ref, m_ref, v_ref, go_ref, d_ref, nm_ref, nv_ref):
        gv = g_ref[...]
        go_ref[...] = gv
        d_ref[...], nm_ref[...], nv_ref[...] = _adamw(w_ref[...], gv, m_ref[...], v_ref[...])

    lead = pl.BlockSpec((None, tr, W), lambda i: (0, i, 0))
    return _pallas(
        body, grid=(R // tr,), name=name, in_specs=[lead, _row(tr, W), lead, lead], out_specs=[lead] * 4,
        out_shape=[_sds((1, R, W), F32)] * 4, compiler_params=_params(("parallel",), 48),
    )(w, g, m, v)


def _adamw(w, g, m, v):
    m = B1 * m + (1.0 - B1) * g
    v = B2 * v + (1.0 - B2) * (g * g)
    m_hat = m / (1.0 - B1 ** STEP)
    v_hat = v / (1.0 - B2 ** STEP)
    return -LR * (m_hat / (jnp.sqrt(v_hat) + AEPS) + WD * w), m, v


PACK_W = 1024


def _exchange_small(sent, comm):
    ns = len(sent)
    n = comm.n
    rows, lay = 0, {}
    for i, t in enumerate(sent):
        if t.ndim == 2 and (t.shape[1] <= PACK_W or (t.shape[0] == 1 and t.shape[1] % PACK_W == 0)):
            nr = t.shape[0] if t.shape[1] <= PACK_W else t.shape[1] // PACK_W
            lay[i] = (rows, nr)
            rows += nr
    alone = [i for i in range(ns) if i not in lay]
    slots = [(8, -(-rows // 8) * 8, PACK_W)] + [(8,) + sent[i].shape for i in alone]
    nx = len(slots)

    def body(*refs):
        p_refs, o_refs = refs[:ns], refs[ns + n:2 * ns + n]
        all_s = refs[2 * ns + 2 * n:2 * ns + 2 * n + nx]
        ssem, rsem = refs[2 * ns + 2 * n + nx:2 * ns + 2 * n + nx + 2]
        cargs = (refs[ns:ns + n], refs[2 * ns + n:2 * ns + 2 * n], refs[2 * ns + 2 * n + nx + 2:])
        comm.start(*cargs)
        x, y, c = lax.axis_index("x"), lax.axis_index("y"), lax.axis_index("c")
        me = 4 * x + 2 * y + c
        pack = all_s[0].at[me]
        pack[...] = jnp.zeros(slots[0][1:], F32)
        for i, (r0, nr) in lay.items():
            w = sent[i].shape[1]
            if w <= PACK_W:
                pack[r0:r0 + nr, 0:w] = p_refs[i][...]
            else:
                for j in range(nr):
                    pack[r0 + j:r0 + j + 1, :] = p_refs[i][:, j * PACK_W:(j + 1) * PACK_W]
        for q, i in enumerate(alone):
            all_s[1 + q][me] = p_refs[i][...]
        cps = []
        for rel in range(1, 8):
            peer = (1 - x if rel & 4 else x, 1 - y if rel & 2 else y, 1 - c if rel & 1 else c)
            for q in range(nx):
                k = (rel - 1) * nx + q
                mine = all_s[q].at[me]
                rc = _remote(mine, mine, ssem.at[k], rsem.at[k], peer)
                rc.start()
                cps.append((rc, q, k, 4 * peer[0] + 2 * peer[1] + peer[2]))
        for rc, q, k, peer_slot in cps:
            rc.wait_send()
            theirs = all_s[q].at[peer_slot]
            _remote(theirs, theirs, ssem.at[k], rsem.at[k], (x, y, c)).wait_recv()
        for k in range(8):
            for i, (r0, nr) in lay.items():
                w = sent[i].shape[1]
                if w <= PACK_W:
                    o_refs[i][k] = all_s[0][k, r0:r0 + nr, 0:w]
                else:
                    for j in range(nr):
                        o_refs[i][k, :, j * PACK_W:(j + 1) * PACK_W] = all_s[0][k, r0 + j:r0 + j + 1, :]
        for q, i in enumerate(alone):
            o_refs[i][...] = all_s[1 + q][...]
        comm.finish(*cargs)

    shapes = [_sds((8,) + t.shape, F32) for t in sent]
    outs = _pallas(
        body, name="exchange_small", grid=(1,), in_specs=[_whole(t) for t in sent] + [ANY] * n,
        out_specs=[_whole(t) for t in shapes] + [ANY] * n, out_shape=shapes + comm.out_shape,
        scratch_shapes=[pltpu.VMEM(s, F32) for s in slots]
        + [pltpu.SemaphoreType.DMA((7 * nx,)), pltpu.SemaphoreType.DMA((7 * nx,))] + comm.scratch,
        compiler_params=_params(("arbitrary",), 32),
    )(*sent, *comm.ins)
    return outs[:ns], outs[ns:]


def _adam_small(ws, ms, vs, parts, loss_part, comm):
    n = len(ws)
    gathered, received = _exchange_small(list(parts) + [loss_part], comm)

    def body(*refs):
        w_refs, m_refs, v_refs, a_refs = refs[:n], refs[n:2 * n], refs[2 * n:3 * n], refs[3 * n:4 * n + 1]
        outs = refs[4 * n + 1:]
        g_refs, d_refs, nm_refs, nv_refs, loss_ref = outs[:n], outs[n:2 * n], outs[2 * n:3 * n], outs[3 * n:4 * n], outs[4 * n]

        def total(i):
            t = a_refs[i][0]
            for k in range(1, 8):
                t = t + a_refs[i][k]
            return t

        for i in range(n):
            g = total(i)
            g_refs[i][...] = g
            d_refs[i][...], nm_refs[i][...], nv_refs[i][...] = _adamw(w_refs[i][...], g, m_refs[i][...], v_refs[i][...])
        loss_ref[...] = total(n)

    shapes = [_sds(w.shape, F32) for w in ws] * 4 + [_sds(loss_part.shape, F32)]
    ins = [*ws, *ms, *vs, *gathered]
    outs = _pallas(
        body, name="adam_small", grid=(1,), in_specs=[_whole(t) for t in ins], out_specs=[_whole(t) for t in shapes],
        out_shape=shapes, compiler_params=_params(("arbitrary",), 32),
    )(*ins)
    return outs[:n], outs[n:2 * n], outs[2 * n:3 * n], outs[3 * n:4 * n], outs[4 * n], received


def _local_step(xs, tgt, mems, weights, small, gather_mid=None, gather_ffn=None, reduce=False):
    wt_in, wt_ba, wt_bs, wo, wq, wkv, wt_o, wt_gu, wd = weights
    g_mix, b_gate, w_sgu, b_sgu, g_sgu, g_cross, g_mem, g_ffn, g_final = small
    wt = jnp.tril(w_sgu).astype(BF16)
    bst = b_sgu.T

    (a, qkv0, qkv1, qkv2, uv, gl), got = _fwd_in(xs, g_mix, wt_in, gather_mid)
    if gather_mid is not None:
        wt_ba, wt_bs, wo, wq, wkv, wt_o = got
    qkvs = (qkv0, qkv1, qkv2)
    os_, ls_ = zip(*[_attn_fwd(qkvs[g], g) for g in range(3)])
    (ya, ys, ba, bs, mg, h1), got = _fwd_mid(xs, os_, ls_, uv, gl, wt, bst, g_sgu, b_gate, wt_ba, wt_bs, wo, gather_ffn)
    if gather_ffn is not None:
        wt_gu, wd = got
    mb, kv = _mem_fwd(mems, g_mem, wkv)
    cb, qc, oc, h2 = _fwd_cross(h1, g_cross, wq, kv, wt_o)
    f, act, dgu, dh3b, dh2, dh2b, dg_ffn, dg_final, loss = _ffn_fwd_bwd(h2, tgt, g_ffn, g_final, wt_gu, wd)

    g_ffn_w = [_tn_matmul(dgu, f, "dw_gate_up", 1408), _tn_matmul(act, dh3b, "dw_down", 1408)]
    r_ffn = _Reduce(g_ffn_w, ["w_gate_up", "w_down"]) if reduce else None
    (dqc, dh1, dh1b, dkv, dg_cross), parts_ffn = _bwd_cross(dh2, h1, qc, g_cross, wq, kv, wt_o,
                                                           r_ffn.scatter if reduce else None)
    dw_kv, dw_kvb, dg_mem = _mem_bwd(dkv, mems, mb, g_mem, wkv)
    (dba, dbs, dgl, duv, do0, do1, do2, c0, c1, c2, db_gate, dg_sgu, dws, dbs_acc) = _bwd_mid(
        dh1, gl, ba, bs, uv, ls_, ya, wt, bst, g_sgu, b_gate, wt_ba, wt_bs, wo)
    dqkvs = [_attn_bwd(qkvs[g], do, ls_[g], corr, g) for g, (do, corr) in enumerate(((do0, c0), (do1, c1), (do2, c2)))]
    grad_x, dproj, dg_mix = _bwd_in(dqkvs, duv, dgl, dh1, xs, g_mix, wt_in)
    g_mid_w = [_tn_matmul(dba, ya, "dw_branch_attn", 1024),
               _tn_matmul(dbs, ys, "dw_branch_sgu", 1024),
               _tn_matmul(mg, dh1b, "dw_out", 1024),
               _tn_matmul(cb, dqc, "dw_q_cross", 1024),
               (dw_kv, dw_kvb),
               _tn_matmul(dh2b, oc, "dw_o_cross", 1024)]
    small_terms = (dg_mix, db_gate, dws, dbs_acc, dg_sgu, dg_cross, dg_mem, dg_ffn, dg_final)
    if not reduce:
        full = [_tn_matmul(dproj, a, "dw_in", 1792)] + g_mid_w + g_ffn_w
        return loss, grad_x, [g for g, _ in full], small_terms
    r_mid = _Reduce(g_mid_w, ["w_branch_attn", "w_branch_sgu", "w_out", "w_q_cross", "w_kv_cross", "w_o_cross"])
    g_in, parts_mid = _tn_matmul(dproj, a, "dw_in", 1792, comm=r_mid.scatter)
    r_in = _Reduce([g_in], ["w_in"])

    def finish(parts_in):
        return r_in.collect(parts_in) + r_mid.collect(parts_mid) + r_ffn.collect(parts_ffn)
    return loss, grad_x, (r_in.scatter, finish), small_terms


def kernel(x, mem, g_mix, w_in, b_gate, w_sgu_spatial, b_sgu_spatial, g_sgu, w_branch_attn, w_branch_sgu, w_out, g_cross, g_mem, w_q_cross, w_kv_cross, w_o_cross, g_ffn, w_gate_up, w_down, g_final, loss_target, m_g_mix, m_w_in, m_b_gate, m_w_sgu_spatial, m_b_sgu_spatial, m_g_sgu, m_w_branch_attn, m_w_branch_sgu, m_w_out, m_g_cross, m_g_mem, m_w_q_cross, m_w_kv_cross, m_w_o_cross, m_g_ffn, m_w_gate_up, m_w_down, m_g_final, v_g_mix, v_w_in, v_b_gate, v_w_sgu_spatial, v_b_sgu_spatial, v_g_sgu, v_w_branch_attn, v_w_branch_sgu, v_w_out, v_g_cross, v_g_mem, v_w_q_cross, v_w_kv_cross, v_w_o_cross, v_g_ffn, v_w_gate_up, v_w_down, v_g_final):
    S = x.shape[1]
    xs, tgt, mems = x.reshape(S, D), loss_target.reshape(S, D), mem.reshape(mem.shape[1], D)
    g_final2 = g_final.reshape(1, D)

    big = [("w_in", w_in, m_w_in, v_w_in, True),
           ("w_branch_attn", w_branch_attn, m_w_branch_attn, v_w_branch_attn, True),
           ("w_branch_sgu", w_branch_sgu, m_w_branch_sgu, v_w_branch_sgu, True),
           ("w_out", w_out, m_w_out, v_w_out, False),
           ("w_q_cross", w_q_cross, m_w_q_cross, v_w_q_cross, False),
           ("w_kv_cross", w_kv_cross, m_w_kv_cross, v_w_kv_cross, False),
           ("w_o_cross", w_o_cross, m_w_o_cross, v_w_o_cross, True),
           ("w_gate_up", w_gate_up, m_w_gate_up, v_w_gate_up, True),
           ("w_down", w_down, m_w_down, v_w_down, False)]
    shards = [(w[0].T if tr else w[0]).astype(BF16) for _, w, _, _, tr in big]
    (wt_in,) = _gather_weights(shards[:1])
    (loss, grad_x, (scatter_in, finish_reduce),
     (dg_mix, db_gate, dws, dbs_acc, dg_sgu, dg_cross, dg_mem, dg_ffn, dg_final)) = _local_step(
        xs, tgt, mems, (wt_in,) + (None,) * 8,
        (g_mix, b_gate, w_sgu_spatial[0], b_sgu_spatial[0], g_sgu, g_cross, g_mem, g_ffn, g_final2),
        _Gather(shards[1:7]), _Gather(shards[7:9]), reduce=True)

    small = [("g_mix", g_mix, m_g_mix, v_g_mix, dg_mix), ("b_gate", b_gate, m_b_gate, v_b_gate, db_gate),
             ("w_sgu_spatial", w_sgu_spatial, m_w_sgu_spatial, v_w_sgu_spatial, jnp.tril(dws)),
             ("b_sgu_spatial", b_sgu_spatial, m_b_sgu_spatial, v_b_sgu_spatial, jnp.sum(dbs_acc, axis=-1)),
             ("g_sgu", g_sgu, m_g_sgu, v_g_sgu, dg_sgu), ("g_cross", g_cross, m_g_cross, v_g_cross, dg_cross),
             ("g_mem", g_mem, m_g_mem, v_g_mem, dg_mem), ("g_ffn", g_ffn, m_g_ffn, v_g_ffn, dg_ffn),
             ("g_final", g_final, m_g_final, v_g_final, dg_final)]
    as_term = lambda s, t: t.reshape(s[4].shape)
    gs, ds, nms, nvs, loss_all, parts_in = _adam_small(
        *[[as_term(s, s[k]) for s in small] for k in (1, 2, 3, 4)], loss, scatter_in)
    small_out = {s[0]: tuple(t[i].reshape(s[1].shape) for t in (gs, ds, nms, nvs)) for i, s in enumerate(small)}
    total_loss = loss_all[0, 0]

    full = _share_halves(finish_reduce(parts_in))
    big_out = {}
    for (name, w, m, v, tr), gsh in zip(big, full):
        if tr and w.shape[2] % 128:
            outs = _adam_shard(*(jnp.swapaxes(t, 1, 2) for t in (w,)), gsh, *(jnp.swapaxes(t, 1, 2) for t in (m, v)),
                               f"adam_{name}")
            big_out[name] = tuple(jnp.swapaxes(t, 1, 2) for t in outs)
        else:
            big_out[name] = tuple(_adam_shard(w, gsh.T if tr else gsh, m, v, f"adam_{name}"))

    order = ["g_mix", "w_in", "b_gate", "w_sgu_spatial", "b_sgu_spatial", "g_sgu", "w_branch_attn", "w_branch_sgu",
             "w_out", "g_cross", "g_mem", "w_q_cross", "w_kv_cross", "w_o_cross", "g_ffn", "w_gate_up", "w_down",
             "g_final"]
    res = {**big_out, **small_out}
    outs = [total_loss, grad_x.reshape(x.shape)]
    for k in range(4):
        outs += [res[nm][k] for nm in order]
    return tuple(outs)
```

```python
import math

import numpy as np
import jax
import jax.numpy as jnp
from jax import lax
from jax.experimental import pallas as pl
from jax.experimental.pallas import tpu as pltpu

F32, BF16 = jnp.float32, jnp.bfloat16
MESH = pl.DeviceIdType.MESH
ANY = pl.BlockSpec(memory_space=pl.ANY)
RES = pl.BlockSpec(memory_space=pltpu.VMEM)


def _pallas(body, **kw):
    call = pl.pallas_call(body, **kw)
    gs = kw.get("grid_spec")
    specs = kw.get("in_specs") if gs is None else [None] * gs.num_scalar_prefetch + list(gs.in_specs)

    def run(*args):
        if specs is not None:
            args = [a if (s is RES or s is None) else pltpu.with_memory_space_constraint(a, pltpu.HBM)
                    for a, s in zip(args, specs)]
        return call(*args)
    return run


def _whole(arr):
    nd = len(arr.shape)
    return pl.BlockSpec(arr.shape, lambda *_: (0,) * nd)

D = 1024
HEAD = 64
GROUP_W = 256
DIL_GROUPS = ((128, 1), (512, 4), (2048, 16))
BLK = 128
SGU_W = 512
MEM_HEADS, MEM_HD, MEM_W = 4, 128, 512
D_FF = 2816
FF_CHUNK = 256
EPS = 1e-6
NEG = -1e30
LR, B1, B2, AEPS, WD, STEP = 0.001, 0.9, 0.999, 1e-08, 0.01, 10
GELU_K, GELU_C = 0.7978845608028654, 0.044715


def _dot(a, b):
    return jnp.dot(a, b, preferred_element_type=F32)


def _dot_nt(a, b):
    return lax.dot_general(a, b, (((1,), (1,)), ((), ())), preferred_element_type=F32)


def _dot_tn(a, b):
    return lax.dot_general(a, b, (((0,), (0,)), ((), ())), preferred_element_type=F32)


def _row(tm, w):
    return pl.BlockSpec((tm, w), lambda i: (i, 0))


def _acc(shape):
    return pl.BlockSpec(shape, lambda i: (0,) * len(shape))


def _params(sem, mb):
    return pltpu.CompilerParams(dimension_semantics=sem, vmem_limit_bytes=mb << 20)


def _sds(shape, dt):
    return jax.ShapeDtypeStruct(shape, dt)


def _rms(h):
    return lax.rsqrt(jnp.mean(h * h, axis=-1, keepdims=True) + EPS)


def _rms_bwd(dy, h, r, g):
    t = dy * g
    dh = r * t - h * (r * r * r) * jnp.mean(t * h, axis=-1, keepdims=True)
    return dh, dy * h * r


def _gelu(x):
    t = jnp.tanh(GELU_K * (x + GELU_C * x * x * x))
    return 0.5 * x * (1.0 + t), t


def _gelu_grad(x, t):
    return 0.5 * (1.0 + t) + 0.5 * x * (1.0 - t * t) * GELU_K * (1.0 + 3.0 * GELU_C * x * x)


def _alibi_slopes():
    def pow2(n):
        start = 2.0 ** (-8.0 / n)
        return [start ** (i + 1) for i in range(n)]
    n = 12
    c = 2 ** int(math.floor(math.log2(n)))
    s = pow2(c) + pow2(2 * c)[0::2][: n - c]
    return np.array(sorted(s, reverse=True), dtype=np.float32).reshape(3, 4)


def _attn_bias(g):
    win, dil = DIL_GROUPS[g]
    steps = (np.arange(BLK)[:, None] + BLK) - np.arange(2 * BLK)[None, :]
    valid = (steps >= 0) & (steps <= win // dil)
    dist = (np.clip(steps, 0, None) * dil).astype(np.float32)
    b = -_alibi_slopes()[g][:, None, None] * dist[None]
    return np.where(valid[None], b, NEG).astype(np.float32)


def _head_masks():
    lane = lax.broadcasted_iota(jnp.int32, (1, GROUP_W), 1)
    return lane, [(lane >= HEAD * h) & (lane < HEAD * (h + 1)) for h in range(4)]


ATT_NB = 8


def _stack_heads(t, masks):
    z = jnp.zeros_like(t)
    return jnp.concatenate([jnp.where(m, t, z) for m in masks], axis=0)


def _unstack_heads(t, masks):
    out = jnp.zeros((BLK, GROUP_W), t.dtype)
    for h, m in enumerate(masks):
        out = jnp.where(m, t[h * BLK:(h + 1) * BLK], out)
    return out


def _stack_cols(ref, rows):
    return jnp.concatenate([ref[rows, HEAD * h:HEAD * h + 1] for h in range(4)], axis=0)


def _dil_spec(d, tm, w):
    return pl.BlockSpec((d, tm // d, w), lambda i: (0, i, 0))


def _to_dilated(val, s_ref, d, write):
    tm, w = val.shape
    for j in range(w // 128):
        s_ref[j, pl.ds(0, tm), :] = val[:, j * 128:(j + 1) * 128]
    for r in range(d):
        for j in range(w // 128):
            write(r, j, s_ref[j, pl.ds(r, tm // d, stride=d), :])


def _from_dilated(ref, s_ref, d, tm, w):
    if d == 1:
        return ref[0].astype(F32)
    for r in range(d):
        for j in range(w // 128):
            s_ref[j, pl.ds(r, tm // d, stride=d), :] = ref[r, :, j * 128:(j + 1) * 128].astype(F32)
    return jnp.concatenate([s_ref[j, pl.ds(0, tm), :] for j in range(w // 128)], axis=1)


def _fwd_in(x, g_mix, wt_in, gather=None, tm=512):
    S = x.shape[0]
    dils = [d for _, d in DIL_GROUPS]
    n = 0 if gather is None else gather.n
    last = S // tm - 1

    def body(*refs):
        x_ref, g_ref, w_ref = refs[:3]
        a_ref, q0_ref, q1_ref, q2_ref, uv_ref, gl_ref = refs[3 + n:9 + n]
        s_ref = refs[9 + 2 * n]
        comm = (refs[3:3 + n], refs[9 + n:9 + 2 * n], refs[10 + 2 * n:])
        if gather is not None:
            pl.when(pl.program_id(0) == 0)(lambda: gather.start(*comm))
        xv = x_ref[...]
        a = (xv * _rms(xv) * g_ref[...]).astype(BF16)
        a_ref[...] = a
        for g, (d, out) in enumerate(zip(dils, (q0_ref, q1_ref, q2_ref))):
            for part in range(3):
                rows = part * 768 + g * 256
                val = _dot_nt(a, w_ref[rows:rows + 256, :])
                if d == 1:
                    out[0, :, part * 256:(part + 1) * 256] = val.astype(BF16)
                else:
                    def write(r, j, piece, out=out, part=part):
                        out[r, :, part * 256 + j * 128:part * 256 + (j + 1) * 128] = piece.astype(BF16)
                    _to_dilated(val, s_ref, d, write)
        uv_ref[...] = _dot_nt(a, w_ref[2304:3328, :]).astype(BF16)
        gl_ref[...] = _dot_nt(a, w_ref[3328:5376, :]).astype(BF16)
        if gather is not None:
            pl.when(pl.program_id(0) == last // 2)(lambda: gather.relay(*comm))
            pl.when(pl.program_id(0) == last)(lambda: gather.finish(*comm))

    outs = _pallas(
        body, grid=(S // tm,), name="fwd_in",
        in_specs=[_row(tm, D), _whole(g_mix), RES] + [ANY] * n,
        out_specs=[_row(tm, D)] + [_dil_spec(d, tm, 768) for d in dils] + [_row(tm, 1024), _row(tm, 2048)] + [ANY] * n,
        out_shape=[_sds((S, D), BF16)] + [_sds((d, S // d, 768), BF16) for d in dils]
        + [_sds((S, 1024), BF16), _sds((S, 2048), BF16)] + ([] if gather is None else gather.out_shape),
        scratch_shapes=[pltpu.VMEM((2, tm, 128), F32)] + ([] if gather is None else gather.scratch),
        compiler_params=_params(("arbitrary",), 60),
    )(x, g_mix, wt_in, *([] if gather is None else gather.halves))
    return outs[:6], ([] if gather is None else gather.full(outs[6:]))


def _attn_fwd(qkv, g):
    d, L, _ = qkv.shape
    nb = L // BLK
    bias = jnp.asarray(_attn_bias(g).reshape(4 * BLK, 2 * BLK))
    NB = min(ATT_NB, nb)
    W = NB * BLK

    def body(q_ref, kc_ref, kp_ref, vc_ref, vp_ref, b_ref, o_ref, l_ref):
        st = pl.program_id(1)
        k_all = jnp.concatenate([kp_ref[...], kc_ref[...]], axis=0)
        v_all = jnp.concatenate([vp_ref[...], vc_ref[...]], axis=0)
        lane, masks = _head_masks()
        for b in range(NB):
            rows = slice(b * BLK, (b + 1) * BLK)
            kk, vv = k_all[b * BLK:(b + 2) * BLK], v_all[b * BLK:(b + 2) * BLK]
            s = _dot_nt(_stack_heads(q_ref[rows, :], masks), kk) * 0.125 + b_ref[...]
            if b == 0:
                s = s + jnp.where((st == 0) & (lane < BLK), NEG, 0.0).astype(F32)
            mx = jnp.max(s, axis=-1, keepdims=True)
            e = jnp.exp(s - mx)
            den = jnp.sum(e, axis=-1, keepdims=True)
            o_ref[rows, :] = _unstack_heads(_dot(e.astype(BF16), vv) / den, masks)
            l_ref[rows, :] = _unstack_heads(mx + jnp.log(den), masks)

    def wide(col):
        return pl.BlockSpec((None, W, GROUP_W), lambda r, s: (r, s, col))

    def before(col):
        return pl.BlockSpec((None, BLK, GROUP_W), lambda r, s: (r, jnp.maximum(s * NB - 1, 0), col))

    return _pallas(
        body, grid=(d, nb // NB), name=f"attn_fwd_g{g}",
        in_specs=[wide(0), wide(1), before(1), wide(2), before(2),
                  pl.BlockSpec((4 * BLK, 2 * BLK), lambda r, s: (0, 0))],
        out_specs=[wide(0), wide(0)],
        out_shape=[_sds((d, L, GROUP_W), F32), _sds((d, L, GROUP_W), F32)],
        compiler_params=_params(("parallel", "parallel"), 32),
    )(qkv, qkv, qkv, qkv, qkv, bias)


def _group_weights(l0, l1, l2):
    m = jnp.maximum(jnp.maximum(l0, l1), l2)
    e0, e1, e2 = jnp.exp(l0 - m), jnp.exp(l1 - m), jnp.exp(l2 - m)
    inv = 1.0 / (e0 + e1 + e2)
    return e0 * inv, e1 * inv, e2 * inv


def _sgu_forward(uvf, gs, wt_ref, bst_ref, mixed_s, tm):
    z, t = _gelu(uvf)
    u, v = z[:, :SGU_W], z[:, SGU_W:]
    rv = _rms(v)
    vnb = (v * rv * gs).astype(BF16)
    for ci in range(tm // 128):
        for g in range(4):
            rs, cs = slice(ci * 128, (ci + 1) * 128), slice(g * 128, (g + 1) * 128)
            mixed_s[rs, cs] = _dot(wt_ref[g], vnb[rs, cs]) + bst_ref[:, g:g + 1]
    return u, v, rv, vnb, t


def _fwd_mid(x, os_, ls_, uv, gl, wt, bst, g_sgu, b_gate, wt_ba, wt_bs, w_out, gather=None, tm=512):
    S = x.shape[0]
    dils = [d for _, d in DIL_GROUPS]
    n = 0 if gather is None else gather.n
    last = S // tm - 1

    def body(*refs):
        (x_ref, o0, o1, o2, l0, l1, l2, uv_ref, gl_ref, wt_ref, bst_ref, gs_ref, bg_ref, wba_ref, wbs_ref,
         wo_ref) = refs[:16]
        ya_ref, ys_ref, ba_ref, bs_ref, mg_ref, h1_ref = refs[16 + n:22 + n]
        mixed_s, il_s = refs[22 + 2 * n:24 + 2 * n]
        comm = (refs[16:16 + n], refs[22 + n:22 + 2 * n], refs[24 + 2 * n:])
        if gather is not None:
            pl.when(pl.program_id(0) == 0)(lambda: gather.start(*comm))
        ls = [_from_dilated(r, il_s, d, tm, GROUP_W) for r, d in zip((l0, l1, l2), dils)]
        alphas = _group_weights(*ls)
        ya = jnp.zeros((tm, GROUP_W), F32)
        for a, r, d in zip(alphas, (o0, o1, o2), dils):
            ya = ya + a * _from_dilated(r, il_s, d, tm, GROUP_W)
        yab = ya.astype(BF16)
        ya_ref[...] = yab
        u, _, _, _, _ = _sgu_forward(uv_ref[...].astype(F32), gs_ref[...], wt_ref, bst_ref, mixed_s, tm)
        ysb = (u * mixed_s[...]).astype(BF16)
        ys_ref[...] = ysb
        gates = jax.nn.sigmoid(gl_ref[...].astype(F32) + bg_ref[...])
        ba = _dot_nt(yab, wba_ref[...])
        bs = _dot_nt(ysb, wbs_ref[...])
        ba_ref[...] = ba.astype(BF16)
        bs_ref[...] = bs.astype(BF16)
        mgb = (gates[:, :D] * ba + gates[:, D:] * bs).astype(BF16)
        mg_ref[...] = mgb
        h1_ref[...] = x_ref[...] + _dot(mgb, wo_ref[...])
        if gather is not None:
            pl.when(pl.program_id(0) == (3 * last) // 4)(lambda: gather.relay(*comm))
            pl.when(pl.program_id(0) == last)(lambda: gather.finish(*comm))

    gw = _row(tm, GROUP_W)
    dil = [_dil_spec(d, tm, GROUP_W) for d in dils]
    outs = _pallas(
        body, grid=(S // tm,), name="fwd_mid",
        in_specs=[_row(tm, D)] + dil + dil + [_row(tm, 1024), _row(tm, 2048)]
        + [_whole(t) for t in (wt, bst, g_sgu, b_gate)] + [RES] * 3 + [ANY] * n,
        out_specs=[gw, _row(tm, SGU_W), _row(tm, D), _row(tm, D), _row(tm, D), _row(tm, D)] + [ANY] * n,
        out_shape=[_sds((S, GROUP_W), BF16), _sds((S, SGU_W), BF16), _sds((S, D), BF16), _sds((S, D), BF16),
                   _sds((S, D), BF16), _sds((S, D), F32)] + ([] if gather is None else gather.out_shape),
        scratch_shapes=[pltpu.VMEM((tm, SGU_W), F32), pltpu.VMEM((2, tm, 128), F32)]
        + ([] if gather is None else gather.scratch),
        compiler_params=_params(("arbitrary",), 56),
    )(x, *os_, *ls_, uv, gl, wt, bst, g_sgu, b_gate, wt_ba, wt_bs, w_out, *([] if gather is None else gather.halves))
    return outs[:6], ([] if gather is None else gather.full(outs[6:]))


def _mem_fwd(mem, g_mem, w_kv):
    def body(m_ref, g_ref, w_ref, mb_ref, kv_ref):
        mv = m_ref[...]
        mb = (mv * _rms(mv) * g_ref[...]).astype(BF16)
        mb_ref[...] = mb
        kv_ref[...] = _dot(mb, w_ref[...]).astype(BF16)

    shapes = [_sds(mem.shape, BF16), _sds((mem.shape[0], 2 * MEM_W), BF16)]
    return _pallas(
        body, name="mem_fwd", grid=(1,), in_specs=[_whole(t) for t in (mem, g_mem, w_kv)],
        out_specs=[_whole(t) for t in shapes], out_shape=shapes, compiler_params=_params(("arbitrary",), 32),
    )(mem, g_mem, w_kv)


def _cross_probs(qh, kh):
    s = _dot_nt(qh, kh) * (MEM_HD ** -0.5)
    e = jnp.exp(s - jnp.max(s, axis=-1, keepdims=True))
    return e / jnp.sum(e, axis=-1, keepdims=True)


def _fwd_cross(h1, g_cross, w_q, kv, wt_o, gather=None, tm=512):
    S = h1.shape[0]
    n = 0 if gather is None else gather.n
    last = S // tm - 1

    def body(*refs):
        h_ref, g_ref, wq_ref, kv_ref, wo_ref = refs[:5]
        c_ref, qc_ref, oc_ref, h2_ref = refs[5 + n:9 + n]
        comm = (refs[5:5 + n], refs[9 + n:9 + 2 * n], refs[9 + 2 * n:])
        if gather is not None:
            pl.when(pl.program_id(0) == 0)(lambda: gather.start(*comm))
        hv = h_ref[...]
        cb = (hv * _rms(hv) * g_ref[...]).astype(BF16)
        c_ref[...] = cb
        qcb = _dot(cb, wq_ref[...]).astype(BF16)
        qc_ref[...] = qcb
        for h in range(MEM_HEADS):
            cs = slice(h * MEM_HD, (h + 1) * MEM_HD)
            p = _cross_probs(qcb[:, cs], kv_ref[:, cs])
            oc_ref[:, cs] = _dot(p.astype(BF16), kv_ref[:, MEM_W + h * MEM_HD:MEM_W + (h + 1) * MEM_HD]).astype(BF16)
        h2_ref[...] = hv + _dot_nt(oc_ref[...], wo_ref[...])
        if gather is not None:
            pl.when(pl.program_id(0) == (3 * last) // 4)(lambda: gather.relay(*comm))
            pl.when(pl.program_id(0) == last)(lambda: gather.finish(*comm))

    outs = _pallas(
        body, grid=(S // tm,), name="fwd_cross",
        in_specs=[_row(tm, D), _whole(g_cross), RES, _whole(kv), RES] + [ANY] * n,
        out_specs=[_row(tm, D), _row(tm, MEM_W), _row(tm, MEM_W), _row(tm, D)] + [ANY] * n,
        out_shape=[_sds((S, D), BF16), _sds((S, MEM_W), BF16), _sds((S, MEM_W), BF16), _sds((S, D), F32)]
        + ([] if gather is None else gather.out_shape),
        scratch_shapes=[] if gather is None else gather.scratch,
        compiler_params=_params(("arbitrary",), 40),
    )(h1, g_cross, w_q, kv, wt_o, *([] if gather is None else gather.halves))
    return outs[:4], ([] if gather is None else gather.full(outs[4:]))


def _ffn_fwd_bwd(h2, target, g_ffn, g_final, wt_gu, w_down, tm=256):
    S = h2.shape[0]
    nch = D_FF // FF_CHUNK

    def body(h_ref, t_ref, gf_ref, gz_ref, wgu_ref, wd_ref,
             f_ref, act_ref, dgu_ref, dh3b_ref, dh2_ref, dh2b_ref, dgf_ref, dgz_ref, loss_ref, gu_s, dact_s):
        i = pl.program_id(0)

        @pl.when(i == 0)
        def _():
            dgf_ref[...] = jnp.zeros_like(dgf_ref)
            dgz_ref[...] = jnp.zeros_like(dgz_ref)
            loss_ref[...] = jnp.zeros_like(loss_ref)

        hv = h_ref[...]
        r2 = _rms(hv)
        gf = gf_ref[...]
        fb = (hv * r2 * gf).astype(BF16)
        f_ref[...] = fb
        gu_s[...] = _dot_nt(fb, wgu_ref[...])
        for c in range(nch):
            cs = slice(c * FF_CHUNK, (c + 1) * FF_CHUNK)
            us = slice(D_FF + c * FF_CHUNK, D_FF + (c + 1) * FF_CHUNK)
            gt = gu_s[:, cs]
            act_ref[:, cs] = (gt * jax.nn.sigmoid(gt) * gu_s[:, us]).astype(BF16)
        h3 = hv + _dot(act_ref[...], wd_ref[...])
        r3 = _rms(h3)
        gz = gz_ref[...]
        diff = h3 * r3 * gz - t_ref[...]
        dy = diff * (1.0 / D)
        dh3, dgz_rows = _rms_bwd(dy, h3, r3, gz)
        dh3b = dh3.astype(BF16)
        dh3b_ref[...] = dh3b
        dact_s[...] = _dot_nt(dh3b, wd_ref[...])
        for c in range(nch):
            cs = slice(c * FF_CHUNK, (c + 1) * FF_CHUNK)
            us = slice(D_FF + c * FF_CHUNK, D_FF + (c + 1) * FF_CHUNK)
            dact, gt, up = dact_s[:, cs], gu_s[:, cs], gu_s[:, us]
            sg = jax.nn.sigmoid(gt)
            dgu_ref[:, cs] = (dact * up * (sg * (1.0 + gt * (1.0 - sg)))).astype(BF16)
            dgu_ref[:, us] = (dact * (gt * sg)).astype(BF16)
        df = _dot(dgu_ref[...], wgu_ref[...])
        dhn, dgf_rows = _rms_bwd(df, hv, r2, gf)
        dh2 = dh3 + dhn
        dh2_ref[...] = dh2
        dh2b_ref[...] = dh2.astype(BF16)
        dgf_ref[...] += jnp.sum(dgf_rows, axis=0, keepdims=True)
        dgz_ref[...] += jnp.sum(dgz_rows, axis=0, keepdims=True)
        loss_ref[...] += jnp.sum(jnp.sum(diff * diff, axis=0, keepdims=True), axis=1, keepdims=True) * (0.5 / D)

    return _pallas(
        body, grid=(S // tm,), name="ffn_fwd_bwd",
        in_specs=[_row(tm, D), _row(tm, D), _whole(g_ffn), _whole(g_final), RES, RES],
        out_specs=[_row(tm, D), _row(tm, D_FF), _row(tm, 2 * D_FF), _row(tm, D), _row(tm, D), _row(tm, D),
                   _acc((1, D)), _acc((1, D)), _acc((1, 128))],
        out_shape=[_sds((S, D), BF16), _sds((S, D_FF), BF16), _sds((S, 2 * D_FF), BF16), _sds((S, D), BF16),
                   _sds((S, D), F32), _sds((S, D), BF16), _sds((1, D), F32), _sds((1, D), F32), _sds((1, 128), F32)],
        scratch_shapes=[pltpu.VMEM((tm, 2 * D_FF), F32), pltpu.VMEM((tm, D_FF), F32)],
        compiler_params=_params(("arbitrary",), 60),
    )(h2, target, g_ffn, g_final, wt_gu, w_down)


def _bwd_cross(dh2, h1, qc, g_cross, w_q, kv, wt_o, comm=None, tm=512):
    S = h1.shape[0]
    n = 0 if comm is None else comm.n
    last = S // tm - 1

    def body(*refs):
        d_ref, h_ref, qc_ref, g_ref, wq_ref, kv_ref, wo_ref = refs[:7]
        dqc_ref, dh1_ref, dh1b_ref, dkv_ref, dg_ref = refs[7 + n:12 + n]
        cargs = (refs[7:7 + n], refs[12 + n:12 + 2 * n], refs[12 + 2 * n:])
        i = pl.program_id(0)

        @pl.when(i == 0)
        def _():
            dkv_ref[...] = jnp.zeros_like(dkv_ref)
            dg_ref[...] = jnp.zeros_like(dg_ref)
            if comm is not None:
                comm.start(*cargs)

        dh2 = d_ref[...]
        doc = _dot(dh2.astype(BF16), wo_ref[...])
        qcb = qc_ref[...]
        for h in range(MEM_HEADS):
            cs = slice(h * MEM_HD, (h + 1) * MEM_HD)
            vs = slice(MEM_W + h * MEM_HD, MEM_W + (h + 1) * MEM_HD)
            qh, kh, vh = qcb[:, cs], kv_ref[:, cs], kv_ref[:, vs]
            p = _cross_probs(qh, kh)
            dohb = doc[:, cs].astype(BF16)
            dp = _dot_nt(dohb, vh)
            dsb = (p * (dp - jnp.sum(dp * p, axis=-1, keepdims=True)) * (MEM_HD ** -0.5)).astype(BF16)
            dqc_ref[:, cs] = _dot(dsb, kh).astype(BF16)
            dkv_ref[:, cs] += _dot_tn(dsb, qh)
            dkv_ref[:, vs] += _dot_tn(p.astype(BF16), dohb)
        dc = _dot_nt(dqc_ref[...], wq_ref[...])
        hv = h_ref[...]
        dhn, dg_rows = _rms_bwd(dc, hv, _rms(hv), g_ref[...])
        dh1 = dh2 + dhn
        dh1_ref[...] = dh1
        dh1b_ref[...] = dh1.astype(BF16)
        dg_ref[...] += jnp.sum(dg_rows, axis=0, keepdims=True)
        if comm is not None:
            pl.when(i == last)(lambda: comm.finish(*cargs))

    outs = _pallas(
        body, grid=(S // tm,), name="bwd_cross",
        in_specs=[_row(tm, D), _row(tm, D), _row(tm, MEM_W), _whole(g_cross), RES, _whole(kv), RES] + [ANY] * n,
        out_specs=[_row(tm, MEM_W), _row(tm, D), _row(tm, D), _acc((256, 2 * MEM_W)), _acc((1, D))] + [ANY] * n,
        out_shape=[_sds((S, MEM_W), BF16), _sds((S, D), F32), _sds((S, D), BF16), _sds((256, 2 * MEM_W), F32),
                   _sds((1, D), F32)] + ([] if comm is None else comm.out_shape),
        scratch_shapes=[] if comm is None else comm.scratch,
        compiler_params=_params(("arbitrary",), 48),
    )(dh2, h1, qc, g_cross, w_q, kv, wt_o, *([] if comm is None else comm.ins))
    return outs[:5], outs[5:]


def _mem_bwd(dkv, mem, mb, g_mem, w_kv):
    def body(dkv_ref, m_ref, mb_ref, g_ref, w_ref, dw_ref, dwb_ref, dg_ref):
        dkvb = dkv_ref[...].astype(BF16)
        dw = _dot_tn(mb_ref[...], dkvb)
        dw_ref[...] = dw
        dwb_ref[...] = dw.astype(BF16)
        dm = _dot_nt(dkvb, w_ref[...])
        mv = m_ref[...]
        dg_ref[...] = jnp.sum(dm * mv * _rms(mv), axis=0, keepdims=True)

    shapes = [_sds((D, 2 * MEM_W), F32), _sds((D, 2 * MEM_W), BF16), _sds((1, D), F32)]
    return _pallas(
        body, name="mem_bwd", grid=(1,), in_specs=[_whole(t) for t in (dkv, mem, mb, g_mem, w_kv)],
        out_specs=[_whole(t) for t in shapes], out_shape=shapes, compiler_params=_params(("arbitrary",), 40),
    )(dkv, mem, mb, g_mem, w_kv)


def _bwd_mid(dh1, gl, ba, bs, uv, ls_, ya, wt, bst, g_sgu, b_gate, wt_ba, wt_bs, w_out, tm=512):
    S = dh1.shape[0]
    dils = [d for _, d in DIL_GROUPS]

    def body(d_ref, gl_ref, ba_ref, bs_ref, uv_ref, l0, l1, l2, ya_ref,
             wt_ref, bst_ref, gs_ref, bg_ref, wba_ref, wbs_ref, wo_ref,
             dba_ref, dbs_ref, dgl_ref, duv_ref, do0, do1, do2, c0, c1, c2,
             dbg_ref, dgs_ref, dws_ref, dbsa_ref, mixed_s, dvn_s, il_s):
        i = pl.program_id(0)

        @pl.when(i == 0)
        def _():
            for r in (dbg_ref, dgs_ref, dws_ref, dbsa_ref):
                r[...] = jnp.zeros_like(r)

        dm = _dot_nt(d_ref[...].astype(BF16), wo_ref[...])
        gates = jax.nn.sigmoid(gl_ref[...].astype(F32) + bg_ref[...])
        g0, g1 = gates[:, :D], gates[:, D:]
        dbab = (dm * g0).astype(BF16)
        dbsb = (dm * g1).astype(BF16)
        dba_ref[...] = dbab
        dbs_ref[...] = dbsb
        dg0 = dm * ba_ref[...].astype(F32) * g0 * (1.0 - g0)
        dg1 = dm * bs_ref[...].astype(F32) * g1 * (1.0 - g1)
        dgl_ref[:, :D] = dg0.astype(BF16)
        dgl_ref[:, D:] = dg1.astype(BF16)
        dbg_ref[:, :D] += jnp.sum(dg0, axis=0, keepdims=True)
        dbg_ref[:, D:] += jnp.sum(dg1, axis=0, keepdims=True)
        dya = _dot(dbab, wba_ref[...])
        dys = _dot(dbsb, wbs_ref[...])

        uvf = uv_ref[...].astype(F32)
        gs = gs_ref[...]
        u, v, rv, vnb, t = _sgu_forward(uvf, gs, wt_ref, bst_ref, mixed_s, tm)
        du = dys * mixed_s[...]
        dmixed = dys * u
        for ci in range(tm // 128):
            for g in range(4):
                rs, cs = slice(ci * 128, (ci + 1) * 128), slice(g * 128, (g + 1) * 128)
                dmx = dmixed[rs, cs]
                dmxb = dmx.astype(BF16)
                dvn_s[rs, cs] = _dot_tn(wt_ref[g], dmxb)
                dws_ref[g] += _dot_nt(dmxb, vnb[rs, cs])
                dbsa_ref[g] += dmx
        dv, dgs_rows = _rms_bwd(dvn_s[...], v, rv, gs)
        dgs_ref[...] += jnp.sum(dgs_rows, axis=0, keepdims=True)
        gg = _gelu_grad(uvf, t)
        duv_ref[:, :SGU_W] = (du * gg[:, :SGU_W]).astype(BF16)
        duv_ref[:, SGU_W:] = (dv * gg[:, SGU_W:]).astype(BF16)

        alphas = _group_weights(*[_from_dilated(r, il_s, d, tm, GROUP_W) for r, d in zip((l0, l1, l2), dils)])
        prod = dya * ya_ref[...].astype(F32)
        _, masks = _head_masks()
        hs = jnp.zeros_like(prod)
        for h in range(4):
            sh = jnp.sum(jnp.where(masks[h], prod, 0.0), axis=-1, keepdims=True)
            hs = jnp.where(masks[h], sh, hs)
        for a, d, do_ref, c_ref in zip(alphas, dils, (do0, do1, do2), (c0, c1, c2)):
            for val, out in ((a * dya, do_ref), (a * hs, c_ref)):
                if d == 1:
                    out[0] = val.astype(out.dtype)
                else:
                    def write(r, j, piece, out=out):
                        out[r, :, j * 128:(j + 1) * 128] = piece.astype(out.dtype)
                    _to_dilated(val, il_s, d, write)

    gw = _row(tm, GROUP_W)
    dil = [_dil_spec(d, tm, GROUP_W) for d in dils]
    return _pallas(
        body, grid=(S // tm,), name="bwd_mid",
        in_specs=[_row(tm, D), _row(tm, 2048), _row(tm, D), _row(tm, D), _row(tm, 1024)] + dil + [gw]
        + [_whole(t) for t in (wt, bst, g_sgu, b_gate)] + [RES] * 3,
        out_specs=[_row(tm, D), _row(tm, D), _row(tm, 2048), _row(tm, 1024)] + dil + dil
        + [_acc((1, 2048)), _acc((1, SGU_W)), _acc((4, 128, 128)), _acc((4, 128, 128))],
        out_shape=[_sds((S, D), BF16), _sds((S, D), BF16), _sds((S, 2048), BF16), _sds((S, 1024), BF16)]
        + [_sds((d, S // d, GROUP_W), BF16) for d in dils] + [_sds((d, S // d, GROUP_W), F32) for d in dils]
        + [_sds((1, 2048), F32), _sds((1, SGU_W), F32), _sds((4, 128, 128), F32), _sds((4, 128, 128), F32)],
        scratch_shapes=[pltpu.VMEM((tm, SGU_W), F32), pltpu.VMEM((tm, SGU_W), F32), pltpu.VMEM((2, tm, 128), F32)],
        compiler_params=_params(("arbitrary",), 60),
    )(dh1, gl, ba, bs, uv, *ls_, ya, wt, bst, g_sgu, b_gate, wt_ba, wt_bs, w_out)


def _attn_bwd(qkv, do, lse, corr, g):
    d, L, _ = qkv.shape
    nb = L // BLK
    NB = min(ATT_NB, nb)
    W = NB * BLK
    nsteps = nb // NB
    bias = jnp.asarray(_attn_bias(g).reshape(4 * BLK, 2 * BLK))

    def body(q_ref, kc_ref, kp_ref, vc_ref, vp_ref, do_ref, l_ref, c_ref, qn_ref, don_ref, ln_ref, cn_ref, b_ref,
             out_ref, dk_s, dv_s):
        st = pl.program_id(1)
        k_all = jnp.concatenate([kp_ref[...], kc_ref[...]], axis=0)
        v_all = jnp.concatenate([vp_ref[...], vc_ref[...]], axis=0)
        lane, masks = _head_masks()
        dk_s[...] = jnp.zeros_like(dk_s)
        dv_s[...] = jnp.zeros_like(dv_s)

        def block_terms(qs, dos, kk, vv, bias_v, lse_c, corr_c):
            s = _dot_nt(qs, kk) * 0.125 + bias_v
            p = jnp.exp(s - lse_c)
            dsb = (p * (_dot_nt(dos, vv) - corr_c) * 0.125).astype(BF16)
            return dsb, p.astype(BF16)

        for b in range(NB):
            rows = slice(b * BLK, (b + 1) * BLK)
            keys = slice(b * BLK, (b + 2) * BLK)
            kk, vv = k_all[keys], v_all[keys]
            qs, dos = _stack_heads(q_ref[rows, :], masks), _stack_heads(do_ref[rows, :], masks)
            bias_v = b_ref[...]
            if b == 0:
                bias_v = bias_v + jnp.where((st == 0) & (lane < BLK), NEG, 0.0).astype(F32)
            dsb, pb = block_terms(qs, dos, kk, vv, bias_v, _stack_cols(l_ref, rows), _stack_cols(c_ref, rows))
            out_ref[rows, 0:GROUP_W] = _unstack_heads(_dot(dsb, kk), masks).astype(BF16)
            dk_s[keys, :] += _dot_tn(dsb, qs)
            dv_s[keys, :] += _dot_tn(pb, dos)

        @pl.when(st < nsteps - 1)
        def _():
            last = slice(NB * BLK, (NB + 1) * BLK)
            qs, dos = _stack_heads(qn_ref[...], masks), _stack_heads(don_ref[...], masks)
            every = slice(None)
            dsb, pb = block_terms(qs, dos, k_all[last], v_all[last], b_ref[:, :BLK],
                                  _stack_cols(ln_ref, every), _stack_cols(cn_ref, every))
            dk_s[last, :] += _dot_tn(dsb, qs)
            dv_s[last, :] += _dot_tn(pb, dos)

        out_ref[:, GROUP_W:2 * GROUP_W] = dk_s[BLK:, :].astype(BF16)
        out_ref[:, 2 * GROUP_W:] = dv_s[BLK:, :].astype(BF16)

    def wide(col, w=GROUP_W):
        return pl.BlockSpec((None, W, w), lambda r, s: (r, s, col))

    def before(col):
        return pl.BlockSpec((None, BLK, GROUP_W), lambda r, s: (r, jnp.maximum(s * NB - 1, 0), col))

    def after(col):
        return pl.BlockSpec((None, BLK, GROUP_W), lambda r, s: (r, jnp.minimum((s + 1) * NB, nb - 1), col))

    return _pallas(
        body, grid=(d, nsteps), name=f"attn_bwd_g{g}",
        in_specs=[wide(0), wide(1), before(1), wide(2), before(2), wide(0), wide(0), wide(0),
                  after(0), after(0), after(0), after(0), pl.BlockSpec((4 * BLK, 2 * BLK), lambda r, s: (0, 0))],
        out_specs=wide(0, 768),
        out_shape=_sds((d, L, 768), BF16),
        scratch_shapes=[pltpu.VMEM(((NB + 1) * BLK, GROUP_W), F32), pltpu.VMEM(((NB + 1) * BLK, GROUP_W), F32)],
        compiler_params=_params(("parallel", "parallel"), 32),
    )(qkv, qkv, qkv, qkv, qkv, do, lse, corr, qkv, do, lse, corr, bias)


def _bwd_in(dqkvs, duv, dgl, dh1, x, g_mix, wt_in, tm=512):
    S = x.shape[0]
    dils = [d for _, d in DIL_GROUPS]

    def body(q0_ref, q1_ref, q2_ref, duv_ref, dgl_ref, d_ref, x_ref, g_ref, w_ref, dx_ref, dp_ref, dg_ref, il_s):
        i = pl.program_id(0)

        @pl.when(i == 0)
        def _():
            dg_ref[...] = jnp.zeros_like(dg_ref)

        for g, (d, ref) in enumerate(zip(dils, (q0_ref, q1_ref, q2_ref))):
            nat = _from_dilated(ref, il_s, d, tm, 768).astype(BF16)
            for part in range(3):
                col = part * 768 + g * 256
                dp_ref[:, col:col + 256] = nat[:, part * 256:(part + 1) * 256]
        dp_ref[:, 2304:3328] = duv_ref[...]
        dp_ref[:, 3328:5376] = dgl_ref[...]
        da = _dot(dp_ref[...], w_ref[...])
        xv = x_ref[...]
        dxn, dg_rows = _rms_bwd(da, xv, _rms(xv), g_ref[...])
        dx_ref[...] = d_ref[...] + dxn
        dg_ref[...] += jnp.sum(dg_rows, axis=0, keepdims=True)

    return _pallas(
        body, grid=(S // tm,), name="bwd_in",
        in_specs=[_dil_spec(d, tm, 768) for d in dils] + [_row(tm, 1024), _row(tm, 2048), _row(tm, D), _row(tm, D),
                                                          _whole(g_mix), RES],
        out_specs=[_row(tm, D), _row(tm, 5376), _acc((1, D))],
        out_shape=[_sds((S, D), F32), _sds((S, 5376), BF16), _sds((1, D), F32)],
        scratch_shapes=[pltpu.VMEM((6, tm, 128), F32)],
        compiler_params=_params(("arbitrary",), 60),
    )(*dqkvs, duv, dgl, dh1, x, g_mix, wt_in)


def _tn_matmul(a, b, name, tk, ts=2048, comm=None):
    S, K = a.shape
    N = b.shape[1]
    n = 0 if comm is None else comm.n
    nk, ns = K // tk, S // ts

    def body(*refs):
        a_ref, b_ref, o_ref, ob_ref = refs[0], refs[1], refs[2 + n], refs[3 + n]
        cargs = (refs[2:2 + n], refs[4 + n:4 + 2 * n], refs[4 + 2 * n:])
        k, s = pl.program_id(0), pl.program_id(1)
        if comm is not None:
            pl.when((k == 0) & (s == 0))(lambda: comm.start(*cargs))

        @pl.when(s == 0)
        def _():
            o_ref[...] = jnp.zeros_like(o_ref)

        o_ref[...] += _dot_tn(a_ref[...], b_ref[...])

        @pl.when(s == ns - 1)
        def _():
            ob_ref[...] = o_ref[...].astype(BF16)

        if comm is not None:
            pl.when((k == nk - 1) & (s == ns - 1))(lambda: comm.finish(*cargs))

    tile = pl.BlockSpec((tk, N), lambda k, s: (k, 0))
    outs = _pallas(
        body, grid=(nk, ns), name=name,
        in_specs=[pl.BlockSpec((ts, tk), lambda k, s: (s, k)), pl.BlockSpec((ts, N), lambda k, s: (s, 0))] + [ANY] * n,
        out_specs=[tile, tile] + [ANY] * n,
        out_shape=[_sds((K, N), F32), _sds((K, N), BF16)] + ([] if comm is None else comm.out_shape),
        scratch_shapes=[] if comm is None else comm.scratch,
        compiler_params=_params(("arbitrary", "arbitrary"), 56),
    )(a, b, *([] if comm is None else comm.ins))
    pair = (outs[0], outs[1])
    return pair if comm is None else (pair, outs[2:])


def _chip_peers(x, y):
    return [(1 - x, y), (x, 1 - y), (1 - x, 1 - y)]


STAGE_BYTES = 2 << 20


def _chunk_plan(shapes, itemsize):
    plan = []
    for i, (rows, w) in enumerate(shapes):
        ch = max(16, min(rows, (STAGE_BYTES // (w * itemsize)) // 16 * 16))
        while rows % ch:
            ch -= 16
        plan += [(i, r0, ch) for r0 in range(0, rows, ch)]
    return plan


def _remote(src, dst, ssem, rsem, dev):
    return pltpu.make_async_remote_copy(src_ref=src, dst_ref=dst, send_sem=ssem, recv_sem=rsem, device_id=dev,
                                        device_id_type=MESH)


class _Gather:
    def __init__(self, shards):
        self.n = len(shards)
        self.shards = shards
        self.halves = [s.reshape(2, s.shape[0] // 2, s.shape[1]) for s in shards]
        self.plan = _chunk_plan([h.shape[1:] for h in self.halves], 2)
        self.out_shape = [_sds((4,) + h.shape, BF16) for h in self.halves]
        n = self.n
        self.scratch = [pltpu.SemaphoreType.DMA((6 * n,)), pltpu.SemaphoreType.DMA((6 * n,)),
                        pltpu.SemaphoreType.DMA((2,)), pltpu.SemaphoreType.DMA((2,)),
                        pltpu.VMEM((2, max(p[2] for p in self.plan), max(h.shape[2] for h in self.halves)), BF16)]

    def full(self, outs):
        return [o.reshape(4 * s.shape[0], s.shape[1]) for o, s in zip(outs, self.shards)]

    def _sends(self, ins, outs, ssem, rsem):
        x, y, c = lax.axis_index("x"), lax.axis_index("y"), lax.axis_index("c")
        me = 2 * x + y
        return [_remote(ins[i].at[c], outs[i].at[me, c], ssem.at[6 * i + k], rsem.at[6 * i + k], (px, py, c))
                for i in range(self.n) for k, (px, py) in enumerate(_chip_peers(x, y))]

    def start(self, ins, outs, scratch):
        ssem, rsem, lsem, osem, buf = scratch
        me = 2 * lax.axis_index("x") + lax.axis_index("y")
        for cp in self._sends(ins, outs, ssem, rsem):
            cp.start()
        pending = {}
        for i, r0, ch in self.plan:
            for h in range(2):
                if h in pending:
                    pending[h].wait()
                stage = buf.at[h, pl.ds(0, ch), pl.ds(0, self.halves[i].shape[2])]
                ld = pltpu.make_async_copy(ins[i].at[h, pl.ds(r0, ch)], stage, lsem.at[h])
                ld.start()
                ld.wait()
                st = pltpu.make_async_copy(stage, outs[i].at[me, h, pl.ds(r0, ch)], osem.at[h])
                st.start()
                pending[h] = st
        for st in pending.values():
            st.wait()

    def _forwards(self, outs, ssem, rsem):
        x, y, c = lax.axis_index("x"), lax.axis_index("y"), lax.axis_index("c")
        cps = []
        for i in range(self.n):
            for k, (px, py) in enumerate(_chip_peers(x, y)):
                landed = outs[i].at[2 * px + py, c]
                cps.append(_remote(landed, landed, ssem.at[6 * i + 3 + k], rsem.at[6 * i + 3 + k], (x, y, 1 - c)))
        return cps

    def relay(self, ins, outs, scratch):
        ssem, rsem = scratch[:2]
        x, y, c = lax.axis_index("x"), lax.axis_index("y"), lax.axis_index("c")
        forwards = self._forwards(outs, ssem, rsem)
        for i in range(self.n):
            for k, (px, py) in enumerate(_chip_peers(x, y)):
                landed = outs[i].at[2 * px + py, c]
                _remote(landed, landed, ssem.at[6 * i + k], rsem.at[6 * i + k], (px, py, c)).wait_recv()
                forwards[3 * i + k].start()

    def finish(self, ins, outs, scratch):
        ssem, rsem = scratch[:2]
        x, y, c = lax.axis_index("x"), lax.axis_index("y"), lax.axis_index("c")
        sib = (x, y, 1 - c)
        for i in range(self.n):
            for k, (px, py) in enumerate(_chip_peers(x, y)):
                passed = outs[i].at[2 * px + py, 1 - c]
                _remote(passed, passed, ssem.at[6 * i + 3 + k], rsem.at[6 * i + 3 + k], sib).wait_recv()
        for cp in self._sends(ins, outs, ssem, rsem) + self._forwards(outs, ssem, rsem):
            cp.wait_send()


def _gather_weights(shards):
    gt = _Gather(shards)
    n = gt.n

    def body(*refs):
        ins, outs, scratch = refs[:n], refs[n:2 * n], refs[2 * n:]
        gt.start(ins, outs, scratch)
        gt.relay(ins, outs, scratch)
        gt.finish(ins, outs, scratch)

    outs = _pallas(
        body, name="gather_weights", in_specs=[ANY] * n, out_specs=[ANY] * n, out_shape=gt.out_shape,
        scratch_shapes=gt.scratch, compiler_params=pltpu.CompilerParams(vmem_limit_bytes=32 << 20),
    )(*gt.halves)
    return gt.full(outs)


def _swap_halves(grads):
    n = len(grads)
    view = lambda g: g.reshape(4, 2, g.shape[0] // 8, g.shape[1])
    g4f = [view(g) for g, _ in grads]
    g4 = [view(gb) for _, gb in grads]

    def body(*refs):
        ins, got = refs[:n], refs[n:2 * n]
        ssem, rsem = refs[2 * n:]
        x, y, c = lax.axis_index("x"), lax.axis_index("y"), lax.axis_index("c")
        sib = (x, y, 1 - c)
        cps = []
        for i in range(n):
            rc = _remote(ins[i].at[:, 1 - c], got[i], ssem.at[i], rsem.at[i], sib)
            rc.start()
            cps.append(rc)
        for cp in cps:
            cp.wait()

    half = [_sds((4, g.shape[2], g.shape[3]), BF16) for g in g4]
    got = _pallas(
        body, name="swap_halves", in_specs=[ANY] * n, out_specs=[ANY] * n, out_shape=half,
        scratch_shapes=[pltpu.SemaphoreType.DMA((n,)), pltpu.SemaphoreType.DMA((n,))],
    )(*g4)
    return g4f, got


def _chip_sum(g4, got, name):
    _, _, R, W = g4.shape
    tr = _tile(R, max(16, min(512, (1 << 18) // W // 16 * 16)))
    c = lax.axis_index("c").astype(jnp.int32).reshape(1)

    def body(c_ref, a_ref, b_ref, s_ref, sb_ref):
        s = a_ref[...] + b_ref[...].astype(F32)
        s_ref[...] = s
        sb_ref[...] = s.astype(BF16)

    plain = pl.BlockSpec((None, tr, W), lambda j, t, c_ref: (j, t, 0))
    return _pallas(
        body, name=name,
        grid_spec=pltpu.PrefetchScalarGridSpec(
            num_scalar_prefetch=1, grid=(4, R // tr),
            in_specs=[pl.BlockSpec((None, None, tr, W), lambda j, t, c_ref: (j, c_ref[0], t, 0)), plain],
            out_specs=[plain, plain]),
        out_shape=[_sds((4, R, W), F32), _sds((4, R, W), BF16)],
        compiler_params=_params(("parallel", "parallel"), 32),
    )(c, g4, got)


class _Scatter:
    def __init__(self, sums_b):
        self.n = len(sums_b)
        self.ins = list(sums_b)
        self.out_shape = [_sds((3,) + s.shape[1:], BF16) for s in sums_b]
        self.scratch = [pltpu.SemaphoreType.DMA((3 * self.n,)), pltpu.SemaphoreType.DMA((3 * self.n,))]

    def _copies(self, ins, outs, scratch):
        ssem, rsem = scratch
        x, y, c = lax.axis_index("x"), lax.axis_index("y"), lax.axis_index("c")
        return [_remote(ins[i].at[2 * px + py], outs[i].at[k], ssem.at[3 * i + k], rsem.at[3 * i + k], (px, py, c))
                for i in range(self.n) for k, (px, py) in enumerate(_chip_peers(x, y))]

    def start(self, ins, outs, scratch):
        for cp in self._copies(ins, outs, scratch):
            cp.start()

    def finish(self, ins, outs, scratch):
        for cp in self._copies(ins, outs, scratch):
            cp.wait()


class _Reduce:
    def __init__(self, grads, names):
        self.names = names
        g4, got = _swap_halves(grads)
        self.sums, sums_b = [], []
        for nm, g, t in zip(names, g4, got):
            s_, sb_ = _chip_sum(g, t, f"chip_sum_{nm}")
            self.sums.append(s_)
            sums_b.append(sb_)
        self.scatter = _Scatter(sums_b)

    def collect(self, parts):
        return [_mesh_sum(s, p, f"mesh_sum_{nm}") for nm, s, p in zip(self.names, self.sums, parts)]


def _mesh_sum(sums, parts, name):
    _, R, W = sums.shape
    tr = _tile(R, max(16, min(512, (1 << 18) // W // 16 * 16)))
    me = (2 * lax.axis_index("x") + lax.axis_index("y")).astype(jnp.int32).reshape(1)

    def body(me_ref, m_ref, p_ref, o_ref):
        o_ref[...] = m_ref[...] + p_ref[0].astype(F32) + p_ref[1].astype(F32) + p_ref[2].astype(F32)

    return _pallas(
        body, name=name,
        grid_spec=pltpu.PrefetchScalarGridSpec(
            num_scalar_prefetch=1, grid=(R // tr,),
            in_specs=[pl.BlockSpec((None, tr, W), lambda i, me_ref: (me_ref[0], i, 0)),
                      pl.BlockSpec((3, tr, W), lambda i, me_ref: (0, i, 0))],
            out_specs=pl.BlockSpec((tr, W), lambda i, me_ref: (i, 0))),
        out_shape=_sds((R, W), F32), compiler_params=_params(("parallel",), 32),
    )(me, sums, parts)


def _share_halves(reduced):
    n = len(reduced)
    plan = _chunk_plan([r.shape for r in reduced], 4)
    max_rows = max(p[2] for p in plan)
    max_w = max(r.shape[1] for r in reduced)

    def body(*refs):
        ins, outs = refs[:n], refs[n:2 * n]
        ssem, rsem, lsem, osem, buf = refs[2 * n:]
        x, y, c = lax.axis_index("x"), lax.axis_index("y"), lax.axis_index("c")
        sib = (x, y, 1 - c)
        pending = {}
        for k, (i, r0, ch) in enumerate(plan):
            slot = k % 2
            if slot in pending:
                rc, lc = pending[slot]
                rc.wait_send()
                lc.wait()
            stage = buf.at[slot, pl.ds(0, ch), pl.ds(0, reduced[i].shape[1])]
            ld = pltpu.make_async_copy(ins[i].at[pl.ds(r0, ch)], stage, lsem.at[slot])
            ld.start()
            ld.wait()
            place = outs[i].at[c, pl.ds(r0, ch)]
            rc = _remote(stage, place, ssem.at[slot], rsem.at[i], sib)
            lc = pltpu.make_async_copy(stage, place, osem.at[slot])
            rc.start()
            lc.start()
            pending[slot] = (rc, lc)
        for rc, lc in pending.values():
            rc.wait_send()
            lc.wait()
        for i in range(n):
            theirs = outs[i].at[1 - c]
            _remote(theirs, theirs, ssem.at[0], rsem.at[i], sib).wait_recv()

    outs = _pallas(
        body, name="share_halves", in_specs=[ANY] * n, out_specs=[ANY] * n,
        out_shape=[_sds((2,) + r.shape, F32) for r in reduced],
        scratch_shapes=[pltpu.SemaphoreType.DMA((2,)), pltpu.SemaphoreType.DMA((n,)), pltpu.SemaphoreType.DMA((2,)),
                        pltpu.SemaphoreType.DMA((2,)), pltpu.VMEM((2, max_rows, max_w), F32)],
        compiler_params=pltpu.CompilerParams(vmem_limit_bytes=32 << 20),
    )(*reduced)
    return [o.reshape(2 * r.shape[0], r.shape[1]) for o, r in zip(outs, reduced)]


def _tile(rows, cap=256):
    t = min(rows, cap) // 16 * 16
    while rows % t:
        t -= 16
    return t


def _adam_shard(w, g, m, v, name):
    _, R, W = w.shape
    tr = _tile(R, max(8, min(512, (1 << 18) // W // 8 * 8)))

    def body(w_ref, g_ref, m_ref, v_ref, go_ref, d_ref, nm_ref, nv_ref):
        gv = g_ref[...]
        go_ref[...] = gv
        d_ref[...], nm_ref[...], nv_ref[...] = _adamw(w_ref[...], gv, m_ref[...], v_ref[...])

    lead = pl.BlockSpec((None, tr, W), lambda i: (0, i, 0))
    return _pallas(
        body, grid=(R // tr,), name=name, in_specs=[lead, _row(tr, W), lead, lead], out_specs=[lead] * 4,
        out_shape=[_sds((1, R, W), F32)] * 4, compiler_params=_params(("parallel",), 48),
    )(w, g, m, v)


def _adamw(w, g, m, v):
    m = B1 * m + (1.0 - B1) * g
    v = B2 * v + (1.0 - B2) * (g * g)
    m_hat = m / (1.0 - B1 ** STEP)
    v_hat = v / (1.0 - B2 ** STEP)
    return -LR * (m_hat / (jnp.sqrt(v_hat) + AEPS) + WD * w), m, v


def _exchange_small(sent, comm):
    ns = len(sent)
    n = comm.n

    def body(*refs):
        p_refs, o_refs = refs[:ns], refs[ns + n:2 * ns + n]
        all_s = refs[2 * ns + 2 * n:3 * ns + 2 * n]
        ssem, rsem = refs[3 * ns + 2 * n:3 * ns + 2 * n + 2]
        cargs = (refs[ns:ns + n], refs[2 * ns + n:2 * ns + 2 * n], refs[3 * ns + 2 * n + 2:])
        comm.start(*cargs)
        x, y, c = lax.axis_index("x"), lax.axis_index("y"), lax.axis_index("c")
        me = 4 * x + 2 * y + c
        for i in range(ns):
            all_s[i][me] = p_refs[i][...]
        cps = []
        for rel in range(1, 8):
            peer = (1 - x if rel & 4 else x, 1 - y if rel & 2 else y, 1 - c if rel & 1 else c)
            for i in range(ns):
                k = (rel - 1) * ns + i
                mine = all_s[i].at[me]
                rc = _remote(mine, mine, ssem.at[k], rsem.at[k], peer)
                rc.start()
                cps.append((rc, i, k, 4 * peer[0] + 2 * peer[1] + peer[2]))
        for rc, i, k, peer_slot in cps:
            rc.wait_send()
            theirs = all_s[i].at[peer_slot]
            _remote(theirs, theirs, ssem.at[k], rsem.at[k], (x, y, c)).wait_recv()
        for i in range(ns):
            o_refs[i][...] = all_s[i][...]
        comm.finish(*cargs)

    shapes = [_sds((8,) + t.shape, F32) for t in sent]
    outs = _pallas(
        body, name="exchange_small", grid=(1,), in_specs=[_whole(t) for t in sent] + [ANY] * n,
        out_specs=[_whole(t) for t in shapes] + [ANY] * n, out_shape=shapes + comm.out_shape,
        scratch_shapes=[pltpu.VMEM(t.shape, F32) for t in shapes]
        + [pltpu.SemaphoreType.DMA((7 * ns,)), pltpu.SemaphoreType.DMA((7 * ns,))] + comm.scratch,
        compiler_params=_params(("arbitrary",), 32),
    )(*sent, *comm.ins)
    return outs[:ns], outs[ns:]


def _adam_small(ws, ms, vs, parts, loss_part, comm):
    n = len(ws)
    gathered, received = _exchange_small(list(parts) + [loss_part], comm)

    def body(*refs):
        w_refs, m_refs, v_refs, a_refs = refs[:n], refs[n:2 * n], refs[2 * n:3 * n], refs[3 * n:4 * n + 1]
        outs = refs[4 * n + 1:]
        g_refs, d_refs, nm_refs, nv_refs, loss_ref = outs[:n], outs[n:2 * n], outs[2 * n:3 * n], outs[3 * n:4 * n], outs[4 * n]

        def total(i):
            t = a_refs[i][0]
            for k in range(1, 8):
                t = t + a_refs[i][k]
            return t

        for i in range(n):
            g = total(i)
            g_refs[i][...] = g
            d_refs[i][...], nm_refs[i][...], nv_refs[i][...] = _adamw(w_refs[i][...], g, m_refs[i][...], v_refs[i][...])
        loss_ref[...] = total(n)

    shapes = [_sds(w.shape, F32) for w in ws] * 4 + [_sds(loss_part.shape, F32)]
    ins = [*ws, *ms, *vs, *gathered]
    outs = _pallas(
        body, name="adam_small", grid=(1,), in_specs=[_whole(t) for t in ins], out_specs=[_whole(t) for t in shapes],
        out_shape=shapes, compiler_params=_params(("arbitrary",), 32),
    )(*ins)
    return outs[:n], outs[n:2 * n], outs[2 * n:3 * n], outs[3 * n:4 * n], outs[4 * n], received


def _local_step(xs, tgt, mems, weights, small, gather_mid=None, gather_gu=None, gather_down=None, reduce=False):
    wt_in, wt_ba, wt_bs, wo, wq, wkv, wt_o, wt_gu, wd = weights
    g_mix, b_gate, w_sgu, b_sgu, g_sgu, g_cross, g_mem, g_ffn, g_final = small
    wt = jnp.tril(w_sgu).astype(BF16)
    bst = b_sgu.T

    (a, qkv0, qkv1, qkv2, uv, gl), got = _fwd_in(xs, g_mix, wt_in, gather_mid)
    if gather_mid is not None:
        wt_ba, wt_bs, wo, wq, wkv, wt_o = got
    qkvs = (qkv0, qkv1, qkv2)
    os_, ls_ = zip(*[_attn_fwd(qkvs[g], g) for g in range(3)])
    (ya, ys, ba, bs, mg, h1), got = _fwd_mid(xs, os_, ls_, uv, gl, wt, bst, g_sgu, b_gate, wt_ba, wt_bs, wo, gather_gu)
    if gather_gu is not None:
        (wt_gu,) = got
    mb, kv = _mem_fwd(mems, g_mem, wkv)
    (cb, qc, oc, h2), got = _fwd_cross(h1, g_cross, wq, kv, wt_o, gather_down)
    if gather_down is not None:
        (wd,) = got
    f, act, dgu, dh3b, dh2, dh2b, dg_ffn, dg_final, loss = _ffn_fwd_bwd(h2, tgt, g_ffn, g_final, wt_gu, wd)

    g_ffn_w = [_tn_matmul(dgu, f, "dw_gate_up", 1408), _tn_matmul(act, dh3b, "dw_down", 1408)]
    r_ffn = _Reduce(g_ffn_w, ["w_gate_up", "w_down"]) if reduce else None
    (dqc, dh1, dh1b, dkv, dg_cross), parts_ffn = _bwd_cross(dh2, h1, qc, g_cross, wq, kv, wt_o,
                                                           r_ffn.scatter if reduce else None)
    dw_kv, dw_kvb, dg_mem = _mem_bwd(dkv, mems, mb, g_mem, wkv)
    (dba, dbs, dgl, duv, do0, do1, do2, c0, c1, c2, db_gate, dg_sgu, dws, dbs_acc) = _bwd_mid(
        dh1, gl, ba, bs, uv, ls_, ya, wt, bst, g_sgu, b_gate, wt_ba, wt_bs, wo)
    dqkvs = [_attn_bwd(qkvs[g], do, ls_[g], corr, g) for g, (do, corr) in enumerate(((do0, c0), (do1, c1), (do2, c2)))]
    grad_x, dproj, dg_mix = _bwd_in(dqkvs, duv, dgl, dh1, xs, g_mix, wt_in)
    g_mid_w = [_tn_matmul(dba, ya, "dw_branch_attn", 1024),
               _tn_matmul(dbs, ys, "dw_branch_sgu", 1024),
               _tn_matmul(mg, dh1b, "dw_out", 1024),
               _tn_matmul(cb, dqc, "dw_q_cross", 1024),
               (dw_kv, dw_kvb),
               _tn_matmul(dh2b, oc, "dw_o_cross", 1024)]
    small_terms = (dg_mix, db_gate, dws, dbs_acc, dg_sgu, dg_cross, dg_mem, dg_ffn, dg_final)
    if not reduce:
        full = [_tn_matmul(dproj, a, "dw_in", 1792)] + g_mid_w + g_ffn_w
        return loss, grad_x, [g for g, _ in full], small_terms
    r_mid = _Reduce(g_mid_w, ["w_branch_attn", "w_branch_sgu", "w_out", "w_q_cross", "w_kv_cross", "w_o_cross"])
    g_in, parts_mid = _tn_matmul(dproj, a, "dw_in", 1792, comm=r_mid.scatter)
    r_in = _Reduce([g_in], ["w_in"])

    def finish(parts_in):
        return r_in.collect(parts_in) + r_mid.collect(parts_mid) + r_ffn.collect(parts_ffn)
    return loss, grad_x, (r_in.scatter, finish), small_terms


def kernel(x, mem, g_mix, w_in, b_gate, w_sgu_spatial, b_sgu_spatial, g_sgu, w_branch_attn, w_branch_sgu, w_out, g_cross, g_mem, w_q_cross, w_kv_cross, w_o_cross, g_ffn, w_gate_up, w_down, g_final, loss_target, m_g_mix, m_w_in, m_b_gate, m_w_sgu_spatial, m_b_sgu_spatial, m_g_sgu, m_w_branch_attn, m_w_branch_sgu, m_w_out, m_g_cross, m_g_mem, m_w_q_cross, m_w_kv_cross, m_w_o_cross, m_g_ffn, m_w_gate_up, m_w_down, m_g_final, v_g_mix, v_w_in, v_b_gate, v_w_sgu_spatial, v_b_sgu_spatial, v_g_sgu, v_w_branch_attn, v_w_branch_sgu, v_w_out, v_g_cross, v_g_mem, v_w_q_cross, v_w_kv_cross, v_w_o_cross, v_g_ffn, v_w_gate_up, v_w_down, v_g_final):
    S = x.shape[1]
    xs, tgt, mems = x.reshape(S, D), loss_target.reshape(S, D), mem.reshape(mem.shape[1], D)
    g_final2 = g_final.reshape(1, D)

    big = [("w_in", w_in, m_w_in, v_w_in, True),
           ("w_branch_attn", w_branch_attn, m_w_branch_attn, v_w_branch_attn, True),
           ("w_branch_sgu", w_branch_sgu, m_w_branch_sgu, v_w_branch_sgu, True),
           ("w_out", w_out, m_w_out, v_w_out, False),
           ("w_q_cross", w_q_cross, m_w_q_cross, v_w_q_cross, False),
           ("w_kv_cross", w_kv_cross, m_w_kv_cross, v_w_kv_cross, False),
           ("w_o_cross", w_o_cross, m_w_o_cross, v_w_o_cross, True),
           ("w_gate_up", w_gate_up, m_w_gate_up, v_w_gate_up, True),
           ("w_down", w_down, m_w_down, v_w_down, False)]
    shards = [(w[0].T if tr else w[0]).astype(BF16) for _, w, _, _, tr in big]
    (wt_in,) = _gather_weights(shards[:1])
    (loss, grad_x, (scatter_in, finish_reduce),
     (dg_mix, db_gate, dws, dbs_acc, dg_sgu, dg_cross, dg_mem, dg_ffn, dg_final)) = _local_step(
        xs, tgt, mems, (wt_in,) + (None,) * 8,
        (g_mix, b_gate, w_sgu_spatial[0], b_sgu_spatial[0], g_sgu, g_cross, g_mem, g_ffn, g_final2),
        _Gather(shards[1:7]), _Gather(shards[7:8]), _Gather(shards[8:9]), reduce=True)

    small = [("g_mix", g_mix, m_g_mix, v_g_mix, dg_mix), ("b_gate", b_gate, m_b_gate, v_b_gate, db_gate),
             ("w_sgu_spatial", w_sgu_spatial, m_w_sgu_spatial, v_w_sgu_spatial, jnp.tril(dws)),
             ("b_sgu_spatial", b_sgu_spatial, m_b_sgu_spatial, v_b_sgu_spatial, jnp.sum(dbs_acc, axis=-1)),
             ("g_sgu", g_sgu, m_g_sgu, v_g_sgu, dg_sgu), ("g_cross", g_cross, m_g_cross, v_g_cross, dg_cross),
             ("g_mem", g_mem, m_g_mem, v_g_mem, dg_mem), ("g_ffn", g_ffn, m_g_ffn, v_g_ffn, dg_ffn),
             ("g_final", g_final, m_g_final, v_g_final, dg_final)]
    as_term = lambda s, t: t.reshape(s[4].shape)
    gs, ds, nms, nvs, loss_all, parts_in = _adam_small(
        *[[as_term(s, s[k]) for s in small] for k in (1, 2, 3, 4)], loss, scatter_in)
    small_out = {s[0]: tuple(t[i].reshape(s[1].shape) for t in (gs, ds, nms, nvs)) for i, s in enumerate(small)}
    total_loss = loss_all[0, 0]

    full = _share_halves(finish_reduce(parts_in))
    big_out = {}
    for (name, w, m, v, tr), gsh in zip(big, full):
        if tr and w.shape[2] % 128:
            outs = _adam_shard(*(jnp.swapaxes(t, 1, 2) for t in (w,)), gsh, *(jnp.swapaxes(t, 1, 2) for t in (m, v)),
                               f"adam_{name}")
            big_out[name] = tuple(jnp.swapaxes(t, 1, 2) for t in outs)
        else:
            big_out[name] = tuple(_adam_shard(w, gsh.T if tr else gsh, m, v, f"adam_{name}"))

    order = ["g_mix", "w_in", "b_gate", "w_sgu_spatial", "b_sgu_spatial", "g_sgu", "w_branch_attn", "w_branch_sgu",
             "w_out", "g_cross", "g_mem", "w_q_cross", "w_kv_cross", "w_o_cross", "g_ffn", "w_gate_up", "w_down",
             "g_final"]
    res = {**big_out, **small_out}
    outs = [total_loss, grad_x.reshape(x.shape)]
    for k in range(4):
        outs += [res[nm][k] for nm in order]
    return tuple(outs)
```

```python
import math

import numpy as np
import jax
import jax.numpy as jnp
from jax import lax
from jax.experimental import pallas as pl
from jax.experimental.pallas import tpu as pltpu

F32, BF16 = jnp.float32, jnp.bfloat16
MESH = pl.DeviceIdType.MESH
ANY = pl.BlockSpec(memory_space=pl.ANY)
RES = pl.BlockSpec(memory_space=pltpu.VMEM)


def _pallas(body, **kw):
    call = pl.pallas_call(body, **kw)
    gs = kw.get("grid_spec")
    specs = kw.get("in_specs") if gs is None else [None] * gs.num_scalar_prefetch + list(gs.in_specs)

    def run(*args):
        if specs is not None:
            args = [a if (s is RES or s is None) else pltpu.with_memory_space_constraint(a, pltpu.HBM)
                    for a, s in zip(args, specs)]
        return call(*args)
    return run


def _whole(arr):
    nd = len(arr.shape)
    return pl.BlockSpec(arr.shape, lambda *_: (0,) * nd)

D = 1024
HEAD = 64
GROUP_W = 256
DIL_GROUPS = ((128, 1), (512, 4), (2048, 16))
BLK = 128
SGU_W = 512
MEM_HEADS, MEM_HD, MEM_W = 4, 128, 512
D_FF = 2816
FF_CHUNK = 256
EPS = 1e-6
NEG = -1e30
LR, B1, B2, AEPS, WD, STEP = 0.001, 0.9, 0.999, 1e-08, 0.01, 10
GELU_K, GELU_C = 0.7978845608028654, 0.044715


def _dot(a, b):
    return jnp.dot(a, b, preferred_element_type=F32)


def _dot_nt(a, b):
    return lax.dot_general(a, b, (((1,), (1,)), ((), ())), preferred_element_type=F32)


def _dot_tn(a, b):
    return lax.dot_general(a, b, (((0,), (0,)), ((), ())), preferred_element_type=F32)


def _row(tm, w):
    return pl.BlockSpec((tm, w), lambda i: (i, 0))


def _acc(shape):
    return pl.BlockSpec(shape, lambda i: (0,) * len(shape))


def _params(sem, mb):
    return pltpu.CompilerParams(dimension_semantics=sem, vmem_limit_bytes=mb << 20)


def _sds(shape, dt):
    return jax.ShapeDtypeStruct(shape, dt)


def _rms(h):
    return lax.rsqrt(jnp.mean(h * h, axis=-1, keepdims=True) + EPS)


def _rms_bwd(dy, h, r, g):
    t = dy * g
    dh = r * t - h * (r * r * r) * jnp.mean(t * h, axis=-1, keepdims=True)
    return dh, dy * h * r


def _gelu(x):
    t = jnp.tanh(GELU_K * (x + GELU_C * x * x * x))
    return 0.5 * x * (1.0 + t), t


def _gelu_grad(x, t):
    return 0.5 * (1.0 + t) + 0.5 * x * (1.0 - t * t) * GELU_K * (1.0 + 3.0 * GELU_C * x * x)


def _alibi_slopes():
    def pow2(n):
        start = 2.0 ** (-8.0 / n)
        return [start ** (i + 1) for i in range(n)]
    n = 12
    c = 2 ** int(math.floor(math.log2(n)))
    s = pow2(c) + pow2(2 * c)[0::2][: n - c]
    return np.array(sorted(s, reverse=True), dtype=np.float32).reshape(3, 4)


def _attn_bias(g):
    win, dil = DIL_GROUPS[g]
    steps = (np.arange(BLK)[:, None] + BLK) - np.arange(2 * BLK)[None, :]
    valid = (steps >= 0) & (steps <= win // dil)
    dist = (np.clip(steps, 0, None) * dil).astype(np.float32)
    b = -_alibi_slopes()[g][:, None, None] * dist[None]
    return np.where(valid[None], b, NEG).astype(np.float32)


def _head_masks():
    lane = lax.broadcasted_iota(jnp.int32, (1, GROUP_W), 1)
    return lane, [(lane >= HEAD * h) & (lane < HEAD * (h + 1)) for h in range(4)]


ATT_NB = 8


def _stack_heads(t, masks):
    z = jnp.zeros_like(t)
    return jnp.concatenate([jnp.where(m, t, z) for m in masks], axis=0)


def _unstack_heads(t, masks):
    out = jnp.zeros((BLK, GROUP_W), t.dtype)
    for h, m in enumerate(masks):
        out = jnp.where(m, t[h * BLK:(h + 1) * BLK], out)
    return out


def _stack_cols(ref, rows):
    return jnp.concatenate([ref[rows, HEAD * h:HEAD * h + 1] for h in range(4)], axis=0)


def _dil_spec(d, tm, w):
    return pl.BlockSpec((d, tm // d, w), lambda i: (0, i, 0))


def _to_dilated(val, s_ref, d, write):
    tm, w = val.shape
    for j in range(w // 128):
        s_ref[j, pl.ds(0, tm), :] = val[:, j * 128:(j + 1) * 128]
    for r in range(d):
        for j in range(w // 128):
            write(r, j, s_ref[j, pl.ds(r, tm // d, stride=d), :])


def _from_dilated(ref, s_ref, d, tm, w):
    if d == 1:
        return ref[0].astype(F32)
    for r in range(d):
        for j in range(w // 128):
            s_ref[j, pl.ds(r, tm // d, stride=d), :] = ref[r, :, j * 128:(j + 1) * 128].astype(F32)
    return jnp.concatenate([s_ref[j, pl.ds(0, tm), :] for j in range(w // 128)], axis=1)


def _norm_in(x, g_mix, gather=None, tm=512):
    S = x.shape[0]
    n = 0 if gather is None else gather.n
    last = S // tm - 1

    def body(*refs):
        x_ref, g_ref, a_ref = refs[0], refs[1], refs[2 + n]
        comm = (refs[2:2 + n], refs[3 + n:3 + 2 * n], refs[3 + 2 * n:])
        if gather is not None:
            pl.when(pl.program_id(0) == 0)(lambda: gather.start(*comm))
        xv = x_ref[...]
        a_ref[...] = (xv * _rms(xv) * g_ref[...]).astype(BF16)
        if gather is not None:
            @pl.when(pl.program_id(0) == last)
            def _():
                gather.relay(*comm)
                gather.finish(*comm)

    outs = _pallas(
        body, grid=(S // tm,), name="norm_in", in_specs=[_row(tm, D), _whole(g_mix)] + [ANY] * n,
        out_specs=[_row(tm, D)] + [ANY] * n,
        out_shape=[_sds((S, D), BF16)] + ([] if gather is None else gather.out_shape),
        scratch_shapes=[] if gather is None else gather.scratch,
        compiler_params=_params(("arbitrary",), 32),
    )(x, g_mix, *([] if gather is None else gather.halves))
    return outs[0], ([] if gather is None else gather.full(outs[1:]))


def _fwd_in(a_in, wt_in, gather=None, tm=512):
    S = a_in.shape[0]
    dils = [d for _, d in DIL_GROUPS]
    n = 0 if gather is None else gather.n
    last = S // tm - 1

    def body(*refs):
        a_ref, w_ref = refs[:2]
        q0_ref, q1_ref, q2_ref, uv_ref, gl_ref = refs[2 + n:7 + n]
        s_ref = refs[7 + 2 * n]
        comm = (refs[2:2 + n], refs[7 + n:7 + 2 * n], refs[8 + 2 * n:])
        if gather is not None:
            pl.when(pl.program_id(0) == 0)(lambda: gather.start(*comm))
        a = a_ref[...]
        for g, (d, out) in enumerate(zip(dils, (q0_ref, q1_ref, q2_ref))):
            for part in range(3):
                rows = part * 768 + g * 256
                val = _dot_nt(a, w_ref[rows:rows + 256, :])
                if d == 1:
                    out[0, :, part * 256:(part + 1) * 256] = val.astype(BF16)
                else:
                    def write(r, j, piece, out=out, part=part):
                        out[r, :, part * 256 + j * 128:part * 256 + (j + 1) * 128] = piece.astype(BF16)
                    _to_dilated(val, s_ref, d, write)
        uv_ref[...] = _dot_nt(a, w_ref[2304:3328, :]).astype(BF16)
        gl_ref[...] = _dot_nt(a, w_ref[3328:5376, :]).astype(BF16)
        if gather is not None:
            pl.when(pl.program_id(0) == last // 2)(lambda: gather.relay(*comm))
            pl.when(pl.program_id(0) == last)(lambda: gather.finish(*comm))

    outs = _pallas(
        body, grid=(S // tm,), name="fwd_in",
        in_specs=[_row(tm, D), RES] + [ANY] * n,
        out_specs=[_dil_spec(d, tm, 768) for d in dils] + [_row(tm, 1024), _row(tm, 2048)] + [ANY] * n,
        out_shape=[_sds((d, S // d, 768), BF16) for d in dils]
        + [_sds((S, 1024), BF16), _sds((S, 2048), BF16)] + ([] if gather is None else gather.out_shape),
        scratch_shapes=[pltpu.VMEM((2, tm, 128), F32)] + ([] if gather is None else gather.scratch),
        compiler_params=_params(("arbitrary",), 60),
    )(a_in, wt_in, *([] if gather is None else gather.halves))
    return outs[:5], ([] if gather is None else gather.full(outs[5:]))


def _attn_fwd(qkv, g):
    d, L, _ = qkv.shape
    nb = L // BLK
    bias = jnp.asarray(_attn_bias(g).reshape(4 * BLK, 2 * BLK))
    NB = min(ATT_NB, nb)
    W = NB * BLK

    def body(q_ref, kc_ref, kp_ref, vc_ref, vp_ref, b_ref, o_ref, l_ref):
        st = pl.program_id(1)
        k_all = jnp.concatenate([kp_ref[...], kc_ref[...]], axis=0)
        v_all = jnp.concatenate([vp_ref[...], vc_ref[...]], axis=0)
        lane, masks = _head_masks()
        for b in range(NB):
            rows = slice(b * BLK, (b + 1) * BLK)
            kk, vv = k_all[b * BLK:(b + 2) * BLK], v_all[b * BLK:(b + 2) * BLK]
            s = _dot_nt(_stack_heads(q_ref[rows, :], masks), kk) * 0.125 + b_ref[...]
            if b == 0:
                s = s + jnp.where((st == 0) & (lane < BLK), NEG, 0.0).astype(F32)
            mx = jnp.max(s, axis=-1, keepdims=True)
            e = jnp.exp(s - mx)
            den = jnp.sum(e, axis=-1, keepdims=True)
            o_ref[rows, :] = _unstack_heads(_dot(e.astype(BF16), vv) / den, masks)
            l_ref[rows, :] = _unstack_heads(mx + jnp.log(den), masks)

    def wide(col):
        return pl.BlockSpec((None, W, GROUP_W), lambda r, s: (r, s, col))

    def before(col):
        return pl.BlockSpec((None, BLK, GROUP_W), lambda r, s: (r, jnp.maximum(s * NB - 1, 0), col))

    return _pallas(
        body, grid=(d, nb // NB), name=f"attn_fwd_g{g}",
        in_specs=[wide(0), wide(1), before(1), wide(2), before(2),
                  pl.BlockSpec((4 * BLK, 2 * BLK), lambda r, s: (0, 0))],
        out_specs=[wide(0), wide(0)],
        out_shape=[_sds((d, L, GROUP_W), F32), _sds((d, L, GROUP_W), F32)],
        compiler_params=_params(("parallel", "parallel"), 32),
    )(qkv, qkv, qkv, qkv, qkv, bias)


def _group_weights(l0, l1, l2):
    m = jnp.maximum(jnp.maximum(l0, l1), l2)
    e0, e1, e2 = jnp.exp(l0 - m), jnp.exp(l1 - m), jnp.exp(l2 - m)
    inv = 1.0 / (e0 + e1 + e2)
    return e0 * inv, e1 * inv, e2 * inv


def _sgu_forward(uvf, gs, wt_ref, bst_ref, mixed_s, tm):
    z, t = _gelu(uvf)
    u, v = z[:, :SGU_W], z[:, SGU_W:]
    rv = _rms(v)
    vnb = (v * rv * gs).astype(BF16)
    for ci in range(tm // 128):
        for g in range(4):
            rs, cs = slice(ci * 128, (ci + 1) * 128), slice(g * 128, (g + 1) * 128)
            mixed_s[rs, cs] = _dot(wt_ref[g], vnb[rs, cs]) + bst_ref[:, g:g + 1]
    return u, v, rv, vnb, t


def _fwd_mid(x, os_, ls_, uv, gl, wt, bst, g_sgu, b_gate, wt_ba, wt_bs, w_out, gather=None, tm=512):
    S = x.shape[0]
    dils = [d for _, d in DIL_GROUPS]
    n = 0 if gather is None else gather.n
    last = S // tm - 1

    def body(*refs):
        (x_ref, o0, o1, o2, l0, l1, l2, uv_ref, gl_ref, wt_ref, bst_ref, gs_ref, bg_ref, wba_ref, wbs_ref,
         wo_ref) = refs[:16]
        ya_ref, ys_ref, ba_ref, bs_ref, mg_ref, h1_ref = refs[16 + n:22 + n]
        mixed_s, il_s = refs[22 + 2 * n:24 + 2 * n]
        comm = (refs[16:16 + n], refs[22 + n:22 + 2 * n], refs[24 + 2 * n:])
        if gather is not None:
            pl.when(pl.program_id(0) == 0)(lambda: gather.start(*comm))
        ls = [_from_dilated(r, il_s, d, tm, GROUP_W) for r, d in zip((l0, l1, l2), dils)]
        alphas = _group_weights(*ls)
        ya = jnp.zeros((tm, GROUP_W), F32)
        for a, r, d in zip(alphas, (o0, o1, o2), dils):
            ya = ya + a * _from_dilated(r, il_s, d, tm, GROUP_W)
        yab = ya.astype(BF16)
        ya_ref[...] = yab
        u, _, _, _, _ = _sgu_forward(uv_ref[...].astype(F32), gs_ref[...], wt_ref, bst_ref, mixed_s, tm)
        ysb = (u * mixed_s[...]).astype(BF16)
        ys_ref[...] = ysb
        gates = jax.nn.sigmoid(gl_ref[...].astype(F32) + bg_ref[...])
        ba = _dot_nt(yab, wba_ref[...])
        bs = _dot_nt(ysb, wbs_ref[...])
        ba_ref[...] = ba.astype(BF16)
        bs_ref[...] = bs.astype(BF16)
        mgb = (gates[:, :D] * ba + gates[:, D:] * bs).astype(BF16)
        mg_ref[...] = mgb
        h1_ref[...] = x_ref[...] + _dot(mgb, wo_ref[...])
        if gather is not None:
            pl.when(pl.program_id(0) == (3 * last) // 4)(lambda: gather.relay(*comm))
            pl.when(pl.program_id(0) == last)(lambda: gather.finish(*comm))

    gw = _row(tm, GROUP_W)
    dil = [_dil_spec(d, tm, GROUP_W) for d in dils]
    outs = _pallas(
        body, grid=(S // tm,), name="fwd_mid",
        in_specs=[_row(tm, D)] + dil + dil + [_row(tm, 1024), _row(tm, 2048)]
        + [_whole(t) for t in (wt, bst, g_sgu, b_gate)] + [RES] * 3 + [ANY] * n,
        out_specs=[gw, _row(tm, SGU_W), _row(tm, D), _row(tm, D), _row(tm, D), _row(tm, D)] + [ANY] * n,
        out_shape=[_sds((S, GROUP_W), BF16), _sds((S, SGU_W), BF16), _sds((S, D), BF16), _sds((S, D), BF16),
                   _sds((S, D), BF16), _sds((S, D), F32)] + ([] if gather is None else gather.out_shape),
        scratch_shapes=[pltpu.VMEM((tm, SGU_W), F32), pltpu.VMEM((2, tm, 128), F32)]
        + ([] if gather is None else gather.scratch),
        compiler_params=_params(("arbitrary",), 56),
    )(x, *os_, *ls_, uv, gl, wt, bst, g_sgu, b_gate, wt_ba, wt_bs, w_out, *([] if gather is None else gather.halves))
    return outs[:6], ([] if gather is None else gather.full(outs[6:]))


def _mem_fwd(mem, g_mem, w_kv):
    def body(m_ref, g_ref, w_ref, mb_ref, kv_ref):
        mv = m_ref[...]
        mb = (mv * _rms(mv) * g_ref[...]).astype(BF16)
        mb_ref[...] = mb
        kv_ref[...] = _dot(mb, w_ref[...]).astype(BF16)

    shapes = [_sds(mem.shape, BF16), _sds((mem.shape[0], 2 * MEM_W), BF16)]
    return _pallas(
        body, name="mem_fwd", grid=(1,), in_specs=[_whole(t) for t in (mem, g_mem, w_kv)],
        out_specs=[_whole(t) for t in shapes], out_shape=shapes, compiler_params=_params(("arbitrary",), 32),
    )(mem, g_mem, w_kv)


def _cross_probs(qh, kh):
    s = _dot_nt(qh, kh) * (MEM_HD ** -0.5)
    e = jnp.exp(s - jnp.max(s, axis=-1, keepdims=True))
    return e / jnp.sum(e, axis=-1, keepdims=True)


def _fwd_cross(h1, g_cross, w_q, kv, wt_o, gather=None, tm=512):
    S = h1.shape[0]
    n = 0 if gather is None else gather.n
    last = S // tm - 1

    def body(*refs):
        h_ref, g_ref, wq_ref, kv_ref, wo_ref = refs[:5]
        c_ref, qc_ref, oc_ref, h2_ref = refs[5 + n:9 + n]
        comm = (refs[5:5 + n], refs[9 + n:9 + 2 * n], refs[9 + 2 * n:])
        if gather is not None:
            pl.when(pl.program_id(0) == 0)(lambda: gather.start(*comm))
        hv = h_ref[...]
        cb = (hv * _rms(hv) * g_ref[...]).astype(BF16)
        c_ref[...] = cb
        qcb = _dot(cb, wq_ref[...]).astype(BF16)
        qc_ref[...] = qcb
        for h in range(MEM_HEADS):
            cs = slice(h * MEM_HD, (h + 1) * MEM_HD)
            p = _cross_probs(qcb[:, cs], kv_ref[:, cs])
            oc_ref[:, cs] = _dot(p.astype(BF16), kv_ref[:, MEM_W + h * MEM_HD:MEM_W + (h + 1) * MEM_HD]).astype(BF16)
        h2_ref[...] = hv + _dot_nt(oc_ref[...], wo_ref[...])
        if gather is not None:
            pl.when(pl.program_id(0) == (3 * last) // 4)(lambda: gather.relay(*comm))
            pl.when(pl.program_id(0) == last)(lambda: gather.finish(*comm))

    outs = _pallas(
        body, grid=(S // tm,), name="fwd_cross",
        in_specs=[_row(tm, D), _whole(g_cross), RES, _whole(kv), RES] + [ANY] * n,
        out_specs=[_row(tm, D), _row(tm, MEM_W), _row(tm, MEM_W), _row(tm, D)] + [ANY] * n,
        out_shape=[_sds((S, D), BF16), _sds((S, MEM_W), BF16), _sds((S, MEM_W), BF16), _sds((S, D), F32)]
        + ([] if gather is None else gather.out_shape),
        scratch_shapes=[] if gather is None else gather.scratch,
        compiler_params=_params(("arbitrary",), 40),
    )(h1, g_cross, w_q, kv, wt_o, *([] if gather is None else gather.halves))
    return outs[:4], ([] if gather is None else gather.full(outs[4:]))


def _ffn_fwd_bwd(h2, target, g_ffn, g_final, wt_gu, w_down, tm=256):
    S = h2.shape[0]
    nch = D_FF // FF_CHUNK

    def body(h_ref, t_ref, gf_ref, gz_ref, wgu_ref, wd_ref,
             f_ref, act_ref, dgu_ref, dh3b_ref, dh2_ref, dh2b_ref, dgf_ref, dgz_ref, loss_ref, gu_s, dact_s):
        i = pl.program_id(0)

        @pl.when(i == 0)
        def _():
            dgf_ref[...] = jnp.zeros_like(dgf_ref)
            dgz_ref[...] = jnp.zeros_like(dgz_ref)
            loss_ref[...] = jnp.zeros_like(loss_ref)

        hv = h_ref[...]
        r2 = _rms(hv)
        gf = gf_ref[...]
        fb = (hv * r2 * gf).astype(BF16)
        f_ref[...] = fb
        gu_s[...] = _dot_nt(fb, wgu_ref[...])
        for c in range(nch):
            cs = slice(c * FF_CHUNK, (c + 1) * FF_CHUNK)
            us = slice(D_FF + c * FF_CHUNK, D_FF + (c + 1) * FF_CHUNK)
            gt = gu_s[:, cs]
            act_ref[:, cs] = (gt * jax.nn.sigmoid(gt) * gu_s[:, us]).astype(BF16)
        h3 = hv + _dot(act_ref[...], wd_ref[...])
        r3 = _rms(h3)
        gz = gz_ref[...]
        diff = h3 * r3 * gz - t_ref[...]
        dy = diff * (1.0 / D)
        dh3, dgz_rows = _rms_bwd(dy, h3, r3, gz)
        dh3b = dh3.astype(BF16)
        dh3b_ref[...] = dh3b
        dact_s[...] = _dot_nt(dh3b, wd_ref[...])
        for c in range(nch):
            cs = slice(c * FF_CHUNK, (c + 1) * FF_CHUNK)
            us = slice(D_FF + c * FF_CHUNK, D_FF + (c + 1) * FF_CHUNK)
            dact, gt, up = dact_s[:, cs], gu_s[:, cs], gu_s[:, us]
            sg = jax.nn.sigmoid(gt)
            dgu_ref[:, cs] = (dact * up * (sg * (1.0 + gt * (1.0 - sg)))).astype(BF16)
            dgu_ref[:, us] = (dact * (gt * sg)).astype(BF16)
        df = _dot(dgu_ref[...], wgu_ref[...])
        dhn, dgf_rows = _rms_bwd(df, hv, r2, gf)
        dh2 = dh3 + dhn
        dh2_ref[...] = dh2
        dh2b_ref[...] = dh2.astype(BF16)
        dgf_ref[...] += jnp.sum(dgf_rows, axis=0, keepdims=True)
        dgz_ref[...] += jnp.sum(dgz_rows, axis=0, keepdims=True)
        loss_ref[...] += jnp.sum(jnp.sum(diff * diff, axis=0, keepdims=True), axis=1, keepdims=True) * (0.5 / D)

    return _pallas(
        body, grid=(S // tm,), name="ffn_fwd_bwd",
        in_specs=[_row(tm, D), _row(tm, D), _whole(g_ffn), _whole(g_final), RES, RES],
        out_specs=[_row(tm, D), _row(tm, D_FF), _row(tm, 2 * D_FF), _row(tm, D), _row(tm, D), _row(tm, D),
                   _acc((1, D)), _acc((1, D)), _acc((1, 128))],
        out_shape=[_sds((S, D), BF16), _sds((S, D_FF), BF16), _sds((S, 2 * D_FF), BF16), _sds((S, D), BF16),
                   _sds((S, D), F32), _sds((S, D), BF16), _sds((1, D), F32), _sds((1, D), F32), _sds((1, 128), F32)],
        scratch_shapes=[pltpu.VMEM((tm, 2 * D_FF), F32), pltpu.VMEM((tm, D_FF), F32)],
        compiler_params=_params(("arbitrary",), 60),
    )(h2, target, g_ffn, g_final, wt_gu, w_down)


def _bwd_cross(dh2, h1, qc, g_cross, w_q, kv, wt_o, comm=None, tm=512):
    S = h1.shape[0]
    n = 0 if comm is None else comm.n
    last = S // tm - 1

    def body(*refs):
        d_ref, h_ref, qc_ref, g_ref, wq_ref, kv_ref, wo_ref = refs[:7]
        dqc_ref, dh1_ref, dh1b_ref, dkv_ref, dg_ref = refs[7 + n:12 + n]
        cargs = (refs[7:7 + n], refs[12 + n:12 + 2 * n], refs[12 + 2 * n:])
        i = pl.program_id(0)

        @pl.when(i == 0)
        def _():
            dkv_ref[...] = jnp.zeros_like(dkv_ref)
            dg_ref[...] = jnp.zeros_like(dg_ref)
            if comm is not None:
                comm.start(*cargs)

        dh2 = d_ref[...]
        doc = _dot(dh2.astype(BF16), wo_ref[...])
        qcb = qc_ref[...]
        for h in range(MEM_HEADS):
            cs = slice(h * MEM_HD, (h + 1) * MEM_HD)
            vs = slice(MEM_W + h * MEM_HD, MEM_W + (h + 1) * MEM_HD)
            qh, kh, vh = qcb[:, cs], kv_ref[:, cs], kv_ref[:, vs]
            p = _cross_probs(qh, kh)
            dohb = doc[:, cs].astype(BF16)
            dp = _dot_nt(dohb, vh)
            dsb = (p * (dp - jnp.sum(dp * p, axis=-1, keepdims=True)) * (MEM_HD ** -0.5)).astype(BF16)
            dqc_ref[:, cs] = _dot(dsb, kh).astype(BF16)
            dkv_ref[:, cs] += _dot_tn(dsb, qh)
            dkv_ref[:, vs] += _dot_tn(p.astype(BF16), dohb)
        dc = _dot_nt(dqc_ref[...], wq_ref[...])
        hv = h_ref[...]
        dhn, dg_rows = _rms_bwd(dc, hv, _rms(hv), g_ref[...])
        dh1 = dh2 + dhn
        dh1_ref[...] = dh1
        dh1b_ref[...] = dh1.astype(BF16)
        dg_ref[...] += jnp.sum(dg_rows, axis=0, keepdims=True)
        if comm is not None:
            pl.when(i == last)(lambda: comm.finish(*cargs))

    outs = _pallas(
        body, grid=(S // tm,), name="bwd_cross",
        in_specs=[_row(tm, D), _row(tm, D), _row(tm, MEM_W), _whole(g_cross), RES, _whole(kv), RES] + [ANY] * n,
        out_specs=[_row(tm, MEM_W), _row(tm, D), _row(tm, D), _acc((256, 2 * MEM_W)), _acc((1, D))] + [ANY] * n,
        out_shape=[_sds((S, MEM_W), BF16), _sds((S, D), F32), _sds((S, D), BF16), _sds((256, 2 * MEM_W), F32),
                   _sds((1, D), F32)] + ([] if comm is None else comm.out_shape),
        scratch_shapes=[] if comm is None else comm.scratch,
        compiler_params=_params(("arbitrary",), 48),
    )(dh2, h1, qc, g_cross, w_q, kv, wt_o, *([] if comm is None else comm.ins))
    return outs[:5], outs[5:]


def _mem_bwd(dkv, mem, mb, g_mem, w_kv):
    def body(dkv_ref, m_ref, mb_ref, g_ref, w_ref, dw_ref, dwb_ref, dg_ref):
        dkvb = dkv_ref[...].astype(BF16)
        dw = _dot_tn(mb_ref[...], dkvb)
        dw_ref[...] = dw
        dwb_ref[...] = dw.astype(BF16)
        dm = _dot_nt(dkvb, w_ref[...])
        mv = m_ref[...]
        dg_ref[...] = jnp.sum(dm * mv * _rms(mv), axis=0, keepdims=True)

    shapes = [_sds((D, 2 * MEM_W), F32), _sds((D, 2 * MEM_W), BF16), _sds((1, D), F32)]
    return _pallas(
        body, name="mem_bwd", grid=(1,), in_specs=[_whole(t) for t in (dkv, mem, mb, g_mem, w_kv)],
        out_specs=[_whole(t) for t in shapes], out_shape=shapes, compiler_params=_params(("arbitrary",), 40),
    )(dkv, mem, mb, g_mem, w_kv)


def _bwd_mid(dh1, gl, ba, bs, uv, ls_, ya, wt, bst, g_sgu, b_gate, wt_ba, wt_bs, w_out, tm=512):
    S = dh1.shape[0]
    dils = [d for _, d in DIL_GROUPS]

    def body(d_ref, gl_ref, ba_ref, bs_ref, uv_ref, l0, l1, l2, ya_ref,
             wt_ref, bst_ref, gs_ref, bg_ref, wba_ref, wbs_ref, wo_ref,
             dba_ref, dbs_ref, dgl_ref, duv_ref, do0, do1, do2, c0, c1, c2,
             dbg_ref, dgs_ref, dws_ref, dbsa_ref, mixed_s, dvn_s, il_s):
        i = pl.program_id(0)

        @pl.when(i == 0)
        def _():
            for r in (dbg_ref, dgs_ref, dws_ref, dbsa_ref):
                r[...] = jnp.zeros_like(r)

        dm = _dot_nt(d_ref[...].astype(BF16), wo_ref[...])
        gates = jax.nn.sigmoid(gl_ref[...].astype(F32) + bg_ref[...])
        g0, g1 = gates[:, :D], gates[:, D:]
        dbab = (dm * g0).astype(BF16)
        dbsb = (dm * g1).astype(BF16)
        dba_ref[...] = dbab
        dbs_ref[...] = dbsb
        dg0 = dm * ba_ref[...].astype(F32) * g0 * (1.0 - g0)
        dg1 = dm * bs_ref[...].astype(F32) * g1 * (1.0 - g1)
        dgl_ref[:, :D] = dg0.astype(BF16)
        dgl_ref[:, D:] = dg1.astype(BF16)
        dbg_ref[:, :D] += jnp.sum(dg0, axis=0, keepdims=True)
        dbg_ref[:, D:] += jnp.sum(dg1, axis=0, keepdims=True)
        dya = _dot(dbab, wba_ref[...])
        dys = _dot(dbsb, wbs_ref[...])

        uvf = uv_ref[...].astype(F32)
        gs = gs_ref[...]
        u, v, rv, vnb, t = _sgu_forward(uvf, gs, wt_ref, bst_ref, mixed_s, tm)
        du = dys * mixed_s[...]
        dmixed = dys * u
        for ci in range(tm // 128):
            for g in range(4):
                rs, cs = slice(ci * 128, (ci + 1) * 128), slice(g * 128, (g + 1) * 128)
                dmx = dmixed[rs, cs]
                dmxb = dmx.astype(BF16)
                dvn_s[rs, cs] = _dot_tn(wt_ref[g], dmxb)
                dws_ref[g] += _dot_nt(dmxb, vnb[rs, cs])
                dbsa_ref[g] += dmx
        dv, dgs_rows = _rms_bwd(dvn_s[...], v, rv, gs)
        dgs_ref[...] += jnp.sum(dgs_rows, axis=0, keepdims=True)
        gg = _gelu_grad(uvf, t)
        duv_ref[:, :SGU_W] = (du * gg[:, :SGU_W]).astype(BF16)
        duv_ref[:, SGU_W:] = (dv * gg[:, SGU_W:]).astype(BF16)

        alphas = _group_weights(*[_from_dilated(r, il_s, d, tm, GROUP_W) for r, d in zip((l0, l1, l2), dils)])
        prod = dya * ya_ref[...].astype(F32)
        _, masks = _head_masks()
        hs = jnp.zeros_like(prod)
        for h in range(4):
            sh = jnp.sum(jnp.where(masks[h], prod, 0.0), axis=-1, keepdims=True)
            hs = jnp.where(masks[h], sh, hs)
        for a, d, do_ref, c_ref in zip(alphas, dils, (do0, do1, do2), (c0, c1, c2)):
            for val, out in ((a * dya, do_ref), (a * hs, c_ref)):
                if d == 1:
                    out[0] = val.astype(out.dtype)
                else:
                    def write(r, j, piece, out=out):
                        out[r, :, j * 128:(j + 1) * 128] = piece.astype(out.dtype)
                    _to_dilated(val, il_s, d, write)

    gw = _row(tm, GROUP_W)
    dil = [_dil_spec(d, tm, GROUP_W) for d in dils]
    return _pallas(
        body, grid=(S // tm,), name="bwd_mid",
        in_specs=[_row(tm, D), _row(tm, 2048), _row(tm, D), _row(tm, D), _row(tm, 1024)] + dil + [gw]
        + [_whole(t) for t in (wt, bst, g_sgu, b_gate)] + [RES] * 3,
        out_specs=[_row(tm, D), _row(tm, D), _row(tm, 2048), _row(tm, 1024)] + dil + dil
        + [_acc((1, 2048)), _acc((1, SGU_W)), _acc((4, 128, 128)), _acc((4, 128, 128))],
        out_shape=[_sds((S, D), BF16), _sds((S, D), BF16), _sds((S, 2048), BF16), _sds((S, 1024), BF16)]
        + [_sds((d, S // d, GROUP_W), BF16) for d in dils] + [_sds((d, S // d, GROUP_W), F32) for d in dils]
        + [_sds((1, 2048), F32), _sds((1, SGU_W), F32), _sds((4, 128, 128), F32), _sds((4, 128, 128), F32)],
        scratch_shapes=[pltpu.VMEM((tm, SGU_W), F32), pltpu.VMEM((tm, SGU_W), F32), pltpu.VMEM((2, tm, 128), F32)],
        compiler_params=_params(("arbitrary",), 60),
    )(dh1, gl, ba, bs, uv, *ls_, ya, wt, bst, g_sgu, b_gate, wt_ba, wt_bs, w_out)


def _attn_bwd(qkv, do, lse, corr, g):
    d, L, _ = qkv.shape
    nb = L // BLK
    NB = min(ATT_NB, nb)
    W = NB * BLK
    nsteps = nb // NB
    bias = jnp.asarray(_attn_bias(g).reshape(4 * BLK, 2 * BLK))

    def body(q_ref, kc_ref, kp_ref, vc_ref, vp_ref, do_ref, l_ref, c_ref, qn_ref, don_ref, ln_ref, cn_ref, b_ref,
             out_ref, dk_s, dv_s):
        st = pl.program_id(1)
        k_all = jnp.concatenate([kp_ref[...], kc_ref[...]], axis=0)
        v_all = jnp.concatenate([vp_ref[...], vc_ref[...]], axis=0)
        lane, masks = _head_masks()
        dk_s[...] = jnp.zeros_like(dk_s)
        dv_s[...] = jnp.zeros_like(dv_s)

        def block_terms(qs, dos, kk, vv, bias_v, lse_c, corr_c):
            s = _dot_nt(qs, kk) * 0.125 + bias_v
            p = jnp.exp(s - lse_c)
            dsb = (p * (_dot_nt(dos, vv) - corr_c) * 0.125).astype(BF16)
            return dsb, p.astype(BF16)

        for b in range(NB):
            rows = slice(b * BLK, (b + 1) * BLK)
            keys = slice(b * BLK, (b + 2) * BLK)
            kk, vv = k_all[keys], v_all[keys]
            qs, dos = _stack_heads(q_ref[rows, :], masks), _stack_heads(do_ref[rows, :], masks)
            bias_v = b_ref[...]
            if b == 0:
                bias_v = bias_v + jnp.where((st == 0) & (lane < BLK), NEG, 0.0).astype(F32)
            dsb, pb = block_terms(qs, dos, kk, vv, bias_v, _stack_cols(l_ref, rows), _stack_cols(c_ref, rows))
            out_ref[rows, 0:GROUP_W] = _unstack_heads(_dot(dsb, kk), masks).astype(BF16)
            dk_s[keys, :] += _dot_tn(dsb, qs)
            dv_s[keys, :] += _dot_tn(pb, dos)

        @pl.when(st < nsteps - 1)
        def _():
            last = slice(NB * BLK, (NB + 1) * BLK)
            qs, dos = _stack_heads(qn_ref[...], masks), _stack_heads(don_ref[...], masks)
            every = slice(None)
            dsb, pb = block_terms(qs, dos, k_all[last], v_all[last], b_ref[:, :BLK],
                                  _stack_cols(ln_ref, every), _stack_cols(cn_ref, every))
            dk_s[last, :] += _dot_tn(dsb, qs)
            dv_s[last, :] += _dot_tn(pb, dos)

        out_ref[:, GROUP_W:2 * GROUP_W] = dk_s[BLK:, :].astype(BF16)
        out_ref[:, 2 * GROUP_W:] = dv_s[BLK:, :].astype(BF16)

    def wide(col, w=GROUP_W):
        return pl.BlockSpec((None, W, w), lambda r, s: (r, s, col))

    def before(col):
        return pl.BlockSpec((None, BLK, GROUP_W), lambda r, s: (r, jnp.maximum(s * NB - 1, 0), col))

    def after(col):
        return pl.BlockSpec((None, BLK, GROUP_W), lambda r, s: (r, jnp.minimum((s + 1) * NB, nb - 1), col))

    return _pallas(
        body, grid=(d, nsteps), name=f"attn_bwd_g{g}",
        in_specs=[wide(0), wide(1), before(1), wide(2), before(2), wide(0), wide(0), wide(0),
                  after(0), after(0), after(0), after(0), pl.BlockSpec((4 * BLK, 2 * BLK), lambda r, s: (0, 0))],
        out_specs=wide(0, 768),
        out_shape=_sds((d, L, 768), BF16),
        scratch_shapes=[pltpu.VMEM(((NB + 1) * BLK, GROUP_W), F32), pltpu.VMEM(((NB + 1) * BLK, GROUP_W), F32)],
        compiler_params=_params(("parallel", "parallel"), 32),
    )(qkv, qkv, qkv, qkv, qkv, do, lse, corr, qkv, do, lse, corr, bias)


def _bwd_in(dqkvs, duv, dgl, dh1, x, g_mix, wt_in, tm=512):
    S = x.shape[0]
    dils = [d for _, d in DIL_GROUPS]

    def body(q0_ref, q1_ref, q2_ref, duv_ref, dgl_ref, d_ref, x_ref, g_ref, w_ref, dx_ref, dp_ref, dg_ref, il_s):
        i = pl.program_id(0)

        @pl.when(i == 0)
        def _():
            dg_ref[...] = jnp.zeros_like(dg_ref)

        for g, (d, ref) in enumerate(zip(dils, (q0_ref, q1_ref, q2_ref))):
            nat = _from_dilated(ref, il_s, d, tm, 768).astype(BF16)
            for part in range(3):
                col = part * 768 + g * 256
                dp_ref[:, col:col + 256] = nat[:, part * 256:(part + 1) * 256]
        dp_ref[:, 2304:3328] = duv_ref[...]
        dp_ref[:, 3328:5376] = dgl_ref[...]
        da = _dot(dp_ref[...], w_ref[...])
        xv = x_ref[...]
        dxn, dg_rows = _rms_bwd(da, xv, _rms(xv), g_ref[...])
        dx_ref[...] = d_ref[...] + dxn
        dg_ref[...] += jnp.sum(dg_rows, axis=0, keepdims=True)

    return _pallas(
        body, grid=(S // tm,), name="bwd_in",
        in_specs=[_dil_spec(d, tm, 768) for d in dils] + [_row(tm, 1024), _row(tm, 2048), _row(tm, D), _row(tm, D),
                                                          _whole(g_mix), RES],
        out_specs=[_row(tm, D), _row(tm, 5376), _acc((1, D))],
        out_shape=[_sds((S, D), F32), _sds((S, 5376), BF16), _sds((1, D), F32)],
        scratch_shapes=[pltpu.VMEM((6, tm, 128), F32)],
        compiler_params=_params(("arbitrary",), 60),
    )(*dqkvs, duv, dgl, dh1, x, g_mix, wt_in)


def _tn_matmul(a, b, name, tk, ts=2048, comm=None):
    S, K = a.shape
    N = b.shape[1]
    n = 0 if comm is None else comm.n
    nk, ns = K // tk, S // ts

    def body(*refs):
        a_ref, b_ref, o_ref, ob_ref = refs[0], refs[1], refs[2 + n], refs[3 + n]
        cargs = (refs[2:2 + n], refs[4 + n:4 + 2 * n], refs[4 + 2 * n:])
        k, s = pl.program_id(0), pl.program_id(1)
        if comm is not None:
            pl.when((k == 0) & (s == 0))(lambda: comm.start(*cargs))

        @pl.when(s == 0)
        def _():
            o_ref[...] = jnp.zeros_like(o_ref)

        o_ref[...] += _dot_tn(a_ref[...], b_ref[...])

        @pl.when(s == ns - 1)
        def _():
            ob_ref[...] = o_ref[...].astype(BF16)

        if comm is not None:
            pl.when((k == nk - 1) & (s == ns - 1))(lambda: comm.finish(*cargs))

    tile = pl.BlockSpec((tk, N), lambda k, s: (k, 0))
    outs = _pallas(
        body, grid=(nk, ns), name=name,
        in_specs=[pl.BlockSpec((ts, tk), lambda k, s: (s, k)), pl.BlockSpec((ts, N), lambda k, s: (s, 0))] + [ANY] * n,
        out_specs=[tile, tile] + [ANY] * n,
        out_shape=[_sds((K, N), F32), _sds((K, N), BF16)] + ([] if comm is None else comm.out_shape),
        scratch_shapes=[] if comm is None else comm.scratch,
        compiler_params=_params(("arbitrary", "arbitrary"), 56),
    )(a, b, *([] if comm is None else comm.ins))
    pair = (outs[0], outs[1])
    return pair if comm is None else (pair, outs[2:])


def _chip_peers(x, y):
    return [(1 - x, y), (x, 1 - y), (1 - x, 1 - y)]


STAGE_BYTES = 2 << 20


def _chunk_plan(shapes, itemsize):
    plan = []
    for i, (rows, w) in enumerate(shapes):
        ch = max(16, min(rows, (STAGE_BYTES // (w * itemsize)) // 16 * 16))
        while rows % ch:
            ch -= 16
        plan += [(i, r0, ch) for r0 in range(0, rows, ch)]
    return plan


def _remote(src, dst, ssem, rsem, dev):
    return pltpu.make_async_remote_copy(src_ref=src, dst_ref=dst, send_sem=ssem, recv_sem=rsem, device_id=dev,
                                        device_id_type=MESH)


class _Gather:
    def __init__(self, shards):
        self.n = len(shards)
        self.shards = shards
        self.halves = [s.reshape(2, s.shape[0] // 2, s.shape[1]) for s in shards]
        self.plan = _chunk_plan([h.shape[1:] for h in self.halves], 2)
        self.out_shape = [_sds((4,) + h.shape, BF16) for h in self.halves]
        n = self.n
        self.scratch = [pltpu.SemaphoreType.DMA((6 * n,)), pltpu.SemaphoreType.DMA((6 * n,)),
                        pltpu.SemaphoreType.DMA((2,)), pltpu.SemaphoreType.DMA((2,)),
                        pltpu.VMEM((2, max(p[2] for p in self.plan), max(h.shape[2] for h in self.halves)), BF16)]

    def full(self, outs):
        return [o.reshape(4 * s.shape[0], s.shape[1]) for o, s in zip(outs, self.shards)]

    def _sends(self, ins, outs, ssem, rsem):
        x, y, c = lax.axis_index("x"), lax.axis_index("y"), lax.axis_index("c")
        me = 2 * x + y
        return [_remote(ins[i].at[c], outs[i].at[me, c], ssem.at[6 * i + k], rsem.at[6 * i + k], (px, py, c))
                for i in range(self.n) for k, (px, py) in enumerate(_chip_peers(x, y))]

    def start(self, ins, outs, scratch):
        ssem, rsem, lsem, osem, buf = scratch
        me = 2 * lax.axis_index("x") + lax.axis_index("y")
        for cp in self._sends(ins, outs, ssem, rsem):
            cp.start()
        pending = {}
        for i, r0, ch in self.plan:
            for h in range(2):
                if h in pending:
                    pending[h].wait()
                stage = buf.at[h, pl.ds(0, ch), pl.ds(0, self.halves[i].shape[2])]
                ld = pltpu.make_async_copy(ins[i].at[h, pl.ds(r0, ch)], stage, lsem.at[h])
                ld.start()
                ld.wait()
                st = pltpu.make_async_copy(stage, outs[i].at[me, h, pl.ds(r0, ch)], osem.at[h])
                st.start()
                pending[h] = st
        for st in pending.values():
            st.wait()

    def _forwards(self, outs, ssem, rsem):
        x, y, c = lax.axis_index("x"), lax.axis_index("y"), lax.axis_index("c")
        cps = []
        for i in range(self.n):
            for k, (px, py) in enumerate(_chip_peers(x, y)):
                landed = outs[i].at[2 * px + py, c]
                cps.append(_remote(landed, landed, ssem.at[6 * i + 3 + k], rsem.at[6 * i + 3 + k], (x, y, 1 - c)))
        return cps

    def relay(self, ins, outs, scratch):
        ssem, rsem = scratch[:2]
        x, y, c = lax.axis_index("x"), lax.axis_index("y"), lax.axis_index("c")
        forwards = self._forwards(outs, ssem, rsem)
        for i in range(self.n):
            for k, (px, py) in enumerate(_chip_peers(x, y)):
                landed = outs[i].at[2 * px + py, c]
                _remote(landed, landed, ssem.at[6 * i + k], rsem.at[6 * i + k], (px, py, c)).wait_recv()
                forwards[3 * i + k].start()

    def finish(self, ins, outs, scratch):
        ssem, rsem = scratch[:2]
        x, y, c = lax.axis_index("x"), lax.axis_index("y"), lax.axis_index("c")
        sib = (x, y, 1 - c)
        for i in range(self.n):
            for k, (px, py) in enumerate(_chip_peers(x, y)):
                passed = outs[i].at[2 * px + py, 1 - c]
                _remote(passed, passed, ssem.at[6 * i + 3 + k], rsem.at[6 * i + 3 + k], sib).wait_recv()
        for cp in self._sends(ins, outs, ssem, rsem) + self._forwards(outs, ssem, rsem):
            cp.wait_send()


def _gather_weights(shards):
    gt = _Gather(shards)
    n = gt.n

    def body(*refs):
        ins, outs, scratch = refs[:n], refs[n:2 * n], refs[2 * n:]
        gt.start(ins, outs, scratch)
        gt.relay(ins, outs, scratch)
        gt.finish(ins, outs, scratch)

    outs = _pallas(
        body, name="gather_weights", in_specs=[ANY] * n, out_specs=[ANY] * n, out_shape=gt.out_shape,
        scratch_shapes=gt.scratch, compiler_params=pltpu.CompilerParams(vmem_limit_bytes=32 << 20),
    )(*gt.halves)
    return gt.full(outs)


def _swap_halves(grads):
    n = len(grads)
    view = lambda g: g.reshape(4, 2, g.shape[0] // 8, g.shape[1])
    g4f = [view(g) for g, _ in grads]
    g4 = [view(gb) for _, gb in grads]

    def body(*refs):
        ins, got = refs[:n], refs[n:2 * n]
        ssem, rsem = refs[2 * n:]
        x, y, c = lax.axis_index("x"), lax.axis_index("y"), lax.axis_index("c")
        sib = (x, y, 1 - c)
        cps = []
        for i in range(n):
            rc = _remote(ins[i].at[:, 1 - c], got[i], ssem.at[i], rsem.at[i], sib)
            rc.start()
            cps.append(rc)
        for cp in cps:
            cp.wait()

    half = [_sds((4, g.shape[2], g.shape[3]), BF16) for g in g4]
    got = _pallas(
        body, name="swap_halves", in_specs=[ANY] * n, out_specs=[ANY] * n, out_shape=half,
        scratch_shapes=[pltpu.SemaphoreType.DMA((n,)), pltpu.SemaphoreType.DMA((n,))],
    )(*g4)
    return g4f, got


def _chip_sum(g4, got, name):
    _, _, R, W = g4.shape
    tr = _tile(R, max(16, min(512, (1 << 18) // W // 16 * 16)))
    c = lax.axis_index("c").astype(jnp.int32).reshape(1)

    def body(c_ref, a_ref, b_ref, s_ref, sb_ref):
        s = a_ref[...] + b_ref[...].astype(F32)
        s_ref[...] = s
        sb_ref[...] = s.astype(BF16)

    plain = pl.BlockSpec((None, tr, W), lambda j, t, c_ref: (j, t, 0))
    return _pallas(
        body, name=name,
        grid_spec=pltpu.PrefetchScalarGridSpec(
            num_scalar_prefetch=1, grid=(4, R // tr),
            in_specs=[pl.BlockSpec((None, None, tr, W), lambda j, t, c_ref: (j, c_ref[0], t, 0)), plain],
            out_specs=[plain, plain]),
        out_shape=[_sds((4, R, W), F32), _sds((4, R, W), BF16)],
        compiler_params=_params(("parallel", "parallel"), 32),
    )(c, g4, got)


class _Scatter:
    def __init__(self, sums_b):
        self.n = len(sums_b)
        self.ins = list(sums_b)
        self.out_shape = [_sds((3,) + s.shape[1:], BF16) for s in sums_b]
        self.scratch = [pltpu.SemaphoreType.DMA((3 * self.n,)), pltpu.SemaphoreType.DMA((3 * self.n,))]

    def _copies(self, ins, outs, scratch):
        ssem, rsem = scratch
        x, y, c = lax.axis_index("x"), lax.axis_index("y"), lax.axis_index("c")
        return [_remote(ins[i].at[2 * px + py], outs[i].at[k], ssem.at[3 * i + k], rsem.at[3 * i + k], (px, py, c))
                for i in range(self.n) for k, (px, py) in enumerate(_chip_peers(x, y))]

    def start(self, ins, outs, scratch):
        for cp in self._copies(ins, outs, scratch):
            cp.start()

    def finish(self, ins, outs, scratch):
        for cp in self._copies(ins, outs, scratch):
            cp.wait()


class _Reduce:
    def __init__(self, grads, names):
        self.names = names
        g4, got = _swap_halves(grads)
        self.sums, sums_b = [], []
        for nm, g, t in zip(names, g4, got):
            s_, sb_ = _chip_sum(g, t, f"chip_sum_{nm}")
            self.sums.append(s_)
            sums_b.append(sb_)
        self.scatter = _Scatter(sums_b)

    def collect(self, parts):
        return [_mesh_sum(s, p, f"mesh_sum_{nm}") for nm, s, p in zip(self.names, self.sums, parts)]


def _mesh_sum(sums, parts, name):
    _, R, W = sums.shape
    tr = _tile(R, max(16, min(512, (1 << 18) // W // 16 * 16)))
    me = (2 * lax.axis_index("x") + lax.axis_index("y")).astype(jnp.int32).reshape(1)

    def body(me_ref, m_ref, p_ref, o_ref):
        o_ref[...] = m_ref[...] + p_ref[0].astype(F32) + p_ref[1].astype(F32) + p_ref[2].astype(F32)

    return _pallas(
        body, name=name,
        grid_spec=pltpu.PrefetchScalarGridSpec(
            num_scalar_prefetch=1, grid=(R // tr,),
            in_specs=[pl.BlockSpec((None, tr, W), lambda i, me_ref: (me_ref[0], i, 0)),
                      pl.BlockSpec((3, tr, W), lambda i, me_ref: (0, i, 0))],
            out_specs=pl.BlockSpec((tr, W), lambda i, me_ref: (i, 0))),
        out_shape=_sds((R, W), F32), compiler_params=_params(("parallel",), 32),
    )(me, sums, parts)


def _share_halves(reduced):
    n = len(reduced)
    plan = _chunk_plan([r.shape for r in reduced], 4)
    max_rows = max(p[2] for p in plan)
    max_w = max(r.shape[1] for r in reduced)

    def body(*refs):
        ins, outs = refs[:n], refs[n:2 * n]
        ssem, rsem, lsem, osem, buf = refs[2 * n:]
        x, y, c = lax.axis_index("x"), lax.axis_index("y"), lax.axis_index("c")
        sib = (x, y, 1 - c)
        pending = {}
        for k, (i, r0, ch) in enumerate(plan):
            slot = k % 2
            if slot in pending:
                rc, lc = pending[slot]
                rc.wait_send()
                lc.wait()
            stage = buf.at[slot, pl.ds(0, ch), pl.ds(0, reduced[i].shape[1])]
            ld = pltpu.make_async_copy(ins[i].at[pl.ds(r0, ch)], stage, lsem.at[slot])
            ld.start()
            ld.wait()
            place = outs[i].at[c, pl.ds(r0, ch)]
            rc = _remote(stage, place, ssem.at[slot], rsem.at[i], sib)
            lc = pltpu.make_async_copy(stage, place, osem.at[slot])
            rc.start()
            lc.start()
            pending[slot] = (rc, lc)
        for rc, lc in pending.values():
            rc.wait_send()
            lc.wait()
        for i in range(n):
            theirs = outs[i].at[1 - c]
            _remote(theirs, theirs, ssem.at[0], rsem.at[i], sib).wait_recv()

    outs = _pallas(
        body, name="share_halves", in_specs=[ANY] * n, out_specs=[ANY] * n,
        out_shape=[_sds((2,) + r.shape, F32) for r in reduced],
        scratch_shapes=[pltpu.SemaphoreType.DMA((2,)), pltpu.SemaphoreType.DMA((n,)), pltpu.SemaphoreType.DMA((2,)),
                        pltpu.SemaphoreType.DMA((2,)), pltpu.VMEM((2, max_rows, max_w), F32)],
        compiler_params=pltpu.CompilerParams(vmem_limit_bytes=32 << 20),
    )(*reduced)
    return [o.reshape(2 * r.shape[0], r.shape[1]) for o, r in zip(outs, reduced)]


def _tile(rows, cap=256):
    t = min(rows, cap) // 16 * 16
    while rows % t:
        t -= 16
    return t


def _adam_shard(w, g, m, v, name):
    _, R, W = w.shape
    tr = _tile(R, max(8, min(512, (1 << 18) // W // 8 * 8)))

    def body(w_ref, g_ref, m_ref, v_ref, go_ref, d_ref, nm_ref, nv_ref):
        gv = g_ref[...]
        go_ref[...] = gv
        d_ref[...], nm_ref[...], nv_ref[...] = _adamw(w_ref[...], gv, m_ref[...], v_ref[...])

    lead = pl.BlockSpec((None, tr, W), lambda i: (0, i, 0))
    return _pallas(
        body, grid=(R // tr,), name=name, in_specs=[lead, _row(tr, W), lead, lead], out_specs=[lead] * 4,
        out_shape=[_sds((1, R, W), F32)] * 4, compiler_params=_params(("parallel",), 48),
    )(w, g, m, v)


def _adamw(w, g, m, v):
    m = B1 * m + (1.0 - B1) * g
    v = B2 * v + (1.0 - B2) * (g * g)
    m_hat = m / (1.0 - B1 ** STEP)
    v_hat = v / (1.0 - B2 ** STEP)
    return -LR * (m_hat / (jnp.sqrt(v_hat) + AEPS) + WD * w), m, v


def _exchange_small(sent, comm):
    ns = len(sent)
    n = comm.n

    def body(*refs):
        p_refs, o_refs = refs[:ns], refs[ns + n:2 * ns + n]
        all_s = refs[2 * ns + 2 * n:3 * ns + 2 * n]
        ssem, rsem = refs[3 * ns + 2 * n:3 * ns + 2 * n + 2]
        cargs = (refs[ns:ns + n], refs[2 * ns + n:2 * ns + 2 * n], refs[3 * ns + 2 * n + 2:])
        comm.start(*cargs)
        x, y, c = lax.axis_index("x"), lax.axis_index("y"), lax.axis_index("c")
        me = 4 * x + 2 * y + c
        for i in range(ns):
            all_s[i][me] = p_refs[i][...]
        cps = []
        for rel in range(1, 8):
            peer = (1 - x if rel & 4 else x, 1 - y if rel & 2 else y, 1 - c if rel & 1 else c)
            for i in range(ns):
                k = (rel - 1) * ns + i
                mine = all_s[i].at[me]
                rc = _remote(mine, mine, ssem.at[k], rsem.at[k], peer)
                rc.start()
                cps.append((rc, i, k, 4 * peer[0] + 2 * peer[1] + peer[2]))
        for rc, i, k, peer_slot in cps:
            rc.wait_send()
            theirs = all_s[i].at[peer_slot]
            _remote(theirs, theirs, ssem.at[k], rsem.at[k], (x, y, c)).wait_recv()
        for i in range(ns):
            o_refs[i][...] = all_s[i][...]
        comm.finish(*cargs)

    shapes = [_sds((8,) + t.shape, F32) for t in sent]
    outs = _pallas(
        body, name="exchange_small", grid=(1,), in_specs=[_whole(t) for t in sent] + [ANY] * n,
        out_specs=[_whole(t) for t in shapes] + [ANY] * n, out_shape=shapes + comm.out_shape,
        scratch_shapes=[pltpu.VMEM(t.shape, F32) for t in shapes]
        + [pltpu.SemaphoreType.DMA((7 * ns,)), pltpu.SemaphoreType.DMA((7 * ns,))] + comm.scratch,
        compiler_params=_params(("arbitrary",), 32),
    )(*sent, *comm.ins)
    return outs[:ns], outs[ns:]


def _adam_small(ws, ms, vs, parts, loss_part, comm):
    n = len(ws)
    gathered, received = _exchange_small(list(parts) + [loss_part], comm)

    def body(*refs):
        w_refs, m_refs, v_refs, a_refs = refs[:n], refs[n:2 * n], refs[2 * n:3 * n], refs[3 * n:4 * n + 1]
        outs = refs[4 * n + 1:]
        g_refs, d_refs, nm_refs, nv_refs, loss_ref = outs[:n], outs[n:2 * n], outs[2 * n:3 * n], outs[3 * n:4 * n], outs[4 * n]

        def total(i):
            t = a_refs[i][0]
            for k in range(1, 8):
                t = t + a_refs[i][k]
            return t

        for i in range(n):
            g = total(i)
            g_refs[i][...] = g
            d_refs[i][...], nm_refs[i][...], nv_refs[i][...] = _adamw(w_refs[i][...], g, m_refs[i][...], v_refs[i][...])
        loss_ref[...] = total(n)

    shapes = [_sds(w.shape, F32) for w in ws] * 4 + [_sds(loss_part.shape, F32)]
    ins = [*ws, *ms, *vs, *gathered]
    outs = _pallas(
        body, name="adam_small", grid=(1,), in_specs=[_whole(t) for t in ins], out_specs=[_whole(t) for t in shapes],
        out_shape=shapes, compiler_params=_params(("arbitrary",), 32),
    )(*ins)
    return outs[:n], outs[n:2 * n], outs[2 * n:3 * n], outs[3 * n:4 * n], outs[4 * n], received


def _local_step(xs, tgt, mems, weights, small, gather_in=None, gather_mid=None, gather_gu=None, gather_down=None,
                reduce=False):
    wt_in, wt_ba, wt_bs, wo, wq, wkv, wt_o, wt_gu, wd = weights
    g_mix, b_gate, w_sgu, b_sgu, g_sgu, g_cross, g_mem, g_ffn, g_final = small
    wt = jnp.tril(w_sgu).astype(BF16)
    bst = b_sgu.T

    if gather_in is None:
        a, _ = _norm_in(xs, g_mix)
    else:
        a, (wt_in,) = _norm_in(xs, g_mix, gather_in)
    (qkv0, qkv1, qkv2, uv, gl), got = _fwd_in(a, wt_in, gather_mid)
    if gather_mid is not None:
        wt_ba, wt_bs, wo, wq, wkv, wt_o = got
    qkvs = (qkv0, qkv1, qkv2)
    os_, ls_ = zip(*[_attn_fwd(qkvs[g], g) for g in range(3)])
    (ya, ys, ba, bs, mg, h1), got = _fwd_mid(xs, os_, ls_, uv, gl, wt, bst, g_sgu, b_gate, wt_ba, wt_bs, wo, gather_gu)
    if gather_gu is not None:
        (wt_gu,) = got
    mb, kv = _mem_fwd(mems, g_mem, wkv)
    (cb, qc, oc, h2), got = _fwd_cross(h1, g_cross, wq, kv, wt_o, gather_down)
    if gather_down is not None:
        (wd,) = got
    f, act, dgu, dh3b, dh2, dh2b, dg_ffn, dg_final, loss = _ffn_fwd_bwd(h2, tgt, g_ffn, g_final, wt_gu, wd)

    g_ffn_w = [_tn_matmul(dgu, f, "dw_gate_up", 1408), _tn_matmul(act, dh3b, "dw_down", 1408)]
    r_ffn = _Reduce(g_ffn_w, ["w_gate_up", "w_down"]) if reduce else None
    (dqc, dh1, dh1b, dkv, dg_cross), parts_ffn = _bwd_cross(dh2, h1, qc, g_cross, wq, kv, wt_o,
                                                           r_ffn.scatter if reduce else None)
    dw_kv, dw_kvb, dg_mem = _mem_bwd(dkv, mems, mb, g_mem, wkv)
    (dba, dbs, dgl, duv, do0, do1, do2, c0, c1, c2, db_gate, dg_sgu, dws, dbs_acc) = _bwd_mid(
        dh1, gl, ba, bs, uv, ls_, ya, wt, bst, g_sgu, b_gate, wt_ba, wt_bs, wo)
    dqkvs = [_attn_bwd(qkvs[g], do, ls_[g], corr, g) for g, (do, corr) in enumerate(((do0, c0), (do1, c1), (do2, c2)))]
    grad_x, dproj, dg_mix = _bwd_in(dqkvs, duv, dgl, dh1, xs, g_mix, wt_in)
    g_mid_w = [_tn_matmul(dba, ya, "dw_branch_attn", 1024),
               _tn_matmul(dbs, ys, "dw_branch_sgu", 1024),
               _tn_matmul(mg, dh1b, "dw_out", 1024),
               _tn_matmul(cb, dqc, "dw_q_cross", 1024),
               (dw_kv, dw_kvb),
               _tn_matmul(dh2b, oc, "dw_o_cross", 1024)]
    small_terms = (dg_mix, db_gate, dws, dbs_acc, dg_sgu, dg_cross, dg_mem, dg_ffn, dg_final)
    if not reduce:
        full = [_tn_matmul(dproj, a, "dw_in", 1792)] + g_mid_w + g_ffn_w
        return loss, grad_x, [g for g, _ in full], small_terms
    r_mid = _Reduce(g_mid_w, ["w_branch_attn", "w_branch_sgu", "w_out", "w_q_cross", "w_kv_cross", "w_o_cross"])
    g_in, parts_mid = _tn_matmul(dproj, a, "dw_in", 1792, comm=r_mid.scatter)
    r_in = _Reduce([g_in], ["w_in"])

    def finish(parts_in):
        return r_in.collect(parts_in) + r_mid.collect(parts_mid) + r_ffn.collect(parts_ffn)
    return loss, grad_x, (r_in.scatter, finish), small_terms


def kernel(x, mem, g_mix, w_in, b_gate, w_sgu_spatial, b_sgu_spatial, g_sgu, w_branch_attn, w_branch_sgu, w_out, g_cross, g_mem, w_q_cross, w_kv_cross, w_o_cross, g_ffn, w_gate_up, w_down, g_final, loss_target, m_g_mix, m_w_in, m_b_gate, m_w_sgu_spatial, m_b_sgu_spatial, m_g_sgu, m_w_branch_attn, m_w_branch_sgu, m_w_out, m_g_cross, m_g_mem, m_w_q_cross, m_w_kv_cross, m_w_o_cross, m_g_ffn, m_w_gate_up, m_w_down, m_g_final, v_g_mix, v_w_in, v_b_gate, v_w_sgu_spatial, v_b_sgu_spatial, v_g_sgu, v_w_branch_attn, v_w_branch_sgu, v_w_out, v_g_cross, v_g_mem, v_w_q_cross, v_w_kv_cross, v_w_o_cross, v_g_ffn, v_w_gate_up, v_w_down, v_g_final):
    S = x.shape[1]
    xs, tgt, mems = x.reshape(S, D), loss_target.reshape(S, D), mem.reshape(mem.shape[1], D)
    g_final2 = g_final.reshape(1, D)

    big = [("w_in", w_in, m_w_in, v_w_in, True),
           ("w_branch_attn", w_branch_attn, m_w_branch_attn, v_w_branch_attn, True),
           ("w_branch_sgu", w_branch_sgu, m_w_branch_sgu, v_w_branch_sgu, True),
           ("w_out", w_out, m_w_out, v_w_out, False),
           ("w_q_cross", w_q_cross, m_w_q_cross, v_w_q_cross, False),
           ("w_kv_cross", w_kv_cross, m_w_kv_cross, v_w_kv_cross, False),
           ("w_o_cross", w_o_cross, m_w_o_cross, v_w_o_cross, True),
           ("w_gate_up", w_gate_up, m_w_gate_up, v_w_gate_up, True),
           ("w_down", w_down, m_w_down, v_w_down, False)]
    shards = [(w[0].T if tr else w[0]).astype(BF16) for _, w, _, _, tr in big]
    (loss, grad_x, (scatter_in, finish_reduce),
     (dg_mix, db_gate, dws, dbs_acc, dg_sgu, dg_cross, dg_mem, dg_ffn, dg_final)) = _local_step(
        xs, tgt, mems, (None,) * 9,
        (g_mix, b_gate, w_sgu_spatial[0], b_sgu_spatial[0], g_sgu, g_cross, g_mem, g_ffn, g_final2),
        _Gather(shards[:1]), _Gather(shards[1:7]), _Gather(shards[7:8]), _Gather(shards[8:9]), reduce=True)

    small = [("g_mix", g_mix, m_g_mix, v_g_mix, dg_mix), ("b_gate", b_gate, m_b_gate, v_b_gate, db_gate),
             ("w_sgu_spatial", w_sgu_spatial, m_w_sgu_spatial, v_w_sgu_spatial, jnp.tril(dws)),
             ("b_sgu_spatial", b_sgu_spatial, m_b_sgu_spatial, v_b_sgu_spatial, jnp.sum(dbs_acc, axis=-1)),
             ("g_sgu", g_sgu, m_g_sgu, v_g_sgu, dg_sgu), ("g_cross", g_cross, m_g_cross, v_g_cross, dg_cross),
             ("g_mem", g_mem, m_g_mem, v_g_mem, dg_mem), ("g_ffn", g_ffn, m_g_ffn, v_g_ffn, dg_ffn),
             ("g_final", g_final, m_g_final, v_g_final, dg_final)]
    as_term = lambda s, t: t.reshape(s[4].shape)
    gs, ds, nms, nvs, loss_all, parts_in = _adam_small(
        *[[as_term(s, s[k]) for s in small] for k in (1, 2, 3, 4)], loss, scatter_in)
    small_out = {s[0]: tuple(t[i].reshape(s[1].shape) for t in (gs, ds, nms, nvs)) for i, s in enumerate(small)}
    total_loss = loss_all[0, 0]

    full = _share_halves(finish_reduce(parts_in))
    big_out = {}
    for (name, w, m, v, tr), gsh in zip(big, full):
        if tr and w.shape[2] % 128:
            outs = _adam_shard(*(jnp.swapaxes(t, 1, 2) for t in (w,)), gsh, *(jnp.swapaxes(t, 1, 2) for t in (m, v)),
                               f"adam_{name}")
            big_out[name] = tuple(jnp.swapaxes(t, 1, 2) for t in outs)
        else:
            big_out[name] = tuple(_adam_shard(w, gsh.T if tr else gsh, m, v, f"adam_{name}"))

    order = ["g_mix", "w_in", "b_gate", "w_sgu_spatial", "b_sgu_spatial", "g_sgu", "w_branch_attn", "w_branch_sgu",
             "w_out", "g_cross", "g_mem", "w_q_cross", "w_kv_cross", "w_o_cross", "g_ffn", "w_gate_up", "w_down",
             "g_final"]
    res = {**big_out, **small_out}
    outs = [total_loss, grad_x.reshape(x.shape)]
    for k in range(4):
        outs += [res[nm][k] for nm in order]
    return tuple(outs)
```
